```python
import jax, jax.numpy as jnp
from jax import lax
import numpy as np

D_MODEL = 1024
BATCH = 32
SEQ = 256
DEPTH = 2
DEC_BATCH = 2
DEC_SEQ = 4096
PAST_LEN = 256

F32 = jnp.float32
GRID_W = 64
EPS = 1e-6
FNET_W = 256
FNET_GROUPS = 4
LRU_W = 256
LRU_BLOCKS = 4
LRU_CONV = 4
LRU_C = 8.0
SC_W = 256
SC_CONV = 3
MLA_HEADS = 8
Q_LORA = 384
KV_LORA = 256
NOPE = 64
ROPE = 32
V_DIM = 64
QK_DIM = NOPE + ROPE
MLA_OUT = MLA_HEADS * V_DIM
N_FREQ = ROPE // 4
ROPE_BASE = 10000.0
ATTN_SCALE = QK_DIM ** -0.5
Q_BLOCK = 128
N_BRANCH = 4
IN_WIDTHS = (FNET_W, LRU_W, LRU_W, SC_W, SC_W, SC_W, Q_LORA, KV_LORA, ROPE)
D_IN = FNET_W + 2 * LRU_W + 3 * SC_W + Q_LORA + KV_LORA + ROPE
N_GROUPS = 4
EXP_PER_GROUP = 8
N_EXPERTS = N_GROUPS * EXP_PER_GROUP
TOP_K = 2
D_EXPERT = 256

kernel_name = 'hybrid_diffusion_prefix_trunk_step'


def split_points():
    return [int(v) for v in np.cumsum(IN_WIDTHS)[:-1]]


def rmsnorm(x, g):
    xf = x.astype(F32)
    y = xf * lax.rsqrt(jnp.mean(xf * xf, axis=-1, keepdims=True) + EPS)
    return (y * g.astype(F32)).astype(x.dtype)


def adaln(cond, w, b):
    m = jax.nn.silu(cond) @ w + b
    return jnp.split(m, 6, axis=-1)


def dwconv(u, w, pad_left):
    K = w.shape[0]
    L = u.shape[1]
    up = jnp.pad(u, ((0, 0), (pad_left, K - 1 - pad_left), (0, 0)))
    y = up[:, 0:L] * w[0]
    for k in range(1, K):
        y = y + up[:, k:k + L] * w[k]
    return y


def fourier_mix(u):
    B, L, _ = u.shape
    ug = u.astype(F32).reshape(B, L, FNET_GROUPS, FNET_W // FNET_GROUPS)
    y = jnp.fft.fftn(ug, axes=(1, 3), norm='ortho').real
    return y.reshape(B, L, FNET_W).astype(u.dtype)


def blockdiag(x, w, b):
    B, L, _ = x.shape
    xb = x.reshape(B, L, LRU_BLOCKS, LRU_W // LRU_BLOCKS)
    return jnp.einsum('blni,nij->blnj', xb, w).reshape(B, L, LRU_W) + b


def linear_scan(a, b, h0, reverse):
    if h0 is not None:
        idx = -1 if reverse else 0
        b = b.at[:, idx].add(a[:, idx] * h0.astype(F32))

    def comb(lft, rgt):
        return (lft[0] * rgt[0], rgt[0] * lft[1] + rgt[1])

    _, h = lax.associative_scan(comb, (a, b), reverse=reverse, axis=1)
    return h


def rglru_dir(xc, p, d, h0, reverse):
    r = jax.nn.sigmoid(blockdiag(xc, p['lru_wa'][d], p['lru_ba'][d]).astype(F32))
    i = jax.nn.sigmoid(blockdiag(xc, p['lru_wx'][d], p['lru_bx'][d]).astype(F32))
    log_a = -LRU_C * r * jax.nn.softplus(-p['lru_lam'][d].astype(F32))
    a = jnp.exp(log_a)
    b = jnp.sqrt(-jnp.expm1(2.0 * log_a)) * (i * xc.astype(F32))
    return linear_scan(a, b, h0, reverse)


def axial_rope(L):
    t = jnp.arange(L, dtype=jnp.int32)
    rows = (t // GRID_W).astype(F32)
    cols = (t % GRID_W).astype(F32)
    inv = ROPE_BASE ** (-jnp.arange(N_FREQ, dtype=F32) / N_FREQ)
    ang = jnp.concatenate([rows[:, None] * inv, cols[:, None] * inv], axis=-1)
    return jnp.cos(ang)[:, None, :], jnp.sin(ang)[:, None, :]


def rope_tail(x, cos, sin):
    xn, xr = x[..., :NOPE], x[..., NOPE:]
    x1 = xr[..., :ROPE // 2].astype(F32)
    x2 = xr[..., ROPE // 2:].astype(F32)
    rot = jnp.concatenate([x1 * cos - x2 * sin, x1 * sin + x2 * cos], axis=-1).astype(x.dtype)
    return jnp.concatenate([xn, rot], axis=-1)


def mla_kv(ckv, kr, w_kvb, g_kn):
    B, L, _ = ckv.shape
    kv = jnp.einsum('blr,rhe->blhe', ckv, w_kvb)
    k_nope, v = kv[..., :NOPE], kv[..., NOPE:]
    k = jnp.concatenate([k_nope, jnp.broadcast_to(kr[:, :, None, :], (B, L, MLA_HEADS, ROPE))], axis=-1)
    return rmsnorm(k, g_kn), v


def attend(q, k, v):
    s = jnp.einsum('bqhe,bkhe->bhqk', q, k).astype(F32) * ATTN_SCALE
    pr = jax.nn.softmax(s, axis=-1).astype(v.dtype)
    return jnp.einsum('bhqk,bkhv->bqhv', pr, v)


def blocked_attend(q, k, v):
    B, L, H, E = q.shape
    qb = q.reshape(B, L // Q_BLOCK, Q_BLOCK, H, E).transpose(1, 0, 2, 3, 4)
    ob = lax.map(lambda qq: attend(qq, k, v), qb)
    return ob.transpose(1, 0, 2, 3, 4).reshape(B, L, H, v.shape[-1])


def token_mixing(h, p, lru_h0, ctx_ckv, ctx_kr):
    B, L, _ = h.shape
    is_ctx = ctx_ckv is None
    z = h @ p['w_in']
    u_f, x_l, g_l, b_s, c_s, x_s, q_c, kv_c, k_r = jnp.split(z, split_points(), axis=-1)
    y_f = fourier_mix(u_f)
    xc = dwconv(x_l, p['lru_conv'], LRU_CONV // 2)
    h_fw = rglru_dir(xc, p, 0, None if is_ctx else lru_h0[:, 0], False)
    h_bw = rglru_dir(xc, p, 1, None if is_ctx else lru_h0[:, 1], True)
    y_l = (h_fw + h_bw).astype(h.dtype) * jax.nn.gelu(g_l)
    y_s = b_s * dwconv(c_s * x_s, p['sc_conv'], SC_CONV // 2)
    q = jnp.einsum('blr,rhe->blhe', rmsnorm(q_c, p['g_qa']), p['w_qb'])
    q = rmsnorm(q, p['g_qn'])
    ckv = rmsnorm(kv_c, p['g_kva'])
    k, v = mla_kv(ckv, k_r, p['w_kvb'], p['g_kn'])
    if not is_ctx:
        cos, sin = axial_rope(L)
        q = rope_tail(q, cos, sin)
        k = rope_tail(k, cos, sin)
        k_ctx, v_ctx = mla_kv(ctx_ckv, ctx_kr, p['w_kvb'], p['g_kn'])
        k = jnp.concatenate([k, k_ctx], axis=1)
        v = jnp.concatenate([v, v_ctx], axis=1)
    o = blocked_attend(q, k, v).reshape(B, L, MLA_OUT)
    br = jnp.stack([y_f @ p['w_pf'], y_l @ p['w_pl'], y_s @ p['w_ps'], o @ p['w_pm']], axis=2)
    gates = jax.nn.sigmoid(h @ p['w_gate']).reshape(B, L, N_BRANCH, D_MODEL)
    out = jnp.sum(gates * br, axis=2) @ p['w_out']
    if is_ctx:
        fin = jnp.stack([h_fw[:, -1], h_bw[:, 0]], axis=1).astype(h.dtype)
        return out, (ckv, k_r, fin)
    return out, None


def hier_moe(h, p):
    B, L, D = h.shape
    t = h.reshape(B * L, D)
    g_logit = (t @ p['w_gr']).astype(F32) + p['b_gr'].astype(F32)
    g_prob = jax.nn.softmax(g_logit, axis=-1)
    g_sel = jnp.argmax(g_logit, axis=-1)
    g_w = jnp.take_along_axis(g_prob, g_sel[:, None], axis=-1)
    e_logit = ((t @ p['w_er']).astype(F32) + p['b_er'].astype(F32)).reshape(-1, N_GROUPS, EXP_PER_GROUP)
    e_in = jnp.einsum('tge,tg->te', e_logit, jax.nn.one_hot(g_sel, N_GROUPS, dtype=F32))
    top_v, top_i = lax.top_k(e_in, TOP_K)
    top_w = jax.nn.softmax(top_v, axis=-1) * g_w
    eid = g_sel[:, None] * EXP_PER_GROUP + top_i
    combine = jnp.einsum('tk,tke->te', top_w, jax.nn.one_hot(eid, N_EXPERTS, dtype=F32)).astype(h.dtype)
    a = jnp.einsum('td,edf->tef', t, p['w1'])
    u = jnp.einsum('td,edf->tef', t, p['w3'])
    y = jnp.einsum('tef,efd->td', jax.nn.silu(a) * u * combine[:, :, None], p['w2'])
    return y.reshape(B, L, D)


def trunk_layer(x, mod, p, lru_h0, ctx_ckv, ctx_kr):
    sh1, sc1, g1, sh2, sc2, g2 = mod
    h = rmsnorm(x, p['norm1_g']) * (1 + sc1) + sh1
    mix, ctx_state = token_mixing(h, p, lru_h0, ctx_ckv, ctx_kr)
    x = x + g1 * mix
    h = rmsnorm(x, p['norm2_g']) * (1 + sc2) + sh2
    x = x + g2 * hier_moe(h, p)
    return x, ctx_state


def setup_inputs(seed: int = 0) -> dict:
    key = jax.random.key(seed)
    ks = iter(jax.random.split(key, 48))
    D = D_MODEL
    bw = LRU_W // LRU_BLOCKS

    def nrm(shape, scale):
        return jax.random.normal(next(ks), shape, F32) * scale

    def gain(shape):
        return 1.0 + nrm(shape, 0.02)

    inputs = {
        'x_prompt': nrm((BATCH, SEQ, D), 1.0),
        'x_sample': nrm((DEC_BATCH, DEC_SEQ, D), 1.0),
        'cache_ckv': nrm((DEC_BATCH, DEPTH, PAST_LEN, KV_LORA), 1.0),
        'cache_krope': nrm((DEC_BATCH, DEPTH, PAST_LEN, ROPE), 1.0),
        'state_rglru': nrm((DEC_BATCH, DEPTH, 2, LRU_W), 0.5),
        'c': nrm((DEC_BATCH, D), 1.0),
        'c_ctx': nrm((D,), 1.0),
        'norm1_g': gain((DEPTH, D)),
        'norm2_g': gain((DEPTH, D)),
        'w_ada': nrm((DEPTH, D, 6 * D), 0.5 * D ** -0.5),
        'b_ada': nrm((DEPTH, 6 * D), 0.02),
        'w_in': nrm((DEPTH, D, D_IN), D ** -0.5),
        'lru_conv': nrm((DEPTH, LRU_CONV, LRU_W), LRU_CONV ** -0.5),
        'lru_wa': nrm((DEPTH, 2, LRU_BLOCKS, bw, bw), bw ** -0.5),
        'lru_ba': nrm((DEPTH, 2, LRU_W), 0.1),
        'lru_wx': nrm((DEPTH, 2, LRU_BLOCKS, bw, bw), bw ** -0.5),
        'lru_bx': nrm((DEPTH, 2, LRU_W), 0.1),
    }
    a8 = jax.random.uniform(next(ks), (DEPTH, 2, LRU_W), F32, 0.9, 0.999)
    a = a8 ** (1.0 / LRU_C)
    inputs['lru_lam'] = jnp.log(a) - jnp.log1p(-a)
    inputs['sc_conv'] = nrm((DEPTH, SC_CONV, SC_W), SC_CONV ** -0.5)
    inputs['g_qa'] = gain((DEPTH, Q_LORA))
    inputs['w_qb'] = nrm((DEPTH, Q_LORA, MLA_HEADS, QK_DIM), Q_LORA ** -0.5)
    inputs['g_kva'] = gain((DEPTH, KV_LORA))
    inputs['w_kvb'] = nrm((DEPTH, KV_LORA, MLA_HEADS, NOPE + V_DIM), KV_LORA ** -0.5)
    inputs['g_qn'] = gain((DEPTH, QK_DIM))
    inputs['g_kn'] = gain((DEPTH, QK_DIM))
    inputs['w_pf'] = nrm((DEPTH, FNET_W, D), FNET_W ** -0.5)
    inputs['w_pl'] = nrm((DEPTH, LRU_W, D), LRU_W ** -0.5)
    inputs['w_ps'] = nrm((DEPTH, SC_W, D), SC_W ** -0.5)
    inputs['w_pm'] = nrm((DEPTH, MLA_OUT, D), MLA_OUT ** -0.5)
    inputs['w_gate'] = nrm((DEPTH, D, N_BRANCH * D), D ** -0.5)
    inputs['w_out'] = nrm((DEPTH, D, D), D ** -0.5)
    inputs['w_gr'] = nrm((DEPTH, D, N_GROUPS), D ** -0.5)
    inputs['b_gr'] = nrm((DEPTH, N_GROUPS), 0.01)
    inputs['w_er'] = nrm((DEPTH, D, N_EXPERTS), D ** -0.5)
    inputs['b_er'] = nrm((DEPTH, N_EXPERTS), 0.01)
    inputs['w1'] = nrm((DEPTH, N_EXPERTS, D, D_EXPERT), D ** -0.5)
    inputs['w3'] = nrm((DEPTH, N_EXPERTS, D, D_EXPERT), D ** -0.5)
    inputs['w2'] = nrm((DEPTH, N_EXPERTS, D_EXPERT, D), D_EXPERT ** -0.5)
    return inputs


def reference(x_prompt, x_sample, cache_ckv, cache_krope, state_rglru, c, c_ctx,
              norm1_g, norm2_g, w_ada, b_ada, w_in, lru_conv, lru_wa, lru_ba, lru_wx, lru_bx,
              lru_lam, sc_conv, g_qa, w_qb, g_kva, w_kvb, g_qn, g_kn, w_pf, w_pl, w_ps, w_pm,
              w_gate, w_out, w_gr, b_gr, w_er, b_er, w1, w3, w2):
    xp, xs = x_prompt, x_sample
    ckv_list, kr_list, lru_list = [], [], []
    for l in range(DEPTH):
        p = {
            'norm1_g': norm1_g[l], 'norm2_g': norm2_g[l], 'w_in': w_in[l],
            'lru_conv': lru_conv[l], 'lru_wa': lru_wa[l], 'lru_ba': lru_ba[l],
            'lru_wx': lru_wx[l], 'lru_bx': lru_bx[l], 'lru_lam': lru_lam[l],
            'sc_conv': sc_conv[l], 'g_qa': g_qa[l], 'w_qb': w_qb[l], 'g_kva': g_kva[l],
            'w_kvb': w_kvb[l], 'g_qn': g_qn[l], 'g_kn': g_kn[l],
            'w_pf': w_pf[l], 'w_pl': w_pl[l], 'w_ps': w_ps[l], 'w_pm': w_pm[l],
            'w_gate': w_gate[l], 'w_out': w_out[l],
            'w_gr': w_gr[l], 'b_gr': b_gr[l], 'w_er': w_er[l], 'b_er': b_er[l],
            'w1': w1[l], 'w3': w3[l], 'w2': w2[l],
        }
        mod_ctx = adaln(c_ctx, w_ada[l], b_ada[l])
        mod_lat = [m[:, None, :] for m in adaln(c, w_ada[l], b_ada[l])]
        xp, (ckv, kr, fin) = trunk_layer(xp, mod_ctx, p, None, None, None)
        ckv_list.append(ckv)
        kr_list.append(kr)
        lru_list.append(fin)
        xs, _ = trunk_layer(xs, mod_lat, p, state_rglru[:, l], cache_ckv[:, l], cache_krope[:, l])
    new_ckv = jnp.stack(ckv_list, axis=1)
    new_krope = jnp.stack(kr_list, axis=1)
    new_rglru = jnp.stack(lru_list, axis=1)
    return (xp, xs, new_ckv, new_krope, new_rglru)
```

```python
import functools
import math

import numpy as np
import jax
import jax.numpy as jnp
from jax import lax
from jax.experimental import pallas as pl
from jax.experimental.pallas import tpu as pltpu

F32 = jnp.float32
BF16 = jnp.bfloat16

D_MODEL = 1024
DEPTH = 2
GRID_W = 64
EPS = 1e-6
FNET_W = 256
FNET_GROUPS = 4
LRU_W = 256
LRU_BLOCKS = 4
LRU_C = 8.0
SC_W = 256
MLA_HEADS = 8
Q_LORA = 384
KV_LORA = 256
NOPE = 64
ROPE = 32
V_DIM = 64
QK_DIM = NOPE + ROPE
N_FREQ = ROPE // 4
ROPE_BASE = 10000.0
ATTN_SCALE = QK_DIM ** -0.5
N_GROUPS = 4
EXP_PER_GROUP = 8
N_EXPERTS = N_GROUPS * EXP_PER_GROUP
D_EXPERT = 256

LANE = 128
SUB = 8
HEAD_PAD = LANE
HP = MLA_HEADS * HEAD_PAD
ROPE_OFF = NOPE
Z_QKV = 6 * 256
Z_KR = Z_QKV + Q_LORA + KV_LORA
Z_W = Z_KR + LANE
QKV_W = Q_LORA + KV_LORA + LANE
VMEM_LIMIT = 52 * 1024 * 1024


def _cparams(sem):
    return pltpu.CompilerParams(dimension_semantics=sem, vmem_limit_bytes=VMEM_LIMIT)


def _bdot(a, b):
    return jnp.dot(a, b, preferred_element_type=F32)


def _rms(x, g):
    return x * lax.rsqrt(jnp.mean(x * x, axis=-1, keepdims=True) + EPS) * g


def _ada_kernel(c_ref, w_ref, b_ref, o_ref):
    c = c_ref[...]
    s = (c * jax.nn.sigmoid(c)).astype(BF16)
    o_ref[...] = _bdot(s, w_ref[...].astype(BF16)) + b_ref[...]


def _ada(cond8, w_ada, b_ada):
    nblk = 6 * D_MODEL // 1024
    return pl.pallas_call(
        _ada_kernel,
        grid=(DEPTH, nblk),
        in_specs=[
            pl.BlockSpec((SUB, D_MODEL), lambda l, n: (0, 0)),
            pl.BlockSpec((None, D_MODEL, 1024), lambda l, n: (l, 0, n)),
            pl.BlockSpec((None, 1, 1024), lambda l, n: (l, 0, n)),
        ],
        out_specs=pl.BlockSpec((None, SUB, 1024), lambda l, n: (l, 0, n)),
        out_shape=jax.ShapeDtypeStruct((DEPTH, SUB, 6 * D_MODEL), F32),
        compiler_params=_cparams(("arbitrary", "arbitrary")),
        name="ada",
    )(cond8, w_ada, b_ada.reshape(DEPTH, 1, 6 * D_MODEL))


def _front_kernel(x_ref, mod_ref, g_ref, w_ref, z_ref):
    x = x_ref[...]
    sh = mod_ref[:, 0:D_MODEL]
    sc = mod_ref[:, D_MODEL:2 * D_MODEL]
    h = _rms(x, g_ref[...]) * (1.0 + sc) + sh
    z_ref[...] = _bdot(h.astype(BF16), w_ref[...])


def _front(x, mod, g1, w_in, tm):
    t = x.shape[0]
    per_mod = t // (mod.shape[0] * tm)
    return pl.pallas_call(
        _front_kernel,
        grid=(t // tm,),
        in_specs=[
            pl.BlockSpec((tm, D_MODEL), lambda i: (i, 0)),
            pl.BlockSpec((None, 1, 6 * D_MODEL), lambda i: (i // per_mod, 0, 0)),
            pl.BlockSpec((1, D_MODEL), lambda i: (0, 0)),
            pl.BlockSpec((D_MODEL, Z_W), lambda i: (0, 0)),
        ],
        out_specs=pl.BlockSpec((tm, Z_W), lambda i: (i, 0)),
        out_shape=jax.ShapeDtypeStruct((t, Z_W), F32),
        compiler_params=_cparams(("arbitrary",)),
        name="front",
    )(x, mod, g1, w_in)


def _rope(x, cos, sina, sinb):
    return x * cos + pltpu.roll(x, LANE - ROPE // 2, 1) * sina + pltpu.roll(x, ROPE // 2, 1) * sinb


def _head_norm(x, g):
    ss = jnp.sum(x * x, axis=-1, keepdims=True) * (1.0 / QK_DIM)
    return x * lax.rsqrt(ss + EPS) * g


def _build_kv(ckv, kr_tile, cos, sina, sinb, wk_ref, wv_ref, gk, k_ref, v_ref):
    cb = ckv.astype(BF16)
    kn = _bdot(cb, wk_ref[...])
    for h in range(MLA_HEADS):
        sl = slice(h * HEAD_PAD, (h + 1) * HEAD_PAD)
        kh = _head_norm(kn[:, sl] + kr_tile, gk)
        k_ref[:, sl] = _rope(kh, cos, sina, sinb).astype(BF16)
    v_ref[...] = _bdot(cb, wv_ref[...]).astype(BF16)


def _qkv_kernel(z_ref, cos_ref, sina_ref, sinb_ref, gqa_ref, wq_ref, gkva_ref, wk_ref, wv_ref,
                gq_ref, gk_ref, q_ref, k_ref, v_ref, ckv_ref):
    cos, sina, sinb = cos_ref[...], sina_ref[...], sinb_ref[...]
    q_c = z_ref[:, 0:Q_LORA]
    kv_c = z_ref[:, Q_LORA:Q_LORA + KV_LORA]
    kr_tile = z_ref[:, Q_LORA + KV_LORA:QKV_W]
    qf = _bdot(_rms(q_c, gqa_ref[...]).astype(BF16), wq_ref[...])
    gq = gq_ref[...]
    for h in range(MLA_HEADS):
        sl = slice(h * HEAD_PAD, (h + 1) * HEAD_PAD)
        qh = _rope(_head_norm(qf[:, sl], gq), cos, sina, sinb)
        q_ref[:, sl] = (qh * ATTN_SCALE).astype(BF16)
    ckv = _rms(kv_c, gkva_ref[...])
    ckv_ref[...] = ckv
    _build_kv(ckv, kr_tile, cos, sina, sinb, wk_ref, wv_ref, gk_ref[...], k_ref, v_ref)


def _qkv(z, tabs, p, seq_len, tm):
    t = z.shape[0]
    per_seq = seq_len // tm
    tab_spec = pl.BlockSpec((tm, LANE), lambda i: (i % per_seq, 0))
    full = lambda shape: pl.BlockSpec(shape, lambda i: (0,) * len(shape))
    return pl.pallas_call(
        _qkv_kernel,
        grid=(t // tm,),
        in_specs=[
            pl.BlockSpec((tm, QKV_W), lambda i: (i, Z_QKV // QKV_W)),
            tab_spec, tab_spec, tab_spec,
            full((1, Q_LORA)), full((Q_LORA, HP)), full((1, KV_LORA)),
            full((KV_LORA, HP)), full((KV_LORA, HP)), full((1, LANE)), full((1, LANE)),
        ],
        out_specs=[
            pl.BlockSpec((tm, HP), lambda i: (i, 0)),
            pl.BlockSpec((tm, HP), lambda i: (i, 0)),
            pl.BlockSpec((tm, HP), lambda i: (i, 0)),
            pl.BlockSpec((tm, KV_LORA), lambda i: (i, 0)),
        ],
        out_shape=[
            jax.ShapeDtypeStruct((t, HP), BF16),
            jax.ShapeDtypeStruct((t, HP), BF16),
            jax.ShapeDtypeStruct((t, HP), BF16),
            jax.ShapeDtypeStruct((t, KV_LORA), F32),
        ],
        compiler_params=_cparams(("arbitrary",)),
        name="qkv",
    )(z, tabs[0], tabs[1], tabs[2], p["g_qa"], p["wq"], p["g_kva"], p["wk"], p["wv"], p["gq"], p["gk"])


def _kvcache_kernel(ckv_ref, kr_ref, cos_ref, sina_ref, sinb_ref, wk_ref, wv_ref, gk_ref, k_ref, v_ref):
    _build_kv(ckv_ref[...], kr_ref[...], cos_ref[...], sina_ref[...], sinb_ref[...],
              wk_ref, wv_ref, gk_ref[...], k_ref, v_ref)


def _kvcache(ckv, kr_tile, tabs, p):
    t = ckv.shape[0]
    return pl.pallas_call(
        _kvcache_kernel,
        out_shape=[jax.ShapeDtypeStruct((t, HP), BF16), jax.ShapeDtypeStruct((t, HP), BF16)],
        compiler_params=_cparams(None),
        name="kvcache",
    )(ckv, kr_tile, tabs[0], tabs[1], tabs[2], p["wk"], p["wv"], p["gk"])


def _fourier_kernel(nb, u_ref, cc_ref, sc_ref, cl_ref, sl_ref, o_ref, a_scr, b_scr):
    @pl.when(pl.program_id(1) == 0)
    def _():
        for b in range(nb):
            u = u_ref[b].astype(BF16)
            a_scr[:, b * FNET_W:(b + 1) * FNET_W] = _bdot(u, cc_ref[...]).astype(BF16)
            b_scr[:, b * FNET_W:(b + 1) * FNET_W] = _bdot(u, sc_ref[...]).astype(BF16)

    y = _bdot(cl_ref[...], a_scr[...]) - _bdot(sl_ref[...], b_scr[...])
    for b in range(nb):
        o_ref[b] = y[:, b * FNET_W:(b + 1) * FNET_W].astype(BF16)


def _fourier(z3, mats, nb, tq):
    s, l, _ = z3.shape
    cc, sc, cl, sl = mats
    return pl.pallas_call(
        functools.partial(_fourier_kernel, nb),
        grid=(s // nb, l // tq),
        in_specs=[
            pl.BlockSpec((nb, l, FNET_W), lambda i, r: (i, 0, 0)),
            pl.BlockSpec((FNET_W, FNET_W), lambda i, r: (0, 0)),
            pl.BlockSpec((FNET_W, FNET_W), lambda i, r: (0, 0)),
            pl.BlockSpec((tq, l), lambda i, r: (r, 0)),
            pl.BlockSpec((tq, l), lambda i, r: (r, 0)),
        ],
        out_specs=pl.BlockSpec((nb, tq, FNET_W), lambda i, r: (i, r, 0)),
        out_shape=jax.ShapeDtypeStruct((s, l, FNET_W), BF16),
        scratch_shapes=[pltpu.VMEM((l, nb * FNET_W), BF16), pltpu.VMEM((l, nb * FNET_W), BF16)],
        compiler_params=_cparams(("arbitrary", "arbitrary")),
        name="fourier",
    )(z3, cc, sc, cl, sl)


def _load_ext(ref, c, rows, nchunks):
    t0 = pl.multiple_of(c * rows, rows)
    main = ref[pl.ds(t0, rows), :]
    lo = pl.multiple_of(jnp.maximum(t0 - SUB, 0), SUB)
    hi = pl.multiple_of(jnp.minimum(t0 + rows, (nchunks - 1) * rows + rows - SUB), SUB)
    prev = jnp.where(c > 0, ref[pl.ds(lo, SUB), :], 0.0)
    nxt = jnp.where(c < nchunks - 1, ref[pl.ds(hi, SUB), :], 0.0)
    return jnp.concatenate([prev, main, nxt], axis=0)


def _shifted(ext, off, rows):
    n = ext.shape[0]
    r = ext if off == 0 else pltpu.roll(ext, (-off) % n, 0)
    return r[SUB:SUB + rows]


def _gelu_tanh(x):
    return 0.5 * x * (1.0 + jnp.tanh(math.sqrt(2.0 / math.pi) * (x + 0.044715 * (x * x * x))))


def _lru_gates(xc, d, wa_ref, wx_ref, ba_ref, bx_ref, lam_ref):
    xb = xc.astype(BF16)
    r = jax.nn.sigmoid(_bdot(xb, wa_ref[d]) + ba_ref[d])
    i = jax.nn.sigmoid(_bdot(xb, wx_ref[d]) + bx_ref[d])
    nl = -lam_ref[d]
    softplus = jnp.maximum(nl, 0.0) + jnp.log1p(jnp.exp(-jnp.abs(nl)))
    la = (-LRU_C) * r * softplus
    a = jnp.exp(la)
    one_m_a2 = -jnp.tanh(la) * (a * a + 1.0)
    b = jnp.sqrt(one_m_a2) * (i * xc)
    return a, b


def _group_scan(a, b, reverse):
    rows = a.shape[0]
    rm = lax.broadcasted_iota(jnp.int32, a.shape, 0) & (SUB - 1)
    for s in (1, 2, 4):
        if reverse:
            sh, m = rows - s, rm + s <= SUB - 1
        else:
            sh, m = s, rm >= s
        a_sh = pltpu.roll(a, sh, 0)
        b_sh = pltpu.roll(b, sh, 0)
        b = jnp.where(m, a * b_sh + b, b)
        a = jnp.where(m, a * a_sh, a)
    return a, b


def _carry_scan(a, b, carry, reverse):
    ng = a.shape[0] // SUB
    out = [None] * ng
    order = range(ng - 1, -1, -1) if reverse else range(ng)
    for g in order:
        hg = a[g * SUB:(g + 1) * SUB] * carry + b[g * SUB:(g + 1) * SUB]
        out[g] = hg
        carry = hg[0:1] if reverse else hg[SUB - 1:SUB]
    return jnp.concatenate(out, axis=0), carry


def _lru_kernel(rows, nchunks, xl_ref, gl_ref, h0_ref, cw_ref, wa_ref, wx_ref, ba_ref, bx_ref, lam_ref,
                y_ref, fin_ref, xc_scr, hf_scr):
    cw = cw_ref[...]

    def conv_body(c, carry):
        ext = _load_ext(xl_ref, c, rows, nchunks)
        xc = _shifted(ext, -2, rows) * cw[0:1]
        for k in range(1, 4):
            xc = xc + _shifted(ext, k - 2, rows) * cw[k:k + 1]
        xc_scr[pl.ds(pl.multiple_of(c * rows, rows), rows), :] = xc
        return carry

    lax.fori_loop(0, nchunks, conv_body, 0)

    def fw_body(c, carry):
        t0 = pl.multiple_of(c * rows, rows)
        a, b = _lru_gates(xc_scr[pl.ds(t0, rows), :], 0, wa_ref, wx_ref, ba_ref, bx_ref, lam_ref)
        a, b = _group_scan(a, b, False)
        h, carry = _carry_scan(a, b, carry, False)
        hf_scr[pl.ds(t0, rows), :] = h
        return carry

    cf = lax.fori_loop(0, nchunks, fw_body, h0_ref[0:1, :])

    def bw_body(j, carry):
        t0 = pl.multiple_of((nchunks - 1 - j) * rows, rows)
        a, b = _lru_gates(xc_scr[pl.ds(t0, rows), :], 1, wa_ref, wx_ref, ba_ref, bx_ref, lam_ref)
        a, b = _group_scan(a, b, True)
        h, carry = _carry_scan(a, b, carry, True)
        y = (hf_scr[pl.ds(t0, rows), :] + h) * _gelu_tanh(gl_ref[pl.ds(t0, rows), :])
        y_ref[pl.ds(t0, rows), :] = y.astype(BF16)
        return carry

    cb = lax.fori_loop(0, nchunks, bw_body, h0_ref[1:2, :])
    fin_ref[0:1, :] = cf
    fin_ref[1:2, :] = cb


def _lru(z3, h0, p):
    s, l, _ = z3.shape
    rows = min(l, 256)
    nchunks = l // rows
    half = lambda blk: pl.BlockSpec((None, l, LANE), lambda i, j: (i, 0, blk + j))
    wspec = lambda shape: pl.BlockSpec((None,) + shape, lambda i, j: (j,) + (0,) * len(shape))
    return pl.pallas_call(
        functools.partial(_lru_kernel, rows, nchunks),
        grid=(s, 2),
        in_specs=[
            half(2), half(4),
            pl.BlockSpec((None, 2, LANE), lambda i, j: (i, 0, j)),
            wspec((4, LANE)), wspec((2, LANE, LANE)), wspec((2, LANE, LANE)),
            wspec((2, 1, LANE)), wspec((2, 1, LANE)), wspec((2, 1, LANE)),
        ],
        out_specs=[
            pl.BlockSpec((None, l, LANE), lambda i, j: (i, 0, j)),
            pl.BlockSpec((None, 2, LANE), lambda i, j: (i, 0, j)),
        ],
        out_shape=[jax.ShapeDtypeStruct((s, l, LRU_W), BF16), jax.ShapeDtypeStruct((s, 2, LRU_W), F32)],
        scratch_shapes=[pltpu.VMEM((l, LANE), F32), pltpu.VMEM((l, LANE), F32)],
        compiler_params=_cparams(("arbitrary", "arbitrary")),
        name="lru",
    )(z3, z3, h0, p["lru_conv"], p["lru_wa"], p["lru_wx"], p["lru_ba"], p["lru_bx"], p["lru_lam"])


def _sconv_kernel(rows, nchunks, b_ref, c_ref, x_ref, w_ref, y_ref):
    w = w_ref[...]

    def body(c, carry):
        ext = _load_ext(c_ref, c, rows, nchunks) * _load_ext(x_ref, c, rows, nchunks)
        acc = _shifted(ext, -1, rows) * w[0:1]
        acc = acc + _shifted(ext, 0, rows) * w[1:2]
        acc = acc + _shifted(ext, 1, rows) * w[2:3]
        t0 = pl.multiple_of(c * rows, rows)
        y_ref[pl.ds(t0, rows), :] = (b_ref[pl.ds(t0, rows), :] * acc).astype(BF16)
        return carry

    lax.fori_loop(0, nchunks, body, 0)


def _sconv(z3, w):
    s, l, _ = z3.shape
    rows = min(l, 256)
    half = lambda blk: pl.BlockSpec((None, l, LANE), lambda i, j: (i, 0, blk + j))
    return pl.pallas_call(
        functools.partial(_sconv_kernel, rows, l // rows),
        grid=(s, 2),
        in_specs=[half(6), half(8), half(10), pl.BlockSpec((None, 3, LANE), lambda i, j: (j, 0, 0))],
        out_specs=pl.BlockSpec((None, l, LANE), lambda i, j: (i, 0, j)),
        out_shape=jax.ShapeDtypeStruct((s, l, SC_W), BF16),
        compiler_params=_cparams(("arbitrary", "arbitrary")),
        name="sconv",
    )(z3, z3, z3, w)


def _attn_kernel(hp, has_cache, *refs):
    if has_cache:
        q_ref, k_ref, v_ref, kc_ref, vc_ref, o_ref = refs
    else:
        q_ref, k_ref, v_ref, o_ref = refs
    nt = (((1,), (1,)), ((), ()))
    for h in range(hp):
        sl = slice(h * HEAD_PAD, (h + 1) * HEAD_PAD)
        q = q_ref[:, sl]
        s = lax.dot_general(q, k_ref[:, sl], nt, preferred_element_type=F32)
        m = jnp.max(s, axis=-1, keepdims=True)
        if has_cache:
            sc = lax.dot_general(q, kc_ref[:, sl], nt, preferred_element_type=F32)
            m = jnp.maximum(m, jnp.max(sc, axis=-1, keepdims=True))
        e = jnp.exp(s - m)
        den = jnp.sum(e, axis=-1, keepdims=True)
        o = _bdot(e.astype(BF16), v_ref[:, sl])
        if has_cache:
            ec = jnp.exp(sc - m)
            den = den + jnp.sum(ec, axis=-1, keepdims=True)
            o = o + _bdot(ec.astype(BF16), vc_ref[:, sl])
        o_ref[:, sl] = (o / den).astype(BF16)


def _attn(q, k, v, cache, hp, tq):
    s, l, _ = q.shape
    w = hp * HEAD_PAD
    qspec = pl.BlockSpec((None, tq, w), lambda i, h, r: (i, r, h))
    kspec = pl.BlockSpec((None, l, w), lambda i, h, r: (i, 0, h))
    in_specs = [qspec, kspec, kspec]
    args = [q, k, v]
    if cache is not None:
        lc = cache[0].shape[1]
        cspec = pl.BlockSpec((None, lc, w), lambda i, h, r: (i, 0, h))
        in_specs += [cspec, cspec]
        args += list(cache)
    return pl.pallas_call(
        functools.partial(_attn_kernel, hp, cache is not None),
        grid=(s, MLA_HEADS // hp, l // tq),
        in_specs=in_specs,
        out_specs=qspec,
        out_shape=jax.ShapeDtypeStruct((s, l, HP), BF16),
        compiler_params=_cparams(("arbitrary", "arbitrary", "arbitrary")),
        name="attn",
    )(*args)


def _route(gl, el):
    lane = lax.broadcasted_iota(jnp.int32, gl.shape, 1).astype(F32)
    neg = -jnp.inf
    g_ok = lane < N_GROUPS
    glm = jnp.where(g_ok, gl, neg)
    gmax = jnp.max(glm, axis=-1, keepdims=True)
    gsel = jnp.min(jnp.where(glm == gmax, lane, float(LANE)), axis=-1, keepdims=True)
    g_w = 1.0 / jnp.sum(jnp.where(g_ok, jnp.exp(gl - gmax), 0.0), axis=-1, keepdims=True)
    lo = gsel * EXP_PER_GROUP
    in_grp = (lane >= lo) & (lane < lo + EXP_PER_GROUP)
    elm = jnp.where(in_grp, el, neg)
    m1 = jnp.max(elm, axis=-1, keepdims=True)
    i1 = jnp.min(jnp.where(elm == m1, lane, float(LANE)), axis=-1, keepdims=True)
    elm2 = jnp.where(lane == i1, neg, elm)
    m2 = jnp.max(elm2, axis=-1, keepdims=True)
    i2 = jnp.min(jnp.where(elm2 == m2, lane, float(LANE)), axis=-1, keepdims=True)
    e2 = jnp.exp(m2 - m1)
    inv = g_w / (1.0 + e2)
    return jnp.where(lane == i1, inv, 0.0) + jnp.where(lane == i2, inv * e2, 0.0)


def _back_kernel(x_ref, mod_ref, g1_ref, g2_ref, yf_ref, yl_ref, ys_ref, o_ref,
                 wg_ref, wpf_ref, wpl_ref, wps_ref, wpm_ref, wo_ref, wrh_ref, wrl_ref, br_ref,
                 x1_ref, h2_ref, comb_ref):
    d = D_MODEL
    x = x_ref[...]
    sh1, sc1, gt1 = mod_ref[:, 0:d], mod_ref[:, d:2 * d], mod_ref[:, 2 * d:3 * d]
    sh2, sc2 = mod_ref[:, 3 * d:4 * d], mod_ref[:, 4 * d:5 * d]
    hb = (_rms(x, g1_ref[...]) * (1.0 + sc1) + sh1).astype(BF16)
    merged = None
    for j, (y_ref, w_ref) in enumerate(((yf_ref, wpf_ref), (yl_ref, wpl_ref), (ys_ref, wps_ref), (o_ref, wpm_ref))):
        gate = jax.nn.sigmoid(_bdot(hb, wg_ref[:, j * d:(j + 1) * d]))
        term = gate * _bdot(y_ref[...], w_ref[...])
        merged = term if merged is None else merged + term
    x1 = x + gt1 * _bdot(merged.astype(BF16), wo_ref[...])
    x1_ref[...] = x1
    h2 = _rms(x1, g2_ref[...]) * (1.0 + sc2) + sh2
    h2b = h2.astype(BF16)
    h2_ref[...] = h2b
    h2l = (h2 - h2b.astype(F32)).astype(BF16)
    logits = _bdot(h2b, wrh_ref[...]) + _bdot(h2l, wrh_ref[...]) + _bdot(h2b, wrl_ref[...]) + br_ref[...]
    comb_ref[...] = _route(logits[:, 0:LANE], logits[:, LANE:2 * LANE])


def _back(x, mod, yf, yl, ys, o, p, tm):
    t = x.shape[0]
    per_mod = t // (mod.shape[0] * tm)
    row = lambda w: pl.BlockSpec((tm, w), lambda i: (i, 0))
    full = lambda a: pl.BlockSpec(a.shape, lambda i: (0,) * a.ndim, pipeline_mode=pl.Buffered(1))
    weights = [p["w_gate"], p["w_pf"], p["w_pl"], p["w_ps"], p["w_pm"], p["w_out"], p["wr_hi"], p["wr_lo"], p["b_r"]]
    return pl.pallas_call(
        _back_kernel,
        grid=(t // tm,),
        in_specs=[
            row(D_MODEL),
            pl.BlockSpec((None, 1, 6 * D_MODEL), lambda i: (i // per_mod, 0, 0)),
            pl.BlockSpec((1, D_MODEL), lambda i: (0, 0)),
            pl.BlockSpec((1, D_MODEL), lambda i: (0, 0)),
            row(FNET_W), row(LRU_W), row(SC_W), row(HP),
        ] + [full(w) for w in weights],
        out_specs=[row(D_MODEL), row(D_MODEL), row(LANE)],
        out_shape=[
            jax.ShapeDtypeStruct((t, D_MODEL), F32),
            jax.ShapeDtypeStruct((t, D_MODEL), BF16),
            jax.ShapeDtypeStruct((t, LANE), F32),
        ],
        compiler_params=_cparams(("arbitrary",)),
        name="back",
    )(x, mod, p["norm1_g"], p["norm2_g"], yf, yl, ys, o, *weights)


def _moe_kernel(h_ref, comb_ref, x1_ref, mod_ref, w1_ref, w3_ref, w2_ref, o_ref, acc):
    e = pl.program_id(1)

    @pl.when(e == 0)
    def _():
        acc[...] = jnp.zeros_like(acc)

    h = h_ref[...]
    a = _bdot(h, w1_ref[...])
    u = _bdot(h, w3_ref[...])
    comb = comb_ref[...]
    lane = lax.broadcasted_iota(jnp.int32, comb.shape, 1)
    cw = jnp.sum(jnp.where(lane == e, comb, 0.0), axis=-1, keepdims=True)
    mid = (a * jax.nn.sigmoid(a)) * u * cw
    acc[...] += _bdot(mid.astype(BF16), w2_ref[...])

    @pl.when(e == N_EXPERTS - 1)
    def _():
        gt2 = mod_ref[:, 5 * D_MODEL:6 * D_MODEL]
        o_ref[...] = x1_ref[...] + gt2 * acc[...]


def _moe(h2, comb, x1, mod, p, tm):
    t = x1.shape[0]
    per_mod = t // (mod.shape[0] * tm)
    return pl.pallas_call(
        _moe_kernel,
        grid=(t // tm, N_EXPERTS),
        in_specs=[
            pl.BlockSpec((tm, D_MODEL), lambda i, e: (i, 0)),
            pl.BlockSpec((tm, LANE), lambda i, e: (i, 0)),
            pl.BlockSpec((tm, D_MODEL), lambda i, e: (i, 0)),
            pl.BlockSpec((None, 1, 6 * D_MODEL), lambda i, e: (i // per_mod, 0, 0)),
            pl.BlockSpec((None, D_MODEL, D_EXPERT), lambda i, e: (e, 0, 0)),
            pl.BlockSpec((None, D_MODEL, D_EXPERT), lambda i, e: (e, 0, 0)),
            pl.BlockSpec((None, D_EXPERT, D_MODEL), lambda i, e: (e, 0, 0)),
        ],
        out_specs=pl.BlockSpec((tm, D_MODEL), lambda i, e: (i, 0)),
        out_shape=jax.ShapeDtypeStruct((t, D_MODEL), F32),
        scratch_shapes=[pltpu.VMEM((tm, D_MODEL), F32)],
        compiler_params=_cparams(("arbitrary", "arbitrary")),
        name="moe",
    )(h2, comb, x1, mod, p["w1"], p["w3"], p["w2"])


def _channel_dft():
    n = np.arange(FNET_W // FNET_GROUPS)
    ang = 2.0 * np.pi * ((n[:, None] * n[None, :]) % n.size) / n.size
    eye = np.eye(FNET_GROUPS)
    scale = 1.0 / math.sqrt(n.size)
    return (jnp.asarray(np.kron(eye, np.cos(ang) * scale), F32).astype(BF16),
            jnp.asarray(np.kron(eye, np.sin(ang) * scale), F32).astype(BF16))


def _position_dft(l):
    scale = 1.0 / math.sqrt(l)
    if l <= 256:
        n = np.arange(l)
        ang = 2.0 * np.pi * ((n[:, None] * n[None, :]) % l) / l
        return jnp.asarray(np.cos(ang) * scale, F32).astype(BF16), jnp.asarray(np.sin(ang) * scale, F32).astype(BF16)
    m = 64
    n = np.arange(l)
    k = np.arange(m)
    ang_a = 2.0 * np.pi * ((k[:, None] * n[None, :]) % m) / m
    ang_b = 2.0 * np.pi * ((k[:, None] * n[None, :]) % l) / l
    ca, sa = jnp.asarray(np.cos(ang_a), F32)[: l // m, None, :], jnp.asarray(np.sin(ang_a), F32)[: l // m, None, :]
    cb, sb = jnp.asarray(np.cos(ang_b) * scale, F32)[None], jnp.asarray(np.sin(ang_b) * scale, F32)[None]
    cl = (ca * cb - sa * sb).reshape(l, l).astype(BF16)
    sl = (sa * cb + ca * sb).reshape(l, l).astype(BF16)
    return cl, sl


def _rope_tables(l, rotate):
    cos = np.ones((l, LANE))
    sina = np.zeros((l, LANE))
    sinb = np.zeros((l, LANE))
    if rotate:
        t = np.arange(l)
        inv = ROPE_BASE ** (-np.arange(N_FREQ) / N_FREQ)
        ang = np.concatenate([(t // GRID_W)[:, None] * inv, (t % GRID_W)[:, None] * inv], axis=-1)
        half = ROPE // 2
        cos[:, ROPE_OFF:ROPE_OFF + half] = np.cos(ang)
        cos[:, ROPE_OFF + half:ROPE_OFF + ROPE] = np.cos(ang)
        sina[:, ROPE_OFF:ROPE_OFF + half] = -np.sin(ang)
        sinb[:, ROPE_OFF + half:ROPE_OFF + ROPE] = np.sin(ang)
    return tuple(jnp.asarray(a, F32) for a in (cos, sina, sinb))


def _pad_heads(w, lo, hi):
    r = w.shape[0]
    part = w[:, :, lo:hi]
    out = jnp.zeros((r, MLA_HEADS, HEAD_PAD), w.dtype).at[:, :, : hi - lo].set(part)
    return out.reshape(r, HP)


def _blockdiag_halves(w):
    bw = LRU_W // LRU_BLOCKS
    out = jnp.zeros((2, 2, LANE, LANE), w.dtype)
    for half in range(2):
        for k in range(2):
            n = 2 * half + k
            out = out.at[half, :, k * bw:(k + 1) * bw, k * bw:(k + 1) * bw].set(w[:, n])
    return out


def _halves(v):
    return jnp.moveaxis(v.reshape(v.shape[:-1] + (2, LANE)), -2, 0)


def _layer_params(l, a):
    d = D_MODEL
    w_in = a["w_in"][l]
    w_in = jnp.concatenate(
        [w_in[:, :Z_KR], jnp.zeros((d, ROPE_OFF), F32), w_in[:, Z_KR:], jnp.zeros((d, LANE - ROPE_OFF - ROPE), F32)],
        axis=1)
    w_r = jnp.zeros((d, 2 * LANE), F32).at[:, :N_GROUPS].set(a["w_gr"][l]).at[:, LANE:LANE + N_EXPERTS].set(a["w_er"][l])
    b_r = jnp.zeros((1, 2 * LANE), F32).at[0, :N_GROUPS].set(a["b_gr"][l]).at[0, LANE:LANE + N_EXPERTS].set(a["b_er"][l])
    wr_hi = w_r.astype(BF16)
    gpad = lambda g: jnp.zeros((1, LANE), F32).at[0, :QK_DIM].set(g)
    w_pm = jnp.zeros((MLA_HEADS, HEAD_PAD, d), F32).at[:, :V_DIM].set(a["w_pm"][l].reshape(MLA_HEADS, V_DIM, d))
    return {
        "norm1_g": a["norm1_g"][l].reshape(1, d), "norm2_g": a["norm2_g"][l].reshape(1, d),
        "w_in": w_in.astype(BF16),
        "lru_conv": _halves(a["lru_conv"][l]),
        "lru_wa": _blockdiag_halves(a["lru_wa"][l]).astype(BF16),
        "lru_wx": _blockdiag_halves(a["lru_wx"][l]).astype(BF16),
        "lru_ba": _halves(a["lru_ba"][l])[:, :, None, :], "lru_bx": _halves(a["lru_bx"][l])[:, :, None, :],
        "lru_lam": _halves(a["lru_lam"][l])[:, :, None, :],
        "sc_conv": _halves(a["sc_conv"][l]),
        "g_qa": a["g_qa"][l].reshape(1, Q_LORA), "g_kva": a["g_kva"][l].reshape(1, KV_LORA),
        "wq": _pad_heads(a["w_qb"][l], 0, QK_DIM).astype(BF16),
        "wk": _pad_heads(a["w_kvb"][l], 0, NOPE).astype(BF16),
        "wv": _pad_heads(a["w_kvb"][l], NOPE, NOPE + V_DIM).astype(BF16),
        "gq": gpad(a["g_qn"][l]), "gk": gpad(a["g_kn"][l]),
        "w_gate": a["w_gate"][l].astype(BF16),
        "w_pf": a["w_pf"][l].astype(BF16), "w_pl": a["w_pl"][l].astype(BF16), "w_ps": a["w_ps"][l].astype(BF16),
        "w_pm": w_pm.reshape(HP, d).astype(BF16), "w_out": a["w_out"][l].astype(BF16),
        "wr_hi": wr_hi, "wr_lo": (w_r - wr_hi.astype(F32)).astype(BF16), "b_r": b_r,
        "w1": a["w1"][l].astype(BF16), "w3": a["w3"][l].astype(BF16), "w2": a["w2"][l].astype(BF16),
    }


def _trunk_layer(x, mod, p, consts, h0, cache, cfg):
    s, l = cfg["s"], cfg["l"]
    z = _front(x, mod, p["norm1_g"], p["w_in"], cfg["tm_front"])
    z3 = z.reshape(s, l, Z_W)
    q, k, v, ckv = _qkv(z, consts["rope"], p, l, cfg["tm_qkv"])
    yf = _fourier(z3, consts["dft"], cfg["nb"], cfg["tq_f"])
    yl, fin = _lru(z3, h0, p)
    ys = _sconv(z3, p["sc_conv"])
    o = _attn(q.reshape(s, l, HP), k.reshape(s, l, HP), v.reshape(s, l, HP), cache, cfg["hp"], cfg["tq_a"])
    t = s * l
    x1, h2, comb = _back(x, mod, yf.reshape(t, FNET_W), yl.reshape(t, LRU_W), ys.reshape(t, SC_W),
                         o.reshape(t, HP), p, cfg["tm_back"])
    x2 = _moe(h2, comb, x1, mod, p, cfg["tm_moe"])
    return x2, z3, ckv, fin


def kernel(x_prompt, x_sample, cache_ckv, cache_krope, state_rglru, c, c_ctx, norm1_g, norm2_g, w_ada, b_ada, w_in, lru_conv, lru_wa, lru_ba, lru_wx, lru_bx, lru_lam, sc_conv, g_qa, w_qb, g_kva, w_kvb, g_qn, g_kn, w_pf, w_pl, w_ps, w_pm, w_gate, w_out, w_gr, b_gr, w_er, b_er, w1, w3, w2):
    a = dict(norm1_g=norm1_g, norm2_g=norm2_g, w_in=w_in, lru_conv=lru_conv, lru_wa=lru_wa, lru_ba=lru_ba,
             lru_wx=lru_wx, lru_bx=lru_bx, lru_lam=lru_lam, sc_conv=sc_conv, g_qa=g_qa, w_qb=w_qb, g_kva=g_kva,
             w_kvb=w_kvb, g_qn=g_qn, g_kn=g_kn, w_pf=w_pf, w_pl=w_pl, w_ps=w_ps, w_pm=w_pm, w_gate=w_gate,
             w_out=w_out, w_gr=w_gr, b_gr=b_gr, w_er=w_er, b_er=b_er, w1=w1, w3=w3, w2=w2)
    bc, lc, d = x_prompt.shape
    bl, ll, _ = x_sample.shape
    past = cache_ckv.shape[2]

    cond8 = jnp.zeros((SUB, d), F32).at[0].set(c_ctx).at[1:1 + bl].set(c)
    mod_all = _ada(cond8, w_ada, b_ada)

    cc, sc = _channel_dft()
    ctx_consts = {"dft": (cc, sc) + _position_dft(lc), "rope": _rope_tables(lc, False)}
    lat_consts = {"dft": (cc, sc) + _position_dft(ll), "rope": _rope_tables(ll, True)}
    cache_tabs = _rope_tables(bl * past, False)
    ctx_cfg = dict(s=bc, l=lc, tm_front=512, tm_qkv=256, nb=8, tq_f=lc, hp=MLA_HEADS, tq_a=lc, tm_back=256, tm_moe=1024)
    lat_cfg = dict(s=bl, l=ll, tm_front=512, tm_qkv=256, nb=bl, tq_f=512, hp=1, tq_a=256, tm_back=256, tm_moe=1024)

    xp = x_prompt.reshape(bc * lc, d)
    xs = x_sample.reshape(bl * ll, d)
    h0_ctx = jnp.zeros((bc, 2, LRU_W), F32)
    ckv_list, kr_list, lru_list = [], [], []
    for l in range(DEPTH):
        p = _layer_params(l, a)
        mod_ctx = mod_all[l, 0:1].reshape(1, 1, 6 * d)
        mod_lat = mod_all[l, 1:1 + bl].reshape(bl, 1, 6 * d)
        xp, z3, ckv, fin = _trunk_layer(xp, mod_ctx, p, ctx_consts, h0_ctx, None, ctx_cfg)
        ckv_list.append(ckv.reshape(bc, lc, KV_LORA))
        kr_list.append(z3[:, :, Z_KR + ROPE_OFF:Z_KR + ROPE_OFF + ROPE])
        lru_list.append(fin)
        kr_tile = jnp.pad(cache_krope[:, l].reshape(bl * past, ROPE), ((0, 0), (ROPE_OFF, LANE - ROPE_OFF - ROPE)))
        kc, vc = _kvcache(cache_ckv[:, l].reshape(bl * past, KV_LORA), kr_tile, cache_tabs, p)
        cache = (kc.reshape(bl, past, HP), vc.reshape(bl, past, HP))
        xs, _, _, _ = _trunk_layer(xs, mod_lat, p, lat_consts, state_rglru[:, l], cache, lat_cfg)
    return (xp.reshape(bc, lc, d), xs.reshape(bl, ll, d),
            jnp.stack(ckv_list, axis=1), jnp.stack(kr_list, axis=1), jnp.stack(lru_list, axis=1))
```

```python
import functools
import math

import numpy as np
import jax
import jax.numpy as jnp
from jax import lax
from jax.experimental import pallas as pl
from jax.experimental.pallas import tpu as pltpu

F32 = jnp.float32
BF16 = jnp.bfloat16

D_MODEL = 1024
DEPTH = 2
GRID_W = 64
EPS = 1e-6
FNET_W = 256
FNET_GROUPS = 4
LRU_W = 256
LRU_BLOCKS = 4
LRU_C = 8.0
SC_W = 256
MLA_HEADS = 8
Q_LORA = 384
KV_LORA = 256
NOPE = 64
ROPE = 32
V_DIM = 64
QK_DIM = NOPE + ROPE
N_FREQ = ROPE // 4
ROPE_BASE = 10000.0
ATTN_SCALE = QK_DIM ** -0.5
LOG2E = math.log2(math.e)
N_GROUPS = 4
EXP_PER_GROUP = 8
N_EXPERTS = N_GROUPS * EXP_PER_GROUP
D_EXPERT = 256

LANE = 128
SUB = 8
HEAD_PAD = LANE
HP = MLA_HEADS * HEAD_PAD
ROPE_OFF = NOPE
DEN_LANE = V_DIM
Z_QKV = 6 * 256
Z_KR = Z_QKV + Q_LORA + KV_LORA
Z_W = Z_KR + LANE
QKV_W = Q_LORA + KV_LORA + LANE
VMEM_LIMIT = 52 * 1024 * 1024


def _cparams(sem):
    return pltpu.CompilerParams(dimension_semantics=sem, vmem_limit_bytes=VMEM_LIMIT)


def _bdot(a, b):
    return jnp.dot(a, b, preferred_element_type=F32)


def _rms(x, g):
    return x * lax.rsqrt(jnp.mean(x * x, axis=-1, keepdims=True) + EPS) * g


def _ada_kernel(c_ref, w_ref, b_ref, o_ref):
    c = c_ref[...]
    s = (c * jax.nn.sigmoid(c)).astype(BF16)
    o_ref[...] = _bdot(s, w_ref[...].astype(BF16)) + b_ref[...]


def _ada(cond8, w_ada, b_ada):
    nblk = 6 * D_MODEL // 1024
    return pl.pallas_call(
        _ada_kernel,
        grid=(DEPTH, nblk),
        in_specs=[
            pl.BlockSpec((SUB, D_MODEL), lambda l, n: (0, 0)),
            pl.BlockSpec((None, D_MODEL, 1024), lambda l, n: (l, 0, n)),
            pl.BlockSpec((None, 1, 1024), lambda l, n: (l, 0, n)),
        ],
        out_specs=pl.BlockSpec((None, SUB, 1024), lambda l, n: (l, 0, n)),
        out_shape=jax.ShapeDtypeStruct((DEPTH, SUB, 6 * D_MODEL), F32),
        compiler_params=_cparams(("arbitrary", "arbitrary")),
        name="ada",
    )(cond8, w_ada, b_ada.reshape(DEPTH, 1, 6 * D_MODEL))


def _front_kernel(x_ref, mod_ref, g_ref, w_ref, z_ref):
    x = x_ref[...]
    sh = mod_ref[:, 0:D_MODEL]
    sc = mod_ref[:, D_MODEL:2 * D_MODEL]
    h = _rms(x, g_ref[...]) * (1.0 + sc) + sh
    z_ref[...] = _bdot(h.astype(BF16), w_ref[...])


def _front(x, mod, g1, w_in, tm):
    t = x.shape[0]
    per_mod = t // (mod.shape[0] * tm)
    return pl.pallas_call(
        _front_kernel,
        grid=(t // tm,),
        in_specs=[
            pl.BlockSpec((tm, D_MODEL), lambda i: (i, 0)),
            pl.BlockSpec((None, 1, 6 * D_MODEL), lambda i: (i // per_mod, 0, 0)),
            pl.BlockSpec((1, D_MODEL), lambda i: (0, 0)),
            pl.BlockSpec((D_MODEL, Z_W), lambda i: (0, 0)),
        ],
        out_specs=pl.BlockSpec((tm, Z_W), lambda i: (i, 0)),
        out_shape=jax.ShapeDtypeStruct((t, Z_W), F32),
        compiler_params=_cparams(("arbitrary",)),
        name="front",
    )(x, mod, g1, w_in)


def _rope(x, cos, sina, sinb):
    return x * cos + pltpu.roll(x, LANE - ROPE // 2, 1) * sina + pltpu.roll(x, ROPE // 2, 1) * sinb


def _head_norm(x, g):
    ss = jnp.sum(x * x, axis=-1, keepdims=True) * (1.0 / QK_DIM)
    return x * lax.rsqrt(ss + EPS) * g


def _build_kv(ckv, kr_tile, cos, sina, sinb, wk_ref, wv_ref, gk, k_ref, v_ref):
    cb = ckv.astype(BF16)
    kn = _bdot(cb, wk_ref[...])
    for h in range(MLA_HEADS):
        sl = slice(h * HEAD_PAD, (h + 1) * HEAD_PAD)
        kh = _head_norm(kn[:, sl] + kr_tile, gk)
        k_ref[:, sl] = _rope(kh, cos, sina, sinb).astype(BF16)
    v = _bdot(cb, wv_ref[...])
    lane = lax.broadcasted_iota(jnp.int32, v.shape, 1)
    v_ref[...] = jnp.where((lane & (HEAD_PAD - 1)) == DEN_LANE, 1.0, v).astype(BF16)


def _qkv_kernel(z_ref, cos_ref, sina_ref, sinb_ref, gqa_ref, wq_ref, gkva_ref, wk_ref, wv_ref,
                gq_ref, gk_ref, q_ref, k_ref, v_ref, ckv_ref):
    cos, sina, sinb = cos_ref[...], sina_ref[...], sinb_ref[...]
    q_c = z_ref[:, 0:Q_LORA]
    kv_c = z_ref[:, Q_LORA:Q_LORA + KV_LORA]
    kr_tile = z_ref[:, Q_LORA + KV_LORA:QKV_W]
    qf = _bdot(_rms(q_c, gqa_ref[...]).astype(BF16), wq_ref[...])
    gq = gq_ref[...]
    for h in range(MLA_HEADS):
        sl = slice(h * HEAD_PAD, (h + 1) * HEAD_PAD)
        qh = _rope(_head_norm(qf[:, sl], gq), cos, sina, sinb)
        q_ref[:, sl] = (qh * (ATTN_SCALE * LOG2E)).astype(BF16)
    ckv = _rms(kv_c, gkva_ref[...])
    ckv_ref[...] = ckv
    _build_kv(ckv, kr_tile, cos, sina, sinb, wk_ref, wv_ref, gk_ref[...], k_ref, v_ref)


def _qkv(z, tabs, p, seq_len, tm):
    t = z.shape[0]
    if tm > seq_len:
        tabs = [jnp.tile(a, (tm // seq_len, 1)) for a in tabs]
    per_seq = max(seq_len // tm, 1)
    tab_spec = pl.BlockSpec((tm, LANE), lambda i: (i % per_seq, 0))
    full = lambda shape: pl.BlockSpec(shape, lambda i: (0,) * len(shape))
    return pl.pallas_call(
        _qkv_kernel,
        grid=(t // tm,),
        in_specs=[
            pl.BlockSpec((tm, QKV_W), lambda i: (i, Z_QKV // QKV_W)),
            tab_spec, tab_spec, tab_spec,
            full((1, Q_LORA)), full((Q_LORA, HP)), full((1, KV_LORA)),
            full((KV_LORA, HP)), full((KV_LORA, HP)), full((1, LANE)), full((1, LANE)),
        ],
        out_specs=[
            pl.BlockSpec((tm, HP), lambda i: (i, 0)),
            pl.BlockSpec((tm, HP), lambda i: (i, 0)),
            pl.BlockSpec((tm, HP), lambda i: (i, 0)),
            pl.BlockSpec((tm, KV_LORA), lambda i: (i, 0)),
        ],
        out_shape=[
            jax.ShapeDtypeStruct((t, HP), BF16),
            jax.ShapeDtypeStruct((t, HP), BF16),
            jax.ShapeDtypeStruct((t, HP), BF16),
            jax.ShapeDtypeStruct((t, KV_LORA), F32),
        ],
        compiler_params=_cparams(("arbitrary",)),
        name="qkv",
    )(z, tabs[0], tabs[1], tabs[2], p["g_qa"], p["wq"], p["g_kva"], p["wk"], p["wv"], p["gq"], p["gk"])


def _kvcache_kernel(ckv_ref, kr_ref, cos_ref, sina_ref, sinb_ref, wk_ref, wv_ref, gk_ref, k_ref, v_ref):
    _build_kv(ckv_ref[...], kr_ref[...], cos_ref[...], sina_ref[...], sinb_ref[...],
              wk_ref, wv_ref, gk_ref[...], k_ref, v_ref)


def _kvcache(ckv, kr_tile, tabs, p):
    t = ckv.shape[0]
    return pl.pallas_call(
        _kvcache_kernel,
        out_shape=[jax.ShapeDtypeStruct((t, HP), BF16), jax.ShapeDtypeStruct((t, HP), BF16)],
        compiler_params=_cparams(None),
        name="kvcache",
    )(ckv, kr_tile, tabs[0], tabs[1], tabs[2], p["wk"], p["wv"], p["gk"])


def _fourier_kernel(nb, u_ref, cc_ref, sc_ref, cl_ref, sl_ref, o_ref, a_scr, b_scr):
    @pl.when(pl.program_id(1) == 0)
    def _():
        for b in range(nb):
            u = u_ref[b].astype(BF16)
            a_scr[:, b * FNET_W:(b + 1) * FNET_W] = _bdot(u, cc_ref[...]).astype(BF16)
            b_scr[:, b * FNET_W:(b + 1) * FNET_W] = _bdot(u, sc_ref[...]).astype(BF16)

    y = _bdot(cl_ref[...], a_scr[...]) - _bdot(sl_ref[...], b_scr[...])
    for b in range(nb):
        o_ref[b] = y[:, b * FNET_W:(b + 1) * FNET_W].astype(BF16)


def _fourier(z3, mats, nb, tq):
    s, l, _ = z3.shape
    cc, sc, cl, sl = mats
    return pl.pallas_call(
        functools.partial(_fourier_kernel, nb),
        grid=(s // nb, l // tq),
        in_specs=[
            pl.BlockSpec((nb, l, FNET_W), lambda i, r: (i, 0, 0)),
            pl.BlockSpec((FNET_W, FNET_W), lambda i, r: (0, 0)),
            pl.BlockSpec((FNET_W, FNET_W), lambda i, r: (0, 0)),
            pl.BlockSpec((tq, l), lambda i, r: (r, 0)),
            pl.BlockSpec((tq, l), lambda i, r: (r, 0)),
        ],
        out_specs=pl.BlockSpec((nb, tq, FNET_W), lambda i, r: (i, r, 0)),
        out_shape=jax.ShapeDtypeStruct((s, l, FNET_W), BF16),
        scratch_shapes=[pltpu.VMEM((l, nb * FNET_W), BF16), pltpu.VMEM((l, nb * FNET_W), BF16)],
        compiler_params=_cparams(("arbitrary", "arbitrary")),
        name="fourier",
    )(z3, cc, sc, cl, sl)


def _load_ext(ref, c, rows, nchunks):
    t0 = pl.multiple_of(c * rows, rows)
    main = ref[pl.ds(t0, rows), :]
    lo = pl.multiple_of(jnp.maximum(t0 - SUB, 0), SUB)
    hi = pl.multiple_of(jnp.minimum(t0 + rows, (nchunks - 1) * rows + rows - SUB), SUB)
    prev = jnp.where(c > 0, ref[pl.ds(lo, SUB), :], 0.0)
    nxt = jnp.where(c < nchunks - 1, ref[pl.ds(hi, SUB), :], 0.0)
    return jnp.concatenate([prev, main, nxt], axis=0)


def _shifted(ext, off, rows):
    n = ext.shape[0]
    r = ext if off == 0 else pltpu.roll(ext, (-off) % n, 0)
    return r[SUB:SUB + rows]


def _gelu_tanh(x):
    return 0.5 * x * (1.0 + jnp.tanh(math.sqrt(2.0 / math.pi) * (x + 0.044715 * (x * x * x))))


def _lru_gates(xc, d, wa_ref, wx_ref, ba_ref, bx_ref, lam_ref):
    xb = xc.astype(BF16)
    r = jax.nn.sigmoid(_bdot(xb, wa_ref[d]) + ba_ref[d])
    i = jax.nn.sigmoid(_bdot(xb, wx_ref[d]) + bx_ref[d])
    nl = -lam_ref[d]
    softplus = jnp.maximum(nl, 0.0) + jnp.log1p(jnp.exp(-jnp.abs(nl)))
    la = (-LRU_C) * r * softplus
    a = jnp.exp(la)
    one_m_a2 = -jnp.tanh(la) * (a * a + 1.0)
    b = jnp.sqrt(one_m_a2) * (i * xc)
    return a, b


def _group_scan(a, b, reverse):
    rows = a.shape[0]
    rm = lax.broadcasted_iota(jnp.int32, a.shape, 0) & (SUB - 1)
    for s in (1, 2, 4):
        if reverse:
            sh, m = rows - s, rm + s <= SUB - 1
        else:
            sh, m = s, rm >= s
        a_sh = pltpu.roll(a, sh, 0)
        b_sh = pltpu.roll(b, sh, 0)
        b = jnp.where(m, a * b_sh + b, b)
        a = jnp.where(m, a * a_sh, a)
    return a, b


def _carry_scan(a, b, carry, reverse):
    ng = a.shape[0] // SUB
    out = [None] * ng
    order = range(ng - 1, -1, -1) if reverse else range(ng)
    for g in order:
        hg = a[g * SUB:(g + 1) * SUB] * carry + b[g * SUB:(g + 1) * SUB]
        out[g] = hg
        carry = hg[0:1] if reverse else hg[SUB - 1:SUB]
    return jnp.concatenate(out, axis=0), carry


def _lru_kernel(rows, nchunks, xl_ref, gl_ref, h0_ref, cw_ref, wa_ref, wx_ref, ba_ref, bx_ref, lam_ref,
                y_ref, fin_ref, xc_scr, hf_scr):
    cw = cw_ref[...]

    def conv_body(c, carry):
        ext = _load_ext(xl_ref, c, rows, nchunks)
        xc = _shifted(ext, -2, rows) * cw[0:1]
        for k in range(1, 4):
            xc = xc + _shifted(ext, k - 2, rows) * cw[k:k + 1]
        xc_scr[pl.ds(pl.multiple_of(c * rows, rows), rows), :] = xc
        return carry

    lax.fori_loop(0, nchunks, conv_body, 0)

    def fw_body(c, carry):
        t0 = pl.multiple_of(c * rows, rows)
        a, b = _lru_gates(xc_scr[pl.ds(t0, rows), :], 0, wa_ref, wx_ref, ba_ref, bx_ref, lam_ref)
        a, b = _group_scan(a, b, False)
        h, carry = _carry_scan(a, b, carry, False)
        hf_scr[pl.ds(t0, rows), :] = h
        return carry

    cf = lax.fori_loop(0, nchunks, fw_body, h0_ref[0:1, :])

    def bw_body(j, carry):
        t0 = pl.multiple_of((nchunks - 1 - j) * rows, rows)
        a, b = _lru_gates(xc_scr[pl.ds(t0, rows), :], 1, wa_ref, wx_ref, ba_ref, bx_ref, lam_ref)
        a, b = _group_scan(a, b, True)
        h, carry = _carry_scan(a, b, carry, True)
        y = (hf_scr[pl.ds(t0, rows), :] + h) * _gelu_tanh(gl_ref[pl.ds(t0, rows), :])
        y_ref[pl.ds(t0, rows), :] = y.astype(BF16)
        return carry

    cb = lax.fori_loop(0, nchunks, bw_body, h0_ref[1:2, :])
    fin_ref[0:1, :] = cf
    fin_ref[1:2, :] = cb


def _lru(z3, h0, p):
    s, l, _ = z3.shape
    rows = min(l, 256)
    nchunks = l // rows
    half = lambda blk: pl.BlockSpec((None, l, LANE), lambda i, j: (i, 0, blk + j))
    wspec = lambda shape: pl.BlockSpec((None,) + shape, lambda i, j: (j,) + (0,) * len(shape))
    return pl.pallas_call(
        functools.partial(_lru_kernel, rows, nchunks),
        grid=(s, 2),
        in_specs=[
            half(2), half(4),
            pl.BlockSpec((None, 2, LANE), lambda i, j: (i, 0, j)),
            wspec((4, LANE)), wspec((2, LANE, LANE)), wspec((2, LANE, LANE)),
            wspec((2, 1, LANE)), wspec((2, 1, LANE)), wspec((2, 1, LANE)),
        ],
        out_specs=[
            pl.BlockSpec((None, l, LANE), lambda i, j: (i, 0, j)),
            pl.BlockSpec((None, 2, LANE), lambda i, j: (i, 0, j)),
        ],
        out_shape=[jax.ShapeDtypeStruct((s, l, LRU_W), BF16), jax.ShapeDtypeStruct((s, 2, LRU_W), F32)],
        scratch_shapes=[pltpu.VMEM((l, LANE), F32), pltpu.VMEM((l, LANE), F32)],
        compiler_params=_cparams(("arbitrary", "arbitrary")),
        name="lru",
    )(z3, z3, h0, p["lru_conv"], p["lru_wa"], p["lru_wx"], p["lru_ba"], p["lru_bx"], p["lru_lam"])


def _sconv_kernel(rows, nchunks, b_ref, c_ref, x_ref, w_ref, y_ref):
    w = w_ref[...]

    def body(c, carry):
        ext = _load_ext(c_ref, c, rows, nchunks) * _load_ext(x_ref, c, rows, nchunks)
        acc = _shifted(ext, -1, rows) * w[0:1]
        acc = acc + _shifted(ext, 0, rows) * w[1:2]
        acc = acc + _shifted(ext, 1, rows) * w[2:3]
        t0 = pl.multiple_of(c * rows, rows)
        y_ref[pl.ds(t0, rows), :] = (b_ref[pl.ds(t0, rows), :] * acc).astype(BF16)
        return carry

    lax.fori_loop(0, nchunks, body, 0)


def _sconv(z3, w):
    s, l, _ = z3.shape
    rows = min(l, 256)
    half = lambda blk: pl.BlockSpec((None, l, LANE), lambda i, j: (i, 0, blk + j))
    return pl.pallas_call(
        functools.partial(_sconv_kernel, rows, l // rows),
        grid=(s, 2),
        in_specs=[half(6), half(8), half(10), pl.BlockSpec((None, 3, LANE), lambda i, j: (j, 0, 0))],
        out_specs=pl.BlockSpec((None, l, LANE), lambda i, j: (i, 0, j)),
        out_shape=jax.ShapeDtypeStruct((s, l, SC_W), BF16),
        compiler_params=_cparams(("arbitrary", "arbitrary")),
        name="sconv",
    )(z3, z3, z3, w)


def _attn_kernel(hp, has_cache, *refs):
    if has_cache:
        q_ref, k_ref, v_ref, kc_ref, vc_ref, o_ref, s_scr, p_scr, m_scr, sc_scr, pc_scr = refs
    else:
        q_ref, k_ref, v_ref, o_ref, s_scr, p_scr, m_scr = refs
    nt = (((1,), (1,)), ((), ()))
    heads = [slice(h * HEAD_PAD, (h + 1) * HEAD_PAD) for h in range(hp)]

    def scores(h):
        q = q_ref[:, heads[h]]
        s = lax.dot_general(q, k_ref[:, heads[h]], nt, preferred_element_type=F32)
        m = jnp.max(s, axis=-1, keepdims=True)
        s_scr[h % 2] = s
        if has_cache:
            sc = lax.dot_general(q, kc_ref[:, heads[h]], nt, preferred_element_type=F32)
            m = jnp.maximum(m, jnp.max(sc, axis=-1, keepdims=True))
            sc_scr[h % 2] = sc
        m_scr[h % 2] = jnp.broadcast_to(m, m_scr.shape[1:])

    def probs(h):
        m = m_scr[h % 2][:, 0:1]
        p_scr[h % 2] = jnp.exp2(s_scr[h % 2] - m).astype(BF16)
        if has_cache:
            pc_scr[h % 2] = jnp.exp2(sc_scr[h % 2] - m).astype(BF16)

    def weighted_values(h):
        o = _bdot(p_scr[h % 2], v_ref[:, heads[h]])
        if has_cache:
            o = o + _bdot(pc_scr[h % 2], vc_ref[:, heads[h]])
        o_ref[:, heads[h]] = (o / o[:, DEN_LANE:DEN_LANE + 1]).astype(BF16)

    scores(0)
    for h in range(hp):
        if h + 1 < hp:
            scores(h + 1)
        if h > 0:
            weighted_values(h - 1)
        probs(h)
    weighted_values(hp - 1)


def _attn(q, k, v, cache, hp, tq):
    s, l, _ = q.shape
    w = hp * HEAD_PAD
    qspec = pl.BlockSpec((None, tq, w), lambda i, h, r: (i, r, h))
    kspec = pl.BlockSpec((None, l, w), lambda i, h, r: (i, 0, h), pipeline_mode=pl.Buffered(1))
    in_specs = [qspec, kspec, kspec]
    args = [q, k, v]
    if cache is not None:
        lc = cache[0].shape[1]
        cspec = pl.BlockSpec((None, lc, w), lambda i, h, r: (i, 0, h), pipeline_mode=pl.Buffered(1))
        in_specs += [cspec, cspec]
        args += list(cache)
    scratch = [pltpu.VMEM((2, tq, l), F32), pltpu.VMEM((2, tq, l), BF16), pltpu.VMEM((2, tq, LANE), F32)]
    if cache is not None:
        scratch += [pltpu.VMEM((2, tq, lc), F32), pltpu.VMEM((2, tq, lc), BF16)]
    return pl.pallas_call(
        functools.partial(_attn_kernel, hp, cache is not None),
        grid=(s, MLA_HEADS // hp, l // tq),
        in_specs=in_specs,
        out_specs=qspec,
        out_shape=jax.ShapeDtypeStruct((s, l, HP), BF16),
        scratch_shapes=scratch,
        compiler_params=_cparams(("arbitrary", "arbitrary", "arbitrary")),
        name="attn",
    )(*args)


def _route(gl, el):
    lane = lax.broadcasted_iota(jnp.int32, gl.shape, 1).astype(F32)
    neg = -jnp.inf
    g_ok = lane < N_GROUPS
    glm = jnp.where(g_ok, gl, neg)
    gmax = jnp.max(glm, axis=-1, keepdims=True)
    gsel = jnp.min(jnp.where(glm == gmax, lane, float(LANE)), axis=-1, keepdims=True)
    g_w = 1.0 / jnp.sum(jnp.where(g_ok, jnp.exp(gl - gmax), 0.0), axis=-1, keepdims=True)
    lo = gsel * EXP_PER_GROUP
    in_grp = (lane >= lo) & (lane < lo + EXP_PER_GROUP)
    elm = jnp.where(in_grp, el, neg)
    m1 = jnp.max(elm, axis=-1, keepdims=True)
    i1 = jnp.min(jnp.where(elm == m1, lane, float(LANE)), axis=-1, keepdims=True)
    elm2 = jnp.where(lane == i1, neg, elm)
    m2 = jnp.max(elm2, axis=-1, keepdims=True)
    i2 = jnp.min(jnp.where(elm2 == m2, lane, float(LANE)), axis=-1, keepdims=True)
    e2 = jnp.exp(m2 - m1)
    inv = g_w / (1.0 + e2)
    return jnp.where(lane == i1, inv, 0.0) + jnp.where(lane == i2, inv * e2, 0.0)


def _back_kernel(x_ref, mod_ref, g1_ref, g2_ref, yf_ref, yl_ref, ys_ref, o_ref,
                 wg_ref, wpf_ref, wpl_ref, wps_ref, wpm_ref, wo_ref, wrh_ref, wrl_ref, br_ref,
                 x1_ref, h2_ref, comb_ref):
    d = D_MODEL
    x = x_ref[...]
    sh1, sc1, gt1 = mod_ref[:, 0:d], mod_ref[:, d:2 * d], mod_ref[:, 2 * d:3 * d]
    sh2, sc2 = mod_ref[:, 3 * d:4 * d], mod_ref[:, 4 * d:5 * d]
    hb = (_rms(x, g1_ref[...]) * (1.0 + sc1) + sh1).astype(BF16)
    merged = None
    for j, (y_ref, w_ref) in enumerate(((yf_ref, wpf_ref), (yl_ref, wpl_ref), (ys_ref, wps_ref), (o_ref, wpm_ref))):
        gate = jax.nn.sigmoid(_bdot(hb, wg_ref[:, j * d:(j + 1) * d]))
        term = gate * _bdot(y_ref[...], w_ref[...])
        merged = term if merged is None else merged + term
    x1 = x + gt1 * _bdot(merged.astype(BF16), wo_ref[...])
    x1_ref[...] = x1
    h2 = _rms(x1, g2_ref[...]) * (1.0 + sc2) + sh2
    h2b = h2.astype(BF16)
    h2_ref[...] = h2b
    h2l = (h2 - h2b.astype(F32)).astype(BF16)
    logits = _bdot(h2b, wrh_ref[...]) + _bdot(h2l, wrh_ref[...]) + _bdot(h2b, wrl_ref[...]) + br_ref[...]
    comb_ref[...] = _route(logits[:, 0:LANE], logits[:, LANE:2 * LANE])


def _back(x, mod, yf, yl, ys, o, p, tm):
    t = x.shape[0]
    per_mod = t // (mod.shape[0] * tm)
    row = lambda w: pl.BlockSpec((tm, w), lambda i: (i, 0))
    full = lambda a: pl.BlockSpec(a.shape, lambda i: (0,) * a.ndim, pipeline_mode=pl.Buffered(1))
    weights = [p["w_gate"], p["w_pf"], p["w_pl"], p["w_ps"], p["w_pm"], p["w_out"], p["wr_hi"], p["wr_lo"], p["b_r"]]
    return pl.pallas_call(
        _back_kernel,
        grid=(t // tm,),
        in_specs=[
            row(D_MODEL),
            pl.BlockSpec((None, 1, 6 * D_MODEL), lambda i: (i // per_mod, 0, 0)),
            pl.BlockSpec((1, D_MODEL), lambda i: (0, 0)),
            pl.BlockSpec((1, D_MODEL), lambda i: (0, 0)),
            row(FNET_W), row(LRU_W), row(SC_W), row(HP),
        ] + [full(w) for w in weights],
        out_specs=[row(D_MODEL), row(D_MODEL), row(LANE)],
        out_shape=[
            jax.ShapeDtypeStruct((t, D_MODEL), F32),
            jax.ShapeDtypeStruct((t, D_MODEL), BF16),
            jax.ShapeDtypeStruct((t, LANE), F32),
        ],
        compiler_params=_cparams(("arbitrary",)),
        name="back",
    )(x, mod, p["norm1_g"], p["norm2_g"], yf, yl, ys, o, *weights)


def _moe_kernel(h_ref, comb_ref, x1_ref, mod_ref, w1_ref, w3_ref, w2_ref, o_ref, acc):
    e = pl.program_id(1)

    @pl.when(e == 0)
    def _():
        acc[...] = jnp.zeros_like(acc)

    h = h_ref[...]
    a = _bdot(h, w1_ref[...])
    u = _bdot(h, w3_ref[...])
    comb = comb_ref[...]
    lane = lax.broadcasted_iota(jnp.int32, comb.shape, 1)
    cw = jnp.sum(jnp.where(lane == e, comb, 0.0), axis=-1, keepdims=True)
    mid = (a * jax.nn.sigmoid(a)) * u * cw
    acc[...] += _bdot(mid.astype(BF16), w2_ref[...])

    @pl.when(e == N_EXPERTS - 1)
    def _():
        gt2 = mod_ref[:, 5 * D_MODEL:6 * D_MODEL]
        o_ref[...] = x1_ref[...] + gt2 * acc[...]


def _moe(h2, comb, x1, mod, p, tm):
    t = x1.shape[0]
    per_mod = t // (mod.shape[0] * tm)
    return pl.pallas_call(
        _moe_kernel,
        grid=(t // tm, N_EXPERTS),
        in_specs=[
            pl.BlockSpec((tm, D_MODEL), lambda i, e: (i, 0)),
            pl.BlockSpec((tm, LANE), lambda i, e: (i, 0)),
            pl.BlockSpec((tm, D_MODEL), lambda i, e: (i, 0)),
            pl.BlockSpec((None, 1, 6 * D_MODEL), lambda i, e: (i // per_mod, 0, 0)),
            pl.BlockSpec((None, D_MODEL, D_EXPERT), lambda i, e: (e, 0, 0)),
            pl.BlockSpec((None, D_MODEL, D_EXPERT), lambda i, e: (e, 0, 0)),
            pl.BlockSpec((None, D_EXPERT, D_MODEL), lambda i, e: (e, 0, 0)),
        ],
        out_specs=pl.BlockSpec((tm, D_MODEL), lambda i, e: (i, 0)),
        out_shape=jax.ShapeDtypeStruct((t, D_MODEL), F32),
        scratch_shapes=[pltpu.VMEM((tm, D_MODEL), F32)],
        compiler_params=_cparams(("arbitrary", "arbitrary")),
        name="moe",
    )(h2, comb, x1, mod, p["w1"], p["w3"], p["w2"])


def _channel_dft():
    n = np.arange(FNET_W // FNET_GROUPS)
    ang = 2.0 * np.pi * ((n[:, None] * n[None, :]) % n.size) / n.size
    eye = np.eye(FNET_GROUPS)
    scale = 1.0 / math.sqrt(n.size)
    return (jnp.asarray(np.kron(eye, np.cos(ang) * scale), F32).astype(BF16),
            jnp.asarray(np.kron(eye, np.sin(ang) * scale), F32).astype(BF16))


def _position_dft(l):
    scale = 1.0 / math.sqrt(l)
    if l <= 256:
        n = np.arange(l)
        ang = 2.0 * np.pi * ((n[:, None] * n[None, :]) % l) / l
        return jnp.asarray(np.cos(ang) * scale, F32).astype(BF16), jnp.asarray(np.sin(ang) * scale, F32).astype(BF16)
    m = 64
    n = np.arange(l)
    k = np.arange(m)
    ang_a = 2.0 * np.pi * ((k[:, None] * n[None, :]) % m) / m
    ang_b = 2.0 * np.pi * ((k[:, None] * n[None, :]) % l) / l
    ca, sa = jnp.asarray(np.cos(ang_a), F32)[: l // m, None, :], jnp.asarray(np.sin(ang_a), F32)[: l // m, None, :]
    cb, sb = jnp.asarray(np.cos(ang_b) * scale, F32)[None], jnp.asarray(np.sin(ang_b) * scale, F32)[None]
    cl = (ca * cb - sa * sb).reshape(l, l).astype(BF16)
    sl = (sa * cb + ca * sb).reshape(l, l).astype(BF16)
    return cl, sl


def _rope_tables(l, rotate):
    cos = np.ones((l, LANE))
    sina = np.zeros((l, LANE))
    sinb = np.zeros((l, LANE))
    if rotate:
        t = np.arange(l)
        inv = ROPE_BASE ** (-np.arange(N_FREQ) / N_FREQ)
        ang = np.concatenate([(t // GRID_W)[:, None] * inv, (t % GRID_W)[:, None] * inv], axis=-1)
        half = ROPE // 2
        cos[:, ROPE_OFF:ROPE_OFF + half] = np.cos(ang)
        cos[:, ROPE_OFF + half:ROPE_OFF + ROPE] = np.cos(ang)
        sina[:, ROPE_OFF:ROPE_OFF + half] = -np.sin(ang)
        sinb[:, ROPE_OFF + half:ROPE_OFF + ROPE] = np.sin(ang)
    return tuple(jnp.asarray(a, F32) for a in (cos, sina, sinb))


def _pad_heads(w, lo, hi):
    r = w.shape[0]
    part = w[:, :, lo:hi]
    out = jnp.zeros((r, MLA_HEADS, HEAD_PAD), w.dtype).at[:, :, : hi - lo].set(part)
    return out.reshape(r, HP)


def _blockdiag_halves(w):
    bw = LRU_W // LRU_BLOCKS
    out = jnp.zeros((2, 2, LANE, LANE), w.dtype)
    for half in range(2):
        for k in range(2):
            n = 2 * half + k
            out = out.at[half, :, k * bw:(k + 1) * bw, k * bw:(k + 1) * bw].set(w[:, n])
    return out


def _halves(v):
    return jnp.moveaxis(v.reshape(v.shape[:-1] + (2, LANE)), -2, 0)


def _layer_params(l, a):
    d = D_MODEL
    w_in = a["w_in"][l]
    w_in = jnp.concatenate(
        [w_in[:, :Z_KR], jnp.zeros((d, ROPE_OFF), F32), w_in[:, Z_KR:], jnp.zeros((d, LANE - ROPE_OFF - ROPE), F32)],
        axis=1)
    w_r = jnp.zeros((d, 2 * LANE), F32).at[:, :N_GROUPS].set(a["w_gr"][l]).at[:, LANE:LANE + N_EXPERTS].set(a["w_er"][l])
    b_r = jnp.zeros((1, 2 * LANE), F32).at[0, :N_GROUPS].set(a["b_gr"][l]).at[0, LANE:LANE + N_EXPERTS].set(a["b_er"][l])
    wr_hi = w_r.astype(BF16)
    gpad = lambda g: jnp.zeros((1, LANE), F32).at[0, :QK_DIM].set(g)
    w_pm = jnp.zeros((MLA_HEADS, HEAD_PAD, d), F32).at[:, :V_DIM].set(a["w_pm"][l].reshape(MLA_HEADS, V_DIM, d))
    return {
        "norm1_g": a["norm1_g"][l].reshape(1, d), "norm2_g": a["norm2_g"][l].reshape(1, d),
        "w_in": w_in.astype(BF16),
        "lru_conv": _halves(a["lru_conv"][l]),
        "lru_wa": _blockdiag_halves(a["lru_wa"][l]).astype(BF16),
        "lru_wx": _blockdiag_halves(a["lru_wx"][l]).astype(BF16),
        "lru_ba": _halves(a["lru_ba"][l])[:, :, None, :], "lru_bx": _halves(a["lru_bx"][l])[:, :, None, :],
        "lru_lam": _halves(a["lru_lam"][l])[:, :, None, :],
        "sc_conv": _halves(a["sc_conv"][l]),
        "g_qa": a["g_qa"][l].reshape(1, Q_LORA), "g_kva": a["g_kva"][l].reshape(1, KV_LORA),
        "wq": _pad_heads(a["w_qb"][l], 0, QK_DIM).astype(BF16),
        "wk": _pad_heads(a["w_kvb"][l], 0, NOPE).astype(BF16),
        "wv": _pad_heads(a["w_kvb"][l], NOPE, NOPE + V_DIM).astype(BF16),
        "gq": gpad(a["g_qn"][l]), "gk": gpad(a["g_kn"][l]),
        "w_gate": a["w_gate"][l].astype(BF16),
        "w_pf": a["w_pf"][l].astype(BF16), "w_pl": a["w_pl"][l].astype(BF16), "w_ps": a["w_ps"][l].astype(BF16),
        "w_pm": w_pm.reshape(HP, d).astype(BF16), "w_out": a["w_out"][l].astype(BF16),
        "wr_hi": wr_hi, "wr_lo": (w_r - wr_hi.astype(F32)).astype(BF16), "b_r": b_r,
        "w1": a["w1"][l].astype(BF16), "w3": a["w3"][l].astype(BF16), "w2": a["w2"][l].astype(BF16),
    }


def _trunk_layer(x, mod, p, consts, h0, cache, cfg):
    s, l = cfg["s"], cfg["l"]
    z = _front(x, mod, p["norm1_g"], p["w_in"], cfg["tm_front"])
    z3 = z.reshape(s, l, Z_W)
    q, k, v, ckv = _qkv(z, consts["rope"], p, l, cfg["tm_qkv"])
    yf = _fourier(z3, consts["dft"], cfg["nb"], cfg["tq_f"])
    yl, fin = _lru(z3, h0, p)
    ys = _sconv(z3, p["sc_conv"])
    o = _attn(q.reshape(s, l, HP), k.reshape(s, l, HP), v.reshape(s, l, HP), cache, cfg["hp"], cfg["tq_a"])
    t = s * l
    x1, h2, comb = _back(x, mod, yf.reshape(t, FNET_W), yl.reshape(t, LRU_W), ys.reshape(t, SC_W),
                         o.reshape(t, HP), p, cfg["tm_back"])
    x2 = _moe(h2, comb, x1, mod, p, cfg["tm_moe"])
    return x2, z3, ckv, fin


def kernel(x_prompt, x_sample, cache_ckv, cache_krope, state_rglru, c, c_ctx, norm1_g, norm2_g, w_ada, b_ada, w_in, lru_conv, lru_wa, lru_ba, lru_wx, lru_bx, lru_lam, sc_conv, g_qa, w_qb, g_kva, w_kvb, g_qn, g_kn, w_pf, w_pl, w_ps, w_pm, w_gate, w_out, w_gr, b_gr, w_er, b_er, w1, w3, w2):
    a = dict(norm1_g=norm1_g, norm2_g=norm2_g, w_in=w_in, lru_conv=lru_conv, lru_wa=lru_wa, lru_ba=lru_ba,
             lru_wx=lru_wx, lru_bx=lru_bx, lru_lam=lru_lam, sc_conv=sc_conv, g_qa=g_qa, w_qb=w_qb, g_kva=g_kva,
             w_kvb=w_kvb, g_qn=g_qn, g_kn=g_kn, w_pf=w_pf, w_pl=w_pl, w_ps=w_ps, w_pm=w_pm, w_gate=w_gate,
             w_out=w_out, w_gr=w_gr, b_gr=b_gr, w_er=w_er, b_er=b_er, w1=w1, w3=w3, w2=w2)
    bc, lc, d = x_prompt.shape
    bl, ll, _ = x_sample.shape
    past = cache_ckv.shape[2]

    cond8 = jnp.zeros((SUB, d), F32).at[0].set(c_ctx).at[1:1 + bl].set(c)
    mod_all = _ada(cond8, w_ada, b_ada)

    cc, sc = _channel_dft()
    ctx_consts = {"dft": (cc, sc) + _position_dft(lc), "rope": _rope_tables(lc, False)}
    lat_consts = {"dft": (cc, sc) + _position_dft(ll), "rope": _rope_tables(ll, True)}
    cache_tabs = _rope_tables(bl * past, False)
    ctx_cfg = dict(s=bc, l=lc, tm_front=512, tm_qkv=512, nb=8, tq_f=lc, hp=MLA_HEADS, tq_a=lc, tm_back=512, tm_moe=1024)
    lat_cfg = dict(s=bl, l=ll, tm_front=512, tm_qkv=512, nb=bl, tq_f=512, hp=MLA_HEADS, tq_a=256, tm_back=512, tm_moe=1024)

    xp = x_prompt.reshape(bc * lc, d)
    xs = x_sample.reshape(bl * ll, d)
    h0_ctx = jnp.zeros((bc, 2, LRU_W), F32)
    ckv_list, kr_list, lru_list = [], [], []
    for l in range(DEPTH):
        p = _layer_params(l, a)
        mod_ctx = mod_all[l, 0:1].reshape(1, 1, 6 * d)
        mod_lat = mod_all[l, 1:1 + bl].reshape(bl, 1, 6 * d)
        xp, z3, ckv, fin = _trunk_layer(xp, mod_ctx, p, ctx_consts, h0_ctx, None, ctx_cfg)
        ckv_list.append(ckv.reshape(bc, lc, KV_LORA))
        kr_list.append(z3[:, :, Z_KR + ROPE_OFF:Z_KR + ROPE_OFF + ROPE])
        lru_list.append(fin)
        kr_tile = jnp.pad(cache_krope[:, l].reshape(bl * past, ROPE), ((0, 0), (ROPE_OFF, LANE - ROPE_OFF - ROPE)))
        kc, vc = _kvcache(cache_ckv[:, l].reshape(bl * past, KV_LORA), kr_tile, cache_tabs, p)
        cache = (kc.reshape(bl, past, HP), vc.reshape(bl, past, HP))
        xs, _, _, _ = _trunk_layer(xs, mod_lat, p, lat_consts, state_rglru[:, l], cache, lat_cfg)
    return (xp.reshape(bc, lc, d), xs.reshape(bl, ll, d),
            jnp.stack(ckv_list, axis=1), jnp.stack(kr_list, axis=1), jnp.stack(lru_list, axis=1))
```

```python
import functools
import math

import numpy as np
import jax
import jax.numpy as jnp
from jax import lax
from jax.experimental import pallas as pl
from jax.experimental.pallas import tpu as pltpu
from jax.experimental.pallas import tpu_sc as plsc

F32 = jnp.float32
BF16 = jnp.bfloat16

D_MODEL = 1024
DEPTH = 2
GRID_W = 64
EPS = 1e-6
FNET_W = 256
FNET_GROUPS = 4
LRU_W = 256
LRU_BLOCKS = 4
LRU_C = 8.0
SC_W = 256
MLA_HEADS = 8
Q_LORA = 384
KV_LORA = 256
NOPE = 64
ROPE = 32
V_DIM = 64
QK_DIM = NOPE + ROPE
N_FREQ = ROPE // 4
ROPE_BASE = 10000.0
ATTN_SCALE = QK_DIM ** -0.5
LOG2E = math.log2(math.e)
N_GROUPS = 4
EXP_PER_GROUP = 8
N_EXPERTS = N_GROUPS * EXP_PER_GROUP
D_EXPERT = 256

LANE = 128
SUB = 8
HEAD_PAD = LANE
HP = MLA_HEADS * HEAD_PAD
ROPE_OFF = NOPE
DEN_LANE = V_DIM
Z_QKV = 6 * 256
Z_KR = Z_QKV + Q_LORA + KV_LORA
Z_W = Z_KR + LANE
QKV_W = Q_LORA + KV_LORA + LANE
VMEM_LIMIT = 52 * 1024 * 1024
PACK_W = D_MODEL // 2
HI_MASK = -65536
RT_ID0, RT_ID1, RT_W0, RT_W1 = 0, 1, 2, 3
TOP_K = 2
SC_CORES = 2
SC_SUBCORES = 16
SC_CHUNK = 128


def _cparams(sem):
    return pltpu.CompilerParams(dimension_semantics=sem, vmem_limit_bytes=VMEM_LIMIT)


def _bdot(a, b):
    return jnp.dot(a, b, preferred_element_type=F32)


def _rms(x, g):
    return x * lax.rsqrt(jnp.mean(x * x, axis=-1, keepdims=True) + EPS) * g


def _ada_kernel(c_ref, w_ref, b_ref, o_ref):
    c = c_ref[...]
    s = (c * jax.nn.sigmoid(c)).astype(BF16)
    o_ref[...] = _bdot(s, w_ref[...].astype(BF16)) + b_ref[...]


def _ada(cond8, w_ada, b_ada):
    nblk = 6 * D_MODEL // 1024
    return pl.pallas_call(
        _ada_kernel,
        grid=(DEPTH, nblk),
        in_specs=[
            pl.BlockSpec((SUB, D_MODEL), lambda l, n: (0, 0)),
            pl.BlockSpec((None, D_MODEL, 1024), lambda l, n: (l, 0, n)),
            pl.BlockSpec((None, 1, 1024), lambda l, n: (l, 0, n)),
        ],
        out_specs=pl.BlockSpec((None, SUB, 1024), lambda l, n: (l, 0, n)),
        out_shape=jax.ShapeDtypeStruct((DEPTH, SUB, 6 * D_MODEL), F32),
        compiler_params=_cparams(("arbitrary", "arbitrary")),
        name="ada",
    )(cond8, w_ada, b_ada.reshape(DEPTH, 1, 6 * D_MODEL))


def _front_kernel(x_ref, mod_ref, g_ref, w_ref, z_ref):
    x = x_ref[...]
    sh = mod_ref[:, 0:D_MODEL]
    sc = mod_ref[:, D_MODEL:2 * D_MODEL]
    h = _rms(x, g_ref[...]) * (1.0 + sc) + sh
    z_ref[...] = _bdot(h.astype(BF16), w_ref[...])


def _front(x, mod, g1, w_in, tm):
    t = x.shape[0]
    per_mod = t // (mod.shape[0] * tm)
    return pl.pallas_call(
        _front_kernel,
        grid=(t // tm,),
        in_specs=[
            pl.BlockSpec((tm, D_MODEL), lambda i: (i, 0)),
            pl.BlockSpec((None, 1, 6 * D_MODEL), lambda i: (i // per_mod, 0, 0)),
            pl.BlockSpec((1, D_MODEL), lambda i: (0, 0)),
            pl.BlockSpec((D_MODEL, Z_W), lambda i: (0, 0)),
        ],
        out_specs=pl.BlockSpec((tm, Z_W), lambda i: (i, 0)),
        out_shape=jax.ShapeDtypeStruct((t, Z_W), F32),
        compiler_params=_cparams(("arbitrary",)),
        name="front",
    )(x, mod, g1, w_in)


def _rope(x, cos, sina, sinb):
    return x * cos + pltpu.roll(x, LANE - ROPE // 2, 1) * sina + pltpu.roll(x, ROPE // 2, 1) * sinb


def _head_norm(x, g):
    ss = jnp.sum(x * x, axis=-1, keepdims=True) * (1.0 / QK_DIM)
    return x * lax.rsqrt(ss + EPS) * g


def _build_kv(ckv, kr_tile, cos, sina, sinb, wk_ref, wv_ref, gk, k_ref, v_ref):
    cb = ckv.astype(BF16)
    kn = _bdot(cb, wk_ref[...])
    for h in range(MLA_HEADS):
        sl = slice(h * HEAD_PAD, (h + 1) * HEAD_PAD)
        kh = _head_norm(kn[:, sl] + kr_tile, gk)
        k_ref[:, sl] = _rope(kh, cos, sina, sinb).astype(BF16)
    v = _bdot(cb, wv_ref[...])
    lane = lax.broadcasted_iota(jnp.int32, v.shape, 1)
    v_ref[...] = jnp.where((lane & (HEAD_PAD - 1)) == DEN_LANE, 1.0, v).astype(BF16)


def _qkv_kernel(z_ref, cos_ref, sina_ref, sinb_ref, gqa_ref, wq_ref, gkva_ref, wk_ref, wv_ref,
                gq_ref, gk_ref, q_ref, k_ref, v_ref, ckv_ref):
    cos, sina, sinb = cos_ref[...], sina_ref[...], sinb_ref[...]
    q_c = z_ref[:, 0:Q_LORA]
    kv_c = z_ref[:, Q_LORA:Q_LORA + KV_LORA]
    kr_tile = z_ref[:, Q_LORA + KV_LORA:QKV_W]
    qf = _bdot(_rms(q_c, gqa_ref[...]).astype(BF16), wq_ref[...])
    gq = gq_ref[...]
    for h in range(MLA_HEADS):
        sl = slice(h * HEAD_PAD, (h + 1) * HEAD_PAD)
        qh = _rope(_head_norm(qf[:, sl], gq), cos, sina, sinb)
        q_ref[:, sl] = (qh * (ATTN_SCALE * LOG2E)).astype(BF16)
    ckv = _rms(kv_c, gkva_ref[...])
    ckv_ref[...] = ckv
    _build_kv(ckv, kr_tile, cos, sina, sinb, wk_ref, wv_ref, gk_ref[...], k_ref, v_ref)


def _qkv(z, tabs, p, seq_len, tm):
    t = z.shape[0]
    if tm > seq_len:
        tabs = [jnp.tile(a, (tm // seq_len, 1)) for a in tabs]
    per_seq = max(seq_len // tm, 1)
    tab_spec = pl.BlockSpec((tm, LANE), lambda i: (i % per_seq, 0))
    full = lambda shape: pl.BlockSpec(shape, lambda i: (0,) * len(shape))
    return pl.pallas_call(
        _qkv_kernel,
        grid=(t // tm,),
        in_specs=[
            pl.BlockSpec((tm, QKV_W), lambda i: (i, Z_QKV // QKV_W)),
            tab_spec, tab_spec, tab_spec,
            full((1, Q_LORA)), full((Q_LORA, HP)), full((1, KV_LORA)),
            full((KV_LORA, HP)), full((KV_LORA, HP)), full((1, LANE)), full((1, LANE)),
        ],
        out_specs=[
            pl.BlockSpec((tm, HP), lambda i: (i, 0)),
            pl.BlockSpec((tm, HP), lambda i: (i, 0)),
            pl.BlockSpec((tm, HP), lambda i: (i, 0)),
            pl.BlockSpec((tm, KV_LORA), lambda i: (i, 0)),
        ],
        out_shape=[
            jax.ShapeDtypeStruct((t, HP), BF16),
            jax.ShapeDtypeStruct((t, HP), BF16),
            jax.ShapeDtypeStruct((t, HP), BF16),
            jax.ShapeDtypeStruct((t, KV_LORA), F32),
        ],
        compiler_params=_cparams(("arbitrary",)),
        name="qkv",
    )(z, tabs[0], tabs[1], tabs[2], p["g_qa"], p["wq"], p["g_kva"], p["wk"], p["wv"], p["gq"], p["gk"])


def _kvcache_kernel(ckv_ref, kr_ref, cos_ref, sina_ref, sinb_ref, wk_ref, wv_ref, gk_ref, k_ref, v_ref):
    _build_kv(ckv_ref[...], kr_ref[...], cos_ref[...], sina_ref[...], sinb_ref[...],
              wk_ref, wv_ref, gk_ref[...], k_ref, v_ref)


def _kvcache(ckv, kr_tile, tabs, p):
    t = ckv.shape[0]
    return pl.pallas_call(
        _kvcache_kernel,
        out_shape=[jax.ShapeDtypeStruct((t, HP), BF16), jax.ShapeDtypeStruct((t, HP), BF16)],
        compiler_params=_cparams(None),
        name="kvcache",
    )(ckv, kr_tile, tabs[0], tabs[1], tabs[2], p["wk"], p["wv"], p["gk"])


def _fourier_kernel(nb, u_ref, cc_ref, sc_ref, cl_ref, sl_ref, o_ref, a_scr, b_scr):
    @pl.when(pl.program_id(1) == 0)
    def _():
        for b in range(nb):
            u = u_ref[b].astype(BF16)
            a_scr[:, b * FNET_W:(b + 1) * FNET_W] = _bdot(u, cc_ref[...]).astype(BF16)
            b_scr[:, b * FNET_W:(b + 1) * FNET_W] = _bdot(u, sc_ref[...]).astype(BF16)

    y = _bdot(cl_ref[...], a_scr[...]) - _bdot(sl_ref[...], b_scr[...])
    for b in range(nb):
        o_ref[b] = y[:, b * FNET_W:(b + 1) * FNET_W].astype(BF16)


def _fourier(z3, mats, nb, tq):
    s, l, _ = z3.shape
    cc, sc, cl, sl = mats
    return pl.pallas_call(
        functools.partial(_fourier_kernel, nb),
        grid=(s // nb, l // tq),
        in_specs=[
            pl.BlockSpec((nb, l, FNET_W), lambda i, r: (i, 0, 0)),
            pl.BlockSpec((FNET_W, FNET_W), lambda i, r: (0, 0)),
            pl.BlockSpec((FNET_W, FNET_W), lambda i, r: (0, 0)),
            pl.BlockSpec((tq, l), lambda i, r: (r, 0)),
            pl.BlockSpec((tq, l), lambda i, r: (r, 0)),
        ],
        out_specs=pl.BlockSpec((nb, tq, FNET_W), lambda i, r: (i, r, 0)),
        out_shape=jax.ShapeDtypeStruct((s, l, FNET_W), BF16),
        scratch_shapes=[pltpu.VMEM((l, nb * FNET_W), BF16), pltpu.VMEM((l, nb * FNET_W), BF16)],
        compiler_params=_cparams(("arbitrary", "arbitrary")),
        name="fourier",
    )(z3, cc, sc, cl, sl)


def _load_ext(ref, c, rows, nchunks):
    t0 = pl.multiple_of(c * rows, rows)
    main = ref[pl.ds(t0, rows), :]
    lo = pl.multiple_of(jnp.maximum(t0 - SUB, 0), SUB)
    hi = pl.multiple_of(jnp.minimum(t0 + rows, (nchunks - 1) * rows + rows - SUB), SUB)
    prev = jnp.where(c > 0, ref[pl.ds(lo, SUB), :], 0.0)
    nxt = jnp.where(c < nchunks - 1, ref[pl.ds(hi, SUB), :], 0.0)
    return jnp.concatenate([prev, main, nxt], axis=0)


def _shifted(ext, off, rows):
    n = ext.shape[0]
    r = ext if off == 0 else pltpu.roll(ext, (-off) % n, 0)
    return r[SUB:SUB + rows]


def _gelu_tanh(x):
    return 0.5 * x * (1.0 + jnp.tanh(math.sqrt(2.0 / math.pi) * (x + 0.044715 * (x * x * x))))


def _lru_gates(xc, d, wa_ref, wx_ref, ba_ref, bx_ref, lam_ref):
    xb = xc.astype(BF16)
    r = jax.nn.sigmoid(_bdot(xb, wa_ref[d]) + ba_ref[d])
    i = jax.nn.sigmoid(_bdot(xb, wx_ref[d]) + bx_ref[d])
    nl = -lam_ref[d]
    softplus = jnp.maximum(nl, 0.0) + jnp.log1p(jnp.exp(-jnp.abs(nl)))
    la = (-LRU_C) * r * softplus
    a = jnp.exp(la)
    one_m_a2 = -jnp.tanh(la) * (a * a + 1.0)
    b = jnp.sqrt(one_m_a2) * (i * xc)
    return a, b


def _group_scan(a, b, reverse):
    rows = a.shape[0]
    rm = lax.broadcasted_iota(jnp.int32, a.shape, 0) & (SUB - 1)
    for s in (1, 2, 4):
        if reverse:
            sh, m = rows - s, rm + s <= SUB - 1
        else:
            sh, m = s, rm >= s
        a_sh = pltpu.roll(a, sh, 0)
        b_sh = pltpu.roll(b, sh, 0)
        b = jnp.where(m, a * b_sh + b, b)
        a = jnp.where(m, a * a_sh, a)
    return a, b


def _carry_scan(a, b, carry, reverse):
    ng = a.shape[0] // SUB
    out = [None] * ng
    order = range(ng - 1, -1, -1) if reverse else range(ng)
    for g in order:
        hg = a[g * SUB:(g + 1) * SUB] * carry + b[g * SUB:(g + 1) * SUB]
        out[g] = hg
        carry = hg[0:1] if reverse else hg[SUB - 1:SUB]
    return jnp.concatenate(out, axis=0), carry


def _lru_kernel(rows, nchunks, xl_ref, gl_ref, h0_ref, cw_ref, wa_ref, wx_ref, ba_ref, bx_ref, lam_ref,
                y_ref, fin_ref, xc_scr, hf_scr):
    cw = cw_ref[...]

    def conv_body(c, carry):
        ext = _load_ext(xl_ref, c, rows, nchunks)
        xc = _shifted(ext, -2, rows) * cw[0:1]
        for k in range(1, 4):
            xc = xc + _shifted(ext, k - 2, rows) * cw[k:k + 1]
        xc_scr[pl.ds(pl.multiple_of(c * rows, rows), rows), :] = xc
        return carry

    lax.fori_loop(0, nchunks, conv_body, 0)

    def fw_body(c, carry):
        t0 = pl.multiple_of(c * rows, rows)
        a, b = _lru_gates(xc_scr[pl.ds(t0, rows), :], 0, wa_ref, wx_ref, ba_ref, bx_ref, lam_ref)
        a, b = _group_scan(a, b, False)
        h, carry = _carry_scan(a, b, carry, False)
        hf_scr[pl.ds(t0, rows), :] = h
        return carry

    cf = lax.fori_loop(0, nchunks, fw_body, h0_ref[0:1, :])

    def bw_body(j, carry):
        t0 = pl.multiple_of((nchunks - 1 - j) * rows, rows)
        a, b = _lru_gates(xc_scr[pl.ds(t0, rows), :], 1, wa_ref, wx_ref, ba_ref, bx_ref, lam_ref)
        a, b = _group_scan(a, b, True)
        h, carry = _carry_scan(a, b, carry, True)
        y = (hf_scr[pl.ds(t0, rows), :] + h) * _gelu_tanh(gl_ref[pl.ds(t0, rows), :])
        y_ref[pl.ds(t0, rows), :] = y.astype(BF16)
        return carry

    cb = lax.fori_loop(0, nchunks, bw_body, h0_ref[1:2, :])
    fin_ref[0:1, :] = cf
    fin_ref[1:2, :] = cb


def _lru(z3, h0, p):
    s, l, _ = z3.shape
    rows = min(l, 256)
    nchunks = l // rows
    half = lambda blk: pl.BlockSpec((None, l, LANE), lambda i, j: (i, 0, blk + j))
    wspec = lambda shape: pl.BlockSpec((None,) + shape, lambda i, j: (j,) + (0,) * len(shape))
    return pl.pallas_call(
        functools.partial(_lru_kernel, rows, nchunks),
        grid=(s, 2),
        in_specs=[
            half(2), half(4),
            pl.BlockSpec((None, 2, LANE), lambda i, j: (i, 0, j)),
            wspec((4, LANE)), wspec((2, LANE, LANE)), wspec((2, LANE, LANE)),
            wspec((2, 1, LANE)), wspec((2, 1, LANE)), wspec((2, 1, LANE)),
        ],
        out_specs=[
            pl.BlockSpec((None, l, LANE), lambda i, j: (i, 0, j)),
            pl.BlockSpec((None, 2, LANE), lambda i, j: (i, 0, j)),
        ],
        out_shape=[jax.ShapeDtypeStruct((s, l, LRU_W), BF16), jax.ShapeDtypeStruct((s, 2, LRU_W), F32)],
        scratch_shapes=[pltpu.VMEM((l, LANE), F32), pltpu.VMEM((l, LANE), F32)],
        compiler_params=_cparams(("arbitrary", "arbitrary")),
        name="lru",
    )(z3, z3, h0, p["lru_conv"], p["lru_wa"], p["lru_wx"], p["lru_ba"], p["lru_bx"], p["lru_lam"])


def _sconv_kernel(rows, nchunks, b_ref, c_ref, x_ref, w_ref, y_ref):
    w = w_ref[...]

    def body(c, carry):
        ext = _load_ext(c_ref, c, rows, nchunks) * _load_ext(x_ref, c, rows, nchunks)
        acc = _shifted(ext, -1, rows) * w[0:1]
        acc = acc + _shifted(ext, 0, rows) * w[1:2]
        acc = acc + _shifted(ext, 1, rows) * w[2:3]
        t0 = pl.multiple_of(c * rows, rows)
        y_ref[pl.ds(t0, rows), :] = (b_ref[pl.ds(t0, rows), :] * acc).astype(BF16)
        return carry

    lax.fori_loop(0, nchunks, body, 0)


def _sconv(z3, w):
    s, l, _ = z3.shape
    rows = min(l, 256)
    half = lambda blk: pl.BlockSpec((None, l, LANE), lambda i, j: (i, 0, blk + j))
    return pl.pallas_call(
        functools.partial(_sconv_kernel, rows, l // rows),
        grid=(s, 2),
        in_specs=[half(6), half(8), half(10), pl.BlockSpec((None, 3, LANE), lambda i, j: (j, 0, 0))],
        out_specs=pl.BlockSpec((None, l, LANE), lambda i, j: (i, 0, j)),
        out_shape=jax.ShapeDtypeStruct((s, l, SC_W), BF16),
        compiler_params=_cparams(("arbitrary", "arbitrary")),
        name="sconv",
    )(z3, z3, z3, w)


def _attn_kernel(hp, has_cache, *refs):
    if has_cache:
        q_ref, k_ref, v_ref, kc_ref, vc_ref, o_ref, s_scr, p_scr, m_scr, sc_scr, pc_scr = refs
    else:
        q_ref, k_ref, v_ref, o_ref, s_scr, p_scr, m_scr = refs
    nt = (((1,), (1,)), ((), ()))
    heads = [slice(h * HEAD_PAD, (h + 1) * HEAD_PAD) for h in range(hp)]

    def scores(h):
        q = q_ref[:, heads[h]]
        s = lax.dot_general(q, k_ref[:, heads[h]], nt, preferred_element_type=F32)
        m = jnp.max(s, axis=-1, keepdims=True)
        s_scr[h % 2] = s
        if has_cache:
            sc = lax.dot_general(q, kc_ref[:, heads[h]], nt, preferred_element_type=F32)
            m = jnp.maximum(m, jnp.max(sc, axis=-1, keepdims=True))
            sc_scr[h % 2] = sc
        m_scr[h % 2] = jnp.broadcast_to(m, m_scr.shape[1:])

    def probs(h):
        m = m_scr[h % 2][:, 0:1]
        p_scr[h % 2] = jnp.exp2(s_scr[h % 2] - m).astype(BF16)
        if has_cache:
            pc_scr[h % 2] = jnp.exp2(sc_scr[h % 2] - m).astype(BF16)

    def weighted_values(h):
        o = _bdot(p_scr[h % 2], v_ref[:, heads[h]])
        if has_cache:
            o = o + _bdot(pc_scr[h % 2], vc_ref[:, heads[h]])
        o_ref[:, heads[h]] = (o / o[:, DEN_LANE:DEN_LANE + 1]).astype(BF16)

    scores(0)
    for h in range(hp):
        if h + 1 < hp:
            scores(h + 1)
        if h > 0:
            weighted_values(h - 1)
        probs(h)
    weighted_values(hp - 1)


def _attn(q, k, v, cache, hp, tq):
    s, l, _ = q.shape
    w = hp * HEAD_PAD
    qspec = pl.BlockSpec((None, tq, w), lambda i, h, r: (i, r, h))
    kv_mode = dict(pipeline_mode=pl.Buffered(1)) if l // tq > 2 else {}
    kspec = pl.BlockSpec((None, l, w), lambda i, h, r: (i, 0, h), **kv_mode)
    in_specs = [qspec, kspec, kspec]
    args = [q, k, v]
    if cache is not None:
        lc = cache[0].shape[1]
        cspec = pl.BlockSpec((None, lc, w), lambda i, h, r: (i, 0, h), **kv_mode)
        in_specs += [cspec, cspec]
        args += list(cache)
    scratch = [pltpu.VMEM((2, tq, l), F32), pltpu.VMEM((2, tq, l), BF16), pltpu.VMEM((2, tq, LANE), F32)]
    if cache is not None:
        scratch += [pltpu.VMEM((2, tq, lc), F32), pltpu.VMEM((2, tq, lc), BF16)]
    return pl.pallas_call(
        functools.partial(_attn_kernel, hp, cache is not None),
        grid=(s, MLA_HEADS // hp, l // tq),
        in_specs=in_specs,
        out_specs=qspec,
        out_shape=jax.ShapeDtypeStruct((s, l, HP), BF16),
        scratch_shapes=scratch,
        compiler_params=_cparams(("arbitrary", "arbitrary", "arbitrary")),
        name="attn",
    )(*args)


def _route(gl, el):
    lane = lax.broadcasted_iota(jnp.int32, gl.shape, 1).astype(F32)
    neg = -jnp.inf
    g_ok = lane < N_GROUPS
    glm = jnp.where(g_ok, gl, neg)
    gmax = jnp.max(glm, axis=-1, keepdims=True)
    gsel = jnp.min(jnp.where(glm == gmax, lane, float(LANE)), axis=-1, keepdims=True)
    g_w = 1.0 / jnp.sum(jnp.where(g_ok, jnp.exp(gl - gmax), 0.0), axis=-1, keepdims=True)
    lo = gsel * EXP_PER_GROUP
    in_grp = (lane >= lo) & (lane < lo + EXP_PER_GROUP)
    elm = jnp.where(in_grp, el, neg)
    m1 = jnp.max(elm, axis=-1, keepdims=True)
    i1 = jnp.min(jnp.where(elm == m1, lane, float(LANE)), axis=-1, keepdims=True)
    elm2 = jnp.where(lane == i1, neg, elm)
    m2 = jnp.max(elm2, axis=-1, keepdims=True)
    i2 = jnp.min(jnp.where(elm2 == m2, lane, float(LANE)), axis=-1, keepdims=True)
    e2 = jnp.exp(m2 - m1)
    inv = g_w / (1.0 + e2)
    return (jnp.where(lane == RT_ID0, i1, 0.0) + jnp.where(lane == RT_ID1, i2, 0.0)
            + jnp.where(lane == RT_W0, inv, 0.0) + jnp.where(lane == RT_W1, inv * e2, 0.0))


def _pack_bf16_pairs(x):
    k = x.shape[1] // 2
    bits = lax.bitcast_convert_type(x.astype(BF16).astype(F32), jnp.int32)
    return lax.shift_right_logical(bits[:, :k], 16) | (bits[:, k:] & HI_MASK)


def _unpack_bf16_pairs(w):
    lo = lax.bitcast_convert_type(lax.shift_left(w, 16), F32).astype(BF16)
    hi = lax.bitcast_convert_type(w & HI_MASK, F32).astype(BF16)
    return lo, hi


def _back_kernel(x_ref, mod_ref, g1_ref, g2_ref, yf_ref, yl_ref, ys_ref, o_ref,
                 wg_ref, wpf_ref, wpl_ref, wps_ref, wpm_ref, wo_ref, wrh_ref, wrl_ref, br_ref,
                 x1_ref, h2_ref, comb_ref):
    d = D_MODEL
    x = x_ref[...]
    sh1, sc1, gt1 = mod_ref[:, 0:d], mod_ref[:, d:2 * d], mod_ref[:, 2 * d:3 * d]
    sh2, sc2 = mod_ref[:, 3 * d:4 * d], mod_ref[:, 4 * d:5 * d]
    hb = (_rms(x, g1_ref[...]) * (1.0 + sc1) + sh1).astype(BF16)
    merged = None
    for j, (y_ref, w_ref) in enumerate(((yf_ref, wpf_ref), (yl_ref, wpl_ref), (ys_ref, wps_ref), (o_ref, wpm_ref))):
        gate = jax.nn.sigmoid(_bdot(hb, wg_ref[:, j * d:(j + 1) * d]))
        term = gate * _bdot(y_ref[...], w_ref[...])
        merged = term if merged is None else merged + term
    x1 = x + gt1 * _bdot(merged.astype(BF16), wo_ref[...])
    x1_ref[...] = x1
    h2 = _rms(x1, g2_ref[...]) * (1.0 + sc2) + sh2
    h2b = h2.astype(BF16)
    h2_ref[...] = _pack_bf16_pairs(h2)
    h2l = (h2 - h2b.astype(F32)).astype(BF16)
    logits = _bdot(h2b, wrh_ref[...]) + _bdot(h2l, wrh_ref[...]) + _bdot(h2b, wrl_ref[...]) + br_ref[...]
    comb_ref[...] = _route(logits[:, 0:LANE], logits[:, LANE:2 * LANE])


def _back(x, mod, yf, yl, ys, o, p, tm):
    t = x.shape[0]
    per_mod = t // (mod.shape[0] * tm)
    row = lambda w: pl.BlockSpec((tm, w), lambda i: (i, 0))
    full = lambda a: pl.BlockSpec(a.shape, lambda i: (0,) * a.ndim, pipeline_mode=pl.Buffered(1))
    weights = [p["w_gate"], p["w_pf"], p["w_pl"], p["w_ps"], p["w_pm"], p["w_out"], p["wr_hi"], p["wr_lo"], p["b_r"]]
    return pl.pallas_call(
        _back_kernel,
        grid=(t // tm,),
        in_specs=[
            row(D_MODEL),
            pl.BlockSpec((None, 1, 6 * D_MODEL), lambda i: (i // per_mod, 0, 0)),
            pl.BlockSpec((1, D_MODEL), lambda i: (0, 0)),
            pl.BlockSpec((1, D_MODEL), lambda i: (0, 0)),
            row(FNET_W), row(LRU_W), row(SC_W), row(HP),
        ] + [full(w) for w in weights],
        out_specs=[row(D_MODEL), row(PACK_W), row(LANE)],
        out_shape=[
            jax.ShapeDtypeStruct((t, D_MODEL), F32),
            jax.ShapeDtypeStruct((t, PACK_W), jnp.int32),
            jax.ShapeDtypeStruct((t, LANE), F32),
        ],
        compiler_params=_cparams(("arbitrary",)),
        name="back",
    )(x, mod, p["norm1_g"], p["norm2_g"], yf, yl, ys, o, *weights)


def _dispatch_plan(rt, tm):
    ids = jnp.concatenate([rt[:, RT_ID0], rt[:, RT_ID1]]).astype(jnp.int32)
    onehot = (ids[:, None] == jnp.arange(N_EXPERTS, dtype=jnp.int32)[None, :]).astype(jnp.int32)
    csum = jnp.cumsum(onehot, axis=0)
    counts = csum[-1]
    tiles = (counts + tm - 1) // tm
    tile_end = jnp.cumsum(tiles)
    row_start = (tile_end - tiles) * tm
    pos = jnp.sum(onehot * (row_start[None, :] + csum - 1), axis=1)
    n_tiles = ids.shape[0] // tm + N_EXPERTS
    k = jnp.arange(n_tiles, dtype=jnp.int32)
    tile_expert = jnp.minimum(jnp.sum((k[:, None] >= tile_end[None, :]).astype(jnp.int32), axis=1), N_EXPERTS - 1)
    return pos, tile_expert, tile_end[-1:].astype(jnp.int32), n_tiles


def _sc_worker_rows(n_rows):
    workers = SC_CORES * SC_SUBCORES
    per_w = n_rows // workers
    assert per_w * workers == n_rows and per_w % SC_CHUNK == 0
    return per_w


def _sc_mesh():
    return plsc.VectorSubcoreMesh(core_axis_name="c", subcore_axis_name="s")


def _sc_scatter_rows(src, pos, n_out):
    t, w = src.shape
    per_w = _sc_worker_rows(pos.shape[0])

    @functools.partial(
        pl.kernel, mesh=_sc_mesh(), out_type=jax.ShapeDtypeStruct((n_out, w), src.dtype),
        scratch_types=[pltpu.VMEM((SC_CHUNK,), jnp.int32), pltpu.VMEM((SC_CHUNK, w), src.dtype)],
        name="moe_dispatch")
    def run(src_hbm, pos_hbm, out_hbm, idx_v, rows_v):
        wid = lax.axis_index("s") * SC_CORES + lax.axis_index("c")

        @pl.loop(0, per_w // SC_CHUNK)
        def _(i):
            j0 = wid * per_w + i * SC_CHUNK
            pltpu.sync_copy(pos_hbm.at[pl.ds(j0, SC_CHUNK)], idx_v)
            pltpu.sync_copy(src_hbm.at[pl.ds(lax.rem(j0, t), SC_CHUNK)], rows_v)
            pltpu.sync_copy(rows_v, out_hbm.at[idx_v])

    return run(src, pos)


def _sc_gather_rows(src, pos):
    w = src.shape[1]
    n = pos.shape[0]
    per_w = _sc_worker_rows(n)

    @functools.partial(
        pl.kernel, mesh=_sc_mesh(), out_type=jax.ShapeDtypeStruct((n, w), src.dtype),
        scratch_types=[pltpu.VMEM((SC_CHUNK,), jnp.int32), pltpu.VMEM((SC_CHUNK, w), src.dtype)],
        name="moe_return")
    def run(src_hbm, pos_hbm, out_hbm, idx_v, rows_v):
        wid = lax.axis_index("s") * SC_CORES + lax.axis_index("c")

        @pl.loop(0, per_w // SC_CHUNK)
        def _(i):
            j0 = wid * per_w + i * SC_CHUNK
            pltpu.sync_copy(pos_hbm.at[pl.ds(j0, SC_CHUNK)], idx_v)
            pltpu.sync_copy(src_hbm.at[idx_v], rows_v)
            pltpu.sync_copy(rows_v, out_hbm.at[pl.ds(j0, SC_CHUNK)])

    return run(src, pos)


def _experts_kernel(te_ref, nu_ref, x_ref, w1_ref, w3_ref, w2_ref, y_ref):
    half = D_MODEL // 2

    @pl.when(pl.program_id(0) < nu_ref[0])
    def _():
        lo, hi = _unpack_bf16_pairs(x_ref[...])
        a = _bdot(lo, w1_ref[:half, :].astype(BF16)) + _bdot(hi, w1_ref[half:, :].astype(BF16))
        u = _bdot(lo, w3_ref[:half, :].astype(BF16)) + _bdot(hi, w3_ref[half:, :].astype(BF16))
        mid = (a * jax.nn.sigmoid(a)) * u
        y_ref[...] = _pack_bf16_pairs(_bdot(mid.astype(BF16), w2_ref[...].astype(BF16)))


def _experts(xs, tile_expert, n_used, n_tiles, w1, w3, w2, tm):
    wspec = lambda a: pl.BlockSpec((None,) + a.shape[1:], lambda i, te, nu: (te[i], 0, 0))
    grid_spec = pltpu.PrefetchScalarGridSpec(
        num_scalar_prefetch=2,
        grid=(n_tiles,),
        in_specs=[pl.BlockSpec((tm, PACK_W), lambda i, te, nu: (i, 0)), wspec(w1), wspec(w3), wspec(w2)],
        out_specs=pl.BlockSpec((tm, PACK_W), lambda i, te, nu: (i, 0)),
    )
    return pl.pallas_call(
        _experts_kernel,
        grid_spec=grid_spec,
        out_shape=jax.ShapeDtypeStruct(xs.shape, jnp.int32),
        compiler_params=_cparams(("arbitrary",)),
        name="experts",
    )(tile_expert, n_used, xs, w1, w3, w2)


def _combine_kernel(x1_ref, mod_ref, rt_ref, g0_ref, g1_ref, o_ref):
    rt = rt_ref[...]
    lo0, hi0 = _unpack_bf16_pairs(g0_ref[...])
    lo1, hi1 = _unpack_bf16_pairs(g1_ref[...])
    w0 = rt[:, RT_W0:RT_W0 + 1]
    w1 = rt[:, RT_W1:RT_W1 + 1]
    half = D_MODEL // 2
    gt2 = mod_ref[:, 5 * D_MODEL:6 * D_MODEL]
    o_ref[:, :half] = x1_ref[:, :half] + gt2[:, :half] * (w0 * lo0.astype(F32) + w1 * lo1.astype(F32))
    o_ref[:, half:] = x1_ref[:, half:] + gt2[:, half:] * (w0 * hi0.astype(F32) + w1 * hi1.astype(F32))


def _combine(x1, mod, rt, g, tm):
    t = x1.shape[0]
    per_mod = t // (mod.shape[0] * tm)
    return pl.pallas_call(
        _combine_kernel,
        grid=(t // tm,),
        in_specs=[
            pl.BlockSpec((tm, D_MODEL), lambda i: (i, 0)),
            pl.BlockSpec((None, 1, 6 * D_MODEL), lambda i: (i // per_mod, 0, 0)),
            pl.BlockSpec((tm, LANE), lambda i: (i, 0)),
            pl.BlockSpec((tm, PACK_W), lambda i: (i, 0)),
            pl.BlockSpec((tm, PACK_W), lambda i: (i + t // tm, 0)),
        ],
        out_specs=pl.BlockSpec((tm, D_MODEL), lambda i: (i, 0)),
        out_shape=jax.ShapeDtypeStruct((t, D_MODEL), F32),
        compiler_params=_cparams(("arbitrary",)),
        name="combine",
    )(x1, mod, rt, g, g)


def _moe(h2p, rt, x1, mod, p, tm_e, tm_c):
    pos, tile_expert, n_used, n_tiles = _dispatch_plan(rt, tm_e)
    xs = _sc_scatter_rows(h2p, pos, n_tiles * tm_e)
    ys = _experts(xs, tile_expert, n_used, n_tiles, p["w1"], p["w3"], p["w2"], tm_e)
    g = _sc_gather_rows(ys, pos)
    return _combine(x1, mod, rt, g, tm_c)


def _channel_dft():
    n = np.arange(FNET_W // FNET_GROUPS)
    ang = 2.0 * np.pi * ((n[:, None] * n[None, :]) % n.size) / n.size
    eye = np.eye(FNET_GROUPS)
    scale = 1.0 / math.sqrt(n.size)
    return (jnp.asarray(np.kron(eye, np.cos(ang) * scale), F32).astype(BF16),
            jnp.asarray(np.kron(eye, np.sin(ang) * scale), F32).astype(BF16))


def _position_dft(l):
    scale = 1.0 / math.sqrt(l)
    if l <= 256:
        n = np.arange(l)
        ang = 2.0 * np.pi * ((n[:, None] * n[None, :]) % l) / l
        return jnp.asarray(np.cos(ang) * scale, F32).astype(BF16), jnp.asarray(np.sin(ang) * scale, F32).astype(BF16)
    m = 64
    n = np.arange(l)
    k = np.arange(m)
    ang_a = 2.0 * np.pi * ((k[:, None] * n[None, :]) % m) / m
    ang_b = 2.0 * np.pi * ((k[:, None] * n[None, :]) % l) / l
    ca, sa = jnp.asarray(np.cos(ang_a), F32)[: l // m, None, :], jnp.asarray(np.sin(ang_a), F32)[: l // m, None, :]
    cb, sb = jnp.asarray(np.cos(ang_b) * scale, F32)[None], jnp.asarray(np.sin(ang_b) * scale, F32)[None]
    cl = (ca * cb - sa * sb).reshape(l, l).astype(BF16)
    sl = (sa * cb + ca * sb).reshape(l, l).astype(BF16)
    return cl, sl


def _rope_tables(l, rotate):
    cos = np.ones((l, LANE))
    sina = np.zeros((l, LANE))
    sinb = np.zeros((l, LANE))
    if rotate:
        t = np.arange(l)
        inv = ROPE_BASE ** (-np.arange(N_FREQ) / N_FREQ)
        ang = np.concatenate([(t // GRID_W)[:, None] * inv, (t % GRID_W)[:, None] * inv], axis=-1)
        half = ROPE // 2
        cos[:, ROPE_OFF:ROPE_OFF + half] = np.cos(ang)
        cos[:, ROPE_OFF + half:ROPE_OFF + ROPE] = np.cos(ang)
        sina[:, ROPE_OFF:ROPE_OFF + half] = -np.sin(ang)
        sinb[:, ROPE_OFF + half:ROPE_OFF + ROPE] = np.sin(ang)
    return tuple(jnp.asarray(a, F32) for a in (cos, sina, sinb))


def _pad_heads(w, lo, hi):
    r = w.shape[0]
    part = w[:, :, lo:hi]
    out = jnp.zeros((r, MLA_HEADS, HEAD_PAD), w.dtype).at[:, :, : hi - lo].set(part)
    return out.reshape(r, HP)


def _blockdiag_halves(w):
    bw = LRU_W // LRU_BLOCKS
    out = jnp.zeros((2, 2, LANE, LANE), w.dtype)
    for half in range(2):
        for k in range(2):
            n = 2 * half + k
            out = out.at[half, :, k * bw:(k + 1) * bw, k * bw:(k + 1) * bw].set(w[:, n])
    return out


def _halves(v):
    return jnp.moveaxis(v.reshape(v.shape[:-1] + (2, LANE)), -2, 0)


def _layer_params(l, a):
    d = D_MODEL
    w_in = a["w_in"][l]
    w_in = jnp.concatenate(
        [w_in[:, :Z_KR], jnp.zeros((d, ROPE_OFF), F32), w_in[:, Z_KR:], jnp.zeros((d, LANE - ROPE_OFF - ROPE), F32)],
        axis=1)
    w_r = jnp.zeros((d, 2 * LANE), F32).at[:, :N_GROUPS].set(a["w_gr"][l]).at[:, LANE:LANE + N_EXPERTS].set(a["w_er"][l])
    b_r = jnp.zeros((1, 2 * LANE), F32).at[0, :N_GROUPS].set(a["b_gr"][l]).at[0, LANE:LANE + N_EXPERTS].set(a["b_er"][l])
    wr_hi = w_r.astype(BF16)
    gpad = lambda g: jnp.zeros((1, LANE), F32).at[0, :QK_DIM].set(g)
    w_pm = jnp.zeros((MLA_HEADS, HEAD_PAD, d), F32).at[:, :V_DIM].set(a["w_pm"][l].reshape(MLA_HEADS, V_DIM, d))
    return {
        "norm1_g": a["norm1_g"][l].reshape(1, d), "norm2_g": a["norm2_g"][l].reshape(1, d),
        "w_in": w_in.astype(BF16),
        "lru_conv": _halves(a["lru_conv"][l]),
        "lru_wa": _blockdiag_halves(a["lru_wa"][l]).astype(BF16),
        "lru_wx": _blockdiag_halves(a["lru_wx"][l]).astype(BF16),
        "lru_ba": _halves(a["lru_ba"][l])[:, :, None, :], "lru_bx": _halves(a["lru_bx"][l])[:, :, None, :],
        "lru_lam": _halves(a["lru_lam"][l])[:, :, None, :],
        "sc_conv": _halves(a["sc_conv"][l]),
        "g_qa": a["g_qa"][l].reshape(1, Q_LORA), "g_kva": a["g_kva"][l].reshape(1, KV_LORA),
        "wq": _pad_heads(a["w_qb"][l], 0, QK_DIM).astype(BF16),
        "wk": _pad_heads(a["w_kvb"][l], 0, NOPE).astype(BF16),
        "wv": _pad_heads(a["w_kvb"][l], NOPE, NOPE + V_DIM).astype(BF16),
        "gq": gpad(a["g_qn"][l]), "gk": gpad(a["g_kn"][l]),
        "w_gate": a["w_gate"][l].astype(BF16),
        "w_pf": a["w_pf"][l].astype(BF16), "w_pl": a["w_pl"][l].astype(BF16), "w_ps": a["w_ps"][l].astype(BF16),
        "w_pm": w_pm.reshape(HP, d).astype(BF16), "w_out": a["w_out"][l].astype(BF16),
        "wr_hi": wr_hi, "wr_lo": (w_r - wr_hi.astype(F32)).astype(BF16), "b_r": b_r,
        "w1": a["w1"][l], "w3": a["w3"][l], "w2": a["w2"][l],
    }


def _trunk_layer(x, mod, p, consts, h0, cache, cfg):
    s, l = cfg["s"], cfg["l"]
    z = _front(x, mod, p["norm1_g"], p["w_in"], cfg["tm_front"])
    z3 = z.reshape(s, l, Z_W)
    q, k, v, ckv = _qkv(z, consts["rope"], p, l, cfg["tm_qkv"])
    yf = _fourier(z3, consts["dft"], cfg["nb"], cfg["tq_f"])
    yl, fin = _lru(z3, h0, p)
    ys = _sconv(z3, p["sc_conv"])
    o = _attn(q.reshape(s, l, HP), k.reshape(s, l, HP), v.reshape(s, l, HP), cache, cfg["hp"], cfg["tq_a"])
    t = s * l
    x1, h2, comb = _back(x, mod, yf.reshape(t, FNET_W), yl.reshape(t, LRU_W), ys.reshape(t, SC_W),
                         o.reshape(t, HP), p, cfg["tm_back"])
    x2 = _moe(h2, comb, x1, mod, p, cfg["tm_moe"], cfg["tm_comb"])
    return x2, z3, ckv, fin


def kernel(x_prompt, x_sample, cache_ckv, cache_krope, state_rglru, c, c_ctx, norm1_g, norm2_g, w_ada, b_ada, w_in, lru_conv, lru_wa, lru_ba, lru_wx, lru_bx, lru_lam, sc_conv, g_qa, w_qb, g_kva, w_kvb, g_qn, g_kn, w_pf, w_pl, w_ps, w_pm, w_gate, w_out, w_gr, b_gr, w_er, b_er, w1, w3, w2):
    a = dict(norm1_g=norm1_g, norm2_g=norm2_g, w_in=w_in, lru_conv=lru_conv, lru_wa=lru_wa, lru_ba=lru_ba,
             lru_wx=lru_wx, lru_bx=lru_bx, lru_lam=lru_lam, sc_conv=sc_conv, g_qa=g_qa, w_qb=w_qb, g_kva=g_kva,
             w_kvb=w_kvb, g_qn=g_qn, g_kn=g_kn, w_pf=w_pf, w_pl=w_pl, w_ps=w_ps, w_pm=w_pm, w_gate=w_gate,
             w_out=w_out, w_gr=w_gr, b_gr=b_gr, w_er=w_er, b_er=b_er, w1=w1, w3=w3, w2=w2)
    bc, lc, d = x_prompt.shape
    bl, ll, _ = x_sample.shape
    past = cache_ckv.shape[2]

    cond8 = jnp.zeros((SUB, d), F32).at[0].set(c_ctx).at[1:1 + bl].set(c)
    mod_all = _ada(cond8, w_ada, b_ada)

    cc, sc = _channel_dft()
    ctx_consts = {"dft": (cc, sc) + _position_dft(lc), "rope": _rope_tables(lc, False)}
    lat_consts = {"dft": (cc, sc) + _position_dft(ll), "rope": _rope_tables(ll, True)}
    cache_tabs = _rope_tables(bl * past, False)
    ctx_cfg = dict(s=bc, l=lc, tm_front=512, tm_qkv=512, nb=8, tq_f=lc, hp=MLA_HEADS, tq_a=lc, tm_back=512, tm_moe=512, tm_comb=512)
    lat_cfg = dict(s=bl, l=ll, tm_front=512, tm_qkv=512, nb=bl, tq_f=512, hp=MLA_HEADS, tq_a=256, tm_back=512, tm_moe=512, tm_comb=512)

    xp = x_prompt.reshape(bc * lc, d)
    xs = x_sample.reshape(bl * ll, d)
    h0_ctx = jnp.zeros((bc, 2, LRU_W), F32)
    ckv_list, kr_list, lru_list = [], [], []
    for l in range(DEPTH):
        p = _layer_params(l, a)
        mod_ctx = mod_all[l, 0:1].reshape(1, 1, 6 * d)
        mod_lat = mod_all[l, 1:1 + bl].reshape(bl, 1, 6 * d)
        xp, z3, ckv, fin = _trunk_layer(xp, mod_ctx, p, ctx_consts, h0_ctx, None, ctx_cfg)
        ckv_list.append(ckv.reshape(bc, lc, KV_LORA))
        kr_list.append(z3[:, :, Z_KR + ROPE_OFF:Z_KR + ROPE_OFF + ROPE])
        lru_list.append(fin)
        kr_tile = jnp.pad(cache_krope[:, l].reshape(bl * past, ROPE), ((0, 0), (ROPE_OFF, LANE - ROPE_OFF - ROPE)))
        kc, vc = _kvcache(cache_ckv[:, l].reshape(bl * past, KV_LORA), kr_tile, cache_tabs, p)
        cache = (kc.reshape(bl, past, HP), vc.reshape(bl, past, HP))
        xs, _, _, _ = _trunk_layer(xs, mod_lat, p, lat_consts, state_rglru[:, l], cache, lat_cfg)
    return (xp.reshape(bc, lc, d), xs.reshape(bl, ll, d),
            jnp.stack(ckv_list, axis=1), jnp.stack(kr_list, axis=1), jnp.stack(lru_list, axis=1))
```

```python
import functools
import math

import numpy as np
import jax
import jax.numpy as jnp
from jax import lax
from jax.experimental import pallas as pl
from jax.experimental.pallas import tpu as pltpu
from jax.experimental.pallas import tpu_sc as plsc

F32 = jnp.float32
BF16 = jnp.bfloat16

D_MODEL = 1024
DEPTH = 2
GRID_W = 64
EPS = 1e-6
FNET_W = 256
FNET_GROUPS = 4
LRU_W = 256
LRU_BLOCKS = 4
LRU_C = 8.0
SC_W = 256
MLA_HEADS = 8
Q_LORA = 384
KV_LORA = 256
NOPE = 64
ROPE = 32
V_DIM = 64
QK_DIM = NOPE + ROPE
N_FREQ = ROPE // 4
ROPE_BASE = 10000.0
ATTN_SCALE = QK_DIM ** -0.5
LOG2E = math.log2(math.e)
N_GROUPS = 4
EXP_PER_GROUP = 8
N_EXPERTS = N_GROUPS * EXP_PER_GROUP
D_EXPERT = 256

LANE = 128
SUB = 8
HEAD_PAD = LANE
HP = MLA_HEADS * HEAD_PAD
ROPE_OFF = NOPE
DEN_LANE = V_DIM
Z_QKV = 6 * 256
Z_KR = Z_QKV + Q_LORA + KV_LORA
Z_W = Z_KR + LANE
QKV_W = Q_LORA + KV_LORA + LANE
VMEM_LIMIT = 52 * 1024 * 1024
PACK_W = D_MODEL // 2
HI_MASK = -65536
RT_ID0, RT_ID1, RT_W0, RT_W1 = 0, 1, 2, 3
RT_ROWS = 48
TOP_K = 2
DFT_SPLIT = 64
ATTN_SUB = 256
SC_CORES = 2
SC_SUBCORES = 16
SC_CHUNK = 128


def _cparams(sem):
    return pltpu.CompilerParams(dimension_semantics=sem, vmem_limit_bytes=VMEM_LIMIT)


def _bdot(a, b):
    return jnp.dot(a, b, preferred_element_type=F32)


def _rms(x, g):
    return x * lax.rsqrt(jnp.mean(x * x, axis=-1, keepdims=True) + EPS) * g


def _ada_kernel(c_ref, w_ref, b_ref, o_ref):
    c = c_ref[...]
    s = (c * jax.nn.sigmoid(c)).astype(BF16)
    o_ref[...] = _bdot(s, w_ref[...].astype(BF16)) + b_ref[...]


def _ada(cond8, w_ada, b_ada):
    nblk = 6 * D_MODEL // 1024
    return pl.pallas_call(
        _ada_kernel,
        grid=(DEPTH, nblk),
        in_specs=[
            pl.BlockSpec((SUB, D_MODEL), lambda l, n: (0, 0)),
            pl.BlockSpec((None, D_MODEL, 1024), lambda l, n: (l, 0, n)),
            pl.BlockSpec((None, 1, 1024), lambda l, n: (l, 0, n)),
        ],
        out_specs=pl.BlockSpec((None, SUB, 1024), lambda l, n: (l, 0, n)),
        out_shape=jax.ShapeDtypeStruct((DEPTH, SUB, 6 * D_MODEL), F32),
        compiler_params=_cparams(("arbitrary", "arbitrary")),
        name="ada",
    )(cond8, w_ada, b_ada.reshape(DEPTH, 1, 6 * D_MODEL))


def _front_kernel(x_ref, mod_ref, g_ref, w_ref, z_ref):
    x = x_ref[...]
    sh = mod_ref[:, 0:D_MODEL]
    sc = mod_ref[:, D_MODEL:2 * D_MODEL]
    h = _rms(x, g_ref[...]) * (1.0 + sc) + sh
    z_ref[...] = _bdot(h.astype(BF16), w_ref[...])


def _front(x, mod, g1, w_in, tm):
    t = x.shape[0]
    per_mod = t // (mod.shape[0] * tm)
    return pl.pallas_call(
        _front_kernel,
        grid=(t // tm,),
        in_specs=[
            pl.BlockSpec((tm, D_MODEL), lambda i: (i, 0)),
            pl.BlockSpec((None, 1, 6 * D_MODEL), lambda i: (i // per_mod, 0, 0)),
            pl.BlockSpec((1, D_MODEL), lambda i: (0, 0)),
            pl.BlockSpec((D_MODEL, Z_W), lambda i: (0, 0)),
        ],
        out_specs=pl.BlockSpec((tm, Z_W), lambda i: (i, 0)),
        out_shape=jax.ShapeDtypeStruct((t, Z_W), F32),
        compiler_params=_cparams(("arbitrary",)),
        name="front",
    )(x, mod, g1, w_in)


def _rope(x, cos, sina, sinb):
    return x * cos + pltpu.roll(x, LANE - ROPE // 2, 1) * sina + pltpu.roll(x, ROPE // 2, 1) * sinb


def _head_norm(x, g):
    ss = jnp.sum(x * x, axis=-1, keepdims=True) * (1.0 / QK_DIM)
    return x * lax.rsqrt(ss + EPS) * g


def _build_kv(ckv, kr_tile, cos, sina, sinb, wk_ref, wv_ref, gk, k_ref, v_ref):
    cb = ckv.astype(BF16)
    kn = _bdot(cb, wk_ref[...])
    for h in range(MLA_HEADS):
        sl = slice(h * HEAD_PAD, (h + 1) * HEAD_PAD)
        kh = _head_norm(kn[:, sl] + kr_tile, gk)
        k_ref[:, sl] = _rope(kh, cos, sina, sinb).astype(BF16)
    v = _bdot(cb, wv_ref[...])
    lane = lax.broadcasted_iota(jnp.int32, v.shape, 1)
    v_ref[...] = jnp.where((lane & (HEAD_PAD - 1)) == DEN_LANE, 1.0, v).astype(BF16)


def _qkv_kernel(z_ref, cos_ref, sina_ref, sinb_ref, gqa_ref, wq_ref, gkva_ref, wk_ref, wv_ref,
                gq_ref, gk_ref, q_ref, k_ref, v_ref, ckv_ref):
    cos, sina, sinb = cos_ref[...], sina_ref[...], sinb_ref[...]
    q_c = z_ref[:, 0:Q_LORA]
    kv_c = z_ref[:, Q_LORA:Q_LORA + KV_LORA]
    kr_tile = z_ref[:, Q_LORA + KV_LORA:QKV_W]
    qf = _bdot(_rms(q_c, gqa_ref[...]).astype(BF16), wq_ref[...])
    gq = gq_ref[...]
    for h in range(MLA_HEADS):
        sl = slice(h * HEAD_PAD, (h + 1) * HEAD_PAD)
        qh = _rope(_head_norm(qf[:, sl], gq), cos, sina, sinb)
        q_ref[:, sl] = (qh * (ATTN_SCALE * LOG2E)).astype(BF16)
    ckv = _rms(kv_c, gkva_ref[...])
    ckv_ref[...] = ckv
    _build_kv(ckv, kr_tile, cos, sina, sinb, wk_ref, wv_ref, gk_ref[...], k_ref, v_ref)


def _qkv(z, tabs, p, seq_len, tm):
    t = z.shape[0]
    if tm > seq_len:
        tabs = [jnp.tile(a, (tm // seq_len, 1)) for a in tabs]
    per_seq = max(seq_len // tm, 1)
    tab_spec = pl.BlockSpec((tm, LANE), lambda i: (i % per_seq, 0))
    full = lambda shape: pl.BlockSpec(shape, lambda i: (0,) * len(shape))
    return pl.pallas_call(
        _qkv_kernel,
        grid=(t // tm,),
        in_specs=[
            pl.BlockSpec((tm, QKV_W), lambda i: (i, Z_QKV // QKV_W)),
            tab_spec, tab_spec, tab_spec,
            full((1, Q_LORA)), full((Q_LORA, HP)), full((1, KV_LORA)),
            full((KV_LORA, HP)), full((KV_LORA, HP)), full((1, LANE)), full((1, LANE)),
        ],
        out_specs=[
            pl.BlockSpec((tm, HP), lambda i: (i, 0)),
            pl.BlockSpec((tm, HP), lambda i: (i, 0)),
            pl.BlockSpec((tm, HP), lambda i: (i, 0)),
            pl.BlockSpec((tm, KV_LORA), lambda i: (i, 0)),
        ],
        out_shape=[
            jax.ShapeDtypeStruct((t, HP), BF16),
            jax.ShapeDtypeStruct((t, HP), BF16),
            jax.ShapeDtypeStruct((t, HP), BF16),
            jax.ShapeDtypeStruct((t, KV_LORA), F32),
        ],
        compiler_params=_cparams(("arbitrary",)),
        name="qkv",
    )(z, tabs[0], tabs[1], tabs[2], p["g_qa"], p["wq"], p["g_kva"], p["wk"], p["wv"], p["gq"], p["gk"])


def _kvcache_kernel(ckv_ref, kr_ref, cos_ref, sina_ref, sinb_ref, wk_ref, wv_ref, gk_ref, k_ref, v_ref):
    _build_kv(ckv_ref[...], kr_ref[...], cos_ref[...], sina_ref[...], sinb_ref[...],
              wk_ref, wv_ref, gk_ref[...], k_ref, v_ref)


def _kvcache(ckv, kr_tile, tabs, p):
    t = ckv.shape[0]
    return pl.pallas_call(
        _kvcache_kernel,
        out_shape=[jax.ShapeDtypeStruct((t, HP), BF16), jax.ShapeDtypeStruct((t, HP), BF16)],
        compiler_params=_cparams(None),
        name="kvcache",
    )(ckv, kr_tile, tabs[0], tabs[1], tabs[2], p["wk"], p["wv"], p["gk"])


def _fourier_kernel(nb, u_ref, cc_ref, sc_ref, cl_ref, sl_ref, o_ref, a_scr, b_scr):
    @pl.when(pl.program_id(1) == 0)
    def _():
        for b in range(nb):
            u = u_ref[b].astype(BF16)
            a_scr[:, b * FNET_W:(b + 1) * FNET_W] = _bdot(u, cc_ref[...]).astype(BF16)
            b_scr[:, b * FNET_W:(b + 1) * FNET_W] = _bdot(u, sc_ref[...]).astype(BF16)

    y = _bdot(cl_ref[...], a_scr[...]) - _bdot(sl_ref[...], b_scr[...])
    for b in range(nb):
        o_ref[b] = y[:, b * FNET_W:(b + 1) * FNET_W].astype(BF16)


def _fourier_split_kernel(nb, u_ref, cc_ref, sc_ref, ca_ref, sa_ref, cb_ref, sb_ref, o_ref,
                          a_scr, b_scr, cl_scr, sl_scr):
    @pl.when(pl.program_id(1) == 0)
    def _():
        for b in range(nb):
            u = u_ref[b].astype(BF16)
            a_scr[:, b * FNET_W:(b + 1) * FNET_W] = _bdot(u, cc_ref[...]).astype(BF16)
            b_scr[:, b * FNET_W:(b + 1) * FNET_W] = _bdot(u, sc_ref[...]).astype(BF16)

    cb, sb = cb_ref[...], sb_ref[...]
    for j in range(ca_ref.shape[0]):
        ca, sa = ca_ref[j:j + 1, :], sa_ref[j:j + 1, :]
        cl_scr[j * DFT_SPLIT:(j + 1) * DFT_SPLIT, :] = (ca * cb - sa * sb).astype(BF16)
        sl_scr[j * DFT_SPLIT:(j + 1) * DFT_SPLIT, :] = (sa * cb + ca * sb).astype(BF16)
    y = _bdot(cl_scr[...], a_scr[...]) - _bdot(sl_scr[...], b_scr[...])
    for b in range(nb):
        o_ref[b] = y[:, b * FNET_W:(b + 1) * FNET_W].astype(BF16)


def _fourier_split(z3, mats, nb, tq):
    s, l, _ = z3.shape
    cc, sc, ca, sa, cb, sb = mats
    full = lambda a: pl.BlockSpec(a.shape, lambda i, r: (0, 0))
    rows = pl.BlockSpec((tq // DFT_SPLIT, l), lambda i, r: (r, 0))
    return pl.pallas_call(
        functools.partial(_fourier_split_kernel, nb),
        grid=(s // nb, l // tq),
        in_specs=[pl.BlockSpec((nb, l, FNET_W), lambda i, r: (i, 0, 0)), full(cc), full(sc), rows, rows,
                  full(cb), full(sb)],
        out_specs=pl.BlockSpec((nb, tq, FNET_W), lambda i, r: (i, r, 0)),
        out_shape=jax.ShapeDtypeStruct((s, l, FNET_W), BF16),
        scratch_shapes=[pltpu.VMEM((l, nb * FNET_W), BF16), pltpu.VMEM((l, nb * FNET_W), BF16),
                        pltpu.VMEM((tq, l), BF16), pltpu.VMEM((tq, l), BF16)],
        compiler_params=_cparams(("arbitrary", "arbitrary")),
        name="fourier_split",
    )(z3, cc, sc, ca, sa, cb, sb)


def _fourier(z3, mats, nb, tq):
    s, l, _ = z3.shape
    if len(mats) == 6:
        return _fourier_split(z3, mats, nb, tq)
    cc, sc, cl, sl = mats
    return pl.pallas_call(
        functools.partial(_fourier_kernel, nb),
        grid=(s // nb, l // tq),
        in_specs=[
            pl.BlockSpec((nb, l, FNET_W), lambda i, r: (i, 0, 0)),
            pl.BlockSpec((FNET_W, FNET_W), lambda i, r: (0, 0)),
            pl.BlockSpec((FNET_W, FNET_W), lambda i, r: (0, 0)),
            pl.BlockSpec((tq, l), lambda i, r: (r, 0)),
            pl.BlockSpec((tq, l), lambda i, r: (r, 0)),
        ],
        out_specs=pl.BlockSpec((nb, tq, FNET_W), lambda i, r: (i, r, 0)),
        out_shape=jax.ShapeDtypeStruct((s, l, FNET_W), BF16),
        scratch_shapes=[pltpu.VMEM((l, nb * FNET_W), BF16), pltpu.VMEM((l, nb * FNET_W), BF16)],
        compiler_params=_cparams(("arbitrary", "arbitrary")),
        name="fourier",
    )(z3, cc, sc, cl, sl)


def _load_ext(ref, c, rows, nchunks):
    t0 = pl.multiple_of(c * rows, rows)
    main = ref[pl.ds(t0, rows), :]
    lo = pl.multiple_of(jnp.maximum(t0 - SUB, 0), SUB)
    hi = pl.multiple_of(jnp.minimum(t0 + rows, (nchunks - 1) * rows + rows - SUB), SUB)
    prev = jnp.where(c > 0, ref[pl.ds(lo, SUB), :], 0.0)
    nxt = jnp.where(c < nchunks - 1, ref[pl.ds(hi, SUB), :], 0.0)
    return jnp.concatenate([prev, main, nxt], axis=0)


def _shifted(ext, off, rows):
    n = ext.shape[0]
    r = ext if off == 0 else pltpu.roll(ext, (-off) % n, 0)
    return r[SUB:SUB + rows]


def _gelu_tanh(x):
    return 0.5 * x * (1.0 + jnp.tanh(math.sqrt(2.0 / math.pi) * (x + 0.044715 * (x * x * x))))


def _lru_gates(xc, d, wa_ref, wx_ref, ba_ref, bx_ref, lam_ref):
    xb = xc.astype(BF16)
    r = jax.nn.sigmoid(_bdot(xb, wa_ref[d]) + ba_ref[d])
    i = jax.nn.sigmoid(_bdot(xb, wx_ref[d]) + bx_ref[d])
    nl = -lam_ref[d]
    softplus = jnp.maximum(nl, 0.0) + jnp.log1p(jnp.exp(-jnp.abs(nl)))
    la = (-LRU_C) * r * softplus
    a = jnp.exp(la)
    one_m_a2 = -jnp.tanh(la) * (a * a + 1.0)
    b = jnp.sqrt(one_m_a2) * (i * xc)
    return a, b


def _group_scan(a, b, reverse):
    rows = a.shape[0]
    rm = lax.broadcasted_iota(jnp.int32, a.shape, 0) & (SUB - 1)
    for s in (1, 2, 4):
        if reverse:
            sh, m = rows - s, rm + s <= SUB - 1
        else:
            sh, m = s, rm >= s
        a_sh = pltpu.roll(a, sh, 0)
        b_sh = pltpu.roll(b, sh, 0)
        b = jnp.where(m, a * b_sh + b, b)
        a = jnp.where(m, a * a_sh, a)
    return a, b


def _carry_scan(a, b, carry, reverse):
    ng = a.shape[0] // SUB
    out = [None] * ng
    order = range(ng - 1, -1, -1) if reverse else range(ng)
    for g in order:
        hg = a[g * SUB:(g + 1) * SUB] * carry + b[g * SUB:(g + 1) * SUB]
        out[g] = hg
        carry = hg[0:1] if reverse else hg[SUB - 1:SUB]
    return jnp.concatenate(out, axis=0), carry


def _lru_kernel(rows, nchunks, xl_ref, gl_ref, h0_ref, cw_ref, wa_ref, wx_ref, ba_ref, bx_ref, lam_ref,
                y_ref, fin_ref, xc_scr, hf_scr):
    cw = cw_ref[...]

    def conv_body(c, carry):
        ext = _load_ext(xl_ref, c, rows, nchunks)
        xc = _shifted(ext, -2, rows) * cw[0:1]
        for k in range(1, 4):
            xc = xc + _shifted(ext, k - 2, rows) * cw[k:k + 1]
        xc_scr[pl.ds(pl.multiple_of(c * rows, rows), rows), :] = xc
        return carry

    lax.fori_loop(0, nchunks, conv_body, 0)

    def fw_body(c, carry):
        t0 = pl.multiple_of(c * rows, rows)
        a, b = _lru_gates(xc_scr[pl.ds(t0, rows), :], 0, wa_ref, wx_ref, ba_ref, bx_ref, lam_ref)
        a, b = _group_scan(a, b, False)
        h, carry = _carry_scan(a, b, carry, False)
        hf_scr[pl.ds(t0, rows), :] = h
        return carry

    cf = lax.fori_loop(0, nchunks, fw_body, h0_ref[0:1, :])

    def bw_body(j, carry):
        t0 = pl.multiple_of((nchunks - 1 - j) * rows, rows)
        a, b = _lru_gates(xc_scr[pl.ds(t0, rows), :], 1, wa_ref, wx_ref, ba_ref, bx_ref, lam_ref)
        a, b = _group_scan(a, b, True)
        h, carry = _carry_scan(a, b, carry, True)
        y = (hf_scr[pl.ds(t0, rows), :] + h) * _gelu_tanh(gl_ref[pl.ds(t0, rows), :])
        y_ref[pl.ds(t0, rows), :] = y.astype(BF16)
        return carry

    cb = lax.fori_loop(0, nchunks, bw_body, h0_ref[1:2, :])
    fin_ref[0:1, :] = cf
    fin_ref[1:2, :] = cb


def _lru(z3, h0, p):
    s, l, _ = z3.shape
    rows = min(l, 256)
    nchunks = l // rows
    half = lambda blk: pl.BlockSpec((None, l, LANE), lambda i, j: (i, 0, blk + j))
    wspec = lambda shape: pl.BlockSpec((None,) + shape, lambda i, j: (j,) + (0,) * len(shape))
    return pl.pallas_call(
        functools.partial(_lru_kernel, rows, nchunks),
        grid=(s, 2),
        in_specs=[
            half(2), half(4),
            pl.BlockSpec((None, 2, LANE), lambda i, j: (i, 0, j)),
            wspec((4, LANE)), wspec((2, LANE, LANE)), wspec((2, LANE, LANE)),
            wspec((2, 1, LANE)), wspec((2, 1, LANE)), wspec((2, 1, LANE)),
        ],
        out_specs=[
            pl.BlockSpec((None, l, LANE), lambda i, j: (i, 0, j)),
            pl.BlockSpec((None, 2, LANE), lambda i, j: (i, 0, j)),
        ],
        out_shape=[jax.ShapeDtypeStruct((s, l, LRU_W), BF16), jax.ShapeDtypeStruct((s, 2, LRU_W), F32)],
        scratch_shapes=[pltpu.VMEM((l, LANE), F32), pltpu.VMEM((l, LANE), F32)],
        compiler_params=_cparams(("arbitrary", "arbitrary")),
        name="lru",
    )(z3, z3, h0, p["lru_conv"], p["lru_wa"], p["lru_wx"], p["lru_ba"], p["lru_bx"], p["lru_lam"])


def _sconv_kernel(rows, nchunks, b_ref, c_ref, x_ref, w_ref, y_ref):
    w = w_ref[...]

    def body(c, carry):
        ext = _load_ext(c_ref, c, rows, nchunks) * _load_ext(x_ref, c, rows, nchunks)
        acc = _shifted(ext, -1, rows) * w[0:1]
        acc = acc + _shifted(ext, 0, rows) * w[1:2]
        acc = acc + _shifted(ext, 1, rows) * w[2:3]
        t0 = pl.multiple_of(c * rows, rows)
        y_ref[pl.ds(t0, rows), :] = (b_ref[pl.ds(t0, rows), :] * acc).astype(BF16)
        return carry

    lax.fori_loop(0, nchunks, body, 0)


def _sconv(z3, w):
    s, l, _ = z3.shape
    rows = min(l, 256)
    half = lambda blk: pl.BlockSpec((None, l, LANE), lambda i, j: (i, 0, blk + j))
    return pl.pallas_call(
        functools.partial(_sconv_kernel, rows, l // rows),
        grid=(s, 2),
        in_specs=[half(6), half(8), half(10), pl.BlockSpec((None, 3, LANE), lambda i, j: (j, 0, 0))],
        out_specs=pl.BlockSpec((None, l, LANE), lambda i, j: (i, 0, j)),
        out_shape=jax.ShapeDtypeStruct((s, l, SC_W), BF16),
        compiler_params=_cparams(("arbitrary", "arbitrary")),
        name="sconv",
    )(z3, z3, z3, w)


def _attn_kernel(hp, sub, has_cache, *refs):
    if has_cache:
        q_ref, k_ref, v_ref, kc_ref, vc_ref, o_ref, s_scr, p_scr, m_scr, sc_scr, pc_scr = refs
    else:
        q_ref, k_ref, v_ref, o_ref, s_scr, p_scr, m_scr = refs
    nt = (((1,), (1,)), ((), ()))
    items = [(slice(r * sub, (r + 1) * sub), slice(h * HEAD_PAD, (h + 1) * HEAD_PAD))
             for r in range(q_ref.shape[0] // sub) for h in range(hp)]
    n = len(items)

    def scores(i):
        rows, head = items[i]
        q = q_ref[rows, head]
        s = lax.dot_general(q, k_ref[:, head], nt, preferred_element_type=F32)
        m = jnp.max(s, axis=-1, keepdims=True)
        s_scr[i % 2] = s
        if has_cache:
            sc = lax.dot_general(q, kc_ref[:, head], nt, preferred_element_type=F32)
            m = jnp.maximum(m, jnp.max(sc, axis=-1, keepdims=True))
            sc_scr[i % 2] = sc
        m_scr[i % 2] = jnp.broadcast_to(m, m_scr.shape[1:])

    def probs(i):
        m = m_scr[i % 2][:, 0:1]
        p_scr[i % 2] = jnp.exp2(s_scr[i % 2] - m).astype(BF16)
        if has_cache:
            pc_scr[i % 2] = jnp.exp2(sc_scr[i % 2] - m).astype(BF16)

    def weighted_values(i):
        rows, head = items[i]
        o = _bdot(p_scr[i % 2], v_ref[:, head])
        if has_cache:
            o = o + _bdot(pc_scr[i % 2], vc_ref[:, head])
        o_ref[rows, head] = (o / o[:, DEN_LANE:DEN_LANE + 1]).astype(BF16)

    scores(0)
    for i in range(n):
        if i + 1 < n:
            scores(i + 1)
        if i > 0:
            weighted_values(i - 1)
        probs(i)
    weighted_values(n - 1)


def _attn_short_kernel(hp, q_ref, k_ref, v_ref, o_ref):
    nt = (((1,), (1,)), ((), ()))
    for h in range(hp):
        head = slice(h * HEAD_PAD, (h + 1) * HEAD_PAD)
        s = lax.dot_general(q_ref[:, head], k_ref[:, head], nt, preferred_element_type=F32)
        p = jnp.exp2(s - jnp.max(s, axis=-1, keepdims=True)).astype(BF16)
        o = _bdot(p, v_ref[:, head])
        o_ref[:, head] = (o / o[:, DEN_LANE:DEN_LANE + 1]).astype(BF16)


def _attn(q, k, v, cache, hp, tq):
    s, l, _ = q.shape
    w = hp * HEAD_PAD
    qspec = pl.BlockSpec((None, tq, w), lambda i, h, r: (i, r, h))
    if cache is None and tq == l:
        kspec = pl.BlockSpec((None, l, w), lambda i, h, r: (i, 0, h))
        return pl.pallas_call(
            functools.partial(_attn_short_kernel, hp),
            grid=(s, MLA_HEADS // hp, 1),
            in_specs=[qspec, kspec, kspec],
            out_specs=qspec,
            out_shape=jax.ShapeDtypeStruct((s, l, HP), BF16),
            compiler_params=_cparams(("arbitrary", "arbitrary", "arbitrary")),
            name="attn_short",
        )(q, k, v)
    kv_mode = dict(pipeline_mode=pl.Buffered(1))
    kspec = pl.BlockSpec((None, l, w), lambda i, h, r: (i, 0, h), **kv_mode)
    in_specs = [qspec, kspec, kspec]
    args = [q, k, v]
    if cache is not None:
        lc = cache[0].shape[1]
        cspec = pl.BlockSpec((None, lc, w), lambda i, h, r: (i, 0, h), **kv_mode)
        in_specs += [cspec, cspec]
        args += list(cache)
    sub = min(tq, ATTN_SUB)
    scratch = [pltpu.VMEM((2, sub, l), F32), pltpu.VMEM((2, sub, l), BF16), pltpu.VMEM((2, sub, LANE), F32)]
    if cache is not None:
        scratch += [pltpu.VMEM((2, sub, lc), F32), pltpu.VMEM((2, sub, lc), BF16)]
    return pl.pallas_call(
        functools.partial(_attn_kernel, hp, sub, cache is not None),
        grid=(s, MLA_HEADS // hp, l // tq),
        in_specs=in_specs,
        out_specs=qspec,
        out_shape=jax.ShapeDtypeStruct((s, l, HP), BF16),
        scratch_shapes=scratch,
        compiler_params=_cparams(("arbitrary", "arbitrary", "arbitrary")),
        name="attn",
    )(*args)


def _route(lt):
    row = lax.broadcasted_iota(jnp.int32, (SUB, lt.shape[1]), 0).astype(F32)
    neg = -jnp.inf
    gl = lt[0:SUB]
    g_ok = row < N_GROUPS
    glm = jnp.where(g_ok, gl, neg)
    gmax = jnp.max(glm, axis=0, keepdims=True)
    gsel = jnp.min(jnp.where(glm == gmax, row, float(SUB)), axis=0, keepdims=True)
    g_w = 1.0 / jnp.sum(jnp.where(g_ok, jnp.exp(gl - gmax), 0.0), axis=0, keepdims=True)
    es = lt[SUB:2 * SUB]
    for g in range(1, N_GROUPS):
        es = jnp.where(gsel == float(g), lt[(g + 1) * SUB:(g + 2) * SUB], es)
    m1 = jnp.max(es, axis=0, keepdims=True)
    i1 = jnp.min(jnp.where(es == m1, row, float(SUB)), axis=0, keepdims=True)
    es2 = jnp.where(row == i1, neg, es)
    m2 = jnp.max(es2, axis=0, keepdims=True)
    i2 = jnp.min(jnp.where(es2 == m2, row, float(SUB)), axis=0, keepdims=True)
    e2 = jnp.exp(m2 - m1)
    inv = g_w / (1.0 + e2)
    base = gsel * EXP_PER_GROUP
    return (jnp.where(row == RT_ID0, base + i1, 0.0) + jnp.where(row == RT_ID1, base + i2, 0.0)
            + jnp.where(row == RT_W0, inv, 0.0) + jnp.where(row == RT_W1, inv * e2, 0.0))


def _pack_bf16_pairs(x):
    k = x.shape[1] // 2
    bits = lax.bitcast_convert_type(x.astype(BF16).astype(F32), jnp.int32)
    return lax.shift_right_logical(bits[:, :k], 16) | (bits[:, k:] & HI_MASK)


def _unpack_bf16_pairs(w):
    lo = lax.bitcast_convert_type(lax.shift_left(w, 16), F32).astype(BF16)
    hi = lax.bitcast_convert_type(w & HI_MASK, F32).astype(BF16)
    return lo, hi


def _back_kernel(x_ref, mod_ref, g1_ref, g2_ref, yf_ref, yl_ref, ys_ref, o_ref,
                 wg_ref, wpf_ref, wpl_ref, wps_ref, wpm_ref, wo_ref, wrh_ref, wrl_ref, br_ref,
                 x1_ref, h2_ref, comb_ref):
    d = D_MODEL
    x = x_ref[...]
    sh1, sc1, gt1 = mod_ref[:, 0:d], mod_ref[:, d:2 * d], mod_ref[:, 2 * d:3 * d]
    sh2, sc2 = mod_ref[:, 3 * d:4 * d], mod_ref[:, 4 * d:5 * d]
    hb = (_rms(x, g1_ref[...]) * (1.0 + sc1) + sh1).astype(BF16)
    merged = None
    for j, (y_ref, w_ref) in enumerate(((yf_ref, wpf_ref), (yl_ref, wpl_ref), (ys_ref, wps_ref), (o_ref, wpm_ref))):
        gate = jax.nn.sigmoid(_bdot(hb, wg_ref[:, j * d:(j + 1) * d]))
        term = gate * _bdot(y_ref[...], w_ref[...])
        merged = term if merged is None else merged + term
    x1 = x + gt1 * _bdot(merged.astype(BF16), wo_ref[...])
    x1_ref[...] = x1
    h2 = _rms(x1, g2_ref[...]) * (1.0 + sc2) + sh2
    h2b = h2.astype(BF16)
    h2_ref[...] = _pack_bf16_pairs(h2)
    h2l = (h2 - h2b.astype(F32)).astype(BF16)
    nt = (((1,), (1,)), ((), ()))
    rdot = lambda w_ref, h: lax.dot_general(w_ref[...], h, nt, preferred_element_type=F32)
    logits_t = rdot(wrh_ref, h2b) + rdot(wrh_ref, h2l) + rdot(wrl_ref, h2b) + br_ref[...]
    comb_ref[...] = _route(logits_t)


def _back(x, mod, yf, yl, ys, o, p, tm):
    t = x.shape[0]
    per_mod = t // (mod.shape[0] * tm)
    row = lambda w: pl.BlockSpec((tm, w), lambda i: (i, 0))
    full = lambda a: pl.BlockSpec(a.shape, lambda i: (0,) * a.ndim, pipeline_mode=pl.Buffered(1))
    weights = [p["w_gate"], p["w_pf"], p["w_pl"], p["w_ps"], p["w_pm"], p["w_out"], p["wr_hi"], p["wr_lo"], p["b_r"]]
    return pl.pallas_call(
        _back_kernel,
        grid=(t // tm,),
        in_specs=[
            row(D_MODEL),
            pl.BlockSpec((None, 1, 6 * D_MODEL), lambda i: (i // per_mod, 0, 0)),
            pl.BlockSpec((1, D_MODEL), lambda i: (0, 0)),
            pl.BlockSpec((1, D_MODEL), lambda i: (0, 0)),
            row(FNET_W), row(LRU_W), row(SC_W), row(HP),
        ] + [full(w) for w in weights],
        out_specs=[row(D_MODEL), row(PACK_W), pl.BlockSpec((SUB, tm), lambda i: (0, i))],
        out_shape=[
            jax.ShapeDtypeStruct((t, D_MODEL), F32),
            jax.ShapeDtypeStruct((t, PACK_W), jnp.int32),
            jax.ShapeDtypeStruct((SUB, t), F32),
        ],
        compiler_params=_cparams(("arbitrary",)),
        name="back",
    )(x, mod, p["norm1_g"], p["norm2_g"], yf, yl, ys, o, *weights)


def _dispatch_plan(rt, tm):
    ids = jnp.concatenate([rt[RT_ID0], rt[RT_ID1]]).astype(jnp.int32)
    onehot = (ids[:, None] == jnp.arange(N_EXPERTS, dtype=jnp.int32)[None, :]).astype(jnp.int32)
    csum = jnp.cumsum(onehot, axis=0)
    counts = csum[-1]
    tiles = (counts + tm - 1) // tm
    tile_end = jnp.cumsum(tiles)
    row_start = (tile_end - tiles) * tm
    pos = jnp.sum(onehot * (row_start[None, :] + csum - 1), axis=1)
    n_tiles = ids.shape[0] // tm + N_EXPERTS
    k = jnp.arange(n_tiles, dtype=jnp.int32)
    tile_expert = jnp.minimum(jnp.sum((k[:, None] >= tile_end[None, :]).astype(jnp.int32), axis=1), N_EXPERTS - 1)
    return pos, tile_expert, tile_end[-1:].astype(jnp.int32), n_tiles


def _sc_worker_rows(n_rows):
    workers = SC_CORES * SC_SUBCORES
    per_w = n_rows // workers
    assert per_w * workers == n_rows and per_w % SC_CHUNK == 0
    return per_w


def _sc_mesh():
    return plsc.VectorSubcoreMesh(core_axis_name="c", subcore_axis_name="s")


def _sc_scatter_rows(src, pos, n_out):
    t, w = src.shape
    per_w = _sc_worker_rows(pos.shape[0])

    @functools.partial(
        pl.kernel, mesh=_sc_mesh(), out_type=jax.ShapeDtypeStruct((n_out, w), src.dtype),
        scratch_types=[pltpu.VMEM((SC_CHUNK,), jnp.int32), pltpu.VMEM((SC_CHUNK, w), src.dtype)],
        name="moe_dispatch")
    def run(src_hbm, pos_hbm, out_hbm, idx_v, rows_v):
        wid = lax.axis_index("s") * SC_CORES + lax.axis_index("c")

        @pl.loop(0, per_w // SC_CHUNK)
        def _(i):
            j0 = wid * per_w + i * SC_CHUNK
            pltpu.sync_copy(pos_hbm.at[pl.ds(j0, SC_CHUNK)], idx_v)
            pltpu.sync_copy(src_hbm.at[pl.ds(lax.rem(j0, t), SC_CHUNK)], rows_v)
            pltpu.sync_copy(rows_v, out_hbm.at[idx_v])

    return run(src, pos)


def _sc_gather_rows(src, pos):
    w = src.shape[1]
    n = pos.shape[0]
    per_w = _sc_worker_rows(n)

    @functools.partial(
        pl.kernel, mesh=_sc_mesh(), out_type=jax.ShapeDtypeStruct((n, w), src.dtype),
        scratch_types=[pltpu.VMEM((SC_CHUNK,), jnp.int32), pltpu.VMEM((SC_CHUNK, w), src.dtype)],
        name="moe_return")
    def run(src_hbm, pos_hbm, out_hbm, idx_v, rows_v):
        wid = lax.axis_index("s") * SC_CORES + lax.axis_index("c")

        @pl.loop(0, per_w // SC_CHUNK)
        def _(i):
            j0 = wid * per_w + i * SC_CHUNK
            pltpu.sync_copy(pos_hbm.at[pl.ds(j0, SC_CHUNK)], idx_v)
            pltpu.sync_copy(src_hbm.at[idx_v], rows_v)
            pltpu.sync_copy(rows_v, out_hbm.at[pl.ds(j0, SC_CHUNK)])

    return run(src, pos)


def _experts_kernel(te_ref, nu_ref, x_ref, w1_ref, w3_ref, w2_ref, y_ref):
    half = D_MODEL // 2

    @pl.when(pl.program_id(0) < nu_ref[0])
    def _():
        lo, hi = _unpack_bf16_pairs(x_ref[...])
        a = _bdot(lo, w1_ref[:half, :].astype(BF16)) + _bdot(hi, w1_ref[half:, :].astype(BF16))
        u = _bdot(lo, w3_ref[:half, :].astype(BF16)) + _bdot(hi, w3_ref[half:, :].astype(BF16))
        mid = (a * jax.nn.sigmoid(a)) * u
        y_ref[...] = _pack_bf16_pairs(_bdot(mid.astype(BF16), w2_ref[...].astype(BF16)))


def _experts(xs, tile_expert, n_used, n_tiles, w1, w3, w2, tm):
    wspec = lambda a: pl.BlockSpec((None,) + a.shape[1:], lambda i, te, nu: (te[i], 0, 0))
    grid_spec = pltpu.PrefetchScalarGridSpec(
        num_scalar_prefetch=2,
        grid=(n_tiles,),
        in_specs=[pl.BlockSpec((tm, PACK_W), lambda i, te, nu: (jnp.minimum(i, nu[0] - 1), 0)),
                  wspec(w1), wspec(w3), wspec(w2)],
        out_specs=pl.BlockSpec((tm, PACK_W), lambda i, te, nu: (jnp.where(i < nu[0], i, n_tiles), 0)),
    )
    return pl.pallas_call(
        _experts_kernel,
        grid_spec=grid_spec,
        out_shape=jax.ShapeDtypeStruct((xs.shape[0] + tm, PACK_W), jnp.int32),
        compiler_params=_cparams(("arbitrary",)),
        name="experts",
    )(tile_expert, n_used, xs, w1, w3, w2)


def _combine_kernel(x1_ref, mod_ref, rt_ref, g0_ref, g1_ref, o_ref):
    rt = rt_ref[...]
    lo0, hi0 = _unpack_bf16_pairs(g0_ref[...])
    lo1, hi1 = _unpack_bf16_pairs(g1_ref[...])
    w0 = rt[:, RT_W0:RT_W0 + 1]
    w1 = rt[:, RT_W1:RT_W1 + 1]
    half = D_MODEL // 2
    gt2 = mod_ref[:, 5 * D_MODEL:6 * D_MODEL]
    o_ref[:, :half] = x1_ref[:, :half] + gt2[:, :half] * (w0 * lo0.astype(F32) + w1 * lo1.astype(F32))
    o_ref[:, half:] = x1_ref[:, half:] + gt2[:, half:] * (w0 * hi0.astype(F32) + w1 * hi1.astype(F32))


def _combine(x1, mod, rt, g, tm):
    t = x1.shape[0]
    per_mod = t // (mod.shape[0] * tm)
    return pl.pallas_call(
        _combine_kernel,
        grid=(t // tm,),
        in_specs=[
            pl.BlockSpec((tm, D_MODEL), lambda i: (i, 0)),
            pl.BlockSpec((None, 1, 6 * D_MODEL), lambda i: (i // per_mod, 0, 0)),
            pl.BlockSpec((tm, LANE), lambda i: (i, 0)),
            pl.BlockSpec((tm, PACK_W), lambda i: (i, 0)),
            pl.BlockSpec((tm, PACK_W), lambda i: (i + t // tm, 0)),
        ],
        out_specs=pl.BlockSpec((tm, D_MODEL), lambda i: (i, 0)),
        out_shape=jax.ShapeDtypeStruct((t, D_MODEL), F32),
        compiler_params=_cparams(("arbitrary",)),
        name="combine",
    )(x1, mod, rt, g, g)


def _moe(h2p, rt, x1, mod, p, tm_e, tm_c):
    pos, tile_expert, n_used, n_tiles = _dispatch_plan(rt, tm_e)
    xs = _sc_scatter_rows(h2p, pos, n_tiles * tm_e)
    ys = _experts(xs, tile_expert, n_used, n_tiles, p["w1"], p["w3"], p["w2"], tm_e)
    g = _sc_gather_rows(ys, pos)
    rt_cols = jnp.pad(rt.T, ((0, 0), (0, LANE - rt.shape[0])))
    return _combine(x1, mod, rt_cols, g, tm_c)


def _channel_dft():
    n = np.arange(FNET_W // FNET_GROUPS)
    ang = 2.0 * np.pi * ((n[:, None] * n[None, :]) % n.size) / n.size
    eye = np.eye(FNET_GROUPS)
    scale = 1.0 / math.sqrt(n.size)
    return (jnp.asarray(np.kron(eye, np.cos(ang) * scale), F32).astype(BF16),
            jnp.asarray(np.kron(eye, np.sin(ang) * scale), F32).astype(BF16))


def _position_dft(l):
    scale = 1.0 / math.sqrt(l)
    if l <= 256:
        n = np.arange(l)
        ang = 2.0 * np.pi * ((n[:, None] * n[None, :]) % l) / l
        return jnp.asarray(np.cos(ang) * scale, F32).astype(BF16), jnp.asarray(np.sin(ang) * scale, F32).astype(BF16)
    m = DFT_SPLIT
    n = np.arange(l)
    ang_a = 2.0 * np.pi * ((np.arange(l // m)[:, None] * m * n[None, :]) % l) / l
    ang_b = 2.0 * np.pi * ((np.arange(m)[:, None] * n[None, :]) % l) / l
    return (jnp.asarray(np.cos(ang_a), F32), jnp.asarray(np.sin(ang_a), F32),
            jnp.asarray(np.cos(ang_b) * scale, F32), jnp.asarray(np.sin(ang_b) * scale, F32))


def _rope_tables(l, rotate):
    cos = np.ones((l, LANE))
    sina = np.zeros((l, LANE))
    sinb = np.zeros((l, LANE))
    if rotate:
        t = np.arange(l)
        inv = ROPE_BASE ** (-np.arange(N_FREQ) / N_FREQ)
        ang = np.concatenate([(t // GRID_W)[:, None] * inv, (t % GRID_W)[:, None] * inv], axis=-1)
        half = ROPE // 2
        cos[:, ROPE_OFF:ROPE_OFF + half] = np.cos(ang)
        cos[:, ROPE_OFF + half:ROPE_OFF + ROPE] = np.cos(ang)
        sina[:, ROPE_OFF:ROPE_OFF + half] = -np.sin(ang)
        sinb[:, ROPE_OFF + half:ROPE_OFF + ROPE] = np.sin(ang)
    return tuple(jnp.asarray(a, F32) for a in (cos, sina, sinb))


def _pad_heads(w, lo, hi):
    r = w.shape[0]
    part = w[:, :, lo:hi]
    out = jnp.zeros((r, MLA_HEADS, HEAD_PAD), w.dtype).at[:, :, : hi - lo].set(part)
    return out.reshape(r, HP)


def _blockdiag_halves(w):
    bw = LRU_W // LRU_BLOCKS
    out = jnp.zeros((2, 2, LANE, LANE), w.dtype)
    for half in range(2):
        for k in range(2):
            n = 2 * half + k
            out = out.at[half, :, k * bw:(k + 1) * bw, k * bw:(k + 1) * bw].set(w[:, n])
    return out


def _halves(v):
    return jnp.moveaxis(v.reshape(v.shape[:-1] + (2, LANE)), -2, 0)


def _layer_params(l, a):
    d = D_MODEL
    w_in = a["w_in"][l]
    w_in = jnp.concatenate(
        [w_in[:, :Z_KR], jnp.zeros((d, ROPE_OFF), F32), w_in[:, Z_KR:], jnp.zeros((d, LANE - ROPE_OFF - ROPE), F32)],
        axis=1)
    w_r = jnp.zeros((RT_ROWS, d), F32).at[:N_GROUPS].set(a["w_gr"][l].T).at[SUB:SUB + N_EXPERTS].set(a["w_er"][l].T)
    b_r = jnp.zeros((RT_ROWS, 1), F32).at[:N_GROUPS, 0].set(a["b_gr"][l]).at[SUB:SUB + N_EXPERTS, 0].set(a["b_er"][l])
    wr_hi = w_r.astype(BF16)
    gpad = lambda g: jnp.zeros((1, LANE), F32).at[0, :QK_DIM].set(g)
    w_pm = jnp.zeros((MLA_HEADS, HEAD_PAD, d), F32).at[:, :V_DIM].set(a["w_pm"][l].reshape(MLA_HEADS, V_DIM, d))
    return {
        "norm1_g": a["norm1_g"][l].reshape(1, d), "norm2_g": a["norm2_g"][l].reshape(1, d),
        "w_in": w_in.astype(BF16),
        "lru_conv": _halves(a["lru_conv"][l]),
        "lru_wa": _blockdiag_halves(a["lru_wa"][l]).astype(BF16),
        "lru_wx": _blockdiag_halves(a["lru_wx"][l]).astype(BF16),
        "lru_ba": _halves(a["lru_ba"][l])[:, :, None, :], "lru_bx": _halves(a["lru_bx"][l])[:, :, None, :],
        "lru_lam": _halves(a["lru_lam"][l])[:, :, None, :],
        "sc_conv": _halves(a["sc_conv"][l]),
        "g_qa": a["g_qa"][l].reshape(1, Q_LORA), "g_kva": a["g_kva"][l].reshape(1, KV_LORA),
        "wq": _pad_heads(a["w_qb"][l], 0, QK_DIM).astype(BF16),
        "wk": _pad_heads(a["w_kvb"][l], 0, NOPE).astype(BF16),
        "wv": _pad_heads(a["w_kvb"][l], NOPE, NOPE + V_DIM).astype(BF16),
        "gq": gpad(a["g_qn"][l]), "gk": gpad(a["g_kn"][l]),
        "w_gate": a["w_gate"][l].astype(BF16),
        "w_pf": a["w_pf"][l].astype(BF16), "w_pl": a["w_pl"][l].astype(BF16), "w_ps": a["w_ps"][l].astype(BF16),
        "w_pm": w_pm.reshape(HP, d).astype(BF16), "w_out": a["w_out"][l].astype(BF16),
        "wr_hi": wr_hi, "wr_lo": (w_r - wr_hi.astype(F32)).astype(BF16), "b_r": b_r,
        "w1": a["w1"][l], "w3": a["w3"][l], "w2": a["w2"][l],
    }


def _trunk_layer(x, mod, p, consts, h0, cache, cfg):
    s, l = cfg["s"], cfg["l"]
    z = _front(x, mod, p["norm1_g"], p["w_in"], cfg["tm_front"])
    z3 = z.reshape(s, l, Z_W)
    q, k, v, ckv = _qkv(z, consts["rope"], p, l, cfg["tm_qkv"])
    yf = _fourier(z3, consts["dft"], cfg["nb"], cfg["tq_f"])
    yl, fin = _lru(z3, h0, p)
    ys = _sconv(z3, p["sc_conv"])
    o = _attn(q.reshape(s, l, HP), k.reshape(s, l, HP), v.reshape(s, l, HP), cache, cfg["hp"], cfg["tq_a"])
    t = s * l
    x1, h2, comb = _back(x, mod, yf.reshape(t, FNET_W), yl.reshape(t, LRU_W), ys.reshape(t, SC_W),
                         o.reshape(t, HP), p, cfg["tm_back"])
    x2 = _moe(h2, comb, x1, mod, p, cfg["tm_moe"], cfg["tm_comb"])
    return x2, z3, ckv, fin


def kernel(x_prompt, x_sample, cache_ckv, cache_krope, state_rglru, c, c_ctx, norm1_g, norm2_g, w_ada, b_ada, w_in, lru_conv, lru_wa, lru_ba, lru_wx, lru_bx, lru_lam, sc_conv, g_qa, w_qb, g_kva, w_kvb, g_qn, g_kn, w_pf, w_pl, w_ps, w_pm, w_gate, w_out, w_gr, b_gr, w_er, b_er, w1, w3, w2):
    a = dict(norm1_g=norm1_g, norm2_g=norm2_g, w_in=w_in, lru_conv=lru_conv, lru_wa=lru_wa, lru_ba=lru_ba,
             lru_wx=lru_wx, lru_bx=lru_bx, lru_lam=lru_lam, sc_conv=sc_conv, g_qa=g_qa, w_qb=w_qb, g_kva=g_kva,
             w_kvb=w_kvb, g_qn=g_qn, g_kn=g_kn, w_pf=w_pf, w_pl=w_pl, w_ps=w_ps, w_pm=w_pm, w_gate=w_gate,
             w_out=w_out, w_gr=w_gr, b_gr=b_gr, w_er=w_er, b_er=b_er, w1=w1, w3=w3, w2=w2)
    bc, lc, d = x_prompt.shape
    bl, ll, _ = x_sample.shape
    past = cache_ckv.shape[2]

    cond8 = jnp.zeros((SUB, d), F32).at[0].set(c_ctx).at[1:1 + bl].set(c)
    mod_all = _ada(cond8, w_ada, b_ada)

    cc, sc = _channel_dft()
    ctx_consts = {"dft": (cc, sc) + _position_dft(lc), "rope": _rope_tables(lc, False)}
    lat_consts = {"dft": (cc, sc) + _position_dft(ll), "rope": _rope_tables(ll, True)}
    cache_tabs = _rope_tables(bl * past, False)
    ctx_cfg = dict(s=bc, l=lc, tm_front=512, tm_qkv=512, nb=8, tq_f=lc, hp=MLA_HEADS, tq_a=lc, tm_back=512, tm_moe=512, tm_comb=512)
    lat_cfg = dict(s=bl, l=ll, tm_front=512, tm_qkv=512, nb=bl, tq_f=512, hp=MLA_HEADS, tq_a=256, tm_back=512, tm_moe=512, tm_comb=512)

    xp = x_prompt.reshape(bc * lc, d)
    xs = x_sample.reshape(bl * ll, d)
    h0_ctx = jnp.zeros((bc, 2, LRU_W), F32)
    ckv_list, kr_list, lru_list = [], [], []
    for l in range(DEPTH):
        p = _layer_params(l, a)
        mod_ctx = mod_all[l, 0:1].reshape(1, 1, 6 * d)
        mod_lat = mod_all[l, 1:1 + bl].reshape(bl, 1, 6 * d)
        xp, z3, ckv, fin = _trunk_layer(xp, mod_ctx, p, ctx_consts, h0_ctx, None, ctx_cfg)
        ckv_list.append(ckv.reshape(bc, lc, KV_LORA))
        kr_list.append(z3[:, :, Z_KR + ROPE_OFF:Z_KR + ROPE_OFF + ROPE])
        lru_list.append(fin)
        kr_tile = jnp.pad(cache_krope[:, l].reshape(bl * past, ROPE), ((0, 0), (ROPE_OFF, LANE - ROPE_OFF - ROPE)))
        kc, vc = _kvcache(cache_ckv[:, l].reshape(bl * past, KV_LORA), kr_tile, cache_tabs, p)
        cache = (kc.reshape(bl, past, HP), vc.reshape(bl, past, HP))
        xs, _, _, _ = _trunk_layer(xs, mod_lat, p, lat_consts, state_rglru[:, l], cache, lat_cfg)
    return (xp.reshape(bc, lc, d), xs.reshape(bl, ll, d),
            jnp.stack(ckv_list, axis=1), jnp.stack(kr_list, axis=1), jnp.stack(lru_list, axis=1))
```

```python
import functools
import math

import numpy as np
import jax
import jax.numpy as jnp
from jax import lax
from jax.experimental import pallas as pl
from jax.experimental.pallas import tpu as pltpu
from jax.experimental.pallas import tpu_sc as plsc

F32 = jnp.float32
BF16 = jnp.bfloat16

D_MODEL = 1024
DEPTH = 2
GRID_W = 64
EPS = 1e-6
FNET_W = 256
FNET_GROUPS = 4
LRU_W = 256
LRU_BLOCKS = 4
LRU_C = 8.0
SC_W = 256
MLA_HEADS = 8
Q_LORA = 384
KV_LORA = 256
NOPE = 64
ROPE = 32
V_DIM = 64
QK_DIM = NOPE + ROPE
N_FREQ = ROPE // 4
ROPE_BASE = 10000.0
ATTN_SCALE = QK_DIM ** -0.5
LOG2E = math.log2(math.e)
N_GROUPS = 4
EXP_PER_GROUP = 8
N_EXPERTS = N_GROUPS * EXP_PER_GROUP
D_EXPERT = 256

LANE = 128
SUB = 8
HEAD_PAD = LANE
HP = MLA_HEADS * HEAD_PAD
ROPE_X1 = 0
ROPE_X2 = LANE // 2
DEN_LANE = V_DIM
Z_QKV = 6 * 256
Z_KR = Z_QKV + Q_LORA + KV_LORA
Z_W = Z_KR + LANE
QKV_W = Q_LORA + KV_LORA + LANE
VMEM_LIMIT = 52 * 1024 * 1024
PACK_W = D_MODEL // 2
HI_MASK = -65536
RT_ID0, RT_ID1, RT_W0, RT_W1 = 0, 1, 2, 3
RT_ROWS = 48
TOP_K = 2
DFT_SPLIT = 64
ATTN_SUB = 256
SC_CORES = 2
SC_SUBCORES = 16
SC_CHUNK = 128


def _cparams(sem):
    return pltpu.CompilerParams(dimension_semantics=sem, vmem_limit_bytes=VMEM_LIMIT)


def _bdot(a, b):
    return jnp.dot(a, b, preferred_element_type=F32)


def _rms(x, g):
    return x * lax.rsqrt(jnp.mean(x * x, axis=-1, keepdims=True) + EPS) * g


def _ada_kernel(c_ref, w_ref, b_ref, o_ref):
    c = c_ref[...]
    s = (c * jax.nn.sigmoid(c)).astype(BF16)
    o_ref[...] = _bdot(s, w_ref[...].astype(BF16)) + b_ref[...]


def _ada(cond8, w_ada, b_ada):
    nblk = 6 * D_MODEL // 1024
    return pl.pallas_call(
        _ada_kernel,
        grid=(DEPTH, nblk),
        in_specs=[
            pl.BlockSpec((SUB, D_MODEL), lambda l, n: (0, 0)),
            pl.BlockSpec((None, D_MODEL, 1024), lambda l, n: (l, 0, n)),
            pl.BlockSpec((None, 1, 1024), lambda l, n: (l, 0, n)),
        ],
        out_specs=pl.BlockSpec((None, SUB, 1024), lambda l, n: (l, 0, n)),
        out_shape=jax.ShapeDtypeStruct((DEPTH, SUB, 6 * D_MODEL), F32),
        compiler_params=_cparams(("arbitrary", "arbitrary")),
        name="ada",
    )(cond8, w_ada, b_ada.reshape(DEPTH, 1, 6 * D_MODEL))


def _front_kernel(x_ref, mod_ref, g_ref, w_ref, z_ref):
    x = x_ref[...]
    sh = mod_ref[:, 0:D_MODEL]
    sc = mod_ref[:, D_MODEL:2 * D_MODEL]
    h = _rms(x, g_ref[...]) * (1.0 + sc) + sh
    z_ref[...] = _bdot(h.astype(BF16), w_ref[...])


def _front(x, mod, g1, w_in, tm):
    t = x.shape[0]
    per_mod = t // (mod.shape[0] * tm)
    return pl.pallas_call(
        _front_kernel,
        grid=(t // tm,),
        in_specs=[
            pl.BlockSpec((tm, D_MODEL), lambda i: (i, 0)),
            pl.BlockSpec((None, 1, 6 * D_MODEL), lambda i: (i // per_mod, 0, 0)),
            pl.BlockSpec((1, D_MODEL), lambda i: (0, 0)),
            pl.BlockSpec((D_MODEL, Z_W), lambda i: (0, 0)),
        ],
        out_specs=pl.BlockSpec((tm, Z_W), lambda i: (i, 0)),
        out_shape=jax.ShapeDtypeStruct((t, Z_W), F32),
        compiler_params=_cparams(("arbitrary",)),
        name="front",
    )(x, mod, g1, w_in)


def _rope(x, cos, sinr):
    return x * cos + pltpu.roll(x, LANE // 2, 1) * sinr


def _head_norm(x, g):
    ss = jnp.sum(x * x, axis=-1, keepdims=True) * (1.0 / QK_DIM)
    return x * lax.rsqrt(ss + EPS) * g


def _build_kv(ckv, kr_tile, cos, sinr, wk_ref, wv_ref, gk, k_ref, v_ref):
    cb = ckv.astype(BF16)
    kn = _bdot(cb, wk_ref[...])
    for h in range(MLA_HEADS):
        sl = slice(h * HEAD_PAD, (h + 1) * HEAD_PAD)
        k_ref[:, sl] = _rope(_head_norm(kn[:, sl] + kr_tile, gk), cos, sinr).astype(BF16)
    v = _bdot(cb, wv_ref[...])
    lane = lax.broadcasted_iota(jnp.int32, v.shape, 1)
    v_ref[...] = jnp.where((lane & (HEAD_PAD - 1)) == DEN_LANE, 1.0, v).astype(BF16)


def _qkv_kernel(z_ref, cos_ref, sinr_ref, gqa_ref, wq_ref, gkva_ref, wk_ref, wv_ref,
                gq_ref, gk_ref, q_ref, k_ref, v_ref, ckv_ref):
    cos, sinr = cos_ref[...], sinr_ref[...]
    q_c = z_ref[:, 0:Q_LORA]
    kv_c = z_ref[:, Q_LORA:Q_LORA + KV_LORA]
    kr_tile = z_ref[:, Q_LORA + KV_LORA:QKV_W]
    qf = _bdot(_rms(q_c, gqa_ref[...]).astype(BF16), wq_ref[...])
    gq = gq_ref[...] * (ATTN_SCALE * LOG2E)
    for h in range(MLA_HEADS):
        sl = slice(h * HEAD_PAD, (h + 1) * HEAD_PAD)
        q_ref[:, sl] = _rope(_head_norm(qf[:, sl], gq), cos, sinr).astype(BF16)
    ckv = _rms(kv_c, gkva_ref[...])
    ckv_ref[...] = ckv
    _build_kv(ckv, kr_tile, cos, sinr, wk_ref, wv_ref, gk_ref[...], k_ref, v_ref)


def _qkv(z, tabs, p, seq_len, tm):
    t = z.shape[0]
    if tm > seq_len:
        tabs = [jnp.tile(a, (tm // seq_len, 1)) for a in tabs]
    per_seq = max(seq_len // tm, 1)
    tab_spec = pl.BlockSpec((tm, LANE), lambda i: (i % per_seq, 0))
    full = lambda shape: pl.BlockSpec(shape, lambda i: (0,) * len(shape))
    return pl.pallas_call(
        _qkv_kernel,
        grid=(t // tm,),
        in_specs=[
            pl.BlockSpec((tm, QKV_W), lambda i: (i, Z_QKV // QKV_W)),
            tab_spec, tab_spec,
            full((1, Q_LORA)), full((Q_LORA, HP)), full((1, KV_LORA)),
            full((KV_LORA, HP)), full((KV_LORA, HP)), full((1, LANE)), full((1, LANE)),
        ],
        out_specs=[
            pl.BlockSpec((tm, HP), lambda i: (i, 0)),
            pl.BlockSpec((tm, HP), lambda i: (i, 0)),
            pl.BlockSpec((tm, HP), lambda i: (i, 0)),
            pl.BlockSpec((tm, KV_LORA), lambda i: (i, 0)),
        ],
        out_shape=[
            jax.ShapeDtypeStruct((t, HP), BF16),
            jax.ShapeDtypeStruct((t, HP), BF16),
            jax.ShapeDtypeStruct((t, HP), BF16),
            jax.ShapeDtypeStruct((t, KV_LORA), F32),
        ],
        compiler_params=_cparams(("arbitrary",)),
        name="qkv",
    )(z, tabs[0], tabs[1], p["g_qa"], p["wq"], p["g_kva"], p["wk"], p["wv"], p["gq"], p["gk"])


def _kvcache_kernel(ckv_ref, kr_ref, cos_ref, sinr_ref, wk_ref, wv_ref, gk_ref, k_ref, v_ref):
    _build_kv(ckv_ref[...], kr_ref[...], cos_ref[...], sinr_ref[...], wk_ref, wv_ref, gk_ref[...], k_ref, v_ref)


def _kvcache(ckv, kr_tile, tabs, p):
    t = ckv.shape[0]
    return pl.pallas_call(
        _kvcache_kernel,
        out_shape=[jax.ShapeDtypeStruct((t, HP), BF16), jax.ShapeDtypeStruct((t, HP), BF16)],
        compiler_params=_cparams(None),
        name="kvcache",
    )(ckv, kr_tile, tabs[0], tabs[1], p["wk"], p["wv"], p["gk"])


def _fourier_kernel(nb, u_ref, cc_ref, sc_ref, cl_ref, sl_ref, o_ref, a_scr, b_scr):
    @pl.when(pl.program_id(1) == 0)
    def _():
        for b in range(nb):
            u = u_ref[b].astype(BF16)
            a_scr[:, b * FNET_W:(b + 1) * FNET_W] = _bdot(u, cc_ref[...]).astype(BF16)
            b_scr[:, b * FNET_W:(b + 1) * FNET_W] = _bdot(u, sc_ref[...]).astype(BF16)

    y = _bdot(cl_ref[...], a_scr[...]) - _bdot(sl_ref[...], b_scr[...])
    for b in range(nb):
        o_ref[b] = y[:, b * FNET_W:(b + 1) * FNET_W].astype(BF16)


def _fourier_split_kernel(nb, u_ref, cc_ref, sc_ref, ca_ref, sa_ref, cb_ref, sb_ref, o_ref,
                          a_scr, b_scr, cl_scr, sl_scr):
    @pl.when(pl.program_id(1) == 0)
    def _():
        for b in range(nb):
            u = u_ref[b].astype(BF16)
            a_scr[:, b * FNET_W:(b + 1) * FNET_W] = _bdot(u, cc_ref[...]).astype(BF16)
            b_scr[:, b * FNET_W:(b + 1) * FNET_W] = _bdot(u, sc_ref[...]).astype(BF16)

    cb, sb = cb_ref[...], sb_ref[...]
    for j in range(ca_ref.shape[0]):
        ca, sa = ca_ref[j:j + 1, :], sa_ref[j:j + 1, :]
        cl_scr[j * DFT_SPLIT:(j + 1) * DFT_SPLIT, :] = (ca * cb - sa * sb).astype(BF16)
        sl_scr[j * DFT_SPLIT:(j + 1) * DFT_SPLIT, :] = (sa * cb + ca * sb).astype(BF16)
    y = _bdot(cl_scr[...], a_scr[...]) - _bdot(sl_scr[...], b_scr[...])
    for b in range(nb):
        o_ref[b] = y[:, b * FNET_W:(b + 1) * FNET_W].astype(BF16)


def _fourier_split(z3, mats, nb, tq):
    s, l, _ = z3.shape
    cc, sc, ca, sa, cb, sb = mats
    full = lambda a: pl.BlockSpec(a.shape, lambda i, r: (0, 0))
    rows = pl.BlockSpec((tq // DFT_SPLIT, l), lambda i, r: (r, 0))
    return pl.pallas_call(
        functools.partial(_fourier_split_kernel, nb),
        grid=(s // nb, l // tq),
        in_specs=[pl.BlockSpec((nb, l, FNET_W), lambda i, r: (i, 0, 0)), full(cc), full(sc), rows, rows,
                  full(cb), full(sb)],
        out_specs=pl.BlockSpec((nb, tq, FNET_W), lambda i, r: (i, r, 0)),
        out_shape=jax.ShapeDtypeStruct((s, l, FNET_W), BF16),
        scratch_shapes=[pltpu.VMEM((l, nb * FNET_W), BF16), pltpu.VMEM((l, nb * FNET_W), BF16),
                        pltpu.VMEM((tq, l), BF16), pltpu.VMEM((tq, l), BF16)],
        compiler_params=_cparams(("arbitrary", "arbitrary")),
        name="fourier_split",
    )(z3, cc, sc, ca, sa, cb, sb)


def _fourier(z3, mats, nb, tq):
    s, l, _ = z3.shape
    if len(mats) == 6:
        return _fourier_split(z3, mats, nb, tq)
    cc, sc, cl, sl = mats
    return pl.pallas_call(
        functools.partial(_fourier_kernel, nb),
        grid=(s // nb, l // tq),
        in_specs=[
            pl.BlockSpec((nb, l, FNET_W), lambda i, r: (i, 0, 0)),
            pl.BlockSpec((FNET_W, FNET_W), lambda i, r: (0, 0)),
            pl.BlockSpec((FNET_W, FNET_W), lambda i, r: (0, 0)),
            pl.BlockSpec((tq, l), lambda i, r: (r, 0)),
            pl.BlockSpec((tq, l), lambda i, r: (r, 0)),
        ],
        out_specs=pl.BlockSpec((nb, tq, FNET_W), lambda i, r: (i, r, 0)),
        out_shape=jax.ShapeDtypeStruct((s, l, FNET_W), BF16),
        scratch_shapes=[pltpu.VMEM((l, nb * FNET_W), BF16), pltpu.VMEM((l, nb * FNET_W), BF16)],
        compiler_params=_cparams(("arbitrary", "arbitrary")),
        name="fourier",
    )(z3, cc, sc, cl, sl)


def _load_ext(ref, c, rows, nchunks):
    t0 = pl.multiple_of(c * rows, rows)
    main = ref[pl.ds(t0, rows), :]
    lo = pl.multiple_of(jnp.maximum(t0 - SUB, 0), SUB)
    hi = pl.multiple_of(jnp.minimum(t0 + rows, (nchunks - 1) * rows + rows - SUB), SUB)
    prev = jnp.where(c > 0, ref[pl.ds(lo, SUB), :], 0.0)
    nxt = jnp.where(c < nchunks - 1, ref[pl.ds(hi, SUB), :], 0.0)
    return jnp.concatenate([prev, main, nxt], axis=0)


def _shifted(ext, off, rows):
    n = ext.shape[0]
    r = ext if off == 0 else pltpu.roll(ext, (-off) % n, 0)
    return r[SUB:SUB + rows]


def _gelu_tanh(x):
    return 0.5 * x * (1.0 + jnp.tanh(math.sqrt(2.0 / math.pi) * (x + 0.044715 * (x * x * x))))


def _lru_gates(xc, d, wa_ref, wx_ref, ba_ref, bx_ref, lam_ref):
    xb = xc.astype(BF16)
    r = jax.nn.sigmoid(_bdot(xb, wa_ref[d]) + ba_ref[d])
    i = jax.nn.sigmoid(_bdot(xb, wx_ref[d]) + bx_ref[d])
    nl = -lam_ref[d]
    softplus = jnp.maximum(nl, 0.0) + jnp.log1p(jnp.exp(-jnp.abs(nl)))
    la = (-LRU_C) * r * softplus
    a = jnp.exp(la)
    one_m_a2 = -jnp.tanh(la) * (a * a + 1.0)
    b = jnp.sqrt(one_m_a2) * (i * xc)
    return a, b


def _group_scan(a, b, reverse):
    rows = a.shape[0]
    rm = lax.broadcasted_iota(jnp.int32, a.shape, 0) & (SUB - 1)
    for s in (1, 2, 4):
        if reverse:
            sh, m = rows - s, rm + s <= SUB - 1
        else:
            sh, m = s, rm >= s
        a_sh = pltpu.roll(a, sh, 0)
        b_sh = pltpu.roll(b, sh, 0)
        b = jnp.where(m, a * b_sh + b, b)
        a = jnp.where(m, a * a_sh, a)
    return a, b


def _carry_scan(a, b, carry, reverse):
    ng = a.shape[0] // SUB
    out = [None] * ng
    order = range(ng - 1, -1, -1) if reverse else range(ng)
    for g in order:
        hg = a[g * SUB:(g + 1) * SUB] * carry + b[g * SUB:(g + 1) * SUB]
        out[g] = hg
        carry = hg[0:1] if reverse else hg[SUB - 1:SUB]
    return jnp.concatenate(out, axis=0), carry


def _lru_kernel(rows, nchunks, xl_ref, gl_ref, h0_ref, cw_ref, wa_ref, wx_ref, ba_ref, bx_ref, lam_ref,
                y_ref, fin_ref, xc_scr, hf_scr):
    cw = cw_ref[...]

    def conv_body(c, carry):
        ext = _load_ext(xl_ref, c, rows, nchunks)
        xc = _shifted(ext, -2, rows) * cw[0:1]
        for k in range(1, 4):
            xc = xc + _shifted(ext, k - 2, rows) * cw[k:k + 1]
        xc_scr[pl.ds(pl.multiple_of(c * rows, rows), rows), :] = xc
        return carry

    lax.fori_loop(0, nchunks, conv_body, 0)

    def fw_body(c, carry):
        t0 = pl.multiple_of(c * rows, rows)
        a, b = _lru_gates(xc_scr[pl.ds(t0, rows), :], 0, wa_ref, wx_ref, ba_ref, bx_ref, lam_ref)
        a, b = _group_scan(a, b, False)
        h, carry = _carry_scan(a, b, carry, False)
        hf_scr[pl.ds(t0, rows), :] = h
        return carry

    cf = lax.fori_loop(0, nchunks, fw_body, h0_ref[0:1, :])

    def bw_body(j, carry):
        t0 = pl.multiple_of((nchunks - 1 - j) * rows, rows)
        a, b = _lru_gates(xc_scr[pl.ds(t0, rows), :], 1, wa_ref, wx_ref, ba_ref, bx_ref, lam_ref)
        a, b = _group_scan(a, b, True)
        h, carry = _carry_scan(a, b, carry, True)
        y = (hf_scr[pl.ds(t0, rows), :] + h) * _gelu_tanh(gl_ref[pl.ds(t0, rows), :])
        y_ref[pl.ds(t0, rows), :] = y.astype(BF16)
        return carry

    cb = lax.fori_loop(0, nchunks, bw_body, h0_ref[1:2, :])
    fin_ref[0:1, :] = cf
    fin_ref[1:2, :] = cb


def _lru(z3, h0, p):
    s, l, _ = z3.shape
    rows = min(l, 256)
    nchunks = l // rows
    half = lambda blk: pl.BlockSpec((None, l, LANE), lambda i, j: (i, 0, blk + j))
    wspec = lambda shape: pl.BlockSpec((None,) + shape, lambda i, j: (j,) + (0,) * len(shape))
    return pl.pallas_call(
        functools.partial(_lru_kernel, rows, nchunks),
        grid=(s, 2),
        in_specs=[
            half(2), half(4),
            pl.BlockSpec((None, 2, LANE), lambda i, j: (i, 0, j)),
            wspec((4, LANE)), wspec((2, LANE, LANE)), wspec((2, LANE, LANE)),
            wspec((2, 1, LANE)), wspec((2, 1, LANE)), wspec((2, 1, LANE)),
        ],
        out_specs=[
            pl.BlockSpec((None, l, LANE), lambda i, j: (i, 0, j)),
            pl.BlockSpec((None, 2, LANE), lambda i, j: (i, 0, j)),
        ],
        out_shape=[jax.ShapeDtypeStruct((s, l, LRU_W), BF16), jax.ShapeDtypeStruct((s, 2, LRU_W), F32)],
        scratch_shapes=[pltpu.VMEM((l, LANE), F32), pltpu.VMEM((l, LANE), F32)],
        compiler_params=_cparams(("arbitrary", "arbitrary")),
        name="lru",
    )(z3, z3, h0, p["lru_conv"], p["lru_wa"], p["lru_wx"], p["lru_ba"], p["lru_bx"], p["lru_lam"])


def _sconv_kernel(rows, nchunks, b_ref, c_ref, x_ref, w_ref, y_ref):
    w = w_ref[...]

    def body(c, carry):
        ext = _load_ext(c_ref, c, rows, nchunks) * _load_ext(x_ref, c, rows, nchunks)
        acc = _shifted(ext, -1, rows) * w[0:1]
        acc = acc + _shifted(ext, 0, rows) * w[1:2]
        acc = acc + _shifted(ext, 1, rows) * w[2:3]
        t0 = pl.multiple_of(c * rows, rows)
        y_ref[pl.ds(t0, rows), :] = (b_ref[pl.ds(t0, rows), :] * acc).astype(BF16)
        return carry

    lax.fori_loop(0, nchunks, body, 0)


def _sconv(z3, w):
    s, l, _ = z3.shape
    rows = min(l, 256)
    half = lambda blk: pl.BlockSpec((None, l, LANE), lambda i, j: (i, 0, blk + j))
    return pl.pallas_call(
        functools.partial(_sconv_kernel, rows, l // rows),
        grid=(s, 2),
        in_specs=[half(6), half(8), half(10), pl.BlockSpec((None, 3, LANE), lambda i, j: (j, 0, 0))],
        out_specs=pl.BlockSpec((None, l, LANE), lambda i, j: (i, 0, j)),
        out_shape=jax.ShapeDtypeStruct((s, l, SC_W), BF16),
        compiler_params=_cparams(("arbitrary", "arbitrary")),
        name="sconv",
    )(z3, z3, z3, w)


def _attn_kernel(hp, sub, has_cache, *refs):
    if has_cache:
        q_ref, k_ref, v_ref, kc_ref, vc_ref, o_ref, s_scr, p_scr, m_scr, sc_scr, pc_scr = refs
    else:
        q_ref, k_ref, v_ref, o_ref, s_scr, p_scr, m_scr = refs
    nt = (((1,), (1,)), ((), ()))
    items = [(slice(r * sub, (r + 1) * sub), slice(h * HEAD_PAD, (h + 1) * HEAD_PAD))
             for r in range(q_ref.shape[0] // sub) for h in range(hp)]
    n = len(items)

    def scores(i):
        rows, head = items[i]
        q = q_ref[rows, head]
        s = lax.dot_general(q, k_ref[:, head], nt, preferred_element_type=F32)
        m = jnp.max(s, axis=-1, keepdims=True)
        s_scr[i % 2] = s
        if has_cache:
            sc = lax.dot_general(q, kc_ref[:, head], nt, preferred_element_type=F32)
            m = jnp.maximum(m, jnp.max(sc, axis=-1, keepdims=True))
            sc_scr[i % 2] = sc
        m_scr[i % 2] = jnp.broadcast_to(m, m_scr.shape[1:])

    def probs(i):
        m = m_scr[i % 2][:, 0:1]
        p_scr[i % 2] = jnp.exp2(s_scr[i % 2] - m).astype(BF16)
        if has_cache:
            pc_scr[i % 2] = jnp.exp2(sc_scr[i % 2] - m).astype(BF16)

    def weighted_values(i):
        rows, head = items[i]
        o = _bdot(p_scr[i % 2], v_ref[:, head])
        if has_cache:
            o = o + _bdot(pc_scr[i % 2], vc_ref[:, head])
        o_ref[rows, head] = (o / o[:, DEN_LANE:DEN_LANE + 1]).astype(BF16)

    scores(0)
    for i in range(n):
        if i + 1 < n:
            scores(i + 1)
        if i > 0:
            weighted_values(i - 1)
        probs(i)
    weighted_values(n - 1)


def _attn_short_kernel(hp, q_ref, k_ref, v_ref, o_ref):
    nt = (((1,), (1,)), ((), ()))
    for h in range(hp):
        head = slice(h * HEAD_PAD, (h + 1) * HEAD_PAD)
        s = lax.dot_general(q_ref[:, head], k_ref[:, head], nt, preferred_element_type=F32)
        p = jnp.exp2(s - jnp.max(s, axis=-1, keepdims=True)).astype(BF16)
        o = _bdot(p, v_ref[:, head])
        o_ref[:, head] = (o / o[:, DEN_LANE:DEN_LANE + 1]).astype(BF16)


def _attn(q, k, v, cache, hp, tq):
    s, l, _ = q.shape
    w = hp * HEAD_PAD
    qspec = pl.BlockSpec((None, tq, w), lambda i, h, r: (i, r, h))
    if cache is None and tq == l:
        kspec = pl.BlockSpec((None, l, w), lambda i, h, r: (i, 0, h))
        return pl.pallas_call(
            functools.partial(_attn_short_kernel, hp),
            grid=(s, MLA_HEADS // hp, 1),
            in_specs=[qspec, kspec, kspec],
            out_specs=qspec,
            out_shape=jax.ShapeDtypeStruct((s, l, HP), BF16),
            compiler_params=_cparams(("arbitrary", "arbitrary", "arbitrary")),
            name="attn_short",
        )(q, k, v)
    kv_mode = dict(pipeline_mode=pl.Buffered(1))
    kspec = pl.BlockSpec((None, l, w), lambda i, h, r: (i, 0, h), **kv_mode)
    in_specs = [qspec, kspec, kspec]
    args = [q, k, v]
    if cache is not None:
        lc = cache[0].shape[1]
        cspec = pl.BlockSpec((None, lc, w), lambda i, h, r: (i, 0, h), **kv_mode)
        in_specs += [cspec, cspec]
        args += list(cache)
    sub = min(tq, ATTN_SUB)
    scratch = [pltpu.VMEM((2, sub, l), F32), pltpu.VMEM((2, sub, l), BF16), pltpu.VMEM((2, sub, LANE), F32)]
    if cache is not None:
        scratch += [pltpu.VMEM((2, sub, lc), F32), pltpu.VMEM((2, sub, lc), BF16)]
    return pl.pallas_call(
        functools.partial(_attn_kernel, hp, sub, cache is not None),
        grid=(s, MLA_HEADS // hp, l // tq),
        in_specs=in_specs,
        out_specs=qspec,
        out_shape=jax.ShapeDtypeStruct((s, l, HP), BF16),
        scratch_shapes=scratch,
        compiler_params=_cparams(("arbitrary", "arbitrary", "arbitrary")),
        name="attn",
    )(*args)


def _route(lt):
    row = lax.broadcasted_iota(jnp.int32, (SUB, lt.shape[1]), 0).astype(F32)
    neg = -jnp.inf
    gl = lt[0:SUB]
    g_ok = row < N_GROUPS
    glm = jnp.where(g_ok, gl, neg)
    gmax = jnp.max(glm, axis=0, keepdims=True)
    gsel = jnp.min(jnp.where(glm == gmax, row, float(SUB)), axis=0, keepdims=True)
    g_w = 1.0 / jnp.sum(jnp.where(g_ok, jnp.exp(gl - gmax), 0.0), axis=0, keepdims=True)
    es = lt[SUB:2 * SUB]
    for g in range(1, N_GROUPS):
        es = jnp.where(gsel == float(g), lt[(g + 1) * SUB:(g + 2) * SUB], es)
    m1 = jnp.max(es, axis=0, keepdims=True)
    i1 = jnp.min(jnp.where(es == m1, row, float(SUB)), axis=0, keepdims=True)
    es2 = jnp.where(row == i1, neg, es)
    m2 = jnp.max(es2, axis=0, keepdims=True)
    i2 = jnp.min(jnp.where(es2 == m2, row, float(SUB)), axis=0, keepdims=True)
    e2 = jnp.exp(m2 - m1)
    inv = g_w / (1.0 + e2)
    base = gsel * EXP_PER_GROUP
    return (jnp.where(row == RT_ID0, base + i1, 0.0) + jnp.where(row == RT_ID1, base + i2, 0.0)
            + jnp.where(row == RT_W0, inv, 0.0) + jnp.where(row == RT_W1, inv * e2, 0.0))


def _pack_bf16_pairs(x):
    k = x.shape[1] // 2
    bits = lax.bitcast_convert_type(x.astype(BF16).astype(F32), jnp.int32)
    return lax.shift_right_logical(bits[:, :k], 16) | (bits[:, k:] & HI_MASK)


def _unpack_bf16_pairs(w):
    lo = lax.bitcast_convert_type(lax.shift_left(w, 16), F32).astype(BF16)
    hi = lax.bitcast_convert_type(w & HI_MASK, F32).astype(BF16)
    return lo, hi


def _back_kernel(x_ref, mod_ref, g1_ref, g2_ref, yf_ref, yl_ref, ys_ref, o_ref,
                 wg_ref, wpf_ref, wpl_ref, wps_ref, wpm_ref, wo_ref, wrh_ref, wrl_ref, br_ref,
                 x1_ref, h2_ref, comb_ref):
    d = D_MODEL
    x = x_ref[...]
    sh1, sc1, gt1 = mod_ref[:, 0:d], mod_ref[:, d:2 * d], mod_ref[:, 2 * d:3 * d]
    sh2, sc2 = mod_ref[:, 3 * d:4 * d], mod_ref[:, 4 * d:5 * d]
    hb = (_rms(x, g1_ref[...]) * (1.0 + sc1) + sh1).astype(BF16)
    merged = None
    for j, (y_ref, w_ref) in enumerate(((yf_ref, wpf_ref), (yl_ref, wpl_ref), (ys_ref, wps_ref), (o_ref, wpm_ref))):
        gate = jax.nn.sigmoid(_bdot(hb, wg_ref[:, j * d:(j + 1) * d]))
        term = gate * _bdot(y_ref[...], w_ref[...])
        merged = term if merged is None else merged + term
    x1 = x + gt1 * _bdot(merged.astype(BF16), wo_ref[...])
    x1_ref[...] = x1
    h2 = _rms(x1, g2_ref[...]) * (1.0 + sc2) + sh2
    h2b = h2.astype(BF16)
    h2_ref[...] = _pack_bf16_pairs(h2)
    h2l = (h2 - h2b.astype(F32)).astype(BF16)
    nt = (((1,), (1,)), ((), ()))
    rdot = lambda w_ref, h: lax.dot_general(w_ref[...], h, nt, preferred_element_type=F32)
    logits_t = rdot(wrh_ref, h2b) + rdot(wrh_ref, h2l) + rdot(wrl_ref, h2b) + br_ref[...]
    comb_ref[...] = _route(logits_t)


def _back(x, mod, yf, yl, ys, o, p, tm):
    t = x.shape[0]
    per_mod = t // (mod.shape[0] * tm)
    row = lambda w: pl.BlockSpec((tm, w), lambda i: (i, 0))
    full = lambda a: pl.BlockSpec(a.shape, lambda i: (0,) * a.ndim, pipeline_mode=pl.Buffered(1))
    weights = [p["w_gate"], p["w_pf"], p["w_pl"], p["w_ps"], p["w_pm"], p["w_out"], p["wr_hi"], p["wr_lo"], p["b_r"]]
    return pl.pallas_call(
        _back_kernel,
        grid=(t // tm,),
        in_specs=[
            row(D_MODEL),
            pl.BlockSpec((None, 1, 6 * D_MODEL), lambda i: (i // per_mod, 0, 0)),
            pl.BlockSpec((1, D_MODEL), lambda i: (0, 0)),
            pl.BlockSpec((1, D_MODEL), lambda i: (0, 0)),
            row(FNET_W), row(LRU_W), row(SC_W), row(HP),
        ] + [full(w) for w in weights],
        out_specs=[row(D_MODEL), row(PACK_W), pl.BlockSpec((SUB, tm), lambda i: (0, i))],
        out_shape=[
            jax.ShapeDtypeStruct((t, D_MODEL), F32),
            jax.ShapeDtypeStruct((t, PACK_W), jnp.int32),
            jax.ShapeDtypeStruct((SUB, t), F32),
        ],
        compiler_params=_cparams(("arbitrary",)),
        name="back",
    )(x, mod, p["norm1_g"], p["norm2_g"], yf, yl, ys, o, *weights)


def _dispatch_plan(rt, tm):
    ids = jnp.concatenate([rt[RT_ID0], rt[RT_ID1]]).astype(jnp.int32)
    onehot = (ids[:, None] == jnp.arange(N_EXPERTS, dtype=jnp.int32)[None, :]).astype(jnp.int32)
    csum = jnp.cumsum(onehot, axis=0)
    counts = csum[-1]
    tiles = (counts + tm - 1) // tm
    tile_end = jnp.cumsum(tiles)
    row_start = (tile_end - tiles) * tm
    pos = jnp.sum(onehot * (row_start[None, :] + csum - 1), axis=1)
    n_tiles = ids.shape[0] // tm + N_EXPERTS
    k = jnp.arange(n_tiles, dtype=jnp.int32)
    tile_expert = jnp.minimum(jnp.sum((k[:, None] >= tile_end[None, :]).astype(jnp.int32), axis=1), N_EXPERTS - 1)
    return pos, tile_expert, tile_end[-1:].astype(jnp.int32), n_tiles


def _sc_worker_rows(n_rows):
    workers = SC_CORES * SC_SUBCORES
    per_w = n_rows // workers
    assert per_w * workers == n_rows and per_w % SC_CHUNK == 0
    return per_w


def _sc_mesh():
    return plsc.VectorSubcoreMesh(core_axis_name="c", subcore_axis_name="s")


def _sc_scatter_rows(src, pos, n_out):
    t, w = src.shape
    per_w = _sc_worker_rows(pos.shape[0])

    @functools.partial(
        pl.kernel, mesh=_sc_mesh(), out_type=jax.ShapeDtypeStruct((n_out, w), src.dtype),
        scratch_types=[pltpu.VMEM((SC_CHUNK,), jnp.int32), pltpu.VMEM((SC_CHUNK, w), src.dtype)],
        name="moe_dispatch")
    def run(src_hbm, pos_hbm, out_hbm, idx_v, rows_v):
        wid = lax.axis_index("s") * SC_CORES + lax.axis_index("c")

        @pl.loop(0, per_w // SC_CHUNK)
        def _(i):
            j0 = wid * per_w + i * SC_CHUNK
            pltpu.sync_copy(pos_hbm.at[pl.ds(j0, SC_CHUNK)], idx_v)
            pltpu.sync_copy(src_hbm.at[pl.ds(lax.rem(j0, t), SC_CHUNK)], rows_v)
            pltpu.sync_copy(rows_v, out_hbm.at[idx_v])

    return run(src, pos)


def _sc_gather_rows(src, pos):
    w = src.shape[1]
    n = pos.shape[0]
    per_w = _sc_worker_rows(n)

    @functools.partial(
        pl.kernel, mesh=_sc_mesh(), out_type=jax.ShapeDtypeStruct((n, w), src.dtype),
        scratch_types=[pltpu.VMEM((SC_CHUNK,), jnp.int32), pltpu.VMEM((SC_CHUNK, w), src.dtype)],
        name="moe_return")
    def run(src_hbm, pos_hbm, out_hbm, idx_v, rows_v):
        wid = lax.axis_index("s") * SC_CORES + lax.axis_index("c")

        @pl.loop(0, per_w // SC_CHUNK)
        def _(i):
            j0 = wid * per_w + i * SC_CHUNK
            pltpu.sync_copy(pos_hbm.at[pl.ds(j0, SC_CHUNK)], idx_v)
            pltpu.sync_copy(src_hbm.at[idx_v], rows_v)
            pltpu.sync_copy(rows_v, out_hbm.at[pl.ds(j0, SC_CHUNK)])

    return run(src, pos)


def _experts_kernel(te_ref, nu_ref, x_ref, w1_ref, w3_ref, w2_ref, y_ref):
    half = D_MODEL // 2

    @pl.when(pl.program_id(0) < nu_ref[0])
    def _():
        lo, hi = _unpack_bf16_pairs(x_ref[...])
        a = _bdot(lo, w1_ref[:half, :].astype(BF16)) + _bdot(hi, w1_ref[half:, :].astype(BF16))
        u = _bdot(lo, w3_ref[:half, :].astype(BF16)) + _bdot(hi, w3_ref[half:, :].astype(BF16))
        mid = (a * jax.nn.sigmoid(a)) * u
        y_ref[...] = _pack_bf16_pairs(_bdot(mid.astype(BF16), w2_ref[...].astype(BF16)))


def _experts(xs, tile_expert, n_used, n_tiles, w1, w3, w2, layer, tm):
    wspec = lambda a: pl.BlockSpec((None, None) + a.shape[2:], lambda i, te, nu: (layer, te[i], 0, 0))
    grid_spec = pltpu.PrefetchScalarGridSpec(
        num_scalar_prefetch=2,
        grid=(n_tiles,),
        in_specs=[pl.BlockSpec((tm, PACK_W), lambda i, te, nu: (jnp.minimum(i, nu[0] - 1), 0)),
                  wspec(w1), wspec(w3), wspec(w2)],
        out_specs=pl.BlockSpec((tm, PACK_W), lambda i, te, nu: (jnp.where(i < nu[0], i, n_tiles), 0)),
    )
    return pl.pallas_call(
        _experts_kernel,
        grid_spec=grid_spec,
        out_shape=jax.ShapeDtypeStruct((xs.shape[0] + tm, PACK_W), jnp.int32),
        compiler_params=_cparams(("arbitrary",)),
        name="experts",
    )(tile_expert, n_used, xs, w1, w3, w2)


def _combine_kernel(x1_ref, mod_ref, rt_ref, g0_ref, g1_ref, o_ref):
    rt = rt_ref[...]
    lo0, hi0 = _unpack_bf16_pairs(g0_ref[...])
    lo1, hi1 = _unpack_bf16_pairs(g1_ref[...])
    w0 = rt[:, RT_W0:RT_W0 + 1]
    w1 = rt[:, RT_W1:RT_W1 + 1]
    half = D_MODEL // 2
    gt2 = mod_ref[:, 5 * D_MODEL:6 * D_MODEL]
    o_ref[:, :half] = x1_ref[:, :half] + gt2[:, :half] * (w0 * lo0.astype(F32) + w1 * lo1.astype(F32))
    o_ref[:, half:] = x1_ref[:, half:] + gt2[:, half:] * (w0 * hi0.astype(F32) + w1 * hi1.astype(F32))


def _combine(x1, mod, rt, g, tm):
    t = x1.shape[0]
    per_mod = t // (mod.shape[0] * tm)
    return pl.pallas_call(
        _combine_kernel,
        grid=(t // tm,),
        in_specs=[
            pl.BlockSpec((tm, D_MODEL), lambda i: (i, 0)),
            pl.BlockSpec((None, 1, 6 * D_MODEL), lambda i: (i // per_mod, 0, 0)),
            pl.BlockSpec((tm, LANE), lambda i: (i, 0)),
            pl.BlockSpec((tm, PACK_W), lambda i: (i, 0)),
            pl.BlockSpec((tm, PACK_W), lambda i: (i + t // tm, 0)),
        ],
        out_specs=pl.BlockSpec((tm, D_MODEL), lambda i: (i, 0)),
        out_shape=jax.ShapeDtypeStruct((t, D_MODEL), F32),
        compiler_params=_cparams(("arbitrary",)),
        name="combine",
    )(x1, mod, rt, g, g)


def _moe(h2p, rt, x1, mod, p, tm_e, tm_c):
    pos, tile_expert, n_used, n_tiles = _dispatch_plan(rt, tm_e)
    xs = _sc_scatter_rows(h2p, pos, n_tiles * tm_e)
    ys = _experts(xs, tile_expert, n_used, n_tiles, p["w1"], p["w3"], p["w2"], p["layer"], tm_e)
    g = _sc_gather_rows(ys, pos)
    rt_cols = jnp.pad(rt.T, ((0, 0), (0, LANE - rt.shape[0])))
    return _combine(x1, mod, rt_cols, g, tm_c)


def _channel_dft():
    n = np.arange(FNET_W // FNET_GROUPS)
    ang = 2.0 * np.pi * ((n[:, None] * n[None, :]) % n.size) / n.size
    eye = np.eye(FNET_GROUPS)
    scale = 1.0 / math.sqrt(n.size)
    return (jnp.asarray(np.kron(eye, np.cos(ang) * scale), F32).astype(BF16),
            jnp.asarray(np.kron(eye, np.sin(ang) * scale), F32).astype(BF16))


def _position_dft(l):
    scale = 1.0 / math.sqrt(l)
    if l <= 256:
        n = np.arange(l)
        ang = 2.0 * np.pi * ((n[:, None] * n[None, :]) % l) / l
        return jnp.asarray(np.cos(ang) * scale, F32).astype(BF16), jnp.asarray(np.sin(ang) * scale, F32).astype(BF16)
    m = DFT_SPLIT
    n = np.arange(l)
    ang_a = 2.0 * np.pi * ((np.arange(l // m)[:, None] * m * n[None, :]) % l) / l
    ang_b = 2.0 * np.pi * ((np.arange(m)[:, None] * n[None, :]) % l) / l
    return (jnp.asarray(np.cos(ang_a), F32), jnp.asarray(np.sin(ang_a), F32),
            jnp.asarray(np.cos(ang_b) * scale, F32), jnp.asarray(np.sin(ang_b) * scale, F32))


def _rope_tables(l, rotate):
    cos = np.ones((l, LANE))
    sinr = np.zeros((l, LANE))
    if rotate:
        t = np.arange(l)
        inv = ROPE_BASE ** (-np.arange(N_FREQ) / N_FREQ)
        ang = np.concatenate([(t // GRID_W)[:, None] * inv, (t % GRID_W)[:, None] * inv], axis=-1)
        half = ROPE // 2
        for lo in (ROPE_X1, ROPE_X2):
            cos[:, lo:lo + half] = np.cos(ang)
        sinr[:, ROPE_X1:ROPE_X1 + half] = -np.sin(ang)
        sinr[:, ROPE_X2:ROPE_X2 + half] = np.sin(ang)
    return tuple(jnp.asarray(a, F32) for a in (cos, sinr))


def _head_lane_source():
    half = ROPE // 2
    src = np.full((HEAD_PAD,), QK_DIM, np.int32)
    src[ROPE_X1:ROPE_X1 + half] = NOPE + np.arange(half)
    src[ROPE_X2:ROPE_X2 + half] = NOPE + half + np.arange(half)
    free = [i for i in range(HEAD_PAD) if src[i] == QK_DIM][:NOPE]
    src[free] = np.arange(NOPE)
    return src


def _place_head_dims(w, n_src):
    src = _head_lane_source()
    src = np.where(src < n_src, src, n_src)
    wz = jnp.concatenate([w[..., :n_src], jnp.zeros(w.shape[:-1] + (1,), w.dtype)], axis=-1)
    out = jnp.take(wz, jnp.asarray(src), axis=-1)
    return out.reshape(out.shape[:-2] + (out.shape[-2] * HEAD_PAD,))


def _place_rope_key(kr):
    half = ROPE // 2
    out = jnp.zeros(kr.shape[:-1] + (LANE,), kr.dtype)
    return out.at[..., ROPE_X1:ROPE_X1 + half].set(kr[..., :half]).at[..., ROPE_X2:ROPE_X2 + half].set(kr[..., half:])


def _take_rope_key(tile):
    half = ROPE // 2
    return jnp.concatenate([tile[..., ROPE_X1:ROPE_X1 + half], tile[..., ROPE_X2:ROPE_X2 + half]], axis=-1)


def _pad_heads(w, lo, hi):
    r = w.shape[0]
    part = w[:, :, lo:hi]
    out = jnp.zeros((r, MLA_HEADS, HEAD_PAD), w.dtype).at[:, :, : hi - lo].set(part)
    return out.reshape(r, HP)


def _blockdiag_halves(w):
    bw = LRU_W // LRU_BLOCKS
    out = jnp.zeros((2, 2, LANE, LANE), w.dtype)
    for half in range(2):
        for k in range(2):
            n = 2 * half + k
            out = out.at[half, :, k * bw:(k + 1) * bw, k * bw:(k + 1) * bw].set(w[:, n])
    return out


def _halves(v):
    return jnp.moveaxis(v.reshape(v.shape[:-1] + (2, LANE)), -2, 0)


def _layer_params(l, a):
    d = D_MODEL
    w_in = a["w_in"][l]
    w_in = jnp.concatenate([w_in[:, :Z_KR], _place_rope_key(w_in[:, Z_KR:])], axis=1)
    w_r = jnp.zeros((RT_ROWS, d), F32).at[:N_GROUPS].set(a["w_gr"][l].T).at[SUB:SUB + N_EXPERTS].set(a["w_er"][l].T)
    b_r = jnp.zeros((RT_ROWS, 1), F32).at[:N_GROUPS, 0].set(a["b_gr"][l]).at[SUB:SUB + N_EXPERTS, 0].set(a["b_er"][l])
    wr_hi = w_r.astype(BF16)
    gpad = lambda g: jnp.take(jnp.concatenate([g, jnp.zeros((1,), F32)]), jnp.asarray(_head_lane_source())).reshape(1, LANE)
    w_pm = jnp.zeros((MLA_HEADS, HEAD_PAD, d), F32).at[:, :V_DIM].set(a["w_pm"][l].reshape(MLA_HEADS, V_DIM, d))
    return {
        "norm1_g": a["norm1_g"][l].reshape(1, d), "norm2_g": a["norm2_g"][l].reshape(1, d),
        "w_in": w_in.astype(BF16),
        "lru_conv": _halves(a["lru_conv"][l]),
        "lru_wa": _blockdiag_halves(a["lru_wa"][l]).astype(BF16),
        "lru_wx": _blockdiag_halves(a["lru_wx"][l]).astype(BF16),
        "lru_ba": _halves(a["lru_ba"][l])[:, :, None, :], "lru_bx": _halves(a["lru_bx"][l])[:, :, None, :],
        "lru_lam": _halves(a["lru_lam"][l])[:, :, None, :],
        "sc_conv": _halves(a["sc_conv"][l]),
        "g_qa": a["g_qa"][l].reshape(1, Q_LORA), "g_kva": a["g_kva"][l].reshape(1, KV_LORA),
        "wq": _place_head_dims(a["w_qb"][l], QK_DIM).astype(BF16),
        "wk": _place_head_dims(a["w_kvb"][l], NOPE).astype(BF16),        "wv": _pad_heads(a["w_kvb"][l], NOPE, NOPE + V_DIM).astype(BF16),
        "gq": gpad(a["g_qn"][l]), "gk": gpad(a["g_kn"][l]),
        "w_gate": a["w_gate"][l].astype(BF16),
        "w_pf": a["w_pf"][l].astype(BF16), "w_pl": a["w_pl"][l].astype(BF16), "w_ps": a["w_ps"][l].astype(BF16),
        "w_pm": w_pm.reshape(HP, d).astype(BF16), "w_out": a["w_out"][l].astype(BF16),
        "wr_hi": wr_hi, "wr_lo": (w_r - wr_hi.astype(F32)).astype(BF16), "b_r": b_r,
        "w1": a["w1"], "w3": a["w3"], "w2": a["w2"], "layer": l,
    }


def _trunk_layer(x, mod, p, consts, h0, cache, cfg):
    s, l = cfg["s"], cfg["l"]
    z = _front(x, mod, p["norm1_g"], p["w_in"], cfg["tm_front"])
    z3 = z.reshape(s, l, Z_W)
    q, k, v, ckv = _qkv(z, consts["rope"], p, l, cfg["tm_qkv"])
    yf = _fourier(z3, consts["dft"], cfg["nb"], cfg["tq_f"])
    yl, fin = _lru(z3, h0, p)
    ys = _sconv(z3, p["sc_conv"])
    o = _attn(q.reshape(s, l, HP), k.reshape(s, l, HP), v.reshape(s, l, HP), cache, cfg["hp"], cfg["tq_a"])
    t = s * l
    x1, h2, comb = _back(x, mod, yf.reshape(t, FNET_W), yl.reshape(t, LRU_W), ys.reshape(t, SC_W),
                         o.reshape(t, HP), p, cfg["tm_back"])
    x2 = _moe(h2, comb, x1, mod, p, cfg["tm_moe"], cfg["tm_comb"])
    return x2, z3, ckv, fin


def kernel(x_prompt, x_sample, cache_ckv, cache_krope, state_rglru, c, c_ctx, norm1_g, norm2_g, w_ada, b_ada, w_in, lru_conv, lru_wa, lru_ba, lru_wx, lru_bx, lru_lam, sc_conv, g_qa, w_qb, g_kva, w_kvb, g_qn, g_kn, w_pf, w_pl, w_ps, w_pm, w_gate, w_out, w_gr, b_gr, w_er, b_er, w1, w3, w2):
    a = dict(norm1_g=norm1_g, norm2_g=norm2_g, w_in=w_in, lru_conv=lru_conv, lru_wa=lru_wa, lru_ba=lru_ba,
             lru_wx=lru_wx, lru_bx=lru_bx, lru_lam=lru_lam, sc_conv=sc_conv, g_qa=g_qa, w_qb=w_qb, g_kva=g_kva,
             w_kvb=w_kvb, g_qn=g_qn, g_kn=g_kn, w_pf=w_pf, w_pl=w_pl, w_ps=w_ps, w_pm=w_pm, w_gate=w_gate,
             w_out=w_out, w_gr=w_gr, b_gr=b_gr, w_er=w_er, b_er=b_er, w1=w1, w3=w3, w2=w2)
    bc, lc, d = x_prompt.shape
    bl, ll, _ = x_sample.shape
    past = cache_ckv.shape[2]

    cond8 = jnp.zeros((SUB, d), F32).at[0].set(c_ctx).at[1:1 + bl].set(c)
    mod_all = _ada(cond8, w_ada, b_ada)

    cc, sc = _channel_dft()
    ctx_consts = {"dft": (cc, sc) + _position_dft(lc), "rope": _rope_tables(lc, False)}
    lat_consts = {"dft": (cc, sc) + _position_dft(ll), "rope": _rope_tables(ll, True)}
    cache_tabs = _rope_tables(bl * past, False)
    ctx_cfg = dict(s=bc, l=lc, tm_front=512, tm_qkv=512, nb=8, tq_f=lc, hp=MLA_HEADS, tq_a=lc, tm_back=512, tm_moe=512, tm_comb=512)
    lat_cfg = dict(s=bl, l=ll, tm_front=512, tm_qkv=512, nb=bl, tq_f=512, hp=MLA_HEADS, tq_a=256, tm_back=512, tm_moe=512, tm_comb=512)

    xp = x_prompt.reshape(bc * lc, d)
    xs = x_sample.reshape(bl * ll, d)
    h0_ctx = jnp.zeros((bc, 2, LRU_W), F32)
    ckv_list, kr_list, lru_list = [], [], []
    for l in range(DEPTH):
        p = _layer_params(l, a)
        mod_ctx = mod_all[l, 0:1].reshape(1, 1, 6 * d)
        mod_lat = mod_all[l, 1:1 + bl].reshape(bl, 1, 6 * d)
        xp, z3, ckv, fin = _trunk_layer(xp, mod_ctx, p, ctx_consts, h0_ctx, None, ctx_cfg)
        ckv_list.append(ckv.reshape(bc, lc, KV_LORA))
        kr_list.append(_take_rope_key(z3[:, :, Z_KR:]))
        lru_list.append(fin)
        kr_tile = _place_rope_key(cache_krope[:, l].reshape(bl * past, ROPE))
        kc, vc = _kvcache(cache_ckv[:, l].reshape(bl * past, KV_LORA), kr_tile, cache_tabs, p)
        cache = (kc.reshape(bl, past, HP), vc.reshape(bl, past, HP))
        xs, _, _, _ = _trunk_layer(xs, mod_lat, p, lat_consts, state_rglru[:, l], cache, lat_cfg)
    return (xp.reshape(bc, lc, d), xs.reshape(bl, ll, d),
            jnp.stack(ckv_list, axis=1), jnp.stack(kr_list, axis=1), jnp.stack(lru_list, axis=1))
```

```python
import functools
import math

import numpy as np
import jax
import jax.numpy as jnp
from jax import lax
from jax.experimental import pallas as pl
from jax.experimental.pallas import tpu as pltpu
from jax.experimental.pallas import tpu_sc as plsc

F32 = jnp.float32
BF16 = jnp.bfloat16

D_MODEL = 1024
DEPTH = 2
GRID_W = 64
EPS = 1e-6
FNET_W = 256
FNET_GROUPS = 4
LRU_W = 256
LRU_BLOCKS = 4
LRU_C = 8.0
SC_W = 256
MLA_HEADS = 8
Q_LORA = 384
KV_LORA = 256
NOPE = 64
ROPE = 32
V_DIM = 64
QK_DIM = NOPE + ROPE
N_FREQ = ROPE // 4
ROPE_BASE = 10000.0
ATTN_SCALE = QK_DIM ** -0.5
LOG2E = math.log2(math.e)
N_GROUPS = 4
EXP_PER_GROUP = 8
N_EXPERTS = N_GROUPS * EXP_PER_GROUP
D_EXPERT = 256

LANE = 128
SUB = 8
HEAD_PAD = LANE
HP = MLA_HEADS * HEAD_PAD
ROPE_X1 = 0
ROPE_X2 = LANE // 2
DEN_LANE = V_DIM
Z_QKV = 6 * 256
Z_KR = Z_QKV + Q_LORA + KV_LORA
Z_W = Z_KR + LANE
QKV_W = Q_LORA + KV_LORA + LANE
VMEM_LIMIT = 52 * 1024 * 1024
PACK_W = D_MODEL // 2
HI_MASK = -65536
RT_ID0, RT_ID1, RT_W0, RT_W1 = 0, 1, 2, 3
RT_ROWS = 48
TOP_K = 2
DFT_SPLIT = 64
ATTN_SUB = 256
SC_CORES = 2
SC_SUBCORES = 16
SC_CHUNK = 128


def _cparams(sem):
    return pltpu.CompilerParams(dimension_semantics=sem, vmem_limit_bytes=VMEM_LIMIT)


def _bdot(a, b):
    return jnp.dot(a, b, preferred_element_type=F32)


def _rms(x, g):
    return x * lax.rsqrt(jnp.mean(x * x, axis=-1, keepdims=True) + EPS) * g


def _ada_kernel(c_ref, w_ref, b_ref, o_ref):
    c = c_ref[...]
    s = (c * jax.nn.sigmoid(c)).astype(BF16)
    o_ref[...] = _bdot(s, w_ref[...].astype(BF16)) + b_ref[...]


def _ada(cond8, w_ada, b_ada):
    nblk = 6 * D_MODEL // 1024
    return pl.pallas_call(
        _ada_kernel,
        grid=(DEPTH, nblk),
        in_specs=[
            pl.BlockSpec((SUB, D_MODEL), lambda l, n: (0, 0)),
            pl.BlockSpec((None, D_MODEL, 1024), lambda l, n: (l, 0, n)),
            pl.BlockSpec((None, 1, 1024), lambda l, n: (l, 0, n)),
        ],
        out_specs=pl.BlockSpec((None, SUB, 1024), lambda l, n: (l, 0, n)),
        out_shape=jax.ShapeDtypeStruct((DEPTH, SUB, 6 * D_MODEL), F32),
        compiler_params=_cparams(("arbitrary", "arbitrary")),
        name="ada",
    )(cond8, w_ada, b_ada.reshape(DEPTH, 1, 6 * D_MODEL))


def _front_kernel(x_ref, mod_ref, g_ref, w_ref, z_ref):
    x = x_ref[...]
    sh = mod_ref[:, 0:D_MODEL]
    sc = mod_ref[:, D_MODEL:2 * D_MODEL]
    h = _rms(x, g_ref[...]) * (1.0 + sc) + sh
    z_ref[...] = _bdot(h.astype(BF16), w_ref[...])


def _front(x, mod, g1, w_in, tm):
    t = x.shape[0]
    per_mod = t // (mod.shape[0] * tm)
    return pl.pallas_call(
        _front_kernel,
        grid=(t // tm,),
        in_specs=[
            pl.BlockSpec((tm, D_MODEL), lambda i: (i, 0)),
            pl.BlockSpec((None, 1, 6 * D_MODEL), lambda i: (i // per_mod, 0, 0)),
            pl.BlockSpec((1, D_MODEL), lambda i: (0, 0)),
            pl.BlockSpec((D_MODEL, Z_W), lambda i: (0, 0)),
        ],
        out_specs=pl.BlockSpec((tm, Z_W), lambda i: (i, 0)),
        out_shape=jax.ShapeDtypeStruct((t, Z_W), F32),
        compiler_params=_cparams(("arbitrary",)),
        name="front",
    )(x, mod, g1, w_in)


def _rope(x, cos, sinr):
    return x * cos + pltpu.roll(x, LANE // 2, 1) * sinr


def _head_norm(x, g):
    ss = jnp.sum(x * x, axis=-1, keepdims=True) * (1.0 / QK_DIM)
    return x * lax.rsqrt(ss + EPS) * g


def _build_kv(ckv, kr_tile, cos, sinr, wk_ref, wv_ref, gk, k_ref, v_ref):
    cb = ckv.astype(BF16)
    kn = _bdot(cb, wk_ref[...])
    for h in range(MLA_HEADS):
        sl = slice(h * HEAD_PAD, (h + 1) * HEAD_PAD)
        k_ref[:, sl] = _rope(_head_norm(kn[:, sl] + kr_tile, gk), cos, sinr).astype(BF16)
    v = _bdot(cb, wv_ref[...])
    lane = lax.broadcasted_iota(jnp.int32, v.shape, 1)
    v_ref[...] = jnp.where((lane & (HEAD_PAD - 1)) == DEN_LANE, 1.0, v).astype(BF16)


def _qkv_kernel(z_ref, cos_ref, sinr_ref, gqa_ref, wq_ref, gkva_ref, wk_ref, wv_ref,
                gq_ref, gk_ref, q_ref, k_ref, v_ref, ckv_ref):
    cos, sinr = cos_ref[...], sinr_ref[...]
    q_c = z_ref[:, 0:Q_LORA]
    kv_c = z_ref[:, Q_LORA:Q_LORA + KV_LORA]
    kr_tile = z_ref[:, Q_LORA + KV_LORA:QKV_W]
    qf = _bdot(_rms(q_c, gqa_ref[...]).astype(BF16), wq_ref[...])
    gq = gq_ref[...] * (ATTN_SCALE * LOG2E)
    for h in range(MLA_HEADS):
        sl = slice(h * HEAD_PAD, (h + 1) * HEAD_PAD)
        q_ref[:, sl] = _rope(_head_norm(qf[:, sl], gq), cos, sinr).astype(BF16)
    ckv = _rms(kv_c, gkva_ref[...])
    ckv_ref[...] = ckv
    _build_kv(ckv, kr_tile, cos, sinr, wk_ref, wv_ref, gk_ref[...], k_ref, v_ref)


def _qkv(z, tabs, p, seq_len, tm):
    t = z.shape[0]
    if tm > seq_len:
        tabs = [jnp.tile(a, (tm // seq_len, 1)) for a in tabs]
    per_seq = max(seq_len // tm, 1)
    tab_spec = pl.BlockSpec((tm, LANE), lambda i: (i % per_seq, 0))
    full = lambda shape: pl.BlockSpec(shape, lambda i: (0,) * len(shape))
    return pl.pallas_call(
        _qkv_kernel,
        grid=(t // tm,),
        in_specs=[
            pl.BlockSpec((tm, QKV_W), lambda i: (i, Z_QKV // QKV_W)),
            tab_spec, tab_spec,
            full((1, Q_LORA)), full((Q_LORA, HP)), full((1, KV_LORA)),
            full((KV_LORA, HP)), full((KV_LORA, HP)), full((1, LANE)), full((1, LANE)),
        ],
        out_specs=[
            pl.BlockSpec((tm, HP), lambda i: (i, 0)),
            pl.BlockSpec((tm, HP), lambda i: (i, 0)),
            pl.BlockSpec((tm, HP), lambda i: (i, 0)),
            pl.BlockSpec((tm, KV_LORA), lambda i: (i, 0)),
        ],
        out_shape=[
            jax.ShapeDtypeStruct((t, HP), BF16),
            jax.ShapeDtypeStruct((t, HP), BF16),
            jax.ShapeDtypeStruct((t, HP), BF16),
            jax.ShapeDtypeStruct((t, KV_LORA), F32),
        ],
        compiler_params=_cparams(("arbitrary",)),
        name="qkv",
    )(z, tabs[0], tabs[1], p["g_qa"], p["wq"], p["g_kva"], p["wk"], p["wv"], p["gq"], p["gk"])


def _kvcache_kernel(ckv_ref, kr_ref, cos_ref, sinr_ref, wk_ref, wv_ref, gk_ref, k_ref, v_ref):
    _build_kv(ckv_ref[...], kr_ref[...], cos_ref[...], sinr_ref[...], wk_ref, wv_ref, gk_ref[...], k_ref, v_ref)


def _kvcache(ckv, kr_tile, tabs, p):
    t = ckv.shape[0]
    return pl.pallas_call(
        _kvcache_kernel,
        out_shape=[jax.ShapeDtypeStruct((t, HP), BF16), jax.ShapeDtypeStruct((t, HP), BF16)],
        compiler_params=_cparams(None),
        name="kvcache",
    )(ckv, kr_tile, tabs[0], tabs[1], p["wk"], p["wv"], p["gk"])


def _fourier_kernel(nb, u_ref, cc_ref, sc_ref, cl_ref, sl_ref, o_ref, a_scr, b_scr):
    @pl.when(pl.program_id(1) == 0)
    def _():
        for b in range(nb):
            u = u_ref[b].astype(BF16)
            a_scr[:, b * FNET_W:(b + 1) * FNET_W] = _bdot(u, cc_ref[...]).astype(BF16)
            b_scr[:, b * FNET_W:(b + 1) * FNET_W] = _bdot(u, sc_ref[...]).astype(BF16)

    y = _bdot(cl_ref[...], a_scr[...]) - _bdot(sl_ref[...], b_scr[...])
    for b in range(nb):
        o_ref[b] = y[:, b * FNET_W:(b + 1) * FNET_W].astype(BF16)


def _fourier_split_kernel(nb, u_ref, cc_ref, sc_ref, ca_ref, sa_ref, cb_ref, sb_ref, o_ref,
                          a_scr, b_scr, cl_scr, sl_scr):
    @pl.when(pl.program_id(1) == 0)
    def _():
        for b in range(nb):
            u = u_ref[b].astype(BF16)
            a_scr[:, b * FNET_W:(b + 1) * FNET_W] = _bdot(u, cc_ref[...]).astype(BF16)
            b_scr[:, b * FNET_W:(b + 1) * FNET_W] = _bdot(u, sc_ref[...]).astype(BF16)

    cb, sb = cb_ref[...], sb_ref[...]
    for j in range(ca_ref.shape[0]):
        ca, sa = ca_ref[j:j + 1, :], sa_ref[j:j + 1, :]
        cl_scr[j * DFT_SPLIT:(j + 1) * DFT_SPLIT, :] = (ca * cb - sa * sb).astype(BF16)
        sl_scr[j * DFT_SPLIT:(j + 1) * DFT_SPLIT, :] = (sa * cb + ca * sb).astype(BF16)
    y = _bdot(cl_scr[...], a_scr[...]) - _bdot(sl_scr[...], b_scr[...])
    for b in range(nb):
        o_ref[b] = y[:, b * FNET_W:(b + 1) * FNET_W].astype(BF16)


def _fourier_split(z3, mats, nb, tq):
    s, l, _ = z3.shape
    cc, sc, ca, sa, cb, sb = mats
    full = lambda a: pl.BlockSpec(a.shape, lambda i, r: (0, 0))
    rows = pl.BlockSpec((tq // DFT_SPLIT, l), lambda i, r: (r, 0))
    return pl.pallas_call(
        functools.partial(_fourier_split_kernel, nb),
        grid=(s // nb, l // tq),
        in_specs=[pl.BlockSpec((nb, l, FNET_W), lambda i, r: (i, 0, 0)), full(cc), full(sc), rows, rows,
                  full(cb), full(sb)],
        out_specs=pl.BlockSpec((nb, tq, FNET_W), lambda i, r: (i, r, 0)),
        out_shape=jax.ShapeDtypeStruct((s, l, FNET_W), BF16),
        scratch_shapes=[pltpu.VMEM((l, nb * FNET_W), BF16), pltpu.VMEM((l, nb * FNET_W), BF16),
                        pltpu.VMEM((tq, l), BF16), pltpu.VMEM((tq, l), BF16)],
        compiler_params=_cparams(("arbitrary", "arbitrary")),
        name="fourier_split",
    )(z3, cc, sc, ca, sa, cb, sb)


def _fourier(z3, mats, nb, tq):
    s, l, _ = z3.shape
    if len(mats) == 6:
        return _fourier_split(z3, mats, nb, tq)
    cc, sc, cl, sl = mats
    return pl.pallas_call(
        functools.partial(_fourier_kernel, nb),
        grid=(s // nb, l // tq),
        in_specs=[
            pl.BlockSpec((nb, l, FNET_W), lambda i, r: (i, 0, 0)),
            pl.BlockSpec((FNET_W, FNET_W), lambda i, r: (0, 0)),
            pl.BlockSpec((FNET_W, FNET_W), lambda i, r: (0, 0)),
            pl.BlockSpec((tq, l), lambda i, r: (r, 0)),
            pl.BlockSpec((tq, l), lambda i, r: (r, 0)),
        ],
        out_specs=pl.BlockSpec((nb, tq, FNET_W), lambda i, r: (i, r, 0)),
        out_shape=jax.ShapeDtypeStruct((s, l, FNET_W), BF16),
        scratch_shapes=[pltpu.VMEM((l, nb * FNET_W), BF16), pltpu.VMEM((l, nb * FNET_W), BF16)],
        compiler_params=_cparams(("arbitrary", "arbitrary")),
        name="fourier",
    )(z3, cc, sc, cl, sl)


def _load_ext(ref, c, rows, nchunks):
    t0 = pl.multiple_of(c * rows, rows)
    main = ref[pl.ds(t0, rows), :]
    lo = pl.multiple_of(jnp.maximum(t0 - SUB, 0), SUB)
    hi = pl.multiple_of(jnp.minimum(t0 + rows, (nchunks - 1) * rows + rows - SUB), SUB)
    prev = jnp.where(c > 0, ref[pl.ds(lo, SUB), :], 0.0)
    nxt = jnp.where(c < nchunks - 1, ref[pl.ds(hi, SUB), :], 0.0)
    return jnp.concatenate([prev, main, nxt], axis=0)


def _shifted(ext, off, rows):
    n = ext.shape[0]
    r = ext if off == 0 else pltpu.roll(ext, (-off) % n, 0)
    return r[SUB:SUB + rows]


def _gelu_tanh(x):
    return 0.5 * x * (1.0 + jnp.tanh(math.sqrt(2.0 / math.pi) * (x + 0.044715 * (x * x * x))))


def _lru_gates(xc, d, wa_ref, wx_ref, ba_ref, bx_ref, lam_ref):
    xb = xc.astype(BF16)
    r = jax.nn.sigmoid(_bdot(xb, wa_ref[d]) + ba_ref[d])
    i = jax.nn.sigmoid(_bdot(xb, wx_ref[d]) + bx_ref[d])
    nl = -lam_ref[d]
    softplus = jnp.maximum(nl, 0.0) + jnp.log1p(jnp.exp(-jnp.abs(nl)))
    la = (-LRU_C) * r * softplus
    a = jnp.exp(la)
    one_m_a2 = -jnp.tanh(la) * (a * a + 1.0)
    b = jnp.sqrt(one_m_a2) * (i * xc)
    return a, b


def _group_scan(a, b, reverse):
    rows = a.shape[0]
    rm = lax.broadcasted_iota(jnp.int32, a.shape, 0) & (SUB - 1)
    for s in (1, 2, 4):
        if reverse:
            sh, m = rows - s, rm + s <= SUB - 1
        else:
            sh, m = s, rm >= s
        a_sh = pltpu.roll(a, sh, 0)
        b_sh = pltpu.roll(b, sh, 0)
        b = jnp.where(m, a * b_sh + b, b)
        a = jnp.where(m, a * a_sh, a)
    return a, b


def _carry_scan(a, b, carry, reverse):
    ng = a.shape[0] // SUB
    out = [None] * ng
    order = range(ng - 1, -1, -1) if reverse else range(ng)
    for g in order:
        hg = a[g * SUB:(g + 1) * SUB] * carry + b[g * SUB:(g + 1) * SUB]
        out[g] = hg
        carry = hg[0:1] if reverse else hg[SUB - 1:SUB]
    return jnp.concatenate(out, axis=0), carry


def _lru_kernel(rows, nchunks, xl_ref, gl_ref, bs_ref, cs_ref, xs_ref, h0_ref, cw_ref, wa_ref, wx_ref,
                ba_ref, bx_ref, lam_ref, sw_ref, yl_ref, ys_ref, fin_ref, xc_scr, hf_scr, hb_scr):
    cw = cw_ref[...]
    sw = sw_ref[...]
    chunk = lambda c: pl.ds(pl.multiple_of(c * rows, rows), rows)

    def conv_body(c, carry):
        ext = _load_ext(xl_ref, c, rows, nchunks)
        xc = _shifted(ext, -2, rows) * cw[0:1]
        for k in range(1, 4):
            xc = xc + _shifted(ext, k - 2, rows) * cw[k:k + 1]
        xc_scr[chunk(c), :] = xc
        return carry

    lax.fori_loop(0, nchunks, conv_body, 0)

    def direction(c, d, carry, h_scr):
        a, b = _lru_gates(xc_scr[chunk(c), :], d, wa_ref, wx_ref, ba_ref, bx_ref, lam_ref)
        a, b = _group_scan(a, b, d == 1)
        h, carry = _carry_scan(a, b, carry, d == 1)
        h_scr[chunk(c), :] = h
        return carry

    def scan_body(j, carries):
        cf, cb = carries
        return direction(j, 0, cf, hf_scr), direction(nchunks - 1 - j, 1, cb, hb_scr)

    cf, cb = lax.fori_loop(0, nchunks, scan_body, (h0_ref[0:1, :], h0_ref[1:2, :]))
    fin_ref[0:1, :] = cf
    fin_ref[1:2, :] = cb

    def out_body(c, carry):
        y = (hf_scr[chunk(c), :] + hb_scr[chunk(c), :]) * _gelu_tanh(gl_ref[chunk(c), :])
        yl_ref[chunk(c), :] = y.astype(BF16)
        ext = _load_ext(cs_ref, c, rows, nchunks) * _load_ext(xs_ref, c, rows, nchunks)
        acc = _shifted(ext, -1, rows) * sw[0:1]
        acc = acc + _shifted(ext, 0, rows) * sw[1:2]
        acc = acc + _shifted(ext, 1, rows) * sw[2:3]
        ys_ref[chunk(c), :] = (bs_ref[chunk(c), :] * acc).astype(BF16)
        return carry

    lax.fori_loop(0, nchunks, out_body, 0)


def _lru(z3, h0, p):
    s, l, _ = z3.shape
    rows = min(l, 256)
    nchunks = l // rows
    half = lambda blk: pl.BlockSpec((None, l, LANE), lambda i, j: (i, 0, blk + j))
    wspec = lambda shape: pl.BlockSpec((None,) + shape, lambda i, j: (j,) + (0,) * len(shape))
    seq_half = pl.BlockSpec((None, l, LANE), lambda i, j: (i, 0, j))
    return pl.pallas_call(
        functools.partial(_lru_kernel, rows, nchunks),
        grid=(s, 2),
        in_specs=[
            half(2), half(4), half(6), half(8), half(10),
            pl.BlockSpec((None, 2, LANE), lambda i, j: (i, 0, j)),
            wspec((4, LANE)), wspec((2, LANE, LANE)), wspec((2, LANE, LANE)),
            wspec((2, 1, LANE)), wspec((2, 1, LANE)), wspec((2, 1, LANE)), wspec((3, LANE)),
        ],
        out_specs=[seq_half, seq_half, pl.BlockSpec((None, 2, LANE), lambda i, j: (i, 0, j))],
        out_shape=[jax.ShapeDtypeStruct((s, l, LRU_W), BF16), jax.ShapeDtypeStruct((s, l, SC_W), BF16),
                   jax.ShapeDtypeStruct((s, 2, LRU_W), F32)],
        scratch_shapes=[pltpu.VMEM((l, LANE), F32), pltpu.VMEM((l, LANE), F32), pltpu.VMEM((l, LANE), F32)],
        compiler_params=_cparams(("arbitrary", "arbitrary")),
        name="lru",
    )(z3, z3, z3, z3, z3, h0, p["lru_conv"], p["lru_wa"], p["lru_wx"], p["lru_ba"], p["lru_bx"], p["lru_lam"],
      p["sc_conv"])


def _attn_kernel(hp, sub, has_cache, *refs):
    if has_cache:
        q_ref, k_ref, v_ref, kc_ref, vc_ref, o_ref, s_scr, p_scr, m_scr, sc_scr, pc_scr = refs
    else:
        q_ref, k_ref, v_ref, o_ref, s_scr, p_scr, m_scr = refs
    nt = (((1,), (1,)), ((), ()))
    items = [(slice(r * sub, (r + 1) * sub), slice(h * HEAD_PAD, (h + 1) * HEAD_PAD))
             for r in range(q_ref.shape[0] // sub) for h in range(hp)]
    n = len(items)

    def scores(i):
        rows, head = items[i]
        q = q_ref[rows, head]
        s = lax.dot_general(q, k_ref[:, head], nt, preferred_element_type=F32)
        m = jnp.max(s, axis=-1, keepdims=True)
        s_scr[i % 2] = s
        if has_cache:
            sc = lax.dot_general(q, kc_ref[:, head], nt, preferred_element_type=F32)
            m = jnp.maximum(m, jnp.max(sc, axis=-1, keepdims=True))
            sc_scr[i % 2] = sc
        m_scr[i % 2] = jnp.broadcast_to(m, m_scr.shape[1:])

    def probs(i):
        m = m_scr[i % 2][:, 0:1]
        p_scr[i % 2] = jnp.exp2(s_scr[i % 2] - m).astype(BF16)
        if has_cache:
            pc_scr[i % 2] = jnp.exp2(sc_scr[i % 2] - m).astype(BF16)

    def weighted_values(i):
        rows, head = items[i]
        o = _bdot(p_scr[i % 2], v_ref[:, head])
        if has_cache:
            o = o + _bdot(pc_scr[i % 2], vc_ref[:, head])
        o_ref[rows, head] = (o / o[:, DEN_LANE:DEN_LANE + 1]).astype(BF16)

    scores(0)
    for i in range(n):
        if i + 1 < n:
            scores(i + 1)
        if i > 0:
            weighted_values(i - 1)
        probs(i)
    weighted_values(n - 1)


def _attn_short_kernel(hp, q_ref, k_ref, v_ref, o_ref):
    nt = (((1,), (1,)), ((), ()))
    heads = [slice(h * HEAD_PAD, (h + 1) * HEAD_PAD) for h in range(hp)]
    scores = [lax.dot_general(q_ref[:, hd], k_ref[:, hd], nt, preferred_element_type=F32) for hd in heads]
    probs = [jnp.exp2(s - jnp.max(s, axis=-1, keepdims=True)).astype(BF16) for s in scores]
    for hd, p in zip(heads, probs):
        o = _bdot(p, v_ref[:, hd])
        o_ref[:, hd] = (o / o[:, DEN_LANE:DEN_LANE + 1]).astype(BF16)


def _attn(q, k, v, cache, hp, tq):
    s, l, _ = q.shape
    w = hp * HEAD_PAD
    qspec = pl.BlockSpec((None, tq, w), lambda i, h, r: (i, r, h))
    if cache is None and tq == l:
        kspec = pl.BlockSpec((None, l, w), lambda i, h, r: (i, 0, h))
        return pl.pallas_call(
            functools.partial(_attn_short_kernel, hp),
            grid=(s, MLA_HEADS // hp, 1),
            in_specs=[qspec, kspec, kspec],
            out_specs=qspec,
            out_shape=jax.ShapeDtypeStruct((s, l, HP), BF16),
            compiler_params=_cparams(("arbitrary", "arbitrary", "arbitrary")),
            name="attn_short",
        )(q, k, v)
    kv_mode = dict(pipeline_mode=pl.Buffered(1))
    kspec = pl.BlockSpec((None, l, w), lambda i, h, r: (i, 0, h), **kv_mode)
    in_specs = [qspec, kspec, kspec]
    args = [q, k, v]
    if cache is not None:
        lc = cache[0].shape[1]
        cspec = pl.BlockSpec((None, lc, w), lambda i, h, r: (i, 0, h), **kv_mode)
        in_specs += [cspec, cspec]
        args += list(cache)
    sub = min(tq, ATTN_SUB)
    scratch = [pltpu.VMEM((2, sub, l), F32), pltpu.VMEM((2, sub, l), BF16), pltpu.VMEM((2, sub, LANE), F32)]
    if cache is not None:
        scratch += [pltpu.VMEM((2, sub, lc), F32), pltpu.VMEM((2, sub, lc), BF16)]
    return pl.pallas_call(
        functools.partial(_attn_kernel, hp, sub, cache is not None),
        grid=(s, MLA_HEADS // hp, l // tq),
        in_specs=in_specs,
        out_specs=qspec,
        out_shape=jax.ShapeDtypeStruct((s, l, HP), BF16),
        scratch_shapes=scratch,
        compiler_params=_cparams(("arbitrary", "arbitrary", "arbitrary")),
        name="attn",
    )(*args)


def _route(lt):
    row = lax.broadcasted_iota(jnp.int32, (SUB, lt.shape[1]), 0).astype(F32)
    neg = -jnp.inf
    gl = lt[0:SUB]
    g_ok = row < N_GROUPS
    glm = jnp.where(g_ok, gl, neg)
    gmax = jnp.max(glm, axis=0, keepdims=True)
    gsel = jnp.min(jnp.where(glm == gmax, row, float(SUB)), axis=0, keepdims=True)
    g_w = 1.0 / jnp.sum(jnp.where(g_ok, jnp.exp(gl - gmax), 0.0), axis=0, keepdims=True)
    es = lt[SUB:2 * SUB]
    for g in range(1, N_GROUPS):
        es = jnp.where(gsel == float(g), lt[(g + 1) * SUB:(g + 2) * SUB], es)
    m1 = jnp.max(es, axis=0, keepdims=True)
    i1 = jnp.min(jnp.where(es == m1, row, float(SUB)), axis=0, keepdims=True)
    es2 = jnp.where(row == i1, neg, es)
    m2 = jnp.max(es2, axis=0, keepdims=True)
    i2 = jnp.min(jnp.where(es2 == m2, row, float(SUB)), axis=0, keepdims=True)
    e2 = jnp.exp(m2 - m1)
    inv = g_w / (1.0 + e2)
    base = gsel * EXP_PER_GROUP
    return (jnp.where(row == RT_ID0, base + i1, 0.0) + jnp.where(row == RT_ID1, base + i2, 0.0)
            + jnp.where(row == RT_W0, inv, 0.0) + jnp.where(row == RT_W1, inv * e2, 0.0))


def _pack_bf16_pairs(x):
    k = x.shape[1] // 2
    bits = lax.bitcast_convert_type(x.astype(BF16).astype(F32), jnp.int32)
    return lax.shift_right_logical(bits[:, :k], 16) | (bits[:, k:] & HI_MASK)


def _unpack_bf16_pairs(w):
    lo = lax.bitcast_convert_type(lax.shift_left(w, 16), F32).astype(BF16)
    hi = lax.bitcast_convert_type(w & HI_MASK, F32).astype(BF16)
    return lo, hi


def _back_kernel(x_ref, mod_ref, g1_ref, g2_ref, yf_ref, yl_ref, ys_ref, o_ref,
                 wg_ref, wpf_ref, wpl_ref, wps_ref, wpm_ref, wo_ref, wrh_ref, wrl_ref, br_ref,
                 x1_ref, h2_ref, comb_ref):
    d = D_MODEL
    x = x_ref[...]
    sh1, sc1, gt1 = mod_ref[:, 0:d], mod_ref[:, d:2 * d], mod_ref[:, 2 * d:3 * d]
    sh2, sc2 = mod_ref[:, 3 * d:4 * d], mod_ref[:, 4 * d:5 * d]
    hb = (_rms(x, g1_ref[...]) * (1.0 + sc1) + sh1).astype(BF16)
    merged = None
    for j, (y_ref, w_ref) in enumerate(((yf_ref, wpf_ref), (yl_ref, wpl_ref), (ys_ref, wps_ref), (o_ref, wpm_ref))):
        gate = jax.nn.sigmoid(_bdot(hb, wg_ref[:, j * d:(j + 1) * d]))
        term = gate * _bdot(y_ref[...], w_ref[...])
        merged = term if merged is None else merged + term
    x1 = x + gt1 * _bdot(merged.astype(BF16), wo_ref[...])
    x1_ref[...] = x1
    h2 = _rms(x1, g2_ref[...]) * (1.0 + sc2) + sh2
    h2b = h2.astype(BF16)
    h2_ref[...] = _pack_bf16_pairs(h2)
    h2l = (h2 - h2b.astype(F32)).astype(BF16)
    nt = (((1,), (1,)), ((), ()))
    rdot = lambda w_ref, h: lax.dot_general(w_ref[...], h, nt, preferred_element_type=F32)
    logits_t = rdot(wrh_ref, h2b) + rdot(wrh_ref, h2l) + rdot(wrl_ref, h2b) + br_ref[...]
    comb_ref[...] = _route(logits_t)


def _back(x, mod, yf, yl, ys, o, p, tm):
    t = x.shape[0]
    per_mod = t // (mod.shape[0] * tm)
    row = lambda w: pl.BlockSpec((tm, w), lambda i: (i, 0))
    full = lambda a: pl.BlockSpec(a.shape, lambda i: (0,) * a.ndim, pipeline_mode=pl.Buffered(1))
    weights = [p["w_gate"], p["w_pf"], p["w_pl"], p["w_ps"], p["w_pm"], p["w_out"], p["wr_hi"], p["wr_lo"], p["b_r"]]
    return pl.pallas_call(
        _back_kernel,
        grid=(t // tm,),
        in_specs=[
            row(D_MODEL),
            pl.BlockSpec((None, 1, 6 * D_MODEL), lambda i: (i // per_mod, 0, 0)),
            pl.BlockSpec((1, D_MODEL), lambda i: (0, 0)),
            pl.BlockSpec((1, D_MODEL), lambda i: (0, 0)),
            row(FNET_W), row(LRU_W), row(SC_W), row(HP),
        ] + [full(w) for w in weights],
        out_specs=[row(D_MODEL), row(PACK_W), pl.BlockSpec((SUB, tm), lambda i: (0, i))],
        out_shape=[
            jax.ShapeDtypeStruct((t, D_MODEL), F32),
            jax.ShapeDtypeStruct((t, PACK_W), jnp.int32),
            jax.ShapeDtypeStruct((SUB, t), F32),
        ],
        compiler_params=_cparams(("arbitrary",)),
        name="back",
    )(x, mod, p["norm1_g"], p["norm2_g"], yf, yl, ys, o, *weights)


def _dispatch_plan(rt, tm):
    ids = jnp.concatenate([rt[RT_ID0], rt[RT_ID1]]).astype(jnp.int32)
    onehot = (ids[:, None] == jnp.arange(N_EXPERTS, dtype=jnp.int32)[None, :]).astype(jnp.int32)
    csum = jnp.cumsum(onehot, axis=0)
    counts = csum[-1]
    tiles = (counts + tm - 1) // tm
    tile_end = jnp.cumsum(tiles)
    row_start = (tile_end - tiles) * tm
    pos = jnp.sum(onehot * (row_start[None, :] + csum - 1), axis=1)
    n_tiles = ids.shape[0] // tm + N_EXPERTS
    k = jnp.arange(n_tiles, dtype=jnp.int32)
    tile_expert = jnp.minimum(jnp.sum((k[:, None] >= tile_end[None, :]).astype(jnp.int32), axis=1), N_EXPERTS - 1)
    return pos, tile_expert, tile_end[-1:].astype(jnp.int32), n_tiles


def _sc_worker_rows(n_rows):
    workers = SC_CORES * SC_SUBCORES
    per_w = n_rows // workers
    assert per_w * workers == n_rows and per_w % SC_CHUNK == 0
    return per_w


def _sc_mesh():
    return plsc.VectorSubcoreMesh(core_axis_name="c", subcore_axis_name="s")


def _sc_scatter_rows(src, pos, n_out):
    t, w = src.shape
    per_w = _sc_worker_rows(pos.shape[0])

    @functools.partial(
        pl.kernel, mesh=_sc_mesh(), out_type=jax.ShapeDtypeStruct((n_out, w), src.dtype),
        scratch_types=[pltpu.VMEM((SC_CHUNK,), jnp.int32), pltpu.VMEM((SC_CHUNK, w), src.dtype)],
        name="moe_dispatch")
    def run(src_hbm, pos_hbm, out_hbm, idx_v, rows_v):
        wid = lax.axis_index("s") * SC_CORES + lax.axis_index("c")

        @pl.loop(0, per_w // SC_CHUNK)
        def _(i):
            j0 = wid * per_w + i * SC_CHUNK
            pltpu.sync_copy(pos_hbm.at[pl.ds(j0, SC_CHUNK)], idx_v)
            pltpu.sync_copy(src_hbm.at[pl.ds(lax.rem(j0, t), SC_CHUNK)], rows_v)
            pltpu.sync_copy(rows_v, out_hbm.at[idx_v])

    return run(src, pos)


def _sc_gather_rows(src, pos):
    w = src.shape[1]
    n = pos.shape[0]
    per_w = _sc_worker_rows(n)

    @functools.partial(
        pl.kernel, mesh=_sc_mesh(), out_type=jax.ShapeDtypeStruct((n, w), src.dtype),
        scratch_types=[pltpu.VMEM((SC_CHUNK,), jnp.int32), pltpu.VMEM((SC_CHUNK, w), src.dtype)],
        name="moe_return")
    def run(src_hbm, pos_hbm, out_hbm, idx_v, rows_v):
        wid = lax.axis_index("s") * SC_CORES + lax.axis_index("c")

        @pl.loop(0, per_w // SC_CHUNK)
        def _(i):
            j0 = wid * per_w + i * SC_CHUNK
            pltpu.sync_copy(pos_hbm.at[pl.ds(j0, SC_CHUNK)], idx_v)
            pltpu.sync_copy(src_hbm.at[idx_v], rows_v)
            pltpu.sync_copy(rows_v, out_hbm.at[pl.ds(j0, SC_CHUNK)])

    return run(src, pos)


def _experts_kernel(te_ref, nu_ref, x_ref, w1_ref, w3_ref, w2_ref, y_ref):
    half = D_MODEL // 2

    @pl.when(pl.program_id(0) < nu_ref[0])
    def _():
        lo, hi = _unpack_bf16_pairs(x_ref[...])
        a = _bdot(lo, w1_ref[:half, :].astype(BF16)) + _bdot(hi, w1_ref[half:, :].astype(BF16))
        u = _bdot(lo, w3_ref[:half, :].astype(BF16)) + _bdot(hi, w3_ref[half:, :].astype(BF16))
        mid = (a * jax.nn.sigmoid(a)) * u
        y_ref[...] = _pack_bf16_pairs(_bdot(mid.astype(BF16), w2_ref[...].astype(BF16)))


def _experts(xs, tile_expert, n_used, n_tiles, w1, w3, w2, layer, tm):
    wspec = lambda a: pl.BlockSpec((None, None) + a.shape[2:], lambda i, te, nu: (layer, te[i], 0, 0))
    grid_spec = pltpu.PrefetchScalarGridSpec(
        num_scalar_prefetch=2,
        grid=(n_tiles,),
        in_specs=[pl.BlockSpec((tm, PACK_W), lambda i, te, nu: (jnp.minimum(i, nu[0] - 1), 0)),
                  wspec(w1), wspec(w3), wspec(w2)],
        out_specs=pl.BlockSpec((tm, PACK_W), lambda i, te, nu: (jnp.where(i < nu[0], i, n_tiles), 0)),
    )
    return pl.pallas_call(
        _experts_kernel,
        grid_spec=grid_spec,
        out_shape=jax.ShapeDtypeStruct((xs.shape[0] + tm, PACK_W), jnp.int32),
        compiler_params=_cparams(("arbitrary",)),
        name="experts",
    )(tile_expert, n_used, xs, w1, w3, w2)


def _combine_kernel(x1_ref, mod_ref, rt_ref, g0_ref, g1_ref, o_ref):
    rt = rt_ref[...]
    lo0, hi0 = _unpack_bf16_pairs(g0_ref[...])
    lo1, hi1 = _unpack_bf16_pairs(g1_ref[...])
    w0 = rt[:, RT_W0:RT_W0 + 1]
    w1 = rt[:, RT_W1:RT_W1 + 1]
    half = D_MODEL // 2
    gt2 = mod_ref[:, 5 * D_MODEL:6 * D_MODEL]
    o_ref[:, :half] = x1_ref[:, :half] + gt2[:, :half] * (w0 * lo0.astype(F32) + w1 * lo1.astype(F32))
    o_ref[:, half:] = x1_ref[:, half:] + gt2[:, half:] * (w0 * hi0.astype(F32) + w1 * hi1.astype(F32))


def _combine(x1, mod, rt, g, tm):
    t = x1.shape[0]
    per_mod = t // (mod.shape[0] * tm)
    return pl.pallas_call(
        _combine_kernel,
        grid=(t // tm,),
        in_specs=[
            pl.BlockSpec((tm, D_MODEL), lambda i: (i, 0)),
            pl.BlockSpec((None, 1, 6 * D_MODEL), lambda i: (i // per_mod, 0, 0)),
            pl.BlockSpec((tm, LANE), lambda i: (i, 0)),
            pl.BlockSpec((tm, PACK_W), lambda i: (i, 0)),
            pl.BlockSpec((tm, PACK_W), lambda i: (i + t // tm, 0)),
        ],
        out_specs=pl.BlockSpec((tm, D_MODEL), lambda i: (i, 0)),
        out_shape=jax.ShapeDtypeStruct((t, D_MODEL), F32),
        compiler_params=_cparams(("arbitrary",)),
        name="combine",
    )(x1, mod, rt, g, g)


def _moe(h2p, rt, x1, mod, p, tm_e, tm_c):
    pos, tile_expert, n_used, n_tiles = _dispatch_plan(rt, tm_e)
    xs = _sc_scatter_rows(h2p, pos, n_tiles * tm_e)
    ys = _experts(xs, tile_expert, n_used, n_tiles, p["w1"], p["w3"], p["w2"], p["layer"], tm_e)
    g = _sc_gather_rows(ys, pos)
    rt_cols = jnp.pad(rt.T, ((0, 0), (0, LANE - rt.shape[0])))
    return _combine(x1, mod, rt_cols, g, tm_c)


def _channel_dft():
    n = np.arange(FNET_W // FNET_GROUPS)
    ang = 2.0 * np.pi * ((n[:, None] * n[None, :]) % n.size) / n.size
    eye = np.eye(FNET_GROUPS)
    scale = 1.0 / math.sqrt(n.size)
    return (jnp.asarray(np.kron(eye, np.cos(ang) * scale), F32).astype(BF16),
            jnp.asarray(np.kron(eye, np.sin(ang) * scale), F32).astype(BF16))


def _position_dft(l):
    scale = 1.0 / math.sqrt(l)
    if l <= 256:
        n = np.arange(l)
        ang = 2.0 * np.pi * ((n[:, None] * n[None, :]) % l) / l
        return jnp.asarray(np.cos(ang) * scale, F32).astype(BF16), jnp.asarray(np.sin(ang) * scale, F32).astype(BF16)
    m = DFT_SPLIT
    n = np.arange(l)
    ang_a = 2.0 * np.pi * ((np.arange(l // m)[:, None] * m * n[None, :]) % l) / l
    ang_b = 2.0 * np.pi * ((np.arange(m)[:, None] * n[None, :]) % l) / l
    return (jnp.asarray(np.cos(ang_a), F32), jnp.asarray(np.sin(ang_a), F32),
            jnp.asarray(np.cos(ang_b) * scale, F32), jnp.asarray(np.sin(ang_b) * scale, F32))


def _rope_tables(l, rotate):
    cos = np.ones((l, LANE))
    sinr = np.zeros((l, LANE))
    if rotate:
        t = np.arange(l)
        inv = ROPE_BASE ** (-np.arange(N_FREQ) / N_FREQ)
        ang = np.concatenate([(t // GRID_W)[:, None] * inv, (t % GRID_W)[:, None] * inv], axis=-1)
        half = ROPE // 2
        for lo in (ROPE_X1, ROPE_X2):
            cos[:, lo:lo + half] = np.cos(ang)
        sinr[:, ROPE_X1:ROPE_X1 + half] = -np.sin(ang)
        sinr[:, ROPE_X2:ROPE_X2 + half] = np.sin(ang)
    return tuple(jnp.asarray(a, F32) for a in (cos, sinr))


def _head_lane_source():
    half = ROPE // 2
    src = np.full((HEAD_PAD,), QK_DIM, np.int32)
    src[ROPE_X1:ROPE_X1 + half] = NOPE + np.arange(half)
    src[ROPE_X2:ROPE_X2 + half] = NOPE + half + np.arange(half)
    free = [i for i in range(HEAD_PAD) if src[i] == QK_DIM][:NOPE]
    src[free] = np.arange(NOPE)
    return src


def _place_head_dims(w, n_src):
    src = _head_lane_source()
    src = np.where(src < n_src, src, n_src)
    wz = jnp.concatenate([w[..., :n_src], jnp.zeros(w.shape[:-1] + (1,), w.dtype)], axis=-1)
    out = jnp.take(wz, jnp.asarray(src), axis=-1)
    return out.reshape(out.shape[:-2] + (out.shape[-2] * HEAD_PAD,))


def _place_rope_key(kr):
    half = ROPE // 2
    out = jnp.zeros(kr.shape[:-1] + (LANE,), kr.dtype)
    return out.at[..., ROPE_X1:ROPE_X1 + half].set(kr[..., :half]).at[..., ROPE_X2:ROPE_X2 + half].set(kr[..., half:])


def _take_rope_key(tile):
    half = ROPE // 2
    return jnp.concatenate([tile[..., ROPE_X1:ROPE_X1 + half], tile[..., ROPE_X2:ROPE_X2 + half]], axis=-1)


def _pad_heads(w, lo, hi):
    r = w.shape[0]
    part = w[:, :, lo:hi]
    out = jnp.zeros((r, MLA_HEADS, HEAD_PAD), w.dtype).at[:, :, : hi - lo].set(part)
    return out.reshape(r, HP)


def _blockdiag_halves(w):
    bw = LRU_W // LRU_BLOCKS
    out = jnp.zeros((2, 2, LANE, LANE), w.dtype)
    for half in range(2):
        for k in range(2):
            n = 2 * half + k
            out = out.at[half, :, k * bw:(k + 1) * bw, k * bw:(k + 1) * bw].set(w[:, n])
    return out


def _halves(v):
    return jnp.moveaxis(v.reshape(v.shape[:-1] + (2, LANE)), -2, 0)


def _layer_params(l, a):
    d = D_MODEL
    w_in = a["w_in"][l]
    w_in = jnp.concatenate([w_in[:, :Z_KR], _place_rope_key(w_in[:, Z_KR:])], axis=1)
    w_r = jnp.zeros((RT_ROWS, d), F32).at[:N_GROUPS].set(a["w_gr"][l].T).at[SUB:SUB + N_EXPERTS].set(a["w_er"][l].T)
    b_r = jnp.zeros((RT_ROWS, 1), F32).at[:N_GROUPS, 0].set(a["b_gr"][l]).at[SUB:SUB + N_EXPERTS, 0].set(a["b_er"][l])
    wr_hi = w_r.astype(BF16)
    gpad = lambda g: jnp.take(jnp.concatenate([g, jnp.zeros((1,), F32)]), jnp.asarray(_head_lane_source())).reshape(1, LANE)
    w_pm = jnp.zeros((MLA_HEADS, HEAD_PAD, d), F32).at[:, :V_DIM].set(a["w_pm"][l].reshape(MLA_HEADS, V_DIM, d))
    return {
        "norm1_g": a["norm1_g"][l].reshape(1, d), "norm2_g": a["norm2_g"][l].reshape(1, d),
        "w_in": w_in.astype(BF16),
        "lru_conv": _halves(a["lru_conv"][l]),
        "lru_wa": _blockdiag_halves(a["lru_wa"][l]).astype(BF16),
        "lru_wx": _blockdiag_halves(a["lru_wx"][l]).astype(BF16),
        "lru_ba": _halves(a["lru_ba"][l])[:, :, None, :], "lru_bx": _halves(a["lru_bx"][l])[:, :, None, :],
        "lru_lam": _halves(a["lru_lam"][l])[:, :, None, :],
        "sc_conv": _halves(a["sc_conv"][l]),
        "g_qa": a["g_qa"][l].reshape(1, Q_LORA), "g_kva": a["g_kva"][l].reshape(1, KV_LORA),
        "wq": _place_head_dims(a["w_qb"][l], QK_DIM).astype(BF16),
        "wk": _place_head_dims(a["w_kvb"][l], NOPE).astype(BF16),        "wv": _pad_heads(a["w_kvb"][l], NOPE, NOPE + V_DIM).astype(BF16),
        "gq": gpad(a["g_qn"][l]), "gk": gpad(a["g_kn"][l]),
        "w_gate": a["w_gate"][l].astype(BF16),
        "w_pf": a["w_pf"][l].astype(BF16), "w_pl": a["w_pl"][l].astype(BF16), "w_ps": a["w_ps"][l].astype(BF16),
        "w_pm": w_pm.reshape(HP, d).astype(BF16), "w_out": a["w_out"][l].astype(BF16),
        "wr_hi": wr_hi, "wr_lo": (w_r - wr_hi.astype(F32)).astype(BF16), "b_r": b_r,
        "w1": a["w1"], "w3": a["w3"], "w2": a["w2"], "layer": l,
    }


def _trunk_layer(x, mod, p, consts, h0, cache, cfg):
    s, l = cfg["s"], cfg["l"]
    z = _front(x, mod, p["norm1_g"], p["w_in"], cfg["tm_front"])
    z3 = z.reshape(s, l, Z_W)
    q, k, v, ckv = _qkv(z, consts["rope"], p, l, cfg["tm_qkv"])
    yf = _fourier(z3, consts["dft"], cfg["nb"], cfg["tq_f"])
    yl, ys, fin = _lru(z3, h0, p)
    o = _attn(q.reshape(s, l, HP), k.reshape(s, l, HP), v.reshape(s, l, HP), cache, cfg["hp"], cfg["tq_a"])
    t = s * l
    x1, h2, comb = _back(x, mod, yf.reshape(t, FNET_W), yl.reshape(t, LRU_W), ys.reshape(t, SC_W),
                         o.reshape(t, HP), p, cfg["tm_back"])
    x2 = _moe(h2, comb, x1, mod, p, cfg["tm_moe"], cfg["tm_comb"])
    return x2, z3, ckv, fin


def kernel(x_prompt, x_sample, cache_ckv, cache_krope, state_rglru, c, c_ctx, norm1_g, norm2_g, w_ada, b_ada, w_in, lru_conv, lru_wa, lru_ba, lru_wx, lru_bx, lru_lam, sc_conv, g_qa, w_qb, g_kva, w_kvb, g_qn, g_kn, w_pf, w_pl, w_ps, w_pm, w_gate, w_out, w_gr, b_gr, w_er, b_er, w1, w3, w2):
    a = dict(norm1_g=norm1_g, norm2_g=norm2_g, w_in=w_in, lru_conv=lru_conv, lru_wa=lru_wa, lru_ba=lru_ba,
             lru_wx=lru_wx, lru_bx=lru_bx, lru_lam=lru_lam, sc_conv=sc_conv, g_qa=g_qa, w_qb=w_qb, g_kva=g_kva,
             w_kvb=w_kvb, g_qn=g_qn, g_kn=g_kn, w_pf=w_pf, w_pl=w_pl, w_ps=w_ps, w_pm=w_pm, w_gate=w_gate,
             w_out=w_out, w_gr=w_gr, b_gr=b_gr, w_er=w_er, b_er=b_er, w1=w1, w3=w3, w2=w2)
    bc, lc, d = x_prompt.shape
    bl, ll, _ = x_sample.shape
    past = cache_ckv.shape[2]

    cond8 = jnp.zeros((SUB, d), F32).at[0].set(c_ctx).at[1:1 + bl].set(c)
    mod_all = _ada(cond8, w_ada, b_ada)

    cc, sc = _channel_dft()
    ctx_consts = {"dft": (cc, sc) + _position_dft(lc), "rope": _rope_tables(lc, False)}
    lat_consts = {"dft": (cc, sc) + _position_dft(ll), "rope": _rope_tables(ll, True)}
    cache_tabs = _rope_tables(bl * past, False)
    ctx_cfg = dict(s=bc, l=lc, tm_front=512, tm_qkv=512, nb=8, tq_f=lc, hp=MLA_HEADS, tq_a=lc, tm_back=512, tm_moe=512, tm_comb=512)
    lat_cfg = dict(s=bl, l=ll, tm_front=512, tm_qkv=512, nb=bl, tq_f=512, hp=MLA_HEADS, tq_a=256, tm_back=512, tm_moe=512, tm_comb=512)

    xp = x_prompt.reshape(bc * lc, d)
    xs = x_sample.reshape(bl * ll, d)
    h0_ctx = jnp.zeros((bc, 2, LRU_W), F32)
    ckv_list, kr_list, lru_list = [], [], []
    for l in range(DEPTH):
        p = _layer_params(l, a)
        mod_ctx = mod_all[l, 0:1].reshape(1, 1, 6 * d)
        mod_lat = mod_all[l, 1:1 + bl].reshape(bl, 1, 6 * d)
        xp, z3, ckv, fin = _trunk_layer(xp, mod_ctx, p, ctx_consts, h0_ctx, None, ctx_cfg)
        ckv_list.append(ckv.reshape(bc, lc, KV_LORA))
        kr_list.append(_take_rope_key(z3[:, :, Z_KR:]))
        lru_list.append(fin)
        kr_tile = _place_rope_key(cache_krope[:, l].reshape(bl * past, ROPE))
        kc, vc = _kvcache(cache_ckv[:, l].reshape(bl * past, KV_LORA), kr_tile, cache_tabs, p)
        cache = (kc.reshape(bl, past, HP), vc.reshape(bl, past, HP))
        xs, _, _, _ = _trunk_layer(xs, mod_lat, p, lat_consts, state_rglru[:, l], cache, lat_cfg)
    return (xp.reshape(bc, lc, d), xs.reshape(bl, ll, d),
            jnp.stack(ckv_list, axis=1), jnp.stack(kr_list, axis=1), jnp.stack(lru_list, axis=1))
```

```python
import functools
import math

import numpy as np
import jax
import jax.numpy as jnp
from jax import lax
from jax.experimental import pallas as pl
from jax.experimental.pallas import tpu as pltpu
from jax.experimental.pallas import tpu_sc as plsc

F32 = jnp.float32
BF16 = jnp.bfloat16

D_MODEL = 1024
DEPTH = 2
GRID_W = 64
EPS = 1e-6
FNET_W = 256
FNET_GROUPS = 4
LRU_W = 256
LRU_BLOCKS = 4
LRU_C = 8.0
SC_W = 256
MLA_HEADS = 8
Q_LORA = 384
KV_LORA = 256
NOPE = 64
ROPE = 32
V_DIM = 64
QK_DIM = NOPE + ROPE
N_FREQ = ROPE // 4
ROPE_BASE = 10000.0
ATTN_SCALE = QK_DIM ** -0.5
LOG2E = math.log2(math.e)
N_GROUPS = 4
EXP_PER_GROUP = 8
N_EXPERTS = N_GROUPS * EXP_PER_GROUP
D_EXPERT = 256

LANE = 128
SUB = 8
HEAD_PAD = LANE
HP = MLA_HEADS * HEAD_PAD
ROPE_X1 = 0
ROPE_X2 = LANE // 2
DEN_LANE = V_DIM
Z_QKV = 6 * 256
Z_KR = Z_QKV + Q_LORA + KV_LORA
Z_W = Z_KR + LANE
QKV_W = Q_LORA + KV_LORA + LANE
VMEM_LIMIT = 52 * 1024 * 1024
PACK_W = D_MODEL // 2
HI_MASK = -65536
RT_ID0, RT_ID1, RT_W0, RT_W1 = 0, 1, 2, 3
RT_ROWS = 48
TOP_K = 2
DFT_SPLIT = 64
ATTN_SUB = 256
SC_CORES = 2
SC_SUBCORES = 16
SC_CHUNK = 128


def _cparams(sem):
    return pltpu.CompilerParams(dimension_semantics=sem, vmem_limit_bytes=VMEM_LIMIT)


def _bdot(a, b):
    return jnp.dot(a, b, preferred_element_type=F32)


def _rms(x, g):
    return x * lax.rsqrt(jnp.mean(x * x, axis=-1, keepdims=True) + EPS) * g


def _ada_kernel(c_ref, w_ref, b_ref, o_ref):
    c = c_ref[...]
    s = (c * jax.nn.sigmoid(c)).astype(BF16)
    o_ref[...] = _bdot(s, w_ref[...].astype(BF16)) + b_ref[...]


def _ada(cond8, w_ada, b_ada):
    nblk = 6 * D_MODEL // 1024
    return pl.pallas_call(
        _ada_kernel,
        grid=(DEPTH, nblk),
        in_specs=[
            pl.BlockSpec((SUB, D_MODEL), lambda l, n: (0, 0)),
            pl.BlockSpec((None, D_MODEL, 1024), lambda l, n: (l, 0, n)),
            pl.BlockSpec((None, 1, 1024), lambda l, n: (l, 0, n)),
        ],
        out_specs=pl.BlockSpec((None, SUB, 1024), lambda l, n: (l, 0, n)),
        out_shape=jax.ShapeDtypeStruct((DEPTH, SUB, 6 * D_MODEL), F32),
        compiler_params=_cparams(("arbitrary", "arbitrary")),
        name="ada",
    )(cond8, w_ada, b_ada.reshape(DEPTH, 1, 6 * D_MODEL))


def _front_kernel(x_ref, mod_ref, g_ref, w_ref, z_ref):
    x = x_ref[...]
    sh = mod_ref[:, 0:D_MODEL]
    sc = mod_ref[:, D_MODEL:2 * D_MODEL]
    h = _rms(x, g_ref[...]) * (1.0 + sc) + sh
    z_ref[...] = _bdot(h.astype(BF16), w_ref[...])


def _front(x, mod, g1, w_in, tm):
    t = x.shape[0]
    per_mod = t // (mod.shape[0] * tm)
    return pl.pallas_call(
        _front_kernel,
        grid=(t // tm,),
        in_specs=[
            pl.BlockSpec((tm, D_MODEL), lambda i: (i, 0)),
            pl.BlockSpec((None, 1, 6 * D_MODEL), lambda i: (i // per_mod, 0, 0)),
            pl.BlockSpec((1, D_MODEL), lambda i: (0, 0)),
            pl.BlockSpec((D_MODEL, Z_W), lambda i: (0, 0)),
        ],
        out_specs=pl.BlockSpec((tm, Z_W), lambda i: (i, 0)),
        out_shape=jax.ShapeDtypeStruct((t, Z_W), F32),
        compiler_params=_cparams(("arbitrary",)),
        name="front",
    )(x, mod, g1, w_in)


def _rope(x, cos, sinr):
    return x * cos + pltpu.roll(x, LANE // 2, 1) * sinr


def _head_norm(x, g):
    ss = jnp.sum(x * x, axis=-1, keepdims=True) * (1.0 / QK_DIM)
    return x * lax.rsqrt(ss + EPS) * g


def _build_kv(ckv, kr_tile, cos, sinr, wk_ref, wv_ref, gk, k_ref, v_ref):
    cb = ckv.astype(BF16)
    kn = _bdot(cb, wk_ref[...])
    for h in range(MLA_HEADS):
        sl = slice(h * HEAD_PAD, (h + 1) * HEAD_PAD)
        k_ref[:, sl] = _rope(_head_norm(kn[:, sl] + kr_tile, gk), cos, sinr).astype(BF16)
    v = _bdot(cb, wv_ref[...])
    lane = lax.broadcasted_iota(jnp.int32, v.shape, 1)
    v_ref[...] = jnp.where((lane & (HEAD_PAD - 1)) == DEN_LANE, 1.0, v).astype(BF16)


def _qkv_kernel(z_ref, cos_ref, sinr_ref, gqa_ref, wq_ref, gkva_ref, wk_ref, wv_ref,
                gq_ref, gk_ref, q_ref, k_ref, v_ref, ckv_ref):
    cos, sinr = cos_ref[...], sinr_ref[...]
    q_c = z_ref[:, 0:Q_LORA]
    kv_c = z_ref[:, Q_LORA:Q_LORA + KV_LORA]
    kr_tile = z_ref[:, Q_LORA + KV_LORA:QKV_W]
    qf = _bdot(_rms(q_c, gqa_ref[...]).astype(BF16), wq_ref[...])
    gq = gq_ref[...] * (ATTN_SCALE * LOG2E)
    for h in range(MLA_HEADS):
        sl = slice(h * HEAD_PAD, (h + 1) * HEAD_PAD)
        q_ref[:, sl] = _rope(_head_norm(qf[:, sl], gq), cos, sinr).astype(BF16)
    ckv = _rms(kv_c, gkva_ref[...])
    ckv_ref[...] = ckv
    _build_kv(ckv, kr_tile, cos, sinr, wk_ref, wv_ref, gk_ref[...], k_ref, v_ref)


def _qkv(z, tabs, p, seq_len, tm):
    t = z.shape[0]
    if tm > seq_len:
        tabs = [jnp.tile(a, (tm // seq_len, 1)) for a in tabs]
    per_seq = max(seq_len // tm, 1)
    tab_spec = pl.BlockSpec((tm, LANE), lambda i: (i % per_seq, 0))
    full = lambda shape: pl.BlockSpec(shape, lambda i: (0,) * len(shape))
    return pl.pallas_call(
        _qkv_kernel,
        grid=(t // tm,),
        in_specs=[
            pl.BlockSpec((tm, QKV_W), lambda i: (i, Z_QKV // QKV_W)),
            tab_spec, tab_spec,
            full((1, Q_LORA)), full((Q_LORA, HP)), full((1, KV_LORA)),
            full((KV_LORA, HP)), full((KV_LORA, HP)), full((1, LANE)), full((1, LANE)),
        ],
        out_specs=[
            pl.BlockSpec((tm, HP), lambda i: (i, 0)),
            pl.BlockSpec((tm, HP), lambda i: (i, 0)),
            pl.BlockSpec((tm, HP), lambda i: (i, 0)),
            pl.BlockSpec((tm, KV_LORA), lambda i: (i, 0)),
        ],
        out_shape=[
            jax.ShapeDtypeStruct((t, HP), BF16),
            jax.ShapeDtypeStruct((t, HP), BF16),
            jax.ShapeDtypeStruct((t, HP), BF16),
            jax.ShapeDtypeStruct((t, KV_LORA), F32),
        ],
        compiler_params=_cparams(("arbitrary",)),
        name="qkv",
    )(z, tabs[0], tabs[1], p["g_qa"], p["wq"], p["g_kva"], p["wk"], p["wv"], p["gq"], p["gk"])


def _kvcache_kernel(ckv_ref, kr_ref, cos_ref, sinr_ref, wk_ref, wv_ref, gk_ref, k_ref, v_ref):
    _build_kv(ckv_ref[...], kr_ref[...], cos_ref[...], sinr_ref[...], wk_ref, wv_ref, gk_ref[...], k_ref, v_ref)


def _kvcache(ckv, kr_tile, tabs, p):
    t = ckv.shape[0]
    return pl.pallas_call(
        _kvcache_kernel,
        out_shape=[jax.ShapeDtypeStruct((t, HP), BF16), jax.ShapeDtypeStruct((t, HP), BF16)],
        compiler_params=_cparams(None),
        name="kvcache",
    )(ckv, kr_tile, tabs[0], tabs[1], p["wk"], p["wv"], p["gk"])


def _fourier_kernel(nb, u_ref, cc_ref, sc_ref, cl_ref, sl_ref, o_ref, a_scr, b_scr):
    @pl.when(pl.program_id(1) == 0)
    def _():
        for b in range(nb):
            u = u_ref[b].astype(BF16)
            a_scr[:, b * FNET_W:(b + 1) * FNET_W] = _bdot(u, cc_ref[...]).astype(BF16)
            b_scr[:, b * FNET_W:(b + 1) * FNET_W] = _bdot(u, sc_ref[...]).astype(BF16)

    y = _bdot(cl_ref[...], a_scr[...]) - _bdot(sl_ref[...], b_scr[...])
    for b in range(nb):
        o_ref[b] = y[:, b * FNET_W:(b + 1) * FNET_W].astype(BF16)


def _fourier_split_kernel(nb, u_ref, cc_ref, sc_ref, ca_ref, sa_ref, cb_ref, sb_ref, o_ref,
                          a_scr, b_scr, cl_scr, sl_scr):
    @pl.when(pl.program_id(1) == 0)
    def _():
        for b in range(nb):
            u = u_ref[b].astype(BF16)
            a_scr[:, b * FNET_W:(b + 1) * FNET_W] = _bdot(u, cc_ref[...]).astype(BF16)
            b_scr[:, b * FNET_W:(b + 1) * FNET_W] = _bdot(u, sc_ref[...]).astype(BF16)

    cb, sb = cb_ref[...], sb_ref[...]
    for j in range(ca_ref.shape[0]):
        ca, sa = ca_ref[j:j + 1, :], sa_ref[j:j + 1, :]
        cl_scr[j * DFT_SPLIT:(j + 1) * DFT_SPLIT, :] = (ca * cb - sa * sb).astype(BF16)
        sl_scr[j * DFT_SPLIT:(j + 1) * DFT_SPLIT, :] = (sa * cb + ca * sb).astype(BF16)
    y = _bdot(cl_scr[...], a_scr[...]) - _bdot(sl_scr[...], b_scr[...])
    for b in range(nb):
        o_ref[b] = y[:, b * FNET_W:(b + 1) * FNET_W].astype(BF16)


def _fourier_split(z3, mats, nb, tq):
    s, l, _ = z3.shape
    cc, sc, ca, sa, cb, sb = mats
    full = lambda a: pl.BlockSpec(a.shape, lambda i, r: (0, 0))
    rows = pl.BlockSpec((tq // DFT_SPLIT, l), lambda i, r: (r, 0))
    return pl.pallas_call(
        functools.partial(_fourier_split_kernel, nb),
        grid=(s // nb, l // tq),
        in_specs=[pl.BlockSpec((nb, l, FNET_W), lambda i, r: (i, 0, 0)), full(cc), full(sc), rows, rows,
                  full(cb), full(sb)],
        out_specs=pl.BlockSpec((nb, tq, FNET_W), lambda i, r: (i, r, 0)),
        out_shape=jax.ShapeDtypeStruct((s, l, FNET_W), BF16),
        scratch_shapes=[pltpu.VMEM((l, nb * FNET_W), BF16), pltpu.VMEM((l, nb * FNET_W), BF16),
                        pltpu.VMEM((tq, l), BF16), pltpu.VMEM((tq, l), BF16)],
        compiler_params=_cparams(("arbitrary", "arbitrary")),
        name="fourier_split",
    )(z3, cc, sc, ca, sa, cb, sb)


def _fourier(z3, mats, nb, tq):
    s, l, _ = z3.shape
    if len(mats) == 6:
        return _fourier_split(z3, mats, nb, tq)
    cc, sc, cl, sl = mats
    return pl.pallas_call(
        functools.partial(_fourier_kernel, nb),
        grid=(s // nb, l // tq),
        in_specs=[
            pl.BlockSpec((nb, l, FNET_W), lambda i, r: (i, 0, 0)),
            pl.BlockSpec((FNET_W, FNET_W), lambda i, r: (0, 0)),
            pl.BlockSpec((FNET_W, FNET_W), lambda i, r: (0, 0)),
            pl.BlockSpec((tq, l), lambda i, r: (r, 0)),
            pl.BlockSpec((tq, l), lambda i, r: (r, 0)),
        ],
        out_specs=pl.BlockSpec((nb, tq, FNET_W), lambda i, r: (i, r, 0)),
        out_shape=jax.ShapeDtypeStruct((s, l, FNET_W), BF16),
        scratch_shapes=[pltpu.VMEM((l, nb * FNET_W), BF16), pltpu.VMEM((l, nb * FNET_W), BF16)],
        compiler_params=_cparams(("arbitrary", "arbitrary")),
        name="fourier",
    )(z3, cc, sc, cl, sl)


def _load_ext(ref, c, rows, nchunks):
    t0 = pl.multiple_of(c * rows, rows)
    main = ref[pl.ds(t0, rows), :]
    lo = pl.multiple_of(jnp.maximum(t0 - SUB, 0), SUB)
    hi = pl.multiple_of(jnp.minimum(t0 + rows, (nchunks - 1) * rows + rows - SUB), SUB)
    prev = jnp.where(c > 0, ref[pl.ds(lo, SUB), :], 0.0)
    nxt = jnp.where(c < nchunks - 1, ref[pl.ds(hi, SUB), :], 0.0)
    return jnp.concatenate([prev, main, nxt], axis=0)


def _shifted(ext, off, rows):
    n = ext.shape[0]
    r = ext if off == 0 else pltpu.roll(ext, (-off) % n, 0)
    return r[SUB:SUB + rows]


def _gelu_tanh(x):
    return 0.5 * x * (1.0 + jnp.tanh(math.sqrt(2.0 / math.pi) * (x + 0.044715 * (x * x * x))))


def _lru_gates(xc, d, wa_ref, wx_ref, ba_ref, bx_ref, lam_ref):
    xb = xc.astype(BF16)
    r = jax.nn.sigmoid(_bdot(xb, wa_ref[d]) + ba_ref[d])
    i = jax.nn.sigmoid(_bdot(xb, wx_ref[d]) + bx_ref[d])
    nl = -lam_ref[d]
    softplus = jnp.maximum(nl, 0.0) + jnp.log1p(jnp.exp(-jnp.abs(nl)))
    la = (-LRU_C) * r * softplus
    a = jnp.exp(la)
    one_m_a2 = -jnp.tanh(la) * (a * a + 1.0)
    b = jnp.sqrt(one_m_a2) * (i * xc)
    return a, b


def _group_scan(a, b, reverse):
    rows = a.shape[0]
    rm = lax.broadcasted_iota(jnp.int32, a.shape, 0) & (SUB - 1)
    for s in (1, 2, 4):
        if reverse:
            sh, m = rows - s, rm + s <= SUB - 1
        else:
            sh, m = s, rm >= s
        a_sh = pltpu.roll(a, sh, 0)
        b_sh = pltpu.roll(b, sh, 0)
        b = jnp.where(m, a * b_sh + b, b)
        a = jnp.where(m, a * a_sh, a)
    return a, b


def _carry_scan(a, b, carry, reverse):
    ng = a.shape[0] // SUB
    out = [None] * ng
    order = range(ng - 1, -1, -1) if reverse else range(ng)
    for g in order:
        hg = a[g * SUB:(g + 1) * SUB] * carry + b[g * SUB:(g + 1) * SUB]
        out[g] = hg
        carry = hg[0:1] if reverse else hg[SUB - 1:SUB]
    return jnp.concatenate(out, axis=0), carry


def _lru_kernel(rows, nchunks, xl_ref, gl_ref, bs_ref, cs_ref, xs_ref, h0_ref, cw_ref, wa_ref, wx_ref,
                ba_ref, bx_ref, lam_ref, sw_ref, yl_ref, ys_ref, fin_ref, xc_scr, hf_scr, hb_scr):
    cw = cw_ref[...]
    sw = sw_ref[...]
    chunk = lambda c: pl.ds(pl.multiple_of(c * rows, rows), rows)

    def conv_body(c, carry):
        ext = _load_ext(xl_ref, c, rows, nchunks)
        xc = _shifted(ext, -2, rows) * cw[0:1]
        for k in range(1, 4):
            xc = xc + _shifted(ext, k - 2, rows) * cw[k:k + 1]
        xc_scr[chunk(c), :] = xc
        return carry

    lax.fori_loop(0, nchunks, conv_body, 0)

    def direction(c, d, carry, h_scr):
        a, b = _lru_gates(xc_scr[chunk(c), :], d, wa_ref, wx_ref, ba_ref, bx_ref, lam_ref)
        a, b = _group_scan(a, b, d == 1)
        h, carry = _carry_scan(a, b, carry, d == 1)
        h_scr[chunk(c), :] = h
        return carry

    def scan_body(j, carries):
        cf, cb = carries
        return direction(j, 0, cf, hf_scr), direction(nchunks - 1 - j, 1, cb, hb_scr)

    cf, cb = lax.fori_loop(0, nchunks, scan_body, (h0_ref[0:1, :], h0_ref[1:2, :]))
    fin_ref[0:1, :] = cf
    fin_ref[1:2, :] = cb

    def out_body(c, carry):
        y = (hf_scr[chunk(c), :] + hb_scr[chunk(c), :]) * _gelu_tanh(gl_ref[chunk(c), :])
        yl_ref[chunk(c), :] = y.astype(BF16)
        ext = _load_ext(cs_ref, c, rows, nchunks) * _load_ext(xs_ref, c, rows, nchunks)
        acc = _shifted(ext, -1, rows) * sw[0:1]
        acc = acc + _shifted(ext, 0, rows) * sw[1:2]
        acc = acc + _shifted(ext, 1, rows) * sw[2:3]
        ys_ref[chunk(c), :] = (bs_ref[chunk(c), :] * acc).astype(BF16)
        return carry

    lax.fori_loop(0, nchunks, out_body, 0)


def _lru(z3, h0, p):
    s, l, _ = z3.shape
    rows = min(l, 256)
    nchunks = l // rows
    half = lambda blk: pl.BlockSpec((None, l, LANE), lambda i, j: (i, 0, blk + j))
    wspec = lambda shape: pl.BlockSpec((None,) + shape, lambda i, j: (j,) + (0,) * len(shape))
    seq_half = pl.BlockSpec((None, l, LANE), lambda i, j: (i, 0, j))
    return pl.pallas_call(
        functools.partial(_lru_kernel, rows, nchunks),
        grid=(s, 2),
        in_specs=[
            half(2), half(4), half(6), half(8), half(10),
            pl.BlockSpec((None, 2, LANE), lambda i, j: (i, 0, j)),
            wspec((4, LANE)), wspec((2, LANE, LANE)), wspec((2, LANE, LANE)),
            wspec((2, 1, LANE)), wspec((2, 1, LANE)), wspec((2, 1, LANE)), wspec((3, LANE)),
        ],
        out_specs=[seq_half, seq_half, pl.BlockSpec((None, 2, LANE), lambda i, j: (i, 0, j))],
        out_shape=[jax.ShapeDtypeStruct((s, l, LRU_W), BF16), jax.ShapeDtypeStruct((s, l, SC_W), BF16),
                   jax.ShapeDtypeStruct((s, 2, LRU_W), F32)],
        scratch_shapes=[pltpu.VMEM((l, LANE), F32), pltpu.VMEM((l, LANE), F32), pltpu.VMEM((l, LANE), F32)],
        compiler_params=_cparams(("arbitrary", "arbitrary")),
        name="lru",
    )(z3, z3, z3, z3, z3, h0, p["lru_conv"], p["lru_wa"], p["lru_wx"], p["lru_ba"], p["lru_bx"], p["lru_lam"],
      p["sc_conv"])


def _attn_kernel(hp, sub, has_cache, *refs):
    if has_cache:
        q_ref, k_ref, v_ref, kc_ref, vc_ref, o_ref, s_scr, p_scr, m_scr, sc_scr, pc_scr = refs
    else:
        q_ref, k_ref, v_ref, o_ref, s_scr, p_scr, m_scr = refs
    nt = (((1,), (1,)), ((), ()))
    items = [(slice(r * sub, (r + 1) * sub), slice(h * HEAD_PAD, (h + 1) * HEAD_PAD))
             for r in range(q_ref.shape[0] // sub) for h in range(hp)]
    n = len(items)

    def scores(i):
        rows, head = items[i]
        q = q_ref[rows, head]
        s = lax.dot_general(q, k_ref[:, head], nt, preferred_element_type=F32)
        m = jnp.max(s, axis=-1, keepdims=True)
        s_scr[i % 2] = s
        if has_cache:
            sc = lax.dot_general(q, kc_ref[:, head], nt, preferred_element_type=F32)
            m = jnp.maximum(m, jnp.max(sc, axis=-1, keepdims=True))
            sc_scr[i % 2] = sc
        m_scr[i % 2] = jnp.broadcast_to(m, m_scr.shape[1:])

    def probs(i):
        m = m_scr[i % 2][:, 0:1]
        p_scr[i % 2] = jnp.exp2(s_scr[i % 2] - m).astype(BF16)
        if has_cache:
            pc_scr[i % 2] = jnp.exp2(sc_scr[i % 2] - m).astype(BF16)

    def weighted_values(i):
        rows, head = items[i]
        o = _bdot(p_scr[i % 2], v_ref[:, head])
        if has_cache:
            o = o + _bdot(pc_scr[i % 2], vc_ref[:, head])
        o_ref[rows, head] = (o / o[:, DEN_LANE:DEN_LANE + 1]).astype(BF16)

    scores(0)
    for i in range(n):
        if i + 1 < n:
            scores(i + 1)
        if i > 0:
            weighted_values(i - 1)
        probs(i)
    weighted_values(n - 1)


def _attn_short_kernel(hp, q_ref, k_ref, v_ref, o_ref):
    nt = (((1,), (1,)), ((), ()))
    heads = [slice(h * HEAD_PAD, (h + 1) * HEAD_PAD) for h in range(hp)]
    scores = [lax.dot_general(q_ref[:, hd], k_ref[:, hd], nt, preferred_element_type=F32) for hd in heads]
    probs = [jnp.exp2(s - jnp.max(s, axis=-1, keepdims=True)).astype(BF16) for s in scores]
    for hd, p in zip(heads, probs):
        o = _bdot(p, v_ref[:, hd])
        o_ref[:, hd] = (o / o[:, DEN_LANE:DEN_LANE + 1]).astype(BF16)


def _attn(q, k, v, cache, hp, tq):
    s, l, _ = q.shape
    w = hp * HEAD_PAD
    qspec = pl.BlockSpec((None, tq, w), lambda i, h, r: (i, r, h))
    if cache is None and tq == l:
        kspec = pl.BlockSpec((None, l, w), lambda i, h, r: (i, 0, h))
        return pl.pallas_call(
            functools.partial(_attn_short_kernel, hp),
            grid=(s, MLA_HEADS // hp, 1),
            in_specs=[qspec, kspec, kspec],
            out_specs=qspec,
            out_shape=jax.ShapeDtypeStruct((s, l, HP), BF16),
            compiler_params=_cparams(("arbitrary", "arbitrary", "arbitrary")),
            name="attn_short",
        )(q, k, v)
    kv_mode = dict(pipeline_mode=pl.Buffered(1))
    kspec = pl.BlockSpec((None, l, w), lambda i, h, r: (i, 0, h), **kv_mode)
    in_specs = [qspec, kspec, kspec]
    args = [q, k, v]
    if cache is not None:
        lc = cache[0].shape[1]
        cspec = pl.BlockSpec((None, lc, w), lambda i, h, r: (i, 0, h), **kv_mode)
        in_specs += [cspec, cspec]
        args += list(cache)
    sub = min(tq, ATTN_SUB)
    scratch = [pltpu.VMEM((2, sub, l), F32), pltpu.VMEM((2, sub, l), BF16), pltpu.VMEM((2, sub, LANE), F32)]
    if cache is not None:
        scratch += [pltpu.VMEM((2, sub, lc), F32), pltpu.VMEM((2, sub, lc), BF16)]
    return pl.pallas_call(
        functools.partial(_attn_kernel, hp, sub, cache is not None),
        grid=(s, MLA_HEADS // hp, l // tq),
        in_specs=in_specs,
        out_specs=qspec,
        out_shape=jax.ShapeDtypeStruct((s, l, HP), BF16),
        scratch_shapes=scratch,
        compiler_params=_cparams(("arbitrary", "arbitrary", "arbitrary")),
        name="attn",
    )(*args)


def _route(lt):
    row = lax.broadcasted_iota(jnp.int32, (SUB, lt.shape[1]), 0).astype(F32)
    neg = -jnp.inf
    gl = lt[0:SUB]
    g_ok = row < N_GROUPS
    glm = jnp.where(g_ok, gl, neg)
    gmax = jnp.max(glm, axis=0, keepdims=True)
    gsel = jnp.min(jnp.where(glm == gmax, row, float(SUB)), axis=0, keepdims=True)
    g_w = 1.0 / jnp.sum(jnp.where(g_ok, jnp.exp(gl - gmax), 0.0), axis=0, keepdims=True)
    es = lt[SUB:2 * SUB]
    for g in range(1, N_GROUPS):
        es = jnp.where(gsel == float(g), lt[(g + 1) * SUB:(g + 2) * SUB], es)
    m1 = jnp.max(es, axis=0, keepdims=True)
    i1 = jnp.min(jnp.where(es == m1, row, float(SUB)), axis=0, keepdims=True)
    es2 = jnp.where(row == i1, neg, es)
    m2 = jnp.max(es2, axis=0, keepdims=True)
    i2 = jnp.min(jnp.where(es2 == m2, row, float(SUB)), axis=0, keepdims=True)
    e2 = jnp.exp(m2 - m1)
    inv = g_w / (1.0 + e2)
    base = gsel * EXP_PER_GROUP
    return (jnp.where(row == RT_ID0, base + i1, 0.0) + jnp.where(row == RT_ID1, base + i2, 0.0)
            + jnp.where(row == RT_W0, inv, 0.0) + jnp.where(row == RT_W1, inv * e2, 0.0))


def _pack_bf16_pairs(x):
    k = x.shape[1] // 2
    bits = lax.bitcast_convert_type(x.astype(BF16).astype(F32), jnp.int32)
    return lax.shift_right_logical(bits[:, :k], 16) | (bits[:, k:] & HI_MASK)


def _unpack_bf16_pairs(w):
    lo = lax.bitcast_convert_type(lax.shift_left(w, 16), F32).astype(BF16)
    hi = lax.bitcast_convert_type(w & HI_MASK, F32).astype(BF16)
    return lo, hi


def _back_kernel(x_ref, mod_ref, g1_ref, g2_ref, yf_ref, yl_ref, ys_ref, o_ref,
                 wg_ref, wpf_ref, wpl_ref, wps_ref, wpm_ref, wo_ref, wrh_ref, wrl_ref, br_ref,
                 x1_ref, h2_ref, comb_ref):
    d = D_MODEL
    x = x_ref[...]
    sh1, sc1, gt1 = mod_ref[:, 0:d], mod_ref[:, d:2 * d], mod_ref[:, 2 * d:3 * d]
    sh2, sc2 = mod_ref[:, 3 * d:4 * d], mod_ref[:, 4 * d:5 * d]
    hb = (_rms(x, g1_ref[...]) * (1.0 + sc1) + sh1).astype(BF16)
    merged = None
    for j, (y_ref, w_ref) in enumerate(((yf_ref, wpf_ref), (yl_ref, wpl_ref), (ys_ref, wps_ref), (o_ref, wpm_ref))):
        gate = jax.nn.sigmoid(_bdot(hb, wg_ref[:, j * d:(j + 1) * d]))
        term = gate * _bdot(y_ref[...], w_ref[...])
        merged = term if merged is None else merged + term
    x1 = x + gt1 * _bdot(merged.astype(BF16), wo_ref[...])
    x1_ref[...] = x1
    h2 = _rms(x1, g2_ref[...]) * (1.0 + sc2) + sh2
    h2b = h2.astype(BF16)
    h2_ref[...] = _pack_bf16_pairs(h2)
    h2l = (h2 - h2b.astype(F32)).astype(BF16)
    nt = (((1,), (1,)), ((), ()))
    rdot = lambda w_ref, h: lax.dot_general(w_ref[...], h, nt, preferred_element_type=F32)
    logits_t = rdot(wrh_ref, h2b) + rdot(wrh_ref, h2l) + rdot(wrl_ref, h2b) + br_ref[...]
    comb_ref[...] = _route(logits_t)


def _back(x, mod, yf, yl, ys, o, p, tm):
    t = x.shape[0]
    per_mod = t // (mod.shape[0] * tm)
    row = lambda w: pl.BlockSpec((tm, w), lambda i: (i, 0))
    full = lambda a: pl.BlockSpec(a.shape, lambda i: (0,) * a.ndim, pipeline_mode=pl.Buffered(1))
    weights = [p["w_gate"], p["w_pf"], p["w_pl"], p["w_ps"], p["w_pm"], p["w_out"], p["wr_hi"], p["wr_lo"], p["b_r"]]
    return pl.pallas_call(
        _back_kernel,
        grid=(t // tm,),
        in_specs=[
            row(D_MODEL),
            pl.BlockSpec((None, 1, 6 * D_MODEL), lambda i: (i // per_mod, 0, 0)),
            pl.BlockSpec((1, D_MODEL), lambda i: (0, 0)),
            pl.BlockSpec((1, D_MODEL), lambda i: (0, 0)),
            row(FNET_W), row(LRU_W), row(SC_W), row(HP),
        ] + [full(w) for w in weights],
        out_specs=[row(D_MODEL), row(PACK_W), pl.BlockSpec((SUB, tm), lambda i: (0, i))],
        out_shape=[
            jax.ShapeDtypeStruct((t, D_MODEL), F32),
            jax.ShapeDtypeStruct((t, PACK_W), jnp.int32),
            jax.ShapeDtypeStruct((SUB, t), F32),
        ],
        compiler_params=_cparams(("arbitrary",)),
        name="back",
    )(x, mod, p["norm1_g"], p["norm2_g"], yf, yl, ys, o, *weights)


def _dispatch_plan(rt, tm):
    ids = jnp.concatenate([rt[RT_ID0], rt[RT_ID1]]).astype(jnp.int32)
    onehot = (ids[:, None] == jnp.arange(N_EXPERTS, dtype=jnp.int32)[None, :]).astype(jnp.int32)
    csum = jnp.cumsum(onehot, axis=0)
    counts = csum[-1]
    tiles = (counts + tm - 1) // tm
    tile_end = jnp.cumsum(tiles)
    row_start = (tile_end - tiles) * tm
    pos = jnp.sum(onehot * (row_start[None, :] + csum - 1), axis=1)
    n_tiles = ids.shape[0] // tm + N_EXPERTS
    k = jnp.arange(n_tiles, dtype=jnp.int32)
    tile_expert = jnp.minimum(jnp.sum((k[:, None] >= tile_end[None, :]).astype(jnp.int32), axis=1), N_EXPERTS - 1)
    return pos, tile_expert, tile_end[-1:].astype(jnp.int32), n_tiles


def _sc_worker_rows(n_rows):
    workers = SC_CORES * SC_SUBCORES
    per_w = n_rows // workers
    assert per_w * workers == n_rows and per_w % SC_CHUNK == 0
    return per_w


def _sc_mesh():
    return plsc.VectorSubcoreMesh(core_axis_name="c", subcore_axis_name="s")


def _sc_scatter_rows(src, pos, n_out):
    t, w = src.shape
    per_w = _sc_worker_rows(pos.shape[0])

    @functools.partial(
        pl.kernel, mesh=_sc_mesh(), out_type=jax.ShapeDtypeStruct((n_out, w), src.dtype),
        scratch_types=[pltpu.VMEM((SC_CHUNK,), jnp.int32), pltpu.VMEM((SC_CHUNK, w), src.dtype)],
        name="moe_dispatch")
    def run(src_hbm, pos_hbm, out_hbm, idx_v, rows_v):
        wid = lax.axis_index("s") * SC_CORES + lax.axis_index("c")

        @pl.loop(0, per_w // SC_CHUNK)
        def _(i):
            j0 = wid * per_w + i * SC_CHUNK
            pltpu.sync_copy(pos_hbm.at[pl.ds(j0, SC_CHUNK)], idx_v)
            pltpu.sync_copy(src_hbm.at[pl.ds(lax.rem(j0, t), SC_CHUNK)], rows_v)
            pltpu.sync_copy(rows_v, out_hbm.at[idx_v])

    return run(src, pos)


def _sc_gather_rows(src, pos):
    w = src.shape[1]
    n = pos.shape[0]
    per_w = _sc_worker_rows(n)

    @functools.partial(
        pl.kernel, mesh=_sc_mesh(), out_type=jax.ShapeDtypeStruct((n, w), src.dtype),
        scratch_types=[pltpu.VMEM((SC_CHUNK,), jnp.int32), pltpu.VMEM((SC_CHUNK, w), src.dtype)],
        name="moe_return")
    def run(src_hbm, pos_hbm, out_hbm, idx_v, rows_v):
        wid = lax.axis_index("s") * SC_CORES + lax.axis_index("c")

        @pl.loop(0, per_w // SC_CHUNK)
        def _(i):
            j0 = wid * per_w + i * SC_CHUNK
            pltpu.sync_copy(pos_hbm.at[pl.ds(j0, SC_CHUNK)], idx_v)
            pltpu.sync_copy(src_hbm.at[idx_v], rows_v)
            pltpu.sync_copy(rows_v, out_hbm.at[pl.ds(j0, SC_CHUNK)])

    return run(src, pos)


def _experts_kernel(te_ref, nu_ref, x_ref, w1_ref, w3_ref, w2_ref, y_ref):
    half = D_MODEL // 2

    @pl.when(pl.program_id(0) < nu_ref[0])
    def _():
        lo, hi = _unpack_bf16_pairs(x_ref[...])
        a = _bdot(lo, w1_ref[:half, :].astype(BF16)) + _bdot(hi, w1_ref[half:, :].astype(BF16))
        u = _bdot(lo, w3_ref[:half, :].astype(BF16)) + _bdot(hi, w3_ref[half:, :].astype(BF16))
        mid = (a * jax.nn.sigmoid(a)) * u
        y_ref[...] = _pack_bf16_pairs(_bdot(mid.astype(BF16), w2_ref[...].astype(BF16)))


def _experts(xs, tile_expert, n_used, n_tiles, w1, w3, w2, layer, tm):
    wspec = lambda a: pl.BlockSpec((None, None) + a.shape[2:], lambda i, te, nu: (layer, te[i], 0, 0))
    grid_spec = pltpu.PrefetchScalarGridSpec(
        num_scalar_prefetch=2,
        grid=(n_tiles,),
        in_specs=[pl.BlockSpec((tm, PACK_W), lambda i, te, nu: (jnp.minimum(i, nu[0] - 1), 0)),
                  wspec(w1), wspec(w3), wspec(w2)],
        out_specs=pl.BlockSpec((tm, PACK_W), lambda i, te, nu: (jnp.where(i < nu[0], i, n_tiles), 0)),
    )
    return pl.pallas_call(
        _experts_kernel,
        grid_spec=grid_spec,
        out_shape=jax.ShapeDtypeStruct((xs.shape[0] + tm, PACK_W), jnp.int32),
        compiler_params=_cparams(("arbitrary",)),
        name="experts",
    )(tile_expert, n_used, xs, w1, w3, w2)


def _combine_kernel(x1_ref, mod_ref, rt_ref, g0_ref, g1_ref, o_ref):
    rt = rt_ref[...]
    lo0, hi0 = _unpack_bf16_pairs(g0_ref[...])
    lo1, hi1 = _unpack_bf16_pairs(g1_ref[...])
    w0 = rt[:, RT_W0:RT_W0 + 1]
    w1 = rt[:, RT_W1:RT_W1 + 1]
    half = D_MODEL // 2
    gt2 = mod_ref[:, 5 * D_MODEL:6 * D_MODEL]
    o_ref[:, :half] = x1_ref[:, :half] + gt2[:, :half] * (w0 * lo0.astype(F32) + w1 * lo1.astype(F32))
    o_ref[:, half:] = x1_ref[:, half:] + gt2[:, half:] * (w0 * hi0.astype(F32) + w1 * hi1.astype(F32))


def _combine(x1, mod, rt, g, tm):
    t = x1.shape[0]
    per_mod = t // (mod.shape[0] * tm)
    return pl.pallas_call(
        _combine_kernel,
        grid=(t // tm,),
        in_specs=[
            pl.BlockSpec((tm, D_MODEL), lambda i: (i, 0)),
            pl.BlockSpec((None, 1, 6 * D_MODEL), lambda i: (i // per_mod, 0, 0)),
            pl.BlockSpec((tm, LANE), lambda i: (i, 0)),
            pl.BlockSpec((tm, PACK_W), lambda i: (i, 0)),
            pl.BlockSpec((tm, PACK_W), lambda i: (i + t // tm, 0)),
        ],
        out_specs=pl.BlockSpec((tm, D_MODEL), lambda i: (i, 0)),
        out_shape=jax.ShapeDtypeStruct((t, D_MODEL), F32),
        compiler_params=_cparams(("arbitrary",)),
        name="combine",
    )(x1, mod, rt, g, g)


def _moe(h2p, rt, x1, mod, p, tm_e, tm_c):
    pos, tile_expert, n_used, n_tiles = _dispatch_plan(rt, tm_e)
    xs = _sc_scatter_rows(h2p, pos, n_tiles * tm_e)
    ys = _experts(xs, tile_expert, n_used, n_tiles, p["w1"], p["w3"], p["w2"], p["layer"], tm_e)
    g = _sc_gather_rows(ys, pos)
    rt_cols = jnp.pad(rt.T, ((0, 0), (0, LANE - rt.shape[0])))
    return _combine(x1, mod, rt_cols, g, tm_c)


def _channel_dft():
    n = np.arange(FNET_W // FNET_GROUPS)
    ang = 2.0 * np.pi * ((n[:, None] * n[None, :]) % n.size) / n.size
    eye = np.eye(FNET_GROUPS)
    scale = 1.0 / math.sqrt(n.size)
    return (jnp.asarray(np.kron(eye, np.cos(ang) * scale), F32).astype(BF16),
            jnp.asarray(np.kron(eye, np.sin(ang) * scale), F32).astype(BF16))


def _position_dft(l):
    scale = 1.0 / math.sqrt(l)
    if l <= 256:
        n = np.arange(l)
        ang = 2.0 * np.pi * ((n[:, None] * n[None, :]) % l) / l
        return jnp.asarray(np.cos(ang) * scale, F32).astype(BF16), jnp.asarray(np.sin(ang) * scale, F32).astype(BF16)
    m = DFT_SPLIT
    n = np.arange(l)
    ang_a = 2.0 * np.pi * ((np.arange(l // m)[:, None] * m * n[None, :]) % l) / l
    ang_b = 2.0 * np.pi * ((np.arange(m)[:, None] * n[None, :]) % l) / l
    return (jnp.asarray(np.cos(ang_a), F32), jnp.asarray(np.sin(ang_a), F32),
            jnp.asarray(np.cos(ang_b) * scale, F32), jnp.asarray(np.sin(ang_b) * scale, F32))


def _rope_tables(l, rotate):
    cos = np.ones((l, LANE))
    sinr = np.zeros((l, LANE))
    if rotate:
        t = np.arange(l)
        inv = ROPE_BASE ** (-np.arange(N_FREQ) / N_FREQ)
        ang = np.concatenate([(t // GRID_W)[:, None] * inv, (t % GRID_W)[:, None] * inv], axis=-1)
        half = ROPE // 2
        for lo in (ROPE_X1, ROPE_X2):
            cos[:, lo:lo + half] = np.cos(ang)
        sinr[:, ROPE_X1:ROPE_X1 + half] = -np.sin(ang)
        sinr[:, ROPE_X2:ROPE_X2 + half] = np.sin(ang)
    return tuple(jnp.asarray(a, F32) for a in (cos, sinr))


def _head_lane_source():
    half = ROPE // 2
    src = np.full((HEAD_PAD,), QK_DIM, np.int32)
    src[ROPE_X1:ROPE_X1 + half] = NOPE + np.arange(half)
    src[ROPE_X2:ROPE_X2 + half] = NOPE + half + np.arange(half)
    free = [i for i in range(HEAD_PAD) if src[i] == QK_DIM][:NOPE]
    src[free] = np.arange(NOPE)
    return src


def _place_head_dims(w, n_src):
    src = _head_lane_source()
    src = np.where(src < n_src, src, n_src)
    wz = jnp.concatenate([w[..., :n_src], jnp.zeros(w.shape[:-1] + (1,), w.dtype)], axis=-1)
    out = jnp.take(wz, jnp.asarray(src), axis=-1)
    return out.reshape(out.shape[:-2] + (out.shape[-2] * HEAD_PAD,))


def _place_rope_key(kr):
    half = ROPE // 2
    out = jnp.zeros(kr.shape[:-1] + (LANE,), kr.dtype)
    return out.at[..., ROPE_X1:ROPE_X1 + half].set(kr[..., :half]).at[..., ROPE_X2:ROPE_X2 + half].set(kr[..., half:])


def _take_rope_key(tile):
    half = ROPE // 2
    return jnp.concatenate([tile[..., ROPE_X1:ROPE_X1 + half], tile[..., ROPE_X2:ROPE_X2 + half]], axis=-1)


def _pad_heads(w, lo, hi):
    r = w.shape[0]
    part = w[:, :, lo:hi]
    out = jnp.zeros((r, MLA_HEADS, HEAD_PAD), w.dtype).at[:, :, : hi - lo].set(part)
    return out.reshape(r, HP)


def _blockdiag_halves(w):
    bw = LRU_W // LRU_BLOCKS
    out = jnp.zeros((2, 2, LANE, LANE), w.dtype)
    for half in range(2):
        for k in range(2):
            n = 2 * half + k
            out = out.at[half, :, k * bw:(k + 1) * bw, k * bw:(k + 1) * bw].set(w[:, n])
    return out


def _halves(v):
    return jnp.moveaxis(v.reshape(v.shape[:-1] + (2, LANE)), -2, 0)


def _layer_params(l, a):
    d = D_MODEL
    w_in = a["w_in"][l]
    w_in = jnp.concatenate([w_in[:, :Z_KR], _place_rope_key(w_in[:, Z_KR:])], axis=1)
    w_r = jnp.zeros((RT_ROWS, d), F32).at[:N_GROUPS].set(a["w_gr"][l].T).at[SUB:SUB + N_EXPERTS].set(a["w_er"][l].T)
    b_r = jnp.zeros((RT_ROWS, 1), F32).at[:N_GROUPS, 0].set(a["b_gr"][l]).at[SUB:SUB + N_EXPERTS, 0].set(a["b_er"][l])
    wr_hi = w_r.astype(BF16)
    gpad = lambda g: jnp.take(jnp.concatenate([g, jnp.zeros((1,), F32)]), jnp.asarray(_head_lane_source())).reshape(1, LANE)
    w_pm = jnp.zeros((MLA_HEADS, HEAD_PAD, d), F32).at[:, :V_DIM].set(a["w_pm"][l].reshape(MLA_HEADS, V_DIM, d))
    return {
        "norm1_g": a["norm1_g"][l].reshape(1, d), "norm2_g": a["norm2_g"][l].reshape(1, d),
        "w_in": w_in.astype(BF16),
        "lru_conv": _halves(a["lru_conv"][l]),
        "lru_wa": _blockdiag_halves(a["lru_wa"][l]).astype(BF16),
        "lru_wx": _blockdiag_halves(a["lru_wx"][l]).astype(BF16),
        "lru_ba": _halves(a["lru_ba"][l])[:, :, None, :], "lru_bx": _halves(a["lru_bx"][l])[:, :, None, :],
        "lru_lam": _halves(a["lru_lam"][l])[:, :, None, :],
        "sc_conv": _halves(a["sc_conv"][l]),
        "g_qa": a["g_qa"][l].reshape(1, Q_LORA), "g_kva": a["g_kva"][l].reshape(1, KV_LORA),
        "wq": _place_head_dims(a["w_qb"][l], QK_DIM).astype(BF16),
        "wk": _place_head_dims(a["w_kvb"][l], NOPE).astype(BF16),        "wv": _pad_heads(a["w_kvb"][l], NOPE, NOPE + V_DIM).astype(BF16),
        "gq": gpad(a["g_qn"][l]), "gk": gpad(a["g_kn"][l]),
        "w_gate": a["w_gate"][l].astype(BF16),
        "w_pf": a["w_pf"][l].astype(BF16), "w_pl": a["w_pl"][l].astype(BF16), "w_ps": a["w_ps"][l].astype(BF16),
        "w_pm": w_pm.reshape(HP, d).astype(BF16), "w_out": a["w_out"][l].astype(BF16),
        "wr_hi": wr_hi, "wr_lo": (w_r - wr_hi.astype(F32)).astype(BF16), "b_r": b_r,
        "w1": a["w1"], "w3": a["w3"], "w2": a["w2"], "layer": l,
    }


def _trunk_layer(x, mod, p, consts, h0, cache, cfg):
    s, l = cfg["s"], cfg["l"]
    z = _front(x, mod, p["norm1_g"], p["w_in"], cfg["tm_front"])
    z3 = z.reshape(s, l, Z_W)
    q, k, v, ckv = _qkv(z, consts["rope"], p, l, cfg["tm_qkv"])
    yf = _fourier(z3, consts["dft"], cfg["nb"], cfg["tq_f"])
    yl, ys, fin = _lru(z3, h0, p)
    o = _attn(q.reshape(s, l, HP), k.reshape(s, l, HP), v.reshape(s, l, HP), cache, cfg["hp"], cfg["tq_a"])
    t = s * l
    x1, h2, comb = _back(x, mod, yf.reshape(t, FNET_W), yl.reshape(t, LRU_W), ys.reshape(t, SC_W),
                         o.reshape(t, HP), p, cfg["tm_back"])
    x2 = _moe(h2, comb, x1, mod, p, cfg["tm_moe"], cfg["tm_comb"])
    return x2, z3, ckv, fin


def kernel(x_prompt, x_sample, cache_ckv, cache_krope, state_rglru, c, c_ctx, norm1_g, norm2_g, w_ada, b_ada, w_in, lru_conv, lru_wa, lru_ba, lru_wx, lru_bx, lru_lam, sc_conv, g_qa, w_qb, g_kva, w_kvb, g_qn, g_kn, w_pf, w_pl, w_ps, w_pm, w_gate, w_out, w_gr, b_gr, w_er, b_er, w1, w3, w2):
    a = dict(norm1_g=norm1_g, norm2_g=norm2_g, w_in=w_in, lru_conv=lru_conv, lru_wa=lru_wa, lru_ba=lru_ba,
             lru_wx=lru_wx, lru_bx=lru_bx, lru_lam=lru_lam, sc_conv=sc_conv, g_qa=g_qa, w_qb=w_qb, g_kva=g_kva,
             w_kvb=w_kvb, g_qn=g_qn, g_kn=g_kn, w_pf=w_pf, w_pl=w_pl, w_ps=w_ps, w_pm=w_pm, w_gate=w_gate,
             w_out=w_out, w_gr=w_gr, b_gr=b_gr, w_er=w_er, b_er=b_er, w1=w1, w3=w3, w2=w2)
    bc, lc, d = x_prompt.shape
    bl, ll, _ = x_sample.shape
    past = cache_ckv.shape[2]

    cond8 = jnp.zeros((SUB, d), F32).at[0].set(c_ctx).at[1:1 + bl].set(c)
    mod_all = _ada(cond8, w_ada, b_ada)

    cc, sc = _channel_dft()
    ctx_consts = {"dft": (cc, sc) + _position_dft(lc), "rope": _rope_tables(lc, False)}
    lat_consts = {"dft": (cc, sc) + _position_dft(ll), "rope": _rope_tables(ll, True)}
    cache_tabs = _rope_tables(bl * past, False)
    ctx_cfg = dict(s=bc, l=lc, tm_front=512, tm_qkv=512, nb=8, tq_f=lc, hp=MLA_HEADS, tq_a=lc, tm_back=512, tm_moe=1024, tm_comb=512)
    lat_cfg = dict(s=bl, l=ll, tm_front=512, tm_qkv=512, nb=bl, tq_f=512, hp=MLA_HEADS, tq_a=256, tm_back=512, tm_moe=1024, tm_comb=512)

    xp = x_prompt.reshape(bc * lc, d)
    xs = x_sample.reshape(bl * ll, d)
    h0_ctx = jnp.zeros((bc, 2, LRU_W), F32)
    ckv_list, kr_list, lru_list = [], [], []
    for l in range(DEPTH):
        p = _layer_params(l, a)
        mod_ctx = mod_all[l, 0:1].reshape(1, 1, 6 * d)
        mod_lat = mod_all[l, 1:1 + bl].reshape(bl, 1, 6 * d)
        xp, z3, ckv, fin = _trunk_layer(xp, mod_ctx, p, ctx_consts, h0_ctx, None, ctx_cfg)
        ckv_list.append(ckv.reshape(bc, lc, KV_LORA))
        kr_list.append(_take_rope_key(z3[:, :, Z_KR:]))
        lru_list.append(fin)
        kr_tile = _place_rope_key(cache_krope[:, l].reshape(bl * past, ROPE))
        kc, vc = _kvcache(cache_ckv[:, l].reshape(bl * past, KV_LORA), kr_tile, cache_tabs, p)
        cache = (kc.reshape(bl, past, HP), vc.reshape(bl, past, HP))
        xs, _, _, _ = _trunk_layer(xs, mod_lat, p, lat_consts, state_rglru[:, l], cache, lat_cfg)
    return (xp.reshape(bc, lc, d), xs.reshape(bl, ll, d),
            jnp.stack(ckv_list, axis=1), jnp.stack(kr_list, axis=1), jnp.stack(lru_list, axis=1))
```

```python
import functools
import math

import numpy as np
import jax
import jax.numpy as jnp
from jax import lax
from jax.experimental import pallas as pl
from jax.experimental.pallas import tpu as pltpu
from jax.experimental.pallas import tpu_sc as plsc

F32 = jnp.float32
BF16 = jnp.bfloat16

D_MODEL = 1024
DEPTH = 2
GRID_W = 64
EPS = 1e-6
FNET_W = 256
FNET_GROUPS = 4
LRU_W = 256
LRU_BLOCKS = 4
LRU_C = 8.0
SC_W = 256
MLA_HEADS = 8
Q_LORA = 384
KV_LORA = 256
NOPE = 64
ROPE = 32
V_DIM = 64
QK_DIM = NOPE + ROPE
MLA_OUT = MLA_HEADS * V_DIM
N_FREQ = ROPE // 4
ROPE_BASE = 10000.0
ATTN_SCALE = QK_DIM ** -0.5
LOG2E = math.log2(math.e)
N_GROUPS = 4
EXP_PER_GROUP = 8
N_EXPERTS = N_GROUPS * EXP_PER_GROUP
D_EXPERT = 256

LANE = 128
SUB = 8
HEAD_PAD = LANE
HP = MLA_HEADS * HEAD_PAD
ROPE_X1 = 0
ROPE_X2 = LANE // 2
DEN_LANE = V_DIM
Z_QKV = 6 * 256
Z_KR = Z_QKV + Q_LORA + KV_LORA
Z_W = Z_KR + LANE
QKV_W = Q_LORA + KV_LORA + LANE
VMEM_LIMIT = 52 * 1024 * 1024
PACK_W = D_MODEL // 2
HI_MASK = -65536
RT_ID0, RT_ID1, RT_W0, RT_W1 = 0, 1, 2, 3
RT_ROWS = 48
TOP_K = 2
DFT_SPLIT = 64
ATTN_SUB = 256
SC_CORES = 2
SC_SUBCORES = 16
SC_CHUNK = 128


def _cparams(sem):
    return pltpu.CompilerParams(dimension_semantics=sem, vmem_limit_bytes=VMEM_LIMIT)


def _bdot(a, b):
    return jnp.dot(a, b, preferred_element_type=F32)


def _rms(x, g):
    return x * lax.rsqrt(jnp.mean(x * x, axis=-1, keepdims=True) + EPS) * g


def _ada_kernel(c_ref, w_ref, b_ref, o_ref):
    c = c_ref[...]
    s = (c * jax.nn.sigmoid(c)).astype(BF16)
    o_ref[...] = _bdot(s, w_ref[...].astype(BF16)) + b_ref[...]


def _ada(cond8, w_ada, b_ada):
    nblk = 6 * D_MODEL // 1024
    return pl.pallas_call(
        _ada_kernel,
        grid=(DEPTH, nblk),
        in_specs=[
            pl.BlockSpec((SUB, D_MODEL), lambda l, n: (0, 0)),
            pl.BlockSpec((None, D_MODEL, 1024), lambda l, n: (l, 0, n)),
            pl.BlockSpec((None, 1, 1024), lambda l, n: (l, 0, n)),
        ],
        out_specs=pl.BlockSpec((None, SUB, 1024), lambda l, n: (l, 0, n)),
        out_shape=jax.ShapeDtypeStruct((DEPTH, SUB, 6 * D_MODEL), F32),
        compiler_params=_cparams(("arbitrary", "arbitrary")),
        name="ada",
    )(cond8, w_ada, b_ada.reshape(DEPTH, 1, 6 * D_MODEL))


def _front_kernel(x_ref, mod_ref, g_ref, w_ref, z_ref):
    x = x_ref[...]
    sh = mod_ref[:, 0:D_MODEL]
    sc = mod_ref[:, D_MODEL:2 * D_MODEL]
    h = _rms(x, g_ref[...]) * (1.0 + sc) + sh
    z_ref[...] = _bdot(h.astype(BF16), w_ref[...])


def _front(x, mod, g1, w_in, tm):
    t = x.shape[0]
    per_mod = t // (mod.shape[0] * tm)
    return pl.pallas_call(
        _front_kernel,
        grid=(t // tm,),
        in_specs=[
            pl.BlockSpec((tm, D_MODEL), lambda i: (i, 0)),
            pl.BlockSpec((None, 1, 6 * D_MODEL), lambda i: (i // per_mod, 0, 0)),
            pl.BlockSpec((1, D_MODEL), lambda i: (0, 0)),
            pl.BlockSpec((D_MODEL, Z_W), lambda i: (0, 0)),
        ],
        out_specs=pl.BlockSpec((tm, Z_W), lambda i: (i, 0)),
        out_shape=jax.ShapeDtypeStruct((t, Z_W), F32),
        compiler_params=_cparams(("arbitrary",)),
        name="front",
    )(x, mod, g1, w_in)


def _rope(x, cos, sinr):
    return x * cos + pltpu.roll(x, LANE // 2, 1) * sinr


def _head_norm(x, g):
    ss = jnp.sum(x * x, axis=-1, keepdims=True) * (1.0 / QK_DIM)
    return x * lax.rsqrt(ss + EPS) * g


def _build_kv(ckv, kr_tile, cos, sinr, wk_ref, wv_ref, gk, k_ref, v_ref):
    cb = ckv.astype(BF16)
    kn = _bdot(cb, wk_ref[...])
    for h in range(MLA_HEADS):
        sl = slice(h * HEAD_PAD, (h + 1) * HEAD_PAD)
        k_ref[:, sl] = _rope(_head_norm(kn[:, sl] + kr_tile, gk), cos, sinr).astype(BF16)
    v = _bdot(cb, wv_ref[...])
    lane = lax.broadcasted_iota(jnp.int32, v.shape, 1)
    v_ref[...] = jnp.where((lane & (HEAD_PAD - 1)) == DEN_LANE, 1.0, v).astype(BF16)


def _qkv_kernel(z_ref, cos_ref, sinr_ref, gqa_ref, wq_ref, gkva_ref, wk_ref, wv_ref,
                gq_ref, gk_ref, q_ref, k_ref, v_ref, ckv_ref):
    cos, sinr = cos_ref[...], sinr_ref[...]
    q_c = z_ref[:, 0:Q_LORA]
    kv_c = z_ref[:, Q_LORA:Q_LORA + KV_LORA]
    kr_tile = z_ref[:, Q_LORA + KV_LORA:QKV_W]
    qf = _bdot(_rms(q_c, gqa_ref[...]).astype(BF16), wq_ref[...])
    gq = gq_ref[...] * (ATTN_SCALE * LOG2E)
    for h in range(MLA_HEADS):
        sl = slice(h * HEAD_PAD, (h + 1) * HEAD_PAD)
        q_ref[:, sl] = _rope(_head_norm(qf[:, sl], gq), cos, sinr).astype(BF16)
    ckv = _rms(kv_c, gkva_ref[...])
    ckv_ref[...] = ckv
    _build_kv(ckv, kr_tile, cos, sinr, wk_ref, wv_ref, gk_ref[...], k_ref, v_ref)


def _qkv(z, tabs, p, seq_len, tm):
    t = z.shape[0]
    if tm > seq_len:
        tabs = [jnp.tile(a, (tm // seq_len, 1)) for a in tabs]
    per_seq = max(seq_len // tm, 1)
    tab_spec = pl.BlockSpec((tm, LANE), lambda i: (i % per_seq, 0))
    full = lambda shape: pl.BlockSpec(shape, lambda i: (0,) * len(shape))
    return pl.pallas_call(
        _qkv_kernel,
        grid=(t // tm,),
        in_specs=[
            pl.BlockSpec((tm, QKV_W), lambda i: (i, Z_QKV // QKV_W)),
            tab_spec, tab_spec,
            full((1, Q_LORA)), full((Q_LORA, HP)), full((1, KV_LORA)),
            full((KV_LORA, HP)), full((KV_LORA, HP)), full((1, LANE)), full((1, LANE)),
        ],
        out_specs=[
            pl.BlockSpec((tm, HP), lambda i: (i, 0)),
            pl.BlockSpec((tm, HP), lambda i: (i, 0)),
            pl.BlockSpec((tm, HP), lambda i: (i, 0)),
            pl.BlockSpec((tm, KV_LORA), lambda i: (i, 0)),
        ],
        out_shape=[
            jax.ShapeDtypeStruct((t, HP), BF16),
            jax.ShapeDtypeStruct((t, HP), BF16),
            jax.ShapeDtypeStruct((t, HP), BF16),
            jax.ShapeDtypeStruct((t, KV_LORA), F32),
        ],
        compiler_params=_cparams(("arbitrary",)),
        name="qkv",
    )(z, tabs[0], tabs[1], p["g_qa"], p["wq"], p["g_kva"], p["wk"], p["wv"], p["gq"], p["gk"])


def _kvcache_kernel(ckv_ref, kr_ref, cos_ref, sinr_ref, wk_ref, wv_ref, gk_ref, k_ref, v_ref):
    _build_kv(ckv_ref[...], kr_ref[...], cos_ref[...], sinr_ref[...], wk_ref, wv_ref, gk_ref[...], k_ref, v_ref)


def _kvcache(ckv, kr_tile, tabs, p):
    t = ckv.shape[0]
    return pl.pallas_call(
        _kvcache_kernel,
        out_shape=[jax.ShapeDtypeStruct((t, HP), BF16), jax.ShapeDtypeStruct((t, HP), BF16)],
        compiler_params=_cparams(None),
        name="kvcache",
    )(ckv, kr_tile, tabs[0], tabs[1], p["wk"], p["wv"], p["gk"])


def _fourier_kernel(nb, u_ref, cc_ref, sc_ref, cl_ref, sl_ref, o_ref, a_scr, b_scr):
    @pl.when(pl.program_id(1) == 0)
    def _():
        for b in range(nb):
            u = u_ref[b].astype(BF16)
            a_scr[:, b * FNET_W:(b + 1) * FNET_W] = _bdot(u, cc_ref[...]).astype(BF16)
            b_scr[:, b * FNET_W:(b + 1) * FNET_W] = _bdot(u, sc_ref[...]).astype(BF16)

    y = _bdot(cl_ref[...], a_scr[...]) - _bdot(sl_ref[...], b_scr[...])
    for b in range(nb):
        o_ref[b] = y[:, b * FNET_W:(b + 1) * FNET_W].astype(BF16)


def _fourier_split_kernel(nb, u_ref, cc_ref, sc_ref, ca_ref, sa_ref, cb_ref, sb_ref, o_ref,
                          a_scr, b_scr, cl_scr, sl_scr):
    @pl.when(pl.program_id(1) == 0)
    def _():
        for b in range(nb):
            u = u_ref[b].astype(BF16)
            a_scr[:, b * FNET_W:(b + 1) * FNET_W] = _bdot(u, cc_ref[...]).astype(BF16)
            b_scr[:, b * FNET_W:(b + 1) * FNET_W] = _bdot(u, sc_ref[...]).astype(BF16)

    cb, sb = cb_ref[...], sb_ref[...]
    for j in range(ca_ref.shape[0]):
        ca, sa = ca_ref[j:j + 1, :], sa_ref[j:j + 1, :]
        cl_scr[j * DFT_SPLIT:(j + 1) * DFT_SPLIT, :] = (ca * cb - sa * sb).astype(BF16)
        sl_scr[j * DFT_SPLIT:(j + 1) * DFT_SPLIT, :] = (sa * cb + ca * sb).astype(BF16)
    y = _bdot(cl_scr[...], a_scr[...]) - _bdot(sl_scr[...], b_scr[...])
    for b in range(nb):
        o_ref[b] = y[:, b * FNET_W:(b + 1) * FNET_W].astype(BF16)


def _fourier_split(z3, mats, nb, tq):
    s, l, _ = z3.shape
    cc, sc, ca, sa, cb, sb = mats
    full = lambda a: pl.BlockSpec(a.shape, lambda i, r: (0, 0))
    rows = pl.BlockSpec((tq // DFT_SPLIT, l), lambda i, r: (r, 0))
    return pl.pallas_call(
        functools.partial(_fourier_split_kernel, nb),
        grid=(s // nb, l // tq),
        in_specs=[pl.BlockSpec((nb, l, FNET_W), lambda i, r: (i, 0, 0)), full(cc), full(sc), rows, rows,
                  full(cb), full(sb)],
        out_specs=pl.BlockSpec((nb, tq, FNET_W), lambda i, r: (i, r, 0)),
        out_shape=jax.ShapeDtypeStruct((s, l, FNET_W), BF16),
        scratch_shapes=[pltpu.VMEM((l, nb * FNET_W), BF16), pltpu.VMEM((l, nb * FNET_W), BF16),
                        pltpu.VMEM((tq, l), BF16), pltpu.VMEM((tq, l), BF16)],
        compiler_params=_cparams(("arbitrary", "arbitrary")),
        name="fourier_split",
    )(z3, cc, sc, ca, sa, cb, sb)


def _fourier(z3, mats, nb, tq):
    s, l, _ = z3.shape
    if len(mats) == 6:
        return _fourier_split(z3, mats, nb, tq)
    cc, sc, cl, sl = mats
    return pl.pallas_call(
        functools.partial(_fourier_kernel, nb),
        grid=(s // nb, l // tq),
        in_specs=[
            pl.BlockSpec((nb, l, FNET_W), lambda i, r: (i, 0, 0)),
            pl.BlockSpec((FNET_W, FNET_W), lambda i, r: (0, 0)),
            pl.BlockSpec((FNET_W, FNET_W), lambda i, r: (0, 0)),
            pl.BlockSpec((tq, l), lambda i, r: (r, 0)),
            pl.BlockSpec((tq, l), lambda i, r: (r, 0)),
        ],
        out_specs=pl.BlockSpec((nb, tq, FNET_W), lambda i, r: (i, r, 0)),
        out_shape=jax.ShapeDtypeStruct((s, l, FNET_W), BF16),
        scratch_shapes=[pltpu.VMEM((l, nb * FNET_W), BF16), pltpu.VMEM((l, nb * FNET_W), BF16)],
        compiler_params=_cparams(("arbitrary", "arbitrary")),
        name="fourier",
    )(z3, cc, sc, cl, sl)


def _load_ext(ref, c, rows, nchunks):
    t0 = pl.multiple_of(c * rows, rows)
    main = ref[pl.ds(t0, rows), :]
    lo = pl.multiple_of(jnp.maximum(t0 - SUB, 0), SUB)
    hi = pl.multiple_of(jnp.minimum(t0 + rows, (nchunks - 1) * rows + rows - SUB), SUB)
    prev = jnp.where(c > 0, ref[pl.ds(lo, SUB), :], 0.0)
    nxt = jnp.where(c < nchunks - 1, ref[pl.ds(hi, SUB), :], 0.0)
    return jnp.concatenate([prev, main, nxt], axis=0)


def _shifted(ext, off, rows):
    n = ext.shape[0]
    r = ext if off == 0 else pltpu.roll(ext, (-off) % n, 0)
    return r[SUB:SUB + rows]


def _gelu_tanh(x):
    return 0.5 * x * (1.0 + jnp.tanh(math.sqrt(2.0 / math.pi) * (x + 0.044715 * (x * x * x))))


def _lru_gates(xc, d, wa_ref, wx_ref, ba_ref, bx_ref, lam_ref):
    xb = xc.astype(BF16)
    r = jax.nn.sigmoid(_bdot(xb, wa_ref[d]) + ba_ref[d])
    i = jax.nn.sigmoid(_bdot(xb, wx_ref[d]) + bx_ref[d])
    nl = -lam_ref[d]
    softplus = jnp.maximum(nl, 0.0) + jnp.log1p(jnp.exp(-jnp.abs(nl)))
    la = (-LRU_C) * r * softplus
    a = jnp.exp(la)
    one_m_a2 = -jnp.tanh(la) * (a * a + 1.0)
    b = jnp.sqrt(one_m_a2) * (i * xc)
    return a, b


def _group_scan(a, b, reverse):
    rows = a.shape[0]
    rm = lax.broadcasted_iota(jnp.int32, a.shape, 0) & (SUB - 1)
    for s in (1, 2, 4):
        if reverse:
            sh, m = rows - s, rm + s <= SUB - 1
        else:
            sh, m = s, rm >= s
        a_sh = pltpu.roll(a, sh, 0)
        b_sh = pltpu.roll(b, sh, 0)
        b = jnp.where(m, a * b_sh + b, b)
        a = jnp.where(m, a * a_sh, a)
    return a, b


def _carry_scan(a, b, carry, reverse):
    ng = a.shape[0] // SUB
    out = [None] * ng
    order = range(ng - 1, -1, -1) if reverse else range(ng)
    for g in order:
        hg = a[g * SUB:(g + 1) * SUB] * carry + b[g * SUB:(g + 1) * SUB]
        out[g] = hg
        carry = hg[0:1] if reverse else hg[SUB - 1:SUB]
    return jnp.concatenate(out, axis=0), carry


def _lru_kernel(rows, nchunks, xl_ref, gl_ref, bs_ref, cs_ref, xs_ref, h0_ref, cw_ref, wa_ref, wx_ref,
                ba_ref, bx_ref, lam_ref, sw_ref, yl_ref, ys_ref, fin_ref, xc_scr, hf_scr, hb_scr):
    cw = cw_ref[...]
    sw = sw_ref[...]
    chunk = lambda c: pl.ds(pl.multiple_of(c * rows, rows), rows)

    def conv_body(c, carry):
        ext = _load_ext(xl_ref, c, rows, nchunks)
        xc = _shifted(ext, -2, rows) * cw[0:1]
        for k in range(1, 4):
            xc = xc + _shifted(ext, k - 2, rows) * cw[k:k + 1]
        xc_scr[chunk(c), :] = xc
        return carry

    lax.fori_loop(0, nchunks, conv_body, 0)

    def direction(c, d, carry, h_scr):
        a, b = _lru_gates(xc_scr[chunk(c), :], d, wa_ref, wx_ref, ba_ref, bx_ref, lam_ref)
        a, b = _group_scan(a, b, d == 1)
        h, carry = _carry_scan(a, b, carry, d == 1)
        h_scr[chunk(c), :] = h
        return carry

    def scan_body(j, carries):
        cf, cb = carries
        return direction(j, 0, cf, hf_scr), direction(nchunks - 1 - j, 1, cb, hb_scr)

    cf, cb = lax.fori_loop(0, nchunks, scan_body, (h0_ref[0:1, :], h0_ref[1:2, :]))
    fin_ref[0:1, :] = cf
    fin_ref[1:2, :] = cb

    def out_body(c, carry):
        y = (hf_scr[chunk(c), :] + hb_scr[chunk(c), :]) * _gelu_tanh(gl_ref[chunk(c), :])
        yl_ref[chunk(c), :] = y.astype(BF16)
        ext = _load_ext(cs_ref, c, rows, nchunks) * _load_ext(xs_ref, c, rows, nchunks)
        acc = _shifted(ext, -1, rows) * sw[0:1]
        acc = acc + _shifted(ext, 0, rows) * sw[1:2]
        acc = acc + _shifted(ext, 1, rows) * sw[2:3]
        ys_ref[chunk(c), :] = (bs_ref[chunk(c), :] * acc).astype(BF16)
        return carry

    lax.fori_loop(0, nchunks, out_body, 0)


def _lru(z3, h0, p):
    s, l, _ = z3.shape
    rows = min(l, 256)
    nchunks = l // rows
    half = lambda blk: pl.BlockSpec((None, l, LANE), lambda i, j: (i, 0, blk + j))
    wspec = lambda shape: pl.BlockSpec((None,) + shape, lambda i, j: (j,) + (0,) * len(shape))
    seq_half = pl.BlockSpec((None, l, LANE), lambda i, j: (i, 0, j))
    return pl.pallas_call(
        functools.partial(_lru_kernel, rows, nchunks),
        grid=(s, 2),
        in_specs=[
            half(2), half(4), half(6), half(8), half(10),
            pl.BlockSpec((None, 2, LANE), lambda i, j: (i, 0, j)),
            wspec((4, LANE)), wspec((2, LANE, LANE)), wspec((2, LANE, LANE)),
            wspec((2, 1, LANE)), wspec((2, 1, LANE)), wspec((2, 1, LANE)), wspec((3, LANE)),
        ],
        out_specs=[seq_half, seq_half, pl.BlockSpec((None, 2, LANE), lambda i, j: (i, 0, j))],
        out_shape=[jax.ShapeDtypeStruct((s, l, LRU_W), BF16), jax.ShapeDtypeStruct((s, l, SC_W), BF16),
                   jax.ShapeDtypeStruct((s, 2, LRU_W), F32)],
        scratch_shapes=[pltpu.VMEM((l, LANE), F32), pltpu.VMEM((l, LANE), F32), pltpu.VMEM((l, LANE), F32)],
        compiler_params=_cparams(("arbitrary", "arbitrary")),
        name="lru",
    )(z3, z3, z3, z3, z3, h0, p["lru_conv"], p["lru_wa"], p["lru_wx"], p["lru_ba"], p["lru_bx"], p["lru_lam"],
      p["sc_conv"])


def _store_head_pair(o_ref, rows, j, o_even, o_odd):
    lane = lax.broadcasted_iota(jnp.int32, o_even.shape, 1)
    pair = jnp.where(lane < V_DIM, o_even, pltpu.roll(o_odd, V_DIM, 1))
    o_ref[rows, j * LANE:(j + 1) * LANE] = pair.astype(BF16)


def _attn_kernel(hp, sub, has_cache, *refs):
    if has_cache:
        q_ref, k_ref, v_ref, kc_ref, vc_ref, o_ref, s_scr, p_scr, m_scr, sc_scr, pc_scr = refs
    else:
        q_ref, k_ref, v_ref, o_ref, s_scr, p_scr, m_scr = refs
    nt = (((1,), (1,)), ((), ()))
    items = [(slice(r * sub, (r + 1) * sub), slice(h * HEAD_PAD, (h + 1) * HEAD_PAD))
             for r in range(q_ref.shape[0] // sub) for h in range(hp)]
    n = len(items)

    def scores(i):
        rows, head = items[i]
        q = q_ref[rows, head]
        s = lax.dot_general(q, k_ref[:, head], nt, preferred_element_type=F32)
        m = jnp.max(s, axis=-1, keepdims=True)
        s_scr[i % 2] = s
        if has_cache:
            sc = lax.dot_general(q, kc_ref[:, head], nt, preferred_element_type=F32)
            m = jnp.maximum(m, jnp.max(sc, axis=-1, keepdims=True))
            sc_scr[i % 2] = sc
        m_scr[i % 2] = jnp.broadcast_to(m, m_scr.shape[1:])

    def probs(i):
        m = m_scr[i % 2][:, 0:1]
        p_scr[i % 2] = jnp.exp2(s_scr[i % 2] - m).astype(BF16)
        if has_cache:
            pc_scr[i % 2] = jnp.exp2(sc_scr[i % 2] - m).astype(BF16)

    pending = {}

    def weighted_values(i):
        rows, head = items[i]
        o = _bdot(p_scr[i % 2], v_ref[:, head])
        if has_cache:
            o = o + _bdot(pc_scr[i % 2], vc_ref[:, head])
        o = o / o[:, DEN_LANE:DEN_LANE + 1]
        h = i % hp
        if h % 2 == 0:
            pending[0] = o
        else:
            _store_head_pair(o_ref, rows, h // 2, pending.pop(0), o)

    scores(0)
    for i in range(n):
        if i + 1 < n:
            scores(i + 1)
        if i > 0:
            weighted_values(i - 1)
        probs(i)
    weighted_values(n - 1)


def _attn_short_kernel(hp, q_ref, k_ref, v_ref, o_ref):
    nt = (((1,), (1,)), ((), ()))
    heads = [slice(h * HEAD_PAD, (h + 1) * HEAD_PAD) for h in range(hp)]
    scores = [lax.dot_general(q_ref[:, hd], k_ref[:, hd], nt, preferred_element_type=F32) for hd in heads]
    probs = [jnp.exp2(s - jnp.max(s, axis=-1, keepdims=True)).astype(BF16) for s in scores]
    outs = []
    for hd, p in zip(heads, probs):
        o = _bdot(p, v_ref[:, hd])
        outs.append(o / o[:, DEN_LANE:DEN_LANE + 1])
    for j in range(hp // 2):
        _store_head_pair(o_ref, slice(None), j, outs[2 * j], outs[2 * j + 1])


def _attn(q, k, v, cache, hp, tq):
    s, l, _ = q.shape
    w = hp * HEAD_PAD
    qspec = pl.BlockSpec((None, tq, w), lambda i, h, r: (i, r, h))
    ospec = pl.BlockSpec((None, tq, hp * V_DIM), lambda i, h, r: (i, r, h))
    if cache is None and tq == l:
        kspec = pl.BlockSpec((None, l, w), lambda i, h, r: (i, 0, h))
        return pl.pallas_call(
            functools.partial(_attn_short_kernel, hp),
            grid=(s, MLA_HEADS // hp, 1),
            in_specs=[qspec, kspec, kspec],
            out_specs=ospec,
            out_shape=jax.ShapeDtypeStruct((s, l, MLA_OUT), BF16),
            compiler_params=_cparams(("arbitrary", "arbitrary", "arbitrary")),
            name="attn_short",
        )(q, k, v)
    kv_mode = dict(pipeline_mode=pl.Buffered(1))
    kspec = pl.BlockSpec((None, l, w), lambda i, h, r: (i, 0, h), **kv_mode)
    in_specs = [qspec, kspec, kspec]
    args = [q, k, v]
    if cache is not None:
        lc = cache[0].shape[1]
        cspec = pl.BlockSpec((None, lc, w), lambda i, h, r: (i, 0, h), **kv_mode)
        in_specs += [cspec, cspec]
        args += list(cache)
    sub = min(tq, ATTN_SUB)
    scratch = [pltpu.VMEM((2, sub, l), F32), pltpu.VMEM((2, sub, l), BF16), pltpu.VMEM((2, sub, LANE), F32)]
    if cache is not None:
        scratch += [pltpu.VMEM((2, sub, lc), F32), pltpu.VMEM((2, sub, lc), BF16)]
    return pl.pallas_call(
        functools.partial(_attn_kernel, hp, sub, cache is not None),
        grid=(s, MLA_HEADS // hp, l // tq),
        in_specs=in_specs,
        out_specs=ospec,
        out_shape=jax.ShapeDtypeStruct((s, l, MLA_OUT), BF16),
        scratch_shapes=scratch,
        compiler_params=_cparams(("arbitrary", "arbitrary", "arbitrary")),
        name="attn",
    )(*args)


def _route(lt):
    row = lax.broadcasted_iota(jnp.int32, (SUB, lt.shape[1]), 0).astype(F32)
    neg = -jnp.inf
    gl = lt[0:SUB]
    g_ok = row < N_GROUPS
    glm = jnp.where(g_ok, gl, neg)
    gmax = jnp.max(glm, axis=0, keepdims=True)
    gsel = jnp.min(jnp.where(glm == gmax, row, float(SUB)), axis=0, keepdims=True)
    g_w = 1.0 / jnp.sum(jnp.where(g_ok, jnp.exp(gl - gmax), 0.0), axis=0, keepdims=True)
    es = lt[SUB:2 * SUB]
    for g in range(1, N_GROUPS):
        es = jnp.where(gsel == float(g), lt[(g + 1) * SUB:(g + 2) * SUB], es)
    m1 = jnp.max(es, axis=0, keepdims=True)
    i1 = jnp.min(jnp.where(es == m1, row, float(SUB)), axis=0, keepdims=True)
    es2 = jnp.where(row == i1, neg, es)
    m2 = jnp.max(es2, axis=0, keepdims=True)
    i2 = jnp.min(jnp.where(es2 == m2, row, float(SUB)), axis=0, keepdims=True)
    e2 = jnp.exp(m2 - m1)
    inv = g_w / (1.0 + e2)
    base = gsel * EXP_PER_GROUP
    return (jnp.where(row == RT_ID0, base + i1, 0.0) + jnp.where(row == RT_ID1, base + i2, 0.0)
            + jnp.where(row == RT_W0, inv, 0.0) + jnp.where(row == RT_W1, inv * e2, 0.0))


def _pack_bf16_pairs(x):
    k = x.shape[1] // 2
    bits = lax.bitcast_convert_type(x.astype(BF16).astype(F32), jnp.int32)
    return lax.shift_right_logical(bits[:, :k], 16) | (bits[:, k:] & HI_MASK)


def _unpack_bf16_pairs(w):
    lo = lax.bitcast_convert_type(lax.shift_left(w, 16), F32).astype(BF16)
    hi = lax.bitcast_convert_type(w & HI_MASK, F32).astype(BF16)
    return lo, hi


def _back_kernel(x_ref, mod_ref, g1_ref, g2_ref, yf_ref, yl_ref, ys_ref, o_ref,
                 wg_ref, wpf_ref, wpl_ref, wps_ref, wpm_ref, wo_ref, wrh_ref, wrl_ref, br_ref,
                 x1_ref, h2_ref, comb_ref):
    d = D_MODEL
    x = x_ref[...]
    sh1, sc1, gt1 = mod_ref[:, 0:d], mod_ref[:, d:2 * d], mod_ref[:, 2 * d:3 * d]
    sh2, sc2 = mod_ref[:, 3 * d:4 * d], mod_ref[:, 4 * d:5 * d]
    hb = (_rms(x, g1_ref[...]) * (1.0 + sc1) + sh1).astype(BF16)
    merged = None
    for j, (y_ref, w_ref) in enumerate(((yf_ref, wpf_ref), (yl_ref, wpl_ref), (ys_ref, wps_ref), (o_ref, wpm_ref))):
        gate = jax.nn.sigmoid(_bdot(hb, wg_ref[:, j * d:(j + 1) * d]))
        term = gate * _bdot(y_ref[...], w_ref[...])
        merged = term if merged is None else merged + term
    x1 = x + gt1 * _bdot(merged.astype(BF16), wo_ref[...])
    x1_ref[...] = x1
    h2 = _rms(x1, g2_ref[...]) * (1.0 + sc2) + sh2
    h2b = h2.astype(BF16)
    h2_ref[...] = _pack_bf16_pairs(h2)
    h2l = (h2 - h2b.astype(F32)).astype(BF16)
    nt = (((1,), (1,)), ((), ()))
    rdot = lambda w_ref, h: lax.dot_general(w_ref[...], h, nt, preferred_element_type=F32)
    logits_t = rdot(wrh_ref, h2b) + rdot(wrh_ref, h2l) + rdot(wrl_ref, h2b) + br_ref[...]
    comb_ref[...] = _route(logits_t)


def _back(x, mod, yf, yl, ys, o, p, tm):
    t = x.shape[0]
    per_mod = t // (mod.shape[0] * tm)
    row = lambda w: pl.BlockSpec((tm, w), lambda i: (i, 0))
    full = lambda a: pl.BlockSpec(a.shape, lambda i: (0,) * a.ndim, pipeline_mode=pl.Buffered(1))
    weights = [p["w_gate"], p["w_pf"], p["w_pl"], p["w_ps"], p["w_pm"], p["w_out"], p["wr_hi"], p["wr_lo"], p["b_r"]]
    return pl.pallas_call(
        _back_kernel,
        grid=(t // tm,),
        in_specs=[
            row(D_MODEL),
            pl.BlockSpec((None, 1, 6 * D_MODEL), lambda i: (i // per_mod, 0, 0)),
            pl.BlockSpec((1, D_MODEL), lambda i: (0, 0)),
            pl.BlockSpec((1, D_MODEL), lambda i: (0, 0)),
            row(FNET_W), row(LRU_W), row(SC_W), row(MLA_OUT),
        ] + [full(w) for w in weights],
        out_specs=[row(D_MODEL), row(PACK_W), pl.BlockSpec((SUB, tm), lambda i: (0, i))],
        out_shape=[
            jax.ShapeDtypeStruct((t, D_MODEL), F32),
            jax.ShapeDtypeStruct((t, PACK_W), jnp.int32),
            jax.ShapeDtypeStruct((SUB, t), F32),
        ],
        compiler_params=_cparams(("arbitrary",)),
        name="back",
    )(x, mod, p["norm1_g"], p["norm2_g"], yf, yl, ys, o, *weights)


def _dispatch_plan(rt, tm):
    ids = jnp.concatenate([rt[RT_ID0], rt[RT_ID1]]).astype(jnp.int32)
    onehot = (ids[:, None] == jnp.arange(N_EXPERTS, dtype=jnp.int32)[None, :]).astype(jnp.int32)
    csum = jnp.cumsum(onehot, axis=0)
    counts = csum[-1]
    tiles = (counts + tm - 1) // tm
    tile_end = jnp.cumsum(tiles)
    row_start = (tile_end - tiles) * tm
    pos = jnp.sum(onehot * (row_start[None, :] + csum - 1), axis=1)
    n_tiles = ids.shape[0] // tm + N_EXPERTS
    k = jnp.arange(n_tiles, dtype=jnp.int32)
    tile_expert = jnp.minimum(jnp.sum((k[:, None] >= tile_end[None, :]).astype(jnp.int32), axis=1), N_EXPERTS - 1)
    return pos, tile_expert, tile_end[-1:].astype(jnp.int32), n_tiles


def _sc_worker_rows(n_rows):
    workers = SC_CORES * SC_SUBCORES
    per_w = n_rows // workers
    assert per_w * workers == n_rows and per_w % SC_CHUNK == 0
    return per_w


def _sc_mesh():
    return plsc.VectorSubcoreMesh(core_axis_name="c", subcore_axis_name="s")


def _sc_scatter_rows(src, pos, n_out):
    t, w = src.shape
    per_w = _sc_worker_rows(pos.shape[0])

    @functools.partial(
        pl.kernel, mesh=_sc_mesh(), out_type=jax.ShapeDtypeStruct((n_out, w), src.dtype),
        scratch_types=[pltpu.VMEM((SC_CHUNK,), jnp.int32), pltpu.VMEM((SC_CHUNK, w), src.dtype)],
        name="moe_dispatch")
    def run(src_hbm, pos_hbm, out_hbm, idx_v, rows_v):
        wid = lax.axis_index("s") * SC_CORES + lax.axis_index("c")

        @pl.loop(0, per_w // SC_CHUNK)
        def _(i):
            j0 = wid * per_w + i * SC_CHUNK
            pltpu.sync_copy(pos_hbm.at[pl.ds(j0, SC_CHUNK)], idx_v)
            pltpu.sync_copy(src_hbm.at[pl.ds(lax.rem(j0, t), SC_CHUNK)], rows_v)
            pltpu.sync_copy(rows_v, out_hbm.at[idx_v])

    return run(src, pos)


def _sc_gather_rows(src, pos):
    w = src.shape[1]
    n = pos.shape[0]
    per_w = _sc_worker_rows(n)

    @functools.partial(
        pl.kernel, mesh=_sc_mesh(), out_type=jax.ShapeDtypeStruct((n, w), src.dtype),
        scratch_types=[pltpu.VMEM((SC_CHUNK,), jnp.int32), pltpu.VMEM((SC_CHUNK, w), src.dtype)],
        name="moe_return")
    def run(src_hbm, pos_hbm, out_hbm, idx_v, rows_v):
        wid = lax.axis_index("s") * SC_CORES + lax.axis_index("c")

        @pl.loop(0, per_w // SC_CHUNK)
        def _(i):
            j0 = wid * per_w + i * SC_CHUNK
            pltpu.sync_copy(pos_hbm.at[pl.ds(j0, SC_CHUNK)], idx_v)
            pltpu.sync_copy(src_hbm.at[idx_v], rows_v)
            pltpu.sync_copy(rows_v, out_hbm.at[pl.ds(j0, SC_CHUNK)])

    return run(src, pos)


def _experts_kernel(te_ref, nu_ref, x_ref, w1_ref, w3_ref, w2_ref, y_ref):
    half = D_MODEL // 2

    @pl.when(pl.program_id(0) < nu_ref[0])
    def _():
        lo, hi = _unpack_bf16_pairs(x_ref[...])
        a = _bdot(lo, w1_ref[:half, :].astype(BF16)) + _bdot(hi, w1_ref[half:, :].astype(BF16))
        u = _bdot(lo, w3_ref[:half, :].astype(BF16)) + _bdot(hi, w3_ref[half:, :].astype(BF16))
        mid = (a * jax.nn.sigmoid(a)) * u
        y_ref[...] = _pack_bf16_pairs(_bdot(mid.astype(BF16), w2_ref[...].astype(BF16)))


def _experts(xs, tile_expert, n_used, n_tiles, w1, w3, w2, layer, tm):
    wspec = lambda a: pl.BlockSpec((None, None) + a.shape[2:], lambda i, te, nu: (layer, te[i], 0, 0))
    grid_spec = pltpu.PrefetchScalarGridSpec(
        num_scalar_prefetch=2,
        grid=(n_tiles,),
        in_specs=[pl.BlockSpec((tm, PACK_W), lambda i, te, nu: (jnp.minimum(i, nu[0] - 1), 0)),
                  wspec(w1), wspec(w3), wspec(w2)],
        out_specs=pl.BlockSpec((tm, PACK_W), lambda i, te, nu: (jnp.where(i < nu[0], i, n_tiles), 0)),
    )
    return pl.pallas_call(
        _experts_kernel,
        grid_spec=grid_spec,
        out_shape=jax.ShapeDtypeStruct((xs.shape[0] + tm, PACK_W), jnp.int32),
        compiler_params=_cparams(("arbitrary",)),
        name="experts",
    )(tile_expert, n_used, xs, w1, w3, w2)


def _combine_kernel(x1_ref, mod_ref, rt_ref, g0_ref, g1_ref, o_ref):
    rt = rt_ref[...]
    lo0, hi0 = _unpack_bf16_pairs(g0_ref[...])
    lo1, hi1 = _unpack_bf16_pairs(g1_ref[...])
    w0 = rt[:, RT_W0:RT_W0 + 1]
    w1 = rt[:, RT_W1:RT_W1 + 1]
    half = D_MODEL // 2
    gt2 = mod_ref[:, 5 * D_MODEL:6 * D_MODEL]
    o_ref[:, :half] = x1_ref[:, :half] + gt2[:, :half] * (w0 * lo0.astype(F32) + w1 * lo1.astype(F32))
    o_ref[:, half:] = x1_ref[:, half:] + gt2[:, half:] * (w0 * hi0.astype(F32) + w1 * hi1.astype(F32))


def _combine(x1, mod, rt, g, tm):
    t = x1.shape[0]
    per_mod = t // (mod.shape[0] * tm)
    return pl.pallas_call(
        _combine_kernel,
        grid=(t // tm,),
        in_specs=[
            pl.BlockSpec((tm, D_MODEL), lambda i: (i, 0)),
            pl.BlockSpec((None, 1, 6 * D_MODEL), lambda i: (i // per_mod, 0, 0)),
            pl.BlockSpec((tm, LANE), lambda i: (i, 0)),
            pl.BlockSpec((tm, PACK_W), lambda i: (i, 0)),
            pl.BlockSpec((tm, PACK_W), lambda i: (i + t // tm, 0)),
        ],
        out_specs=pl.BlockSpec((tm, D_MODEL), lambda i: (i, 0)),
        out_shape=jax.ShapeDtypeStruct((t, D_MODEL), F32),
        compiler_params=_cparams(("arbitrary",)),
        name="combine",
    )(x1, mod, rt, g, g)


def _moe(h2p, rt, x1, mod, p, tm_e, tm_c):
    pos, tile_expert, n_used, n_tiles = _dispatch_plan(rt, tm_e)
    xs = _sc_scatter_rows(h2p, pos, n_tiles * tm_e)
    ys = _experts(xs, tile_expert, n_used, n_tiles, p["w1"], p["w3"], p["w2"], p["layer"], tm_e)
    g = _sc_gather_rows(ys, pos)
    rt_cols = jnp.pad(rt.T, ((0, 0), (0, LANE - rt.shape[0])))
    return _combine(x1, mod, rt_cols, g, tm_c)


def _channel_dft():
    n = np.arange(FNET_W // FNET_GROUPS)
    ang = 2.0 * np.pi * ((n[:, None] * n[None, :]) % n.size) / n.size
    eye = np.eye(FNET_GROUPS)
    scale = 1.0 / math.sqrt(n.size)
    return (jnp.asarray(np.kron(eye, np.cos(ang) * scale), F32).astype(BF16),
            jnp.asarray(np.kron(eye, np.sin(ang) * scale), F32).astype(BF16))


def _position_dft(l):
    scale = 1.0 / math.sqrt(l)
    if l <= 256:
        n = np.arange(l)
        ang = 2.0 * np.pi * ((n[:, None] * n[None, :]) % l) / l
        return jnp.asarray(np.cos(ang) * scale, F32).astype(BF16), jnp.asarray(np.sin(ang) * scale, F32).astype(BF16)
    m = DFT_SPLIT
    n = np.arange(l)
    ang_a = 2.0 * np.pi * ((np.arange(l // m)[:, None] * m * n[None, :]) % l) / l
    ang_b = 2.0 * np.pi * ((np.arange(m)[:, None] * n[None, :]) % l) / l
    return (jnp.asarray(np.cos(ang_a), F32), jnp.asarray(np.sin(ang_a), F32),
            jnp.asarray(np.cos(ang_b) * scale, F32), jnp.asarray(np.sin(ang_b) * scale, F32))


def _rope_tables(l, rotate):
    cos = np.ones((l, LANE))
    sinr = np.zeros((l, LANE))
    if rotate:
        t = np.arange(l)
        inv = ROPE_BASE ** (-np.arange(N_FREQ) / N_FREQ)
        ang = np.concatenate([(t // GRID_W)[:, None] * inv, (t % GRID_W)[:, None] * inv], axis=-1)
        half = ROPE // 2
        for lo in (ROPE_X1, ROPE_X2):
            cos[:, lo:lo + half] = np.cos(ang)
        sinr[:, ROPE_X1:ROPE_X1 + half] = -np.sin(ang)
        sinr[:, ROPE_X2:ROPE_X2 + half] = np.sin(ang)
    return tuple(jnp.asarray(a, F32) for a in (cos, sinr))


def _head_lane_source():
    half = ROPE // 2
    src = np.full((HEAD_PAD,), QK_DIM, np.int32)
    src[ROPE_X1:ROPE_X1 + half] = NOPE + np.arange(half)
    src[ROPE_X2:ROPE_X2 + half] = NOPE + half + np.arange(half)
    free = [i for i in range(HEAD_PAD) if src[i] == QK_DIM][:NOPE]
    src[free] = np.arange(NOPE)
    return src


def _place_head_dims(w, n_src):
    src = _head_lane_source()
    src = np.where(src < n_src, src, n_src)
    wz = jnp.concatenate([w[..., :n_src], jnp.zeros(w.shape[:-1] + (1,), w.dtype)], axis=-1)
    out = jnp.take(wz, jnp.asarray(src), axis=-1)
    return out.reshape(out.shape[:-2] + (out.shape[-2] * HEAD_PAD,))


def _place_rope_key(kr):
    half = ROPE // 2
    out = jnp.zeros(kr.shape[:-1] + (LANE,), kr.dtype)
    return out.at[..., ROPE_X1:ROPE_X1 + half].set(kr[..., :half]).at[..., ROPE_X2:ROPE_X2 + half].set(kr[..., half:])


def _take_rope_key(tile):
    half = ROPE // 2
    return jnp.concatenate([tile[..., ROPE_X1:ROPE_X1 + half], tile[..., ROPE_X2:ROPE_X2 + half]], axis=-1)


def _pad_heads(w, lo, hi):
    r = w.shape[0]
    part = w[:, :, lo:hi]
    out = jnp.zeros((r, MLA_HEADS, HEAD_PAD), w.dtype).at[:, :, : hi - lo].set(part)
    return out.reshape(r, HP)


def _blockdiag_halves(w):
    bw = LRU_W // LRU_BLOCKS
    out = jnp.zeros((2, 2, LANE, LANE), w.dtype)
    for half in range(2):
        for k in range(2):
            n = 2 * half + k
            out = out.at[half, :, k * bw:(k + 1) * bw, k * bw:(k + 1) * bw].set(w[:, n])
    return out


def _halves(v):
    return jnp.moveaxis(v.reshape(v.shape[:-1] + (2, LANE)), -2, 0)


def _layer_params(l, a):
    d = D_MODEL
    w_in = a["w_in"][l]
    w_in = jnp.concatenate([w_in[:, :Z_KR], _place_rope_key(w_in[:, Z_KR:])], axis=1)
    w_r = jnp.zeros((RT_ROWS, d), F32).at[:N_GROUPS].set(a["w_gr"][l].T).at[SUB:SUB + N_EXPERTS].set(a["w_er"][l].T)
    b_r = jnp.zeros((RT_ROWS, 1), F32).at[:N_GROUPS, 0].set(a["b_gr"][l]).at[SUB:SUB + N_EXPERTS, 0].set(a["b_er"][l])
    wr_hi = w_r.astype(BF16)
    gpad = lambda g: jnp.take(jnp.concatenate([g, jnp.zeros((1,), F32)]), jnp.asarray(_head_lane_source())).reshape(1, LANE)
    return {
        "norm1_g": a["norm1_g"][l].reshape(1, d), "norm2_g": a["norm2_g"][l].reshape(1, d),
        "w_in": w_in.astype(BF16),
        "lru_conv": _halves(a["lru_conv"][l]),
        "lru_wa": _blockdiag_halves(a["lru_wa"][l]).astype(BF16),
        "lru_wx": _blockdiag_halves(a["lru_wx"][l]).astype(BF16),
        "lru_ba": _halves(a["lru_ba"][l])[:, :, None, :], "lru_bx": _halves(a["lru_bx"][l])[:, :, None, :],
        "lru_lam": _halves(a["lru_lam"][l])[:, :, None, :],
        "sc_conv": _halves(a["sc_conv"][l]),
        "g_qa": a["g_qa"][l].reshape(1, Q_LORA), "g_kva": a["g_kva"][l].reshape(1, KV_LORA),
        "wq": _place_head_dims(a["w_qb"][l], QK_DIM).astype(BF16),
        "wk": _place_head_dims(a["w_kvb"][l], NOPE).astype(BF16),        "wv": _pad_heads(a["w_kvb"][l], NOPE, NOPE + V_DIM).astype(BF16),
        "gq": gpad(a["g_qn"][l]), "gk": gpad(a["g_kn"][l]),
        "w_gate": a["w_gate"][l].astype(BF16),
        "w_pf": a["w_pf"][l].astype(BF16), "w_pl": a["w_pl"][l].astype(BF16), "w_ps": a["w_ps"][l].astype(BF16),
        "w_pm": a["w_pm"][l].astype(BF16), "w_out": a["w_out"][l].astype(BF16),
        "wr_hi": wr_hi, "wr_lo": (w_r - wr_hi.astype(F32)).astype(BF16), "b_r": b_r,
        "w1": a["w1"], "w3": a["w3"], "w2": a["w2"], "layer": l,
    }


def _trunk_layer(x, mod, p, consts, h0, cache, cfg):
    s, l = cfg["s"], cfg["l"]
    z = _front(x, mod, p["norm1_g"], p["w_in"], cfg["tm_front"])
    z3 = z.reshape(s, l, Z_W)
    q, k, v, ckv = _qkv(z, consts["rope"], p, l, cfg["tm_qkv"])
    yf = _fourier(z3, consts["dft"], cfg["nb"], cfg["tq_f"])
    yl, ys, fin = _lru(z3, h0, p)
    o = _attn(q.reshape(s, l, HP), k.reshape(s, l, HP), v.reshape(s, l, HP), cache, cfg["hp"], cfg["tq_a"])
    t = s * l
    x1, h2, comb = _back(x, mod, yf.reshape(t, FNET_W), yl.reshape(t, LRU_W), ys.reshape(t, SC_W),
                         o.reshape(t, MLA_OUT), p, cfg["tm_back"])
    x2 = _moe(h2, comb, x1, mod, p, cfg["tm_moe"], cfg["tm_comb"])
    return x2, z3, ckv, fin


def kernel(x_prompt, x_sample, cache_ckv, cache_krope, state_rglru, c, c_ctx, norm1_g, norm2_g, w_ada, b_ada, w_in, lru_conv, lru_wa, lru_ba, lru_wx, lru_bx, lru_lam, sc_conv, g_qa, w_qb, g_kva, w_kvb, g_qn, g_kn, w_pf, w_pl, w_ps, w_pm, w_gate, w_out, w_gr, b_gr, w_er, b_er, w1, w3, w2):
    a = dict(norm1_g=norm1_g, norm2_g=norm2_g, w_in=w_in, lru_conv=lru_conv, lru_wa=lru_wa, lru_ba=lru_ba,
             lru_wx=lru_wx, lru_bx=lru_bx, lru_lam=lru_lam, sc_conv=sc_conv, g_qa=g_qa, w_qb=w_qb, g_kva=g_kva,
             w_kvb=w_kvb, g_qn=g_qn, g_kn=g_kn, w_pf=w_pf, w_pl=w_pl, w_ps=w_ps, w_pm=w_pm, w_gate=w_gate,
             w_out=w_out, w_gr=w_gr, b_gr=b_gr, w_er=w_er, b_er=b_er, w1=w1, w3=w3, w2=w2)
    bc, lc, d = x_prompt.shape
    bl, ll, _ = x_sample.shape
    past = cache_ckv.shape[2]

    cond8 = jnp.zeros((SUB, d), F32).at[0].set(c_ctx).at[1:1 + bl].set(c)
    mod_all = _ada(cond8, w_ada, b_ada)

    cc, sc = _channel_dft()
    ctx_consts = {"dft": (cc, sc) + _position_dft(lc), "rope": _rope_tables(lc, False)}
    lat_consts = {"dft": (cc, sc) + _position_dft(ll), "rope": _rope_tables(ll, True)}
    cache_tabs = _rope_tables(bl * past, False)
    ctx_cfg = dict(s=bc, l=lc, tm_front=512, tm_qkv=512, nb=8, tq_f=lc, hp=MLA_HEADS, tq_a=lc, tm_back=1024, tm_moe=512, tm_comb=512)
    lat_cfg = dict(s=bl, l=ll, tm_front=512, tm_qkv=512, nb=bl, tq_f=512, hp=MLA_HEADS, tq_a=256, tm_back=1024, tm_moe=512, tm_comb=512)

    xp = x_prompt.reshape(bc * lc, d)
    xs = x_sample.reshape(bl * ll, d)
    h0_ctx = jnp.zeros((bc, 2, LRU_W), F32)
    ckv_list, kr_list, lru_list = [], [], []
    for l in range(DEPTH):
        p = _layer_params(l, a)
        mod_ctx = mod_all[l, 0:1].reshape(1, 1, 6 * d)
        mod_lat = mod_all[l, 1:1 + bl].reshape(bl, 1, 6 * d)
        xp, z3, ckv, fin = _trunk_layer(xp, mod_ctx, p, ctx_consts, h0_ctx, None, ctx_cfg)
        ckv_list.append(ckv.reshape(bc, lc, KV_LORA))
        kr_list.append(_take_rope_key(z3[:, :, Z_KR:]))
        lru_list.append(fin)
        kr_tile = _place_rope_key(cache_krope[:, l].reshape(bl * past, ROPE))
        kc, vc = _kvcache(cache_ckv[:, l].reshape(bl * past, KV_LORA), kr_tile, cache_tabs, p)
        cache = (kc.reshape(bl, past, HP), vc.reshape(bl, past, HP))
        xs, _, _, _ = _trunk_layer(xs, mod_lat, p, lat_consts, state_rglru[:, l], cache, lat_cfg)
    return (xp.reshape(bc, lc, d), xs.reshape(bl, ll, d),
            jnp.stack(ckv_list, axis=1), jnp.stack(kr_list, axis=1), jnp.stack(lru_list, axis=1))
```

```python
import functools
import math

import numpy as np
import jax
import jax.numpy as jnp
from jax import lax
from jax.experimental import pallas as pl
from jax.experimental.pallas import tpu as pltpu
from jax.experimental.pallas import tpu_sc as plsc

F32 = jnp.float32
BF16 = jnp.bfloat16

D_MODEL = 1024
DEPTH = 2
GRID_W = 64
EPS = 1e-6
FNET_W = 256
FNET_GROUPS = 4
LRU_W = 256
LRU_BLOCKS = 4
LRU_C = 8.0
SC_W = 256
MLA_HEADS = 8
Q_LORA = 384
KV_LORA = 256
NOPE = 64
ROPE = 32
V_DIM = 64
QK_DIM = NOPE + ROPE
MLA_OUT = MLA_HEADS * V_DIM
N_FREQ = ROPE // 4
ROPE_BASE = 10000.0
ATTN_SCALE = QK_DIM ** -0.5
LOG2E = math.log2(math.e)
N_GROUPS = 4
EXP_PER_GROUP = 8
N_EXPERTS = N_GROUPS * EXP_PER_GROUP
D_EXPERT = 256

LANE = 128
SUB = 8
HEAD_PAD = LANE
HP = MLA_HEADS * HEAD_PAD
ROPE_X1 = 0
ROPE_X2 = LANE // 2
DEN_LANE = V_DIM
Z_QKV = 6 * 256
Z_KR = Z_QKV + Q_LORA + KV_LORA
Z_W = Z_KR + LANE
QKV_W = Q_LORA + KV_LORA + LANE
VMEM_LIMIT = 52 * 1024 * 1024
PACK_W = D_MODEL // 2
HI_MASK = -65536
RT_ID0, RT_ID1, RT_W0, RT_W1 = 0, 1, 2, 3
RT_ROWS = 48
TOP_K = 2
PLAN_BLK = 256
DFT_SPLIT = 64
ATTN_SUB = 256
SC_CORES = 2
SC_SUBCORES = 16
SC_CHUNK = 128


def _cparams(sem):
    return pltpu.CompilerParams(dimension_semantics=sem, vmem_limit_bytes=VMEM_LIMIT)


def _bdot(a, b):
    return jnp.dot(a, b, preferred_element_type=F32)


def _rms(x, g):
    return x * lax.rsqrt(jnp.mean(x * x, axis=-1, keepdims=True) + EPS) * g


def _ada_kernel(c_ref, w_ref, b_ref, o_ref):
    c = c_ref[...]
    s = (c * jax.nn.sigmoid(c)).astype(BF16)
    o_ref[...] = _bdot(s, w_ref[...].astype(BF16)) + b_ref[...]


def _ada(cond8, w_ada, b_ada):
    nblk = 6 * D_MODEL // 1024
    return pl.pallas_call(
        _ada_kernel,
        grid=(DEPTH, nblk),
        in_specs=[
            pl.BlockSpec((SUB, D_MODEL), lambda l, n: (0, 0)),
            pl.BlockSpec((None, D_MODEL, 1024), lambda l, n: (l, 0, n)),
            pl.BlockSpec((None, 1, 1024), lambda l, n: (l, 0, n)),
        ],
        out_specs=pl.BlockSpec((None, SUB, 1024), lambda l, n: (l, 0, n)),
        out_shape=jax.ShapeDtypeStruct((DEPTH, SUB, 6 * D_MODEL), F32),
        compiler_params=_cparams(("arbitrary", "arbitrary")),
        name="ada",
    )(cond8, w_ada, b_ada.reshape(DEPTH, 1, 6 * D_MODEL))


def _front_kernel(x_ref, mod_ref, g_ref, w_ref, z_ref):
    x = x_ref[...]
    sh = mod_ref[:, 0:D_MODEL]
    sc = mod_ref[:, D_MODEL:2 * D_MODEL]
    h = _rms(x, g_ref[...]) * (1.0 + sc) + sh
    z_ref[...] = _bdot(h.astype(BF16), w_ref[...])


def _front(x, mod, g1, w_in, tm):
    t = x.shape[0]
    per_mod = t // (mod.shape[0] * tm)
    return pl.pallas_call(
        _front_kernel,
        grid=(t // tm,),
        in_specs=[
            pl.BlockSpec((tm, D_MODEL), lambda i: (i, 0)),
            pl.BlockSpec((None, 1, 6 * D_MODEL), lambda i: (i // per_mod, 0, 0)),
            pl.BlockSpec((1, D_MODEL), lambda i: (0, 0)),
            pl.BlockSpec((D_MODEL, Z_W), lambda i: (0, 0)),
        ],
        out_specs=pl.BlockSpec((tm, Z_W), lambda i: (i, 0)),
        out_shape=jax.ShapeDtypeStruct((t, Z_W), F32),
        compiler_params=_cparams(("arbitrary",)),
        name="front",
    )(x, mod, g1, w_in)


def _rope(x, cos, sinr):
    return x * cos + pltpu.roll(x, LANE // 2, 1) * sinr


def _head_norm(x, g):
    ss = jnp.sum(x * x, axis=-1, keepdims=True) * (1.0 / QK_DIM)
    return x * lax.rsqrt(ss + EPS) * g


def _build_kv(ckv, kr_tile, cos, sinr, wk_ref, wv_ref, gk, k_ref, v_ref):
    cb = ckv.astype(BF16)
    kn = _bdot(cb, wk_ref[...])
    for h in range(MLA_HEADS):
        sl = slice(h * HEAD_PAD, (h + 1) * HEAD_PAD)
        k_ref[:, sl] = _rope(_head_norm(kn[:, sl] + kr_tile, gk), cos, sinr).astype(BF16)
    v = _bdot(cb, wv_ref[...])
    lane = lax.broadcasted_iota(jnp.int32, v.shape, 1)
    v_ref[...] = jnp.where((lane & (HEAD_PAD - 1)) == DEN_LANE, 1.0, v).astype(BF16)


def _qkv_kernel(z_ref, cos_ref, sinr_ref, gqa_ref, wq_ref, gkva_ref, wk_ref, wv_ref,
                gq_ref, gk_ref, q_ref, k_ref, v_ref, ckv_ref):
    cos, sinr = cos_ref[...], sinr_ref[...]
    q_c = z_ref[:, 0:Q_LORA]
    kv_c = z_ref[:, Q_LORA:Q_LORA + KV_LORA]
    kr_tile = z_ref[:, Q_LORA + KV_LORA:QKV_W]
    qf = _bdot(_rms(q_c, gqa_ref[...]).astype(BF16), wq_ref[...])
    gq = gq_ref[...] * (ATTN_SCALE * LOG2E)
    for h in range(MLA_HEADS):
        sl = slice(h * HEAD_PAD, (h + 1) * HEAD_PAD)
        q_ref[:, sl] = _rope(_head_norm(qf[:, sl], gq), cos, sinr).astype(BF16)
    ckv = _rms(kv_c, gkva_ref[...])
    ckv_ref[...] = ckv
    _build_kv(ckv, kr_tile, cos, sinr, wk_ref, wv_ref, gk_ref[...], k_ref, v_ref)


def _qkv(z, tabs, p, seq_len, tm):
    t = z.shape[0]
    if tm > seq_len:
        tabs = [jnp.tile(a, (tm // seq_len, 1)) for a in tabs]
    per_seq = max(seq_len // tm, 1)
    tab_spec = pl.BlockSpec((tm, LANE), lambda i: (i % per_seq, 0))
    full = lambda shape: pl.BlockSpec(shape, lambda i: (0,) * len(shape))
    return pl.pallas_call(
        _qkv_kernel,
        grid=(t // tm,),
        in_specs=[
            pl.BlockSpec((tm, QKV_W), lambda i: (i, Z_QKV // QKV_W)),
            tab_spec, tab_spec,
            full((1, Q_LORA)), full((Q_LORA, HP)), full((1, KV_LORA)),
            full((KV_LORA, HP)), full((KV_LORA, HP)), full((1, LANE)), full((1, LANE)),
        ],
        out_specs=[
            pl.BlockSpec((tm, HP), lambda i: (i, 0)),
            pl.BlockSpec((tm, HP), lambda i: (i, 0)),
            pl.BlockSpec((tm, HP), lambda i: (i, 0)),
            pl.BlockSpec((tm, KV_LORA), lambda i: (i, 0)),
        ],
        out_shape=[
            jax.ShapeDtypeStruct((t, HP), BF16),
            jax.ShapeDtypeStruct((t, HP), BF16),
            jax.ShapeDtypeStruct((t, HP), BF16),
            jax.ShapeDtypeStruct((t, KV_LORA), F32),
        ],
        compiler_params=_cparams(("arbitrary",)),
        name="qkv",
    )(z, tabs[0], tabs[1], p["g_qa"], p["wq"], p["g_kva"], p["wk"], p["wv"], p["gq"], p["gk"])


def _kvcache_kernel(ckv_ref, kr_ref, cos_ref, sinr_ref, wk_ref, wv_ref, gk_ref, k_ref, v_ref):
    _build_kv(ckv_ref[...], kr_ref[...], cos_ref[...], sinr_ref[...], wk_ref, wv_ref, gk_ref[...], k_ref, v_ref)


def _kvcache(ckv, kr_tile, tabs, p):
    t = ckv.shape[0]
    return pl.pallas_call(
        _kvcache_kernel,
        out_shape=[jax.ShapeDtypeStruct((t, HP), BF16), jax.ShapeDtypeStruct((t, HP), BF16)],
        compiler_params=_cparams(None),
        name="kvcache",
    )(ckv, kr_tile, tabs[0], tabs[1], p["wk"], p["wv"], p["gk"])


def _fourier_kernel(nb, u_ref, cc_ref, sc_ref, cl_ref, sl_ref, o_ref, a_scr, b_scr):
    @pl.when(pl.program_id(1) == 0)
    def _():
        for b in range(nb):
            u = u_ref[b].astype(BF16)
            a_scr[:, b * FNET_W:(b + 1) * FNET_W] = _bdot(u, cc_ref[...]).astype(BF16)
            b_scr[:, b * FNET_W:(b + 1) * FNET_W] = _bdot(u, sc_ref[...]).astype(BF16)

    y = _bdot(cl_ref[...], a_scr[...]) - _bdot(sl_ref[...], b_scr[...])
    for b in range(nb):
        o_ref[b] = y[:, b * FNET_W:(b + 1) * FNET_W].astype(BF16)


def _fourier_split_kernel(nb, u_ref, cc_ref, sc_ref, ca_ref, sa_ref, cb_ref, sb_ref, o_ref,
                          a_scr, b_scr, cl_scr, sl_scr):
    @pl.when(pl.program_id(1) == 0)
    def _():
        for b in range(nb):
            u = u_ref[b].astype(BF16)
            a_scr[:, b * FNET_W:(b + 1) * FNET_W] = _bdot(u, cc_ref[...]).astype(BF16)
            b_scr[:, b * FNET_W:(b + 1) * FNET_W] = _bdot(u, sc_ref[...]).astype(BF16)

    cb, sb = cb_ref[...], sb_ref[...]
    for j in range(ca_ref.shape[0]):
        ca, sa = ca_ref[j:j + 1, :], sa_ref[j:j + 1, :]
        cl_scr[j * DFT_SPLIT:(j + 1) * DFT_SPLIT, :] = (ca * cb - sa * sb).astype(BF16)
        sl_scr[j * DFT_SPLIT:(j + 1) * DFT_SPLIT, :] = (sa * cb + ca * sb).astype(BF16)
    y = _bdot(cl_scr[...], a_scr[...]) - _bdot(sl_scr[...], b_scr[...])
    for b in range(nb):
        o_ref[b] = y[:, b * FNET_W:(b + 1) * FNET_W].astype(BF16)


def _fourier_split(z3, mats, nb, tq):
    s, l, _ = z3.shape
    cc, sc, ca, sa, cb, sb = mats
    full = lambda a: pl.BlockSpec(a.shape, lambda i, r: (0, 0))
    rows = pl.BlockSpec((tq // DFT_SPLIT, l), lambda i, r: (r, 0))
    return pl.pallas_call(
        functools.partial(_fourier_split_kernel, nb),
        grid=(s // nb, l // tq),
        in_specs=[pl.BlockSpec((nb, l, FNET_W), lambda i, r: (i, 0, 0)), full(cc), full(sc), rows, rows,
                  full(cb), full(sb)],
        out_specs=pl.BlockSpec((nb, tq, FNET_W), lambda i, r: (i, r, 0)),
        out_shape=jax.ShapeDtypeStruct((s, l, FNET_W), BF16),
        scratch_shapes=[pltpu.VMEM((l, nb * FNET_W), BF16), pltpu.VMEM((l, nb * FNET_W), BF16),
                        pltpu.VMEM((tq, l), BF16), pltpu.VMEM((tq, l), BF16)],
        compiler_params=_cparams(("arbitrary", "arbitrary")),
        name="fourier_split",
    )(z3, cc, sc, ca, sa, cb, sb)


def _fourier(z3, mats, nb, tq):
    s, l, _ = z3.shape
    if len(mats) == 6:
        return _fourier_split(z3, mats, nb, tq)
    cc, sc, cl, sl = mats
    return pl.pallas_call(
        functools.partial(_fourier_kernel, nb),
        grid=(s // nb, l // tq),
        in_specs=[
            pl.BlockSpec((nb, l, FNET_W), lambda i, r: (i, 0, 0)),
            pl.BlockSpec((FNET_W, FNET_W), lambda i, r: (0, 0)),
            pl.BlockSpec((FNET_W, FNET_W), lambda i, r: (0, 0)),
            pl.BlockSpec((tq, l), lambda i, r: (r, 0)),
            pl.BlockSpec((tq, l), lambda i, r: (r, 0)),
        ],
        out_specs=pl.BlockSpec((nb, tq, FNET_W), lambda i, r: (i, r, 0)),
        out_shape=jax.ShapeDtypeStruct((s, l, FNET_W), BF16),
        scratch_shapes=[pltpu.VMEM((l, nb * FNET_W), BF16), pltpu.VMEM((l, nb * FNET_W), BF16)],
        compiler_params=_cparams(("arbitrary", "arbitrary")),
        name="fourier",
    )(z3, cc, sc, cl, sl)


def _load_ext(ref, c, rows, nchunks):
    t0 = pl.multiple_of(c * rows, rows)
    main = ref[pl.ds(t0, rows), :]
    lo = pl.multiple_of(jnp.maximum(t0 - SUB, 0), SUB)
    hi = pl.multiple_of(jnp.minimum(t0 + rows, (nchunks - 1) * rows + rows - SUB), SUB)
    prev = jnp.where(c > 0, ref[pl.ds(lo, SUB), :], 0.0)
    nxt = jnp.where(c < nchunks - 1, ref[pl.ds(hi, SUB), :], 0.0)
    return jnp.concatenate([prev, main, nxt], axis=0)


def _shifted(ext, off, rows):
    n = ext.shape[0]
    r = ext if off == 0 else pltpu.roll(ext, (-off) % n, 0)
    return r[SUB:SUB + rows]


def _gelu_tanh(x):
    return 0.5 * x * (1.0 + jnp.tanh(math.sqrt(2.0 / math.pi) * (x + 0.044715 * (x * x * x))))


def _lru_gates(xc, d, wa_ref, wx_ref, ba_ref, bx_ref, lam_ref):
    xb = xc.astype(BF16)
    r = jax.nn.sigmoid(_bdot(xb, wa_ref[d]) + ba_ref[d])
    i = jax.nn.sigmoid(_bdot(xb, wx_ref[d]) + bx_ref[d])
    nl = -lam_ref[d]
    softplus = jnp.maximum(nl, 0.0) + jnp.log1p(jnp.exp(-jnp.abs(nl)))
    la = (-LRU_C) * r * softplus
    a = jnp.exp(la)
    one_m_a2 = -jnp.tanh(la) * (a * a + 1.0)
    b = jnp.sqrt(one_m_a2) * (i * xc)
    return a, b


def _group_scan(a, b, reverse):
    rows = a.shape[0]
    rm = lax.broadcasted_iota(jnp.int32, a.shape, 0) & (SUB - 1)
    for s in (1, 2, 4):
        if reverse:
            sh, m = rows - s, rm + s <= SUB - 1
        else:
            sh, m = s, rm >= s
        a_sh = pltpu.roll(a, sh, 0)
        b_sh = pltpu.roll(b, sh, 0)
        b = jnp.where(m, a * b_sh + b, b)
        a = jnp.where(m, a * a_sh, a)
    return a, b


def _carry_scan(a, b, carry, reverse):
    ng = a.shape[0] // SUB
    out = [None] * ng
    order = range(ng - 1, -1, -1) if reverse else range(ng)
    for g in order:
        hg = a[g * SUB:(g + 1) * SUB] * carry + b[g * SUB:(g + 1) * SUB]
        out[g] = hg
        carry = hg[0:1] if reverse else hg[SUB - 1:SUB]
    return jnp.concatenate(out, axis=0), carry


def _lru_kernel(rows, nchunks, xl_ref, gl_ref, bs_ref, cs_ref, xs_ref, h0_ref, cw_ref, wa_ref, wx_ref,
                ba_ref, bx_ref, lam_ref, sw_ref, yl_ref, ys_ref, fin_ref, xc_scr, hf_scr, hb_scr):
    cw = cw_ref[...]
    sw = sw_ref[...]
    chunk = lambda c: pl.ds(pl.multiple_of(c * rows, rows), rows)

    def conv_body(c, carry):
        ext = _load_ext(xl_ref, c, rows, nchunks)
        xc = _shifted(ext, -2, rows) * cw[0:1]
        for k in range(1, 4):
            xc = xc + _shifted(ext, k - 2, rows) * cw[k:k + 1]
        xc_scr[chunk(c), :] = xc
        return carry

    lax.fori_loop(0, nchunks, conv_body, 0)

    def direction(c, d, carry, h_scr):
        a, b = _lru_gates(xc_scr[chunk(c), :], d, wa_ref, wx_ref, ba_ref, bx_ref, lam_ref)
        a, b = _group_scan(a, b, d == 1)
        h, carry = _carry_scan(a, b, carry, d == 1)
        h_scr[chunk(c), :] = h
        return carry

    def scan_body(j, carries):
        cf, cb = carries
        return direction(j, 0, cf, hf_scr), direction(nchunks - 1 - j, 1, cb, hb_scr)

    cf, cb = lax.fori_loop(0, nchunks, scan_body, (h0_ref[0:1, :], h0_ref[1:2, :]))
    fin_ref[0:1, :] = cf
    fin_ref[1:2, :] = cb

    def out_body(c, carry):
        y = (hf_scr[chunk(c), :] + hb_scr[chunk(c), :]) * _gelu_tanh(gl_ref[chunk(c), :])
        yl_ref[chunk(c), :] = y.astype(BF16)
        ext = _load_ext(cs_ref, c, rows, nchunks) * _load_ext(xs_ref, c, rows, nchunks)
        acc = _shifted(ext, -1, rows) * sw[0:1]
        acc = acc + _shifted(ext, 0, rows) * sw[1:2]
        acc = acc + _shifted(ext, 1, rows) * sw[2:3]
        ys_ref[chunk(c), :] = (bs_ref[chunk(c), :] * acc).astype(BF16)
        return carry

    lax.fori_loop(0, nchunks, out_body, 0)


def _lru(z3, h0, p):
    s, l, _ = z3.shape
    rows = min(l, 256)
    nchunks = l // rows
    half = lambda blk: pl.BlockSpec((None, l, LANE), lambda i, j: (i, 0, blk + j))
    wspec = lambda shape: pl.BlockSpec((None,) + shape, lambda i, j: (j,) + (0,) * len(shape))
    seq_half = pl.BlockSpec((None, l, LANE), lambda i, j: (i, 0, j))
    return pl.pallas_call(
        functools.partial(_lru_kernel, rows, nchunks),
        grid=(s, 2),
        in_specs=[
            half(2), half(4), half(6), half(8), half(10),
            pl.BlockSpec((None, 2, LANE), lambda i, j: (i, 0, j)),
            wspec((4, LANE)), wspec((2, LANE, LANE)), wspec((2, LANE, LANE)),
            wspec((2, 1, LANE)), wspec((2, 1, LANE)), wspec((2, 1, LANE)), wspec((3, LANE)),
        ],
        out_specs=[seq_half, seq_half, pl.BlockSpec((None, 2, LANE), lambda i, j: (i, 0, j))],
        out_shape=[jax.ShapeDtypeStruct((s, l, LRU_W), BF16), jax.ShapeDtypeStruct((s, l, SC_W), BF16),
                   jax.ShapeDtypeStruct((s, 2, LRU_W), F32)],
        scratch_shapes=[pltpu.VMEM((l, LANE), F32), pltpu.VMEM((l, LANE), F32), pltpu.VMEM((l, LANE), F32)],
        compiler_params=_cparams(("arbitrary", "arbitrary")),
        name="lru",
    )(z3, z3, z3, z3, z3, h0, p["lru_conv"], p["lru_wa"], p["lru_wx"], p["lru_ba"], p["lru_bx"], p["lru_lam"],
      p["sc_conv"])


def _store_head_pair(o_ref, rows, j, o_even, o_odd):
    lane = lax.broadcasted_iota(jnp.int32, o_even.shape, 1)
    pair = jnp.where(lane < V_DIM, o_even, pltpu.roll(o_odd, V_DIM, 1))
    o_ref[rows, j * LANE:(j + 1) * LANE] = pair.astype(BF16)


def _attn_kernel(hp, sub, has_cache, *refs):
    if has_cache:
        q_ref, k_ref, v_ref, kc_ref, vc_ref, o_ref, s_scr, p_scr, m_scr, sc_scr, pc_scr = refs
    else:
        q_ref, k_ref, v_ref, o_ref, s_scr, p_scr, m_scr = refs
    nt = (((1,), (1,)), ((), ()))
    items = [(slice(r * sub, (r + 1) * sub), slice(h * HEAD_PAD, (h + 1) * HEAD_PAD))
             for r in range(q_ref.shape[0] // sub) for h in range(hp)]
    n = len(items)

    def scores(i):
        rows, head = items[i]
        q = q_ref[rows, head]
        s = lax.dot_general(q, k_ref[:, head], nt, preferred_element_type=F32)
        m = jnp.max(s, axis=-1, keepdims=True)
        s_scr[i % 2] = s
        if has_cache:
            sc = lax.dot_general(q, kc_ref[:, head], nt, preferred_element_type=F32)
            m = jnp.maximum(m, jnp.max(sc, axis=-1, keepdims=True))
            sc_scr[i % 2] = sc
        m_scr[i % 2] = jnp.broadcast_to(m, m_scr.shape[1:])

    def probs(i):
        m = m_scr[i % 2][:, 0:1]
        p_scr[i % 2] = jnp.exp2(s_scr[i % 2] - m).astype(BF16)
        if has_cache:
            pc_scr[i % 2] = jnp.exp2(sc_scr[i % 2] - m).astype(BF16)

    pending = {}

    def weighted_values(i):
        rows, head = items[i]
        o = _bdot(p_scr[i % 2], v_ref[:, head])
        if has_cache:
            o = o + _bdot(pc_scr[i % 2], vc_ref[:, head])
        o = o / o[:, DEN_LANE:DEN_LANE + 1]
        h = i % hp
        if h % 2 == 0:
            pending[0] = o
        else:
            _store_head_pair(o_ref, rows, h // 2, pending.pop(0), o)

    scores(0)
    for i in range(n):
        if i + 1 < n:
            scores(i + 1)
        if i > 0:
            weighted_values(i - 1)
        probs(i)
    weighted_values(n - 1)


def _attn_short_kernel(hp, q_ref, k_ref, v_ref, o_ref):
    nt = (((1,), (1,)), ((), ()))
    heads = [slice(h * HEAD_PAD, (h + 1) * HEAD_PAD) for h in range(hp)]
    scores = [lax.dot_general(q_ref[:, hd], k_ref[:, hd], nt, preferred_element_type=F32) for hd in heads]
    probs = [jnp.exp2(s - jnp.max(s, axis=-1, keepdims=True)).astype(BF16) for s in scores]
    outs = []
    for hd, p in zip(heads, probs):
        o = _bdot(p, v_ref[:, hd])
        outs.append(o / o[:, DEN_LANE:DEN_LANE + 1])
    for j in range(hp // 2):
        _store_head_pair(o_ref, slice(None), j, outs[2 * j], outs[2 * j + 1])


def _attn(q, k, v, cache, hp, tq):
    s, l, _ = q.shape
    w = hp * HEAD_PAD
    qspec = pl.BlockSpec((None, tq, w), lambda i, h, r: (i, r, h))
    ospec = pl.BlockSpec((None, tq, hp * V_DIM), lambda i, h, r: (i, r, h))
    if cache is None and tq == l:
        kspec = pl.BlockSpec((None, l, w), lambda i, h, r: (i, 0, h))
        return pl.pallas_call(
            functools.partial(_attn_short_kernel, hp),
            grid=(s, MLA_HEADS // hp, 1),
            in_specs=[qspec, kspec, kspec],
            out_specs=ospec,
            out_shape=jax.ShapeDtypeStruct((s, l, MLA_OUT), BF16),
            compiler_params=_cparams(("arbitrary", "arbitrary", "arbitrary")),
            name="attn_short",
        )(q, k, v)
    kv_mode = dict(pipeline_mode=pl.Buffered(1))
    kspec = pl.BlockSpec((None, l, w), lambda i, h, r: (i, 0, h), **kv_mode)
    in_specs = [qspec, kspec, kspec]
    args = [q, k, v]
    if cache is not None:
        lc = cache[0].shape[1]
        cspec = pl.BlockSpec((None, lc, w), lambda i, h, r: (i, 0, h), **kv_mode)
        in_specs += [cspec, cspec]
        args += list(cache)
    sub = min(tq, ATTN_SUB)
    scratch = [pltpu.VMEM((2, sub, l), F32), pltpu.VMEM((2, sub, l), BF16), pltpu.VMEM((2, sub, LANE), F32)]
    if cache is not None:
        scratch += [pltpu.VMEM((2, sub, lc), F32), pltpu.VMEM((2, sub, lc), BF16)]
    return pl.pallas_call(
        functools.partial(_attn_kernel, hp, sub, cache is not None),
        grid=(s, MLA_HEADS // hp, l // tq),
        in_specs=in_specs,
        out_specs=ospec,
        out_shape=jax.ShapeDtypeStruct((s, l, MLA_OUT), BF16),
        scratch_shapes=scratch,
        compiler_params=_cparams(("arbitrary", "arbitrary", "arbitrary")),
        name="attn",
    )(*args)


def _route(lt):
    row = lax.broadcasted_iota(jnp.int32, (SUB, lt.shape[1]), 0).astype(F32)
    neg = -jnp.inf
    gl = lt[0:SUB]
    g_ok = row < N_GROUPS
    glm = jnp.where(g_ok, gl, neg)
    gmax = jnp.max(glm, axis=0, keepdims=True)
    gsel = jnp.min(jnp.where(glm == gmax, row, float(SUB)), axis=0, keepdims=True)
    g_w = 1.0 / jnp.sum(jnp.where(g_ok, jnp.exp(gl - gmax), 0.0), axis=0, keepdims=True)
    es = lt[SUB:2 * SUB]
    for g in range(1, N_GROUPS):
        es = jnp.where(gsel == float(g), lt[(g + 1) * SUB:(g + 2) * SUB], es)
    m1 = jnp.max(es, axis=0, keepdims=True)
    i1 = jnp.min(jnp.where(es == m1, row, float(SUB)), axis=0, keepdims=True)
    es2 = jnp.where(row == i1, neg, es)
    m2 = jnp.max(es2, axis=0, keepdims=True)
    i2 = jnp.min(jnp.where(es2 == m2, row, float(SUB)), axis=0, keepdims=True)
    e2 = jnp.exp(m2 - m1)
    inv = g_w / (1.0 + e2)
    base = gsel * EXP_PER_GROUP
    return (jnp.where(row == RT_ID0, base + i1, 0.0) + jnp.where(row == RT_ID1, base + i2, 0.0)
            + jnp.where(row == RT_W0, inv, 0.0) + jnp.where(row == RT_W1, inv * e2, 0.0))


def _pack_bf16_pairs(x):
    k = x.shape[1] // 2
    bits = lax.bitcast_convert_type(x.astype(BF16).astype(F32), jnp.int32)
    return lax.shift_right_logical(bits[:, :k], 16) | (bits[:, k:] & HI_MASK)


def _unpack_bf16_pairs(w):
    lo = lax.bitcast_convert_type(lax.shift_left(w, 16), F32).astype(BF16)
    hi = lax.bitcast_convert_type(w & HI_MASK, F32).astype(BF16)
    return lo, hi


def _back_kernel(x_ref, mod_ref, g1_ref, g2_ref, yf_ref, yl_ref, ys_ref, o_ref,
                 wg_ref, wpf_ref, wpl_ref, wps_ref, wpm_ref, wo_ref, wrh_ref, wrl_ref, br_ref,
                 x1_ref, h2_ref, comb_ref):
    d = D_MODEL
    x = x_ref[...]
    sh1, sc1, gt1 = mod_ref[:, 0:d], mod_ref[:, d:2 * d], mod_ref[:, 2 * d:3 * d]
    sh2, sc2 = mod_ref[:, 3 * d:4 * d], mod_ref[:, 4 * d:5 * d]
    hb = (_rms(x, g1_ref[...]) * (1.0 + sc1) + sh1).astype(BF16)
    merged = None
    for j, (y_ref, w_ref) in enumerate(((yf_ref, wpf_ref), (yl_ref, wpl_ref), (ys_ref, wps_ref), (o_ref, wpm_ref))):
        gate = jax.nn.sigmoid(_bdot(hb, wg_ref[:, j * d:(j + 1) * d]))
        term = gate * _bdot(y_ref[...], w_ref[...])
        merged = term if merged is None else merged + term
    x1 = x + gt1 * _bdot(merged.astype(BF16), wo_ref[...])
    x1_ref[...] = x1
    h2 = _rms(x1, g2_ref[...]) * (1.0 + sc2) + sh2
    h2b = h2.astype(BF16)
    h2_ref[...] = _pack_bf16_pairs(h2)
    h2l = (h2 - h2b.astype(F32)).astype(BF16)
    nt = (((1,), (1,)), ((), ()))
    rdot = lambda w_ref, h: lax.dot_general(w_ref[...], h, nt, preferred_element_type=F32)
    logits_t = rdot(wrh_ref, h2b) + rdot(wrh_ref, h2l) + rdot(wrl_ref, h2b) + br_ref[...]
    comb_ref[...] = _route(logits_t)


def _back(x, mod, yf, yl, ys, o, p, tm):
    t = x.shape[0]
    per_mod = t // (mod.shape[0] * tm)
    row = lambda w: pl.BlockSpec((tm, w), lambda i: (i, 0))
    full = lambda a: pl.BlockSpec(a.shape, lambda i: (0,) * a.ndim, pipeline_mode=pl.Buffered(1))
    weights = [p["w_gate"], p["w_pf"], p["w_pl"], p["w_ps"], p["w_pm"], p["w_out"], p["wr_hi"], p["wr_lo"], p["b_r"]]
    return pl.pallas_call(
        _back_kernel,
        grid=(t // tm,),
        in_specs=[
            row(D_MODEL),
            pl.BlockSpec((None, 1, 6 * D_MODEL), lambda i: (i // per_mod, 0, 0)),
            pl.BlockSpec((1, D_MODEL), lambda i: (0, 0)),
            pl.BlockSpec((1, D_MODEL), lambda i: (0, 0)),
            row(FNET_W), row(LRU_W), row(SC_W), row(MLA_OUT),
        ] + [full(w) for w in weights],
        out_specs=[row(D_MODEL), row(PACK_W), pl.BlockSpec((SUB, tm), lambda i: (0, i))],
        out_shape=[
            jax.ShapeDtypeStruct((t, D_MODEL), F32),
            jax.ShapeDtypeStruct((t, PACK_W), jnp.int32),
            jax.ShapeDtypeStruct((SUB, t), F32),
        ],
        compiler_params=_cparams(("arbitrary",)),
        name="back",
    )(x, mod, p["norm1_g"], p["norm2_g"], yf, yl, ys, o, *weights)


def _plan_kernel(tm, rt_ref, pos_ref, te_ref, nu_ref):
    ids = jnp.concatenate([rt_ref[RT_ID0:RT_ID0 + 1, :], rt_ref[RT_ID1:RT_ID1 + 1, :]], axis=1)
    n_rows = ids.shape[1]
    erow = lax.broadcasted_iota(jnp.int32, (N_EXPERTS, PLAN_BLK), 0).astype(F32)
    col = lax.broadcasted_iota(jnp.int32, (N_EXPERTS, LANE), 0)
    blocks = [(ids[:, b * PLAN_BLK:(b + 1) * PLAN_BLK] == erow).astype(F32) for b in range(n_rows // PLAN_BLK)]
    tri = (lax.broadcasted_iota(jnp.int32, (PLAN_BLK, PLAN_BLK), 0)
           <= lax.broadcasted_iota(jnp.int32, (PLAN_BLK, PLAN_BLK), 1)).astype(BF16)
    prefix = [_bdot(blk.astype(BF16), tri) for blk in blocks]
    counts = prefix[0][:, PLAN_BLK - 1:PLAN_BLK]
    for pre in prefix[1:]:
        counts = counts + pre[:, PLAN_BLK - 1:PLAN_BLK]
    tiles = jnp.floor((counts + float(tm - 1)) * (1.0 / tm))
    tile_end = jnp.broadcast_to(tiles, (N_EXPERTS, LANE))
    for s in (1, 2, 4, 8, 16):
        tile_end = tile_end + jnp.where(col >= s, pltpu.roll(tile_end, s, 0), 0.0)
    base = (tile_end[:, 0:1] - tiles) * float(tm) - 1.0
    for b, (blk, pre) in enumerate(zip(blocks, prefix)):
        pos = jnp.sum(blk * (base + pre), axis=0, keepdims=True)
        pos_ref[:, b * PLAN_BLK:(b + 1) * PLAN_BLK] = pos.astype(jnp.int32)
        base = base + pre[:, PLAN_BLK - 1:PLAN_BLK]
    k = lax.broadcasted_iota(jnp.int32, (N_EXPERTS, LANE), 1).astype(F32)
    owner = jnp.sum((k >= tile_end).astype(F32), axis=0, keepdims=True)
    te_ref[...] = jnp.minimum(owner, float(N_EXPERTS - 1)).astype(jnp.int32)
    nu_ref[...] = jnp.broadcast_to(tile_end[N_EXPERTS - 1:N_EXPERTS, :], nu_ref.shape).astype(jnp.int32)


def _dispatch_plan(rt, tm):
    n_rows = TOP_K * rt.shape[1]
    n_tiles = n_rows // tm + N_EXPERTS
    assert n_tiles <= LANE and tm & (tm - 1) == 0 and n_rows % PLAN_BLK == 0
    pos, te, nu = pl.pallas_call(
        functools.partial(_plan_kernel, tm),
        out_shape=[jax.ShapeDtypeStruct((1, n_rows), jnp.int32), jax.ShapeDtypeStruct((1, LANE), jnp.int32),
                   jax.ShapeDtypeStruct((1, LANE), jnp.int32)],
        compiler_params=_cparams(None),
        name="plan",
    )(rt)
    return pos.reshape(n_rows), te[0, :n_tiles], nu[0, :1], n_tiles


def _sc_worker_rows(n_rows):
    workers = SC_CORES * SC_SUBCORES
    per_w = n_rows // workers
    assert per_w * workers == n_rows and per_w % SC_CHUNK == 0
    return per_w


def _sc_mesh():
    return plsc.VectorSubcoreMesh(core_axis_name="c", subcore_axis_name="s")


def _sc_scatter_rows(src, pos, n_out):
    t, w = src.shape
    per_w = _sc_worker_rows(pos.shape[0])

    @functools.partial(
        pl.kernel, mesh=_sc_mesh(), out_type=jax.ShapeDtypeStruct((n_out, w), src.dtype),
        scratch_types=[pltpu.VMEM((SC_CHUNK,), jnp.int32), pltpu.VMEM((SC_CHUNK, w), src.dtype)],
        name="moe_dispatch")
    def run(src_hbm, pos_hbm, out_hbm, idx_v, rows_v):
        wid = lax.axis_index("s") * SC_CORES + lax.axis_index("c")

        @pl.loop(0, per_w // SC_CHUNK)
        def _(i):
            j0 = wid * per_w + i * SC_CHUNK
            pltpu.sync_copy(pos_hbm.at[pl.ds(j0, SC_CHUNK)], idx_v)
            pltpu.sync_copy(src_hbm.at[pl.ds(lax.rem(j0, t), SC_CHUNK)], rows_v)
            pltpu.sync_copy(rows_v, out_hbm.at[idx_v])

    return run(src, pos)


def _sc_gather_rows(src, pos):
    w = src.shape[1]
    n = pos.shape[0]
    per_w = _sc_worker_rows(n)

    @functools.partial(
        pl.kernel, mesh=_sc_mesh(), out_type=jax.ShapeDtypeStruct((n, w), src.dtype),
        scratch_types=[pltpu.VMEM((SC_CHUNK,), jnp.int32), pltpu.VMEM((SC_CHUNK, w), src.dtype)],
        name="moe_return")
    def run(src_hbm, pos_hbm, out_hbm, idx_v, rows_v):
        wid = lax.axis_index("s") * SC_CORES + lax.axis_index("c")

        @pl.loop(0, per_w // SC_CHUNK)
        def _(i):
            j0 = wid * per_w + i * SC_CHUNK
            pltpu.sync_copy(pos_hbm.at[pl.ds(j0, SC_CHUNK)], idx_v)
            pltpu.sync_copy(src_hbm.at[idx_v], rows_v)
            pltpu.sync_copy(rows_v, out_hbm.at[pl.ds(j0, SC_CHUNK)])

    return run(src, pos)


def _experts_kernel(te_ref, nu_ref, x_ref, w1_ref, w3_ref, w2_ref, y_ref):
    half = D_MODEL // 2

    @pl.when(pl.program_id(0) < nu_ref[0])
    def _():
        lo, hi = _unpack_bf16_pairs(x_ref[...])
        a = _bdot(lo, w1_ref[:half, :].astype(BF16)) + _bdot(hi, w1_ref[half:, :].astype(BF16))
        u = _bdot(lo, w3_ref[:half, :].astype(BF16)) + _bdot(hi, w3_ref[half:, :].astype(BF16))
        mid = (a * jax.nn.sigmoid(a)) * u
        y_ref[...] = _pack_bf16_pairs(_bdot(mid.astype(BF16), w2_ref[...].astype(BF16)))


def _experts(xs, tile_expert, n_used, n_tiles, w1, w3, w2, layer, tm):
    wspec = lambda a: pl.BlockSpec((None, None) + a.shape[2:], lambda i, te, nu: (layer, te[i], 0, 0))
    grid_spec = pltpu.PrefetchScalarGridSpec(
        num_scalar_prefetch=2,
        grid=(n_tiles,),
        in_specs=[pl.BlockSpec((tm, PACK_W), lambda i, te, nu: (jnp.minimum(i, nu[0] - 1), 0)),
                  wspec(w1), wspec(w3), wspec(w2)],
        out_specs=pl.BlockSpec((tm, PACK_W), lambda i, te, nu: (jnp.where(i < nu[0], i, n_tiles), 0)),
    )
    return pl.pallas_call(
        _experts_kernel,
        grid_spec=grid_spec,
        out_shape=jax.ShapeDtypeStruct((xs.shape[0] + tm, PACK_W), jnp.int32),
        compiler_params=_cparams(("arbitrary",)),
        name="experts",
    )(tile_expert, n_used, xs, w1, w3, w2)


def _combine_kernel(x1_ref, mod_ref, rt_ref, g0_ref, g1_ref, o_ref):
    rt = rt_ref[...]
    lo0, hi0 = _unpack_bf16_pairs(g0_ref[...])
    lo1, hi1 = _unpack_bf16_pairs(g1_ref[...])
    w0 = rt[:, RT_W0:RT_W0 + 1]
    w1 = rt[:, RT_W1:RT_W1 + 1]
    half = D_MODEL // 2
    gt2 = mod_ref[:, 5 * D_MODEL:6 * D_MODEL]
    o_ref[:, :half] = x1_ref[:, :half] + gt2[:, :half] * (w0 * lo0.astype(F32) + w1 * lo1.astype(F32))
    o_ref[:, half:] = x1_ref[:, half:] + gt2[:, half:] * (w0 * hi0.astype(F32) + w1 * hi1.astype(F32))


def _combine(x1, mod, rt, g, tm):
    t = x1.shape[0]
    per_mod = t // (mod.shape[0] * tm)
    return pl.pallas_call(
        _combine_kernel,
        grid=(t // tm,),
        in_specs=[
            pl.BlockSpec((tm, D_MODEL), lambda i: (i, 0)),
            pl.BlockSpec((None, 1, 6 * D_MODEL), lambda i: (i // per_mod, 0, 0)),
            pl.BlockSpec((tm, LANE), lambda i: (i, 0)),
            pl.BlockSpec((tm, PACK_W), lambda i: (i, 0)),
            pl.BlockSpec((tm, PACK_W), lambda i: (i + t // tm, 0)),
        ],
        out_specs=pl.BlockSpec((tm, D_MODEL), lambda i: (i, 0)),
        out_shape=jax.ShapeDtypeStruct((t, D_MODEL), F32),
        compiler_params=_cparams(("arbitrary",)),
        name="combine",
    )(x1, mod, rt, g, g)


def _moe(h2p, rt, x1, mod, p, tm_e, tm_c):
    pos, tile_expert, n_used, n_tiles = _dispatch_plan(rt, tm_e)
    xs = _sc_scatter_rows(h2p, pos, n_tiles * tm_e)
    ys = _experts(xs, tile_expert, n_used, n_tiles, p["w1"], p["w3"], p["w2"], p["layer"], tm_e)
    g = _sc_gather_rows(ys, pos)
    rt_cols = jnp.pad(rt.T, ((0, 0), (0, LANE - rt.shape[0])))
    return _combine(x1, mod, rt_cols, g, tm_c)


def _channel_dft():
    n = np.arange(FNET_W // FNET_GROUPS)
    ang = 2.0 * np.pi * ((n[:, None] * n[None, :]) % n.size) / n.size
    eye = np.eye(FNET_GROUPS)
    scale = 1.0 / math.sqrt(n.size)
    return (jnp.asarray(np.kron(eye, np.cos(ang) * scale), F32).astype(BF16),
            jnp.asarray(np.kron(eye, np.sin(ang) * scale), F32).astype(BF16))


def _position_dft(l):
    scale = 1.0 / math.sqrt(l)
    if l <= 256:
        n = np.arange(l)
        ang = 2.0 * np.pi * ((n[:, None] * n[None, :]) % l) / l
        return jnp.asarray(np.cos(ang) * scale, F32).astype(BF16), jnp.asarray(np.sin(ang) * scale, F32).astype(BF16)
    m = DFT_SPLIT
    n = np.arange(l)
    ang_a = 2.0 * np.pi * ((np.arange(l // m)[:, None] * m * n[None, :]) % l) / l
    ang_b = 2.0 * np.pi * ((np.arange(m)[:, None] * n[None, :]) % l) / l
    return (jnp.asarray(np.cos(ang_a), F32), jnp.asarray(np.sin(ang_a), F32),
            jnp.asarray(np.cos(ang_b) * scale, F32), jnp.asarray(np.sin(ang_b) * scale, F32))


def _rope_tables(l, rotate):
    cos = np.ones((l, LANE))
    sinr = np.zeros((l, LANE))
    if rotate:
        t = np.arange(l)
        inv = ROPE_BASE ** (-np.arange(N_FREQ) / N_FREQ)
        ang = np.concatenate([(t // GRID_W)[:, None] * inv, (t % GRID_W)[:, None] * inv], axis=-1)
        half = ROPE // 2
        for lo in (ROPE_X1, ROPE_X2):
            cos[:, lo:lo + half] = np.cos(ang)
        sinr[:, ROPE_X1:ROPE_X1 + half] = -np.sin(ang)
        sinr[:, ROPE_X2:ROPE_X2 + half] = np.sin(ang)
    return tuple(jnp.asarray(a, F32) for a in (cos, sinr))


def _head_lane_source():
    half = ROPE // 2
    src = np.full((HEAD_PAD,), QK_DIM, np.int32)
    src[ROPE_X1:ROPE_X1 + half] = NOPE + np.arange(half)
    src[ROPE_X2:ROPE_X2 + half] = NOPE + half + np.arange(half)
    free = [i for i in range(HEAD_PAD) if src[i] == QK_DIM][:NOPE]
    src[free] = np.arange(NOPE)
    return src


def _place_head_dims(w, n_src):
    src = _head_lane_source()
    src = np.where(src < n_src, src, n_src)
    wz = jnp.concatenate([w[..., :n_src], jnp.zeros(w.shape[:-1] + (1,), w.dtype)], axis=-1)
    out = jnp.take(wz, jnp.asarray(src), axis=-1)
    return out.reshape(out.shape[:-2] + (out.shape[-2] * HEAD_PAD,))


def _place_rope_key(kr):
    half = ROPE // 2
    out = jnp.zeros(kr.shape[:-1] + (LANE,), kr.dtype)
    return out.at[..., ROPE_X1:ROPE_X1 + half].set(kr[..., :half]).at[..., ROPE_X2:ROPE_X2 + half].set(kr[..., half:])


def _take_rope_key(tile):
    half = ROPE // 2
    return jnp.concatenate([tile[..., ROPE_X1:ROPE_X1 + half], tile[..., ROPE_X2:ROPE_X2 + half]], axis=-1)


def _pad_heads(w, lo, hi):
    r = w.shape[0]
    part = w[:, :, lo:hi]
    out = jnp.zeros((r, MLA_HEADS, HEAD_PAD), w.dtype).at[:, :, : hi - lo].set(part)
    return out.reshape(r, HP)


def _blockdiag_halves(w):
    bw = LRU_W // LRU_BLOCKS
    out = jnp.zeros((2, 2, LANE, LANE), w.dtype)
    for half in range(2):
        for k in range(2):
            n = 2 * half + k
            out = out.at[half, :, k * bw:(k + 1) * bw, k * bw:(k + 1) * bw].set(w[:, n])
    return out


def _halves(v):
    return jnp.moveaxis(v.reshape(v.shape[:-1] + (2, LANE)), -2, 0)


def _layer_params(l, a):
    d = D_MODEL
    w_in = a["w_in"][l]
    w_in = jnp.concatenate([w_in[:, :Z_KR], _place_rope_key(w_in[:, Z_KR:])], axis=1)
    w_r = jnp.zeros((RT_ROWS, d), F32).at[:N_GROUPS].set(a["w_gr"][l].T).at[SUB:SUB + N_EXPERTS].set(a["w_er"][l].T)
    b_r = jnp.zeros((RT_ROWS, 1), F32).at[:N_GROUPS, 0].set(a["b_gr"][l]).at[SUB:SUB + N_EXPERTS, 0].set(a["b_er"][l])
    wr_hi = w_r.astype(BF16)
    gpad = lambda g: jnp.take(jnp.concatenate([g, jnp.zeros((1,), F32)]), jnp.asarray(_head_lane_source())).reshape(1, LANE)
    return {
        "norm1_g": a["norm1_g"][l].reshape(1, d), "norm2_g": a["norm2_g"][l].reshape(1, d),
        "w_in": w_in.astype(BF16),
        "lru_conv": _halves(a["lru_conv"][l]),
        "lru_wa": _blockdiag_halves(a["lru_wa"][l]).astype(BF16),
        "lru_wx": _blockdiag_halves(a["lru_wx"][l]).astype(BF16),
        "lru_ba": _halves(a["lru_ba"][l])[:, :, None, :], "lru_bx": _halves(a["lru_bx"][l])[:, :, None, :],
        "lru_lam": _halves(a["lru_lam"][l])[:, :, None, :],
        "sc_conv": _halves(a["sc_conv"][l]),
        "g_qa": a["g_qa"][l].reshape(1, Q_LORA), "g_kva": a["g_kva"][l].reshape(1, KV_LORA),
        "wq": _place_head_dims(a["w_qb"][l], QK_DIM).astype(BF16),
        "wk": _place_head_dims(a["w_kvb"][l], NOPE).astype(BF16),        "wv": _pad_heads(a["w_kvb"][l], NOPE, NOPE + V_DIM).astype(BF16),
        "gq": gpad(a["g_qn"][l]), "gk": gpad(a["g_kn"][l]),
        "w_gate": a["w_gate"][l].astype(BF16),
        "w_pf": a["w_pf"][l].astype(BF16), "w_pl": a["w_pl"][l].astype(BF16), "w_ps": a["w_ps"][l].astype(BF16),
        "w_pm": a["w_pm"][l].astype(BF16), "w_out": a["w_out"][l].astype(BF16),
        "wr_hi": wr_hi, "wr_lo": (w_r - wr_hi.astype(F32)).astype(BF16), "b_r": b_r,
        "w1": a["w1"], "w3": a["w3"], "w2": a["w2"], "layer": l,
    }


def _trunk_layer(x, mod, p, consts, h0, cache, cfg):
    s, l = cfg["s"], cfg["l"]
    z = _front(x, mod, p["norm1_g"], p["w_in"], cfg["tm_front"])
    z3 = z.reshape(s, l, Z_W)
    q, k, v, ckv = _qkv(z, consts["rope"], p, l, cfg["tm_qkv"])
    yf = _fourier(z3, consts["dft"], cfg["nb"], cfg["tq_f"])
    yl, ys, fin = _lru(z3, h0, p)
    o = _attn(q.reshape(s, l, HP), k.reshape(s, l, HP), v.reshape(s, l, HP), cache, cfg["hp"], cfg["tq_a"])
    t = s * l
    x1, h2, comb = _back(x, mod, yf.reshape(t, FNET_W), yl.reshape(t, LRU_W), ys.reshape(t, SC_W),
                         o.reshape(t, MLA_OUT), p, cfg["tm_back"])
    x2 = _moe(h2, comb, x1, mod, p, cfg["tm_moe"], cfg["tm_comb"])
    return x2, z3, ckv, fin


def kernel(x_prompt, x_sample, cache_ckv, cache_krope, state_rglru, c, c_ctx, norm1_g, norm2_g, w_ada, b_ada, w_in, lru_conv, lru_wa, lru_ba, lru_wx, lru_bx, lru_lam, sc_conv, g_qa, w_qb, g_kva, w_kvb, g_qn, g_kn, w_pf, w_pl, w_ps, w_pm, w_gate, w_out, w_gr, b_gr, w_er, b_er, w1, w3, w2):
    a = dict(norm1_g=norm1_g, norm2_g=norm2_g, w_in=w_in, lru_conv=lru_conv, lru_wa=lru_wa, lru_ba=lru_ba,
             lru_wx=lru_wx, lru_bx=lru_bx, lru_lam=lru_lam, sc_conv=sc_conv, g_qa=g_qa, w_qb=w_qb, g_kva=g_kva,
             w_kvb=w_kvb, g_qn=g_qn, g_kn=g_kn, w_pf=w_pf, w_pl=w_pl, w_ps=w_ps, w_pm=w_pm, w_gate=w_gate,
             w_out=w_out, w_gr=w_gr, b_gr=b_gr, w_er=w_er, b_er=b_er, w1=w1, w3=w3, w2=w2)
    bc, lc, d = x_prompt.shape
    bl, ll, _ = x_sample.shape
    past = cache_ckv.shape[2]

    cond8 = jnp.zeros((SUB, d), F32).at[0].set(c_ctx).at[1:1 + bl].set(c)
    mod_all = _ada(cond8, w_ada, b_ada)

    cc, sc = _channel_dft()
    ctx_consts = {"dft": (cc, sc) + _position_dft(lc), "rope": _rope_tables(lc, False)}
    lat_consts = {"dft": (cc, sc) + _position_dft(ll), "rope": _rope_tables(ll, True)}
    cache_tabs = _rope_tables(bl * past, False)
    ctx_cfg = dict(s=bc, l=lc, tm_front=512, tm_qkv=512, nb=8, tq_f=lc, hp=MLA_HEADS, tq_a=lc, tm_back=1024, tm_moe=512, tm_comb=512)
    lat_cfg = dict(s=bl, l=ll, tm_front=512, tm_qkv=512, nb=bl, tq_f=512, hp=MLA_HEADS, tq_a=256, tm_back=1024, tm_moe=512, tm_comb=512)

    xp = x_prompt.reshape(bc * lc, d)
    xs = x_sample.reshape(bl * ll, d)
    h0_ctx = jnp.zeros((bc, 2, LRU_W), F32)
    ckv_list, kr_list, lru_list = [], [], []
    for l in range(DEPTH):
        p = _layer_params(l, a)
        mod_ctx = mod_all[l, 0:1].reshape(1, 1, 6 * d)
        mod_lat = mod_all[l, 1:1 + bl].reshape(bl, 1, 6 * d)
        xp, z3, ckv, fin = _trunk_layer(xp, mod_ctx, p, ctx_consts, h0_ctx, None, ctx_cfg)
        ckv_list.append(ckv.reshape(bc, lc, KV_LORA))
        kr_list.append(_take_rope_key(z3[:, :, Z_KR:]))
        lru_list.append(fin)
        kr_tile = _place_rope_key(cache_krope[:, l].reshape(bl * past, ROPE))
        kc, vc = _kvcache(cache_ckv[:, l].reshape(bl * past, KV_LORA), kr_tile, cache_tabs, p)
        cache = (kc.reshape(bl, past, HP), vc.reshape(bl, past, HP))
        xs, _, _, _ = _trunk_layer(xs, mod_lat, p, lat_consts, state_rglru[:, l], cache, lat_cfg)
    return (xp.reshape(bc, lc, d), xs.reshape(bl, ll, d),
            jnp.stack(ckv_list, axis=1), jnp.stack(kr_list, axis=1), jnp.stack(lru_list, axis=1))
```

```python
import functools
import math

import numpy as np
import jax
import jax.numpy as jnp
from jax import lax
from jax.experimental import pallas as pl
from jax.experimental.pallas import tpu as pltpu
from jax.experimental.pallas import tpu_sc as plsc

F32 = jnp.float32
BF16 = jnp.bfloat16

D_MODEL = 1024
DEPTH = 2
GRID_W = 64
EPS = 1e-6
FNET_W = 256
FNET_GROUPS = 4
LRU_W = 256
LRU_BLOCKS = 4
LRU_C = 8.0
SC_W = 256
MLA_HEADS = 8
Q_LORA = 384
KV_LORA = 256
NOPE = 64
ROPE = 32
V_DIM = 64
QK_DIM = NOPE + ROPE
MLA_OUT = MLA_HEADS * V_DIM
N_FREQ = ROPE // 4
ROPE_BASE = 10000.0
ATTN_SCALE = QK_DIM ** -0.5
LOG2E = math.log2(math.e)
N_GROUPS = 4
EXP_PER_GROUP = 8
N_EXPERTS = N_GROUPS * EXP_PER_GROUP
D_EXPERT = 256

LANE = 128
SUB = 8
HEAD_PAD = LANE
HP = MLA_HEADS * HEAD_PAD
ROPE_X1 = 0
ROPE_X2 = LANE // 2
DEN_LANE = V_DIM
Z_QKV = 6 * 256
Z_KR = Z_QKV + Q_LORA + KV_LORA
Z_W = Z_KR + LANE
QKV_W = Q_LORA + KV_LORA + LANE
VMEM_LIMIT = 52 * 1024 * 1024
PACK_W = D_MODEL // 2
HI_MASK = -65536
RT_ID0, RT_ID1, RT_W0, RT_W1 = 0, 1, 2, 3
RT_ROWS = 48
TOP_K = 2
PLAN_BLK = 256
DFT_SPLIT = 64
ATTN_SUB = 256
SC_CORES = 2
SC_SUBCORES = 16
SC_CHUNK = 128


def _cparams(sem):
    return pltpu.CompilerParams(dimension_semantics=sem, vmem_limit_bytes=VMEM_LIMIT)


def _bdot(a, b):
    return jnp.dot(a, b, preferred_element_type=F32)


def _rms(x, g):
    return x * lax.rsqrt(jnp.mean(x * x, axis=-1, keepdims=True) + EPS) * g


def _ada_kernel(c_ref, w_ref, b_ref, o_ref):
    c = c_ref[...]
    s = (c * jax.nn.sigmoid(c)).astype(BF16)
    o_ref[...] = _bdot(s, w_ref[...].astype(BF16)) + b_ref[...]


def _ada(cond8, w_ada, b_ada):
    nblk = 6 * D_MODEL // 1024
    return pl.pallas_call(
        _ada_kernel,
        grid=(DEPTH, nblk),
        in_specs=[
            pl.BlockSpec((SUB, D_MODEL), lambda l, n: (0, 0)),
            pl.BlockSpec((None, D_MODEL, 1024), lambda l, n: (l, 0, n)),
            pl.BlockSpec((None, 1, 1024), lambda l, n: (l, 0, n)),
        ],
        out_specs=pl.BlockSpec((None, SUB, 1024), lambda l, n: (l, 0, n)),
        out_shape=jax.ShapeDtypeStruct((DEPTH, SUB, 6 * D_MODEL), F32),
        compiler_params=_cparams(("arbitrary", "arbitrary")),
        name="ada",
    )(cond8, w_ada, b_ada.reshape(DEPTH, 1, 6 * D_MODEL))


def _front_kernel(x_ref, mod_ref, g_ref, w_ref, z_ref):
    x = x_ref[...]
    sh = mod_ref[:, 0:D_MODEL]
    sc = mod_ref[:, D_MODEL:2 * D_MODEL]
    h = _rms(x, g_ref[...]) * (1.0 + sc) + sh
    z_ref[...] = _bdot(h.astype(BF16), w_ref[...])


def _front(x, mod, g1, w_in, tm):
    t = x.shape[0]
    per_mod = t // (mod.shape[0] * tm)
    return pl.pallas_call(
        _front_kernel,
        grid=(t // tm,),
        in_specs=[
            pl.BlockSpec((tm, D_MODEL), lambda i: (i, 0)),
            pl.BlockSpec((None, 1, 6 * D_MODEL), lambda i: (i // per_mod, 0, 0)),
            pl.BlockSpec((1, D_MODEL), lambda i: (0, 0)),
            pl.BlockSpec((D_MODEL, Z_W), lambda i: (0, 0)),
        ],
        out_specs=pl.BlockSpec((tm, Z_W), lambda i: (i, 0)),
        out_shape=jax.ShapeDtypeStruct((t, Z_W), F32),
        compiler_params=_cparams(("arbitrary",)),
        name="front",
    )(x, mod, g1, w_in)


def _rope(x, cos, sinr):
    return x * cos + pltpu.roll(x, LANE // 2, 1) * sinr


def _head_norm(x, g):
    ss = jnp.sum(x * x, axis=-1, keepdims=True) * (1.0 / QK_DIM)
    return x * lax.rsqrt(ss + EPS) * g


def _build_kv(ckv, kr_tile, cos, sinr, wk_ref, wv_ref, gk, k_ref, v_ref):
    cb = ckv.astype(BF16)
    kn = _bdot(cb, wk_ref[...])
    for h in range(MLA_HEADS):
        sl = slice(h * HEAD_PAD, (h + 1) * HEAD_PAD)
        k_ref[:, sl] = _rope(_head_norm(kn[:, sl] + kr_tile, gk), cos, sinr).astype(BF16)
    v = _bdot(cb, wv_ref[...])
    lane = lax.broadcasted_iota(jnp.int32, v.shape, 1)
    v_ref[...] = jnp.where((lane & (HEAD_PAD - 1)) == DEN_LANE, 1.0, v).astype(BF16)


def _qkv_kernel(z_ref, cos_ref, sinr_ref, gqa_ref, wq_ref, gkva_ref, wk_ref, wv_ref,
                gq_ref, gk_ref, q_ref, k_ref, v_ref, ckv_ref):
    cos, sinr = cos_ref[...], sinr_ref[...]
    q_c = z_ref[:, 0:Q_LORA]
    kv_c = z_ref[:, Q_LORA:Q_LORA + KV_LORA]
    kr_tile = z_ref[:, Q_LORA + KV_LORA:QKV_W]
    qf = _bdot(_rms(q_c, gqa_ref[...]).astype(BF16), wq_ref[...])
    gq = gq_ref[...] * (ATTN_SCALE * LOG2E)
    for h in range(MLA_HEADS):
        sl = slice(h * HEAD_PAD, (h + 1) * HEAD_PAD)
        q_ref[:, sl] = _rope(_head_norm(qf[:, sl], gq), cos, sinr).astype(BF16)
    ckv = _rms(kv_c, gkva_ref[...])
    ckv_ref[...] = ckv
    _build_kv(ckv, kr_tile, cos, sinr, wk_ref, wv_ref, gk_ref[...], k_ref, v_ref)


def _qkv(z, tabs, p, seq_len, tm):
    t = z.shape[0]
    if tm > seq_len:
        tabs = [jnp.tile(a, (tm // seq_len, 1)) for a in tabs]
    per_seq = max(seq_len // tm, 1)
    tab_spec = pl.BlockSpec((tm, LANE), lambda i: (i % per_seq, 0))
    full = lambda shape: pl.BlockSpec(shape, lambda i: (0,) * len(shape))
    return pl.pallas_call(
        _qkv_kernel,
        grid=(t // tm,),
        in_specs=[
            pl.BlockSpec((tm, QKV_W), lambda i: (i, Z_QKV // QKV_W)),
            tab_spec, tab_spec,
            full((1, Q_LORA)), full((Q_LORA, HP)), full((1, KV_LORA)),
            full((KV_LORA, HP)), full((KV_LORA, HP)), full((1, LANE)), full((1, LANE)),
        ],
        out_specs=[
            pl.BlockSpec((tm, HP), lambda i: (i, 0)),
            pl.BlockSpec((tm, HP), lambda i: (i, 0)),
            pl.BlockSpec((tm, HP), lambda i: (i, 0)),
            pl.BlockSpec((tm, KV_LORA), lambda i: (i, 0)),
        ],
        out_shape=[
            jax.ShapeDtypeStruct((t, HP), BF16),
            jax.ShapeDtypeStruct((t, HP), BF16),
            jax.ShapeDtypeStruct((t, HP), BF16),
            jax.ShapeDtypeStruct((t, KV_LORA), F32),
        ],
        compiler_params=_cparams(("arbitrary",)),
        name="qkv",
    )(z, tabs[0], tabs[1], p["g_qa"], p["wq"], p["g_kva"], p["wk"], p["wv"], p["gq"], p["gk"])


def _kvcache_kernel(ckv_ref, kr_ref, cos_ref, sinr_ref, wk_ref, wv_ref, gk_ref, k_ref, v_ref):
    _build_kv(ckv_ref[...], kr_ref[...], cos_ref[...], sinr_ref[...], wk_ref, wv_ref, gk_ref[...], k_ref, v_ref)


def _kvcache(ckv, kr_tile, tabs, p):
    t = ckv.shape[0]
    return pl.pallas_call(
        _kvcache_kernel,
        out_shape=[jax.ShapeDtypeStruct((t, HP), BF16), jax.ShapeDtypeStruct((t, HP), BF16)],
        compiler_params=_cparams(None),
        name="kvcache",
    )(ckv, kr_tile, tabs[0], tabs[1], p["wk"], p["wv"], p["gk"])


def _fourier_kernel(nb, u_ref, cc_ref, sc_ref, cl_ref, sl_ref, o_ref, a_scr, b_scr):
    @pl.when(pl.program_id(1) == 0)
    def _():
        for b in range(nb):
            u = u_ref[b].astype(BF16)
            a_scr[:, b * FNET_W:(b + 1) * FNET_W] = _bdot(u, cc_ref[...]).astype(BF16)
            b_scr[:, b * FNET_W:(b + 1) * FNET_W] = _bdot(u, sc_ref[...]).astype(BF16)

    y = _bdot(cl_ref[...], a_scr[...]) - _bdot(sl_ref[...], b_scr[...])
    for b in range(nb):
        o_ref[b] = y[:, b * FNET_W:(b + 1) * FNET_W].astype(BF16)


def _fourier_split_kernel(nb, u_ref, cc_ref, sc_ref, ca_ref, sa_ref, cb_ref, sb_ref, o_ref,
                          a_scr, b_scr, cl_scr, sl_scr):
    @pl.when(pl.program_id(1) == 0)
    def _():
        for b in range(nb):
            u = u_ref[b].astype(BF16)
            a_scr[:, b * FNET_W:(b + 1) * FNET_W] = _bdot(u, cc_ref[...]).astype(BF16)
            b_scr[:, b * FNET_W:(b + 1) * FNET_W] = _bdot(u, sc_ref[...]).astype(BF16)

    cb, sb = cb_ref[...], sb_ref[...]
    for j in range(ca_ref.shape[0]):
        ca, sa = ca_ref[j:j + 1, :], sa_ref[j:j + 1, :]
        cl_scr[j * DFT_SPLIT:(j + 1) * DFT_SPLIT, :] = (ca * cb - sa * sb).astype(BF16)
        sl_scr[j * DFT_SPLIT:(j + 1) * DFT_SPLIT, :] = (sa * cb + ca * sb).astype(BF16)
    y = _bdot(cl_scr[...], a_scr[...]) - _bdot(sl_scr[...], b_scr[...])
    for b in range(nb):
        o_ref[b] = y[:, b * FNET_W:(b + 1) * FNET_W].astype(BF16)


def _fourier_split(z3, mats, nb, tq):
    s, l, _ = z3.shape
    cc, sc, ca, sa, cb, sb = mats
    full = lambda a: pl.BlockSpec(a.shape, lambda i, r: (0, 0))
    rows = pl.BlockSpec((tq // DFT_SPLIT, l), lambda i, r: (r, 0))
    return pl.pallas_call(
        functools.partial(_fourier_split_kernel, nb),
        grid=(s // nb, l // tq),
        in_specs=[pl.BlockSpec((nb, l, FNET_W), lambda i, r: (i, 0, 0)), full(cc), full(sc), rows, rows,
                  full(cb), full(sb)],
        out_specs=pl.BlockSpec((nb, tq, FNET_W), lambda i, r: (i, r, 0)),
        out_shape=jax.ShapeDtypeStruct((s, l, FNET_W), BF16),
        scratch_shapes=[pltpu.VMEM((l, nb * FNET_W), BF16), pltpu.VMEM((l, nb * FNET_W), BF16),
                        pltpu.VMEM((tq, l), BF16), pltpu.VMEM((tq, l), BF16)],
        compiler_params=_cparams(("arbitrary", "arbitrary")),
        name="fourier_split",
    )(z3, cc, sc, ca, sa, cb, sb)


def _fourier(z3, mats, nb, tq):
    s, l, _ = z3.shape
    if len(mats) == 6:
        return _fourier_split(z3, mats, nb, tq)
    cc, sc, cl, sl = mats
    return pl.pallas_call(
        functools.partial(_fourier_kernel, nb),
        grid=(s // nb, l // tq),
        in_specs=[
            pl.BlockSpec((nb, l, FNET_W), lambda i, r: (i, 0, 0)),
            pl.BlockSpec((FNET_W, FNET_W), lambda i, r: (0, 0)),
            pl.BlockSpec((FNET_W, FNET_W), lambda i, r: (0, 0)),
            pl.BlockSpec((tq, l), lambda i, r: (r, 0)),
            pl.BlockSpec((tq, l), lambda i, r: (r, 0)),
        ],
        out_specs=pl.BlockSpec((nb, tq, FNET_W), lambda i, r: (i, r, 0)),
        out_shape=jax.ShapeDtypeStruct((s, l, FNET_W), BF16),
        scratch_shapes=[pltpu.VMEM((l, nb * FNET_W), BF16), pltpu.VMEM((l, nb * FNET_W), BF16)],
        compiler_params=_cparams(("arbitrary", "arbitrary")),
        name="fourier",
    )(z3, cc, sc, cl, sl)


def _load_ext(ref, c, rows, nchunks):
    t0 = pl.multiple_of(c * rows, rows)
    main = ref[pl.ds(t0, rows), :]
    lo = pl.multiple_of(jnp.maximum(t0 - SUB, 0), SUB)
    hi = pl.multiple_of(jnp.minimum(t0 + rows, (nchunks - 1) * rows + rows - SUB), SUB)
    prev = jnp.where(c > 0, ref[pl.ds(lo, SUB), :], 0.0)
    nxt = jnp.where(c < nchunks - 1, ref[pl.ds(hi, SUB), :], 0.0)
    return jnp.concatenate([prev, main, nxt], axis=0)


def _shifted(ext, off, rows):
    n = ext.shape[0]
    r = ext if off == 0 else pltpu.roll(ext, (-off) % n, 0)
    return r[SUB:SUB + rows]


def _gelu_tanh(x):
    return 0.5 * x * (1.0 + jnp.tanh(math.sqrt(2.0 / math.pi) * (x + 0.044715 * (x * x * x))))


def _lru_gates(xc, d, wa_ref, wx_ref, ba_ref, bx_ref, lam_ref):
    xb = xc.astype(BF16)
    r = jax.nn.sigmoid(_bdot(xb, wa_ref[d]) + ba_ref[d])
    i = jax.nn.sigmoid(_bdot(xb, wx_ref[d]) + bx_ref[d])
    nl = -lam_ref[d]
    softplus = jnp.maximum(nl, 0.0) + jnp.log1p(jnp.exp(-jnp.abs(nl)))
    la = (-LRU_C) * r * softplus
    a = jnp.exp(la)
    one_m_a2 = -jnp.tanh(la) * (a * a + 1.0)
    b = jnp.sqrt(one_m_a2) * (i * xc)
    return a, b


def _group_scan(a, b, reverse):
    rows = a.shape[0]
    rm = lax.broadcasted_iota(jnp.int32, a.shape, 0) & (SUB - 1)
    for s in (1, 2, 4):
        if reverse:
            sh, m = rows - s, rm + s <= SUB - 1
        else:
            sh, m = s, rm >= s
        a_sh = pltpu.roll(a, sh, 0)
        b_sh = pltpu.roll(b, sh, 0)
        b = jnp.where(m, a * b_sh + b, b)
        a = jnp.where(m, a * a_sh, a)
    return a, b


def _carry_scan(a, b, carry, reverse):
    ng = a.shape[0] // SUB
    out = [None] * ng
    order = range(ng - 1, -1, -1) if reverse else range(ng)
    for g in order:
        hg = a[g * SUB:(g + 1) * SUB] * carry + b[g * SUB:(g + 1) * SUB]
        out[g] = hg
        carry = hg[0:1] if reverse else hg[SUB - 1:SUB]
    return jnp.concatenate(out, axis=0), carry


def _lru_kernel(rows, nchunks, xl_ref, gl_ref, bs_ref, cs_ref, xs_ref, h0_ref, cw_ref, wa_ref, wx_ref,
                ba_ref, bx_ref, lam_ref, sw_ref, yl_ref, ys_ref, fin_ref, xc_scr, hf_scr, hb_scr):
    cw = cw_ref[...]
    sw = sw_ref[...]
    chunk = lambda c: pl.ds(pl.multiple_of(c * rows, rows), rows)

    def conv_body(c, carry):
        ext = _load_ext(xl_ref, c, rows, nchunks)
        xc = _shifted(ext, -2, rows) * cw[0:1]
        for k in range(1, 4):
            xc = xc + _shifted(ext, k - 2, rows) * cw[k:k + 1]
        xc_scr[chunk(c), :] = xc
        return carry

    lax.fori_loop(0, nchunks, conv_body, 0)

    def direction(c, d, carry, h_scr):
        a, b = _lru_gates(xc_scr[chunk(c), :], d, wa_ref, wx_ref, ba_ref, bx_ref, lam_ref)
        a, b = _group_scan(a, b, d == 1)
        h, carry = _carry_scan(a, b, carry, d == 1)
        h_scr[chunk(c), :] = h
        return carry

    def scan_body(j, carries):
        cf, cb = carries
        return direction(j, 0, cf, hf_scr), direction(nchunks - 1 - j, 1, cb, hb_scr)

    cf, cb = lax.fori_loop(0, nchunks, scan_body, (h0_ref[0:1, :], h0_ref[1:2, :]))
    fin_ref[0:1, :] = cf
    fin_ref[1:2, :] = cb

    def out_body(c, carry):
        y = (hf_scr[chunk(c), :] + hb_scr[chunk(c), :]) * _gelu_tanh(gl_ref[chunk(c), :])
        yl_ref[chunk(c), :] = y.astype(BF16)
        ext = _load_ext(cs_ref, c, rows, nchunks) * _load_ext(xs_ref, c, rows, nchunks)
        acc = _shifted(ext, -1, rows) * sw[0:1]
        acc = acc + _shifted(ext, 0, rows) * sw[1:2]
        acc = acc + _shifted(ext, 1, rows) * sw[2:3]
        ys_ref[chunk(c), :] = (bs_ref[chunk(c), :] * acc).astype(BF16)
        return carry

    lax.fori_loop(0, nchunks, out_body, 0)


def _lru(z3, h0, p):
    s, l, _ = z3.shape
    rows = min(l, 512)
    nchunks = l // rows
    half = lambda blk: pl.BlockSpec((None, l, LANE), lambda i, j: (i, 0, blk + j))
    wspec = lambda shape: pl.BlockSpec((None,) + shape, lambda i, j: (j,) + (0,) * len(shape))
    seq_half = pl.BlockSpec((None, l, LANE), lambda i, j: (i, 0, j))
    return pl.pallas_call(
        functools.partial(_lru_kernel, rows, nchunks),
        grid=(s, 2),
        in_specs=[
            half(2), half(4), half(6), half(8), half(10),
            pl.BlockSpec((None, 2, LANE), lambda i, j: (i, 0, j)),
            wspec((4, LANE)), wspec((2, LANE, LANE)), wspec((2, LANE, LANE)),
            wspec((2, 1, LANE)), wspec((2, 1, LANE)), wspec((2, 1, LANE)), wspec((3, LANE)),
        ],
        out_specs=[seq_half, seq_half, pl.BlockSpec((None, 2, LANE), lambda i, j: (i, 0, j))],
        out_shape=[jax.ShapeDtypeStruct((s, l, LRU_W), BF16), jax.ShapeDtypeStruct((s, l, SC_W), BF16),
                   jax.ShapeDtypeStruct((s, 2, LRU_W), F32)],
        scratch_shapes=[pltpu.VMEM((l, LANE), F32), pltpu.VMEM((l, LANE), F32), pltpu.VMEM((l, LANE), F32)],
        compiler_params=_cparams(("arbitrary", "arbitrary")),
        name="lru",
    )(z3, z3, z3, z3, z3, h0, p["lru_conv"], p["lru_wa"], p["lru_wx"], p["lru_ba"], p["lru_bx"], p["lru_lam"],
      p["sc_conv"])


def _store_head_pair(o_ref, rows, j, o_even, o_odd):
    lane = lax.broadcasted_iota(jnp.int32, o_even.shape, 1)
    pair = jnp.where(lane < V_DIM, o_even, pltpu.roll(o_odd, V_DIM, 1))
    o_ref[rows, j * LANE:(j + 1) * LANE] = pair.astype(BF16)


def _attn_kernel(hp, sub, has_cache, *refs):
    if has_cache:
        q_ref, k_ref, v_ref, kc_ref, vc_ref, o_ref, s_scr, p_scr, m_scr, sc_scr, pc_scr = refs
    else:
        q_ref, k_ref, v_ref, o_ref, s_scr, p_scr, m_scr = refs
    nt = (((1,), (1,)), ((), ()))
    items = [(slice(r * sub, (r + 1) * sub), slice(h * HEAD_PAD, (h + 1) * HEAD_PAD))
             for r in range(q_ref.shape[0] // sub) for h in range(hp)]
    n = len(items)

    def scores(i):
        rows, head = items[i]
        q = q_ref[rows, head]
        s = lax.dot_general(q, k_ref[:, head], nt, preferred_element_type=F32)
        m = jnp.max(s, axis=-1, keepdims=True)
        s_scr[i % 2] = s
        if has_cache:
            sc = lax.dot_general(q, kc_ref[:, head], nt, preferred_element_type=F32)
            m = jnp.maximum(m, jnp.max(sc, axis=-1, keepdims=True))
            sc_scr[i % 2] = sc
        m_scr[i % 2] = jnp.broadcast_to(m, m_scr.shape[1:])

    def probs(i):
        m = m_scr[i % 2][:, 0:1]
        p_scr[i % 2] = jnp.exp2(s_scr[i % 2] - m).astype(BF16)
        if has_cache:
            pc_scr[i % 2] = jnp.exp2(sc_scr[i % 2] - m).astype(BF16)

    pending = {}

    def weighted_values(i):
        rows, head = items[i]
        o = _bdot(p_scr[i % 2], v_ref[:, head])
        if has_cache:
            o = o + _bdot(pc_scr[i % 2], vc_ref[:, head])
        o = o / o[:, DEN_LANE:DEN_LANE + 1]
        h = i % hp
        if h % 2 == 0:
            pending[0] = o
        else:
            _store_head_pair(o_ref, rows, h // 2, pending.pop(0), o)

    scores(0)
    for i in range(n):
        if i + 1 < n:
            scores(i + 1)
        if i > 0:
            weighted_values(i - 1)
        probs(i)
    weighted_values(n - 1)


def _attn_short_kernel(hp, q_ref, k_ref, v_ref, o_ref):
    nt = (((1,), (1,)), ((), ()))
    heads = [slice(h * HEAD_PAD, (h + 1) * HEAD_PAD) for h in range(hp)]
    scores = [lax.dot_general(q_ref[:, hd], k_ref[:, hd], nt, preferred_element_type=F32) for hd in heads]
    probs = [jnp.exp2(s - jnp.max(s, axis=-1, keepdims=True)).astype(BF16) for s in scores]
    outs = []
    for hd, p in zip(heads, probs):
        o = _bdot(p, v_ref[:, hd])
        outs.append(o / o[:, DEN_LANE:DEN_LANE + 1])
    for j in range(hp // 2):
        _store_head_pair(o_ref, slice(None), j, outs[2 * j], outs[2 * j + 1])


def _attn(q, k, v, cache, hp, tq):
    s, l, _ = q.shape
    w = hp * HEAD_PAD
    qspec = pl.BlockSpec((None, tq, w), lambda i, h, r: (i, r, h))
    ospec = pl.BlockSpec((None, tq, hp * V_DIM), lambda i, h, r: (i, r, h))
    if cache is None and tq == l:
        kspec = pl.BlockSpec((None, l, w), lambda i, h, r: (i, 0, h))
        return pl.pallas_call(
            functools.partial(_attn_short_kernel, hp),
            grid=(s, MLA_HEADS // hp, 1),
            in_specs=[qspec, kspec, kspec],
            out_specs=ospec,
            out_shape=jax.ShapeDtypeStruct((s, l, MLA_OUT), BF16),
            compiler_params=_cparams(("arbitrary", "arbitrary", "arbitrary")),
            name="attn_short",
        )(q, k, v)
    kv_mode = dict(pipeline_mode=pl.Buffered(1))
    kspec = pl.BlockSpec((None, l, w), lambda i, h, r: (i, 0, h), **kv_mode)
    in_specs = [qspec, kspec, kspec]
    args = [q, k, v]
    if cache is not None:
        lc = cache[0].shape[1]
        cspec = pl.BlockSpec((None, lc, w), lambda i, h, r: (i, 0, h), **kv_mode)
        in_specs += [cspec, cspec]
        args += list(cache)
    sub = min(tq, ATTN_SUB)
    scratch = [pltpu.VMEM((2, sub, l), F32), pltpu.VMEM((2, sub, l), BF16), pltpu.VMEM((2, sub, LANE), F32)]
    if cache is not None:
        scratch += [pltpu.VMEM((2, sub, lc), F32), pltpu.VMEM((2, sub, lc), BF16)]
    return pl.pallas_call(
        functools.partial(_attn_kernel, hp, sub, cache is not None),
        grid=(s, MLA_HEADS // hp, l // tq),
        in_specs=in_specs,
        out_specs=ospec,
        out_shape=jax.ShapeDtypeStruct((s, l, MLA_OUT), BF16),
        scratch_shapes=scratch,
        compiler_params=_cparams(("arbitrary", "arbitrary", "arbitrary")),
        name="attn",
    )(*args)


def _route(lt):
    row = lax.broadcasted_iota(jnp.int32, (SUB, lt.shape[1]), 0).astype(F32)
    neg = -jnp.inf
    gl = lt[0:SUB]
    g_ok = row < N_GROUPS
    glm = jnp.where(g_ok, gl, neg)
    gmax = jnp.max(glm, axis=0, keepdims=True)
    gsel = jnp.min(jnp.where(glm == gmax, row, float(SUB)), axis=0, keepdims=True)
    g_w = 1.0 / jnp.sum(jnp.where(g_ok, jnp.exp(gl - gmax), 0.0), axis=0, keepdims=True)
    es = lt[SUB:2 * SUB]
    for g in range(1, N_GROUPS):
        es = jnp.where(gsel == float(g), lt[(g + 1) * SUB:(g + 2) * SUB], es)
    m1 = jnp.max(es, axis=0, keepdims=True)
    i1 = jnp.min(jnp.where(es == m1, row, float(SUB)), axis=0, keepdims=True)
    es2 = jnp.where(row == i1, neg, es)
    m2 = jnp.max(es2, axis=0, keepdims=True)
    i2 = jnp.min(jnp.where(es2 == m2, row, float(SUB)), axis=0, keepdims=True)
    e2 = jnp.exp(m2 - m1)
    inv = g_w / (1.0 + e2)
    base = gsel * EXP_PER_GROUP
    return (jnp.where(row == RT_ID0, base + i1, 0.0) + jnp.where(row == RT_ID1, base + i2, 0.0)
            + jnp.where(row == RT_W0, inv, 0.0) + jnp.where(row == RT_W1, inv * e2, 0.0))


def _pack_bf16_pairs(x):
    k = x.shape[1] // 2
    bits = lax.bitcast_convert_type(x.astype(BF16).astype(F32), jnp.int32)
    return lax.shift_right_logical(bits[:, :k], 16) | (bits[:, k:] & HI_MASK)


def _unpack_bf16_pairs(w):
    lo = lax.bitcast_convert_type(lax.shift_left(w, 16), F32).astype(BF16)
    hi = lax.bitcast_convert_type(w & HI_MASK, F32).astype(BF16)
    return lo, hi


def _back_kernel(x_ref, mod_ref, g1_ref, g2_ref, yf_ref, yl_ref, ys_ref, o_ref,
                 wg_ref, wpf_ref, wpl_ref, wps_ref, wpm_ref, wo_ref, wrh_ref, wrl_ref, br_ref,
                 x1_ref, h2_ref, comb_ref):
    d = D_MODEL
    x = x_ref[...]
    sh1, sc1, gt1 = mod_ref[:, 0:d], mod_ref[:, d:2 * d], mod_ref[:, 2 * d:3 * d]
    sh2, sc2 = mod_ref[:, 3 * d:4 * d], mod_ref[:, 4 * d:5 * d]
    hb = (_rms(x, g1_ref[...]) * (1.0 + sc1) + sh1).astype(BF16)
    merged = None
    for j, (y_ref, w_ref) in enumerate(((yf_ref, wpf_ref), (yl_ref, wpl_ref), (ys_ref, wps_ref), (o_ref, wpm_ref))):
        gate = jax.nn.sigmoid(_bdot(hb, wg_ref[:, j * d:(j + 1) * d]))
        term = gate * _bdot(y_ref[...], w_ref[...])
        merged = term if merged is None else merged + term
    x1 = x + gt1 * _bdot(merged.astype(BF16), wo_ref[...])
    x1_ref[...] = x1
    h2 = _rms(x1, g2_ref[...]) * (1.0 + sc2) + sh2
    h2b = h2.astype(BF16)
    h2_ref[...] = _pack_bf16_pairs(h2)
    h2l = (h2 - h2b.astype(F32)).astype(BF16)
    nt = (((1,), (1,)), ((), ()))
    rdot = lambda w_ref, h: lax.dot_general(w_ref[...], h, nt, preferred_element_type=F32)
    logits_t = rdot(wrh_ref, h2b) + rdot(wrh_ref, h2l) + rdot(wrl_ref, h2b) + br_ref[...]
    comb_ref[...] = _route(logits_t)


def _back(x, mod, yf, yl, ys, o, p, tm):
    t = x.shape[0]
    per_mod = t // (mod.shape[0] * tm)
    row = lambda w: pl.BlockSpec((tm, w), lambda i: (i, 0))
    full = lambda a: pl.BlockSpec(a.shape, lambda i: (0,) * a.ndim, pipeline_mode=pl.Buffered(1))
    weights = [p["w_gate"], p["w_pf"], p["w_pl"], p["w_ps"], p["w_pm"], p["w_out"], p["wr_hi"], p["wr_lo"], p["b_r"]]
    return pl.pallas_call(
        _back_kernel,
        grid=(t // tm,),
        in_specs=[
            row(D_MODEL),
            pl.BlockSpec((None, 1, 6 * D_MODEL), lambda i: (i // per_mod, 0, 0)),
            pl.BlockSpec((1, D_MODEL), lambda i: (0, 0)),
            pl.BlockSpec((1, D_MODEL), lambda i: (0, 0)),
            row(FNET_W), row(LRU_W), row(SC_W), row(MLA_OUT),
        ] + [full(w) for w in weights],
        out_specs=[row(D_MODEL), row(PACK_W), pl.BlockSpec((SUB, tm), lambda i: (0, i))],
        out_shape=[
            jax.ShapeDtypeStruct((t, D_MODEL), F32),
            jax.ShapeDtypeStruct((t, PACK_W), jnp.int32),
            jax.ShapeDtypeStruct((SUB, t), F32),
        ],
        compiler_params=_cparams(("arbitrary",)),
        name="back",
    )(x, mod, p["norm1_g"], p["norm2_g"], yf, yl, ys, o, *weights)


def _plan_kernel(tm, rt_ref, pos_ref, te_ref, nu_ref):
    ids = jnp.concatenate([rt_ref[RT_ID0:RT_ID0 + 1, :], rt_ref[RT_ID1:RT_ID1 + 1, :]], axis=1)
    n_rows = ids.shape[1]
    erow = lax.broadcasted_iota(jnp.int32, (N_EXPERTS, PLAN_BLK), 0).astype(F32)
    col = lax.broadcasted_iota(jnp.int32, (N_EXPERTS, LANE), 0)
    blocks = [(ids[:, b * PLAN_BLK:(b + 1) * PLAN_BLK] == erow).astype(F32) for b in range(n_rows // PLAN_BLK)]
    tri = (lax.broadcasted_iota(jnp.int32, (PLAN_BLK, PLAN_BLK), 0)
           <= lax.broadcasted_iota(jnp.int32, (PLAN_BLK, PLAN_BLK), 1)).astype(BF16)
    prefix = [_bdot(blk.astype(BF16), tri) for blk in blocks]
    counts = prefix[0][:, PLAN_BLK - 1:PLAN_BLK]
    for pre in prefix[1:]:
        counts = counts + pre[:, PLAN_BLK - 1:PLAN_BLK]
    tiles = jnp.floor((counts + float(tm - 1)) * (1.0 / tm))
    tile_end = jnp.broadcast_to(tiles, (N_EXPERTS, LANE))
    for s in (1, 2, 4, 8, 16):
        tile_end = tile_end + jnp.where(col >= s, pltpu.roll(tile_end, s, 0), 0.0)
    base = (tile_end[:, 0:1] - tiles) * float(tm) - 1.0
    for b, (blk, pre) in enumerate(zip(blocks, prefix)):
        pos = jnp.sum(blk * (base + pre), axis=0, keepdims=True)
        pos_ref[:, b * PLAN_BLK:(b + 1) * PLAN_BLK] = pos.astype(jnp.int32)
        base = base + pre[:, PLAN_BLK - 1:PLAN_BLK]
    k = lax.broadcasted_iota(jnp.int32, (N_EXPERTS, LANE), 1).astype(F32)
    owner = jnp.sum((k >= tile_end).astype(F32), axis=0, keepdims=True)
    te_ref[...] = jnp.minimum(owner, float(N_EXPERTS - 1)).astype(jnp.int32)
    nu_ref[...] = jnp.broadcast_to(tile_end[N_EXPERTS - 1:N_EXPERTS, :], nu_ref.shape).astype(jnp.int32)


def _dispatch_plan(rt, tm):
    n_rows = TOP_K * rt.shape[1]
    n_tiles = n_rows // tm + N_EXPERTS
    assert n_tiles <= LANE and tm & (tm - 1) == 0 and n_rows % PLAN_BLK == 0
    pos, te, nu = pl.pallas_call(
        functools.partial(_plan_kernel, tm),
        out_shape=[jax.ShapeDtypeStruct((1, n_rows), jnp.int32), jax.ShapeDtypeStruct((1, LANE), jnp.int32),
                   jax.ShapeDtypeStruct((1, LANE), jnp.int32)],
        compiler_params=_cparams(None),
        name="plan",
    )(rt)
    return pos.reshape(n_rows), te[0, :n_tiles], nu[0, :1], n_tiles


def _sc_worker_rows(n_rows):
    workers = SC_CORES * SC_SUBCORES
    per_w = n_rows // workers
    assert per_w * workers == n_rows and per_w % SC_CHUNK == 0
    return per_w


def _sc_mesh():
    return plsc.VectorSubcoreMesh(core_axis_name="c", subcore_axis_name="s",
                                  num_cores=SC_CORES, num_subcores=SC_SUBCORES)


def _sc_scatter_rows(src, pos, n_out):
    t, w = src.shape
    per_w = _sc_worker_rows(pos.shape[0])

    @functools.partial(
        pl.kernel, mesh=_sc_mesh(), out_type=jax.ShapeDtypeStruct((n_out, w), src.dtype),
        scratch_types=[pltpu.VMEM((SC_CHUNK,), jnp.int32), pltpu.VMEM((SC_CHUNK, w), src.dtype)],
        name="moe_dispatch")
    def run(src_hbm, pos_hbm, out_hbm, idx_v, rows_v):
        wid = lax.axis_index("s") * SC_CORES + lax.axis_index("c")

        @pl.loop(0, per_w // SC_CHUNK)
        def _(i):
            j0 = wid * per_w + i * SC_CHUNK
            pltpu.sync_copy(pos_hbm.at[pl.ds(j0, SC_CHUNK)], idx_v)
            pltpu.sync_copy(src_hbm.at[pl.ds(lax.rem(j0, t), SC_CHUNK)], rows_v)
            pltpu.sync_copy(rows_v, out_hbm.at[idx_v])

    return run(src, pos)


def _sc_gather_rows(src, pos):
    w = src.shape[1]
    n = pos.shape[0]
    per_w = _sc_worker_rows(n)

    @functools.partial(
        pl.kernel, mesh=_sc_mesh(), out_type=jax.ShapeDtypeStruct((n, w), src.dtype),
        scratch_types=[pltpu.VMEM((SC_CHUNK,), jnp.int32), pltpu.VMEM((SC_CHUNK, w), src.dtype)],
        name="moe_return")
    def run(src_hbm, pos_hbm, out_hbm, idx_v, rows_v):
        wid = lax.axis_index("s") * SC_CORES + lax.axis_index("c")

        @pl.loop(0, per_w // SC_CHUNK)
        def _(i):
            j0 = wid * per_w + i * SC_CHUNK
            pltpu.sync_copy(pos_hbm.at[pl.ds(j0, SC_CHUNK)], idx_v)
            pltpu.sync_copy(src_hbm.at[idx_v], rows_v)
            pltpu.sync_copy(rows_v, out_hbm.at[pl.ds(j0, SC_CHUNK)])

    return run(src, pos)


def _experts_kernel(te_ref, nu_ref, x_ref, w1_ref, w3_ref, w2_ref, y_ref):
    half = D_MODEL // 2

    @pl.when(pl.program_id(0) < nu_ref[0])
    def _():
        lo, hi = _unpack_bf16_pairs(x_ref[...])
        a = _bdot(lo, w1_ref[:half, :].astype(BF16)) + _bdot(hi, w1_ref[half:, :].astype(BF16))
        u = _bdot(lo, w3_ref[:half, :].astype(BF16)) + _bdot(hi, w3_ref[half:, :].astype(BF16))
        mid = (a * jax.nn.sigmoid(a)) * u
        y_ref[...] = _pack_bf16_pairs(_bdot(mid.astype(BF16), w2_ref[...].astype(BF16)))


def _experts(xs, tile_expert, n_used, n_tiles, w1, w3, w2, layer, tm):
    wspec = lambda a: pl.BlockSpec((None, None) + a.shape[2:], lambda i, te, nu: (layer, te[i], 0, 0))
    grid_spec = pltpu.PrefetchScalarGridSpec(
        num_scalar_prefetch=2,
        grid=(n_tiles,),
        in_specs=[pl.BlockSpec((tm, PACK_W), lambda i, te, nu: (jnp.minimum(i, nu[0] - 1), 0)),
                  wspec(w1), wspec(w3), wspec(w2)],
        out_specs=pl.BlockSpec((tm, PACK_W), lambda i, te, nu: (jnp.where(i < nu[0], i, n_tiles), 0)),
    )
    return pl.pallas_call(
        _experts_kernel,
        grid_spec=grid_spec,
        out_shape=jax.ShapeDtypeStruct((xs.shape[0] + tm, PACK_W), jnp.int32),
        compiler_params=_cparams(("arbitrary",)),
        name="experts",
    )(tile_expert, n_used, xs, w1, w3, w2)


def _combine_kernel(x1_ref, mod_ref, rt_ref, g0_ref, g1_ref, o_ref):
    rt = rt_ref[...]
    lo0, hi0 = _unpack_bf16_pairs(g0_ref[...])
    lo1, hi1 = _unpack_bf16_pairs(g1_ref[...])
    w0 = rt[:, RT_W0:RT_W0 + 1]
    w1 = rt[:, RT_W1:RT_W1 + 1]
    half = D_MODEL // 2
    gt2 = mod_ref[:, 5 * D_MODEL:6 * D_MODEL]
    o_ref[:, :half] = x1_ref[:, :half] + gt2[:, :half] * (w0 * lo0.astype(F32) + w1 * lo1.astype(F32))
    o_ref[:, half:] = x1_ref[:, half:] + gt2[:, half:] * (w0 * hi0.astype(F32) + w1 * hi1.astype(F32))


def _combine(x1, mod, rt, g, tm):
    t = x1.shape[0]
    per_mod = t // (mod.shape[0] * tm)
    return pl.pallas_call(
        _combine_kernel,
        grid=(t // tm,),
        in_specs=[
            pl.BlockSpec((tm, D_MODEL), lambda i: (i, 0)),
            pl.BlockSpec((None, 1, 6 * D_MODEL), lambda i: (i // per_mod, 0, 0)),
            pl.BlockSpec((tm, LANE), lambda i: (i, 0)),
            pl.BlockSpec((tm, PACK_W), lambda i: (i, 0)),
            pl.BlockSpec((tm, PACK_W), lambda i: (i + t // tm, 0)),
        ],
        out_specs=pl.BlockSpec((tm, D_MODEL), lambda i: (i, 0)),
        out_shape=jax.ShapeDtypeStruct((t, D_MODEL), F32),
        compiler_params=_cparams(("arbitrary",)),
        name="combine",
    )(x1, mod, rt, g, g)


def _moe(h2p, rt, x1, mod, p, tm_e, tm_c):
    pos, tile_expert, n_used, n_tiles = _dispatch_plan(rt, tm_e)
    xs = _sc_scatter_rows(h2p, pos, n_tiles * tm_e)
    ys = _experts(xs, tile_expert, n_used, n_tiles, p["w1"], p["w3"], p["w2"], p["layer"], tm_e)
    g = _sc_gather_rows(ys, pos)
    rt_cols = jnp.pad(rt.T, ((0, 0), (0, LANE - rt.shape[0])))
    return _combine(x1, mod, rt_cols, g, tm_c)


def _channel_dft():
    n = np.arange(FNET_W // FNET_GROUPS)
    ang = 2.0 * np.pi * ((n[:, None] * n[None, :]) % n.size) / n.size
    eye = np.eye(FNET_GROUPS)
    scale = 1.0 / math.sqrt(n.size)
    return (jnp.asarray(np.kron(eye, np.cos(ang) * scale), F32).astype(BF16),
            jnp.asarray(np.kron(eye, np.sin(ang) * scale), F32).astype(BF16))


def _position_dft(l):
    scale = 1.0 / math.sqrt(l)
    if l <= 256:
        n = np.arange(l)
        ang = 2.0 * np.pi * ((n[:, None] * n[None, :]) % l) / l
        return jnp.asarray(np.cos(ang) * scale, F32).astype(BF16), jnp.asarray(np.sin(ang) * scale, F32).astype(BF16)
    m = DFT_SPLIT
    n = np.arange(l)
    ang_a = 2.0 * np.pi * ((np.arange(l // m)[:, None] * m * n[None, :]) % l) / l
    ang_b = 2.0 * np.pi * ((np.arange(m)[:, None] * n[None, :]) % l) / l
    return (jnp.asarray(np.cos(ang_a), F32), jnp.asarray(np.sin(ang_a), F32),
            jnp.asarray(np.cos(ang_b) * scale, F32), jnp.asarray(np.sin(ang_b) * scale, F32))


def _rope_tables(l, rotate):
    cos = np.ones((l, LANE))
    sinr = np.zeros((l, LANE))
    if rotate:
        t = np.arange(l)
        inv = ROPE_BASE ** (-np.arange(N_FREQ) / N_FREQ)
        ang = np.concatenate([(t // GRID_W)[:, None] * inv, (t % GRID_W)[:, None] * inv], axis=-1)
        half = ROPE // 2
        for lo in (ROPE_X1, ROPE_X2):
            cos[:, lo:lo + half] = np.cos(ang)
        sinr[:, ROPE_X1:ROPE_X1 + half] = -np.sin(ang)
        sinr[:, ROPE_X2:ROPE_X2 + half] = np.sin(ang)
    return tuple(jnp.asarray(a, F32) for a in (cos, sinr))


def _head_lane_source():
    half = ROPE // 2
    src = np.full((HEAD_PAD,), QK_DIM, np.int32)
    src[ROPE_X1:ROPE_X1 + half] = NOPE + np.arange(half)
    src[ROPE_X2:ROPE_X2 + half] = NOPE + half + np.arange(half)
    free = [i for i in range(HEAD_PAD) if src[i] == QK_DIM][:NOPE]
    src[free] = np.arange(NOPE)
    return src


def _place_head_dims(w, n_src):
    src = _head_lane_source()
    src = np.where(src < n_src, src, n_src)
    wz = jnp.concatenate([w[..., :n_src], jnp.zeros(w.shape[:-1] + (1,), w.dtype)], axis=-1)
    out = jnp.take(wz, jnp.asarray(src), axis=-1)
    return out.reshape(out.shape[:-2] + (out.shape[-2] * HEAD_PAD,))


def _place_rope_key(kr):
    half = ROPE // 2
    out = jnp.zeros(kr.shape[:-1] + (LANE,), kr.dtype)
    return out.at[..., ROPE_X1:ROPE_X1 + half].set(kr[..., :half]).at[..., ROPE_X2:ROPE_X2 + half].set(kr[..., half:])


def _take_rope_key(tile):
    half = ROPE // 2
    return jnp.concatenate([tile[..., ROPE_X1:ROPE_X1 + half], tile[..., ROPE_X2:ROPE_X2 + half]], axis=-1)


def _pad_heads(w, lo, hi):
    r = w.shape[0]
    part = w[:, :, lo:hi]
    out = jnp.zeros((r, MLA_HEADS, HEAD_PAD), w.dtype).at[:, :, : hi - lo].set(part)
    return out.reshape(r, HP)


def _blockdiag_halves(w):
    bw = LRU_W // LRU_BLOCKS
    out = jnp.zeros((2, 2, LANE, LANE), w.dtype)
    for half in range(2):
        for k in range(2):
            n = 2 * half + k
            out = out.at[half, :, k * bw:(k + 1) * bw, k * bw:(k + 1) * bw].set(w[:, n])
    return out


def _halves(v):
    return jnp.moveaxis(v.reshape(v.shape[:-1] + (2, LANE)), -2, 0)


def _layer_params(l, a):
    d = D_MODEL
    w_in = a["w_in"][l]
    w_in = jnp.concatenate([w_in[:, :Z_KR], _place_rope_key(w_in[:, Z_KR:])], axis=1)
    w_r = jnp.zeros((RT_ROWS, d), F32).at[:N_GROUPS].set(a["w_gr"][l].T).at[SUB:SUB + N_EXPERTS].set(a["w_er"][l].T)
    b_r = jnp.zeros((RT_ROWS, 1), F32).at[:N_GROUPS, 0].set(a["b_gr"][l]).at[SUB:SUB + N_EXPERTS, 0].set(a["b_er"][l])
    wr_hi = w_r.astype(BF16)
    gpad = lambda g: jnp.take(jnp.concatenate([g, jnp.zeros((1,), F32)]), jnp.asarray(_head_lane_source())).reshape(1, LANE)
    return {
        "norm1_g": a["norm1_g"][l].reshape(1, d), "norm2_g": a["norm2_g"][l].reshape(1, d),
        "w_in": w_in.astype(BF16),
        "lru_conv": _halves(a["lru_conv"][l]),
        "lru_wa": _blockdiag_halves(a["lru_wa"][l]).astype(BF16),
        "lru_wx": _blockdiag_halves(a["lru_wx"][l]).astype(BF16),
        "lru_ba": _halves(a["lru_ba"][l])[:, :, None, :], "lru_bx": _halves(a["lru_bx"][l])[:, :, None, :],
        "lru_lam": _halves(a["lru_lam"][l])[:, :, None, :],
        "sc_conv": _halves(a["sc_conv"][l]),
        "g_qa": a["g_qa"][l].reshape(1, Q_LORA), "g_kva": a["g_kva"][l].reshape(1, KV_LORA),
        "wq": _place_head_dims(a["w_qb"][l], QK_DIM).astype(BF16),
        "wk": _place_head_dims(a["w_kvb"][l], NOPE).astype(BF16),        "wv": _pad_heads(a["w_kvb"][l], NOPE, NOPE + V_DIM).astype(BF16),
        "gq": gpad(a["g_qn"][l]), "gk": gpad(a["g_kn"][l]),
        "w_gate": a["w_gate"][l].astype(BF16),
        "w_pf": a["w_pf"][l].astype(BF16), "w_pl": a["w_pl"][l].astype(BF16), "w_ps": a["w_ps"][l].astype(BF16),
        "w_pm": a["w_pm"][l].astype(BF16), "w_out": a["w_out"][l].astype(BF16),
        "wr_hi": wr_hi, "wr_lo": (w_r - wr_hi.astype(F32)).astype(BF16), "b_r": b_r,
        "w1": a["w1"], "w3": a["w3"], "w2": a["w2"], "layer": l,
    }


def _trunk_layer(x, mod, p, consts, h0, cache, cfg):
    s, l = cfg["s"], cfg["l"]
    z = _front(x, mod, p["norm1_g"], p["w_in"], cfg["tm_front"])
    z3 = z.reshape(s, l, Z_W)
    q, k, v, ckv = _qkv(z, consts["rope"], p, l, cfg["tm_qkv"])
    yf = _fourier(z3, consts["dft"], cfg["nb"], cfg["tq_f"])
    yl, ys, fin = _lru(z3, h0, p)
    o = _attn(q.reshape(s, l, HP), k.reshape(s, l, HP), v.reshape(s, l, HP), cache, cfg["hp"], cfg["tq_a"])
    t = s * l
    x1, h2, comb = _back(x, mod, yf.reshape(t, FNET_W), yl.reshape(t, LRU_W), ys.reshape(t, SC_W),
                         o.reshape(t, MLA_OUT), p, cfg["tm_back"])
    x2 = _moe(h2, comb, x1, mod, p, cfg["tm_moe"], cfg["tm_comb"])
    return x2, z3, ckv, fin


def kernel(x_prompt, x_sample, cache_ckv, cache_krope, state_rglru, c, c_ctx, norm1_g, norm2_g, w_ada, b_ada, w_in, lru_conv, lru_wa, lru_ba, lru_wx, lru_bx, lru_lam, sc_conv, g_qa, w_qb, g_kva, w_kvb, g_qn, g_kn, w_pf, w_pl, w_ps, w_pm, w_gate, w_out, w_gr, b_gr, w_er, b_er, w1, w3, w2):
    a = dict(norm1_g=norm1_g, norm2_g=norm2_g, w_in=w_in, lru_conv=lru_conv, lru_wa=lru_wa, lru_ba=lru_ba,
             lru_wx=lru_wx, lru_bx=lru_bx, lru_lam=lru_lam, sc_conv=sc_conv, g_qa=g_qa, w_qb=w_qb, g_kva=g_kva,
             w_kvb=w_kvb, g_qn=g_qn, g_kn=g_kn, w_pf=w_pf, w_pl=w_pl, w_ps=w_ps, w_pm=w_pm, w_gate=w_gate,
             w_out=w_out, w_gr=w_gr, b_gr=b_gr, w_er=w_er, b_er=b_er, w1=w1, w3=w3, w2=w2)
    bc, lc, d = x_prompt.shape
    bl, ll, _ = x_sample.shape
    past = cache_ckv.shape[2]

    cond8 = jnp.zeros((SUB, d), F32).at[0].set(c_ctx).at[1:1 + bl].set(c)
    mod_all = _ada(cond8, w_ada, b_ada)

    cc, sc = _channel_dft()
    ctx_consts = {"dft": (cc, sc) + _position_dft(lc), "rope": _rope_tables(lc, False)}
    lat_consts = {"dft": (cc, sc) + _position_dft(ll), "rope": _rope_tables(ll, True)}
    cache_tabs = _rope_tables(bl * past, False)
    ctx_cfg = dict(s=bc, l=lc, tm_front=512, tm_qkv=512, nb=8, tq_f=lc, hp=MLA_HEADS, tq_a=lc, tm_back=1024, tm_moe=512, tm_comb=512)
    lat_cfg = dict(s=bl, l=ll, tm_front=512, tm_qkv=512, nb=bl, tq_f=512, hp=MLA_HEADS, tq_a=256, tm_back=1024, tm_moe=512, tm_comb=512)

    xp = x_prompt.reshape(bc * lc, d)
    xs = x_sample.reshape(bl * ll, d)
    h0_ctx = jnp.zeros((bc, 2, LRU_W), F32)
    ckv_list, kr_list, lru_list = [], [], []
    for l in range(DEPTH):
        p = _layer_params(l, a)
        mod_ctx = mod_all[l, 0:1].reshape(1, 1, 6 * d)
        mod_lat = mod_all[l, 1:1 + bl].reshape(bl, 1, 6 * d)
        xp, z3, ckv, fin = _trunk_layer(xp, mod_ctx, p, ctx_consts, h0_ctx, None, ctx_cfg)
        ckv_list.append(ckv.reshape(bc, lc, KV_LORA))
        kr_list.append(_take_rope_key(z3[:, :, Z_KR:]))
        lru_list.append(fin)
        kr_tile = _place_rope_key(cache_krope[:, l].reshape(bl * past, ROPE))
        kc, vc = _kvcache(cache_ckv[:, l].reshape(bl * past, KV_LORA), kr_tile, cache_tabs, p)
        cache = (kc.reshape(bl, past, HP), vc.reshape(bl, past, HP))
        xs, _, _, _ = _trunk_layer(xs, mod_lat, p, lat_consts, state_rglru[:, l], cache, lat_cfg)
    return (xp.reshape(bc, lc, d), xs.reshape(bl, ll, d),
            jnp.stack(ckv_list, axis=1), jnp.stack(kr_list, axis=1), jnp.stack(lru_list, axis=1))
```

```python
import functools
import math

import numpy as np
import jax
import jax.numpy as jnp
from jax import lax
from jax.experimental import pallas as pl
from jax.experimental.pallas import tpu as pltpu
from jax.experimental.pallas import tpu_sc as plsc

F32 = jnp.float32
BF16 = jnp.bfloat16

D_MODEL = 1024
DEPTH = 2
GRID_W = 64
EPS = 1e-6
FNET_W = 256
FNET_GROUPS = 4
LRU_W = 256
LRU_BLOCKS = 4
LRU_C = 8.0
SC_W = 256
MLA_HEADS = 8
Q_LORA = 384
KV_LORA = 256
NOPE = 64
ROPE = 32
V_DIM = 64
QK_DIM = NOPE + ROPE
MLA_OUT = MLA_HEADS * V_DIM
N_FREQ = ROPE // 4
ROPE_BASE = 10000.0
ATTN_SCALE = QK_DIM ** -0.5
LOG2E = math.log2(math.e)
N_GROUPS = 4
EXP_PER_GROUP = 8
N_EXPERTS = N_GROUPS * EXP_PER_GROUP
D_EXPERT = 256

LANE = 128
SUB = 8
HEAD_PAD = LANE
HP = MLA_HEADS * HEAD_PAD
ROPE_X1 = 0
ROPE_X2 = LANE // 2
DEN_LANE = V_DIM
Z_QKV = 6 * 256
Z_KR = Z_QKV + Q_LORA + KV_LORA
Z_W = Z_KR + LANE
QKV_W = Q_LORA + KV_LORA + LANE
VMEM_LIMIT = 52 * 1024 * 1024
PACK_W = D_MODEL // 2
HI_MASK = -65536
RT_ID0, RT_ID1, RT_W0, RT_W1 = 0, 1, 2, 3
RT_ROWS = 48
TOP_K = 2
PLAN_BLK = 256
DFT_SPLIT = 64
ATTN_SUB = 256
SC_CORES = 2
SC_SUBCORES = 16
SC_CHUNK = 128


def _cparams(sem):
    return pltpu.CompilerParams(dimension_semantics=sem, vmem_limit_bytes=VMEM_LIMIT)


def _bdot(a, b):
    return jnp.dot(a, b, preferred_element_type=F32)


def _rms(x, g):
    return x * lax.rsqrt(jnp.mean(x * x, axis=-1, keepdims=True) + EPS) * g


def _ada_kernel(c_ref, w_ref, b_ref, o_ref):
    c = c_ref[...]
    s = (c * jax.nn.sigmoid(c)).astype(BF16)
    o_ref[...] = _bdot(s, w_ref[...].astype(BF16)) + b_ref[...]


def _ada(cond8, w_ada, b_ada):
    nblk = 6 * D_MODEL // 1024
    return pl.pallas_call(
        _ada_kernel,
        grid=(DEPTH, nblk),
        in_specs=[
            pl.BlockSpec((SUB, D_MODEL), lambda l, n: (0, 0)),
            pl.BlockSpec((None, D_MODEL, 1024), lambda l, n: (l, 0, n)),
            pl.BlockSpec((None, 1, 1024), lambda l, n: (l, 0, n)),
        ],
        out_specs=pl.BlockSpec((None, SUB, 1024), lambda l, n: (l, 0, n)),
        out_shape=jax.ShapeDtypeStruct((DEPTH, SUB, 6 * D_MODEL), F32),
        compiler_params=_cparams(("arbitrary", "arbitrary")),
        name="ada",
    )(cond8, w_ada, b_ada.reshape(DEPTH, 1, 6 * D_MODEL))


def _front_kernel(x_ref, mod_ref, g_ref, w_ref, z_ref):
    x = x_ref[...]
    sh = mod_ref[:, 0:D_MODEL]
    sc = mod_ref[:, D_MODEL:2 * D_MODEL]
    h = _rms(x, g_ref[...]) * (1.0 + sc) + sh
    z_ref[...] = _bdot(h.astype(BF16), w_ref[...])


def _front(x, mod, g1, w_in, tm):
    t = x.shape[0]
    per_mod = t // (mod.shape[0] * tm)
    return pl.pallas_call(
        _front_kernel,
        grid=(t // tm,),
        in_specs=[
            pl.BlockSpec((tm, D_MODEL), lambda i: (i, 0)),
            pl.BlockSpec((None, 1, 6 * D_MODEL), lambda i: (i // per_mod, 0, 0)),
            pl.BlockSpec((1, D_MODEL), lambda i: (0, 0)),
            pl.BlockSpec((D_MODEL, Z_W), lambda i: (0, 0)),
        ],
        out_specs=pl.BlockSpec((tm, Z_W), lambda i: (i, 0)),
        out_shape=jax.ShapeDtypeStruct((t, Z_W), F32),
        compiler_params=_cparams(("arbitrary",)),
        name="front",
    )(x, mod, g1, w_in)


def _rope(x, cos, sinr):
    return x * cos + pltpu.roll(x, LANE // 2, 1) * sinr


def _head_norm(x, g):
    ss = jnp.sum(x * x, axis=-1, keepdims=True) * (1.0 / QK_DIM)
    return x * lax.rsqrt(ss + EPS) * g


def _build_kv(ckv, kr_tile, cos_t, sinr_t, wk_t_ref, wv_ref, gk_col, eye_ref, kt_ref, v_ref):
    nt = (((1,), (1,)), ((), ()))
    cb = ckv.astype(BF16)
    kn_t = lax.dot_general(wk_t_ref[...], cb, nt, preferred_element_type=F32)
    kr_hi = kr_tile.astype(BF16)
    kr_lo = (kr_tile - kr_hi.astype(F32)).astype(BF16)
    kr_t = (lax.dot_general(eye_ref[...], kr_hi, nt, preferred_element_type=F32)
            + lax.dot_general(eye_ref[...], kr_lo, nt, preferred_element_type=F32))
    for h in range(MLA_HEADS):
        sl = slice(h * HEAD_PAD, (h + 1) * HEAD_PAD)
        kh = kn_t[sl, :] + kr_t
        ss = jnp.sum(kh * kh, axis=0, keepdims=True) * (1.0 / QK_DIM)
        kh = kh * lax.rsqrt(ss + EPS) * gk_col
        kt_ref[sl, :] = (kh * cos_t + pltpu.roll(kh, HEAD_PAD // 2, 0) * sinr_t).astype(BF16)
    v = _bdot(cb, wv_ref[...])
    lane = lax.broadcasted_iota(jnp.int32, v.shape, 1)
    v_ref[...] = jnp.where((lane & (HEAD_PAD - 1)) == DEN_LANE, 1.0, v).astype(BF16)


def _qkv_kernel(z_ref, cos_ref, sinr_ref, cos_t_ref, sinr_t_ref, gqa_ref, wq_ref, gkva_ref, wk_ref, wv_ref,
                gq_ref, gk_ref, eye_ref, q_ref, k_ref, v_ref, ckv_ref):
    cos, sinr = cos_ref[...], sinr_ref[...]
    q_c = z_ref[:, 0:Q_LORA]
    kv_c = z_ref[:, Q_LORA:Q_LORA + KV_LORA]
    kr_tile = z_ref[:, Q_LORA + KV_LORA:QKV_W]
    qf = _bdot(_rms(q_c, gqa_ref[...]).astype(BF16), wq_ref[...])
    gq = gq_ref[...] * (ATTN_SCALE * LOG2E)
    for h in range(MLA_HEADS):
        sl = slice(h * HEAD_PAD, (h + 1) * HEAD_PAD)
        q_ref[:, sl] = _rope(_head_norm(qf[:, sl], gq), cos, sinr).astype(BF16)
    ckv = _rms(kv_c, gkva_ref[...])
    ckv_ref[...] = ckv
    _build_kv(ckv, kr_tile, cos_t_ref[...], sinr_t_ref[...], wk_ref, wv_ref, gk_ref[...], eye_ref, k_ref, v_ref)


def _qkv(z, tabs, p, seq_len, tm):
    t = z.shape[0]
    if tm > seq_len:
        tabs = [jnp.tile(a, (tm // seq_len, 1)) for a in tabs]
    tabs_t = [a.T for a in tabs]
    per_seq = max(seq_len // tm, 1)
    tab_spec = pl.BlockSpec((tm, LANE), lambda i: (i % per_seq, 0))
    tab_t_spec = pl.BlockSpec((LANE, tm), lambda i: (0, i % per_seq))
    full = lambda shape: pl.BlockSpec(shape, lambda i: (0,) * len(shape))
    return pl.pallas_call(
        _qkv_kernel,
        grid=(t // tm,),
        in_specs=[
            pl.BlockSpec((tm, QKV_W), lambda i: (i, Z_QKV // QKV_W)),
            tab_spec, tab_spec, tab_t_spec, tab_t_spec,
            full((1, Q_LORA)), full((Q_LORA, HP)), full((1, KV_LORA)),
            full((HP, KV_LORA)), full((KV_LORA, HP)), full((1, LANE)), full((LANE, 1)), full((LANE, LANE)),
        ],
        out_specs=[
            pl.BlockSpec((tm, HP), lambda i: (i, 0)),
            pl.BlockSpec((HP, tm), lambda i: (0, i)),
            pl.BlockSpec((tm, HP), lambda i: (i, 0)),
            pl.BlockSpec((tm, KV_LORA), lambda i: (i, 0)),
        ],
        out_shape=[
            jax.ShapeDtypeStruct((t, HP), BF16),
            jax.ShapeDtypeStruct((HP, t), BF16),
            jax.ShapeDtypeStruct((t, HP), BF16),
            jax.ShapeDtypeStruct((t, KV_LORA), F32),
        ],
        compiler_params=_cparams(("arbitrary",)),
        name="qkv",
    )(z, tabs[0], tabs[1], tabs_t[0], tabs_t[1], p["g_qa"], p["wq"], p["g_kva"], p["wk_t"], p["wv"],
      p["gq"], p["gk_col"], p["eye"])


def _kvcache_kernel(ckv_ref, kr_ref, cos_t_ref, sinr_t_ref, wk_ref, wv_ref, gk_ref, eye_ref, k_ref, v_ref):
    _build_kv(ckv_ref[...], kr_ref[...], cos_t_ref[...], sinr_t_ref[...], wk_ref, wv_ref, gk_ref[...], eye_ref,
              k_ref, v_ref)


def _kvcache(ckv, kr_tile, tabs, p):
    t = ckv.shape[0]
    return pl.pallas_call(
        _kvcache_kernel,
        out_shape=[jax.ShapeDtypeStruct((HP, t), BF16), jax.ShapeDtypeStruct((t, HP), BF16)],
        compiler_params=_cparams(None),
        name="kvcache",
    )(ckv, kr_tile, tabs[0].T, tabs[1].T, p["wk_t"], p["wv"], p["gk_col"], p["eye"])


def _fourier_kernel(nb, u_ref, cc_ref, sc_ref, cl_ref, sl_ref, o_ref, a_scr, b_scr):
    @pl.when(pl.program_id(1) == 0)
    def _():
        for b in range(nb):
            u = u_ref[b].astype(BF16)
            a_scr[:, b * FNET_W:(b + 1) * FNET_W] = _bdot(u, cc_ref[...]).astype(BF16)
            b_scr[:, b * FNET_W:(b + 1) * FNET_W] = _bdot(u, sc_ref[...]).astype(BF16)

    y = _bdot(cl_ref[...], a_scr[...]) - _bdot(sl_ref[...], b_scr[...])
    for b in range(nb):
        o_ref[b] = y[:, b * FNET_W:(b + 1) * FNET_W].astype(BF16)


def _fourier_split_kernel(nb, u_ref, cc_ref, sc_ref, ca_ref, sa_ref, cb_ref, sb_ref, o_ref,
                          a_scr, b_scr, cl_scr, sl_scr):
    @pl.when(pl.program_id(1) == 0)
    def _():
        for b in range(nb):
            u = u_ref[b].astype(BF16)
            a_scr[:, b * FNET_W:(b + 1) * FNET_W] = _bdot(u, cc_ref[...]).astype(BF16)
            b_scr[:, b * FNET_W:(b + 1) * FNET_W] = _bdot(u, sc_ref[...]).astype(BF16)

    cb, sb = cb_ref[...], sb_ref[...]
    for j in range(ca_ref.shape[0]):
        ca, sa = ca_ref[j:j + 1, :], sa_ref[j:j + 1, :]
        cl_scr[j * DFT_SPLIT:(j + 1) * DFT_SPLIT, :] = (ca * cb - sa * sb).astype(BF16)
        sl_scr[j * DFT_SPLIT:(j + 1) * DFT_SPLIT, :] = (sa * cb + ca * sb).astype(BF16)
    y = _bdot(cl_scr[...], a_scr[...]) - _bdot(sl_scr[...], b_scr[...])
    for b in range(nb):
        o_ref[b] = y[:, b * FNET_W:(b + 1) * FNET_W].astype(BF16)


def _fourier_split(z3, mats, nb, tq):
    s, l, _ = z3.shape
    cc, sc, ca, sa, cb, sb = mats
    full = lambda a: pl.BlockSpec(a.shape, lambda i, r: (0, 0))
    rows = pl.BlockSpec((tq // DFT_SPLIT, l), lambda i, r: (r, 0))
    return pl.pallas_call(
        functools.partial(_fourier_split_kernel, nb),
        grid=(s // nb, l // tq),
        in_specs=[pl.BlockSpec((nb, l, FNET_W), lambda i, r: (i, 0, 0)), full(cc), full(sc), rows, rows,
                  full(cb), full(sb)],
        out_specs=pl.BlockSpec((nb, tq, FNET_W), lambda i, r: (i, r, 0)),
        out_shape=jax.ShapeDtypeStruct((s, l, FNET_W), BF16),
        scratch_shapes=[pltpu.VMEM((l, nb * FNET_W), BF16), pltpu.VMEM((l, nb * FNET_W), BF16),
                        pltpu.VMEM((tq, l), BF16), pltpu.VMEM((tq, l), BF16)],
        compiler_params=_cparams(("arbitrary", "arbitrary")),
        name="fourier_split",
    )(z3, cc, sc, ca, sa, cb, sb)


def _fourier(z3, mats, nb, tq):
    s, l, _ = z3.shape
    if len(mats) == 6:
        return _fourier_split(z3, mats, nb, tq)
    cc, sc, cl, sl = mats
    return pl.pallas_call(
        functools.partial(_fourier_kernel, nb),
        grid=(s // nb, l // tq),
        in_specs=[
            pl.BlockSpec((nb, l, FNET_W), lambda i, r: (i, 0, 0)),
            pl.BlockSpec((FNET_W, FNET_W), lambda i, r: (0, 0)),
            pl.BlockSpec((FNET_W, FNET_W), lambda i, r: (0, 0)),
            pl.BlockSpec((tq, l), lambda i, r: (r, 0)),
            pl.BlockSpec((tq, l), lambda i, r: (r, 0)),
        ],
        out_specs=pl.BlockSpec((nb, tq, FNET_W), lambda i, r: (i, r, 0)),
        out_shape=jax.ShapeDtypeStruct((s, l, FNET_W), BF16),
        scratch_shapes=[pltpu.VMEM((l, nb * FNET_W), BF16), pltpu.VMEM((l, nb * FNET_W), BF16)],
        compiler_params=_cparams(("arbitrary", "arbitrary")),
        name="fourier",
    )(z3, cc, sc, cl, sl)


def _load_ext(ref, c, rows, nchunks):
    t0 = pl.multiple_of(c * rows, rows)
    main = ref[pl.ds(t0, rows), :]
    lo = pl.multiple_of(jnp.maximum(t0 - SUB, 0), SUB)
    hi = pl.multiple_of(jnp.minimum(t0 + rows, (nchunks - 1) * rows + rows - SUB), SUB)
    prev = jnp.where(c > 0, ref[pl.ds(lo, SUB), :], 0.0)
    nxt = jnp.where(c < nchunks - 1, ref[pl.ds(hi, SUB), :], 0.0)
    return jnp.concatenate([prev, main, nxt], axis=0)


def _shifted(ext, off, rows):
    n = ext.shape[0]
    r = ext if off == 0 else pltpu.roll(ext, (-off) % n, 0)
    return r[SUB:SUB + rows]


def _gelu_tanh(x):
    return 0.5 * x * (1.0 + jnp.tanh(math.sqrt(2.0 / math.pi) * (x + 0.044715 * (x * x * x))))


def _lru_gates(xc, d, wa_ref, wx_ref, ba_ref, bx_ref, lam_ref):
    xb = xc.astype(BF16)
    r = jax.nn.sigmoid(_bdot(xb, wa_ref[d]) + ba_ref[d])
    i = jax.nn.sigmoid(_bdot(xb, wx_ref[d]) + bx_ref[d])
    nl = -lam_ref[d]
    softplus = jnp.maximum(nl, 0.0) + jnp.log1p(jnp.exp(-jnp.abs(nl)))
    la = (-LRU_C) * r * softplus
    a = jnp.exp(la)
    one_m_a2 = -jnp.tanh(la) * (a * a + 1.0)
    b = jnp.sqrt(one_m_a2) * (i * xc)
    return a, b


def _group_scan(a, b, reverse):
    rows = a.shape[0]
    rm = lax.broadcasted_iota(jnp.int32, a.shape, 0) & (SUB - 1)
    for s in (1, 2, 4):
        if reverse:
            sh, m = rows - s, rm + s <= SUB - 1
        else:
            sh, m = s, rm >= s
        a_sh = pltpu.roll(a, sh, 0)
        b_sh = pltpu.roll(b, sh, 0)
        b = jnp.where(m, a * b_sh + b, b)
        a = jnp.where(m, a * a_sh, a)
    return a, b


def _carry_scan(a, b, carry, reverse):
    ng = a.shape[0] // SUB
    out = [None] * ng
    order = range(ng - 1, -1, -1) if reverse else range(ng)
    for g in order:
        hg = a[g * SUB:(g + 1) * SUB] * carry + b[g * SUB:(g + 1) * SUB]
        out[g] = hg
        carry = hg[0:1] if reverse else hg[SUB - 1:SUB]
    return jnp.concatenate(out, axis=0), carry


def _lru_kernel(rows, nchunks, xl_ref, gl_ref, bs_ref, cs_ref, xs_ref, h0_ref, cw_ref, wa_ref, wx_ref,
                ba_ref, bx_ref, lam_ref, sw_ref, yl_ref, ys_ref, fin_ref, xc_scr, hf_scr, hb_scr):
    cw = cw_ref[...]
    sw = sw_ref[...]
    chunk = lambda c: pl.ds(pl.multiple_of(c * rows, rows), rows)

    def conv_body(c, carry):
        ext = _load_ext(xl_ref, c, rows, nchunks)
        xc = _shifted(ext, -2, rows) * cw[0:1]
        for k in range(1, 4):
            xc = xc + _shifted(ext, k - 2, rows) * cw[k:k + 1]
        xc_scr[chunk(c), :] = xc
        return carry

    lax.fori_loop(0, nchunks, conv_body, 0)

    def direction(c, d, carry, h_scr):
        a, b = _lru_gates(xc_scr[chunk(c), :], d, wa_ref, wx_ref, ba_ref, bx_ref, lam_ref)
        a, b = _group_scan(a, b, d == 1)
        h, carry = _carry_scan(a, b, carry, d == 1)
        h_scr[chunk(c), :] = h
        return carry

    def scan_body(j, carries):
        cf, cb = carries
        return direction(j, 0, cf, hf_scr), direction(nchunks - 1 - j, 1, cb, hb_scr)

    cf, cb = lax.fori_loop(0, nchunks, scan_body, (h0_ref[0:1, :], h0_ref[1:2, :]))
    fin_ref[0:1, :] = cf
    fin_ref[1:2, :] = cb

    def out_body(c, carry):
        y = (hf_scr[chunk(c), :] + hb_scr[chunk(c), :]) * _gelu_tanh(gl_ref[chunk(c), :])
        yl_ref[chunk(c), :] = y.astype(BF16)
        ext = _load_ext(cs_ref, c, rows, nchunks) * _load_ext(xs_ref, c, rows, nchunks)
        acc = _shifted(ext, -1, rows) * sw[0:1]
        acc = acc + _shifted(ext, 0, rows) * sw[1:2]
        acc = acc + _shifted(ext, 1, rows) * sw[2:3]
        ys_ref[chunk(c), :] = (bs_ref[chunk(c), :] * acc).astype(BF16)
        return carry

    lax.fori_loop(0, nchunks, out_body, 0)


def _lru(z3, h0, p):
    s, l, _ = z3.shape
    rows = min(l, 1024)
    nchunks = l // rows
    half = lambda blk: pl.BlockSpec((None, l, LANE), lambda i, j: (i, 0, blk + j))
    wspec = lambda shape: pl.BlockSpec((None,) + shape, lambda i, j: (j,) + (0,) * len(shape))
    seq_half = pl.BlockSpec((None, l, LANE), lambda i, j: (i, 0, j))
    return pl.pallas_call(
        functools.partial(_lru_kernel, rows, nchunks),
        grid=(s, 2),
        in_specs=[
            half(2), half(4), half(6), half(8), half(10),
            pl.BlockSpec((None, 2, LANE), lambda i, j: (i, 0, j)),
            wspec((4, LANE)), wspec((2, LANE, LANE)), wspec((2, LANE, LANE)),
            wspec((2, 1, LANE)), wspec((2, 1, LANE)), wspec((2, 1, LANE)), wspec((3, LANE)),
        ],
        out_specs=[seq_half, seq_half, pl.BlockSpec((None, 2, LANE), lambda i, j: (i, 0, j))],
        out_shape=[jax.ShapeDtypeStruct((s, l, LRU_W), BF16), jax.ShapeDtypeStruct((s, l, SC_W), BF16),
                   jax.ShapeDtypeStruct((s, 2, LRU_W), F32)],
        scratch_shapes=[pltpu.VMEM((l, LANE), F32), pltpu.VMEM((l, LANE), F32), pltpu.VMEM((l, LANE), F32)],
        compiler_params=_cparams(("arbitrary", "arbitrary")),
        name="lru",
    )(z3, z3, z3, z3, z3, h0, p["lru_conv"], p["lru_wa"], p["lru_wx"], p["lru_ba"], p["lru_bx"], p["lru_lam"],
      p["sc_conv"])


def _store_head_pair(o_ref, rows, j, o_even, o_odd):
    lane = lax.broadcasted_iota(jnp.int32, o_even.shape, 1)
    pair = jnp.where(lane < V_DIM, o_even, pltpu.roll(o_odd, V_DIM, 1))
    o_ref[rows, j * LANE:(j + 1) * LANE] = pair.astype(BF16)


def _attn_kernel(hp, sub, has_cache, *refs):
    if has_cache:
        q_ref, k_ref, v_ref, kc_ref, vc_ref, o_ref, s_scr, p_scr, m_scr, sc_scr, pc_scr = refs
    else:
        q_ref, k_ref, v_ref, o_ref, s_scr, p_scr, m_scr = refs
    nt = (((1,), (1,)), ((), ()))
    items = [(slice(r * sub, (r + 1) * sub), slice(h * HEAD_PAD, (h + 1) * HEAD_PAD))
             for r in range(q_ref.shape[0] // sub) for h in range(hp)]
    n = len(items)

    def scores(i):
        rows, head = items[i]
        q = q_ref[rows, head]
        s = _bdot(q, k_ref[head, :])
        m = jnp.max(s, axis=-1, keepdims=True)
        s_scr[i % 2] = s
        if has_cache:
            sc = _bdot(q, kc_ref[head, :])
            m = jnp.maximum(m, jnp.max(sc, axis=-1, keepdims=True))
            sc_scr[i % 2] = sc
        m_scr[i % 2] = jnp.broadcast_to(m, m_scr.shape[1:])

    def probs(i):
        m = m_scr[i % 2][:, 0:1]
        p_scr[i % 2] = jnp.exp2(s_scr[i % 2] - m).astype(BF16)
        if has_cache:
            pc_scr[i % 2] = jnp.exp2(sc_scr[i % 2] - m).astype(BF16)

    pending = {}

    def weighted_values(i):
        rows, head = items[i]
        o = _bdot(p_scr[i % 2], v_ref[:, head])
        if has_cache:
            o = o + _bdot(pc_scr[i % 2], vc_ref[:, head])
        o = o / o[:, DEN_LANE:DEN_LANE + 1]
        h = i % hp
        if h % 2 == 0:
            pending[0] = o
        else:
            _store_head_pair(o_ref, rows, h // 2, pending.pop(0), o)

    scores(0)
    for i in range(n):
        if i + 1 < n:
            scores(i + 1)
        if i > 0:
            weighted_values(i - 1)
        probs(i)
    weighted_values(n - 1)


def _attn_short_kernel(hp, q_ref, k_ref, v_ref, o_ref):
    nt = (((1,), (1,)), ((), ()))
    heads = [slice(h * HEAD_PAD, (h + 1) * HEAD_PAD) for h in range(hp)]
    scores = [_bdot(q_ref[:, hd], k_ref[hd, :]) for hd in heads]
    probs = [jnp.exp2(s - jnp.max(s, axis=-1, keepdims=True)).astype(BF16) for s in scores]
    outs = []
    for hd, p in zip(heads, probs):
        o = _bdot(p, v_ref[:, hd])
        outs.append(o / o[:, DEN_LANE:DEN_LANE + 1])
    for j in range(hp // 2):
        _store_head_pair(o_ref, slice(None), j, outs[2 * j], outs[2 * j + 1])


def _attn(q, k, v, cache, hp, tq):
    s, l, _ = q.shape
    w = hp * HEAD_PAD
    qspec = pl.BlockSpec((None, tq, w), lambda i, h, r: (i, r, h))
    ospec = pl.BlockSpec((None, tq, hp * V_DIM), lambda i, h, r: (i, r, h))
    if cache is None and tq == l:
        kspec = pl.BlockSpec((w, l), lambda i, h, r: (h, i))
        vspec = pl.BlockSpec((None, l, w), lambda i, h, r: (i, 0, h))
        return pl.pallas_call(
            functools.partial(_attn_short_kernel, hp),
            grid=(s, MLA_HEADS // hp, 1),
            in_specs=[qspec, kspec, vspec],
            out_specs=ospec,
            out_shape=jax.ShapeDtypeStruct((s, l, MLA_OUT), BF16),
            compiler_params=_cparams(("arbitrary", "arbitrary", "arbitrary")),
            name="attn_short",
        )(q, k, v)
    kv_mode = dict(pipeline_mode=pl.Buffered(1))
    kspec = pl.BlockSpec((w, l), lambda i, h, r: (h, i), **kv_mode)
    vspec = pl.BlockSpec((None, l, w), lambda i, h, r: (i, 0, h), **kv_mode)
    in_specs = [qspec, kspec, vspec]
    args = [q, k, v]
    if cache is not None:
        lc = cache[1].shape[1]
        in_specs += [pl.BlockSpec((w, lc), lambda i, h, r: (h, i), **kv_mode),
                     pl.BlockSpec((None, lc, w), lambda i, h, r: (i, 0, h), **kv_mode)]
        args += list(cache)
    sub = min(tq, ATTN_SUB)
    scratch = [pltpu.VMEM((2, sub, l), F32), pltpu.VMEM((2, sub, l), BF16), pltpu.VMEM((2, sub, LANE), F32)]
    if cache is not None:
        scratch += [pltpu.VMEM((2, sub, lc), F32), pltpu.VMEM((2, sub, lc), BF16)]
    return pl.pallas_call(
        functools.partial(_attn_kernel, hp, sub, cache is not None),
        grid=(s, MLA_HEADS // hp, l // tq),
        in_specs=in_specs,
        out_specs=ospec,
        out_shape=jax.ShapeDtypeStruct((s, l, MLA_OUT), BF16),
        scratch_shapes=scratch,
        compiler_params=_cparams(("arbitrary", "arbitrary", "arbitrary")),
        name="attn",
    )(*args)


def _route(lt):
    row = lax.broadcasted_iota(jnp.int32, (SUB, lt.shape[1]), 0).astype(F32)
    neg = -jnp.inf
    gl = lt[0:SUB]
    g_ok = row < N_GROUPS
    glm = jnp.where(g_ok, gl, neg)
    gmax = jnp.max(glm, axis=0, keepdims=True)
    gsel = jnp.min(jnp.where(glm == gmax, row, float(SUB)), axis=0, keepdims=True)
    g_w = 1.0 / jnp.sum(jnp.where(g_ok, jnp.exp(gl - gmax), 0.0), axis=0, keepdims=True)
    es = lt[SUB:2 * SUB]
    for g in range(1, N_GROUPS):
        es = jnp.where(gsel == float(g), lt[(g + 1) * SUB:(g + 2) * SUB], es)
    m1 = jnp.max(es, axis=0, keepdims=True)
    i1 = jnp.min(jnp.where(es == m1, row, float(SUB)), axis=0, keepdims=True)
    es2 = jnp.where(row == i1, neg, es)
    m2 = jnp.max(es2, axis=0, keepdims=True)
    i2 = jnp.min(jnp.where(es2 == m2, row, float(SUB)), axis=0, keepdims=True)
    e2 = jnp.exp(m2 - m1)
    inv = g_w / (1.0 + e2)
    base = gsel * EXP_PER_GROUP
    return (jnp.where(row == RT_ID0, base + i1, 0.0) + jnp.where(row == RT_ID1, base + i2, 0.0)
            + jnp.where(row == RT_W0, inv, 0.0) + jnp.where(row == RT_W1, inv * e2, 0.0))


def _pack_bf16_pairs(x):
    k = x.shape[1] // 2
    bits = lax.bitcast_convert_type(x.astype(BF16).astype(F32), jnp.int32)
    return lax.shift_right_logical(bits[:, :k], 16) | (bits[:, k:] & HI_MASK)


def _unpack_bf16_pairs(w):
    lo = lax.bitcast_convert_type(lax.shift_left(w, 16), F32).astype(BF16)
    hi = lax.bitcast_convert_type(w & HI_MASK, F32).astype(BF16)
    return lo, hi


def _back_kernel(x_ref, mod_ref, g1_ref, g2_ref, yf_ref, yl_ref, ys_ref, o_ref,
                 wg_ref, wpf_ref, wpl_ref, wps_ref, wpm_ref, wo_ref, wrh_ref, wrl_ref, br_ref,
                 x1_ref, h2_ref, comb_ref):
    d = D_MODEL
    x = x_ref[...]
    sh1, sc1, gt1 = mod_ref[:, 0:d], mod_ref[:, d:2 * d], mod_ref[:, 2 * d:3 * d]
    sh2, sc2 = mod_ref[:, 3 * d:4 * d], mod_ref[:, 4 * d:5 * d]
    hb = (_rms(x, g1_ref[...]) * (1.0 + sc1) + sh1).astype(BF16)
    merged = None
    for j, (y_ref, w_ref) in enumerate(((yf_ref, wpf_ref), (yl_ref, wpl_ref), (ys_ref, wps_ref), (o_ref, wpm_ref))):
        gate = jax.nn.sigmoid(_bdot(hb, wg_ref[:, j * d:(j + 1) * d]))
        term = gate * _bdot(y_ref[...], w_ref[...])
        merged = term if merged is None else merged + term
    x1 = x + gt1 * _bdot(merged.astype(BF16), wo_ref[...])
    x1_ref[...] = x1
    h2 = _rms(x1, g2_ref[...]) * (1.0 + sc2) + sh2
    h2b = h2.astype(BF16)
    h2_ref[...] = _pack_bf16_pairs(h2)
    h2l = (h2 - h2b.astype(F32)).astype(BF16)
    nt = (((1,), (1,)), ((), ()))
    rdot = lambda w_ref, h: lax.dot_general(w_ref[...], h, nt, preferred_element_type=F32)
    logits_t = rdot(wrh_ref, h2b) + rdot(wrh_ref, h2l) + rdot(wrl_ref, h2b) + br_ref[...]
    comb_ref[...] = _route(logits_t)


def _back(x, mod, yf, yl, ys, o, p, tm):
    t = x.shape[0]
    per_mod = t // (mod.shape[0] * tm)
    row = lambda w: pl.BlockSpec((tm, w), lambda i: (i, 0))
    full = lambda a: pl.BlockSpec(a.shape, lambda i: (0,) * a.ndim, pipeline_mode=pl.Buffered(1))
    weights = [p["w_gate"], p["w_pf"], p["w_pl"], p["w_ps"], p["w_pm"], p["w_out"], p["wr_hi"], p["wr_lo"], p["b_r"]]
    return pl.pallas_call(
        _back_kernel,
        grid=(t // tm,),
        in_specs=[
            row(D_MODEL),
            pl.BlockSpec((None, 1, 6 * D_MODEL), lambda i: (i // per_mod, 0, 0)),
            pl.BlockSpec((1, D_MODEL), lambda i: (0, 0)),
            pl.BlockSpec((1, D_MODEL), lambda i: (0, 0)),
            row(FNET_W), row(LRU_W), row(SC_W), row(MLA_OUT),
        ] + [full(w) for w in weights],
        out_specs=[row(D_MODEL), row(PACK_W), pl.BlockSpec((SUB, tm), lambda i: (0, i))],
        out_shape=[
            jax.ShapeDtypeStruct((t, D_MODEL), F32),
            jax.ShapeDtypeStruct((t, PACK_W), jnp.int32),
            jax.ShapeDtypeStruct((SUB, t), F32),
        ],
        compiler_params=_cparams(("arbitrary",)),
        name="back",
    )(x, mod, p["norm1_g"], p["norm2_g"], yf, yl, ys, o, *weights)


def _plan_kernel(tm, rt_ref, pos_ref, te_ref, nu_ref):
    ids = jnp.concatenate([rt_ref[RT_ID0:RT_ID0 + 1, :], rt_ref[RT_ID1:RT_ID1 + 1, :]], axis=1)
    n_rows = ids.shape[1]
    erow = lax.broadcasted_iota(jnp.int32, (N_EXPERTS, PLAN_BLK), 0).astype(F32)
    col = lax.broadcasted_iota(jnp.int32, (N_EXPERTS, LANE), 0)
    blocks = [(ids[:, b * PLAN_BLK:(b + 1) * PLAN_BLK] == erow).astype(F32) for b in range(n_rows // PLAN_BLK)]
    tri = (lax.broadcasted_iota(jnp.int32, (PLAN_BLK, PLAN_BLK), 0)
           <= lax.broadcasted_iota(jnp.int32, (PLAN_BLK, PLAN_BLK), 1)).astype(BF16)
    prefix = [_bdot(blk.astype(BF16), tri) for blk in blocks]
    counts = prefix[0][:, PLAN_BLK - 1:PLAN_BLK]
    for pre in prefix[1:]:
        counts = counts + pre[:, PLAN_BLK - 1:PLAN_BLK]
    tiles = jnp.floor((counts + float(tm - 1)) * (1.0 / tm))
    tile_end = jnp.broadcast_to(tiles, (N_EXPERTS, LANE))
    for s in (1, 2, 4, 8, 16):
        tile_end = tile_end + jnp.where(col >= s, pltpu.roll(tile_end, s, 0), 0.0)
    base = (tile_end[:, 0:1] - tiles) * float(tm) - 1.0
    for b, (blk, pre) in enumerate(zip(blocks, prefix)):
        pos = jnp.sum(blk * (base + pre), axis=0, keepdims=True)
        pos_ref[:, b * PLAN_BLK:(b + 1) * PLAN_BLK] = pos.astype(jnp.int32)
        base = base + pre[:, PLAN_BLK - 1:PLAN_BLK]
    k = lax.broadcasted_iota(jnp.int32, (N_EXPERTS, LANE), 1).astype(F32)
    owner = jnp.sum((k >= tile_end).astype(F32), axis=0, keepdims=True)
    te_ref[...] = jnp.minimum(owner, float(N_EXPERTS - 1)).astype(jnp.int32)
    nu_ref[...] = jnp.broadcast_to(tile_end[N_EXPERTS - 1:N_EXPERTS, :], nu_ref.shape).astype(jnp.int32)


def _dispatch_plan(rt, tm):
    n_rows = TOP_K * rt.shape[1]
    n_tiles = n_rows // tm + N_EXPERTS
    assert n_tiles <= LANE and tm & (tm - 1) == 0 and n_rows % PLAN_BLK == 0
    pos, te, nu = pl.pallas_call(
        functools.partial(_plan_kernel, tm),
        out_shape=[jax.ShapeDtypeStruct((1, n_rows), jnp.int32), jax.ShapeDtypeStruct((1, LANE), jnp.int32),
                   jax.ShapeDtypeStruct((1, LANE), jnp.int32)],
        compiler_params=_cparams(None),
        name="plan",
    )(rt)
    return pos.reshape(n_rows), te[0, :n_tiles], nu[0, :1], n_tiles


def _sc_worker_rows(n_rows):
    workers = SC_CORES * SC_SUBCORES
    per_w = n_rows // workers
    assert per_w * workers == n_rows and per_w % SC_CHUNK == 0
    return per_w


def _sc_mesh():
    return plsc.VectorSubcoreMesh(core_axis_name="c", subcore_axis_name="s",
                                  num_cores=SC_CORES, num_subcores=SC_SUBCORES)


def _sc_scatter_rows(src, pos, n_out):
    t, w = src.shape
    per_w = _sc_worker_rows(pos.shape[0])

    @functools.partial(
        pl.kernel, mesh=_sc_mesh(), out_type=jax.ShapeDtypeStruct((n_out, w), src.dtype),
        scratch_types=[pltpu.VMEM((SC_CHUNK,), jnp.int32), pltpu.VMEM((SC_CHUNK, w), src.dtype)],
        name="moe_dispatch")
    def run(src_hbm, pos_hbm, out_hbm, idx_v, rows_v):
        wid = lax.axis_index("s") * SC_CORES + lax.axis_index("c")

        @pl.loop(0, per_w // SC_CHUNK)
        def _(i):
            j0 = wid * per_w + i * SC_CHUNK
            pltpu.sync_copy(pos_hbm.at[pl.ds(j0, SC_CHUNK)], idx_v)
            pltpu.sync_copy(src_hbm.at[pl.ds(lax.rem(j0, t), SC_CHUNK)], rows_v)
            pltpu.sync_copy(rows_v, out_hbm.at[idx_v])

    return run(src, pos)


def _sc_gather_rows(src, pos):
    w = src.shape[1]
    n = pos.shape[0]
    per_w = _sc_worker_rows(n)

    @functools.partial(
        pl.kernel, mesh=_sc_mesh(), out_type=jax.ShapeDtypeStruct((n, w), src.dtype),
        scratch_types=[pltpu.VMEM((SC_CHUNK,), jnp.int32), pltpu.VMEM((SC_CHUNK, w), src.dtype)],
        name="moe_return")
    def run(src_hbm, pos_hbm, out_hbm, idx_v, rows_v):
        wid = lax.axis_index("s") * SC_CORES + lax.axis_index("c")

        @pl.loop(0, per_w // SC_CHUNK)
        def _(i):
            j0 = wid * per_w + i * SC_CHUNK
            pltpu.sync_copy(pos_hbm.at[pl.ds(j0, SC_CHUNK)], idx_v)
            pltpu.sync_copy(src_hbm.at[idx_v], rows_v)
            pltpu.sync_copy(rows_v, out_hbm.at[pl.ds(j0, SC_CHUNK)])

    return run(src, pos)


def _experts_kernel(te_ref, nu_ref, x_ref, w1_ref, w3_ref, w2_ref, y_ref):
    half = D_MODEL // 2

    @pl.when(pl.program_id(0) < nu_ref[0])
    def _():
        lo, hi = _unpack_bf16_pairs(x_ref[...])
        a = _bdot(lo, w1_ref[:half, :].astype(BF16)) + _bdot(hi, w1_ref[half:, :].astype(BF16))
        u = _bdot(lo, w3_ref[:half, :].astype(BF16)) + _bdot(hi, w3_ref[half:, :].astype(BF16))
        mid = (a * jax.nn.sigmoid(a)) * u
        y_ref[...] = _pack_bf16_pairs(_bdot(mid.astype(BF16), w2_ref[...].astype(BF16)))


def _experts(xs, tile_expert, n_used, n_tiles, w1, w3, w2, layer, tm):
    wspec = lambda a: pl.BlockSpec((None, None) + a.shape[2:], lambda i, te, nu: (layer, te[i], 0, 0))
    grid_spec = pltpu.PrefetchScalarGridSpec(
        num_scalar_prefetch=2,
        grid=(n_tiles,),
        in_specs=[pl.BlockSpec((tm, PACK_W), lambda i, te, nu: (jnp.minimum(i, nu[0] - 1), 0)),
                  wspec(w1), wspec(w3), wspec(w2)],
        out_specs=pl.BlockSpec((tm, PACK_W), lambda i, te, nu: (jnp.where(i < nu[0], i, n_tiles), 0)),
    )
    return pl.pallas_call(
        _experts_kernel,
        grid_spec=grid_spec,
        out_shape=jax.ShapeDtypeStruct((xs.shape[0] + tm, PACK_W), jnp.int32),
        compiler_params=_cparams(("arbitrary",)),
        name="experts",
    )(tile_expert, n_used, xs, w1, w3, w2)


def _combine_kernel(x1_ref, mod_ref, rt_ref, g0_ref, g1_ref, o_ref):
    rt = rt_ref[...]
    lo0, hi0 = _unpack_bf16_pairs(g0_ref[...])
    lo1, hi1 = _unpack_bf16_pairs(g1_ref[...])
    w0 = rt[:, RT_W0:RT_W0 + 1]
    w1 = rt[:, RT_W1:RT_W1 + 1]
    half = D_MODEL // 2
    gt2 = mod_ref[:, 5 * D_MODEL:6 * D_MODEL]
    o_ref[:, :half] = x1_ref[:, :half] + gt2[:, :half] * (w0 * lo0.astype(F32) + w1 * lo1.astype(F32))
    o_ref[:, half:] = x1_ref[:, half:] + gt2[:, half:] * (w0 * hi0.astype(F32) + w1 * hi1.astype(F32))


def _combine(x1, mod, rt, g, tm):
    t = x1.shape[0]
    per_mod = t // (mod.shape[0] * tm)
    return pl.pallas_call(
        _combine_kernel,
        grid=(t // tm,),
        in_specs=[
            pl.BlockSpec((tm, D_MODEL), lambda i: (i, 0)),
            pl.BlockSpec((None, 1, 6 * D_MODEL), lambda i: (i // per_mod, 0, 0)),
            pl.BlockSpec((tm, LANE), lambda i: (i, 0)),
            pl.BlockSpec((tm, PACK_W), lambda i: (i, 0)),
            pl.BlockSpec((tm, PACK_W), lambda i: (i + t // tm, 0)),
        ],
        out_specs=pl.BlockSpec((tm, D_MODEL), lambda i: (i, 0)),
        out_shape=jax.ShapeDtypeStruct((t, D_MODEL), F32),
        compiler_params=_cparams(("arbitrary",)),
        name="combine",
    )(x1, mod, rt, g, g)


def _moe(h2p, rt, x1, mod, p, tm_e, tm_c):
    pos, tile_expert, n_used, n_tiles = _dispatch_plan(rt, tm_e)
    xs = _sc_scatter_rows(h2p, pos, n_tiles * tm_e)
    ys = _experts(xs, tile_expert, n_used, n_tiles, p["w1"], p["w3"], p["w2"], p["layer"], tm_e)
    g = _sc_gather_rows(ys, pos)
    rt_cols = jnp.pad(rt.T, ((0, 0), (0, LANE - rt.shape[0])))
    return _combine(x1, mod, rt_cols, g, tm_c)


def _channel_dft():
    n = np.arange(FNET_W // FNET_GROUPS)
    ang = 2.0 * np.pi * ((n[:, None] * n[None, :]) % n.size) / n.size
    eye = np.eye(FNET_GROUPS)
    scale = 1.0 / math.sqrt(n.size)
    return (jnp.asarray(np.kron(eye, np.cos(ang) * scale), F32).astype(BF16),
            jnp.asarray(np.kron(eye, np.sin(ang) * scale), F32).astype(BF16))


def _position_dft(l):
    scale = 1.0 / math.sqrt(l)
    if l <= 256:
        n = np.arange(l)
        ang = 2.0 * np.pi * ((n[:, None] * n[None, :]) % l) / l
        return jnp.asarray(np.cos(ang) * scale, F32).astype(BF16), jnp.asarray(np.sin(ang) * scale, F32).astype(BF16)
    m = DFT_SPLIT
    n = np.arange(l)
    ang_a = 2.0 * np.pi * ((np.arange(l // m)[:, None] * m * n[None, :]) % l) / l
    ang_b = 2.0 * np.pi * ((np.arange(m)[:, None] * n[None, :]) % l) / l
    return (jnp.asarray(np.cos(ang_a), F32), jnp.asarray(np.sin(ang_a), F32),
            jnp.asarray(np.cos(ang_b) * scale, F32), jnp.asarray(np.sin(ang_b) * scale, F32))


def _rope_tables(l, rotate):
    cos = np.ones((l, LANE))
    sinr = np.zeros((l, LANE))
    if rotate:
        t = np.arange(l)
        inv = ROPE_BASE ** (-np.arange(N_FREQ) / N_FREQ)
        ang = np.concatenate([(t // GRID_W)[:, None] * inv, (t % GRID_W)[:, None] * inv], axis=-1)
        half = ROPE // 2
        for lo in (ROPE_X1, ROPE_X2):
            cos[:, lo:lo + half] = np.cos(ang)
        sinr[:, ROPE_X1:ROPE_X1 + half] = -np.sin(ang)
        sinr[:, ROPE_X2:ROPE_X2 + half] = np.sin(ang)
    return tuple(jnp.asarray(a, F32) for a in (cos, sinr))


def _head_lane_source():
    half = ROPE // 2
    src = np.full((HEAD_PAD,), QK_DIM, np.int32)
    src[ROPE_X1:ROPE_X1 + half] = NOPE + np.arange(half)
    src[ROPE_X2:ROPE_X2 + half] = NOPE + half + np.arange(half)
    free = [i for i in range(HEAD_PAD) if src[i] == QK_DIM][:NOPE]
    src[free] = np.arange(NOPE)
    return src


def _place_head_dims(w, n_src):
    src = _head_lane_source()
    src = np.where(src < n_src, src, n_src)
    wz = jnp.concatenate([w[..., :n_src], jnp.zeros(w.shape[:-1] + (1,), w.dtype)], axis=-1)
    out = jnp.take(wz, jnp.asarray(src), axis=-1)
    return out.reshape(out.shape[:-2] + (out.shape[-2] * HEAD_PAD,))


def _place_rope_key(kr):
    half = ROPE // 2
    out = jnp.zeros(kr.shape[:-1] + (LANE,), kr.dtype)
    return out.at[..., ROPE_X1:ROPE_X1 + half].set(kr[..., :half]).at[..., ROPE_X2:ROPE_X2 + half].set(kr[..., half:])


def _take_rope_key(tile):
    half = ROPE // 2
    return jnp.concatenate([tile[..., ROPE_X1:ROPE_X1 + half], tile[..., ROPE_X2:ROPE_X2 + half]], axis=-1)


def _pad_heads(w, lo, hi):
    r = w.shape[0]
    part = w[:, :, lo:hi]
    out = jnp.zeros((r, MLA_HEADS, HEAD_PAD), w.dtype).at[:, :, : hi - lo].set(part)
    return out.reshape(r, HP)


def _blockdiag_halves(w):
    bw = LRU_W // LRU_BLOCKS
    out = jnp.zeros((2, 2, LANE, LANE), w.dtype)
    for half in range(2):
        for k in range(2):
            n = 2 * half + k
            out = out.at[half, :, k * bw:(k + 1) * bw, k * bw:(k + 1) * bw].set(w[:, n])
    return out


def _halves(v):
    return jnp.moveaxis(v.reshape(v.shape[:-1] + (2, LANE)), -2, 0)


def _layer_params(l, a):
    d = D_MODEL
    w_in = a["w_in"][l]
    w_in = jnp.concatenate([w_in[:, :Z_KR], _place_rope_key(w_in[:, Z_KR:])], axis=1)
    w_r = jnp.zeros((RT_ROWS, d), F32).at[:N_GROUPS].set(a["w_gr"][l].T).at[SUB:SUB + N_EXPERTS].set(a["w_er"][l].T)
    b_r = jnp.zeros((RT_ROWS, 1), F32).at[:N_GROUPS, 0].set(a["b_gr"][l]).at[SUB:SUB + N_EXPERTS, 0].set(a["b_er"][l])
    wr_hi = w_r.astype(BF16)
    gpad = lambda g: jnp.take(jnp.concatenate([g, jnp.zeros((1,), F32)]), jnp.asarray(_head_lane_source())).reshape(1, LANE)
    return {
        "norm1_g": a["norm1_g"][l].reshape(1, d), "norm2_g": a["norm2_g"][l].reshape(1, d),
        "w_in": w_in.astype(BF16),
        "lru_conv": _halves(a["lru_conv"][l]),
        "lru_wa": _blockdiag_halves(a["lru_wa"][l]).astype(BF16),
        "lru_wx": _blockdiag_halves(a["lru_wx"][l]).astype(BF16),
        "lru_ba": _halves(a["lru_ba"][l])[:, :, None, :], "lru_bx": _halves(a["lru_bx"][l])[:, :, None, :],
        "lru_lam": _halves(a["lru_lam"][l])[:, :, None, :],
        "sc_conv": _halves(a["sc_conv"][l]),
        "g_qa": a["g_qa"][l].reshape(1, Q_LORA), "g_kva": a["g_kva"][l].reshape(1, KV_LORA),
        "wq": _place_head_dims(a["w_qb"][l], QK_DIM).astype(BF16),
        "wk_t": _place_head_dims(a["w_kvb"][l], NOPE).T.astype(BF16),
        "wv": _pad_heads(a["w_kvb"][l], NOPE, NOPE + V_DIM).astype(BF16),
        "gq": gpad(a["g_qn"][l]), "gk_col": gpad(a["g_kn"][l]).reshape(LANE, 1),
        "eye": jnp.eye(LANE, dtype=F32).astype(BF16),
        "w_gate": a["w_gate"][l].astype(BF16),
        "w_pf": a["w_pf"][l].astype(BF16), "w_pl": a["w_pl"][l].astype(BF16), "w_ps": a["w_ps"][l].astype(BF16),
        "w_pm": a["w_pm"][l].astype(BF16), "w_out": a["w_out"][l].astype(BF16),
        "wr_hi": wr_hi, "wr_lo": (w_r - wr_hi.astype(F32)).astype(BF16), "b_r": b_r,
        "w1": a["w1"], "w3": a["w3"], "w2": a["w2"], "layer": l,
    }


def _trunk_layer(x, mod, p, consts, h0, cache, cfg):
    s, l = cfg["s"], cfg["l"]
    z = _front(x, mod, p["norm1_g"], p["w_in"], cfg["tm_front"])
    z3 = z.reshape(s, l, Z_W)
    q, k, v, ckv = _qkv(z, consts["rope"], p, l, cfg["tm_qkv"])
    yf = _fourier(z3, consts["dft"], cfg["nb"], cfg["tq_f"])
    yl, ys, fin = _lru(z3, h0, p)
    o = _attn(q.reshape(s, l, HP), k, v.reshape(s, l, HP), cache, cfg["hp"], cfg["tq_a"])
    t = s * l
    x1, h2, comb = _back(x, mod, yf.reshape(t, FNET_W), yl.reshape(t, LRU_W), ys.reshape(t, SC_W),
                         o.reshape(t, MLA_OUT), p, cfg["tm_back"])
    x2 = _moe(h2, comb, x1, mod, p, cfg["tm_moe"], cfg["tm_comb"])
    return x2, z3, ckv, fin


def kernel(x_prompt, x_sample, cache_ckv, cache_krope, state_rglru, c, c_ctx, norm1_g, norm2_g, w_ada, b_ada, w_in, lru_conv, lru_wa, lru_ba, lru_wx, lru_bx, lru_lam, sc_conv, g_qa, w_qb, g_kva, w_kvb, g_qn, g_kn, w_pf, w_pl, w_ps, w_pm, w_gate, w_out, w_gr, b_gr, w_er, b_er, w1, w3, w2):
    a = dict(norm1_g=norm1_g, norm2_g=norm2_g, w_in=w_in, lru_conv=lru_conv, lru_wa=lru_wa, lru_ba=lru_ba,
             lru_wx=lru_wx, lru_bx=lru_bx, lru_lam=lru_lam, sc_conv=sc_conv, g_qa=g_qa, w_qb=w_qb, g_kva=g_kva,
             w_kvb=w_kvb, g_qn=g_qn, g_kn=g_kn, w_pf=w_pf, w_pl=w_pl, w_ps=w_ps, w_pm=w_pm, w_gate=w_gate,
             w_out=w_out, w_gr=w_gr, b_gr=b_gr, w_er=w_er, b_er=b_er, w1=w1, w3=w3, w2=w2)
    bc, lc, d = x_prompt.shape
    bl, ll, _ = x_sample.shape
    past = cache_ckv.shape[2]

    cond8 = jnp.zeros((SUB, d), F32).at[0].set(c_ctx).at[1:1 + bl].set(c)
    mod_all = _ada(cond8, w_ada, b_ada)

    cc, sc = _channel_dft()
    ctx_consts = {"dft": (cc, sc) + _position_dft(lc), "rope": _rope_tables(lc, False)}
    lat_consts = {"dft": (cc, sc) + _position_dft(ll), "rope": _rope_tables(ll, True)}
    cache_tabs = _rope_tables(bl * past, False)
    ctx_cfg = dict(s=bc, l=lc, tm_front=512, tm_qkv=512, nb=8, tq_f=lc, hp=MLA_HEADS, tq_a=lc, tm_back=1024, tm_moe=512, tm_comb=512)
    lat_cfg = dict(s=bl, l=ll, tm_front=512, tm_qkv=512, nb=bl, tq_f=512, hp=MLA_HEADS, tq_a=256, tm_back=1024, tm_moe=512, tm_comb=512)

    xp = x_prompt.reshape(bc * lc, d)
    xs = x_sample.reshape(bl * ll, d)
    h0_ctx = jnp.zeros((bc, 2, LRU_W), F32)
    ckv_list, kr_list, lru_list = [], [], []
    for l in range(DEPTH):
        p = _layer_params(l, a)
        mod_ctx = mod_all[l, 0:1].reshape(1, 1, 6 * d)
        mod_lat = mod_all[l, 1:1 + bl].reshape(bl, 1, 6 * d)
        xp, z3, ckv, fin = _trunk_layer(xp, mod_ctx, p, ctx_consts, h0_ctx, None, ctx_cfg)
        ckv_list.append(ckv.reshape(bc, lc, KV_LORA))
        kr_list.append(_take_rope_key(z3[:, :, Z_KR:]))
        lru_list.append(fin)
        kr_tile = _place_rope_key(cache_krope[:, l].reshape(bl * past, ROPE))
        kc, vc = _kvcache(cache_ckv[:, l].reshape(bl * past, KV_LORA), kr_tile, cache_tabs, p)
        cache = (kc, vc.reshape(bl, past, HP))
        xs, _, _, _ = _trunk_layer(xs, mod_lat, p, lat_consts, state_rglru[:, l], cache, lat_cfg)
    return (xp.reshape(bc, lc, d), xs.reshape(bl, ll, d),
            jnp.stack(ckv_list, axis=1), jnp.stack(kr_list, axis=1), jnp.stack(lru_list, axis=1))
```

```python
import functools
import math

import numpy as np
import jax
import jax.numpy as jnp
from jax import lax
from jax.experimental import pallas as pl
from jax.experimental.pallas import tpu as pltpu
from jax.experimental.pallas import tpu_sc as plsc

F32 = jnp.float32
BF16 = jnp.bfloat16

D_MODEL = 1024
DEPTH = 2
GRID_W = 64
EPS = 1e-6
FNET_W = 256
FNET_GROUPS = 4
LRU_W = 256
LRU_BLOCKS = 4
LRU_C = 8.0
SC_W = 256
MLA_HEADS = 8
Q_LORA = 384
KV_LORA = 256
NOPE = 64
ROPE = 32
V_DIM = 64
QK_DIM = NOPE + ROPE
MLA_OUT = MLA_HEADS * V_DIM
N_FREQ = ROPE // 4
ROPE_BASE = 10000.0
ATTN_SCALE = QK_DIM ** -0.5
LOG2E = math.log2(math.e)
N_GROUPS = 4
EXP_PER_GROUP = 8
N_EXPERTS = N_GROUPS * EXP_PER_GROUP
D_EXPERT = 256

LANE = 128
SUB = 8
HEAD_PAD = LANE
HP = MLA_HEADS * HEAD_PAD
ROPE_X1 = 0
ROPE_X2 = LANE // 2
DEN_LANE = V_DIM
Z_QKV = 6 * 256
Z_KR = Z_QKV + Q_LORA + KV_LORA
Z_W = Z_KR + LANE
QKV_W = Q_LORA + KV_LORA + LANE
VMEM_LIMIT = 52 * 1024 * 1024
PACK_W = D_MODEL // 2
HI_MASK = -65536
RT_ID0, RT_ID1, RT_W0, RT_W1 = 0, 1, 2, 3
RT_ROWS = 48
TOP_K = 2
PLAN_BLK = 256
DFT_SPLIT = 64
ATTN_SUB = 256
SC_CORES = 2
SC_SUBCORES = 16
SC_CHUNK = 128


def _cparams(sem):
    return pltpu.CompilerParams(dimension_semantics=sem, vmem_limit_bytes=VMEM_LIMIT)


def _bdot(a, b):
    return jnp.dot(a, b, preferred_element_type=F32)


def _rms(x, g):
    return x * lax.rsqrt(jnp.mean(x * x, axis=-1, keepdims=True) + EPS) * g


def _ada_kernel(c_ref, w_ref, b_ref, o_ref):
    c = c_ref[...]
    s = (c * jax.nn.sigmoid(c)).astype(BF16)
    o_ref[...] = _bdot(s, w_ref[...].astype(BF16)) + b_ref[...]


def _ada(cond8, w_ada, b_ada):
    nblk = 6 * D_MODEL // 1024
    return pl.pallas_call(
        _ada_kernel,
        grid=(DEPTH, nblk),
        in_specs=[
            pl.BlockSpec((SUB, D_MODEL), lambda l, n: (0, 0)),
            pl.BlockSpec((None, D_MODEL, 1024), lambda l, n: (l, 0, n)),
            pl.BlockSpec((None, 1, 1024), lambda l, n: (l, 0, n)),
        ],
        out_specs=pl.BlockSpec((None, SUB, 1024), lambda l, n: (l, 0, n)),
        out_shape=jax.ShapeDtypeStruct((DEPTH, SUB, 6 * D_MODEL), F32),
        compiler_params=_cparams(("arbitrary", "arbitrary")),
        name="ada",
    )(cond8, w_ada, b_ada.reshape(DEPTH, 1, 6 * D_MODEL))


def _front_kernel(x_ref, mod_ref, g_ref, w_ref, z_ref):
    x = x_ref[...]
    sh = mod_ref[:, 0:D_MODEL]
    sc = mod_ref[:, D_MODEL:2 * D_MODEL]
    h = _rms(x, g_ref[...]) * (1.0 + sc) + sh
    z_ref[...] = _bdot(h.astype(BF16), w_ref[...])


def _front(x, mod, g1, w_in, tm):
    t = x.shape[0]
    per_mod = t // (mod.shape[0] * tm)
    return pl.pallas_call(
        _front_kernel,
        grid=(t // tm,),
        in_specs=[
            pl.BlockSpec((tm, D_MODEL), lambda i: (i, 0)),
            pl.BlockSpec((None, 1, 6 * D_MODEL), lambda i: (i // per_mod, 0, 0)),
            pl.BlockSpec((1, D_MODEL), lambda i: (0, 0)),
            pl.BlockSpec((D_MODEL, Z_W), lambda i: (0, 0)),
        ],
        out_specs=pl.BlockSpec((tm, Z_W), lambda i: (i, 0)),
        out_shape=jax.ShapeDtypeStruct((t, Z_W), F32),
        compiler_params=_cparams(("arbitrary",)),
        name="front",
    )(x, mod, g1, w_in)


def _rope(x, cos, sinr):
    return x * cos + pltpu.roll(x, LANE // 2, 1) * sinr


def _head_norm(x, g):
    ss = jnp.sum(x * x, axis=-1, keepdims=True) * (1.0 / QK_DIM)
    return x * lax.rsqrt(ss + EPS) * g


def _head_norm_rope_t(x_t, g_col, cos_t, sinr_t):
    ss = jnp.sum(x_t * x_t, axis=0, keepdims=True) * (1.0 / QK_DIM)
    x_t = x_t * lax.rsqrt(ss + EPS) * g_col
    return (x_t * cos_t + pltpu.roll(x_t, HEAD_PAD // 2, 0) * sinr_t).astype(BF16)


def _build_kv(ckv, kr_tile, cos_t, sinr_t, wk_t_ref, wv_ref, gk_col, eye_ref, kt_ref, v_ref):
    nt = (((1,), (1,)), ((), ()))
    cb = ckv.astype(BF16)
    kn_t = lax.dot_general(wk_t_ref[...], cb, nt, preferred_element_type=F32)
    kr_hi = kr_tile.astype(BF16)
    kr_lo = (kr_tile - kr_hi.astype(F32)).astype(BF16)
    kr_t = (lax.dot_general(eye_ref[...], kr_hi, nt, preferred_element_type=F32)
            + lax.dot_general(eye_ref[...], kr_lo, nt, preferred_element_type=F32))
    for h in range(MLA_HEADS):
        sl = slice(h * HEAD_PAD, (h + 1) * HEAD_PAD)
        kt_ref[sl, :] = _head_norm_rope_t(kn_t[sl, :] + kr_t, gk_col, cos_t, sinr_t)
    v = _bdot(cb, wv_ref[...])
    lane = lax.broadcasted_iota(jnp.int32, v.shape, 1)
    v_ref[...] = jnp.where((lane & (HEAD_PAD - 1)) == DEN_LANE, 1.0, v).astype(BF16)


def _qkv_kernel(z_ref, cos_t_ref, sinr_t_ref, gqa_ref, wq_ref, gkva_ref, wk_ref, wv_ref,
                gq_ref, gk_ref, eye_ref, q_ref, k_ref, v_ref, ckv_ref):
    cos_t, sinr_t = cos_t_ref[...], sinr_t_ref[...]
    q_c = z_ref[:, 0:Q_LORA]
    kv_c = z_ref[:, Q_LORA:Q_LORA + KV_LORA]
    kr_tile = z_ref[:, Q_LORA + KV_LORA:QKV_W]
    nt = (((1,), (1,)), ((), ()))
    qf_t = lax.dot_general(wq_ref[...], _rms(q_c, gqa_ref[...]).astype(BF16), nt, preferred_element_type=F32)
    gq_col = gq_ref[...] * (ATTN_SCALE * LOG2E)
    for h in range(MLA_HEADS):
        sl = slice(h * HEAD_PAD, (h + 1) * HEAD_PAD)
        q_ref[sl, :] = _head_norm_rope_t(qf_t[sl, :], gq_col, cos_t, sinr_t)
    ckv = _rms(kv_c, gkva_ref[...])
    ckv_ref[...] = ckv
    _build_kv(ckv, kr_tile, cos_t, sinr_t, wk_ref, wv_ref, gk_ref[...], eye_ref, k_ref, v_ref)


def _qkv(z, tabs, p, seq_len, tm):
    t = z.shape[0]
    if tm > seq_len:
        tabs = [jnp.tile(a, (tm // seq_len, 1)) for a in tabs]
    tabs_t = [a.T for a in tabs]
    per_seq = max(seq_len // tm, 1)
    tab_t_spec = pl.BlockSpec((LANE, tm), lambda i: (0, i % per_seq))
    full = lambda shape: pl.BlockSpec(shape, lambda i: (0,) * len(shape))
    return pl.pallas_call(
        _qkv_kernel,
        grid=(t // tm,),
        in_specs=[
            pl.BlockSpec((tm, QKV_W), lambda i: (i, Z_QKV // QKV_W)),
            tab_t_spec, tab_t_spec,
            full((1, Q_LORA)), full((HP, Q_LORA)), full((1, KV_LORA)),
            full((HP, KV_LORA)), full((KV_LORA, HP)), full((LANE, 1)), full((LANE, 1)), full((LANE, LANE)),
        ],
        out_specs=[
            pl.BlockSpec((HP, tm), lambda i: (0, i)),
            pl.BlockSpec((HP, tm), lambda i: (0, i)),
            pl.BlockSpec((tm, HP), lambda i: (i, 0)),
            pl.BlockSpec((tm, KV_LORA), lambda i: (i, 0)),
        ],
        out_shape=[
            jax.ShapeDtypeStruct((HP, t), BF16),
            jax.ShapeDtypeStruct((HP, t), BF16),
            jax.ShapeDtypeStruct((t, HP), BF16),
            jax.ShapeDtypeStruct((t, KV_LORA), F32),
        ],
        compiler_params=_cparams(("arbitrary",)),
        name="qkv",
    )(z, tabs_t[0], tabs_t[1], p["g_qa"], p["wq_t"], p["g_kva"], p["wk_t"], p["wv"],
      p["gq_col"], p["gk_col"], p["eye"])


def _kvcache_kernel(ckv_ref, kr_ref, cos_t_ref, sinr_t_ref, wk_ref, wv_ref, gk_ref, eye_ref, k_ref, v_ref):
    _build_kv(ckv_ref[...], kr_ref[...], cos_t_ref[...], sinr_t_ref[...], wk_ref, wv_ref, gk_ref[...], eye_ref,
              k_ref, v_ref)


def _kvcache(ckv, kr_tile, tabs, p):
    t = ckv.shape[0]
    return pl.pallas_call(
        _kvcache_kernel,
        out_shape=[jax.ShapeDtypeStruct((HP, t), BF16), jax.ShapeDtypeStruct((t, HP), BF16)],
        compiler_params=_cparams(None),
        name="kvcache",
    )(ckv, kr_tile, tabs[0].T, tabs[1].T, p["wk_t"], p["wv"], p["gk_col"], p["eye"])


def _fourier_kernel(nb, u_ref, cc_ref, sc_ref, cl_ref, sl_ref, o_ref, a_scr, b_scr):
    @pl.when(pl.program_id(1) == 0)
    def _():
        for b in range(nb):
            u = u_ref[b].astype(BF16)
            a_scr[:, b * FNET_W:(b + 1) * FNET_W] = _bdot(u, cc_ref[...]).astype(BF16)
            b_scr[:, b * FNET_W:(b + 1) * FNET_W] = _bdot(u, sc_ref[...]).astype(BF16)

    y = _bdot(cl_ref[...], a_scr[...]) - _bdot(sl_ref[...], b_scr[...])
    for b in range(nb):
        o_ref[b] = y[:, b * FNET_W:(b + 1) * FNET_W].astype(BF16)


def _fourier_split_kernel(nb, u_ref, cc_ref, sc_ref, ca_ref, sa_ref, cb_ref, sb_ref, o_ref,
                          a_scr, b_scr, cl_scr, sl_scr):
    @pl.when(pl.program_id(1) == 0)
    def _():
        for b in range(nb):
            u = u_ref[b].astype(BF16)
            a_scr[:, b * FNET_W:(b + 1) * FNET_W] = _bdot(u, cc_ref[...]).astype(BF16)
            b_scr[:, b * FNET_W:(b + 1) * FNET_W] = _bdot(u, sc_ref[...]).astype(BF16)

    cb, sb = cb_ref[...], sb_ref[...]
    for j in range(ca_ref.shape[0]):
        ca, sa = ca_ref[j:j + 1, :], sa_ref[j:j + 1, :]
        cl_scr[j * DFT_SPLIT:(j + 1) * DFT_SPLIT, :] = (ca * cb - sa * sb).astype(BF16)
        sl_scr[j * DFT_SPLIT:(j + 1) * DFT_SPLIT, :] = (sa * cb + ca * sb).astype(BF16)
    y = _bdot(cl_scr[...], a_scr[...]) - _bdot(sl_scr[...], b_scr[...])
    for b in range(nb):
        o_ref[b] = y[:, b * FNET_W:(b + 1) * FNET_W].astype(BF16)


def _fourier_split(z3, mats, nb, tq):
    s, l, _ = z3.shape
    cc, sc, ca, sa, cb, sb = mats
    full = lambda a: pl.BlockSpec(a.shape, lambda i, r: (0, 0))
    rows = pl.BlockSpec((tq // DFT_SPLIT, l), lambda i, r: (r, 0))
    return pl.pallas_call(
        functools.partial(_fourier_split_kernel, nb),
        grid=(s // nb, l // tq),
        in_specs=[pl.BlockSpec((nb, l, FNET_W), lambda i, r: (i, 0, 0)), full(cc), full(sc), rows, rows,
                  full(cb), full(sb)],
        out_specs=pl.BlockSpec((nb, tq, FNET_W), lambda i, r: (i, r, 0)),
        out_shape=jax.ShapeDtypeStruct((s, l, FNET_W), BF16),
        scratch_shapes=[pltpu.VMEM((l, nb * FNET_W), BF16), pltpu.VMEM((l, nb * FNET_W), BF16),
                        pltpu.VMEM((tq, l), BF16), pltpu.VMEM((tq, l), BF16)],
        compiler_params=_cparams(("arbitrary", "arbitrary")),
        name="fourier_split",
    )(z3, cc, sc, ca, sa, cb, sb)


def _fourier(z3, mats, nb, tq):
    s, l, _ = z3.shape
    if len(mats) == 6:
        return _fourier_split(z3, mats, nb, tq)
    cc, sc, cl, sl = mats
    return pl.pallas_call(
        functools.partial(_fourier_kernel, nb),
        grid=(s // nb, l // tq),
        in_specs=[
            pl.BlockSpec((nb, l, FNET_W), lambda i, r: (i, 0, 0)),
            pl.BlockSpec((FNET_W, FNET_W), lambda i, r: (0, 0)),
            pl.BlockSpec((FNET_W, FNET_W), lambda i, r: (0, 0)),
            pl.BlockSpec((tq, l), lambda i, r: (r, 0)),
            pl.BlockSpec((tq, l), lambda i, r: (r, 0)),
        ],
        out_specs=pl.BlockSpec((nb, tq, FNET_W), lambda i, r: (i, r, 0)),
        out_shape=jax.ShapeDtypeStruct((s, l, FNET_W), BF16),
        scratch_shapes=[pltpu.VMEM((l, nb * FNET_W), BF16), pltpu.VMEM((l, nb * FNET_W), BF16)],
        compiler_params=_cparams(("arbitrary", "arbitrary")),
        name="fourier",
    )(z3, cc, sc, cl, sl)


def _load_ext(ref, c, rows, nchunks):
    t0 = pl.multiple_of(c * rows, rows)
    main = ref[pl.ds(t0, rows), :]
    lo = pl.multiple_of(jnp.maximum(t0 - SUB, 0), SUB)
    hi = pl.multiple_of(jnp.minimum(t0 + rows, (nchunks - 1) * rows + rows - SUB), SUB)
    prev = jnp.where(c > 0, ref[pl.ds(lo, SUB), :], 0.0)
    nxt = jnp.where(c < nchunks - 1, ref[pl.ds(hi, SUB), :], 0.0)
    return jnp.concatenate([prev, main, nxt], axis=0)


def _shifted(ext, off, rows):
    n = ext.shape[0]
    r = ext if off == 0 else pltpu.roll(ext, (-off) % n, 0)
    return r[SUB:SUB + rows]


def _gelu_tanh(x):
    return 0.5 * x * (1.0 + jnp.tanh(math.sqrt(2.0 / math.pi) * (x + 0.044715 * (x * x * x))))


def _lru_gates(xc, d, wa_ref, wx_ref, ba_ref, bx_ref, lam_ref):
    xb = xc.astype(BF16)
    r = jax.nn.sigmoid(_bdot(xb, wa_ref[d]) + ba_ref[d])
    i = jax.nn.sigmoid(_bdot(xb, wx_ref[d]) + bx_ref[d])
    nl = -lam_ref[d]
    softplus = jnp.maximum(nl, 0.0) + jnp.log1p(jnp.exp(-jnp.abs(nl)))
    la = (-LRU_C) * r * softplus
    a = jnp.exp(la)
    one_m_a2 = -jnp.tanh(la) * (a * a + 1.0)
    b = jnp.sqrt(one_m_a2) * (i * xc)
    return a, b


def _group_scan(a, b, reverse):
    rows = a.shape[0]
    rm = lax.broadcasted_iota(jnp.int32, a.shape, 0) & (SUB - 1)
    for s in (1, 2, 4):
        if reverse:
            sh, m = rows - s, rm + s <= SUB - 1
        else:
            sh, m = s, rm >= s
        a_sh = pltpu.roll(a, sh, 0)
        b_sh = pltpu.roll(b, sh, 0)
        b = jnp.where(m, a * b_sh + b, b)
        a = jnp.where(m, a * a_sh, a)
    return a, b


def _carry_scan(a, b, carry, reverse):
    ng = a.shape[0] // SUB
    out = [None] * ng
    order = range(ng - 1, -1, -1) if reverse else range(ng)
    for g in order:
        hg = a[g * SUB:(g + 1) * SUB] * carry + b[g * SUB:(g + 1) * SUB]
        out[g] = hg
        carry = hg[0:1] if reverse else hg[SUB - 1:SUB]
    return jnp.concatenate(out, axis=0), carry


def _lru_kernel(rows, nchunks, xl_ref, gl_ref, bs_ref, cs_ref, xs_ref, h0_ref, cw_ref, wa_ref, wx_ref,
                ba_ref, bx_ref, lam_ref, sw_ref, yl_ref, ys_ref, fin_ref, xc_scr, hf_scr, hb_scr):
    cw = cw_ref[...]
    sw = sw_ref[...]
    chunk = lambda c: pl.ds(pl.multiple_of(c * rows, rows), rows)

    def conv_body(c, carry):
        ext = _load_ext(xl_ref, c, rows, nchunks)
        xc = _shifted(ext, -2, rows) * cw[0:1]
        for k in range(1, 4):
            xc = xc + _shifted(ext, k - 2, rows) * cw[k:k + 1]
        xc_scr[chunk(c), :] = xc
        return carry

    lax.fori_loop(0, nchunks, conv_body, 0)

    def direction(c, d, carry, h_scr):
        a, b = _lru_gates(xc_scr[chunk(c), :], d, wa_ref, wx_ref, ba_ref, bx_ref, lam_ref)
        a, b = _group_scan(a, b, d == 1)
        h, carry = _carry_scan(a, b, carry, d == 1)
        h_scr[chunk(c), :] = h
        return carry

    def scan_body(j, carries):
        cf, cb = carries
        return direction(j, 0, cf, hf_scr), direction(nchunks - 1 - j, 1, cb, hb_scr)

    cf, cb = lax.fori_loop(0, nchunks, scan_body, (h0_ref[0:1, :], h0_ref[1:2, :]))
    fin_ref[0:1, :] = cf
    fin_ref[1:2, :] = cb

    def out_body(c, carry):
        y = (hf_scr[chunk(c), :] + hb_scr[chunk(c), :]) * _gelu_tanh(gl_ref[chunk(c), :])
        yl_ref[chunk(c), :] = y.astype(BF16)
        ext = _load_ext(cs_ref, c, rows, nchunks) * _load_ext(xs_ref, c, rows, nchunks)
        acc = _shifted(ext, -1, rows) * sw[0:1]
        acc = acc + _shifted(ext, 0, rows) * sw[1:2]
        acc = acc + _shifted(ext, 1, rows) * sw[2:3]
        ys_ref[chunk(c), :] = (bs_ref[chunk(c), :] * acc).astype(BF16)
        return carry

    lax.fori_loop(0, nchunks, out_body, 0)


def _lru(z3, h0, p):
    s, l, _ = z3.shape
    rows = min(l, 1024)
    nchunks = l // rows
    half = lambda blk: pl.BlockSpec((None, l, LANE), lambda i, j: (i, 0, blk + j))
    wspec = lambda shape: pl.BlockSpec((None,) + shape, lambda i, j: (j,) + (0,) * len(shape))
    seq_half = pl.BlockSpec((None, l, LANE), lambda i, j: (i, 0, j))
    return pl.pallas_call(
        functools.partial(_lru_kernel, rows, nchunks),
        grid=(s, 2),
        in_specs=[
            half(2), half(4), half(6), half(8), half(10),
            pl.BlockSpec((None, 2, LANE), lambda i, j: (i, 0, j)),
            wspec((4, LANE)), wspec((2, LANE, LANE)), wspec((2, LANE, LANE)),
            wspec((2, 1, LANE)), wspec((2, 1, LANE)), wspec((2, 1, LANE)), wspec((3, LANE)),
        ],
        out_specs=[seq_half, seq_half, pl.BlockSpec((None, 2, LANE), lambda i, j: (i, 0, j))],
        out_shape=[jax.ShapeDtypeStruct((s, l, LRU_W), BF16), jax.ShapeDtypeStruct((s, l, SC_W), BF16),
                   jax.ShapeDtypeStruct((s, 2, LRU_W), F32)],
        scratch_shapes=[pltpu.VMEM((l, LANE), F32), pltpu.VMEM((l, LANE), F32), pltpu.VMEM((l, LANE), F32)],
        compiler_params=_cparams(("arbitrary", "arbitrary")),
        name="lru",
    )(z3, z3, z3, z3, z3, h0, p["lru_conv"], p["lru_wa"], p["lru_wx"], p["lru_ba"], p["lru_bx"], p["lru_lam"],
      p["sc_conv"])


def _store_head_pair(o_ref, rows, j, o_even, o_odd):
    lane = lax.broadcasted_iota(jnp.int32, o_even.shape, 1)
    pair = jnp.where(lane < V_DIM, o_even, pltpu.roll(o_odd, V_DIM, 1))
    o_ref[rows, j * LANE:(j + 1) * LANE] = pair.astype(BF16)


def _attn_kernel(hp, sub, has_cache, *refs):
    if has_cache:
        q_ref, k_ref, v_ref, kc_ref, vc_ref, o_ref, s_scr, p_scr, m_scr, sc_scr, pc_scr = refs
    else:
        q_ref, k_ref, v_ref, o_ref, s_scr, p_scr, m_scr = refs
    tn = (((0,), (0,)), ((), ()))
    items = [(slice(r * sub, (r + 1) * sub), slice(h * HEAD_PAD, (h + 1) * HEAD_PAD))
             for r in range(q_ref.shape[1] // sub) for h in range(hp)]
    n = len(items)

    def scores(i):
        rows, head = items[i]
        q_t = q_ref[head, rows]
        s = lax.dot_general(q_t, k_ref[head, :], tn, preferred_element_type=F32)
        m = jnp.max(s, axis=-1, keepdims=True)
        s_scr[i % 2] = s
        if has_cache:
            sc = lax.dot_general(q_t, kc_ref[head, :], tn, preferred_element_type=F32)
            m = jnp.maximum(m, jnp.max(sc, axis=-1, keepdims=True))
            sc_scr[i % 2] = sc
        m_scr[i % 2] = jnp.broadcast_to(m, m_scr.shape[1:])

    def probs(i):
        m = m_scr[i % 2][:, 0:1]
        p_scr[i % 2] = jnp.exp2(s_scr[i % 2] - m).astype(BF16)
        if has_cache:
            pc_scr[i % 2] = jnp.exp2(sc_scr[i % 2] - m).astype(BF16)

    pending = {}

    def weighted_values(i):
        rows, head = items[i]
        o = _bdot(p_scr[i % 2], v_ref[:, head])
        if has_cache:
            o = o + _bdot(pc_scr[i % 2], vc_ref[:, head])
        o = o / o[:, DEN_LANE:DEN_LANE + 1]
        h = i % hp
        if h % 2 == 0:
            pending[0] = o
        else:
            _store_head_pair(o_ref, rows, h // 2, pending.pop(0), o)

    scores(0)
    for i in range(n):
        if i + 1 < n:
            scores(i + 1)
        if i > 0:
            weighted_values(i - 1)
        probs(i)
    weighted_values(n - 1)


def _attn_short_kernel(hp, q_ref, k_ref, v_ref, o_ref):
    tn = (((0,), (0,)), ((), ()))
    heads = [slice(h * HEAD_PAD, (h + 1) * HEAD_PAD) for h in range(hp)]
    scores = [lax.dot_general(q_ref[hd, :], k_ref[hd, :], tn, preferred_element_type=F32) for hd in heads]
    probs = [jnp.exp2(s - jnp.max(s, axis=-1, keepdims=True)).astype(BF16) for s in scores]
    outs = []
    for hd, p in zip(heads, probs):
        o = _bdot(p, v_ref[:, hd])
        outs.append(o / o[:, DEN_LANE:DEN_LANE + 1])
    for j in range(hp // 2):
        _store_head_pair(o_ref, slice(None), j, outs[2 * j], outs[2 * j + 1])


def _attn(q, k, v, cache, hp, tq):
    s, l, _ = v.shape
    w = hp * HEAD_PAD
    qspec = pl.BlockSpec((w, tq), lambda i, h, r: (h, i * (l // tq) + r))
    ospec = pl.BlockSpec((None, tq, hp * V_DIM), lambda i, h, r: (i, r, h))
    if cache is None and tq == l:
        kspec = pl.BlockSpec((w, l), lambda i, h, r: (h, i))
        vspec = pl.BlockSpec((None, l, w), lambda i, h, r: (i, 0, h))
        return pl.pallas_call(
            functools.partial(_attn_short_kernel, hp),
            grid=(s, MLA_HEADS // hp, 1),
            in_specs=[qspec, kspec, vspec],
            out_specs=ospec,
            out_shape=jax.ShapeDtypeStruct((s, l, MLA_OUT), BF16),
            compiler_params=_cparams(("arbitrary", "arbitrary", "arbitrary")),
            name="attn_short",
        )(q, k, v)
    kv_mode = dict(pipeline_mode=pl.Buffered(1))
    kspec = pl.BlockSpec((w, l), lambda i, h, r: (h, i), **kv_mode)
    vspec = pl.BlockSpec((None, l, w), lambda i, h, r: (i, 0, h), **kv_mode)
    in_specs = [qspec, kspec, vspec]
    args = [q, k, v]
    if cache is not None:
        lc = cache[1].shape[1]
        in_specs += [pl.BlockSpec((w, lc), lambda i, h, r: (h, i), **kv_mode),
                     pl.BlockSpec((None, lc, w), lambda i, h, r: (i, 0, h), **kv_mode)]
        args += list(cache)
    sub = min(tq, ATTN_SUB)
    scratch = [pltpu.VMEM((2, sub, l), F32), pltpu.VMEM((2, sub, l), BF16), pltpu.VMEM((2, sub, LANE), F32)]
    if cache is not None:
        scratch += [pltpu.VMEM((2, sub, lc), F32), pltpu.VMEM((2, sub, lc), BF16)]
    return pl.pallas_call(
        functools.partial(_attn_kernel, hp, sub, cache is not None),
        grid=(s, MLA_HEADS // hp, l // tq),
        in_specs=in_specs,
        out_specs=ospec,
        out_shape=jax.ShapeDtypeStruct((s, l, MLA_OUT), BF16),
        scratch_shapes=scratch,
        compiler_params=_cparams(("arbitrary", "arbitrary", "arbitrary")),
        name="attn",
    )(*args)


def _route(lt):
    row = lax.broadcasted_iota(jnp.int32, (SUB, lt.shape[1]), 0).astype(F32)
    neg = -jnp.inf
    gl = lt[0:SUB]
    g_ok = row < N_GROUPS
    glm = jnp.where(g_ok, gl, neg)
    gmax = jnp.max(glm, axis=0, keepdims=True)
    gsel = jnp.min(jnp.where(glm == gmax, row, float(SUB)), axis=0, keepdims=True)
    g_w = 1.0 / jnp.sum(jnp.where(g_ok, jnp.exp(gl - gmax), 0.0), axis=0, keepdims=True)
    es = lt[SUB:2 * SUB]
    for g in range(1, N_GROUPS):
        es = jnp.where(gsel == float(g), lt[(g + 1) * SUB:(g + 2) * SUB], es)
    m1 = jnp.max(es, axis=0, keepdims=True)
    i1 = jnp.min(jnp.where(es == m1, row, float(SUB)), axis=0, keepdims=True)
    es2 = jnp.where(row == i1, neg, es)
    m2 = jnp.max(es2, axis=0, keepdims=True)
    i2 = jnp.min(jnp.where(es2 == m2, row, float(SUB)), axis=0, keepdims=True)
    e2 = jnp.exp(m2 - m1)
    inv = g_w / (1.0 + e2)
    base = gsel * EXP_PER_GROUP
    return (jnp.where(row == RT_ID0, base + i1, 0.0) + jnp.where(row == RT_ID1, base + i2, 0.0)
            + jnp.where(row == RT_W0, inv, 0.0) + jnp.where(row == RT_W1, inv * e2, 0.0))


def _pack_bf16_pairs(x):
    k = x.shape[1] // 2
    bits = lax.bitcast_convert_type(x.astype(BF16).astype(F32), jnp.int32)
    return lax.shift_right_logical(bits[:, :k], 16) | (bits[:, k:] & HI_MASK)


def _unpack_bf16_pairs(w):
    lo = lax.bitcast_convert_type(lax.shift_left(w, 16), F32).astype(BF16)
    hi = lax.bitcast_convert_type(w & HI_MASK, F32).astype(BF16)
    return lo, hi


def _back_kernel(x_ref, mod_ref, g1_ref, g2_ref, yf_ref, yl_ref, ys_ref, o_ref,
                 wg_ref, wpf_ref, wpl_ref, wps_ref, wpm_ref, wo_ref, wrh_ref, wrl_ref, br_ref,
                 x1_ref, h2_ref, comb_ref):
    d = D_MODEL
    x = x_ref[...]
    sh1, sc1, gt1 = mod_ref[:, 0:d], mod_ref[:, d:2 * d], mod_ref[:, 2 * d:3 * d]
    sh2, sc2 = mod_ref[:, 3 * d:4 * d], mod_ref[:, 4 * d:5 * d]
    hb = (_rms(x, g1_ref[...]) * (1.0 + sc1) + sh1).astype(BF16)
    merged = None
    for j, (y_ref, w_ref) in enumerate(((yf_ref, wpf_ref), (yl_ref, wpl_ref), (ys_ref, wps_ref), (o_ref, wpm_ref))):
        gate = jax.nn.sigmoid(_bdot(hb, wg_ref[:, j * d:(j + 1) * d]))
        term = gate * _bdot(y_ref[...], w_ref[...])
        merged = term if merged is None else merged + term
    x1 = x + gt1 * _bdot(merged.astype(BF16), wo_ref[...])
    x1_ref[...] = x1
    h2 = _rms(x1, g2_ref[...]) * (1.0 + sc2) + sh2
    h2b = h2.astype(BF16)
    h2_ref[...] = _pack_bf16_pairs(h2)
    h2l = (h2 - h2b.astype(F32)).astype(BF16)
    nt = (((1,), (1,)), ((), ()))
    rdot = lambda w_ref, h: lax.dot_general(w_ref[...], h, nt, preferred_element_type=F32)
    logits_t = rdot(wrh_ref, h2b) + rdot(wrh_ref, h2l) + rdot(wrl_ref, h2b) + br_ref[...]
    comb_ref[...] = _route(logits_t)


def _back(x, mod, yf, yl, ys, o, p, tm):
    t = x.shape[0]
    per_mod = t // (mod.shape[0] * tm)
    row = lambda w: pl.BlockSpec((tm, w), lambda i: (i, 0))
    full = lambda a: pl.BlockSpec(a.shape, lambda i: (0,) * a.ndim, pipeline_mode=pl.Buffered(1))
    weights = [p["w_gate"], p["w_pf"], p["w_pl"], p["w_ps"], p["w_pm"], p["w_out"], p["wr_hi"], p["wr_lo"], p["b_r"]]
    return pl.pallas_call(
        _back_kernel,
        grid=(t // tm,),
        in_specs=[
            row(D_MODEL),
            pl.BlockSpec((None, 1, 6 * D_MODEL), lambda i: (i // per_mod, 0, 0)),
            pl.BlockSpec((1, D_MODEL), lambda i: (0, 0)),
            pl.BlockSpec((1, D_MODEL), lambda i: (0, 0)),
            row(FNET_W), row(LRU_W), row(SC_W), row(MLA_OUT),
        ] + [full(w) for w in weights],
        out_specs=[row(D_MODEL), row(PACK_W), pl.BlockSpec((SUB, tm), lambda i: (0, i))],
        out_shape=[
            jax.ShapeDtypeStruct((t, D_MODEL), F32),
            jax.ShapeDtypeStruct((t, PACK_W), jnp.int32),
            jax.ShapeDtypeStruct((SUB, t), F32),
        ],
        compiler_params=_cparams(("arbitrary",)),
        name="back",
    )(x, mod, p["norm1_g"], p["norm2_g"], yf, yl, ys, o, *weights)


def _plan_kernel(tm, rt_ref, pos_ref, te_ref, nu_ref):
    ids = jnp.concatenate([rt_ref[RT_ID0:RT_ID0 + 1, :], rt_ref[RT_ID1:RT_ID1 + 1, :]], axis=1)
    n_rows = ids.shape[1]
    erow = lax.broadcasted_iota(jnp.int32, (N_EXPERTS, PLAN_BLK), 0).astype(F32)
    col = lax.broadcasted_iota(jnp.int32, (N_EXPERTS, LANE), 0)
    blocks = [(ids[:, b * PLAN_BLK:(b + 1) * PLAN_BLK] == erow).astype(F32) for b in range(n_rows // PLAN_BLK)]
    tri = (lax.broadcasted_iota(jnp.int32, (PLAN_BLK, PLAN_BLK), 0)
           <= lax.broadcasted_iota(jnp.int32, (PLAN_BLK, PLAN_BLK), 1)).astype(BF16)
    prefix = [_bdot(blk.astype(BF16), tri) for blk in blocks]
    counts = prefix[0][:, PLAN_BLK - 1:PLAN_BLK]
    for pre in prefix[1:]:
        counts = counts + pre[:, PLAN_BLK - 1:PLAN_BLK]
    tiles = jnp.floor((counts + float(tm - 1)) * (1.0 / tm))
    tile_end = jnp.broadcast_to(tiles, (N_EXPERTS, LANE))
    for s in (1, 2, 4, 8, 16):
        tile_end = tile_end + jnp.where(col >= s, pltpu.roll(tile_end, s, 0), 0.0)
    base = (tile_end[:, 0:1] - tiles) * float(tm) - 1.0
    for b, (blk, pre) in enumerate(zip(blocks, prefix)):
        pos = jnp.sum(blk * (base + pre), axis=0, keepdims=True)
        pos_ref[:, b * PLAN_BLK:(b + 1) * PLAN_BLK] = pos.astype(jnp.int32)
        base = base + pre[:, PLAN_BLK - 1:PLAN_BLK]
    k = lax.broadcasted_iota(jnp.int32, (N_EXPERTS, LANE), 1).astype(F32)
    owner = jnp.sum((k >= tile_end).astype(F32), axis=0, keepdims=True)
    te_ref[...] = jnp.minimum(owner, float(N_EXPERTS - 1)).astype(jnp.int32)
    nu_ref[...] = jnp.broadcast_to(tile_end[N_EXPERTS - 1:N_EXPERTS, :], nu_ref.shape).astype(jnp.int32)


def _dispatch_plan(rt, tm):
    n_rows = TOP_K * rt.shape[1]
    n_tiles = n_rows // tm + N_EXPERTS
    assert n_tiles <= LANE and tm & (tm - 1) == 0 and n_rows % PLAN_BLK == 0
    pos, te, nu = pl.pallas_call(
        functools.partial(_plan_kernel, tm),
        out_shape=[jax.ShapeDtypeStruct((1, n_rows), jnp.int32), jax.ShapeDtypeStruct((1, LANE), jnp.int32),
                   jax.ShapeDtypeStruct((1, LANE), jnp.int32)],
        compiler_params=_cparams(None),
        name="plan",
    )(rt)
    return pos.reshape(n_rows), te[0, :n_tiles], nu[0, :1], n_tiles


def _sc_worker_rows(n_rows):
    workers = SC_CORES * SC_SUBCORES
    per_w = n_rows // workers
    assert per_w * workers == n_rows and per_w % SC_CHUNK == 0
    return per_w


def _sc_mesh():
    return plsc.VectorSubcoreMesh(core_axis_name="c", subcore_axis_name="s",
                                  num_cores=SC_CORES, num_subcores=SC_SUBCORES)


def _sc_scatter_rows(src, pos, n_out):
    t, w = src.shape
    per_w = _sc_worker_rows(pos.shape[0])

    @functools.partial(
        pl.kernel, mesh=_sc_mesh(), out_type=jax.ShapeDtypeStruct((n_out, w), src.dtype),
        scratch_types=[pltpu.VMEM((SC_CHUNK,), jnp.int32), pltpu.VMEM((SC_CHUNK, w), src.dtype)],
        name="moe_dispatch")
    def run(src_hbm, pos_hbm, out_hbm, idx_v, rows_v):
        wid = lax.axis_index("s") * SC_CORES + lax.axis_index("c")

        @pl.loop(0, per_w // SC_CHUNK)
        def _(i):
            j0 = wid * per_w + i * SC_CHUNK
            pltpu.sync_copy(pos_hbm.at[pl.ds(j0, SC_CHUNK)], idx_v)
            pltpu.sync_copy(src_hbm.at[pl.ds(lax.rem(j0, t), SC_CHUNK)], rows_v)
            pltpu.sync_copy(rows_v, out_hbm.at[idx_v])

    return run(src, pos)


def _sc_gather_rows(src, pos):
    w = src.shape[1]
    n = pos.shape[0]
    per_w = _sc_worker_rows(n)

    @functools.partial(
        pl.kernel, mesh=_sc_mesh(), out_type=jax.ShapeDtypeStruct((n, w), src.dtype),
        scratch_types=[pltpu.VMEM((SC_CHUNK,), jnp.int32), pltpu.VMEM((SC_CHUNK, w), src.dtype)],
        name="moe_return")
    def run(src_hbm, pos_hbm, out_hbm, idx_v, rows_v):
        wid = lax.axis_index("s") * SC_CORES + lax.axis_index("c")

        @pl.loop(0, per_w // SC_CHUNK)
        def _(i):
            j0 = wid * per_w + i * SC_CHUNK
            pltpu.sync_copy(pos_hbm.at[pl.ds(j0, SC_CHUNK)], idx_v)
            pltpu.sync_copy(src_hbm.at[idx_v], rows_v)
            pltpu.sync_copy(rows_v, out_hbm.at[pl.ds(j0, SC_CHUNK)])

    return run(src, pos)


def _experts_kernel(te_ref, nu_ref, x_ref, w1_ref, w3_ref, w2_ref, y_ref):
    half = D_MODEL // 2

    @pl.when(pl.program_id(0) < nu_ref[0])
    def _():
        lo, hi = _unpack_bf16_pairs(x_ref[...])
        a = _bdot(lo, w1_ref[:half, :].astype(BF16)) + _bdot(hi, w1_ref[half:, :].astype(BF16))
        u = _bdot(lo, w3_ref[:half, :].astype(BF16)) + _bdot(hi, w3_ref[half:, :].astype(BF16))
        mid = (a * jax.nn.sigmoid(a)) * u
        y_ref[...] = _pack_bf16_pairs(_bdot(mid.astype(BF16), w2_ref[...].astype(BF16)))


def _experts(xs, tile_expert, n_used, n_tiles, w1, w3, w2, layer, tm):
    wspec = lambda a: pl.BlockSpec((None, None) + a.shape[2:], lambda i, te, nu: (layer, te[i], 0, 0))
    grid_spec = pltpu.PrefetchScalarGridSpec(
        num_scalar_prefetch=2,
        grid=(n_tiles,),
        in_specs=[pl.BlockSpec((tm, PACK_W), lambda i, te, nu: (jnp.minimum(i, nu[0] - 1), 0)),
                  wspec(w1), wspec(w3), wspec(w2)],
        out_specs=pl.BlockSpec((tm, PACK_W), lambda i, te, nu: (jnp.where(i < nu[0], i, n_tiles), 0)),
    )
    return pl.pallas_call(
        _experts_kernel,
        grid_spec=grid_spec,
        out_shape=jax.ShapeDtypeStruct((xs.shape[0] + tm, PACK_W), jnp.int32),
        compiler_params=_cparams(("arbitrary",)),
        name="experts",
    )(tile_expert, n_used, xs, w1, w3, w2)


def _combine_kernel(x1_ref, mod_ref, rt_ref, g0_ref, g1_ref, o_ref):
    rt = rt_ref[...]
    lo0, hi0 = _unpack_bf16_pairs(g0_ref[...])
    lo1, hi1 = _unpack_bf16_pairs(g1_ref[...])
    w0 = rt[:, RT_W0:RT_W0 + 1]
    w1 = rt[:, RT_W1:RT_W1 + 1]
    half = D_MODEL // 2
    gt2 = mod_ref[:, 5 * D_MODEL:6 * D_MODEL]
    o_ref[:, :half] = x1_ref[:, :half] + gt2[:, :half] * (w0 * lo0.astype(F32) + w1 * lo1.astype(F32))
    o_ref[:, half:] = x1_ref[:, half:] + gt2[:, half:] * (w0 * hi0.astype(F32) + w1 * hi1.astype(F32))


def _combine(x1, mod, rt, g, tm):
    t = x1.shape[0]
    per_mod = t // (mod.shape[0] * tm)
    return pl.pallas_call(
        _combine_kernel,
        grid=(t // tm,),
        in_specs=[
            pl.BlockSpec((tm, D_MODEL), lambda i: (i, 0)),
            pl.BlockSpec((None, 1, 6 * D_MODEL), lambda i: (i // per_mod, 0, 0)),
            pl.BlockSpec((tm, LANE), lambda i: (i, 0)),
            pl.BlockSpec((tm, PACK_W), lambda i: (i, 0)),
            pl.BlockSpec((tm, PACK_W), lambda i: (i + t // tm, 0)),
        ],
        out_specs=pl.BlockSpec((tm, D_MODEL), lambda i: (i, 0)),
        out_shape=jax.ShapeDtypeStruct((t, D_MODEL), F32),
        compiler_params=_cparams(("arbitrary",)),
        name="combine",
    )(x1, mod, rt, g, g)


def _moe(h2p, rt, x1, mod, p, tm_e, tm_c):
    pos, tile_expert, n_used, n_tiles = _dispatch_plan(rt, tm_e)
    xs = _sc_scatter_rows(h2p, pos, n_tiles * tm_e)
    ys = _experts(xs, tile_expert, n_used, n_tiles, p["w1"], p["w3"], p["w2"], p["layer"], tm_e)
    g = _sc_gather_rows(ys, pos)
    rt_cols = jnp.pad(rt.T, ((0, 0), (0, LANE - rt.shape[0])))
    return _combine(x1, mod, rt_cols, g, tm_c)


def _channel_dft():
    n = np.arange(FNET_W // FNET_GROUPS)
    ang = 2.0 * np.pi * ((n[:, None] * n[None, :]) % n.size) / n.size
    eye = np.eye(FNET_GROUPS)
    scale = 1.0 / math.sqrt(n.size)
    return (jnp.asarray(np.kron(eye, np.cos(ang) * scale), F32).astype(BF16),
            jnp.asarray(np.kron(eye, np.sin(ang) * scale), F32).astype(BF16))


def _position_dft(l):
    scale = 1.0 / math.sqrt(l)
    if l <= 256:
        n = np.arange(l)
        ang = 2.0 * np.pi * ((n[:, None] * n[None, :]) % l) / l
        return jnp.asarray(np.cos(ang) * scale, F32).astype(BF16), jnp.asarray(np.sin(ang) * scale, F32).astype(BF16)
    m = DFT_SPLIT
    n = np.arange(l)
    ang_a = 2.0 * np.pi * ((np.arange(l // m)[:, None] * m * n[None, :]) % l) / l
    ang_b = 2.0 * np.pi * ((np.arange(m)[:, None] * n[None, :]) % l) / l
    return (jnp.asarray(np.cos(ang_a), F32), jnp.asarray(np.sin(ang_a), F32),
            jnp.asarray(np.cos(ang_b) * scale, F32), jnp.asarray(np.sin(ang_b) * scale, F32))


def _rope_tables(l, rotate):
    cos = np.ones((l, LANE))
    sinr = np.zeros((l, LANE))
    if rotate:
        t = np.arange(l)
        inv = ROPE_BASE ** (-np.arange(N_FREQ) / N_FREQ)
        ang = np.concatenate([(t // GRID_W)[:, None] * inv, (t % GRID_W)[:, None] * inv], axis=-1)
        half = ROPE // 2
        for lo in (ROPE_X1, ROPE_X2):
            cos[:, lo:lo + half] = np.cos(ang)
        sinr[:, ROPE_X1:ROPE_X1 + half] = -np.sin(ang)
        sinr[:, ROPE_X2:ROPE_X2 + half] = np.sin(ang)
    return tuple(jnp.asarray(a, F32) for a in (cos, sinr))


def _head_lane_source():
    half = ROPE // 2
    src = np.full((HEAD_PAD,), QK_DIM, np.int32)
    src[ROPE_X1:ROPE_X1 + half] = NOPE + np.arange(half)
    src[ROPE_X2:ROPE_X2 + half] = NOPE + half + np.arange(half)
    free = [i for i in range(HEAD_PAD) if src[i] == QK_DIM][:NOPE]
    src[free] = np.arange(NOPE)
    return src


def _place_head_dims(w, n_src):
    src = _head_lane_source()
    src = np.where(src < n_src, src, n_src)
    wz = jnp.concatenate([w[..., :n_src], jnp.zeros(w.shape[:-1] + (1,), w.dtype)], axis=-1)
    out = jnp.take(wz, jnp.asarray(src), axis=-1)
    return out.reshape(out.shape[:-2] + (out.shape[-2] * HEAD_PAD,))


def _place_rope_key(kr):
    half = ROPE // 2
    out = jnp.zeros(kr.shape[:-1] + (LANE,), kr.dtype)
    return out.at[..., ROPE_X1:ROPE_X1 + half].set(kr[..., :half]).at[..., ROPE_X2:ROPE_X2 + half].set(kr[..., half:])


def _take_rope_key(tile):
    half = ROPE // 2
    return jnp.concatenate([tile[..., ROPE_X1:ROPE_X1 + half], tile[..., ROPE_X2:ROPE_X2 + half]], axis=-1)


def _pad_heads(w, lo, hi):
    r = w.shape[0]
    part = w[:, :, lo:hi]
    out = jnp.zeros((r, MLA_HEADS, HEAD_PAD), w.dtype).at[:, :, : hi - lo].set(part)
    return out.reshape(r, HP)


def _blockdiag_halves(w):
    bw = LRU_W // LRU_BLOCKS
    out = jnp.zeros((2, 2, LANE, LANE), w.dtype)
    for half in range(2):
        for k in range(2):
            n = 2 * half + k
            out = out.at[half, :, k * bw:(k + 1) * bw, k * bw:(k + 1) * bw].set(w[:, n])
    return out


def _halves(v):
    return jnp.moveaxis(v.reshape(v.shape[:-1] + (2, LANE)), -2, 0)


def _layer_params(l, a):
    d = D_MODEL
    w_in = a["w_in"][l]
    w_in = jnp.concatenate([w_in[:, :Z_KR], _place_rope_key(w_in[:, Z_KR:])], axis=1)
    w_r = jnp.zeros((RT_ROWS, d), F32).at[:N_GROUPS].set(a["w_gr"][l].T).at[SUB:SUB + N_EXPERTS].set(a["w_er"][l].T)
    b_r = jnp.zeros((RT_ROWS, 1), F32).at[:N_GROUPS, 0].set(a["b_gr"][l]).at[SUB:SUB + N_EXPERTS, 0].set(a["b_er"][l])
    wr_hi = w_r.astype(BF16)
    gpad = lambda g: jnp.take(jnp.concatenate([g, jnp.zeros((1,), F32)]), jnp.asarray(_head_lane_source())).reshape(1, LANE)
    return {
        "norm1_g": a["norm1_g"][l].reshape(1, d), "norm2_g": a["norm2_g"][l].reshape(1, d),
        "w_in": w_in.astype(BF16),
        "lru_conv": _halves(a["lru_conv"][l]),
        "lru_wa": _blockdiag_halves(a["lru_wa"][l]).astype(BF16),
        "lru_wx": _blockdiag_halves(a["lru_wx"][l]).astype(BF16),
        "lru_ba": _halves(a["lru_ba"][l])[:, :, None, :], "lru_bx": _halves(a["lru_bx"][l])[:, :, None, :],
        "lru_lam": _halves(a["lru_lam"][l])[:, :, None, :],
        "sc_conv": _halves(a["sc_conv"][l]),
        "g_qa": a["g_qa"][l].reshape(1, Q_LORA), "g_kva": a["g_kva"][l].reshape(1, KV_LORA),
        "wq_t": _place_head_dims(a["w_qb"][l], QK_DIM).T.astype(BF16),
        "wk_t": _place_head_dims(a["w_kvb"][l], NOPE).T.astype(BF16),
        "wv": _pad_heads(a["w_kvb"][l], NOPE, NOPE + V_DIM).astype(BF16),
        "gq_col": gpad(a["g_qn"][l]).reshape(LANE, 1), "gk_col": gpad(a["g_kn"][l]).reshape(LANE, 1),
        "eye": jnp.eye(LANE, dtype=F32).astype(BF16),
        "w_gate": a["w_gate"][l].astype(BF16),
        "w_pf": a["w_pf"][l].astype(BF16), "w_pl": a["w_pl"][l].astype(BF16), "w_ps": a["w_ps"][l].astype(BF16),
        "w_pm": a["w_pm"][l].astype(BF16), "w_out": a["w_out"][l].astype(BF16),
        "wr_hi": wr_hi, "wr_lo": (w_r - wr_hi.astype(F32)).astype(BF16), "b_r": b_r,
        "w1": a["w1"], "w3": a["w3"], "w2": a["w2"], "layer": l,
    }


def _trunk_layer(x, mod, p, consts, h0, cache, cfg):
    s, l = cfg["s"], cfg["l"]
    z = _front(x, mod, p["norm1_g"], p["w_in"], cfg["tm_front"])
    z3 = z.reshape(s, l, Z_W)
    q, k, v, ckv = _qkv(z, consts["rope"], p, l, cfg["tm_qkv"])
    yf = _fourier(z3, consts["dft"], cfg["nb"], cfg["tq_f"])
    yl, ys, fin = _lru(z3, h0, p)
    o = _attn(q, k, v.reshape(s, l, HP), cache, cfg["hp"], cfg["tq_a"])
    t = s * l
    x1, h2, comb = _back(x, mod, yf.reshape(t, FNET_W), yl.reshape(t, LRU_W), ys.reshape(t, SC_W),
                         o.reshape(t, MLA_OUT), p, cfg["tm_back"])
    x2 = _moe(h2, comb, x1, mod, p, cfg["tm_moe"], cfg["tm_comb"])
    return x2, z3, ckv, fin


def kernel(x_prompt, x_sample, cache_ckv, cache_krope, state_rglru, c, c_ctx, norm1_g, norm2_g, w_ada, b_ada, w_in, lru_conv, lru_wa, lru_ba, lru_wx, lru_bx, lru_lam, sc_conv, g_qa, w_qb, g_kva, w_kvb, g_qn, g_kn, w_pf, w_pl, w_ps, w_pm, w_gate, w_out, w_gr, b_gr, w_er, b_er, w1, w3, w2):
    a = dict(norm1_g=norm1_g, norm2_g=norm2_g, w_in=w_in, lru_conv=lru_conv, lru_wa=lru_wa, lru_ba=lru_ba,
             lru_wx=lru_wx, lru_bx=lru_bx, lru_lam=lru_lam, sc_conv=sc_conv, g_qa=g_qa, w_qb=w_qb, g_kva=g_kva,
             w_kvb=w_kvb, g_qn=g_qn, g_kn=g_kn, w_pf=w_pf, w_pl=w_pl, w_ps=w_ps, w_pm=w_pm, w_gate=w_gate,
             w_out=w_out, w_gr=w_gr, b_gr=b_gr, w_er=w_er, b_er=b_er, w1=w1, w3=w3, w2=w2)
    bc, lc, d = x_prompt.shape
    bl, ll, _ = x_sample.shape
    past = cache_ckv.shape[2]

    cond8 = jnp.zeros((SUB, d), F32).at[0].set(c_ctx).at[1:1 + bl].set(c)
    mod_all = _ada(cond8, w_ada, b_ada)

    cc, sc = _channel_dft()
    ctx_consts = {"dft": (cc, sc) + _position_dft(lc), "rope": _rope_tables(lc, False)}
    lat_consts = {"dft": (cc, sc) + _position_dft(ll), "rope": _rope_tables(ll, True)}
    cache_tabs = _rope_tables(bl * past, False)
    ctx_cfg = dict(s=bc, l=lc, tm_front=512, tm_qkv=512, nb=8, tq_f=lc, hp=MLA_HEADS, tq_a=lc, tm_back=1024, tm_moe=512, tm_comb=512)
    lat_cfg = dict(s=bl, l=ll, tm_front=512, tm_qkv=512, nb=bl, tq_f=512, hp=MLA_HEADS, tq_a=256, tm_back=1024, tm_moe=512, tm_comb=512)

    xp = x_prompt.reshape(bc * lc, d)
    xs = x_sample.reshape(bl * ll, d)
    h0_ctx = jnp.zeros((bc, 2, LRU_W), F32)
    ckv_list, kr_list, lru_list = [], [], []
    for l in range(DEPTH):
        p = _layer_params(l, a)
        mod_ctx = mod_all[l, 0:1].reshape(1, 1, 6 * d)
        mod_lat = mod_all[l, 1:1 + bl].reshape(bl, 1, 6 * d)
        xp, z3, ckv, fin = _trunk_layer(xp, mod_ctx, p, ctx_consts, h0_ctx, None, ctx_cfg)
        ckv_list.append(ckv.reshape(bc, lc, KV_LORA))
        kr_list.append(_take_rope_key(z3[:, :, Z_KR:]))
        lru_list.append(fin)
        kr_tile = _place_rope_key(cache_krope[:, l].reshape(bl * past, ROPE))
        kc, vc = _kvcache(cache_ckv[:, l].reshape(bl * past, KV_LORA), kr_tile, cache_tabs, p)
        cache = (kc, vc.reshape(bl, past, HP))
        xs, _, _, _ = _trunk_layer(xs, mod_lat, p, lat_consts, state_rglru[:, l], cache, lat_cfg)
    return (xp.reshape(bc, lc, d), xs.reshape(bl, ll, d),
            jnp.stack(ckv_list, axis=1), jnp.stack(kr_list, axis=1), jnp.stack(lru_list, axis=1))
```

```python
import functools
import math

import numpy as np
import jax
import jax.numpy as jnp
from jax import lax
from jax.experimental import pallas as pl
from jax.experimental.pallas import tpu as pltpu
from jax.experimental.pallas import tpu_sc as plsc

F32 = jnp.float32
BF16 = jnp.bfloat16

D_MODEL = 1024
DEPTH = 2
GRID_W = 64
EPS = 1e-6
FNET_W = 256
FNET_GROUPS = 4
LRU_W = 256
LRU_BLOCKS = 4
LRU_C = 8.0
SC_W = 256
MLA_HEADS = 8
Q_LORA = 384
KV_LORA = 256
NOPE = 64
ROPE = 32
V_DIM = 64
QK_DIM = NOPE + ROPE
MLA_OUT = MLA_HEADS * V_DIM
N_FREQ = ROPE // 4
ROPE_BASE = 10000.0
ATTN_SCALE = QK_DIM ** -0.5
LOG2E = math.log2(math.e)
N_GROUPS = 4
EXP_PER_GROUP = 8
N_EXPERTS = N_GROUPS * EXP_PER_GROUP
D_EXPERT = 256

LANE = 128
SUB = 8
HEAD_PAD = LANE
HP = MLA_HEADS * HEAD_PAD
ROPE_X1 = 0
ROPE_X2 = LANE // 2
DEN_LANE = V_DIM
Z_QKV = 6 * 256
Z_KR = Z_QKV + Q_LORA + KV_LORA
Z_W = Z_KR + LANE
QKV_W = Q_LORA + KV_LORA + LANE
VMEM_LIMIT = 52 * 1024 * 1024
PACK_W = D_MODEL // 2
HI_MASK = -65536
RT_ID0, RT_ID1, RT_W0, RT_W1 = 0, 1, 2, 3
RT_ROWS = 48
TOP_K = 2
MERGE_BLK = 256
PLAN_BLK = 256
DFT_SPLIT = 64
ATTN_SUB = 256
SC_CORES = 2
SC_SUBCORES = 16
SC_CHUNK = 128


def _cparams(sem):
    return pltpu.CompilerParams(dimension_semantics=sem, vmem_limit_bytes=VMEM_LIMIT)


def _bdot(a, b):
    return jnp.dot(a, b, preferred_element_type=F32)


def _rms(x, g):
    return x * lax.rsqrt(jnp.mean(x * x, axis=-1, keepdims=True) + EPS) * g


def _ada_kernel(c_ref, w_ref, b_ref, o_ref):
    c = c_ref[...]
    s = (c * jax.nn.sigmoid(c)).astype(BF16)
    o_ref[...] = _bdot(s, w_ref[...].astype(BF16)) + b_ref[...]


def _ada(cond8, w_ada, b_ada):
    nblk = 6 * D_MODEL // 1024
    return pl.pallas_call(
        _ada_kernel,
        grid=(DEPTH, nblk),
        in_specs=[
            pl.BlockSpec((SUB, D_MODEL), lambda l, n: (0, 0)),
            pl.BlockSpec((None, D_MODEL, 1024), lambda l, n: (l, 0, n)),
            pl.BlockSpec((None, 1, 1024), lambda l, n: (l, 0, n)),
        ],
        out_specs=pl.BlockSpec((None, SUB, 1024), lambda l, n: (l, 0, n)),
        out_shape=jax.ShapeDtypeStruct((DEPTH, SUB, 6 * D_MODEL), F32),
        compiler_params=_cparams(("arbitrary", "arbitrary")),
        name="ada",
    )(cond8, w_ada, b_ada.reshape(DEPTH, 1, 6 * D_MODEL))


def _front_kernel(x_ref, mod_ref, g_ref, w_ref, z_ref):
    x = x_ref[...]
    sh = mod_ref[:, 0:D_MODEL]
    sc = mod_ref[:, D_MODEL:2 * D_MODEL]
    h = _rms(x, g_ref[...]) * (1.0 + sc) + sh
    z_ref[...] = _bdot(h.astype(BF16), w_ref[...])


def _front(x, mod, g1, w_in, tm):
    t = x.shape[0]
    per_mod = t // (mod.shape[0] * tm)
    return pl.pallas_call(
        _front_kernel,
        grid=(t // tm,),
        in_specs=[
            pl.BlockSpec((tm, D_MODEL), lambda i: (i, 0)),
            pl.BlockSpec((None, 1, 6 * D_MODEL), lambda i: (i // per_mod, 0, 0)),
            pl.BlockSpec((1, D_MODEL), lambda i: (0, 0)),
            pl.BlockSpec((D_MODEL, Z_W), lambda i: (0, 0)),
        ],
        out_specs=pl.BlockSpec((tm, Z_W), lambda i: (i, 0)),
        out_shape=jax.ShapeDtypeStruct((t, Z_W), F32),
        compiler_params=_cparams(("arbitrary",)),
        name="front",
    )(x, mod, g1, w_in)


def _rope(x, cos, sinr):
    return x * cos + pltpu.roll(x, LANE // 2, 1) * sinr


def _head_norm(x, g):
    ss = jnp.sum(x * x, axis=-1, keepdims=True) * (1.0 / QK_DIM)
    return x * lax.rsqrt(ss + EPS) * g


def _head_norm_rope_t(x_t, g_col, cos_t, sinr_t):
    ss = jnp.sum(x_t * x_t, axis=0, keepdims=True) * (1.0 / QK_DIM)
    x_t = x_t * lax.rsqrt(ss + EPS) * g_col
    return (x_t * cos_t + pltpu.roll(x_t, HEAD_PAD // 2, 0) * sinr_t).astype(BF16)


def _build_kv(ckv, kr_tile, cos_t, sinr_t, wk_t_ref, wv_ref, gk_col, eye_ref, kt_ref, v_ref):
    nt = (((1,), (1,)), ((), ()))
    cb = ckv.astype(BF16)
    kn_t = lax.dot_general(wk_t_ref[...], cb, nt, preferred_element_type=F32)
    kr_hi = kr_tile.astype(BF16)
    kr_lo = (kr_tile - kr_hi.astype(F32)).astype(BF16)
    kr_t = (lax.dot_general(eye_ref[...], kr_hi, nt, preferred_element_type=F32)
            + lax.dot_general(eye_ref[...], kr_lo, nt, preferred_element_type=F32))
    for h in range(MLA_HEADS):
        sl = slice(h * HEAD_PAD, (h + 1) * HEAD_PAD)
        kt_ref[sl, :] = _head_norm_rope_t(kn_t[sl, :] + kr_t, gk_col, cos_t, sinr_t)
    v = _bdot(cb, wv_ref[...])
    lane = lax.broadcasted_iota(jnp.int32, v.shape, 1)
    v_ref[...] = jnp.where((lane & (HEAD_PAD - 1)) == DEN_LANE, 1.0, v).astype(BF16)


def _qkv_kernel(z_ref, cos_t_ref, sinr_t_ref, gqa_ref, wq_ref, gkva_ref, wk_ref, wv_ref,
                gq_ref, gk_ref, eye_ref, q_ref, k_ref, v_ref, ckv_ref):
    cos_t, sinr_t = cos_t_ref[...], sinr_t_ref[...]
    q_c = z_ref[:, 0:Q_LORA]
    kv_c = z_ref[:, Q_LORA:Q_LORA + KV_LORA]
    kr_tile = z_ref[:, Q_LORA + KV_LORA:QKV_W]
    nt = (((1,), (1,)), ((), ()))
    qf_t = lax.dot_general(wq_ref[...], _rms(q_c, gqa_ref[...]).astype(BF16), nt, preferred_element_type=F32)
    gq_col = gq_ref[...] * (ATTN_SCALE * LOG2E)
    for h in range(MLA_HEADS):
        sl = slice(h * HEAD_PAD, (h + 1) * HEAD_PAD)
        q_ref[sl, :] = _head_norm_rope_t(qf_t[sl, :], gq_col, cos_t, sinr_t)
    ckv = _rms(kv_c, gkva_ref[...])
    ckv_ref[...] = ckv
    _build_kv(ckv, kr_tile, cos_t, sinr_t, wk_ref, wv_ref, gk_ref[...], eye_ref, k_ref, v_ref)


def _qkv(z, tabs, p, seq_len, tm):
    t = z.shape[0]
    if tm > seq_len:
        tabs = [jnp.tile(a, (tm // seq_len, 1)) for a in tabs]
    tabs_t = [a.T for a in tabs]
    per_seq = max(seq_len // tm, 1)
    tab_t_spec = pl.BlockSpec((LANE, tm), lambda i: (0, i % per_seq))
    full = lambda shape: pl.BlockSpec(shape, lambda i: (0,) * len(shape))
    return pl.pallas_call(
        _qkv_kernel,
        grid=(t // tm,),
        in_specs=[
            pl.BlockSpec((tm, QKV_W), lambda i: (i, Z_QKV // QKV_W)),
            tab_t_spec, tab_t_spec,
            full((1, Q_LORA)), full((HP, Q_LORA)), full((1, KV_LORA)),
            full((HP, KV_LORA)), full((KV_LORA, HP)), full((LANE, 1)), full((LANE, 1)), full((LANE, LANE)),
        ],
        out_specs=[
            pl.BlockSpec((HP, tm), lambda i: (0, i)),
            pl.BlockSpec((HP, tm), lambda i: (0, i)),
            pl.BlockSpec((tm, HP), lambda i: (i, 0)),
            pl.BlockSpec((tm, KV_LORA), lambda i: (i, 0)),
        ],
        out_shape=[
            jax.ShapeDtypeStruct((HP, t), BF16),
            jax.ShapeDtypeStruct((HP, t), BF16),
            jax.ShapeDtypeStruct((t, HP), BF16),
            jax.ShapeDtypeStruct((t, KV_LORA), F32),
        ],
        compiler_params=_cparams(("arbitrary",)),
        name="qkv",
    )(z, tabs_t[0], tabs_t[1], p["g_qa"], p["wq_t"], p["g_kva"], p["wk_t"], p["wv"],
      p["gq_col"], p["gk_col"], p["eye"])


def _kvcache_kernel(ckv_ref, kr_ref, cos_t_ref, sinr_t_ref, wk_ref, wv_ref, gk_ref, eye_ref, k_ref, v_ref):
    _build_kv(ckv_ref[...], kr_ref[...], cos_t_ref[...], sinr_t_ref[...], wk_ref, wv_ref, gk_ref[...], eye_ref,
              k_ref, v_ref)


def _kvcache(ckv, kr_tile, tabs, p):
    t = ckv.shape[0]
    return pl.pallas_call(
        _kvcache_kernel,
        out_shape=[jax.ShapeDtypeStruct((HP, t), BF16), jax.ShapeDtypeStruct((t, HP), BF16)],
        compiler_params=_cparams(None),
        name="kvcache",
    )(ckv, kr_tile, tabs[0].T, tabs[1].T, p["wk_t"], p["wv"], p["gk_col"], p["eye"])


def _fourier_kernel(nb, u_ref, cc_ref, sc_ref, cl_ref, sl_ref, o_ref, a_scr, b_scr):
    @pl.when(pl.program_id(1) == 0)
    def _():
        for b in range(nb):
            u = u_ref[b].astype(BF16)
            a_scr[:, b * FNET_W:(b + 1) * FNET_W] = _bdot(u, cc_ref[...]).astype(BF16)
            b_scr[:, b * FNET_W:(b + 1) * FNET_W] = _bdot(u, sc_ref[...]).astype(BF16)

    y = _bdot(cl_ref[...], a_scr[...]) - _bdot(sl_ref[...], b_scr[...])
    for b in range(nb):
        o_ref[b] = y[:, b * FNET_W:(b + 1) * FNET_W].astype(BF16)


def _fourier_split_kernel(nb, u_ref, cc_ref, sc_ref, ca_ref, sa_ref, cb_ref, sb_ref, o_ref,
                          a_scr, b_scr, cl_scr, sl_scr):
    @pl.when(pl.program_id(1) == 0)
    def _():
        for b in range(nb):
            u = u_ref[b].astype(BF16)
            a_scr[:, b * FNET_W:(b + 1) * FNET_W] = _bdot(u, cc_ref[...]).astype(BF16)
            b_scr[:, b * FNET_W:(b + 1) * FNET_W] = _bdot(u, sc_ref[...]).astype(BF16)

    cb, sb = cb_ref[...], sb_ref[...]
    for j in range(ca_ref.shape[0]):
        ca, sa = ca_ref[j:j + 1, :], sa_ref[j:j + 1, :]
        cl_scr[j * DFT_SPLIT:(j + 1) * DFT_SPLIT, :] = (ca * cb - sa * sb).astype(BF16)
        sl_scr[j * DFT_SPLIT:(j + 1) * DFT_SPLIT, :] = (sa * cb + ca * sb).astype(BF16)
    y = _bdot(cl_scr[...], a_scr[...]) - _bdot(sl_scr[...], b_scr[...])
    for b in range(nb):
        o_ref[b] = y[:, b * FNET_W:(b + 1) * FNET_W].astype(BF16)


def _fourier_split(z3, mats, nb, tq):
    s, l, _ = z3.shape
    cc, sc, ca, sa, cb, sb = mats
    full = lambda a: pl.BlockSpec(a.shape, lambda i, r: (0, 0))
    rows = pl.BlockSpec((tq // DFT_SPLIT, l), lambda i, r: (r, 0))
    return pl.pallas_call(
        functools.partial(_fourier_split_kernel, nb),
        grid=(s // nb, l // tq),
        in_specs=[pl.BlockSpec((nb, l, FNET_W), lambda i, r: (i, 0, 0)), full(cc), full(sc), rows, rows,
                  full(cb), full(sb)],
        out_specs=pl.BlockSpec((nb, tq, FNET_W), lambda i, r: (i, r, 0)),
        out_shape=jax.ShapeDtypeStruct((s, l, FNET_W), BF16),
        scratch_shapes=[pltpu.VMEM((l, nb * FNET_W), BF16), pltpu.VMEM((l, nb * FNET_W), BF16),
                        pltpu.VMEM((tq, l), BF16), pltpu.VMEM((tq, l), BF16)],
        compiler_params=_cparams(("arbitrary", "arbitrary")),
        name="fourier_split",
    )(z3, cc, sc, ca, sa, cb, sb)


def _fourier(z3, mats, nb, tq):
    s, l, _ = z3.shape
    if len(mats) == 6:
        return _fourier_split(z3, mats, nb, tq)
    cc, sc, cl, sl = mats
    return pl.pallas_call(
        functools.partial(_fourier_kernel, nb),
        grid=(s // nb, l // tq),
        in_specs=[
            pl.BlockSpec((nb, l, FNET_W), lambda i, r: (i, 0, 0)),
            pl.BlockSpec((FNET_W, FNET_W), lambda i, r: (0, 0)),
            pl.BlockSpec((FNET_W, FNET_W), lambda i, r: (0, 0)),
            pl.BlockSpec((tq, l), lambda i, r: (r, 0)),
            pl.BlockSpec((tq, l), lambda i, r: (r, 0)),
        ],
        out_specs=pl.BlockSpec((nb, tq, FNET_W), lambda i, r: (i, r, 0)),
        out_shape=jax.ShapeDtypeStruct((s, l, FNET_W), BF16),
        scratch_shapes=[pltpu.VMEM((l, nb * FNET_W), BF16), pltpu.VMEM((l, nb * FNET_W), BF16)],
        compiler_params=_cparams(("arbitrary", "arbitrary")),
        name="fourier",
    )(z3, cc, sc, cl, sl)


def _load_ext(ref, c, rows, nchunks):
    t0 = pl.multiple_of(c * rows, rows)
    main = ref[pl.ds(t0, rows), :]
    lo = pl.multiple_of(jnp.maximum(t0 - SUB, 0), SUB)
    hi = pl.multiple_of(jnp.minimum(t0 + rows, (nchunks - 1) * rows + rows - SUB), SUB)
    prev = jnp.where(c > 0, ref[pl.ds(lo, SUB), :], 0.0)
    nxt = jnp.where(c < nchunks - 1, ref[pl.ds(hi, SUB), :], 0.0)
    return jnp.concatenate([prev, main, nxt], axis=0)


def _shifted(ext, off, rows):
    n = ext.shape[0]
    r = ext if off == 0 else pltpu.roll(ext, (-off) % n, 0)
    return r[SUB:SUB + rows]


def _gelu_tanh(x):
    return 0.5 * x * (1.0 + jnp.tanh(math.sqrt(2.0 / math.pi) * (x + 0.044715 * (x * x * x))))


def _lru_gates(xc, d, wa_ref, wx_ref, ba_ref, bx_ref, lam_ref):
    xb = xc.astype(BF16)
    r = jax.nn.sigmoid(_bdot(xb, wa_ref[d]) + ba_ref[d])
    i = jax.nn.sigmoid(_bdot(xb, wx_ref[d]) + bx_ref[d])
    nl = -lam_ref[d]
    softplus = jnp.maximum(nl, 0.0) + jnp.log1p(jnp.exp(-jnp.abs(nl)))
    la = (-LRU_C) * r * softplus
    a = jnp.exp(la)
    one_m_a2 = -jnp.tanh(la) * (a * a + 1.0)
    b = jnp.sqrt(one_m_a2) * (i * xc)
    return a, b


def _group_scan(a, b, reverse):
    rows = a.shape[0]
    rm = lax.broadcasted_iota(jnp.int32, a.shape, 0) & (SUB - 1)
    for s in (1, 2, 4):
        if reverse:
            sh, m = rows - s, rm + s <= SUB - 1
        else:
            sh, m = s, rm >= s
        a_sh = pltpu.roll(a, sh, 0)
        b_sh = pltpu.roll(b, sh, 0)
        b = jnp.where(m, a * b_sh + b, b)
        a = jnp.where(m, a * a_sh, a)
    return a, b


def _carry_scan(a, b, carry, reverse):
    ng = a.shape[0] // SUB
    out = [None] * ng
    order = range(ng - 1, -1, -1) if reverse else range(ng)
    for g in order:
        hg = a[g * SUB:(g + 1) * SUB] * carry + b[g * SUB:(g + 1) * SUB]
        out[g] = hg
        carry = hg[0:1] if reverse else hg[SUB - 1:SUB]
    return jnp.concatenate(out, axis=0), carry


def _lru_kernel(rows, nchunks, xl_ref, gl_ref, bs_ref, cs_ref, xs_ref, h0_ref, cw_ref, wa_ref, wx_ref,
                ba_ref, bx_ref, lam_ref, sw_ref, yl_ref, ys_ref, fin_ref, xc_scr, hf_scr, hb_scr):
    cw = cw_ref[...]
    sw = sw_ref[...]
    chunk = lambda c: pl.ds(pl.multiple_of(c * rows, rows), rows)

    def conv_body(c, carry):
        ext = _load_ext(xl_ref, c, rows, nchunks)
        xc = _shifted(ext, -2, rows) * cw[0:1]
        for k in range(1, 4):
            xc = xc + _shifted(ext, k - 2, rows) * cw[k:k + 1]
        xc_scr[chunk(c), :] = xc
        return carry

    lax.fori_loop(0, nchunks, conv_body, 0)

    def direction(c, d, carry, h_scr):
        a, b = _lru_gates(xc_scr[chunk(c), :], d, wa_ref, wx_ref, ba_ref, bx_ref, lam_ref)
        a, b = _group_scan(a, b, d == 1)
        h, carry = _carry_scan(a, b, carry, d == 1)
        h_scr[chunk(c), :] = h
        return carry

    def scan_body(j, carries):
        cf, cb = carries
        return direction(j, 0, cf, hf_scr), direction(nchunks - 1 - j, 1, cb, hb_scr)

    cf, cb = lax.fori_loop(0, nchunks, scan_body, (h0_ref[0:1, :], h0_ref[1:2, :]))
    fin_ref[0:1, :] = cf
    fin_ref[1:2, :] = cb

    def out_body(c, carry):
        y = (hf_scr[chunk(c), :] + hb_scr[chunk(c), :]) * _gelu_tanh(gl_ref[chunk(c), :])
        yl_ref[chunk(c), :] = y.astype(BF16)
        ext = _load_ext(cs_ref, c, rows, nchunks) * _load_ext(xs_ref, c, rows, nchunks)
        acc = _shifted(ext, -1, rows) * sw[0:1]
        acc = acc + _shifted(ext, 0, rows) * sw[1:2]
        acc = acc + _shifted(ext, 1, rows) * sw[2:3]
        ys_ref[chunk(c), :] = (bs_ref[chunk(c), :] * acc).astype(BF16)
        return carry

    lax.fori_loop(0, nchunks, out_body, 0)


def _lru(z3, h0, p):
    s, l, _ = z3.shape
    rows = min(l, 1024)
    nchunks = l // rows
    half = lambda blk: pl.BlockSpec((None, l, LANE), lambda i, j: (i, 0, blk + j))
    wspec = lambda shape: pl.BlockSpec((None,) + shape, lambda i, j: (j,) + (0,) * len(shape))
    seq_half = pl.BlockSpec((None, l, LANE), lambda i, j: (i, 0, j))
    return pl.pallas_call(
        functools.partial(_lru_kernel, rows, nchunks),
        grid=(s, 2),
        in_specs=[
            half(2), half(4), half(6), half(8), half(10),
            pl.BlockSpec((None, 2, LANE), lambda i, j: (i, 0, j)),
            wspec((4, LANE)), wspec((2, LANE, LANE)), wspec((2, LANE, LANE)),
            wspec((2, 1, LANE)), wspec((2, 1, LANE)), wspec((2, 1, LANE)), wspec((3, LANE)),
        ],
        out_specs=[seq_half, seq_half, pl.BlockSpec((None, 2, LANE), lambda i, j: (i, 0, j))],
        out_shape=[jax.ShapeDtypeStruct((s, l, LRU_W), BF16), jax.ShapeDtypeStruct((s, l, SC_W), BF16),
                   jax.ShapeDtypeStruct((s, 2, LRU_W), F32)],
        scratch_shapes=[pltpu.VMEM((l, LANE), F32), pltpu.VMEM((l, LANE), F32), pltpu.VMEM((l, LANE), F32)],
        compiler_params=_cparams(("arbitrary", "arbitrary")),
        name="lru",
    )(z3, z3, z3, z3, z3, h0, p["lru_conv"], p["lru_wa"], p["lru_wx"], p["lru_ba"], p["lru_bx"], p["lru_lam"],
      p["sc_conv"])


def _store_head_pair(o_ref, rows, j, o_even, o_odd):
    lane = lax.broadcasted_iota(jnp.int32, o_even.shape, 1)
    pair = jnp.where(lane < V_DIM, o_even, pltpu.roll(o_odd, V_DIM, 1))
    o_ref[rows, j * LANE:(j + 1) * LANE] = pair.astype(BF16)


def _attn_kernel(hp, sub, has_cache, *refs):
    if has_cache:
        q_ref, k_ref, v_ref, kc_ref, vc_ref, o_ref, s_scr, p_scr, m_scr, sc_scr, pc_scr = refs
    else:
        q_ref, k_ref, v_ref, o_ref, s_scr, p_scr, m_scr = refs
    tn = (((0,), (0,)), ((), ()))
    items = [(slice(r * sub, (r + 1) * sub), slice(h * HEAD_PAD, (h + 1) * HEAD_PAD))
             for r in range(q_ref.shape[1] // sub) for h in range(hp)]
    n = len(items)

    def scores(i):
        rows, head = items[i]
        q_t = q_ref[head, rows]
        s = lax.dot_general(q_t, k_ref[head, :], tn, preferred_element_type=F32)
        m = jnp.max(s, axis=-1, keepdims=True)
        s_scr[i % 2] = s
        if has_cache:
            sc = lax.dot_general(q_t, kc_ref[head, :], tn, preferred_element_type=F32)
            m = jnp.maximum(m, jnp.max(sc, axis=-1, keepdims=True))
            sc_scr[i % 2] = sc
        m_scr[i % 2] = jnp.broadcast_to(m, m_scr.shape[1:])

    def probs(i):
        m = m_scr[i % 2][:, 0:1]
        p_scr[i % 2] = jnp.exp2(s_scr[i % 2] - m).astype(BF16)
        if has_cache:
            pc_scr[i % 2] = jnp.exp2(sc_scr[i % 2] - m).astype(BF16)

    pending = {}

    def weighted_values(i):
        rows, head = items[i]
        o = _bdot(p_scr[i % 2], v_ref[:, head])
        if has_cache:
            o = o + _bdot(pc_scr[i % 2], vc_ref[:, head])
        o = o / o[:, DEN_LANE:DEN_LANE + 1]
        h = i % hp
        if h % 2 == 0:
            pending[0] = o
        else:
            _store_head_pair(o_ref, rows, h // 2, pending.pop(0), o)

    scores(0)
    for i in range(n):
        if i + 1 < n:
            scores(i + 1)
        if i > 0:
            weighted_values(i - 1)
        probs(i)
    weighted_values(n - 1)


def _attn_short_kernel(hp, q_ref, k_ref, v_ref, o_ref):
    tn = (((0,), (0,)), ((), ()))
    heads = [slice(h * HEAD_PAD, (h + 1) * HEAD_PAD) for h in range(hp)]
    scores = [lax.dot_general(q_ref[hd, :], k_ref[hd, :], tn, preferred_element_type=F32) for hd in heads]
    probs = [jnp.exp2(s - jnp.max(s, axis=-1, keepdims=True)).astype(BF16) for s in scores]
    outs = []
    for hd, p in zip(heads, probs):
        o = _bdot(p, v_ref[:, hd])
        outs.append(o / o[:, DEN_LANE:DEN_LANE + 1])
    for j in range(hp // 2):
        _store_head_pair(o_ref, slice(None), j, outs[2 * j], outs[2 * j + 1])


def _attn(q, k, v, cache, hp, tq):
    s, l, _ = v.shape
    w = hp * HEAD_PAD
    qspec = pl.BlockSpec((w, tq), lambda i, h, r: (h, i * (l // tq) + r))
    ospec = pl.BlockSpec((None, tq, hp * V_DIM), lambda i, h, r: (i, r, h))
    if cache is None and tq == l:
        kspec = pl.BlockSpec((w, l), lambda i, h, r: (h, i))
        vspec = pl.BlockSpec((None, l, w), lambda i, h, r: (i, 0, h))
        return pl.pallas_call(
            functools.partial(_attn_short_kernel, hp),
            grid=(s, MLA_HEADS // hp, 1),
            in_specs=[qspec, kspec, vspec],
            out_specs=ospec,
            out_shape=jax.ShapeDtypeStruct((s, l, MLA_OUT), BF16),
            compiler_params=_cparams(("arbitrary", "arbitrary", "arbitrary")),
            name="attn_short",
        )(q, k, v)
    kv_mode = dict(pipeline_mode=pl.Buffered(1))
    kspec = pl.BlockSpec((w, l), lambda i, h, r: (h, i), **kv_mode)
    vspec = pl.BlockSpec((None, l, w), lambda i, h, r: (i, 0, h), **kv_mode)
    in_specs = [qspec, kspec, vspec]
    args = [q, k, v]
    if cache is not None:
        lc = cache[1].shape[1]
        in_specs += [pl.BlockSpec((w, lc), lambda i, h, r: (h, i), **kv_mode),
                     pl.BlockSpec((None, lc, w), lambda i, h, r: (i, 0, h), **kv_mode)]
        args += list(cache)
    sub = min(tq, ATTN_SUB)
    scratch = [pltpu.VMEM((2, sub, l), F32), pltpu.VMEM((2, sub, l), BF16), pltpu.VMEM((2, sub, LANE), F32)]
    if cache is not None:
        scratch += [pltpu.VMEM((2, sub, lc), F32), pltpu.VMEM((2, sub, lc), BF16)]
    return pl.pallas_call(
        functools.partial(_attn_kernel, hp, sub, cache is not None),
        grid=(s, MLA_HEADS // hp, l // tq),
        in_specs=in_specs,
        out_specs=ospec,
        out_shape=jax.ShapeDtypeStruct((s, l, MLA_OUT), BF16),
        scratch_shapes=scratch,
        compiler_params=_cparams(("arbitrary", "arbitrary", "arbitrary")),
        name="attn",
    )(*args)


def _route(lt):
    row = lax.broadcasted_iota(jnp.int32, (SUB, lt.shape[1]), 0).astype(F32)
    neg = -jnp.inf
    gl = lt[0:SUB]
    g_ok = row < N_GROUPS
    glm = jnp.where(g_ok, gl, neg)
    gmax = jnp.max(glm, axis=0, keepdims=True)
    gsel = jnp.min(jnp.where(glm == gmax, row, float(SUB)), axis=0, keepdims=True)
    g_w = 1.0 / jnp.sum(jnp.where(g_ok, jnp.exp(gl - gmax), 0.0), axis=0, keepdims=True)
    es = lt[SUB:2 * SUB]
    for g in range(1, N_GROUPS):
        es = jnp.where(gsel == float(g), lt[(g + 1) * SUB:(g + 2) * SUB], es)
    m1 = jnp.max(es, axis=0, keepdims=True)
    i1 = jnp.min(jnp.where(es == m1, row, float(SUB)), axis=0, keepdims=True)
    es2 = jnp.where(row == i1, neg, es)
    m2 = jnp.max(es2, axis=0, keepdims=True)
    i2 = jnp.min(jnp.where(es2 == m2, row, float(SUB)), axis=0, keepdims=True)
    e2 = jnp.exp(m2 - m1)
    inv = g_w / (1.0 + e2)
    base = gsel * EXP_PER_GROUP
    return (jnp.where(row == RT_ID0, base + i1, 0.0) + jnp.where(row == RT_ID1, base + i2, 0.0)
            + jnp.where(row == RT_W0, inv, 0.0) + jnp.where(row == RT_W1, inv * e2, 0.0))


def _pack_bf16_pairs(x):
    k = x.shape[1] // 2
    bits = lax.bitcast_convert_type(x.astype(BF16).astype(F32), jnp.int32)
    return lax.shift_right_logical(bits[:, :k], 16) | (bits[:, k:] & HI_MASK)


def _unpack_bf16_pairs(w):
    lo = lax.bitcast_convert_type(lax.shift_left(w, 16), F32).astype(BF16)
    hi = lax.bitcast_convert_type(w & HI_MASK, F32).astype(BF16)
    return lo, hi


def _back_kernel(x_ref, mod_ref, g1_ref, g2_ref, yf_ref, yl_ref, ys_ref, o_ref,
                 wg_ref, wpf_ref, wpl_ref, wps_ref, wpm_ref, wo_ref, wrh_ref, wrl_ref, br_ref,
                 x1_ref, h2_ref, comb_ref):
    d = D_MODEL
    x = x_ref[...]
    sh1, sc1, gt1 = mod_ref[:, 0:d], mod_ref[:, d:2 * d], mod_ref[:, 2 * d:3 * d]
    sh2, sc2 = mod_ref[:, 3 * d:4 * d], mod_ref[:, 4 * d:5 * d]
    hb = (_rms(x, g1_ref[...]) * (1.0 + sc1) + sh1).astype(BF16)
    branches = ((yf_ref, wpf_ref), (yl_ref, wpl_ref), (ys_ref, wps_ref), (o_ref, wpm_ref))
    blocks = []
    for n in range(0, d, MERGE_BLK):
        acc = None
        for j, (y_ref, w_ref) in enumerate(branches):
            gate = jax.nn.sigmoid(_bdot(hb, wg_ref[:, j * d + n:j * d + n + MERGE_BLK]))
            term = gate * _bdot(y_ref[...], w_ref[:, n:n + MERGE_BLK])
            acc = term if acc is None else acc + term
        blocks.append(acc.astype(BF16))
    x1 = x + gt1 * _bdot(jnp.concatenate(blocks, axis=1), wo_ref[...])
    x1_ref[...] = x1
    h2 = _rms(x1, g2_ref[...]) * (1.0 + sc2) + sh2
    h2b = h2.astype(BF16)
    h2_ref[...] = _pack_bf16_pairs(h2)
    h2l = (h2 - h2b.astype(F32)).astype(BF16)
    nt = (((1,), (1,)), ((), ()))
    rdot = lambda w_ref, h: lax.dot_general(w_ref[...], h, nt, preferred_element_type=F32)
    logits_t = rdot(wrh_ref, h2b) + rdot(wrh_ref, h2l) + rdot(wrl_ref, h2b) + br_ref[...]
    comb_ref[...] = _route(logits_t)


def _back(x, mod, yf, yl, ys, o, p, tm):
    t = x.shape[0]
    per_mod = t // (mod.shape[0] * tm)
    row = lambda w: pl.BlockSpec((tm, w), lambda i: (i, 0))
    full = lambda a: pl.BlockSpec(a.shape, lambda i: (0,) * a.ndim, pipeline_mode=pl.Buffered(1))
    weights = [p["w_gate"], p["w_pf"], p["w_pl"], p["w_ps"], p["w_pm"], p["w_out"], p["wr_hi"], p["wr_lo"], p["b_r"]]
    return pl.pallas_call(
        _back_kernel,
        grid=(t // tm,),
        in_specs=[
            row(D_MODEL),
            pl.BlockSpec((None, 1, 6 * D_MODEL), lambda i: (i // per_mod, 0, 0)),
            pl.BlockSpec((1, D_MODEL), lambda i: (0, 0)),
            pl.BlockSpec((1, D_MODEL), lambda i: (0, 0)),
            row(FNET_W), row(LRU_W), row(SC_W), row(MLA_OUT),
        ] + [full(w) for w in weights],
        out_specs=[row(D_MODEL), row(PACK_W), pl.BlockSpec((SUB, tm), lambda i: (0, i))],
        out_shape=[
            jax.ShapeDtypeStruct((t, D_MODEL), F32),
            jax.ShapeDtypeStruct((t, PACK_W), jnp.int32),
            jax.ShapeDtypeStruct((SUB, t), F32),
        ],
        compiler_params=_cparams(("arbitrary",)),
        name="back",
    )(x, mod, p["norm1_g"], p["norm2_g"], yf, yl, ys, o, *weights)


def _plan_kernel(tm, rt_ref, pos_ref, te_ref, nu_ref):
    ids = jnp.concatenate([rt_ref[RT_ID0:RT_ID0 + 1, :], rt_ref[RT_ID1:RT_ID1 + 1, :]], axis=1)
    n_rows = ids.shape[1]
    erow = lax.broadcasted_iota(jnp.int32, (N_EXPERTS, PLAN_BLK), 0).astype(F32)
    col = lax.broadcasted_iota(jnp.int32, (N_EXPERTS, LANE), 0)
    blocks = [(ids[:, b * PLAN_BLK:(b + 1) * PLAN_BLK] == erow).astype(F32) for b in range(n_rows // PLAN_BLK)]
    tri = (lax.broadcasted_iota(jnp.int32, (PLAN_BLK, PLAN_BLK), 0)
           <= lax.broadcasted_iota(jnp.int32, (PLAN_BLK, PLAN_BLK), 1)).astype(BF16)
    prefix = [_bdot(blk.astype(BF16), tri) for blk in blocks]
    counts = prefix[0][:, PLAN_BLK - 1:PLAN_BLK]
    for pre in prefix[1:]:
        counts = counts + pre[:, PLAN_BLK - 1:PLAN_BLK]
    tiles = jnp.floor((counts + float(tm - 1)) * (1.0 / tm))
    tile_end = jnp.broadcast_to(tiles, (N_EXPERTS, LANE))
    for s in (1, 2, 4, 8, 16):
        tile_end = tile_end + jnp.where(col >= s, pltpu.roll(tile_end, s, 0), 0.0)
    base = (tile_end[:, 0:1] - tiles) * float(tm) - 1.0
    for b, (blk, pre) in enumerate(zip(blocks, prefix)):
        pos = jnp.sum(blk * (base + pre), axis=0, keepdims=True)
        pos_ref[:, b * PLAN_BLK:(b + 1) * PLAN_BLK] = pos.astype(jnp.int32)
        base = base + pre[:, PLAN_BLK - 1:PLAN_BLK]
    k = lax.broadcasted_iota(jnp.int32, (N_EXPERTS, LANE), 1).astype(F32)
    owner = jnp.sum((k >= tile_end).astype(F32), axis=0, keepdims=True)
    te_ref[...] = jnp.minimum(owner, float(N_EXPERTS - 1)).astype(jnp.int32)
    nu_ref[...] = jnp.broadcast_to(tile_end[N_EXPERTS - 1:N_EXPERTS, :], nu_ref.shape).astype(jnp.int32)


def _dispatch_plan(rt, tm):
    n_rows = TOP_K * rt.shape[1]
    n_tiles = n_rows // tm + N_EXPERTS
    assert n_tiles <= LANE and tm & (tm - 1) == 0 and n_rows % PLAN_BLK == 0
    pos, te, nu = pl.pallas_call(
        functools.partial(_plan_kernel, tm),
        out_shape=[jax.ShapeDtypeStruct((1, n_rows), jnp.int32), jax.ShapeDtypeStruct((1, LANE), jnp.int32),
                   jax.ShapeDtypeStruct((1, LANE), jnp.int32)],
        compiler_params=_cparams(None),
        name="plan",
    )(rt)
    return pos.reshape(n_rows), te[0, :n_tiles], nu[0, :1], n_tiles


def _sc_worker_rows(n_rows):
    workers = SC_CORES * SC_SUBCORES
    per_w = n_rows // workers
    assert per_w * workers == n_rows and per_w % SC_CHUNK == 0
    return per_w


def _sc_mesh():
    return plsc.VectorSubcoreMesh(core_axis_name="c", subcore_axis_name="s",
                                  num_cores=SC_CORES, num_subcores=SC_SUBCORES)


def _sc_scatter_rows(src, pos, n_out):
    t, w = src.shape
    per_w = _sc_worker_rows(pos.shape[0])

    @functools.partial(
        pl.kernel, mesh=_sc_mesh(), out_type=jax.ShapeDtypeStruct((n_out, w), src.dtype),
        scratch_types=[pltpu.VMEM((SC_CHUNK,), jnp.int32), pltpu.VMEM((SC_CHUNK, w), src.dtype)],
        name="moe_dispatch")
    def run(src_hbm, pos_hbm, out_hbm, idx_v, rows_v):
        wid = lax.axis_index("s") * SC_CORES + lax.axis_index("c")

        @pl.loop(0, per_w // SC_CHUNK)
        def _(i):
            j0 = wid * per_w + i * SC_CHUNK
            pltpu.sync_copy(pos_hbm.at[pl.ds(j0, SC_CHUNK)], idx_v)
            pltpu.sync_copy(src_hbm.at[pl.ds(lax.rem(j0, t), SC_CHUNK)], rows_v)
            pltpu.sync_copy(rows_v, out_hbm.at[idx_v])

    return run(src, pos)


def _sc_gather_rows(src, pos):
    w = src.shape[1]
    n = pos.shape[0]
    per_w = _sc_worker_rows(n)

    @functools.partial(
        pl.kernel, mesh=_sc_mesh(), out_type=jax.ShapeDtypeStruct((n, w), src.dtype),
        scratch_types=[pltpu.VMEM((SC_CHUNK,), jnp.int32), pltpu.VMEM((SC_CHUNK, w), src.dtype)],
        name="moe_return")
    def run(src_hbm, pos_hbm, out_hbm, idx_v, rows_v):
        wid = lax.axis_index("s") * SC_CORES + lax.axis_index("c")

        @pl.loop(0, per_w // SC_CHUNK)
        def _(i):
            j0 = wid * per_w + i * SC_CHUNK
            pltpu.sync_copy(pos_hbm.at[pl.ds(j0, SC_CHUNK)], idx_v)
            pltpu.sync_copy(src_hbm.at[idx_v], rows_v)
            pltpu.sync_copy(rows_v, out_hbm.at[pl.ds(j0, SC_CHUNK)])

    return run(src, pos)


def _experts_kernel(te_ref, nu_ref, x_ref, w1_ref, w3_ref, w2_ref, y_ref):
    half = D_MODEL // 2

    @pl.when(pl.program_id(0) < nu_ref[0])
    def _():
        lo, hi = _unpack_bf16_pairs(x_ref[...])
        a = _bdot(lo, w1_ref[:half, :].astype(BF16)) + _bdot(hi, w1_ref[half:, :].astype(BF16))
        u = _bdot(lo, w3_ref[:half, :].astype(BF16)) + _bdot(hi, w3_ref[half:, :].astype(BF16))
        mid = (a * jax.nn.sigmoid(a)) * u
        y_ref[...] = _pack_bf16_pairs(_bdot(mid.astype(BF16), w2_ref[...].astype(BF16)))


def _experts(xs, tile_expert, n_used, n_tiles, w1, w3, w2, layer, tm):
    wspec = lambda a: pl.BlockSpec((None, None) + a.shape[2:], lambda i, te, nu: (layer, te[i], 0, 0))
    grid_spec = pltpu.PrefetchScalarGridSpec(
        num_scalar_prefetch=2,
        grid=(n_tiles,),
        in_specs=[pl.BlockSpec((tm, PACK_W), lambda i, te, nu: (jnp.minimum(i, nu[0] - 1), 0)),
                  wspec(w1), wspec(w3), wspec(w2)],
        out_specs=pl.BlockSpec((tm, PACK_W), lambda i, te, nu: (jnp.where(i < nu[0], i, n_tiles), 0)),
    )
    return pl.pallas_call(
        _experts_kernel,
        grid_spec=grid_spec,
        out_shape=jax.ShapeDtypeStruct((xs.shape[0] + tm, PACK_W), jnp.int32),
        compiler_params=_cparams(("arbitrary",)),
        name="experts",
    )(tile_expert, n_used, xs, w1, w3, w2)


def _combine_kernel(x1_ref, mod_ref, rt_ref, g0_ref, g1_ref, o_ref):
    rt = rt_ref[...]
    lo0, hi0 = _unpack_bf16_pairs(g0_ref[...])
    lo1, hi1 = _unpack_bf16_pairs(g1_ref[...])
    w0 = rt[:, RT_W0:RT_W0 + 1]
    w1 = rt[:, RT_W1:RT_W1 + 1]
    half = D_MODEL // 2
    gt2 = mod_ref[:, 5 * D_MODEL:6 * D_MODEL]
    o_ref[:, :half] = x1_ref[:, :half] + gt2[:, :half] * (w0 * lo0.astype(F32) + w1 * lo1.astype(F32))
    o_ref[:, half:] = x1_ref[:, half:] + gt2[:, half:] * (w0 * hi0.astype(F32) + w1 * hi1.astype(F32))


def _combine(x1, mod, rt, g, tm):
    t = x1.shape[0]
    per_mod = t // (mod.shape[0] * tm)
    return pl.pallas_call(
        _combine_kernel,
        grid=(t // tm,),
        in_specs=[
            pl.BlockSpec((tm, D_MODEL), lambda i: (i, 0)),
            pl.BlockSpec((None, 1, 6 * D_MODEL), lambda i: (i // per_mod, 0, 0)),
            pl.BlockSpec((tm, LANE), lambda i: (i, 0)),
            pl.BlockSpec((tm, PACK_W), lambda i: (i, 0)),
            pl.BlockSpec((tm, PACK_W), lambda i: (i + t // tm, 0)),
        ],
        out_specs=pl.BlockSpec((tm, D_MODEL), lambda i: (i, 0)),
        out_shape=jax.ShapeDtypeStruct((t, D_MODEL), F32),
        compiler_params=_cparams(("arbitrary",)),
        name="combine",
    )(x1, mod, rt, g, g)


def _moe(h2p, rt, x1, mod, p, tm_e, tm_c):
    pos, tile_expert, n_used, n_tiles = _dispatch_plan(rt, tm_e)
    xs = _sc_scatter_rows(h2p, pos, n_tiles * tm_e)
    ys = _experts(xs, tile_expert, n_used, n_tiles, p["w1"], p["w3"], p["w2"], p["layer"], tm_e)
    g = _sc_gather_rows(ys, pos)
    rt_cols = jnp.pad(rt.T, ((0, 0), (0, LANE - rt.shape[0])))
    return _combine(x1, mod, rt_cols, g, tm_c)


def _channel_dft():
    n = np.arange(FNET_W // FNET_GROUPS)
    ang = 2.0 * np.pi * ((n[:, None] * n[None, :]) % n.size) / n.size
    eye = np.eye(FNET_GROUPS)
    scale = 1.0 / math.sqrt(n.size)
    return (jnp.asarray(np.kron(eye, np.cos(ang) * scale), F32).astype(BF16),
            jnp.asarray(np.kron(eye, np.sin(ang) * scale), F32).astype(BF16))


def _position_dft(l):
    scale = 1.0 / math.sqrt(l)
    if l <= 256:
        n = np.arange(l)
        ang = 2.0 * np.pi * ((n[:, None] * n[None, :]) % l) / l
        return jnp.asarray(np.cos(ang) * scale, F32).astype(BF16), jnp.asarray(np.sin(ang) * scale, F32).astype(BF16)
    m = DFT_SPLIT
    n = np.arange(l)
    ang_a = 2.0 * np.pi * ((np.arange(l // m)[:, None] * m * n[None, :]) % l) / l
    ang_b = 2.0 * np.pi * ((np.arange(m)[:, None] * n[None, :]) % l) / l
    return (jnp.asarray(np.cos(ang_a), F32), jnp.asarray(np.sin(ang_a), F32),
            jnp.asarray(np.cos(ang_b) * scale, F32), jnp.asarray(np.sin(ang_b) * scale, F32))


def _rope_tables(l, rotate):
    cos = np.ones((l, LANE))
    sinr = np.zeros((l, LANE))
    if rotate:
        t = np.arange(l)
        inv = ROPE_BASE ** (-np.arange(N_FREQ) / N_FREQ)
        ang = np.concatenate([(t // GRID_W)[:, None] * inv, (t % GRID_W)[:, None] * inv], axis=-1)
        half = ROPE // 2
        for lo in (ROPE_X1, ROPE_X2):
            cos[:, lo:lo + half] = np.cos(ang)
        sinr[:, ROPE_X1:ROPE_X1 + half] = -np.sin(ang)
        sinr[:, ROPE_X2:ROPE_X2 + half] = np.sin(ang)
    return tuple(jnp.asarray(a, F32) for a in (cos, sinr))


def _head_lane_source():
    half = ROPE // 2
    src = np.full((HEAD_PAD,), QK_DIM, np.int32)
    src[ROPE_X1:ROPE_X1 + half] = NOPE + np.arange(half)
    src[ROPE_X2:ROPE_X2 + half] = NOPE + half + np.arange(half)
    free = [i for i in range(HEAD_PAD) if src[i] == QK_DIM][:NOPE]
    src[free] = np.arange(NOPE)
    return src


def _place_head_dims(w, n_src):
    src = _head_lane_source()
    src = np.where(src < n_src, src, n_src)
    wz = jnp.concatenate([w[..., :n_src], jnp.zeros(w.shape[:-1] + (1,), w.dtype)], axis=-1)
    out = jnp.take(wz, jnp.asarray(src), axis=-1)
    return out.reshape(out.shape[:-2] + (out.shape[-2] * HEAD_PAD,))


def _place_rope_key(kr):
    half = ROPE // 2
    out = jnp.zeros(kr.shape[:-1] + (LANE,), kr.dtype)
    return out.at[..., ROPE_X1:ROPE_X1 + half].set(kr[..., :half]).at[..., ROPE_X2:ROPE_X2 + half].set(kr[..., half:])


def _take_rope_key(tile):
    half = ROPE // 2
    return jnp.concatenate([tile[..., ROPE_X1:ROPE_X1 + half], tile[..., ROPE_X2:ROPE_X2 + half]], axis=-1)


def _pad_heads(w, lo, hi):
    r = w.shape[0]
    part = w[:, :, lo:hi]
    out = jnp.zeros((r, MLA_HEADS, HEAD_PAD), w.dtype).at[:, :, : hi - lo].set(part)
    return out.reshape(r, HP)


def _blockdiag_halves(w):
    bw = LRU_W // LRU_BLOCKS
    out = jnp.zeros((2, 2, LANE, LANE), w.dtype)
    for half in range(2):
        for k in range(2):
            n = 2 * half + k
            out = out.at[half, :, k * bw:(k + 1) * bw, k * bw:(k + 1) * bw].set(w[:, n])
    return out


def _halves(v):
    return jnp.moveaxis(v.reshape(v.shape[:-1] + (2, LANE)), -2, 0)


def _layer_params(l, a):
    d = D_MODEL
    w_in = a["w_in"][l]
    w_in = jnp.concatenate([w_in[:, :Z_KR], _place_rope_key(w_in[:, Z_KR:])], axis=1)
    w_r = jnp.zeros((RT_ROWS, d), F32).at[:N_GROUPS].set(a["w_gr"][l].T).at[SUB:SUB + N_EXPERTS].set(a["w_er"][l].T)
    b_r = jnp.zeros((RT_ROWS, 1), F32).at[:N_GROUPS, 0].set(a["b_gr"][l]).at[SUB:SUB + N_EXPERTS, 0].set(a["b_er"][l])
    wr_hi = w_r.astype(BF16)
    gpad = lambda g: jnp.take(jnp.concatenate([g, jnp.zeros((1,), F32)]), jnp.asarray(_head_lane_source())).reshape(1, LANE)
    return {
        "norm1_g": a["norm1_g"][l].reshape(1, d), "norm2_g": a["norm2_g"][l].reshape(1, d),
        "w_in": w_in.astype(BF16),
        "lru_conv": _halves(a["lru_conv"][l]),
        "lru_wa": _blockdiag_halves(a["lru_wa"][l]).astype(BF16),
        "lru_wx": _blockdiag_halves(a["lru_wx"][l]).astype(BF16),
        "lru_ba": _halves(a["lru_ba"][l])[:, :, None, :], "lru_bx": _halves(a["lru_bx"][l])[:, :, None, :],
        "lru_lam": _halves(a["lru_lam"][l])[:, :, None, :],
        "sc_conv": _halves(a["sc_conv"][l]),
        "g_qa": a["g_qa"][l].reshape(1, Q_LORA), "g_kva": a["g_kva"][l].reshape(1, KV_LORA),
        "wq_t": _place_head_dims(a["w_qb"][l], QK_DIM).T.astype(BF16),
        "wk_t": _place_head_dims(a["w_kvb"][l], NOPE).T.astype(BF16),
        "wv": _pad_heads(a["w_kvb"][l], NOPE, NOPE + V_DIM).astype(BF16),
        "gq_col": gpad(a["g_qn"][l]).reshape(LANE, 1), "gk_col": gpad(a["g_kn"][l]).reshape(LANE, 1),
        "eye": jnp.eye(LANE, dtype=F32).astype(BF16),
        "w_gate": a["w_gate"][l].astype(BF16),
        "w_pf": a["w_pf"][l].astype(BF16), "w_pl": a["w_pl"][l].astype(BF16), "w_ps": a["w_ps"][l].astype(BF16),
        "w_pm": a["w_pm"][l].astype(BF16), "w_out": a["w_out"][l].astype(BF16),
        "wr_hi": wr_hi, "wr_lo": (w_r - wr_hi.astype(F32)).astype(BF16), "b_r": b_r,
        "w1": a["w1"], "w3": a["w3"], "w2": a["w2"], "layer": l,
    }


def _trunk_layer(x, mod, p, consts, h0, cache, cfg):
    s, l = cfg["s"], cfg["l"]
    z = _front(x, mod, p["norm1_g"], p["w_in"], cfg["tm_front"])
    z3 = z.reshape(s, l, Z_W)
    q, k, v, ckv = _qkv(z, consts["rope"], p, l, cfg["tm_qkv"])
    yf = _fourier(z3, consts["dft"], cfg["nb"], cfg["tq_f"])
    yl, ys, fin = _lru(z3, h0, p)
    o = _attn(q, k, v.reshape(s, l, HP), cache, cfg["hp"], cfg["tq_a"])
    t = s * l
    x1, h2, comb = _back(x, mod, yf.reshape(t, FNET_W), yl.reshape(t, LRU_W), ys.reshape(t, SC_W),
                         o.reshape(t, MLA_OUT), p, cfg["tm_back"])
    x2 = _moe(h2, comb, x1, mod, p, cfg["tm_moe"], cfg["tm_comb"])
    return x2, z3, ckv, fin


def kernel(x_prompt, x_sample, cache_ckv, cache_krope, state_rglru, c, c_ctx, norm1_g, norm2_g, w_ada, b_ada, w_in, lru_conv, lru_wa, lru_ba, lru_wx, lru_bx, lru_lam, sc_conv, g_qa, w_qb, g_kva, w_kvb, g_qn, g_kn, w_pf, w_pl, w_ps, w_pm, w_gate, w_out, w_gr, b_gr, w_er, b_er, w1, w3, w2):
    a = dict(norm1_g=norm1_g, norm2_g=norm2_g, w_in=w_in, lru_conv=lru_conv, lru_wa=lru_wa, lru_ba=lru_ba,
             lru_wx=lru_wx, lru_bx=lru_bx, lru_lam=lru_lam, sc_conv=sc_conv, g_qa=g_qa, w_qb=w_qb, g_kva=g_kva,
             w_kvb=w_kvb, g_qn=g_qn, g_kn=g_kn, w_pf=w_pf, w_pl=w_pl, w_ps=w_ps, w_pm=w_pm, w_gate=w_gate,
             w_out=w_out, w_gr=w_gr, b_gr=b_gr, w_er=w_er, b_er=b_er, w1=w1, w3=w3, w2=w2)
    bc, lc, d = x_prompt.shape
    bl, ll, _ = x_sample.shape
    past = cache_ckv.shape[2]

    cond8 = jnp.zeros((SUB, d), F32).at[0].set(c_ctx).at[1:1 + bl].set(c)
    mod_all = _ada(cond8, w_ada, b_ada)

    cc, sc = _channel_dft()
    ctx_consts = {"dft": (cc, sc) + _position_dft(lc), "rope": _rope_tables(lc, False)}
    lat_consts = {"dft": (cc, sc) + _position_dft(ll), "rope": _rope_tables(ll, True)}
    cache_tabs = _rope_tables(bl * past, False)
    ctx_cfg = dict(s=bc, l=lc, tm_front=512, tm_qkv=512, nb=8, tq_f=lc, hp=MLA_HEADS, tq_a=lc, tm_back=1024, tm_moe=512, tm_comb=512)
    lat_cfg = dict(s=bl, l=ll, tm_front=512, tm_qkv=512, nb=bl, tq_f=512, hp=MLA_HEADS, tq_a=256, tm_back=1024, tm_moe=512, tm_comb=512)

    xp = x_prompt.reshape(bc * lc, d)
    xs = x_sample.reshape(bl * ll, d)
    h0_ctx = jnp.zeros((bc, 2, LRU_W), F32)
    ckv_list, kr_list, lru_list = [], [], []
    for l in range(DEPTH):
        p = _layer_params(l, a)
        mod_ctx = mod_all[l, 0:1].reshape(1, 1, 6 * d)
        mod_lat = mod_all[l, 1:1 + bl].reshape(bl, 1, 6 * d)
        xp, z3, ckv, fin = _trunk_layer(xp, mod_ctx, p, ctx_consts, h0_ctx, None, ctx_cfg)
        ckv_list.append(ckv.reshape(bc, lc, KV_LORA))
        kr_list.append(_take_rope_key(z3[:, :, Z_KR:]))
        lru_list.append(fin)
        kr_tile = _place_rope_key(cache_krope[:, l].reshape(bl * past, ROPE))
        kc, vc = _kvcache(cache_ckv[:, l].reshape(bl * past, KV_LORA), kr_tile, cache_tabs, p)
        cache = (kc, vc.reshape(bl, past, HP))
        xs, _, _, _ = _trunk_layer(xs, mod_lat, p, lat_consts, state_rglru[:, l], cache, lat_cfg)
    return (xp.reshape(bc, lc, d), xs.reshape(bl, ll, d),
            jnp.stack(ckv_list, axis=1), jnp.stack(kr_list, axis=1), jnp.stack(lru_list, axis=1))
```

```python
import functools
import math

import numpy as np
import jax
import jax.numpy as jnp
from jax import lax
from jax.experimental import pallas as pl
from jax.experimental.pallas import tpu as pltpu
from jax.experimental.pallas import tpu_sc as plsc

F32 = jnp.float32
BF16 = jnp.bfloat16

D_MODEL = 1024
DEPTH = 2
GRID_W = 64
EPS = 1e-6
FNET_W = 256
FNET_GROUPS = 4
LRU_W = 256
LRU_BLOCKS = 4
LRU_C = 8.0
SC_W = 256
MLA_HEADS = 8
Q_LORA = 384
KV_LORA = 256
NOPE = 64
ROPE = 32
V_DIM = 64
QK_DIM = NOPE + ROPE
MLA_OUT = MLA_HEADS * V_DIM
N_FREQ = ROPE // 4
ROPE_BASE = 10000.0
ATTN_SCALE = QK_DIM ** -0.5
LOG2E = math.log2(math.e)
N_GROUPS = 4
EXP_PER_GROUP = 8
N_EXPERTS = N_GROUPS * EXP_PER_GROUP
D_EXPERT = 256

LANE = 128
SUB = 8
HEAD_PAD = LANE
HP = MLA_HEADS * HEAD_PAD
ROPE_X1 = 0
ROPE_X2 = LANE // 2
DEN_LANE = V_DIM
Z_QKV = 6 * 256
Z_KR = Z_QKV + Q_LORA + KV_LORA
Z_W = Z_KR + LANE
QKV_W = Q_LORA + KV_LORA + LANE
VMEM_LIMIT = 52 * 1024 * 1024
PACK_W = D_MODEL // 2
HI_MASK = -65536
RT_ID0, RT_ID1, RT_W0, RT_W1 = 0, 1, 2, 3
RT_ROWS = 48
TOP_K = 2
MERGE_BLK = 256
PLAN_BLK = 256
DFT_SPLIT = 64
ATTN_SUB = 256
SC_CORES = 2
SC_SUBCORES = 16
SC_CHUNK = 128


def _cparams(sem):
    return pltpu.CompilerParams(dimension_semantics=sem, vmem_limit_bytes=VMEM_LIMIT)


def _bdot(a, b):
    return jnp.dot(a, b, preferred_element_type=F32)


def _rms(x, g):
    return x * lax.rsqrt(jnp.mean(x * x, axis=-1, keepdims=True) + EPS) * g


def _ada_kernel(c_ref, w_ref, b_ref, o_ref):
    c = c_ref[...]
    s = (c * jax.nn.sigmoid(c)).astype(BF16)
    o_ref[...] = _bdot(s, w_ref[...].astype(BF16)) + b_ref[...]


def _ada(cond8, w_ada, b_ada):
    nblk = 6 * D_MODEL // 1024
    return pl.pallas_call(
        _ada_kernel,
        grid=(DEPTH, nblk),
        in_specs=[
            pl.BlockSpec((SUB, D_MODEL), lambda l, n: (0, 0)),
            pl.BlockSpec((None, D_MODEL, 1024), lambda l, n: (l, 0, n)),
            pl.BlockSpec((None, 1, 1024), lambda l, n: (l, 0, n)),
        ],
        out_specs=pl.BlockSpec((None, SUB, 1024), lambda l, n: (l, 0, n)),
        out_shape=jax.ShapeDtypeStruct((DEPTH, SUB, 6 * D_MODEL), F32),
        compiler_params=_cparams(("arbitrary", "arbitrary")),
        name="ada",
    )(cond8, w_ada, b_ada.reshape(DEPTH, 1, 6 * D_MODEL))


def _front_kernel(x_ref, mod_ref, g_ref, w_ref, z_ref):
    x = x_ref[...]
    sh = mod_ref[:, 0:D_MODEL]
    sc = mod_ref[:, D_MODEL:2 * D_MODEL]
    h = _rms(x, g_ref[...]) * (1.0 + sc) + sh
    z_ref[...] = _bdot(h.astype(BF16), w_ref[...])


def _front(x, mod, g1, w_in, tm):
    t = x.shape[0]
    per_mod = t // (mod.shape[0] * tm)
    return pl.pallas_call(
        _front_kernel,
        grid=(t // tm,),
        in_specs=[
            pl.BlockSpec((tm, D_MODEL), lambda i: (i, 0)),
            pl.BlockSpec((None, 1, 6 * D_MODEL), lambda i: (i // per_mod, 0, 0)),
            pl.BlockSpec((1, D_MODEL), lambda i: (0, 0)),
            pl.BlockSpec((D_MODEL, Z_W), lambda i: (0, 0)),
        ],
        out_specs=pl.BlockSpec((tm, Z_W), lambda i: (i, 0)),
        out_shape=jax.ShapeDtypeStruct((t, Z_W), F32),
        compiler_params=_cparams(("arbitrary",)),
        name="front",
    )(x, mod, g1, w_in)


def _rope(x, cos, sinr):
    return x * cos + pltpu.roll(x, LANE // 2, 1) * sinr


def _head_norm(x, g):
    ss = jnp.sum(x * x, axis=-1, keepdims=True) * (1.0 / QK_DIM)
    return x * lax.rsqrt(ss + EPS) * g


def _head_norm_rope_t(x_t, g_col, cos_t, sinr_t):
    ss = jnp.sum(x_t * x_t, axis=0, keepdims=True) * (1.0 / QK_DIM)
    x_t = x_t * lax.rsqrt(ss + EPS) * g_col
    return (x_t * cos_t + pltpu.roll(x_t, HEAD_PAD // 2, 0) * sinr_t).astype(BF16)


def _build_kv(ckv, kr_tile, cos_t, sinr_t, wk_t_ref, wv_ref, gk_col, eye_ref, kt_ref, v_ref):
    nt = (((1,), (1,)), ((), ()))
    cb = ckv.astype(BF16)
    kn_t = lax.dot_general(wk_t_ref[...], cb, nt, preferred_element_type=F32)
    kr_hi = kr_tile.astype(BF16)
    kr_lo = (kr_tile - kr_hi.astype(F32)).astype(BF16)
    kr_t = (lax.dot_general(eye_ref[...], kr_hi, nt, preferred_element_type=F32)
            + lax.dot_general(eye_ref[...], kr_lo, nt, preferred_element_type=F32))
    for h in range(MLA_HEADS):
        sl = slice(h * HEAD_PAD, (h + 1) * HEAD_PAD)
        kt_ref[sl, :] = _head_norm_rope_t(kn_t[sl, :] + kr_t, gk_col, cos_t, sinr_t)
    v = _bdot(cb, wv_ref[...])
    lane = lax.broadcasted_iota(jnp.int32, v.shape, 1)
    v_ref[...] = jnp.where((lane & (HEAD_PAD - 1)) == DEN_LANE, 1.0, v).astype(BF16)


def _qkv_kernel(z_ref, cos_t_ref, sinr_t_ref, gqa_ref, wq_ref, gkva_ref, wk_ref, wv_ref,
                gq_ref, gk_ref, eye_ref, q_ref, k_ref, v_ref, ckv_ref):
    cos_t, sinr_t = cos_t_ref[...], sinr_t_ref[...]
    q_c = z_ref[:, 0:Q_LORA]
    kv_c = z_ref[:, Q_LORA:Q_LORA + KV_LORA]
    kr_tile = z_ref[:, Q_LORA + KV_LORA:QKV_W]
    nt = (((1,), (1,)), ((), ()))
    qf_t = lax.dot_general(wq_ref[...], _rms(q_c, gqa_ref[...]).astype(BF16), nt, preferred_element_type=F32)
    gq_col = gq_ref[...] * (ATTN_SCALE * LOG2E)
    for h in range(MLA_HEADS):
        sl = slice(h * HEAD_PAD, (h + 1) * HEAD_PAD)
        q_ref[sl, :] = _head_norm_rope_t(qf_t[sl, :], gq_col, cos_t, sinr_t)
    ckv = _rms(kv_c, gkva_ref[...])
    ckv_ref[...] = ckv
    _build_kv(ckv, kr_tile, cos_t, sinr_t, wk_ref, wv_ref, gk_ref[...], eye_ref, k_ref, v_ref)


def _qkv(z, tabs, p, seq_len, tm):
    t = z.shape[0]
    if tm > seq_len:
        tabs = [jnp.tile(a, (tm // seq_len, 1)) for a in tabs]
    tabs_t = [a.T for a in tabs]
    per_seq = max(seq_len // tm, 1)
    tab_t_spec = pl.BlockSpec((LANE, tm), lambda i: (0, i % per_seq))
    full = lambda shape: pl.BlockSpec(shape, lambda i: (0,) * len(shape))
    return pl.pallas_call(
        _qkv_kernel,
        grid=(t // tm,),
        in_specs=[
            pl.BlockSpec((tm, QKV_W), lambda i: (i, Z_QKV // QKV_W)),
            tab_t_spec, tab_t_spec,
            full((1, Q_LORA)), full((HP, Q_LORA)), full((1, KV_LORA)),
            full((HP, KV_LORA)), full((KV_LORA, HP)), full((LANE, 1)), full((LANE, 1)), full((LANE, LANE)),
        ],
        out_specs=[
            pl.BlockSpec((HP, tm), lambda i: (0, i)),
            pl.BlockSpec((HP, tm), lambda i: (0, i)),
            pl.BlockSpec((tm, HP), lambda i: (i, 0)),
            pl.BlockSpec((tm, KV_LORA), lambda i: (i, 0)),
        ],
        out_shape=[
            jax.ShapeDtypeStruct((HP, t), BF16),
            jax.ShapeDtypeStruct((HP, t), BF16),
            jax.ShapeDtypeStruct((t, HP), BF16),
            jax.ShapeDtypeStruct((t, KV_LORA), F32),
        ],
        compiler_params=_cparams(("arbitrary",)),
        name="qkv",
    )(z, tabs_t[0], tabs_t[1], p["g_qa"], p["wq_t"], p["g_kva"], p["wk_t"], p["wv"],
      p["gq_col"], p["gk_col"], p["eye"])


def _kvcache_kernel(ckv_ref, kr_ref, cos_t_ref, sinr_t_ref, wk_ref, wv_ref, gk_ref, eye_ref, k_ref, v_ref):
    _build_kv(ckv_ref[...], kr_ref[...], cos_t_ref[...], sinr_t_ref[...], wk_ref, wv_ref, gk_ref[...], eye_ref,
              k_ref, v_ref)


def _kvcache(ckv, kr_tile, tabs, p):
    t = ckv.shape[0]
    return pl.pallas_call(
        _kvcache_kernel,
        out_shape=[jax.ShapeDtypeStruct((HP, t), BF16), jax.ShapeDtypeStruct((t, HP), BF16)],
        compiler_params=_cparams(None),
        name="kvcache",
    )(ckv, kr_tile, tabs[0].T, tabs[1].T, p["wk_t"], p["wv"], p["gk_col"], p["eye"])


def _fourier_kernel(nb, u_ref, cc_ref, sc_ref, cl_ref, sl_ref, o_ref, a_scr, b_scr):
    @pl.when(pl.program_id(1) == 0)
    def _():
        for b in range(nb):
            u = u_ref[b].astype(BF16)
            a_scr[:, b * FNET_W:(b + 1) * FNET_W] = _bdot(u, cc_ref[...]).astype(BF16)
            b_scr[:, b * FNET_W:(b + 1) * FNET_W] = _bdot(u, sc_ref[...]).astype(BF16)

    y = _bdot(cl_ref[...], a_scr[...]) - _bdot(sl_ref[...], b_scr[...])
    for b in range(nb):
        o_ref[b] = y[:, b * FNET_W:(b + 1) * FNET_W].astype(BF16)


def _fourier_split_kernel(nb, u_ref, cc_ref, sc_ref, ca_ref, sa_ref, cb_ref, sb_ref, o_ref,
                          a_scr, b_scr, cl_scr, sl_scr):
    @pl.when(pl.program_id(1) == 0)
    def _():
        for b in range(nb):
            u = u_ref[b].astype(BF16)
            a_scr[:, b * FNET_W:(b + 1) * FNET_W] = _bdot(u, cc_ref[...]).astype(BF16)
            b_scr[:, b * FNET_W:(b + 1) * FNET_W] = _bdot(u, sc_ref[...]).astype(BF16)

    cb, sb = cb_ref[...], sb_ref[...]
    for j in range(ca_ref.shape[0]):
        ca, sa = ca_ref[j:j + 1, :], sa_ref[j:j + 1, :]
        cl_scr[j * DFT_SPLIT:(j + 1) * DFT_SPLIT, :] = (ca * cb - sa * sb).astype(BF16)
        sl_scr[j * DFT_SPLIT:(j + 1) * DFT_SPLIT, :] = (sa * cb + ca * sb).astype(BF16)
    y = _bdot(cl_scr[...], a_scr[...]) - _bdot(sl_scr[...], b_scr[...])
    for b in range(nb):
        o_ref[b] = y[:, b * FNET_W:(b + 1) * FNET_W].astype(BF16)


def _fourier_split(z3, mats, nb, tq):
    s, l, _ = z3.shape
    cc, sc, ca, sa, cb, sb = mats
    full = lambda a: pl.BlockSpec(a.shape, lambda i, r: (0, 0))
    rows = pl.BlockSpec((tq // DFT_SPLIT, l), lambda i, r: (r, 0))
    return pl.pallas_call(
        functools.partial(_fourier_split_kernel, nb),
        grid=(s // nb, l // tq),
        in_specs=[pl.BlockSpec((nb, l, FNET_W), lambda i, r: (i, 0, 0)), full(cc), full(sc), rows, rows,
                  full(cb), full(sb)],
        out_specs=pl.BlockSpec((nb, tq, FNET_W), lambda i, r: (i, r, 0)),
        out_shape=jax.ShapeDtypeStruct((s, l, FNET_W), BF16),
        scratch_shapes=[pltpu.VMEM((l, nb * FNET_W), BF16), pltpu.VMEM((l, nb * FNET_W), BF16),
                        pltpu.VMEM((tq, l), BF16), pltpu.VMEM((tq, l), BF16)],
        compiler_params=_cparams(("arbitrary", "arbitrary")),
        name="fourier_split",
    )(z3, cc, sc, ca, sa, cb, sb)


def _fourier(z3, mats, nb, tq):
    s, l, _ = z3.shape
    if len(mats) == 6:
        return _fourier_split(z3, mats, nb, tq)
    cc, sc, cl, sl = mats
    return pl.pallas_call(
        functools.partial(_fourier_kernel, nb),
        grid=(s // nb, l // tq),
        in_specs=[
            pl.BlockSpec((nb, l, FNET_W), lambda i, r: (i, 0, 0)),
            pl.BlockSpec((FNET_W, FNET_W), lambda i, r: (0, 0)),
            pl.BlockSpec((FNET_W, FNET_W), lambda i, r: (0, 0)),
            pl.BlockSpec((tq, l), lambda i, r: (r, 0)),
            pl.BlockSpec((tq, l), lambda i, r: (r, 0)),
        ],
        out_specs=pl.BlockSpec((nb, tq, FNET_W), lambda i, r: (i, r, 0)),
        out_shape=jax.ShapeDtypeStruct((s, l, FNET_W), BF16),
        scratch_shapes=[pltpu.VMEM((l, nb * FNET_W), BF16), pltpu.VMEM((l, nb * FNET_W), BF16)],
        compiler_params=_cparams(("arbitrary", "arbitrary")),
        name="fourier",
    )(z3, cc, sc, cl, sl)


def _load_ext(ref, c, rows, nchunks):
    t0 = pl.multiple_of(c * rows, rows)
    main = ref[pl.ds(t0, rows), :]
    lo = pl.multiple_of(jnp.maximum(t0 - SUB, 0), SUB)
    hi = pl.multiple_of(jnp.minimum(t0 + rows, (nchunks - 1) * rows + rows - SUB), SUB)
    prev = jnp.where(c > 0, ref[pl.ds(lo, SUB), :], 0.0)
    nxt = jnp.where(c < nchunks - 1, ref[pl.ds(hi, SUB), :], 0.0)
    return jnp.concatenate([prev, main, nxt], axis=0)


def _shifted(ext, off, rows):
    n = ext.shape[0]
    r = ext if off == 0 else pltpu.roll(ext, (-off) % n, 0)
    return r[SUB:SUB + rows]


def _gelu_tanh(x):
    return 0.5 * x * (1.0 + jnp.tanh(math.sqrt(2.0 / math.pi) * (x + 0.044715 * (x * x * x))))


def _lru_gates(xc, d, wa_ref, wx_ref, ba_ref, bx_ref, lam_ref):
    xb = xc.astype(BF16)
    r = jax.nn.sigmoid(_bdot(xb, wa_ref[d]) + ba_ref[d])
    i = jax.nn.sigmoid(_bdot(xb, wx_ref[d]) + bx_ref[d])
    nl = -lam_ref[d]
    softplus = jnp.maximum(nl, 0.0) + jnp.log1p(jnp.exp(-jnp.abs(nl)))
    la = (-LRU_C) * r * softplus
    a = jnp.exp(la)
    one_m_a2 = -jnp.tanh(la) * (a * a + 1.0)
    b = jnp.sqrt(one_m_a2) * (i * xc)
    return a, b


def _group_scan(a, b, reverse):
    rows = a.shape[0]
    rm = lax.broadcasted_iota(jnp.int32, a.shape, 0) & (SUB - 1)
    for s in (1, 2, 4):
        if reverse:
            sh, m = rows - s, rm + s <= SUB - 1
        else:
            sh, m = s, rm >= s
        a_sh = pltpu.roll(a, sh, 0)
        b_sh = pltpu.roll(b, sh, 0)
        b = jnp.where(m, a * b_sh + b, b)
        a = jnp.where(m, a * a_sh, a)
    return a, b


def _carry_scan(a, b, carry, reverse):
    ng = a.shape[0] // SUB
    out = [None] * ng
    order = range(ng - 1, -1, -1) if reverse else range(ng)
    for g in order:
        hg = a[g * SUB:(g + 1) * SUB] * carry + b[g * SUB:(g + 1) * SUB]
        out[g] = hg
        carry = hg[0:1] if reverse else hg[SUB - 1:SUB]
    return jnp.concatenate(out, axis=0), carry


def _lru_kernel(nb, rows, nchunks, xl_ref, gl_ref, bs_ref, cs_ref, xs_ref, h0_ref, cw_ref, wa_ref, wx_ref,
                ba_ref, bx_ref, lam_ref, sw_ref, yl_ref, ys_ref, fin_ref, xc_scr, hf_scr, hb_scr):
    cw = cw_ref[...]
    sw = sw_ref[...]
    chunk = lambda c: pl.ds(pl.multiple_of(c * rows, rows), rows)
    seqs = range(nb)

    def conv_body(c, carry):
        for s in seqs:
            ext = _load_ext(xl_ref.at[s], c, rows, nchunks)
            xc = _shifted(ext, -2, rows) * cw[0:1]
            for k in range(1, 4):
                xc = xc + _shifted(ext, k - 2, rows) * cw[k:k + 1]
            xc_scr[s, chunk(c), :] = xc
        return carry

    lax.fori_loop(0, nchunks, conv_body, 0)

    def direction(s, c, d, carry, h_scr):
        a, b = _lru_gates(xc_scr[s, chunk(c), :], d, wa_ref, wx_ref, ba_ref, bx_ref, lam_ref)
        a, b = _group_scan(a, b, d == 1)
        h, carry = _carry_scan(a, b, carry, d == 1)
        h_scr[s, chunk(c), :] = h
        return carry

    def scan_body(j, carries):
        return tuple((direction(s, j, 0, cf, hf_scr), direction(s, nchunks - 1 - j, 1, cb, hb_scr))
                     for s, (cf, cb) in zip(seqs, carries))

    init = tuple((h0_ref[s, 0:1, :], h0_ref[s, 1:2, :]) for s in seqs)
    for s, (cf, cb) in zip(seqs, lax.fori_loop(0, nchunks, scan_body, init)):
        fin_ref[s, 0:1, :] = cf
        fin_ref[s, 1:2, :] = cb

    def out_body(c, carry):
        for s in seqs:
            y = (hf_scr[s, chunk(c), :] + hb_scr[s, chunk(c), :]) * _gelu_tanh(gl_ref[s, chunk(c), :])
            yl_ref[s, chunk(c), :] = y.astype(BF16)
            ext = _load_ext(cs_ref.at[s], c, rows, nchunks) * _load_ext(xs_ref.at[s], c, rows, nchunks)
            acc = _shifted(ext, -1, rows) * sw[0:1]
            acc = acc + _shifted(ext, 0, rows) * sw[1:2]
            acc = acc + _shifted(ext, 1, rows) * sw[2:3]
            ys_ref[s, chunk(c), :] = (bs_ref[s, chunk(c), :] * acc).astype(BF16)
        return carry

    lax.fori_loop(0, nchunks, out_body, 0)


def _lru(z3, h0, p, nb):
    s, l, _ = z3.shape
    rows = min(l, 1024)
    nchunks = l // rows
    half = lambda blk: pl.BlockSpec((nb, l, LANE), lambda i, j: (i, 0, blk + j))
    wspec = lambda shape: pl.BlockSpec((None,) + shape, lambda i, j: (j,) + (0,) * len(shape))
    seq_half = pl.BlockSpec((nb, l, LANE), lambda i, j: (i, 0, j))
    state = pl.BlockSpec((nb, 2, LANE), lambda i, j: (i, 0, j))
    return pl.pallas_call(
        functools.partial(_lru_kernel, nb, rows, nchunks),
        grid=(s // nb, 2),
        in_specs=[
            half(2), half(4), half(6), half(8), half(10), state,
            wspec((4, LANE)), wspec((2, LANE, LANE)), wspec((2, LANE, LANE)),
            wspec((2, 1, LANE)), wspec((2, 1, LANE)), wspec((2, 1, LANE)), wspec((3, LANE)),
        ],
        out_specs=[seq_half, seq_half, state],
        out_shape=[jax.ShapeDtypeStruct((s, l, LRU_W), BF16), jax.ShapeDtypeStruct((s, l, SC_W), BF16),
                   jax.ShapeDtypeStruct((s, 2, LRU_W), F32)],
        scratch_shapes=[pltpu.VMEM((nb, l, LANE), F32)] * 3,
        compiler_params=_cparams(("arbitrary", "arbitrary")),
        name="lru",
    )(z3, z3, z3, z3, z3, h0, p["lru_conv"], p["lru_wa"], p["lru_wx"], p["lru_ba"], p["lru_bx"], p["lru_lam"],
      p["sc_conv"])


def _store_head_pair(o_ref, rows, j, o_even, o_odd):
    lane = lax.broadcasted_iota(jnp.int32, o_even.shape, 1)
    pair = jnp.where(lane < V_DIM, o_even, pltpu.roll(o_odd, V_DIM, 1))
    o_ref[rows, j * LANE:(j + 1) * LANE] = pair.astype(BF16)


def _attn_kernel(hp, sub, has_cache, *refs):
    if has_cache:
        q_ref, k_ref, v_ref, kc_ref, vc_ref, o_ref, s_scr, p_scr, m_scr, sc_scr, pc_scr = refs
    else:
        q_ref, k_ref, v_ref, o_ref, s_scr, p_scr, m_scr = refs
    tn = (((0,), (0,)), ((), ()))
    items = [(slice(r * sub, (r + 1) * sub), slice(h * HEAD_PAD, (h + 1) * HEAD_PAD))
             for r in range(q_ref.shape[1] // sub) for h in range(hp)]
    n = len(items)

    def scores(i):
        rows, head = items[i]
        q_t = q_ref[head, rows]
        s = lax.dot_general(q_t, k_ref[head, :], tn, preferred_element_type=F32)
        m = jnp.max(s, axis=-1, keepdims=True)
        s_scr[i % 2] = s
        if has_cache:
            sc = lax.dot_general(q_t, kc_ref[head, :], tn, preferred_element_type=F32)
            m = jnp.maximum(m, jnp.max(sc, axis=-1, keepdims=True))
            sc_scr[i % 2] = sc
        m_scr[i % 2] = jnp.broadcast_to(m, m_scr.shape[1:])

    def probs(i):
        m = m_scr[i % 2][:, 0:1]
        p_scr[i % 2] = jnp.exp2(s_scr[i % 2] - m).astype(BF16)
        if has_cache:
            pc_scr[i % 2] = jnp.exp2(sc_scr[i % 2] - m).astype(BF16)

    pending = {}

    def weighted_values(i):
        rows, head = items[i]
        o = _bdot(p_scr[i % 2], v_ref[:, head])
        if has_cache:
            o = o + _bdot(pc_scr[i % 2], vc_ref[:, head])
        o = o / o[:, DEN_LANE:DEN_LANE + 1]
        h = i % hp
        if h % 2 == 0:
            pending[0] = o
        else:
            _store_head_pair(o_ref, rows, h // 2, pending.pop(0), o)

    scores(0)
    for i in range(n):
        if i + 1 < n:
            scores(i + 1)
        if i > 0:
            weighted_values(i - 1)
        probs(i)
    weighted_values(n - 1)


def _attn_short_kernel(hp, q_ref, k_ref, v_ref, o_ref):
    tn = (((0,), (0,)), ((), ()))
    heads = [slice(h * HEAD_PAD, (h + 1) * HEAD_PAD) for h in range(hp)]
    scores = [lax.dot_general(q_ref[hd, :], k_ref[hd, :], tn, preferred_element_type=F32) for hd in heads]
    probs = [jnp.exp2(s - jnp.max(s, axis=-1, keepdims=True)).astype(BF16) for s in scores]
    outs = []
    for hd, p in zip(heads, probs):
        o = _bdot(p, v_ref[:, hd])
        outs.append(o / o[:, DEN_LANE:DEN_LANE + 1])
    for j in range(hp // 2):
        _store_head_pair(o_ref, slice(None), j, outs[2 * j], outs[2 * j + 1])


def _attn(q, k, v, cache, hp, tq):
    s, l, _ = v.shape
    w = hp * HEAD_PAD
    qspec = pl.BlockSpec((w, tq), lambda i, h, r: (h, i * (l // tq) + r))
    ospec = pl.BlockSpec((None, tq, hp * V_DIM), lambda i, h, r: (i, r, h))
    if cache is None and tq == l:
        kspec = pl.BlockSpec((w, l), lambda i, h, r: (h, i))
        vspec = pl.BlockSpec((None, l, w), lambda i, h, r: (i, 0, h))
        return pl.pallas_call(
            functools.partial(_attn_short_kernel, hp),
            grid=(s, MLA_HEADS // hp, 1),
            in_specs=[qspec, kspec, vspec],
            out_specs=ospec,
            out_shape=jax.ShapeDtypeStruct((s, l, MLA_OUT), BF16),
            compiler_params=_cparams(("arbitrary", "arbitrary", "arbitrary")),
            name="attn_short",
        )(q, k, v)
    kv_mode = dict(pipeline_mode=pl.Buffered(1))
    kspec = pl.BlockSpec((w, l), lambda i, h, r: (h, i), **kv_mode)
    vspec = pl.BlockSpec((None, l, w), lambda i, h, r: (i, 0, h), **kv_mode)
    in_specs = [qspec, kspec, vspec]
    args = [q, k, v]
    if cache is not None:
        lc = cache[1].shape[1]
        in_specs += [pl.BlockSpec((w, lc), lambda i, h, r: (h, i), **kv_mode),
                     pl.BlockSpec((None, lc, w), lambda i, h, r: (i, 0, h), **kv_mode)]
        args += list(cache)
    sub = min(tq, ATTN_SUB)
    scratch = [pltpu.VMEM((2, sub, l), F32), pltpu.VMEM((2, sub, l), BF16), pltpu.VMEM((2, sub, LANE), F32)]
    if cache is not None:
        scratch += [pltpu.VMEM((2, sub, lc), F32), pltpu.VMEM((2, sub, lc), BF16)]
    return pl.pallas_call(
        functools.partial(_attn_kernel, hp, sub, cache is not None),
        grid=(s, MLA_HEADS // hp, l // tq),
        in_specs=in_specs,
        out_specs=ospec,
        out_shape=jax.ShapeDtypeStruct((s, l, MLA_OUT), BF16),
        scratch_shapes=scratch,
        compiler_params=_cparams(("arbitrary", "arbitrary", "arbitrary")),
        name="attn",
    )(*args)


def _route(lt):
    row = lax.broadcasted_iota(jnp.int32, (SUB, lt.shape[1]), 0).astype(F32)
    neg = -jnp.inf
    gl = lt[0:SUB]
    g_ok = row < N_GROUPS
    glm = jnp.where(g_ok, gl, neg)
    gmax = jnp.max(glm, axis=0, keepdims=True)
    gsel = jnp.min(jnp.where(glm == gmax, row, float(SUB)), axis=0, keepdims=True)
    g_w = 1.0 / jnp.sum(jnp.where(g_ok, jnp.exp(gl - gmax), 0.0), axis=0, keepdims=True)
    es = lt[SUB:2 * SUB]
    for g in range(1, N_GROUPS):
        es = jnp.where(gsel == float(g), lt[(g + 1) * SUB:(g + 2) * SUB], es)
    m1 = jnp.max(es, axis=0, keepdims=True)
    i1 = jnp.min(jnp.where(es == m1, row, float(SUB)), axis=0, keepdims=True)
    es2 = jnp.where(row == i1, neg, es)
    m2 = jnp.max(es2, axis=0, keepdims=True)
    i2 = jnp.min(jnp.where(es2 == m2, row, float(SUB)), axis=0, keepdims=True)
    e2 = jnp.exp(m2 - m1)
    inv = g_w / (1.0 + e2)
    base = gsel * EXP_PER_GROUP
    return (jnp.where(row == RT_ID0, base + i1, 0.0) + jnp.where(row == RT_ID1, base + i2, 0.0)
            + jnp.where(row == RT_W0, inv, 0.0) + jnp.where(row == RT_W1, inv * e2, 0.0))


def _pack_bf16_pairs(x):
    k = x.shape[1] // 2
    bits = lax.bitcast_convert_type(x.astype(BF16).astype(F32), jnp.int32)
    return lax.shift_right_logical(bits[:, :k], 16) | (bits[:, k:] & HI_MASK)


def _unpack_bf16_pairs(w):
    lo = lax.bitcast_convert_type(lax.shift_left(w, 16), F32).astype(BF16)
    hi = lax.bitcast_convert_type(w & HI_MASK, F32).astype(BF16)
    return lo, hi


def _back_kernel(x_ref, mod_ref, g1_ref, g2_ref, yf_ref, yl_ref, ys_ref, o_ref,
                 wg_ref, wpf_ref, wpl_ref, wps_ref, wpm_ref, wo_ref, wrh_ref, wrl_ref, br_ref,
                 x1_ref, h2_ref, comb_ref):
    d = D_MODEL
    x = x_ref[...]
    sh1, sc1, gt1 = mod_ref[:, 0:d], mod_ref[:, d:2 * d], mod_ref[:, 2 * d:3 * d]
    sh2, sc2 = mod_ref[:, 3 * d:4 * d], mod_ref[:, 4 * d:5 * d]
    hb = (_rms(x, g1_ref[...]) * (1.0 + sc1) + sh1).astype(BF16)
    branches = ((yf_ref, wpf_ref), (yl_ref, wpl_ref), (ys_ref, wps_ref), (o_ref, wpm_ref))
    blocks = []
    for n in range(0, d, MERGE_BLK):
        acc = None
        for j, (y_ref, w_ref) in enumerate(branches):
            gate = jax.nn.sigmoid(_bdot(hb, wg_ref[:, j * d + n:j * d + n + MERGE_BLK]))
            term = gate * _bdot(y_ref[...], w_ref[:, n:n + MERGE_BLK])
            acc = term if acc is None else acc + term
        blocks.append(acc.astype(BF16))
    x1 = x + gt1 * _bdot(jnp.concatenate(blocks, axis=1), wo_ref[...])
    x1_ref[...] = x1
    h2 = _rms(x1, g2_ref[...]) * (1.0 + sc2) + sh2
    h2b = h2.astype(BF16)
    h2_ref[...] = _pack_bf16_pairs(h2)
    h2l = (h2 - h2b.astype(F32)).astype(BF16)
    nt = (((1,), (1,)), ((), ()))
    rdot = lambda w_ref, h: lax.dot_general(w_ref[...], h, nt, preferred_element_type=F32)
    logits_t = rdot(wrh_ref, h2b) + rdot(wrh_ref, h2l) + rdot(wrl_ref, h2b) + br_ref[...]
    comb_ref[...] = _route(logits_t)


def _back(x, mod, yf, yl, ys, o, p, tm):
    t = x.shape[0]
    per_mod = t // (mod.shape[0] * tm)
    row = lambda w: pl.BlockSpec((tm, w), lambda i: (i, 0))
    full = lambda a: pl.BlockSpec(a.shape, lambda i: (0,) * a.ndim, pipeline_mode=pl.Buffered(1))
    weights = [p["w_gate"], p["w_pf"], p["w_pl"], p["w_ps"], p["w_pm"], p["w_out"], p["wr_hi"], p["wr_lo"], p["b_r"]]
    return pl.pallas_call(
        _back_kernel,
        grid=(t // tm,),
        in_specs=[
            row(D_MODEL),
            pl.BlockSpec((None, 1, 6 * D_MODEL), lambda i: (i // per_mod, 0, 0)),
            pl.BlockSpec((1, D_MODEL), lambda i: (0, 0)),
            pl.BlockSpec((1, D_MODEL), lambda i: (0, 0)),
            row(FNET_W), row(LRU_W), row(SC_W), row(MLA_OUT),
        ] + [full(w) for w in weights],
        out_specs=[row(D_MODEL), row(PACK_W), pl.BlockSpec((SUB, tm), lambda i: (0, i))],
        out_shape=[
            jax.ShapeDtypeStruct((t, D_MODEL), F32),
            jax.ShapeDtypeStruct((t, PACK_W), jnp.int32),
            jax.ShapeDtypeStruct((SUB, t), F32),
        ],
        compiler_params=_cparams(("arbitrary",)),
        name="back",
    )(x, mod, p["norm1_g"], p["norm2_g"], yf, yl, ys, o, *weights)


def _plan_kernel(tm, rt_ref, pos_ref, te_ref, nu_ref):
    ids = jnp.concatenate([rt_ref[RT_ID0:RT_ID0 + 1, :], rt_ref[RT_ID1:RT_ID1 + 1, :]], axis=1)
    n_rows = ids.shape[1]
    erow = lax.broadcasted_iota(jnp.int32, (N_EXPERTS, PLAN_BLK), 0).astype(F32)
    col = lax.broadcasted_iota(jnp.int32, (N_EXPERTS, LANE), 0)
    blocks = [(ids[:, b * PLAN_BLK:(b + 1) * PLAN_BLK] == erow).astype(F32) for b in range(n_rows // PLAN_BLK)]
    tri = (lax.broadcasted_iota(jnp.int32, (PLAN_BLK, PLAN_BLK), 0)
           <= lax.broadcasted_iota(jnp.int32, (PLAN_BLK, PLAN_BLK), 1)).astype(BF16)
    prefix = [_bdot(blk.astype(BF16), tri) for blk in blocks]
    counts = prefix[0][:, PLAN_BLK - 1:PLAN_BLK]
    for pre in prefix[1:]:
        counts = counts + pre[:, PLAN_BLK - 1:PLAN_BLK]
    tiles = jnp.floor((counts + float(tm - 1)) * (1.0 / tm))
    tile_end = jnp.broadcast_to(tiles, (N_EXPERTS, LANE))
    for s in (1, 2, 4, 8, 16):
        tile_end = tile_end + jnp.where(col >= s, pltpu.roll(tile_end, s, 0), 0.0)
    base = (tile_end[:, 0:1] - tiles) * float(tm) - 1.0
    for b, (blk, pre) in enumerate(zip(blocks, prefix)):
        pos = jnp.sum(blk * (base + pre), axis=0, keepdims=True)
        pos_ref[:, b * PLAN_BLK:(b + 1) * PLAN_BLK] = pos.astype(jnp.int32)
        base = base + pre[:, PLAN_BLK - 1:PLAN_BLK]
    k = lax.broadcasted_iota(jnp.int32, (N_EXPERTS, LANE), 1).astype(F32)
    owner = jnp.sum((k >= tile_end).astype(F32), axis=0, keepdims=True)
    te_ref[...] = jnp.minimum(owner, float(N_EXPERTS - 1)).astype(jnp.int32)
    nu_ref[...] = jnp.broadcast_to(tile_end[N_EXPERTS - 1:N_EXPERTS, :], nu_ref.shape).astype(jnp.int32)


def _dispatch_plan(rt, tm):
    n_rows = TOP_K * rt.shape[1]
    n_tiles = n_rows // tm + N_EXPERTS
    assert n_tiles <= LANE and tm & (tm - 1) == 0 and n_rows % PLAN_BLK == 0
    pos, te, nu = pl.pallas_call(
        functools.partial(_plan_kernel, tm),
        out_shape=[jax.ShapeDtypeStruct((1, n_rows), jnp.int32), jax.ShapeDtypeStruct((1, LANE), jnp.int32),
                   jax.ShapeDtypeStruct((1, LANE), jnp.int32)],
        compiler_params=_cparams(None),
        name="plan",
    )(rt)
    return pos.reshape(n_rows), te[0, :n_tiles], nu[0, :1], n_tiles


def _sc_worker_rows(n_rows):
    workers = SC_CORES * SC_SUBCORES
    per_w = n_rows // workers
    assert per_w * workers == n_rows and per_w % SC_CHUNK == 0
    return per_w


def _sc_mesh():
    return plsc.VectorSubcoreMesh(core_axis_name="c", subcore_axis_name="s",
                                  num_cores=SC_CORES, num_subcores=SC_SUBCORES)


def _sc_scatter_rows(src, pos, n_out):
    t, w = src.shape
    per_w = _sc_worker_rows(pos.shape[0])

    @functools.partial(
        pl.kernel, mesh=_sc_mesh(), out_type=jax.ShapeDtypeStruct((n_out, w), src.dtype),
        scratch_types=[pltpu.VMEM((SC_CHUNK,), jnp.int32), pltpu.VMEM((SC_CHUNK, w), src.dtype)],
        name="moe_dispatch")
    def run(src_hbm, pos_hbm, out_hbm, idx_v, rows_v):
        wid = lax.axis_index("s") * SC_CORES + lax.axis_index("c")

        @pl.loop(0, per_w // SC_CHUNK)
        def _(i):
            j0 = wid * per_w + i * SC_CHUNK
            pltpu.sync_copy(pos_hbm.at[pl.ds(j0, SC_CHUNK)], idx_v)
            pltpu.sync_copy(src_hbm.at[pl.ds(lax.rem(j0, t), SC_CHUNK)], rows_v)
            pltpu.sync_copy(rows_v, out_hbm.at[idx_v])

    return run(src, pos)


def _sc_gather_rows(src, pos):
    w = src.shape[1]
    n = pos.shape[0]
    per_w = _sc_worker_rows(n)

    @functools.partial(
        pl.kernel, mesh=_sc_mesh(), out_type=jax.ShapeDtypeStruct((n, w), src.dtype),
        scratch_types=[pltpu.VMEM((SC_CHUNK,), jnp.int32), pltpu.VMEM((SC_CHUNK, w), src.dtype)],
        name="moe_return")
    def run(src_hbm, pos_hbm, out_hbm, idx_v, rows_v):
        wid = lax.axis_index("s") * SC_CORES + lax.axis_index("c")

        @pl.loop(0, per_w // SC_CHUNK)
        def _(i):
            j0 = wid * per_w + i * SC_CHUNK
            pltpu.sync_copy(pos_hbm.at[pl.ds(j0, SC_CHUNK)], idx_v)
            pltpu.sync_copy(src_hbm.at[idx_v], rows_v)
            pltpu.sync_copy(rows_v, out_hbm.at[pl.ds(j0, SC_CHUNK)])

    return run(src, pos)


def _experts_kernel(te_ref, nu_ref, x_ref, w1_ref, w3_ref, w2_ref, y_ref):
    half = D_MODEL // 2

    @pl.when(pl.program_id(0) < nu_ref[0])
    def _():
        lo, hi = _unpack_bf16_pairs(x_ref[...])
        a = _bdot(lo, w1_ref[:half, :].astype(BF16)) + _bdot(hi, w1_ref[half:, :].astype(BF16))
        u = _bdot(lo, w3_ref[:half, :].astype(BF16)) + _bdot(hi, w3_ref[half:, :].astype(BF16))
        mid = (a * jax.nn.sigmoid(a)) * u
        y_ref[...] = _pack_bf16_pairs(_bdot(mid.astype(BF16), w2_ref[...].astype(BF16)))


def _experts(xs, tile_expert, n_used, n_tiles, w1, w3, w2, layer, tm):
    wspec = lambda a: pl.BlockSpec((None, None) + a.shape[2:], lambda i, te, nu: (layer, te[i], 0, 0))
    grid_spec = pltpu.PrefetchScalarGridSpec(
        num_scalar_prefetch=2,
        grid=(n_tiles,),
        in_specs=[pl.BlockSpec((tm, PACK_W), lambda i, te, nu: (jnp.minimum(i, nu[0] - 1), 0)),
                  wspec(w1), wspec(w3), wspec(w2)],
        out_specs=pl.BlockSpec((tm, PACK_W), lambda i, te, nu: (jnp.where(i < nu[0], i, n_tiles), 0)),
    )
    return pl.pallas_call(
        _experts_kernel,
        grid_spec=grid_spec,
        out_shape=jax.ShapeDtypeStruct((xs.shape[0] + tm, PACK_W), jnp.int32),
        compiler_params=_cparams(("arbitrary",)),
        name="experts",
    )(tile_expert, n_used, xs, w1, w3, w2)


def _combine_kernel(x1_ref, mod_ref, rt_ref, g0_ref, g1_ref, o_ref):
    rt = rt_ref[...]
    lo0, hi0 = _unpack_bf16_pairs(g0_ref[...])
    lo1, hi1 = _unpack_bf16_pairs(g1_ref[...])
    w0 = rt[:, RT_W0:RT_W0 + 1]
    w1 = rt[:, RT_W1:RT_W1 + 1]
    half = D_MODEL // 2
    gt2 = mod_ref[:, 5 * D_MODEL:6 * D_MODEL]
    o_ref[:, :half] = x1_ref[:, :half] + gt2[:, :half] * (w0 * lo0.astype(F32) + w1 * lo1.astype(F32))
    o_ref[:, half:] = x1_ref[:, half:] + gt2[:, half:] * (w0 * hi0.astype(F32) + w1 * hi1.astype(F32))


def _combine(x1, mod, rt, g, tm):
    t = x1.shape[0]
    per_mod = t // (mod.shape[0] * tm)
    return pl.pallas_call(
        _combine_kernel,
        grid=(t // tm,),
        in_specs=[
            pl.BlockSpec((tm, D_MODEL), lambda i: (i, 0)),
            pl.BlockSpec((None, 1, 6 * D_MODEL), lambda i: (i // per_mod, 0, 0)),
            pl.BlockSpec((tm, LANE), lambda i: (i, 0)),
            pl.BlockSpec((tm, PACK_W), lambda i: (i, 0)),
            pl.BlockSpec((tm, PACK_W), lambda i: (i + t // tm, 0)),
        ],
        out_specs=pl.BlockSpec((tm, D_MODEL), lambda i: (i, 0)),
        out_shape=jax.ShapeDtypeStruct((t, D_MODEL), F32),
        compiler_params=_cparams(("arbitrary",)),
        name="combine",
    )(x1, mod, rt, g, g)


def _moe(h2p, rt, x1, mod, p, tm_e, tm_c):
    pos, tile_expert, n_used, n_tiles = _dispatch_plan(rt, tm_e)
    xs = _sc_scatter_rows(h2p, pos, n_tiles * tm_e)
    ys = _experts(xs, tile_expert, n_used, n_tiles, p["w1"], p["w3"], p["w2"], p["layer"], tm_e)
    g = _sc_gather_rows(ys, pos)
    rt_cols = jnp.pad(rt.T, ((0, 0), (0, LANE - rt.shape[0])))
    return _combine(x1, mod, rt_cols, g, tm_c)


def _channel_dft():
    n = np.arange(FNET_W // FNET_GROUPS)
    ang = 2.0 * np.pi * ((n[:, None] * n[None, :]) % n.size) / n.size
    eye = np.eye(FNET_GROUPS)
    scale = 1.0 / math.sqrt(n.size)
    return (jnp.asarray(np.kron(eye, np.cos(ang) * scale), F32).astype(BF16),
            jnp.asarray(np.kron(eye, np.sin(ang) * scale), F32).astype(BF16))


def _position_dft(l):
    scale = 1.0 / math.sqrt(l)
    if l <= 256:
        n = np.arange(l)
        ang = 2.0 * np.pi * ((n[:, None] * n[None, :]) % l) / l
        return jnp.asarray(np.cos(ang) * scale, F32).astype(BF16), jnp.asarray(np.sin(ang) * scale, F32).astype(BF16)
    m = DFT_SPLIT
    n = np.arange(l)
    ang_a = 2.0 * np.pi * ((np.arange(l // m)[:, None] * m * n[None, :]) % l) / l
    ang_b = 2.0 * np.pi * ((np.arange(m)[:, None] * n[None, :]) % l) / l
    return (jnp.asarray(np.cos(ang_a), F32), jnp.asarray(np.sin(ang_a), F32),
            jnp.asarray(np.cos(ang_b) * scale, F32), jnp.asarray(np.sin(ang_b) * scale, F32))


def _rope_tables(l, rotate):
    cos = np.ones((l, LANE))
    sinr = np.zeros((l, LANE))
    if rotate:
        t = np.arange(l)
        inv = ROPE_BASE ** (-np.arange(N_FREQ) / N_FREQ)
        ang = np.concatenate([(t // GRID_W)[:, None] * inv, (t % GRID_W)[:, None] * inv], axis=-1)
        half = ROPE // 2
        for lo in (ROPE_X1, ROPE_X2):
            cos[:, lo:lo + half] = np.cos(ang)
        sinr[:, ROPE_X1:ROPE_X1 + half] = -np.sin(ang)
        sinr[:, ROPE_X2:ROPE_X2 + half] = np.sin(ang)
    return tuple(jnp.asarray(a, F32) for a in (cos, sinr))


def _head_lane_source():
    half = ROPE // 2
    src = np.full((HEAD_PAD,), QK_DIM, np.int32)
    src[ROPE_X1:ROPE_X1 + half] = NOPE + np.arange(half)
    src[ROPE_X2:ROPE_X2 + half] = NOPE + half + np.arange(half)
    free = [i for i in range(HEAD_PAD) if src[i] == QK_DIM][:NOPE]
    src[free] = np.arange(NOPE)
    return src


def _place_head_dims(w, n_src):
    src = _head_lane_source()
    src = np.where(src < n_src, src, n_src)
    wz = jnp.concatenate([w[..., :n_src], jnp.zeros(w.shape[:-1] + (1,), w.dtype)], axis=-1)
    out = jnp.take(wz, jnp.asarray(src), axis=-1)
    return out.reshape(out.shape[:-2] + (out.shape[-2] * HEAD_PAD,))


def _place_rope_key(kr):
    half = ROPE // 2
    out = jnp.zeros(kr.shape[:-1] + (LANE,), kr.dtype)
    return out.at[..., ROPE_X1:ROPE_X1 + half].set(kr[..., :half]).at[..., ROPE_X2:ROPE_X2 + half].set(kr[..., half:])


def _take_rope_key(tile):
    half = ROPE // 2
    return jnp.concatenate([tile[..., ROPE_X1:ROPE_X1 + half], tile[..., ROPE_X2:ROPE_X2 + half]], axis=-1)


def _pad_heads(w, lo, hi):
    r = w.shape[0]
    part = w[:, :, lo:hi]
    out = jnp.zeros((r, MLA_HEADS, HEAD_PAD), w.dtype).at[:, :, : hi - lo].set(part)
    return out.reshape(r, HP)


def _blockdiag_halves(w):
    bw = LRU_W // LRU_BLOCKS
    out = jnp.zeros((2, 2, LANE, LANE), w.dtype)
    for half in range(2):
        for k in range(2):
            n = 2 * half + k
            out = out.at[half, :, k * bw:(k + 1) * bw, k * bw:(k + 1) * bw].set(w[:, n])
    return out


def _halves(v):
    return jnp.moveaxis(v.reshape(v.shape[:-1] + (2, LANE)), -2, 0)


def _layer_params(l, a):
    d = D_MODEL
    w_in = a["w_in"][l]
    w_in = jnp.concatenate([w_in[:, :Z_KR], _place_rope_key(w_in[:, Z_KR:])], axis=1)
    w_r = jnp.zeros((RT_ROWS, d), F32).at[:N_GROUPS].set(a["w_gr"][l].T).at[SUB:SUB + N_EXPERTS].set(a["w_er"][l].T)
    b_r = jnp.zeros((RT_ROWS, 1), F32).at[:N_GROUPS, 0].set(a["b_gr"][l]).at[SUB:SUB + N_EXPERTS, 0].set(a["b_er"][l])
    wr_hi = w_r.astype(BF16)
    gpad = lambda g: jnp.take(jnp.concatenate([g, jnp.zeros((1,), F32)]), jnp.asarray(_head_lane_source())).reshape(1, LANE)
    return {
        "norm1_g": a["norm1_g"][l].reshape(1, d), "norm2_g": a["norm2_g"][l].reshape(1, d),
        "w_in": w_in.astype(BF16),
        "lru_conv": _halves(a["lru_conv"][l]),
        "lru_wa": _blockdiag_halves(a["lru_wa"][l]).astype(BF16),
        "lru_wx": _blockdiag_halves(a["lru_wx"][l]).astype(BF16),
        "lru_ba": _halves(a["lru_ba"][l])[:, :, None, :], "lru_bx": _halves(a["lru_bx"][l])[:, :, None, :],
        "lru_lam": _halves(a["lru_lam"][l])[:, :, None, :],
        "sc_conv": _halves(a["sc_conv"][l]),
        "g_qa": a["g_qa"][l].reshape(1, Q_LORA), "g_kva": a["g_kva"][l].reshape(1, KV_LORA),
        "wq_t": _place_head_dims(a["w_qb"][l], QK_DIM).T.astype(BF16),
        "wk_t": _place_head_dims(a["w_kvb"][l], NOPE).T.astype(BF16),
        "wv": _pad_heads(a["w_kvb"][l], NOPE, NOPE + V_DIM).astype(BF16),
        "gq_col": gpad(a["g_qn"][l]).reshape(LANE, 1), "gk_col": gpad(a["g_kn"][l]).reshape(LANE, 1),
        "eye": jnp.eye(LANE, dtype=F32).astype(BF16),
        "w_gate": a["w_gate"][l].astype(BF16),
        "w_pf": a["w_pf"][l].astype(BF16), "w_pl": a["w_pl"][l].astype(BF16), "w_ps": a["w_ps"][l].astype(BF16),
        "w_pm": a["w_pm"][l].astype(BF16), "w_out": a["w_out"][l].astype(BF16),
        "wr_hi": wr_hi, "wr_lo": (w_r - wr_hi.astype(F32)).astype(BF16), "b_r": b_r,
        "w1": a["w1"], "w3": a["w3"], "w2": a["w2"], "layer": l,
    }


def _trunk_layer(x, mod, p, consts, h0, cache, cfg):
    s, l = cfg["s"], cfg["l"]
    z = _front(x, mod, p["norm1_g"], p["w_in"], cfg["tm_front"])
    z3 = z.reshape(s, l, Z_W)
    q, k, v, ckv = _qkv(z, consts["rope"], p, l, cfg["tm_qkv"])
    yf = _fourier(z3, consts["dft"], cfg["nb"], cfg["tq_f"])
    yl, ys, fin = _lru(z3, h0, p, cfg["nb_lru"])
    o = _attn(q, k, v.reshape(s, l, HP), cache, cfg["hp"], cfg["tq_a"])
    t = s * l
    x1, h2, comb = _back(x, mod, yf.reshape(t, FNET_W), yl.reshape(t, LRU_W), ys.reshape(t, SC_W),
                         o.reshape(t, MLA_OUT), p, cfg["tm_back"])
    x2 = _moe(h2, comb, x1, mod, p, cfg["tm_moe"], cfg["tm_comb"])
    return x2, z3, ckv, fin


def kernel(x_prompt, x_sample, cache_ckv, cache_krope, state_rglru, c, c_ctx, norm1_g, norm2_g, w_ada, b_ada, w_in, lru_conv, lru_wa, lru_ba, lru_wx, lru_bx, lru_lam, sc_conv, g_qa, w_qb, g_kva, w_kvb, g_qn, g_kn, w_pf, w_pl, w_ps, w_pm, w_gate, w_out, w_gr, b_gr, w_er, b_er, w1, w3, w2):
    a = dict(norm1_g=norm1_g, norm2_g=norm2_g, w_in=w_in, lru_conv=lru_conv, lru_wa=lru_wa, lru_ba=lru_ba,
             lru_wx=lru_wx, lru_bx=lru_bx, lru_lam=lru_lam, sc_conv=sc_conv, g_qa=g_qa, w_qb=w_qb, g_kva=g_kva,
             w_kvb=w_kvb, g_qn=g_qn, g_kn=g_kn, w_pf=w_pf, w_pl=w_pl, w_ps=w_ps, w_pm=w_pm, w_gate=w_gate,
             w_out=w_out, w_gr=w_gr, b_gr=b_gr, w_er=w_er, b_er=b_er, w1=w1, w3=w3, w2=w2)
    bc, lc, d = x_prompt.shape
    bl, ll, _ = x_sample.shape
    past = cache_ckv.shape[2]

    cond8 = jnp.zeros((SUB, d), F32).at[0].set(c_ctx).at[1:1 + bl].set(c)
    mod_all = _ada(cond8, w_ada, b_ada)

    cc, sc = _channel_dft()
    ctx_consts = {"dft": (cc, sc) + _position_dft(lc), "rope": _rope_tables(lc, False)}
    lat_consts = {"dft": (cc, sc) + _position_dft(ll), "rope": _rope_tables(ll, True)}
    cache_tabs = _rope_tables(bl * past, False)
    ctx_cfg = dict(s=bc, l=lc, tm_front=512, tm_qkv=512, nb=8, nb_lru=4, tq_f=lc, hp=MLA_HEADS, tq_a=lc, tm_back=1024, tm_moe=512, tm_comb=512)
    lat_cfg = dict(s=bl, l=ll, tm_front=512, tm_qkv=512, nb=bl, nb_lru=1, tq_f=512, hp=MLA_HEADS, tq_a=256, tm_back=1024, tm_moe=512, tm_comb=512)

    xp = x_prompt.reshape(bc * lc, d)
    xs = x_sample.reshape(bl * ll, d)
    h0_ctx = jnp.zeros((bc, 2, LRU_W), F32)
    ckv_list, kr_list, lru_list = [], [], []
    for l in range(DEPTH):
        p = _layer_params(l, a)
        mod_ctx = mod_all[l, 0:1].reshape(1, 1, 6 * d)
        mod_lat = mod_all[l, 1:1 + bl].reshape(bl, 1, 6 * d)
        xp, z3, ckv, fin = _trunk_layer(xp, mod_ctx, p, ctx_consts, h0_ctx, None, ctx_cfg)
        ckv_list.append(ckv.reshape(bc, lc, KV_LORA))
        kr_list.append(_take_rope_key(z3[:, :, Z_KR:]))
        lru_list.append(fin)
        kr_tile = _place_rope_key(cache_krope[:, l].reshape(bl * past, ROPE))
        kc, vc = _kvcache(cache_ckv[:, l].reshape(bl * past, KV_LORA), kr_tile, cache_tabs, p)
        cache = (kc, vc.reshape(bl, past, HP))
        xs, _, _, _ = _trunk_layer(xs, mod_lat, p, lat_consts, state_rglru[:, l], cache, lat_cfg)
    return (xp.reshape(bc, lc, d), xs.reshape(bl, ll, d),
            jnp.stack(ckv_list, axis=1), jnp.stack(kr_list, axis=1), jnp.stack(lru_list, axis=1))
```

```python
import functools
import math

import numpy as np
import jax
import jax.numpy as jnp
from jax import lax
from jax.experimental import pallas as pl
from jax.experimental.pallas import tpu as pltpu
from jax.experimental.pallas import tpu_sc as plsc

F32 = jnp.float32
BF16 = jnp.bfloat16

D_MODEL = 1024
DEPTH = 2
GRID_W = 64
EPS = 1e-6
FNET_W = 256
FNET_GROUPS = 4
LRU_W = 256
LRU_BLOCKS = 4
LRU_C = 8.0
SC_W = 256
MLA_HEADS = 8
Q_LORA = 384
KV_LORA = 256
NOPE = 64
ROPE = 32
V_DIM = 64
QK_DIM = NOPE + ROPE
MLA_OUT = MLA_HEADS * V_DIM
N_FREQ = ROPE // 4
ROPE_BASE = 10000.0
ATTN_SCALE = QK_DIM ** -0.5
LOG2E = math.log2(math.e)
N_GROUPS = 4
EXP_PER_GROUP = 8
N_EXPERTS = N_GROUPS * EXP_PER_GROUP
D_EXPERT = 256

LANE = 128
SUB = 8
HEAD_PAD = LANE
HP = MLA_HEADS * HEAD_PAD
ROPE_X1 = 0
ROPE_X2 = LANE // 2
DEN_LANE = V_DIM
Z_QKV = 6 * 256
Z_KR = Z_QKV + Q_LORA + KV_LORA
Z_W = Z_KR + LANE
QKV_W = Q_LORA + KV_LORA + LANE
VMEM_LIMIT = 52 * 1024 * 1024
PACK_W = D_MODEL // 2
HI_MASK = -65536
RT_ID0, RT_ID1, RT_W0, RT_W1 = 0, 1, 2, 3
RT_ROWS = 48
TOP_K = 2
MERGE_BLK = 256
PLAN_BLK = 256
DFT_SPLIT = 64
ATTN_SUB = 256
SC_CORES = 2
SC_SUBCORES = 16
SC_CHUNK = 128


def _cparams(sem):
    return pltpu.CompilerParams(dimension_semantics=sem, vmem_limit_bytes=VMEM_LIMIT)


def _bdot(a, b):
    return jnp.dot(a, b, preferred_element_type=F32)


def _rms(x, g):
    return x * lax.rsqrt(jnp.mean(x * x, axis=-1, keepdims=True) + EPS) * g


def _ada_kernel(c_ref, w_ref, b_ref, o_ref):
    c = c_ref[...]
    s = (c * jax.nn.sigmoid(c)).astype(BF16)
    o_ref[...] = _bdot(s, w_ref[...].astype(BF16)) + b_ref[...]


def _ada(cond8, w_ada, b_ada):
    nblk = 6 * D_MODEL // 1024
    return pl.pallas_call(
        _ada_kernel,
        grid=(DEPTH, nblk),
        in_specs=[
            pl.BlockSpec((SUB, D_MODEL), lambda l, n: (0, 0)),
            pl.BlockSpec((None, D_MODEL, 1024), lambda l, n: (l, 0, n)),
            pl.BlockSpec((None, 1, 1024), lambda l, n: (l, 0, n)),
        ],
        out_specs=pl.BlockSpec((None, SUB, 1024), lambda l, n: (l, 0, n)),
        out_shape=jax.ShapeDtypeStruct((DEPTH, SUB, 6 * D_MODEL), F32),
        compiler_params=_cparams(("arbitrary", "arbitrary")),
        name="ada",
    )(cond8, w_ada, b_ada.reshape(DEPTH, 1, 6 * D_MODEL))


def _front_kernel(x_ref, mod_ref, g_ref, w_ref, z_ref):
    x = x_ref[...]
    sh = mod_ref[:, 0:D_MODEL]
    sc = mod_ref[:, D_MODEL:2 * D_MODEL]
    h = _rms(x, g_ref[...]) * (1.0 + sc) + sh
    z_ref[...] = _bdot(h.astype(BF16), w_ref[...])


def _front(x, mod, g1, w_in, tm):
    t = x.shape[0]
    per_mod = t // (mod.shape[0] * tm)
    return pl.pallas_call(
        _front_kernel,
        grid=(t // tm,),
        in_specs=[
            pl.BlockSpec((tm, D_MODEL), lambda i: (i, 0)),
            pl.BlockSpec((None, 1, 6 * D_MODEL), lambda i: (i // per_mod, 0, 0)),
            pl.BlockSpec((1, D_MODEL), lambda i: (0, 0)),
            pl.BlockSpec((D_MODEL, Z_W), lambda i: (0, 0)),
        ],
        out_specs=pl.BlockSpec((tm, Z_W), lambda i: (i, 0)),
        out_shape=jax.ShapeDtypeStruct((t, Z_W), F32),
        compiler_params=_cparams(("arbitrary",)),
        name="front",
    )(x, mod, g1, w_in)


def _rope(x, cos, sinr):
    return x * cos + pltpu.roll(x, LANE // 2, 1) * sinr


def _head_norm(x, g):
    ss = jnp.sum(x * x, axis=-1, keepdims=True) * (1.0 / QK_DIM)
    return x * lax.rsqrt(ss + EPS) * g


def _head_norm_rope_t(x_t, g_col, cos_t, sinr_t):
    ss = jnp.sum(x_t * x_t, axis=0, keepdims=True) * (1.0 / QK_DIM)
    x_t = x_t * lax.rsqrt(ss + EPS) * g_col
    return (x_t * cos_t + pltpu.roll(x_t, HEAD_PAD // 2, 0) * sinr_t).astype(BF16)


def _build_kv(ckv, kr_tile, cos_t, sinr_t, wk_t_ref, wv_ref, gk_col, eye_ref, kt_ref, v_ref):
    nt = (((1,), (1,)), ((), ()))
    cb = ckv.astype(BF16)
    kn_t = lax.dot_general(wk_t_ref[...], cb, nt, preferred_element_type=F32)
    kr_hi = kr_tile.astype(BF16)
    kr_lo = (kr_tile - kr_hi.astype(F32)).astype(BF16)
    kr_t = (lax.dot_general(eye_ref[...], kr_hi, nt, preferred_element_type=F32)
            + lax.dot_general(eye_ref[...], kr_lo, nt, preferred_element_type=F32))
    for h in range(MLA_HEADS):
        sl = slice(h * HEAD_PAD, (h + 1) * HEAD_PAD)
        kt_ref[sl, :] = _head_norm_rope_t(kn_t[sl, :] + kr_t, gk_col, cos_t, sinr_t)
    v = _bdot(cb, wv_ref[...])
    lane = lax.broadcasted_iota(jnp.int32, v.shape, 1)
    v_ref[...] = jnp.where((lane & (HEAD_PAD - 1)) == DEN_LANE, 1.0, v).astype(BF16)


def _qkv_kernel(z_ref, cos_t_ref, sinr_t_ref, gqa_ref, wq_ref, gkva_ref, wk_ref, wv_ref,
                gq_ref, gk_ref, eye_ref, q_ref, k_ref, v_ref, ckv_ref):
    cos_t, sinr_t = cos_t_ref[...], sinr_t_ref[...]
    q_c = z_ref[:, 0:Q_LORA]
    kv_c = z_ref[:, Q_LORA:Q_LORA + KV_LORA]
    kr_tile = z_ref[:, Q_LORA + KV_LORA:QKV_W]
    nt = (((1,), (1,)), ((), ()))
    qf_t = lax.dot_general(wq_ref[...], _rms(q_c, gqa_ref[...]).astype(BF16), nt, preferred_element_type=F32)
    gq_col = gq_ref[...] * (ATTN_SCALE * LOG2E)
    for h in range(MLA_HEADS):
        sl = slice(h * HEAD_PAD, (h + 1) * HEAD_PAD)
        q_ref[sl, :] = _head_norm_rope_t(qf_t[sl, :], gq_col, cos_t, sinr_t)
    ckv = _rms(kv_c, gkva_ref[...])
    ckv_ref[...] = ckv
    _build_kv(ckv, kr_tile, cos_t, sinr_t, wk_ref, wv_ref, gk_ref[...], eye_ref, k_ref, v_ref)


def _qkv(z, tabs, p, seq_len, tm):
    t = z.shape[0]
    if tm > seq_len:
        tabs = [jnp.tile(a, (tm // seq_len, 1)) for a in tabs]
    tabs_t = [a.T for a in tabs]
    per_seq = max(seq_len // tm, 1)
    tab_t_spec = pl.BlockSpec((LANE, tm), lambda i: (0, i % per_seq))
    full = lambda shape: pl.BlockSpec(shape, lambda i: (0,) * len(shape))
    return pl.pallas_call(
        _qkv_kernel,
        grid=(t // tm,),
        in_specs=[
            pl.BlockSpec((tm, QKV_W), lambda i: (i, Z_QKV // QKV_W)),
            tab_t_spec, tab_t_spec,
            full((1, Q_LORA)), full((HP, Q_LORA)), full((1, KV_LORA)),
            full((HP, KV_LORA)), full((KV_LORA, HP)), full((LANE, 1)), full((LANE, 1)), full((LANE, LANE)),
        ],
        out_specs=[
            pl.BlockSpec((HP, tm), lambda i: (0, i)),
            pl.BlockSpec((HP, tm), lambda i: (0, i)),
            pl.BlockSpec((tm, HP), lambda i: (i, 0)),
            pl.BlockSpec((tm, KV_LORA), lambda i: (i, 0)),
        ],
        out_shape=[
            jax.ShapeDtypeStruct((HP, t), BF16),
            jax.ShapeDtypeStruct((HP, t), BF16),
            jax.ShapeDtypeStruct((t, HP), BF16),
            jax.ShapeDtypeStruct((t, KV_LORA), F32),
        ],
        compiler_params=_cparams(("arbitrary",)),
        name="qkv",
    )(z, tabs_t[0], tabs_t[1], p["g_qa"], p["wq_t"], p["g_kva"], p["wk_t"], p["wv"],
      p["gq_col"], p["gk_col"], p["eye"])


def _kvcache_kernel(ckv_ref, kr_ref, cos_t_ref, sinr_t_ref, wk_ref, wv_ref, gk_ref, eye_ref, k_ref, v_ref):
    _build_kv(ckv_ref[...], kr_ref[...], cos_t_ref[...], sinr_t_ref[...], wk_ref, wv_ref, gk_ref[...], eye_ref,
              k_ref, v_ref)


def _kvcache(ckv, kr_tile, tabs, p):
    t = ckv.shape[0]
    return pl.pallas_call(
        _kvcache_kernel,
        out_shape=[jax.ShapeDtypeStruct((HP, t), BF16), jax.ShapeDtypeStruct((t, HP), BF16)],
        compiler_params=_cparams(None),
        name="kvcache",
    )(ckv, kr_tile, tabs[0].T, tabs[1].T, p["wk_t"], p["wv"], p["gk_col"], p["eye"])


def _fourier_kernel(nb, u_ref, cc_ref, sc_ref, cl_ref, sl_ref, o_ref, a_scr, b_scr):
    @pl.when(pl.program_id(1) == 0)
    def _():
        for b in range(nb):
            u = u_ref[b].astype(BF16)
            a_scr[:, b * FNET_W:(b + 1) * FNET_W] = _bdot(u, cc_ref[...]).astype(BF16)
            b_scr[:, b * FNET_W:(b + 1) * FNET_W] = _bdot(u, sc_ref[...]).astype(BF16)

    y = _bdot(cl_ref[...], a_scr[...]) - _bdot(sl_ref[...], b_scr[...])
    for b in range(nb):
        o_ref[b] = y[:, b * FNET_W:(b + 1) * FNET_W].astype(BF16)


def _fourier_split_kernel(nb, u_ref, cc_ref, sc_ref, ca_ref, sa_ref, cb_ref, sb_ref, o_ref,
                          a_scr, b_scr, cl_scr, sl_scr):
    @pl.when(pl.program_id(1) == 0)
    def _():
        for b in range(nb):
            u = u_ref[b].astype(BF16)
            a_scr[:, b * FNET_W:(b + 1) * FNET_W] = _bdot(u, cc_ref[...]).astype(BF16)
            b_scr[:, b * FNET_W:(b + 1) * FNET_W] = _bdot(u, sc_ref[...]).astype(BF16)

    cb, sb = cb_ref[...], sb_ref[...]
    for j in range(ca_ref.shape[0]):
        ca, sa = ca_ref[j:j + 1, :], sa_ref[j:j + 1, :]
        cl_scr[j * DFT_SPLIT:(j + 1) * DFT_SPLIT, :] = (ca * cb - sa * sb).astype(BF16)
        sl_scr[j * DFT_SPLIT:(j + 1) * DFT_SPLIT, :] = (sa * cb + ca * sb).astype(BF16)
    y = _bdot(cl_scr[...], a_scr[...]) - _bdot(sl_scr[...], b_scr[...])
    for b in range(nb):
        o_ref[b] = y[:, b * FNET_W:(b + 1) * FNET_W].astype(BF16)


def _fourier_split(z3, mats, nb, tq):
    s, l, _ = z3.shape
    cc, sc, ca, sa, cb, sb = mats
    full = lambda a: pl.BlockSpec(a.shape, lambda i, r: (0, 0))
    rows = pl.BlockSpec((tq // DFT_SPLIT, l), lambda i, r: (r, 0))
    return pl.pallas_call(
        functools.partial(_fourier_split_kernel, nb),
        grid=(s // nb, l // tq),
        in_specs=[pl.BlockSpec((nb, l, FNET_W), lambda i, r: (i, 0, 0)), full(cc), full(sc), rows, rows,
                  full(cb), full(sb)],
        out_specs=pl.BlockSpec((nb, tq, FNET_W), lambda i, r: (i, r, 0)),
        out_shape=jax.ShapeDtypeStruct((s, l, FNET_W), BF16),
        scratch_shapes=[pltpu.VMEM((l, nb * FNET_W), BF16), pltpu.VMEM((l, nb * FNET_W), BF16),
                        pltpu.VMEM((tq, l), BF16), pltpu.VMEM((tq, l), BF16)],
        compiler_params=_cparams(("arbitrary", "arbitrary")),
        name="fourier_split",
    )(z3, cc, sc, ca, sa, cb, sb)


def _fourier(z3, mats, nb, tq):
    s, l, _ = z3.shape
    if len(mats) == 6:
        return _fourier_split(z3, mats, nb, tq)
    cc, sc, cl, sl = mats
    return pl.pallas_call(
        functools.partial(_fourier_kernel, nb),
        grid=(s // nb, l // tq),
        in_specs=[
            pl.BlockSpec((nb, l, FNET_W), lambda i, r: (i, 0, 0)),
            pl.BlockSpec((FNET_W, FNET_W), lambda i, r: (0, 0)),
            pl.BlockSpec((FNET_W, FNET_W), lambda i, r: (0, 0)),
            pl.BlockSpec((tq, l), lambda i, r: (r, 0)),
            pl.BlockSpec((tq, l), lambda i, r: (r, 0)),
        ],
        out_specs=pl.BlockSpec((nb, tq, FNET_W), lambda i, r: (i, r, 0)),
        out_shape=jax.ShapeDtypeStruct((s, l, FNET_W), BF16),
        scratch_shapes=[pltpu.VMEM((l, nb * FNET_W), BF16), pltpu.VMEM((l, nb * FNET_W), BF16)],
        compiler_params=_cparams(("arbitrary", "arbitrary")),
        name="fourier",
    )(z3, cc, sc, cl, sl)


def _load_ext(ref, c, rows, nchunks):
    t0 = pl.multiple_of(c * rows, rows)
    main = ref[pl.ds(t0, rows), :]
    lo = pl.multiple_of(jnp.maximum(t0 - SUB, 0), SUB)
    hi = pl.multiple_of(jnp.minimum(t0 + rows, (nchunks - 1) * rows + rows - SUB), SUB)
    prev = jnp.where(c > 0, ref[pl.ds(lo, SUB), :], 0.0)
    nxt = jnp.where(c < nchunks - 1, ref[pl.ds(hi, SUB), :], 0.0)
    return jnp.concatenate([prev, main, nxt], axis=0)


def _shifted(ext, off, rows):
    n = ext.shape[0]
    r = ext if off == 0 else pltpu.roll(ext, (-off) % n, 0)
    return r[SUB:SUB + rows]


def _gelu_tanh(x):
    return 0.5 * x * (1.0 + jnp.tanh(math.sqrt(2.0 / math.pi) * (x + 0.044715 * (x * x * x))))


def _lru_gates(xc, d, wa_ref, wx_ref, ba_ref, bx_ref, lam_ref):
    xb = xc.astype(BF16)
    r = jax.nn.sigmoid(_bdot(xb, wa_ref[d]) + ba_ref[d])
    i = jax.nn.sigmoid(_bdot(xb, wx_ref[d]) + bx_ref[d])
    nl = -lam_ref[d]
    softplus = jnp.maximum(nl, 0.0) + jnp.log1p(jnp.exp(-jnp.abs(nl)))
    la = (-LRU_C) * r * softplus
    a = jnp.exp(la)
    one_m_a2 = -jnp.tanh(la) * (a * a + 1.0)
    b = jnp.sqrt(one_m_a2) * (i * xc)
    return a, b


def _group_scan(a, b, reverse):
    rows = a.shape[0]
    rm = lax.broadcasted_iota(jnp.int32, a.shape, 0) & (SUB - 1)
    for s in (1, 2, 4):
        if reverse:
            sh, m = rows - s, rm + s <= SUB - 1
        else:
            sh, m = s, rm >= s
        a_sh = pltpu.roll(a, sh, 0)
        b_sh = pltpu.roll(b, sh, 0)
        b = jnp.where(m, a * b_sh + b, b)
        a = jnp.where(m, a * a_sh, a)
    return a, b


def _carry_scan(a, b, carry, reverse):
    ng = a.shape[0] // SUB
    out = [None] * ng
    order = range(ng - 1, -1, -1) if reverse else range(ng)
    for g in order:
        hg = a[g * SUB:(g + 1) * SUB] * carry + b[g * SUB:(g + 1) * SUB]
        out[g] = hg
        carry = hg[0:1] if reverse else hg[SUB - 1:SUB]
    return jnp.concatenate(out, axis=0), carry


def _lru_kernel(nb, rows, nchunks, xl_ref, gl_ref, bs_ref, cs_ref, xs_ref, h0_ref, cw_ref, wa_ref, wx_ref,
                ba_ref, bx_ref, lam_ref, sw_ref, yl_ref, ys_ref, fin_ref, xc_scr, hf_scr, hb_scr):
    cw = cw_ref[...]
    sw = sw_ref[...]
    chunk = lambda c: pl.ds(pl.multiple_of(c * rows, rows), rows)
    seqs = range(nb)

    def conv_body(c, carry):
        for s in seqs:
            ext = _load_ext(xl_ref.at[s], c, rows, nchunks)
            xc = _shifted(ext, -2, rows) * cw[0:1]
            for k in range(1, 4):
                xc = xc + _shifted(ext, k - 2, rows) * cw[k:k + 1]
            xc_scr[s, chunk(c), :] = xc
        return carry

    lax.fori_loop(0, nchunks, conv_body, 0)

    def direction(s, c, d, carry, h_scr):
        a, b = _lru_gates(xc_scr[s, chunk(c), :], d, wa_ref, wx_ref, ba_ref, bx_ref, lam_ref)
        a, b = _group_scan(a, b, d == 1)
        h, carry = _carry_scan(a, b, carry, d == 1)
        h_scr[s, chunk(c), :] = h
        return carry

    def scan_body(j, carries):
        return tuple((direction(s, j, 0, cf, hf_scr), direction(s, nchunks - 1 - j, 1, cb, hb_scr))
                     for s, (cf, cb) in zip(seqs, carries))

    init = tuple((h0_ref[s, 0:1, :], h0_ref[s, 1:2, :]) for s in seqs)
    for s, (cf, cb) in zip(seqs, lax.fori_loop(0, nchunks, scan_body, init)):
        fin_ref[s, 0:1, :] = cf
        fin_ref[s, 1:2, :] = cb

    def out_body(c, carry):
        for s in seqs:
            y = (hf_scr[s, chunk(c), :] + hb_scr[s, chunk(c), :]) * _gelu_tanh(gl_ref[s, chunk(c), :])
            yl_ref[s, chunk(c), :] = y.astype(BF16)
            ext = _load_ext(cs_ref.at[s], c, rows, nchunks) * _load_ext(xs_ref.at[s], c, rows, nchunks)
            acc = _shifted(ext, -1, rows) * sw[0:1]
            acc = acc + _shifted(ext, 0, rows) * sw[1:2]
            acc = acc + _shifted(ext, 1, rows) * sw[2:3]
            ys_ref[s, chunk(c), :] = (bs_ref[s, chunk(c), :] * acc).astype(BF16)
        return carry

    lax.fori_loop(0, nchunks, out_body, 0)


def _lru(z3, h0, p, nb):
    s, l, _ = z3.shape
    rows = min(l, 1024)
    nchunks = l // rows
    half = lambda blk: pl.BlockSpec((nb, l, LANE), lambda i, j: (i, 0, blk + j))
    wspec = lambda shape: pl.BlockSpec((None,) + shape, lambda i, j: (j,) + (0,) * len(shape))
    seq_half = pl.BlockSpec((nb, l, LANE), lambda i, j: (i, 0, j))
    state = pl.BlockSpec((nb, 2, LANE), lambda i, j: (i, 0, j))
    return pl.pallas_call(
        functools.partial(_lru_kernel, nb, rows, nchunks),
        grid=(s // nb, 2),
        in_specs=[
            half(2), half(4), half(6), half(8), half(10), state,
            wspec((4, LANE)), wspec((2, LANE, LANE)), wspec((2, LANE, LANE)),
            wspec((2, 1, LANE)), wspec((2, 1, LANE)), wspec((2, 1, LANE)), wspec((3, LANE)),
        ],
        out_specs=[seq_half, seq_half, state],
        out_shape=[jax.ShapeDtypeStruct((s, l, LRU_W), BF16), jax.ShapeDtypeStruct((s, l, SC_W), BF16),
                   jax.ShapeDtypeStruct((s, 2, LRU_W), F32)],
        scratch_shapes=[pltpu.VMEM((nb, l, LANE), F32)] * 3,
        compiler_params=_cparams(("arbitrary", "arbitrary")),
        name="lru",
    )(z3, z3, z3, z3, z3, h0, p["lru_conv"], p["lru_wa"], p["lru_wx"], p["lru_ba"], p["lru_bx"], p["lru_lam"],
      p["sc_conv"])


def _store_head_pair(o_ref, rows, j, o_even, o_odd):
    lane = lax.broadcasted_iota(jnp.int32, o_even.shape, 1)
    pair = jnp.where(lane < V_DIM, o_even, pltpu.roll(o_odd, V_DIM, 1))
    o_ref[rows, j * LANE:(j + 1) * LANE] = pair.astype(BF16)


def _attn_kernel(hp, sub, has_cache, *refs):
    if has_cache:
        q_ref, k_ref, v_ref, kc_ref, vc_ref, o_ref, s_scr, p_scr, m_scr, sc_scr, pc_scr = refs
    else:
        q_ref, k_ref, v_ref, o_ref, s_scr, p_scr, m_scr = refs
    tn = (((0,), (0,)), ((), ()))
    items = [(slice(r * sub, (r + 1) * sub), slice(h * HEAD_PAD, (h + 1) * HEAD_PAD))
             for r in range(q_ref.shape[1] // sub) for h in range(hp)]
    n = len(items)

    def scores(i):
        rows, head = items[i]
        q_t = q_ref[head, rows]
        s = lax.dot_general(q_t, k_ref[head, :], tn, preferred_element_type=F32)
        m = jnp.max(s, axis=-1, keepdims=True)
        s_scr[i % 2] = s
        if has_cache:
            sc = lax.dot_general(q_t, kc_ref[head, :], tn, preferred_element_type=F32)
            m = jnp.maximum(m, jnp.max(sc, axis=-1, keepdims=True))
            sc_scr[i % 2] = sc
        m_scr[i % 2] = jnp.broadcast_to(m, m_scr.shape[1:])

    def probs(i):
        m = m_scr[i % 2][:, 0:1]
        p_scr[i % 2] = jnp.exp2(s_scr[i % 2] - m).astype(BF16)
        if has_cache:
            pc_scr[i % 2] = jnp.exp2(sc_scr[i % 2] - m).astype(BF16)

    pending = {}

    def weighted_values(i):
        rows, head = items[i]
        o = _bdot(p_scr[i % 2], v_ref[:, head])
        if has_cache:
            o = o + _bdot(pc_scr[i % 2], vc_ref[:, head])
        o = o / o[:, DEN_LANE:DEN_LANE + 1]
        h = i % hp
        if h % 2 == 0:
            pending[0] = o
        else:
            _store_head_pair(o_ref, rows, h // 2, pending.pop(0), o)

    scores(0)
    for i in range(n):
        if i + 1 < n:
            scores(i + 1)
        if i > 0:
            weighted_values(i - 1)
        probs(i)
    weighted_values(n - 1)


def _attn_short_kernel(hp, q_ref, k_ref, v_ref, o_ref):
    tn = (((0,), (0,)), ((), ()))
    heads = [slice(h * HEAD_PAD, (h + 1) * HEAD_PAD) for h in range(hp)]
    scores = [lax.dot_general(q_ref[hd, :], k_ref[hd, :], tn, preferred_element_type=F32) for hd in heads]
    probs = [jnp.exp2(s - jnp.max(s, axis=-1, keepdims=True)).astype(BF16) for s in scores]
    outs = []
    for hd, p in zip(heads, probs):
        o = _bdot(p, v_ref[:, hd])
        outs.append(o / o[:, DEN_LANE:DEN_LANE + 1])
    for j in range(hp // 2):
        _store_head_pair(o_ref, slice(None), j, outs[2 * j], outs[2 * j + 1])


def _attn(q, k, v, cache, hp, tq):
    s, l, _ = v.shape
    w = hp * HEAD_PAD
    qspec = pl.BlockSpec((w, tq), lambda i, h, r: (h, i * (l // tq) + r))
    ospec = pl.BlockSpec((None, tq, hp * V_DIM), lambda i, h, r: (i, r, h))
    if cache is None and tq == l:
        kspec = pl.BlockSpec((w, l), lambda i, h, r: (h, i))
        vspec = pl.BlockSpec((None, l, w), lambda i, h, r: (i, 0, h))
        return pl.pallas_call(
            functools.partial(_attn_short_kernel, hp),
            grid=(s, MLA_HEADS // hp, 1),
            in_specs=[qspec, kspec, vspec],
            out_specs=ospec,
            out_shape=jax.ShapeDtypeStruct((s, l, MLA_OUT), BF16),
            compiler_params=_cparams(("arbitrary", "arbitrary", "arbitrary")),
            name="attn_short",
        )(q, k, v)
    kv_mode = dict(pipeline_mode=pl.Buffered(1))
    kspec = pl.BlockSpec((w, l), lambda i, h, r: (h, i), **kv_mode)
    vspec = pl.BlockSpec((None, l, w), lambda i, h, r: (i, 0, h), **kv_mode)
    in_specs = [qspec, kspec, vspec]
    args = [q, k, v]
    if cache is not None:
        lc = cache[1].shape[1]
        in_specs += [pl.BlockSpec((w, lc), lambda i, h, r: (h, i), **kv_mode),
                     pl.BlockSpec((None, lc, w), lambda i, h, r: (i, 0, h), **kv_mode)]
        args += list(cache)
    sub = min(tq, ATTN_SUB)
    scratch = [pltpu.VMEM((2, sub, l), F32), pltpu.VMEM((2, sub, l), BF16), pltpu.VMEM((2, sub, LANE), F32)]
    if cache is not None:
        scratch += [pltpu.VMEM((2, sub, lc), F32), pltpu.VMEM((2, sub, lc), BF16)]
    return pl.pallas_call(
        functools.partial(_attn_kernel, hp, sub, cache is not None),
        grid=(s, MLA_HEADS // hp, l // tq),
        in_specs=in_specs,
        out_specs=ospec,
        out_shape=jax.ShapeDtypeStruct((s, l, MLA_OUT), BF16),
        scratch_shapes=scratch,
        compiler_params=_cparams(("arbitrary", "arbitrary", "arbitrary")),
        name="attn",
    )(*args)


def _route(lt):
    row = lax.broadcasted_iota(jnp.int32, (SUB, lt.shape[1]), 0).astype(F32)
    neg = -jnp.inf
    gl = lt[0:SUB]
    g_ok = row < N_GROUPS
    glm = jnp.where(g_ok, gl, neg)
    gmax = jnp.max(glm, axis=0, keepdims=True)
    gsel = jnp.min(jnp.where(glm == gmax, row, float(SUB)), axis=0, keepdims=True)
    g_w = 1.0 / jnp.sum(jnp.where(g_ok, jnp.exp(gl - gmax), 0.0), axis=0, keepdims=True)
    es = lt[SUB:2 * SUB]
    for g in range(1, N_GROUPS):
        es = jnp.where(gsel == float(g), lt[(g + 1) * SUB:(g + 2) * SUB], es)
    m1 = jnp.max(es, axis=0, keepdims=True)
    i1 = jnp.min(jnp.where(es == m1, row, float(SUB)), axis=0, keepdims=True)
    es2 = jnp.where(row == i1, neg, es)
    m2 = jnp.max(es2, axis=0, keepdims=True)
    i2 = jnp.min(jnp.where(es2 == m2, row, float(SUB)), axis=0, keepdims=True)
    e2 = jnp.exp(m2 - m1)
    inv = g_w / (1.0 + e2)
    base = gsel * EXP_PER_GROUP
    return (jnp.where(row == RT_ID0, base + i1, 0.0) + jnp.where(row == RT_ID1, base + i2, 0.0)
            + jnp.where(row == RT_W0, inv, 0.0) + jnp.where(row == RT_W1, inv * e2, 0.0))


def _pack_bf16_pairs(x):
    k = x.shape[1] // 2
    bits = lax.bitcast_convert_type(x.astype(BF16).astype(F32), jnp.int32)
    return lax.shift_right_logical(bits[:, :k], 16) | (bits[:, k:] & HI_MASK)


def _unpack_bf16_pairs(w):
    lo = lax.bitcast_convert_type(lax.shift_left(w, 16), F32).astype(BF16)
    hi = lax.bitcast_convert_type(w & HI_MASK, F32).astype(BF16)
    return lo, hi


def _back_kernel(x_ref, mod_ref, g1_ref, g2_ref, yf_ref, yl_ref, ys_ref, o_ref,
                 wg_ref, wpf_ref, wpl_ref, wps_ref, wpm_ref, wo_ref, wrh_ref, wrl_ref, br_ref,
                 x1_ref, h2_ref, comb_ref):
    d = D_MODEL
    x = x_ref[...]
    sh1, sc1, gt1 = mod_ref[:, 0:d], mod_ref[:, d:2 * d], mod_ref[:, 2 * d:3 * d]
    sh2, sc2 = mod_ref[:, 3 * d:4 * d], mod_ref[:, 4 * d:5 * d]
    hb = (_rms(x, g1_ref[...]) * (1.0 + sc1) + sh1).astype(BF16)
    branches = ((yf_ref, wpf_ref), (yl_ref, wpl_ref), (ys_ref, wps_ref), (o_ref, wpm_ref))
    blocks = []
    for n in range(0, d, MERGE_BLK):
        acc = None
        for j, (y_ref, w_ref) in enumerate(branches):
            gate = jax.nn.sigmoid(_bdot(hb, wg_ref[:, j * d + n:j * d + n + MERGE_BLK]))
            term = gate * _bdot(y_ref[...], w_ref[:, n:n + MERGE_BLK])
            acc = term if acc is None else acc + term
        blocks.append(acc.astype(BF16))
    x1 = x + gt1 * _bdot(jnp.concatenate(blocks, axis=1), wo_ref[...])
    x1_ref[...] = x1
    h2 = _rms(x1, g2_ref[...]) * (1.0 + sc2) + sh2
    h2b = h2.astype(BF16)
    h2_ref[...] = _pack_bf16_pairs(h2)
    h2l = (h2 - h2b.astype(F32)).astype(BF16)
    nt = (((1,), (1,)), ((), ()))
    rdot = lambda w_ref, h: lax.dot_general(w_ref[...], h, nt, preferred_element_type=F32)
    logits_t = rdot(wrh_ref, h2b) + rdot(wrh_ref, h2l) + rdot(wrl_ref, h2b) + br_ref[...]
    comb_ref[...] = _route(logits_t)


def _back(x, mod, yf, yl, ys, o, p, tm):
    t = x.shape[0]
    per_mod = t // (mod.shape[0] * tm)
    row = lambda w: pl.BlockSpec((tm, w), lambda i: (i, 0))
    full = lambda a: pl.BlockSpec(a.shape, lambda i: (0,) * a.ndim, pipeline_mode=pl.Buffered(1))
    weights = [p["w_gate"], p["w_pf"], p["w_pl"], p["w_ps"], p["w_pm"], p["w_out"], p["wr_hi"], p["wr_lo"], p["b_r"]]
    return pl.pallas_call(
        _back_kernel,
        grid=(t // tm,),
        in_specs=[
            row(D_MODEL),
            pl.BlockSpec((None, 1, 6 * D_MODEL), lambda i: (i // per_mod, 0, 0)),
            pl.BlockSpec((1, D_MODEL), lambda i: (0, 0)),
            pl.BlockSpec((1, D_MODEL), lambda i: (0, 0)),
            row(FNET_W), row(LRU_W), row(SC_W), row(MLA_OUT),
        ] + [full(w) for w in weights],
        out_specs=[row(D_MODEL), row(PACK_W), pl.BlockSpec((SUB, tm), lambda i: (0, i))],
        out_shape=[
            jax.ShapeDtypeStruct((t, D_MODEL), F32),
            jax.ShapeDtypeStruct((t, PACK_W), jnp.int32),
            jax.ShapeDtypeStruct((SUB, t), F32),
        ],
        compiler_params=_cparams(("arbitrary",)),
        name="back",
    )(x, mod, p["norm1_g"], p["norm2_g"], yf, yl, ys, o, *weights)


def _plan_kernel(tm, rt_ref, pos_ref, te_ref, nu_ref):
    ids = jnp.concatenate([rt_ref[RT_ID0:RT_ID0 + 1, :], rt_ref[RT_ID1:RT_ID1 + 1, :]], axis=1)
    n_rows = ids.shape[1]
    erow = lax.broadcasted_iota(jnp.int32, (N_EXPERTS, PLAN_BLK), 0).astype(F32)
    col = lax.broadcasted_iota(jnp.int32, (N_EXPERTS, LANE), 0)
    blocks = [(ids[:, b * PLAN_BLK:(b + 1) * PLAN_BLK] == erow).astype(F32) for b in range(n_rows // PLAN_BLK)]
    tri = (lax.broadcasted_iota(jnp.int32, (PLAN_BLK, PLAN_BLK), 0)
           <= lax.broadcasted_iota(jnp.int32, (PLAN_BLK, PLAN_BLK), 1)).astype(BF16)
    prefix = [_bdot(blk.astype(BF16), tri) for blk in blocks]
    counts = prefix[0][:, PLAN_BLK - 1:PLAN_BLK]
    for pre in prefix[1:]:
        counts = counts + pre[:, PLAN_BLK - 1:PLAN_BLK]
    tiles = jnp.floor((counts + float(tm - 1)) * (1.0 / tm))
    tile_end = jnp.broadcast_to(tiles, (N_EXPERTS, LANE))
    for s in (1, 2, 4, 8, 16):
        tile_end = tile_end + jnp.where(col >= s, pltpu.roll(tile_end, s, 0), 0.0)
    base = (tile_end[:, 0:1] - tiles) * float(tm) - 1.0
    for b, (blk, pre) in enumerate(zip(blocks, prefix)):
        pos = jnp.sum(blk * (base + pre), axis=0, keepdims=True)
        pos_ref[:, b * PLAN_BLK:(b + 1) * PLAN_BLK] = pos.astype(jnp.int32)
        base = base + pre[:, PLAN_BLK - 1:PLAN_BLK]
    k = lax.broadcasted_iota(jnp.int32, (N_EXPERTS, LANE), 1).astype(F32)
    owner = jnp.sum((k >= tile_end).astype(F32), axis=0, keepdims=True)
    te_ref[...] = jnp.minimum(owner, float(N_EXPERTS - 1)).astype(jnp.int32)
    nu_ref[...] = jnp.broadcast_to(tile_end[N_EXPERTS - 1:N_EXPERTS, :], nu_ref.shape).astype(jnp.int32)


def _dispatch_plan(rt, tm):
    n_rows = TOP_K * rt.shape[1]
    n_tiles = n_rows // tm + N_EXPERTS
    assert n_tiles <= LANE and tm & (tm - 1) == 0 and n_rows % PLAN_BLK == 0
    pos, te, nu = pl.pallas_call(
        functools.partial(_plan_kernel, tm),
        out_shape=[jax.ShapeDtypeStruct((1, n_rows), jnp.int32), jax.ShapeDtypeStruct((1, LANE), jnp.int32),
                   jax.ShapeDtypeStruct((1, LANE), jnp.int32)],
        compiler_params=_cparams(None),
        name="plan",
    )(rt)
    return pos.reshape(n_rows), te[0, :n_tiles], nu[0, :1], n_tiles


def _sc_worker_rows(n_rows):
    workers = SC_CORES * SC_SUBCORES
    per_w = n_rows // workers
    assert per_w * workers == n_rows and per_w % SC_CHUNK == 0
    return per_w


def _sc_mesh():
    return plsc.VectorSubcoreMesh(core_axis_name="c", subcore_axis_name="s",
                                  num_cores=SC_CORES, num_subcores=SC_SUBCORES)


def _sc_scatter_rows(src, pos, n_out):
    t, w = src.shape
    per_w = _sc_worker_rows(pos.shape[0])

    @functools.partial(
        pl.kernel, mesh=_sc_mesh(), out_type=jax.ShapeDtypeStruct((n_out, w), src.dtype),
        scratch_types=[pltpu.VMEM((SC_CHUNK,), jnp.int32), pltpu.VMEM((SC_CHUNK, w), src.dtype)],
        name="moe_dispatch")
    def run(src_hbm, pos_hbm, out_hbm, idx_v, rows_v):
        wid = lax.axis_index("s") * SC_CORES + lax.axis_index("c")

        @pl.loop(0, per_w // SC_CHUNK)
        def _(i):
            j0 = wid * per_w + i * SC_CHUNK
            pltpu.sync_copy(pos_hbm.at[pl.ds(j0, SC_CHUNK)], idx_v)
            pltpu.sync_copy(src_hbm.at[pl.ds(lax.rem(j0, t), SC_CHUNK)], rows_v)
            pltpu.sync_copy(rows_v, out_hbm.at[idx_v])

    return run(src, pos)


def _sc_gather_rows(src, pos):
    w = src.shape[1]
    n = pos.shape[0]
    per_w = _sc_worker_rows(n)

    @functools.partial(
        pl.kernel, mesh=_sc_mesh(), out_type=jax.ShapeDtypeStruct((n, w), src.dtype),
        scratch_types=[pltpu.VMEM((SC_CHUNK,), jnp.int32), pltpu.VMEM((SC_CHUNK, w), src.dtype)],
        name="moe_return")
    def run(src_hbm, pos_hbm, out_hbm, idx_v, rows_v):
        wid = lax.axis_index("s") * SC_CORES + lax.axis_index("c")

        @pl.loop(0, per_w // SC_CHUNK)
        def _(i):
            j0 = wid * per_w + i * SC_CHUNK
            pltpu.sync_copy(pos_hbm.at[pl.ds(j0, SC_CHUNK)], idx_v)
            pltpu.sync_copy(src_hbm.at[idx_v], rows_v)
            pltpu.sync_copy(rows_v, out_hbm.at[pl.ds(j0, SC_CHUNK)])

    return run(src, pos)


def _experts_kernel(te_ref, nu_ref, x_ref, w1_ref, w3_ref, w2_ref, y_ref):
    half = D_MODEL // 2

    @pl.when(pl.program_id(0) < nu_ref[0])
    def _():
        lo, hi = _unpack_bf16_pairs(x_ref[...])
        a = _bdot(lo, w1_ref[:half, :].astype(BF16)) + _bdot(hi, w1_ref[half:, :].astype(BF16))
        u = _bdot(lo, w3_ref[:half, :].astype(BF16)) + _bdot(hi, w3_ref[half:, :].astype(BF16))
        mid = (a * jax.nn.sigmoid(a)) * u
        y_ref[...] = _pack_bf16_pairs(_bdot(mid.astype(BF16), w2_ref[...].astype(BF16)))


def _experts(xs, tile_expert, n_used, n_tiles, w1, w3, w2, layer, tm):
    wspec = lambda a: pl.BlockSpec((None, None) + a.shape[2:], lambda i, te, nu: (layer, te[i], 0, 0))
    grid_spec = pltpu.PrefetchScalarGridSpec(
        num_scalar_prefetch=2,
        grid=(n_tiles,),
        in_specs=[pl.BlockSpec((tm, PACK_W), lambda i, te, nu: (jnp.minimum(i, nu[0] - 1), 0)),
                  wspec(w1), wspec(w3), wspec(w2)],
        out_specs=pl.BlockSpec((tm, PACK_W), lambda i, te, nu: (jnp.where(i < nu[0], i, n_tiles), 0)),
    )
    return pl.pallas_call(
        _experts_kernel,
        grid_spec=grid_spec,
        out_shape=jax.ShapeDtypeStruct((xs.shape[0] + tm, PACK_W), jnp.int32),
        compiler_params=_cparams(("arbitrary",)),
        name="experts",
    )(tile_expert, n_used, xs, w1, w3, w2)


def _combine_kernel(x1_ref, mod_ref, rt_ref, g0_ref, g1_ref, o_ref):
    rt = rt_ref[...]
    lo0, hi0 = _unpack_bf16_pairs(g0_ref[...])
    lo1, hi1 = _unpack_bf16_pairs(g1_ref[...])
    w0 = rt[:, RT_W0:RT_W0 + 1]
    w1 = rt[:, RT_W1:RT_W1 + 1]
    half = D_MODEL // 2
    gt2 = mod_ref[:, 5 * D_MODEL:6 * D_MODEL]
    o_ref[:, :half] = x1_ref[:, :half] + gt2[:, :half] * (w0 * lo0.astype(F32) + w1 * lo1.astype(F32))
    o_ref[:, half:] = x1_ref[:, half:] + gt2[:, half:] * (w0 * hi0.astype(F32) + w1 * hi1.astype(F32))


def _combine(x1, mod, rt, g, tm):
    t = x1.shape[0]
    per_mod = t // (mod.shape[0] * tm)
    return pl.pallas_call(
        _combine_kernel,
        grid=(t // tm,),
        in_specs=[
            pl.BlockSpec((tm, D_MODEL), lambda i: (i, 0)),
            pl.BlockSpec((None, 1, 6 * D_MODEL), lambda i: (i // per_mod, 0, 0)),
            pl.BlockSpec((tm, LANE), lambda i: (i, 0)),
            pl.BlockSpec((tm, PACK_W), lambda i: (i, 0)),
            pl.BlockSpec((tm, PACK_W), lambda i: (i + t // tm, 0)),
        ],
        out_specs=pl.BlockSpec((tm, D_MODEL), lambda i: (i, 0)),
        out_shape=jax.ShapeDtypeStruct((t, D_MODEL), F32),
        compiler_params=_cparams(("arbitrary",)),
        name="combine",
    )(x1, mod, rt, g, g)


def _moe(h2p, rt, x1, mod, p, tm_e, tm_c):
    pos, tile_expert, n_used, n_tiles = _dispatch_plan(rt, tm_e)
    xs = _sc_scatter_rows(h2p, pos, n_tiles * tm_e)
    ys = _experts(xs, tile_expert, n_used, n_tiles, p["w1"], p["w3"], p["w2"], p["layer"], tm_e)
    g = _sc_gather_rows(ys, pos)
    rt_cols = jnp.pad(rt.T, ((0, 0), (0, LANE - rt.shape[0])))
    return _combine(x1, mod, rt_cols, g, tm_c)


def _channel_dft():
    n = np.arange(FNET_W // FNET_GROUPS)
    ang = 2.0 * np.pi * ((n[:, None] * n[None, :]) % n.size) / n.size
    eye = np.eye(FNET_GROUPS)
    scale = 1.0 / math.sqrt(n.size)
    return (jnp.asarray(np.kron(eye, np.cos(ang) * scale), F32).astype(BF16),
            jnp.asarray(np.kron(eye, np.sin(ang) * scale), F32).astype(BF16))


def _position_dft(l):
    scale = 1.0 / math.sqrt(l)
    if l <= 256:
        n = np.arange(l)
        ang = 2.0 * np.pi * ((n[:, None] * n[None, :]) % l) / l
        return jnp.asarray(np.cos(ang) * scale, F32).astype(BF16), jnp.asarray(np.sin(ang) * scale, F32).astype(BF16)
    m = DFT_SPLIT
    n = np.arange(l)
    ang_a = 2.0 * np.pi * ((np.arange(l // m)[:, None] * m * n[None, :]) % l) / l
    ang_b = 2.0 * np.pi * ((np.arange(m)[:, None] * n[None, :]) % l) / l
    return (jnp.asarray(np.cos(ang_a), F32), jnp.asarray(np.sin(ang_a), F32),
            jnp.asarray(np.cos(ang_b) * scale, F32), jnp.asarray(np.sin(ang_b) * scale, F32))


def _rope_tables(l, rotate):
    cos = np.ones((l, LANE))
    sinr = np.zeros((l, LANE))
    if rotate:
        t = np.arange(l)
        inv = ROPE_BASE ** (-np.arange(N_FREQ) / N_FREQ)
        ang = np.concatenate([(t // GRID_W)[:, None] * inv, (t % GRID_W)[:, None] * inv], axis=-1)
        half = ROPE // 2
        for lo in (ROPE_X1, ROPE_X2):
            cos[:, lo:lo + half] = np.cos(ang)
        sinr[:, ROPE_X1:ROPE_X1 + half] = -np.sin(ang)
        sinr[:, ROPE_X2:ROPE_X2 + half] = np.sin(ang)
    return tuple(jnp.asarray(a, F32) for a in (cos, sinr))


def _head_lane_source():
    half = ROPE // 2
    src = np.full((HEAD_PAD,), QK_DIM, np.int32)
    src[ROPE_X1:ROPE_X1 + half] = NOPE + np.arange(half)
    src[ROPE_X2:ROPE_X2 + half] = NOPE + half + np.arange(half)
    free = [i for i in range(HEAD_PAD) if src[i] == QK_DIM][:NOPE]
    src[free] = np.arange(NOPE)
    return src


def _place_head_dims(w, n_src):
    src = _head_lane_source()
    src = np.where(src < n_src, src, n_src)
    wz = jnp.concatenate([w[..., :n_src], jnp.zeros(w.shape[:-1] + (1,), w.dtype)], axis=-1)
    out = jnp.take(wz, jnp.asarray(src), axis=-1)
    return out.reshape(out.shape[:-2] + (out.shape[-2] * HEAD_PAD,))


def _place_rope_key(kr):
    half = ROPE // 2
    out = jnp.zeros(kr.shape[:-1] + (LANE,), kr.dtype)
    return out.at[..., ROPE_X1:ROPE_X1 + half].set(kr[..., :half]).at[..., ROPE_X2:ROPE_X2 + half].set(kr[..., half:])


def _take_rope_key(tile):
    half = ROPE // 2
    return jnp.concatenate([tile[..., ROPE_X1:ROPE_X1 + half], tile[..., ROPE_X2:ROPE_X2 + half]], axis=-1)


def _pad_heads(w, lo, hi):
    r = w.shape[0]
    part = w[:, :, lo:hi]
    out = jnp.zeros((r, MLA_HEADS, HEAD_PAD), w.dtype).at[:, :, : hi - lo].set(part)
    return out.reshape(r, HP)


def _blockdiag_halves(w):
    bw = LRU_W // LRU_BLOCKS
    out = jnp.zeros((2, 2, LANE, LANE), w.dtype)
    for half in range(2):
        for k in range(2):
            n = 2 * half + k
            out = out.at[half, :, k * bw:(k + 1) * bw, k * bw:(k + 1) * bw].set(w[:, n])
    return out


def _halves(v):
    return jnp.moveaxis(v.reshape(v.shape[:-1] + (2, LANE)), -2, 0)


def _layer_params(l, a):
    d = D_MODEL
    w_in = a["w_in"][l]
    w_in = jnp.concatenate([w_in[:, :Z_KR], _place_rope_key(w_in[:, Z_KR:])], axis=1)
    w_r = jnp.zeros((RT_ROWS, d), F32).at[:N_GROUPS].set(a["w_gr"][l].T).at[SUB:SUB + N_EXPERTS].set(a["w_er"][l].T)
    b_r = jnp.zeros((RT_ROWS, 1), F32).at[:N_GROUPS, 0].set(a["b_gr"][l]).at[SUB:SUB + N_EXPERTS, 0].set(a["b_er"][l])
    wr_hi = w_r.astype(BF16)
    gpad = lambda g: jnp.take(jnp.concatenate([g, jnp.zeros((1,), F32)]), jnp.asarray(_head_lane_source())).reshape(1, LANE)
    return {
        "norm1_g": a["norm1_g"][l].reshape(1, d), "norm2_g": a["norm2_g"][l].reshape(1, d),
        "w_in": w_in.astype(BF16),
        "lru_conv": _halves(a["lru_conv"][l]),
        "lru_wa": _blockdiag_halves(a["lru_wa"][l]).astype(BF16),
        "lru_wx": _blockdiag_halves(a["lru_wx"][l]).astype(BF16),
        "lru_ba": _halves(a["lru_ba"][l])[:, :, None, :], "lru_bx": _halves(a["lru_bx"][l])[:, :, None, :],
        "lru_lam": _halves(a["lru_lam"][l])[:, :, None, :],
        "sc_conv": _halves(a["sc_conv"][l]),
        "g_qa": a["g_qa"][l].reshape(1, Q_LORA), "g_kva": a["g_kva"][l].reshape(1, KV_LORA),
        "wq_t": _place_head_dims(a["w_qb"][l], QK_DIM).T.astype(BF16),
        "wk_t": _place_head_dims(a["w_kvb"][l], NOPE).T.astype(BF16),
        "wv": _pad_heads(a["w_kvb"][l], NOPE, NOPE + V_DIM).astype(BF16),
        "gq_col": gpad(a["g_qn"][l]).reshape(LANE, 1), "gk_col": gpad(a["g_kn"][l]).reshape(LANE, 1),
        "eye": jnp.eye(LANE, dtype=F32).astype(BF16),
        "w_gate": a["w_gate"][l].astype(BF16),
        "w_pf": a["w_pf"][l].astype(BF16), "w_pl": a["w_pl"][l].astype(BF16), "w_ps": a["w_ps"][l].astype(BF16),
        "w_pm": a["w_pm"][l].astype(BF16), "w_out": a["w_out"][l].astype(BF16),
        "wr_hi": wr_hi, "wr_lo": (w_r - wr_hi.astype(F32)).astype(BF16), "b_r": b_r,
        "w1": a["w1"], "w3": a["w3"], "w2": a["w2"], "layer": l,
    }


def _trunk_layer(x, mod, p, consts, h0, cache, cfg):
    s, l = cfg["s"], cfg["l"]
    z = _front(x, mod, p["norm1_g"], p["w_in"], cfg["tm_front"])
    z3 = z.reshape(s, l, Z_W)
    q, k, v, ckv = _qkv(z, consts["rope"], p, l, cfg["tm_qkv"])
    yf = _fourier(z3, consts["dft"], cfg["nb"], cfg["tq_f"])
    yl, ys, fin = _lru(z3, h0, p, cfg["nb_lru"])
    o = _attn(q, k, v.reshape(s, l, HP), cache, cfg["hp"], cfg["tq_a"])
    t = s * l
    x1, h2, comb = _back(x, mod, yf.reshape(t, FNET_W), yl.reshape(t, LRU_W), ys.reshape(t, SC_W),
                         o.reshape(t, MLA_OUT), p, cfg["tm_back"])
    x2 = _moe(h2, comb, x1, mod, p, cfg["tm_moe"], cfg["tm_comb"])
    return x2, z3, ckv, fin


def kernel(x_prompt, x_sample, cache_ckv, cache_krope, state_rglru, c, c_ctx, norm1_g, norm2_g, w_ada, b_ada, w_in, lru_conv, lru_wa, lru_ba, lru_wx, lru_bx, lru_lam, sc_conv, g_qa, w_qb, g_kva, w_kvb, g_qn, g_kn, w_pf, w_pl, w_ps, w_pm, w_gate, w_out, w_gr, b_gr, w_er, b_er, w1, w3, w2):
    a = dict(norm1_g=norm1_g, norm2_g=norm2_g, w_in=w_in, lru_conv=lru_conv, lru_wa=lru_wa, lru_ba=lru_ba,
             lru_wx=lru_wx, lru_bx=lru_bx, lru_lam=lru_lam, sc_conv=sc_conv, g_qa=g_qa, w_qb=w_qb, g_kva=g_kva,
             w_kvb=w_kvb, g_qn=g_qn, g_kn=g_kn, w_pf=w_pf, w_pl=w_pl, w_ps=w_ps, w_pm=w_pm, w_gate=w_gate,
             w_out=w_out, w_gr=w_gr, b_gr=b_gr, w_er=w_er, b_er=b_er, w1=w1, w3=w3, w2=w2)
    bc, lc, d = x_prompt.shape
    bl, ll, _ = x_sample.shape
    past = cache_ckv.shape[2]

    cond8 = jnp.zeros((SUB, d), F32).at[0].set(c_ctx).at[1:1 + bl].set(c)
    mod_all = _ada(cond8, w_ada, b_ada)

    cc, sc = _channel_dft()
    ctx_consts = {"dft": (cc, sc) + _position_dft(lc), "rope": _rope_tables(lc, False)}
    lat_consts = {"dft": (cc, sc) + _position_dft(ll), "rope": _rope_tables(ll, True)}
    cache_tabs = _rope_tables(bl * past, False)
    ctx_cfg = dict(s=bc, l=lc, tm_front=1024, tm_qkv=1024, nb=8, nb_lru=4, tq_f=lc, hp=MLA_HEADS, tq_a=lc, tm_back=1024, tm_moe=512, tm_comb=1024)
    lat_cfg = dict(s=bl, l=ll, tm_front=1024, tm_qkv=1024, nb=bl, nb_lru=1, tq_f=512, hp=MLA_HEADS, tq_a=256, tm_back=1024, tm_moe=512, tm_comb=1024)

    xp = x_prompt.reshape(bc * lc, d)
    xs = x_sample.reshape(bl * ll, d)
    h0_ctx = jnp.zeros((bc, 2, LRU_W), F32)
    ckv_list, kr_list, lru_list = [], [], []
    for l in range(DEPTH):
        p = _layer_params(l, a)
        mod_ctx = mod_all[l, 0:1].reshape(1, 1, 6 * d)
        mod_lat = mod_all[l, 1:1 + bl].reshape(bl, 1, 6 * d)
        xp, z3, ckv, fin = _trunk_layer(xp, mod_ctx, p, ctx_consts, h0_ctx, None, ctx_cfg)
        ckv_list.append(ckv.reshape(bc, lc, KV_LORA))
        kr_list.append(_take_rope_key(z3[:, :, Z_KR:]))
        lru_list.append(fin)
        kr_tile = _place_rope_key(cache_krope[:, l].reshape(bl * past, ROPE))
        kc, vc = _kvcache(cache_ckv[:, l].reshape(bl * past, KV_LORA), kr_tile, cache_tabs, p)
        cache = (kc, vc.reshape(bl, past, HP))
        xs, _, _, _ = _trunk_layer(xs, mod_lat, p, lat_consts, state_rglru[:, l], cache, lat_cfg)
    return (xp.reshape(bc, lc, d), xs.reshape(bl, ll, d),
            jnp.stack(ckv_list, axis=1), jnp.stack(kr_list, axis=1), jnp.stack(lru_list, axis=1))
```

```python
import functools
import math

import numpy as np
import jax
import jax.numpy as jnp
from jax import lax
from jax.experimental import pallas as pl
from jax.experimental.pallas import tpu as pltpu
from jax.experimental.pallas import tpu_sc as plsc

F32 = jnp.float32
BF16 = jnp.bfloat16

D_MODEL = 1024
DEPTH = 2
GRID_W = 64
EPS = 1e-6
FNET_W = 256
FNET_GROUPS = 4
LRU_W = 256
LRU_BLOCKS = 4
LRU_C = 8.0
SC_W = 256
MLA_HEADS = 8
Q_LORA = 384
KV_LORA = 256
NOPE = 64
ROPE = 32
V_DIM = 64
QK_DIM = NOPE + ROPE
MLA_OUT = MLA_HEADS * V_DIM
N_FREQ = ROPE // 4
ROPE_BASE = 10000.0
ATTN_SCALE = QK_DIM ** -0.5
LOG2E = math.log2(math.e)
N_GROUPS = 4
EXP_PER_GROUP = 8
N_EXPERTS = N_GROUPS * EXP_PER_GROUP

LANE = 128
SUB = 8
HEAD_PAD = LANE
HP = MLA_HEADS * HEAD_PAD
ROPE_X1 = 0
ROPE_X2 = LANE // 2
DEN_LANE = V_DIM
Z_QKV = 6 * 256
Z_KR = Z_QKV + Q_LORA + KV_LORA
Z_W = Z_KR + LANE
QKV_W = Q_LORA + KV_LORA + LANE
VMEM_LIMIT = 52 * 1024 * 1024
PACK_W = D_MODEL // 2
HI_MASK = -65536
RT_ID0, RT_ID1, RT_W0, RT_W1 = 0, 1, 2, 3
RT_ROWS = 48
TOP_K = 2
MERGE_BLK = 256
PLAN_BLK = 256
DFT_SPLIT = 64
ATTN_SUB = 256
SC_CORES = 2
SC_SUBCORES = 16
SC_CHUNK = 128


def _cparams(sem):
    return pltpu.CompilerParams(dimension_semantics=sem, vmem_limit_bytes=VMEM_LIMIT)


def _bdot(a, b):
    return jnp.dot(a, b, preferred_element_type=F32)


def _rms(x, g):
    return x * lax.rsqrt(jnp.mean(x * x, axis=-1, keepdims=True) + EPS) * g


def _ada_kernel(c_ref, w_ref, b_ref, o_ref):
    c = c_ref[...]
    s = (c * jax.nn.sigmoid(c)).astype(BF16)
    o_ref[...] = _bdot(s, w_ref[...].astype(BF16)) + b_ref[...]


def _ada(cond8, w_ada, b_ada):
    nblk = 6 * D_MODEL // 1024
    return pl.pallas_call(
        _ada_kernel,
        grid=(DEPTH, nblk),
        in_specs=[
            pl.BlockSpec((SUB, D_MODEL), lambda l, n: (0, 0)),
            pl.BlockSpec((None, D_MODEL, 1024), lambda l, n: (l, 0, n)),
            pl.BlockSpec((None, 1, 1024), lambda l, n: (l, 0, n)),
        ],
        out_specs=pl.BlockSpec((None, SUB, 1024), lambda l, n: (l, 0, n)),
        out_shape=jax.ShapeDtypeStruct((DEPTH, SUB, 6 * D_MODEL), F32),
        compiler_params=_cparams(("arbitrary", "arbitrary")),
        name="ada",
    )(cond8, w_ada, b_ada.reshape(DEPTH, 1, 6 * D_MODEL))


def _front_kernel(x_ref, mod_ref, g_ref, w_ref, z_ref):
    x = x_ref[...]
    sh = mod_ref[:, 0:D_MODEL]
    sc = mod_ref[:, D_MODEL:2 * D_MODEL]
    h = _rms(x, g_ref[...]) * (1.0 + sc) + sh
    z_ref[...] = _bdot(h.astype(BF16), w_ref[...])


def _front(x, mod, g1, w_in, tm):
    t = x.shape[0]
    per_mod = t // (mod.shape[0] * tm)
    return pl.pallas_call(
        _front_kernel,
        grid=(t // tm,),
        in_specs=[
            pl.BlockSpec((tm, D_MODEL), lambda i: (i, 0)),
            pl.BlockSpec((None, 1, 6 * D_MODEL), lambda i: (i // per_mod, 0, 0)),
            pl.BlockSpec((1, D_MODEL), lambda i: (0, 0)),
            pl.BlockSpec((D_MODEL, Z_W), lambda i: (0, 0)),
        ],
        out_specs=pl.BlockSpec((tm, Z_W), lambda i: (i, 0)),
        out_shape=jax.ShapeDtypeStruct((t, Z_W), F32),
        compiler_params=_cparams(("arbitrary",)),
        name="front",
    )(x, mod, g1, w_in)


def _head_norm_rope_t(x_t, g_col, cos_t, sinr_t):
    ss = jnp.sum(x_t * x_t, axis=0, keepdims=True) * (1.0 / QK_DIM)
    x_t = x_t * lax.rsqrt(ss + EPS) * g_col
    return (x_t * cos_t + pltpu.roll(x_t, HEAD_PAD // 2, 0) * sinr_t).astype(BF16)


def _build_kv(ckv, kr_tile, cos_t, sinr_t, wk_t_ref, wv_ref, gk_col, eye_ref, kt_ref, v_ref):
    nt = (((1,), (1,)), ((), ()))
    cb = ckv.astype(BF16)
    kn_t = lax.dot_general(wk_t_ref[...], cb, nt, preferred_element_type=F32)
    kr_hi = kr_tile.astype(BF16)
    kr_lo = (kr_tile - kr_hi.astype(F32)).astype(BF16)
    kr_t = (lax.dot_general(eye_ref[...], kr_hi, nt, preferred_element_type=F32)
            + lax.dot_general(eye_ref[...], kr_lo, nt, preferred_element_type=F32))
    for h in range(MLA_HEADS):
        sl = slice(h * HEAD_PAD, (h + 1) * HEAD_PAD)
        kt_ref[sl, :] = _head_norm_rope_t(kn_t[sl, :] + kr_t, gk_col, cos_t, sinr_t)
    v = _bdot(cb, wv_ref[...])
    lane = lax.broadcasted_iota(jnp.int32, v.shape, 1)
    v_ref[...] = jnp.where((lane & (HEAD_PAD - 1)) == DEN_LANE, 1.0, v).astype(BF16)


def _qkv_kernel(z_ref, cos_t_ref, sinr_t_ref, gqa_ref, wq_ref, gkva_ref, wk_ref, wv_ref,
                gq_ref, gk_ref, eye_ref, q_ref, k_ref, v_ref, ckv_ref):
    cos_t, sinr_t = cos_t_ref[...], sinr_t_ref[...]
    q_c = z_ref[:, 0:Q_LORA]
    kv_c = z_ref[:, Q_LORA:Q_LORA + KV_LORA]
    kr_tile = z_ref[:, Q_LORA + KV_LORA:QKV_W]
    nt = (((1,), (1,)), ((), ()))
    qf_t = lax.dot_general(wq_ref[...], _rms(q_c, gqa_ref[...]).astype(BF16), nt, preferred_element_type=F32)
    gq_col = gq_ref[...] * (ATTN_SCALE * LOG2E)
    for h in range(MLA_HEADS):
        sl = slice(h * HEAD_PAD, (h + 1) * HEAD_PAD)
        q_ref[sl, :] = _head_norm_rope_t(qf_t[sl, :], gq_col, cos_t, sinr_t)
    ckv = _rms(kv_c, gkva_ref[...])
    ckv_ref[...] = ckv
    _build_kv(ckv, kr_tile, cos_t, sinr_t, wk_ref, wv_ref, gk_ref[...], eye_ref, k_ref, v_ref)


def _qkv(z, tabs, p, seq_len, tm):
    t = z.shape[0]
    if tm > seq_len:
        tabs = [jnp.tile(a, (tm // seq_len, 1)) for a in tabs]
    tabs_t = [a.T for a in tabs]
    per_seq = max(seq_len // tm, 1)
    tab_t_spec = pl.BlockSpec((LANE, tm), lambda i: (0, i % per_seq))
    full = lambda shape: pl.BlockSpec(shape, lambda i: (0,) * len(shape))
    return pl.pallas_call(
        _qkv_kernel,
        grid=(t // tm,),
        in_specs=[
            pl.BlockSpec((tm, QKV_W), lambda i: (i, Z_QKV // QKV_W)),
            tab_t_spec, tab_t_spec,
            full((1, Q_LORA)), full((HP, Q_LORA)), full((1, KV_LORA)),
            full((HP, KV_LORA)), full((KV_LORA, HP)), full((LANE, 1)), full((LANE, 1)), full((LANE, LANE)),
        ],
        out_specs=[
            pl.BlockSpec((HP, tm), lambda i: (0, i)),
            pl.BlockSpec((HP, tm), lambda i: (0, i)),
            pl.BlockSpec((tm, HP), lambda i: (i, 0)),
            pl.BlockSpec((tm, KV_LORA), lambda i: (i, 0)),
        ],
        out_shape=[
            jax.ShapeDtypeStruct((HP, t), BF16),
            jax.ShapeDtypeStruct((HP, t), BF16),
            jax.ShapeDtypeStruct((t, HP), BF16),
            jax.ShapeDtypeStruct((t, KV_LORA), F32),
        ],
        compiler_params=_cparams(("arbitrary",)),
        name="qkv",
    )(z, tabs_t[0], tabs_t[1], p["g_qa"], p["wq_t"], p["g_kva"], p["wk_t"], p["wv"],
      p["gq_col"], p["gk_col"], p["eye"])


def _kvcache_kernel(ckv_ref, kr_ref, cos_t_ref, sinr_t_ref, wk_ref, wv_ref, gk_ref, eye_ref, k_ref, v_ref):
    _build_kv(ckv_ref[...], kr_ref[...], cos_t_ref[...], sinr_t_ref[...], wk_ref, wv_ref, gk_ref[...], eye_ref,
              k_ref, v_ref)


def _kvcache(ckv, kr_tile, tabs, p):
    t = ckv.shape[0]
    return pl.pallas_call(
        _kvcache_kernel,
        out_shape=[jax.ShapeDtypeStruct((HP, t), BF16), jax.ShapeDtypeStruct((t, HP), BF16)],
        compiler_params=_cparams(None),
        name="kvcache",
    )(ckv, kr_tile, tabs[0].T, tabs[1].T, p["wk_t"], p["wv"], p["gk_col"], p["eye"])


def _fourier_kernel(nb, u_ref, cc_ref, sc_ref, cl_ref, sl_ref, o_ref, a_scr, b_scr):
    @pl.when(pl.program_id(1) == 0)
    def _():
        for b in range(nb):
            u = u_ref[b].astype(BF16)
            a_scr[:, b * FNET_W:(b + 1) * FNET_W] = _bdot(u, cc_ref[...]).astype(BF16)
            b_scr[:, b * FNET_W:(b + 1) * FNET_W] = _bdot(u, sc_ref[...]).astype(BF16)

    y = _bdot(cl_ref[...], a_scr[...]) - _bdot(sl_ref[...], b_scr[...])
    for b in range(nb):
        o_ref[b] = y[:, b * FNET_W:(b + 1) * FNET_W].astype(BF16)


def _fourier_split_kernel(nb, u_ref, cc_ref, sc_ref, ca_ref, sa_ref, cb_ref, sb_ref, o_ref,
                          a_scr, b_scr, cl_scr, sl_scr):
    @pl.when(pl.program_id(1) == 0)
    def _():
        for b in range(nb):
            u = u_ref[b].astype(BF16)
            a_scr[:, b * FNET_W:(b + 1) * FNET_W] = _bdot(u, cc_ref[...]).astype(BF16)
            b_scr[:, b * FNET_W:(b + 1) * FNET_W] = _bdot(u, sc_ref[...]).astype(BF16)

    cb, sb = cb_ref[...], sb_ref[...]
    for j in range(ca_ref.shape[0]):
        ca, sa = ca_ref[j:j + 1, :], sa_ref[j:j + 1, :]
        cl_scr[j * DFT_SPLIT:(j + 1) * DFT_SPLIT, :] = (ca * cb - sa * sb).astype(BF16)
        sl_scr[j * DFT_SPLIT:(j + 1) * DFT_SPLIT, :] = (sa * cb + ca * sb).astype(BF16)
    y = _bdot(cl_scr[...], a_scr[...]) - _bdot(sl_scr[...], b_scr[...])
    for b in range(nb):
        o_ref[b] = y[:, b * FNET_W:(b + 1) * FNET_W].astype(BF16)


def _fourier_split(z3, mats, nb, tq):
    s, l, _ = z3.shape
    cc, sc, ca, sa, cb, sb = mats
    full = lambda a: pl.BlockSpec(a.shape, lambda i, r: (0, 0))
    rows = pl.BlockSpec((tq // DFT_SPLIT, l), lambda i, r: (r, 0))
    return pl.pallas_call(
        functools.partial(_fourier_split_kernel, nb),
        grid=(s // nb, l // tq),
        in_specs=[pl.BlockSpec((nb, l, FNET_W), lambda i, r: (i, 0, 0)), full(cc), full(sc), rows, rows,
                  full(cb), full(sb)],
        out_specs=pl.BlockSpec((nb, tq, FNET_W), lambda i, r: (i, r, 0)),
        out_shape=jax.ShapeDtypeStruct((s, l, FNET_W), BF16),
        scratch_shapes=[pltpu.VMEM((l, nb * FNET_W), BF16), pltpu.VMEM((l, nb * FNET_W), BF16),
                        pltpu.VMEM((tq, l), BF16), pltpu.VMEM((tq, l), BF16)],
        compiler_params=_cparams(("arbitrary", "arbitrary")),
        name="fourier_split",
    )(z3, cc, sc, ca, sa, cb, sb)


def _fourier(z3, mats, nb, tq):
    s, l, _ = z3.shape
    if len(mats) == 6:
        return _fourier_split(z3, mats, nb, tq)
    cc, sc, cl, sl = mats
    return pl.pallas_call(
        functools.partial(_fourier_kernel, nb),
        grid=(s // nb, l // tq),
        in_specs=[
            pl.BlockSpec((nb, l, FNET_W), lambda i, r: (i, 0, 0)),
            pl.BlockSpec((FNET_W, FNET_W), lambda i, r: (0, 0)),
            pl.BlockSpec((FNET_W, FNET_W), lambda i, r: (0, 0)),
            pl.BlockSpec((tq, l), lambda i, r: (r, 0)),
            pl.BlockSpec((tq, l), lambda i, r: (r, 0)),
        ],
        out_specs=pl.BlockSpec((nb, tq, FNET_W), lambda i, r: (i, r, 0)),
        out_shape=jax.ShapeDtypeStruct((s, l, FNET_W), BF16),
        scratch_shapes=[pltpu.VMEM((l, nb * FNET_W), BF16), pltpu.VMEM((l, nb * FNET_W), BF16)],
        compiler_params=_cparams(("arbitrary", "arbitrary")),
        name="fourier",
    )(z3, cc, sc, cl, sl)


def _load_ext(ref, c, rows, nchunks):
    t0 = pl.multiple_of(c * rows, rows)
    main = ref[pl.ds(t0, rows), :]
    lo = pl.multiple_of(jnp.maximum(t0 - SUB, 0), SUB)
    hi = pl.multiple_of(jnp.minimum(t0 + rows, (nchunks - 1) * rows + rows - SUB), SUB)
    prev = jnp.where(c > 0, ref[pl.ds(lo, SUB), :], 0.0)
    nxt = jnp.where(c < nchunks - 1, ref[pl.ds(hi, SUB), :], 0.0)
    return jnp.concatenate([prev, main, nxt], axis=0)


def _shifted(ext, off, rows):
    n = ext.shape[0]
    r = ext if off == 0 else pltpu.roll(ext, (-off) % n, 0)
    return r[SUB:SUB + rows]


def _gelu_tanh(x):
    return 0.5 * x * (1.0 + jnp.tanh(math.sqrt(2.0 / math.pi) * (x + 0.044715 * (x * x * x))))


def _lru_gates(xc, d, wa_ref, wx_ref, ba_ref, bx_ref, lam_ref):
    xb = xc.astype(BF16)
    r = jax.nn.sigmoid(_bdot(xb, wa_ref[d]) + ba_ref[d])
    i = jax.nn.sigmoid(_bdot(xb, wx_ref[d]) + bx_ref[d])
    nl = -lam_ref[d]
    softplus = jnp.maximum(nl, 0.0) + jnp.log1p(jnp.exp(-jnp.abs(nl)))
    la = (-LRU_C) * r * softplus
    a = jnp.exp(la)
    one_m_a2 = -jnp.tanh(la) * (a * a + 1.0)
    b = jnp.sqrt(one_m_a2) * (i * xc)
    return a, b


def _group_scan(a, b, reverse):
    rows = a.shape[0]
    rm = lax.broadcasted_iota(jnp.int32, a.shape, 0) & (SUB - 1)
    for s in (1, 2, 4):
        if reverse:
            sh, m = rows - s, rm + s <= SUB - 1
        else:
            sh, m = s, rm >= s
        a_sh = pltpu.roll(a, sh, 0)
        b_sh = pltpu.roll(b, sh, 0)
        b = jnp.where(m, a * b_sh + b, b)
        a = jnp.where(m, a * a_sh, a)
    return a, b


def _carry_scan(a, b, carry, reverse):
    ng = a.shape[0] // SUB
    out = [None] * ng
    order = range(ng - 1, -1, -1) if reverse else range(ng)
    for g in order:
        hg = a[g * SUB:(g + 1) * SUB] * carry + b[g * SUB:(g + 1) * SUB]
        out[g] = hg
        carry = hg[0:1] if reverse else hg[SUB - 1:SUB]
    return jnp.concatenate(out, axis=0), carry


def _lru_kernel(nb, rows, nchunks, xl_ref, gl_ref, bs_ref, cs_ref, xs_ref, h0_ref, cw_ref, wa_ref, wx_ref,
                ba_ref, bx_ref, lam_ref, sw_ref, yl_ref, ys_ref, fin_ref, xc_scr, hf_scr, hb_scr):
    cw = cw_ref[...]
    sw = sw_ref[...]
    chunk = lambda c: pl.ds(pl.multiple_of(c * rows, rows), rows)
    seqs = range(nb)

    def conv_body(c, carry):
        for s in seqs:
            ext = _load_ext(xl_ref.at[s], c, rows, nchunks)
            xc = _shifted(ext, -2, rows) * cw[0:1]
            for k in range(1, 4):
                xc = xc + _shifted(ext, k - 2, rows) * cw[k:k + 1]
            xc_scr[s, chunk(c), :] = xc
        return carry

    lax.fori_loop(0, nchunks, conv_body, 0)

    def direction(s, c, d, carry, h_scr):
        a, b = _lru_gates(xc_scr[s, chunk(c), :], d, wa_ref, wx_ref, ba_ref, bx_ref, lam_ref)
        a, b = _group_scan(a, b, d == 1)
        h, carry = _carry_scan(a, b, carry, d == 1)
        h_scr[s, chunk(c), :] = h
        return carry

    def scan_body(j, carries):
        return tuple((direction(s, j, 0, cf, hf_scr), direction(s, nchunks - 1 - j, 1, cb, hb_scr))
                     for s, (cf, cb) in zip(seqs, carries))

    init = tuple((h0_ref[s, 0:1, :], h0_ref[s, 1:2, :]) for s in seqs)
    for s, (cf, cb) in zip(seqs, lax.fori_loop(0, nchunks, scan_body, init)):
        fin_ref[s, 0:1, :] = cf
        fin_ref[s, 1:2, :] = cb

    def out_body(c, carry):
        for s in seqs:
            y = (hf_scr[s, chunk(c), :] + hb_scr[s, chunk(c), :]) * _gelu_tanh(gl_ref[s, chunk(c), :])
            yl_ref[s, chunk(c), :] = y.astype(BF16)
            ext = _load_ext(cs_ref.at[s], c, rows, nchunks) * _load_ext(xs_ref.at[s], c, rows, nchunks)
            acc = _shifted(ext, -1, rows) * sw[0:1]
            acc = acc + _shifted(ext, 0, rows) * sw[1:2]
            acc = acc + _shifted(ext, 1, rows) * sw[2:3]
            ys_ref[s, chunk(c), :] = (bs_ref[s, chunk(c), :] * acc).astype(BF16)
        return carry

    lax.fori_loop(0, nchunks, out_body, 0)


def _lru(z3, h0, p, nb):
    s, l, _ = z3.shape
    rows = min(l, 1024)
    nchunks = l // rows
    half = lambda blk: pl.BlockSpec((nb, l, LANE), lambda i, j: (i, 0, blk + j))
    wspec = lambda shape: pl.BlockSpec((None,) + shape, lambda i, j: (j,) + (0,) * len(shape))
    seq_half = pl.BlockSpec((nb, l, LANE), lambda i, j: (i, 0, j))
    state = pl.BlockSpec((nb, 2, LANE), lambda i, j: (i, 0, j))
    return pl.pallas_call(
        functools.partial(_lru_kernel, nb, rows, nchunks),
        grid=(s // nb, 2),
        in_specs=[
            half(2), half(4), half(6), half(8), half(10), state,
            wspec((4, LANE)), wspec((2, LANE, LANE)), wspec((2, LANE, LANE)),
            wspec((2, 1, LANE)), wspec((2, 1, LANE)), wspec((2, 1, LANE)), wspec((3, LANE)),
        ],
        out_specs=[seq_half, seq_half, state],
        out_shape=[jax.ShapeDtypeStruct((s, l, LRU_W), BF16), jax.ShapeDtypeStruct((s, l, SC_W), BF16),
                   jax.ShapeDtypeStruct((s, 2, LRU_W), F32)],
        scratch_shapes=[pltpu.VMEM((nb, l, LANE), F32)] * 3,
        compiler_params=_cparams(("arbitrary", "arbitrary")),
        name="lru",
    )(z3, z3, z3, z3, z3, h0, p["lru_conv"], p["lru_wa"], p["lru_wx"], p["lru_ba"], p["lru_bx"], p["lru_lam"],
      p["sc_conv"])


def _store_head_pair(o_ref, rows, j, o_even, o_odd):
    lane = lax.broadcasted_iota(jnp.int32, o_even.shape, 1)
    pair = jnp.where(lane < V_DIM, o_even, pltpu.roll(o_odd, V_DIM, 1))
    o_ref[rows, j * LANE:(j + 1) * LANE] = pair.astype(BF16)


def _attn_kernel(hp, sub, has_cache, *refs):
    if has_cache:
        q_ref, k_ref, v_ref, kc_ref, vc_ref, o_ref, s_scr, p_scr, m_scr, sc_scr, pc_scr = refs
    else:
        q_ref, k_ref, v_ref, o_ref, s_scr, p_scr, m_scr = refs
    tn = (((0,), (0,)), ((), ()))
    items = [(slice(r * sub, (r + 1) * sub), slice(h * HEAD_PAD, (h + 1) * HEAD_PAD))
             for r in range(q_ref.shape[1] // sub) for h in range(hp)]
    n = len(items)

    def scores(i):
        rows, head = items[i]
        q_t = q_ref[head, rows]
        s = lax.dot_general(q_t, k_ref[head, :], tn, preferred_element_type=F32)
        m = jnp.max(s, axis=-1, keepdims=True)
        s_scr[i % 2] = s
        if has_cache:
            sc = lax.dot_general(q_t, kc_ref[head, :], tn, preferred_element_type=F32)
            m = jnp.maximum(m, jnp.max(sc, axis=-1, keepdims=True))
            sc_scr[i % 2] = sc
        m_scr[i % 2] = jnp.broadcast_to(m, m_scr.shape[1:])

    def probs(i):
        m = m_scr[i % 2][:, 0:1]
        p_scr[i % 2] = jnp.exp2(s_scr[i % 2] - m).astype(BF16)
        if has_cache:
            pc_scr[i % 2] = jnp.exp2(sc_scr[i % 2] - m).astype(BF16)

    pending = {}

    def weighted_values(i):
        rows, head = items[i]
        o = _bdot(p_scr[i % 2], v_ref[:, head])
        if has_cache:
            o = o + _bdot(pc_scr[i % 2], vc_ref[:, head])
        o = o / o[:, DEN_LANE:DEN_LANE + 1]
        h = i % hp
        if h % 2 == 0:
            pending[0] = o
        else:
            _store_head_pair(o_ref, rows, h // 2, pending.pop(0), o)

    scores(0)
    for i in range(n):
        if i + 1 < n:
            scores(i + 1)
        if i > 0:
            weighted_values(i - 1)
        probs(i)
    weighted_values(n - 1)


def _attn_short_kernel(hp, q_ref, k_ref, v_ref, o_ref):
    tn = (((0,), (0,)), ((), ()))
    heads = [slice(h * HEAD_PAD, (h + 1) * HEAD_PAD) for h in range(hp)]
    scores = [lax.dot_general(q_ref[hd, :], k_ref[hd, :], tn, preferred_element_type=F32) for hd in heads]
    probs = [jnp.exp2(s - jnp.max(s, axis=-1, keepdims=True)).astype(BF16) for s in scores]
    outs = []
    for hd, p in zip(heads, probs):
        o = _bdot(p, v_ref[:, hd])
        outs.append(o / o[:, DEN_LANE:DEN_LANE + 1])
    for j in range(hp // 2):
        _store_head_pair(o_ref, slice(None), j, outs[2 * j], outs[2 * j + 1])


def _attn(q, k, v, cache, hp, tq):
    s, l, _ = v.shape
    w = hp * HEAD_PAD
    qspec = pl.BlockSpec((w, tq), lambda i, h, r: (h, i * (l // tq) + r))
    ospec = pl.BlockSpec((None, tq, hp * V_DIM), lambda i, h, r: (i, r, h))
    if cache is None and tq == l:
        kspec = pl.BlockSpec((w, l), lambda i, h, r: (h, i))
        vspec = pl.BlockSpec((None, l, w), lambda i, h, r: (i, 0, h))
        return pl.pallas_call(
            functools.partial(_attn_short_kernel, hp),
            grid=(s, MLA_HEADS // hp, 1),
            in_specs=[qspec, kspec, vspec],
            out_specs=ospec,
            out_shape=jax.ShapeDtypeStruct((s, l, MLA_OUT), BF16),
            compiler_params=_cparams(("arbitrary", "arbitrary", "arbitrary")),
            name="attn_short",
        )(q, k, v)
    kv_mode = dict(pipeline_mode=pl.Buffered(1))
    kspec = pl.BlockSpec((w, l), lambda i, h, r: (h, i), **kv_mode)
    vspec = pl.BlockSpec((None, l, w), lambda i, h, r: (i, 0, h), **kv_mode)
    in_specs = [qspec, kspec, vspec]
    args = [q, k, v]
    if cache is not None:
        lc = cache[1].shape[1]
        in_specs += [pl.BlockSpec((w, lc), lambda i, h, r: (h, i), **kv_mode),
                     pl.BlockSpec((None, lc, w), lambda i, h, r: (i, 0, h), **kv_mode)]
        args += list(cache)
    sub = min(tq, ATTN_SUB)
    scratch = [pltpu.VMEM((2, sub, l), F32), pltpu.VMEM((2, sub, l), BF16), pltpu.VMEM((2, sub, LANE), F32)]
    if cache is not None:
        scratch += [pltpu.VMEM((2, sub, lc), F32), pltpu.VMEM((2, sub, lc), BF16)]
    return pl.pallas_call(
        functools.partial(_attn_kernel, hp, sub, cache is not None),
        grid=(s, MLA_HEADS // hp, l // tq),
        in_specs=in_specs,
        out_specs=ospec,
        out_shape=jax.ShapeDtypeStruct((s, l, MLA_OUT), BF16),
        scratch_shapes=scratch,
        compiler_params=_cparams(("arbitrary", "arbitrary", "arbitrary")),
        name="attn",
    )(*args)


def _route(lt):
    row = lax.broadcasted_iota(jnp.int32, (SUB, lt.shape[1]), 0).astype(F32)
    neg = -jnp.inf
    gl = lt[0:SUB]
    g_ok = row < N_GROUPS
    glm = jnp.where(g_ok, gl, neg)
    gmax = jnp.max(glm, axis=0, keepdims=True)
    gsel = jnp.min(jnp.where(glm == gmax, row, float(SUB)), axis=0, keepdims=True)
    g_w = 1.0 / jnp.sum(jnp.where(g_ok, jnp.exp(gl - gmax), 0.0), axis=0, keepdims=True)
    es = lt[SUB:2 * SUB]
    for g in range(1, N_GROUPS):
        es = jnp.where(gsel == float(g), lt[(g + 1) * SUB:(g + 2) * SUB], es)
    m1 = jnp.max(es, axis=0, keepdims=True)
    i1 = jnp.min(jnp.where(es == m1, row, float(SUB)), axis=0, keepdims=True)
    es2 = jnp.where(row == i1, neg, es)
    m2 = jnp.max(es2, axis=0, keepdims=True)
    i2 = jnp.min(jnp.where(es2 == m2, row, float(SUB)), axis=0, keepdims=True)
    e2 = jnp.exp(m2 - m1)
    inv = g_w / (1.0 + e2)
    base = gsel * EXP_PER_GROUP
    return (jnp.where(row == RT_ID0, base + i1, 0.0) + jnp.where(row == RT_ID1, base + i2, 0.0)
            + jnp.where(row == RT_W0, inv, 0.0) + jnp.where(row == RT_W1, inv * e2, 0.0))


def _pack_bf16_pairs(x):
    k = x.shape[1] // 2
    bits = lax.bitcast_convert_type(x.astype(BF16).astype(F32), jnp.int32)
    return lax.shift_right_logical(bits[:, :k], 16) | (bits[:, k:] & HI_MASK)


def _unpack_bf16_pairs(w):
    lo = lax.bitcast_convert_type(lax.shift_left(w, 16), F32).astype(BF16)
    hi = lax.bitcast_convert_type(w & HI_MASK, F32).astype(BF16)
    return lo, hi


def _back_kernel(x_ref, mod_ref, g1_ref, g2_ref, yf_ref, yl_ref, ys_ref, o_ref,
                 wg_ref, wpf_ref, wpl_ref, wps_ref, wpm_ref, wo_ref, wrh_ref, wrl_ref, br_ref,
                 x1_ref, h2_ref, comb_ref):
    d = D_MODEL
    x = x_ref[...]
    sh1, sc1, gt1 = mod_ref[:, 0:d], mod_ref[:, d:2 * d], mod_ref[:, 2 * d:3 * d]
    sh2, sc2 = mod_ref[:, 3 * d:4 * d], mod_ref[:, 4 * d:5 * d]
    hb = (_rms(x, g1_ref[...]) * (1.0 + sc1) + sh1).astype(BF16)
    branches = ((yf_ref, wpf_ref), (yl_ref, wpl_ref), (ys_ref, wps_ref), (o_ref, wpm_ref))
    blocks = []
    for n in range(0, d, MERGE_BLK):
        acc = None
        for j, (y_ref, w_ref) in enumerate(branches):
            gate = jax.nn.sigmoid(_bdot(hb, wg_ref[:, j * d + n:j * d + n + MERGE_BLK]))
            term = gate * _bdot(y_ref[...], w_ref[:, n:n + MERGE_BLK])
            acc = term if acc is None else acc + term
        blocks.append(acc.astype(BF16))
    x1 = x + gt1 * _bdot(jnp.concatenate(blocks, axis=1), wo_ref[...])
    x1_ref[...] = x1
    h2 = _rms(x1, g2_ref[...]) * (1.0 + sc2) + sh2
    h2b = h2.astype(BF16)
    h2_ref[...] = _pack_bf16_pairs(h2)
    h2l = (h2 - h2b.astype(F32)).astype(BF16)
    nt = (((1,), (1,)), ((), ()))
    rdot = lambda w_ref, h: lax.dot_general(w_ref[...], h, nt, preferred_element_type=F32)
    logits_t = rdot(wrh_ref, h2b) + rdot(wrh_ref, h2l) + rdot(wrl_ref, h2b) + br_ref[...]
    comb_ref[...] = _route(logits_t)


def _back(x, mod, yf, yl, ys, o, p, tm):
    t = x.shape[0]
    per_mod = t // (mod.shape[0] * tm)
    row = lambda w: pl.BlockSpec((tm, w), lambda i: (i, 0))
    full = lambda a: pl.BlockSpec(a.shape, lambda i: (0,) * a.ndim, pipeline_mode=pl.Buffered(1))
    weights = [p["w_gate"], p["w_pf"], p["w_pl"], p["w_ps"], p["w_pm"], p["w_out"], p["wr_hi"], p["wr_lo"], p["b_r"]]
    return pl.pallas_call(
        _back_kernel,
        grid=(t // tm,),
        in_specs=[
            row(D_MODEL),
            pl.BlockSpec((None, 1, 6 * D_MODEL), lambda i: (i // per_mod, 0, 0)),
            pl.BlockSpec((1, D_MODEL), lambda i: (0, 0)),
            pl.BlockSpec((1, D_MODEL), lambda i: (0, 0)),
            row(FNET_W), row(LRU_W), row(SC_W), row(MLA_OUT),
        ] + [full(w) for w in weights],
        out_specs=[row(D_MODEL), row(PACK_W), pl.BlockSpec((SUB, tm), lambda i: (0, i))],
        out_shape=[
            jax.ShapeDtypeStruct((t, D_MODEL), F32),
            jax.ShapeDtypeStruct((t, PACK_W), jnp.int32),
            jax.ShapeDtypeStruct((SUB, t), F32),
        ],
        compiler_params=_cparams(("arbitrary",)),
        name="back",
    )(x, mod, p["norm1_g"], p["norm2_g"], yf, yl, ys, o, *weights)


def _plan_kernel(tm, rt_ref, pos_ref, te_ref, nu_ref):
    ids = jnp.concatenate([rt_ref[RT_ID0:RT_ID0 + 1, :], rt_ref[RT_ID1:RT_ID1 + 1, :]], axis=1)
    n_rows = ids.shape[1]
    erow = lax.broadcasted_iota(jnp.int32, (N_EXPERTS, PLAN_BLK), 0).astype(F32)
    col = lax.broadcasted_iota(jnp.int32, (N_EXPERTS, LANE), 0)
    blocks = [(ids[:, b * PLAN_BLK:(b + 1) * PLAN_BLK] == erow).astype(F32) for b in range(n_rows // PLAN_BLK)]
    tri = (lax.broadcasted_iota(jnp.int32, (PLAN_BLK, PLAN_BLK), 0)
           <= lax.broadcasted_iota(jnp.int32, (PLAN_BLK, PLAN_BLK), 1)).astype(BF16)
    prefix = [_bdot(blk.astype(BF16), tri) for blk in blocks]
    counts = prefix[0][:, PLAN_BLK - 1:PLAN_BLK]
    for pre in prefix[1:]:
        counts = counts + pre[:, PLAN_BLK - 1:PLAN_BLK]
    tiles = jnp.floor((counts + float(tm - 1)) * (1.0 / tm))
    tile_end = jnp.broadcast_to(tiles, (N_EXPERTS, LANE))
    for s in (1, 2, 4, 8, 16):
        tile_end = tile_end + jnp.where(col >= s, pltpu.roll(tile_end, s, 0), 0.0)
    base = (tile_end[:, 0:1] - tiles) * float(tm) - 1.0
    for b, (blk, pre) in enumerate(zip(blocks, prefix)):
        pos = jnp.sum(blk * (base + pre), axis=0, keepdims=True)
        pos_ref[:, b * PLAN_BLK:(b + 1) * PLAN_BLK] = pos.astype(jnp.int32)
        base = base + pre[:, PLAN_BLK - 1:PLAN_BLK]
    k = lax.broadcasted_iota(jnp.int32, (N_EXPERTS, LANE), 1).astype(F32)
    owner = jnp.sum((k >= tile_end).astype(F32), axis=0, keepdims=True)
    te_ref[...] = jnp.minimum(owner, float(N_EXPERTS - 1)).astype(jnp.int32)
    nu_ref[...] = jnp.broadcast_to(tile_end[N_EXPERTS - 1:N_EXPERTS, :], nu_ref.shape).astype(jnp.int32)


def _dispatch_plan(rt, tm):
    n_rows = TOP_K * rt.shape[1]
    n_tiles = n_rows // tm + N_EXPERTS
    assert n_tiles <= LANE and tm & (tm - 1) == 0 and n_rows % PLAN_BLK == 0
    pos, te, nu = pl.pallas_call(
        functools.partial(_plan_kernel, tm),
        out_shape=[jax.ShapeDtypeStruct((1, n_rows), jnp.int32), jax.ShapeDtypeStruct((1, LANE), jnp.int32),
                   jax.ShapeDtypeStruct((1, LANE), jnp.int32)],
        compiler_params=_cparams(None),
        name="plan",
    )(rt)
    return pos.reshape(n_rows), te[0, :n_tiles], nu[0, :1], n_tiles


def _sc_worker_rows(n_rows):
    workers = SC_CORES * SC_SUBCORES
    per_w = n_rows // workers
    assert per_w * workers == n_rows and per_w % SC_CHUNK == 0
    return per_w


def _sc_mesh():
    return plsc.VectorSubcoreMesh(core_axis_name="c", subcore_axis_name="s",
                                  num_cores=SC_CORES, num_subcores=SC_SUBCORES)


def _sc_scatter_rows(src, pos, n_out):
    t, w = src.shape
    per_w = _sc_worker_rows(pos.shape[0])

    @functools.partial(
        pl.kernel, mesh=_sc_mesh(), out_type=jax.ShapeDtypeStruct((n_out, w), src.dtype),
        scratch_types=[pltpu.VMEM((SC_CHUNK,), jnp.int32), pltpu.VMEM((SC_CHUNK, w), src.dtype)],
        name="moe_dispatch")
    def run(src_hbm, pos_hbm, out_hbm, idx_v, rows_v):
        wid = lax.axis_index("s") * SC_CORES + lax.axis_index("c")

        @pl.loop(0, per_w // SC_CHUNK)
        def _(i):
            j0 = wid * per_w + i * SC_CHUNK
            pltpu.sync_copy(pos_hbm.at[pl.ds(j0, SC_CHUNK)], idx_v)
            pltpu.sync_copy(src_hbm.at[pl.ds(lax.rem(j0, t), SC_CHUNK)], rows_v)
            pltpu.sync_copy(rows_v, out_hbm.at[idx_v])

    return run(src, pos)


def _sc_gather_rows(src, pos):
    w = src.shape[1]
    n = pos.shape[0]
    per_w = _sc_worker_rows(n)

    @functools.partial(
        pl.kernel, mesh=_sc_mesh(), out_type=jax.ShapeDtypeStruct((n, w), src.dtype),
        scratch_types=[pltpu.VMEM((SC_CHUNK,), jnp.int32), pltpu.VMEM((SC_CHUNK, w), src.dtype)],
        name="moe_return")
    def run(src_hbm, pos_hbm, out_hbm, idx_v, rows_v):
        wid = lax.axis_index("s") * SC_CORES + lax.axis_index("c")

        @pl.loop(0, per_w // SC_CHUNK)
        def _(i):
            j0 = wid * per_w + i * SC_CHUNK
            pltpu.sync_copy(pos_hbm.at[pl.ds(j0, SC_CHUNK)], idx_v)
            pltpu.sync_copy(src_hbm.at[idx_v], rows_v)
            pltpu.sync_copy(rows_v, out_hbm.at[pl.ds(j0, SC_CHUNK)])

    return run(src, pos)


def _experts_kernel(te_ref, nu_ref, x_ref, w1_ref, w3_ref, w2_ref, y_ref):
    half = D_MODEL // 2

    @pl.when(pl.program_id(0) < nu_ref[0])
    def _():
        lo, hi = _unpack_bf16_pairs(x_ref[...])
        a = _bdot(lo, w1_ref[:half, :].astype(BF16)) + _bdot(hi, w1_ref[half:, :].astype(BF16))
        u = _bdot(lo, w3_ref[:half, :].astype(BF16)) + _bdot(hi, w3_ref[half:, :].astype(BF16))
        mid = (a * jax.nn.sigmoid(a)) * u
        y_ref[...] = _pack_bf16_pairs(_bdot(mid.astype(BF16), w2_ref[...].astype(BF16)))


def _experts(xs, tile_expert, n_used, n_tiles, w1, w3, w2, layer, tm):
    wspec = lambda a: pl.BlockSpec((None, None) + a.shape[2:], lambda i, te, nu: (layer, te[i], 0, 0))
    grid_spec = pltpu.PrefetchScalarGridSpec(
        num_scalar_prefetch=2,
        grid=(n_tiles,),
        in_specs=[pl.BlockSpec((tm, PACK_W), lambda i, te, nu: (jnp.minimum(i, nu[0] - 1), 0)),
                  wspec(w1), wspec(w3), wspec(w2)],
        out_specs=pl.BlockSpec((tm, PACK_W), lambda i, te, nu: (jnp.where(i < nu[0], i, n_tiles), 0)),
    )
    return pl.pallas_call(
        _experts_kernel,
        grid_spec=grid_spec,
        out_shape=jax.ShapeDtypeStruct((xs.shape[0] + tm, PACK_W), jnp.int32),
        compiler_params=_cparams(("arbitrary",)),
        name="experts",
    )(tile_expert, n_used, xs, w1, w3, w2)


def _combine_kernel(x1_ref, mod_ref, rt_ref, g0_ref, g1_ref, o_ref):
    rt = rt_ref[...]
    lo0, hi0 = _unpack_bf16_pairs(g0_ref[...])
    lo1, hi1 = _unpack_bf16_pairs(g1_ref[...])
    w0 = rt[:, RT_W0:RT_W0 + 1]
    w1 = rt[:, RT_W1:RT_W1 + 1]
    half = D_MODEL // 2
    gt2 = mod_ref[:, 5 * D_MODEL:6 * D_MODEL]
    o_ref[:, :half] = x1_ref[:, :half] + gt2[:, :half] * (w0 * lo0.astype(F32) + w1 * lo1.astype(F32))
    o_ref[:, half:] = x1_ref[:, half:] + gt2[:, half:] * (w0 * hi0.astype(F32) + w1 * hi1.astype(F32))


def _combine(x1, mod, rt, g, tm):
    t = x1.shape[0]
    per_mod = t // (mod.shape[0] * tm)
    return pl.pallas_call(
        _combine_kernel,
        grid=(t // tm,),
        in_specs=[
            pl.BlockSpec((tm, D_MODEL), lambda i: (i, 0)),
            pl.BlockSpec((None, 1, 6 * D_MODEL), lambda i: (i // per_mod, 0, 0)),
            pl.BlockSpec((tm, LANE), lambda i: (i, 0)),
            pl.BlockSpec((tm, PACK_W), lambda i: (i, 0)),
            pl.BlockSpec((tm, PACK_W), lambda i: (i + t // tm, 0)),
        ],
        out_specs=pl.BlockSpec((tm, D_MODEL), lambda i: (i, 0)),
        out_shape=jax.ShapeDtypeStruct((t, D_MODEL), F32),
        compiler_params=_cparams(("arbitrary",)),
        name="combine",
    )(x1, mod, rt, g, g)


def _moe(h2p, rt, x1, mod, p, tm_e, tm_c):
    pos, tile_expert, n_used, n_tiles = _dispatch_plan(rt, tm_e)
    xs = _sc_scatter_rows(h2p, pos, n_tiles * tm_e)
    ys = _experts(xs, tile_expert, n_used, n_tiles, p["w1"], p["w3"], p["w2"], p["layer"], tm_e)
    g = _sc_gather_rows(ys, pos)
    rt_cols = jnp.pad(rt.T, ((0, 0), (0, LANE - rt.shape[0])))
    return _combine(x1, mod, rt_cols, g, tm_c)


def _channel_dft():
    n = np.arange(FNET_W // FNET_GROUPS)
    ang = 2.0 * np.pi * ((n[:, None] * n[None, :]) % n.size) / n.size
    eye = np.eye(FNET_GROUPS)
    scale = 1.0 / math.sqrt(n.size)
    return (jnp.asarray(np.kron(eye, np.cos(ang) * scale), F32).astype(BF16),
            jnp.asarray(np.kron(eye, np.sin(ang) * scale), F32).astype(BF16))


def _position_dft(l):
    scale = 1.0 / math.sqrt(l)
    if l <= 256:
        n = np.arange(l)
        ang = 2.0 * np.pi * ((n[:, None] * n[None, :]) % l) / l
        return jnp.asarray(np.cos(ang) * scale, F32).astype(BF16), jnp.asarray(np.sin(ang) * scale, F32).astype(BF16)
    m = DFT_SPLIT
    n = np.arange(l)
    ang_a = 2.0 * np.pi * ((np.arange(l // m)[:, None] * m * n[None, :]) % l) / l
    ang_b = 2.0 * np.pi * ((np.arange(m)[:, None] * n[None, :]) % l) / l
    return (jnp.asarray(np.cos(ang_a), F32), jnp.asarray(np.sin(ang_a), F32),
            jnp.asarray(np.cos(ang_b) * scale, F32), jnp.asarray(np.sin(ang_b) * scale, F32))


def _rope_tables(l, rotate):
    cos = np.ones((l, LANE))
    sinr = np.zeros((l, LANE))
    if rotate:
        t = np.arange(l)
        inv = ROPE_BASE ** (-np.arange(N_FREQ) / N_FREQ)
        ang = np.concatenate([(t // GRID_W)[:, None] * inv, (t % GRID_W)[:, None] * inv], axis=-1)
        half = ROPE // 2
        for lo in (ROPE_X1, ROPE_X2):
            cos[:, lo:lo + half] = np.cos(ang)
        sinr[:, ROPE_X1:ROPE_X1 + half] = -np.sin(ang)
        sinr[:, ROPE_X2:ROPE_X2 + half] = np.sin(ang)
    return tuple(jnp.asarray(a, F32) for a in (cos, sinr))


def _head_lane_source():
    half = ROPE // 2
    src = np.full((HEAD_PAD,), QK_DIM, np.int32)
    src[ROPE_X1:ROPE_X1 + half] = NOPE + np.arange(half)
    src[ROPE_X2:ROPE_X2 + half] = NOPE + half + np.arange(half)
    free = [i for i in range(HEAD_PAD) if src[i] == QK_DIM][:NOPE]
    src[free] = np.arange(NOPE)
    return src


def _place_head_dims(w, n_src):
    src = _head_lane_source()
    src = np.where(src < n_src, src, n_src)
    wz = jnp.concatenate([w[..., :n_src], jnp.zeros(w.shape[:-1] + (1,), w.dtype)], axis=-1)
    out = jnp.take(wz, jnp.asarray(src), axis=-1)
    return out.reshape(out.shape[:-2] + (out.shape[-2] * HEAD_PAD,))


def _place_rope_key(kr):
    half = ROPE // 2
    out = jnp.zeros(kr.shape[:-1] + (LANE,), kr.dtype)
    return out.at[..., ROPE_X1:ROPE_X1 + half].set(kr[..., :half]).at[..., ROPE_X2:ROPE_X2 + half].set(kr[..., half:])


def _take_rope_key(tile):
    half = ROPE // 2
    return jnp.concatenate([tile[..., ROPE_X1:ROPE_X1 + half], tile[..., ROPE_X2:ROPE_X2 + half]], axis=-1)


def _pad_heads(w, lo, hi):
    r = w.shape[0]
    part = w[:, :, lo:hi]
    out = jnp.zeros((r, MLA_HEADS, HEAD_PAD), w.dtype).at[:, :, : hi - lo].set(part)
    return out.reshape(r, HP)


def _blockdiag_halves(w):
    bw = LRU_W // LRU_BLOCKS
    out = jnp.zeros((2, 2, LANE, LANE), w.dtype)
    for half in range(2):
        for k in range(2):
            n = 2 * half + k
            out = out.at[half, :, k * bw:(k + 1) * bw, k * bw:(k + 1) * bw].set(w[:, n])
    return out


def _halves(v):
    return jnp.moveaxis(v.reshape(v.shape[:-1] + (2, LANE)), -2, 0)


def _layer_params(l, a):
    d = D_MODEL
    w_in = a["w_in"][l]
    w_in = jnp.concatenate([w_in[:, :Z_KR], _place_rope_key(w_in[:, Z_KR:])], axis=1)
    w_r = jnp.zeros((RT_ROWS, d), F32).at[:N_GROUPS].set(a["w_gr"][l].T).at[SUB:SUB + N_EXPERTS].set(a["w_er"][l].T)
    b_r = jnp.zeros((RT_ROWS, 1), F32).at[:N_GROUPS, 0].set(a["b_gr"][l]).at[SUB:SUB + N_EXPERTS, 0].set(a["b_er"][l])
    wr_hi = w_r.astype(BF16)
    gpad = lambda g: jnp.take(jnp.concatenate([g, jnp.zeros((1,), F32)]), jnp.asarray(_head_lane_source())).reshape(1, LANE)
    return {
        "norm1_g": a["norm1_g"][l].reshape(1, d), "norm2_g": a["norm2_g"][l].reshape(1, d),
        "w_in": w_in.astype(BF16),
        "lru_conv": _halves(a["lru_conv"][l]),
        "lru_wa": _blockdiag_halves(a["lru_wa"][l]).astype(BF16),
        "lru_wx": _blockdiag_halves(a["lru_wx"][l]).astype(BF16),
        "lru_ba": _halves(a["lru_ba"][l])[:, :, None, :], "lru_bx": _halves(a["lru_bx"][l])[:, :, None, :],
        "lru_lam": _halves(a["lru_lam"][l])[:, :, None, :],
        "sc_conv": _halves(a["sc_conv"][l]),
        "g_qa": a["g_qa"][l].reshape(1, Q_LORA), "g_kva": a["g_kva"][l].reshape(1, KV_LORA),
        "wq_t": _place_head_dims(a["w_qb"][l], QK_DIM).T.astype(BF16),
        "wk_t": _place_head_dims(a["w_kvb"][l], NOPE).T.astype(BF16),
        "wv": _pad_heads(a["w_kvb"][l], NOPE, NOPE + V_DIM).astype(BF16),
        "gq_col": gpad(a["g_qn"][l]).reshape(LANE, 1), "gk_col": gpad(a["g_kn"][l]).reshape(LANE, 1),
        "eye": jnp.eye(LANE, dtype=F32).astype(BF16),
        "w_gate": a["w_gate"][l].astype(BF16),
        "w_pf": a["w_pf"][l].astype(BF16), "w_pl": a["w_pl"][l].astype(BF16), "w_ps": a["w_ps"][l].astype(BF16),
        "w_pm": a["w_pm"][l].astype(BF16), "w_out": a["w_out"][l].astype(BF16),
        "wr_hi": wr_hi, "wr_lo": (w_r - wr_hi.astype(F32)).astype(BF16), "b_r": b_r,
        "w1": a["w1"], "w3": a["w3"], "w2": a["w2"], "layer": l,
    }


def _trunk_layer(x, mod, p, consts, h0, cache, cfg):
    s, l = cfg["s"], cfg["l"]
    z = _front(x, mod, p["norm1_g"], p["w_in"], cfg["tm_front"])
    z3 = z.reshape(s, l, Z_W)
    q, k, v, ckv = _qkv(z, consts["rope"], p, l, cfg["tm_qkv"])
    yf = _fourier(z3, consts["dft"], cfg["nb"], cfg["tq_f"])
    yl, ys, fin = _lru(z3, h0, p, cfg["nb_lru"])
    o = _attn(q, k, v.reshape(s, l, HP), cache, cfg["hp"], cfg["tq_a"])
    t = s * l
    x1, h2, comb = _back(x, mod, yf.reshape(t, FNET_W), yl.reshape(t, LRU_W), ys.reshape(t, SC_W),
                         o.reshape(t, MLA_OUT), p, cfg["tm_back"])
    x2 = _moe(h2, comb, x1, mod, p, cfg["tm_moe"], cfg["tm_comb"])
    return x2, z3, ckv, fin


def kernel(x_prompt, x_sample, cache_ckv, cache_krope, state_rglru, c, c_ctx, norm1_g, norm2_g, w_ada, b_ada, w_in, lru_conv, lru_wa, lru_ba, lru_wx, lru_bx, lru_lam, sc_conv, g_qa, w_qb, g_kva, w_kvb, g_qn, g_kn, w_pf, w_pl, w_ps, w_pm, w_gate, w_out, w_gr, b_gr, w_er, b_er, w1, w3, w2):
    a = dict(norm1_g=norm1_g, norm2_g=norm2_g, w_in=w_in, lru_conv=lru_conv, lru_wa=lru_wa, lru_ba=lru_ba,
             lru_wx=lru_wx, lru_bx=lru_bx, lru_lam=lru_lam, sc_conv=sc_conv, g_qa=g_qa, w_qb=w_qb, g_kva=g_kva,
             w_kvb=w_kvb, g_qn=g_qn, g_kn=g_kn, w_pf=w_pf, w_pl=w_pl, w_ps=w_ps, w_pm=w_pm, w_gate=w_gate,
             w_out=w_out, w_gr=w_gr, b_gr=b_gr, w_er=w_er, b_er=b_er, w1=w1, w3=w3, w2=w2)
    bc, lc, d = x_prompt.shape
    bl, ll, _ = x_sample.shape
    past = cache_ckv.shape[2]

    cond8 = jnp.zeros((SUB, d), F32).at[0].set(c_ctx).at[1:1 + bl].set(c)
    mod_all = _ada(cond8, w_ada, b_ada)

    cc, sc = _channel_dft()
    ctx_consts = {"dft": (cc, sc) + _position_dft(lc), "rope": _rope_tables(lc, False)}
    lat_consts = {"dft": (cc, sc) + _position_dft(ll), "rope": _rope_tables(ll, True)}
    cache_tabs = _rope_tables(bl * past, False)
    ctx_cfg = dict(s=bc, l=lc, tm_front=1024, tm_qkv=1024, nb=8, nb_lru=4, tq_f=lc, hp=MLA_HEADS, tq_a=lc, tm_back=1024, tm_moe=512, tm_comb=1024)
    lat_cfg = dict(s=bl, l=ll, tm_front=1024, tm_qkv=1024, nb=bl, nb_lru=1, tq_f=512, hp=MLA_HEADS, tq_a=256, tm_back=1024, tm_moe=512, tm_comb=1024)

    xp = x_prompt.reshape(bc * lc, d)
    xs = x_sample.reshape(bl * ll, d)
    h0_ctx = jnp.zeros((bc, 2, LRU_W), F32)
    ckv_list, kr_list, lru_list = [], [], []
    for l in range(DEPTH):
        p = _layer_params(l, a)
        mod_ctx = mod_all[l, 0:1].reshape(1, 1, 6 * d)
        mod_lat = mod_all[l, 1:1 + bl].reshape(bl, 1, 6 * d)
        xp, z3, ckv, fin = _trunk_layer(xp, mod_ctx, p, ctx_consts, h0_ctx, None, ctx_cfg)
        ckv_list.append(ckv.reshape(bc, lc, KV_LORA))
        kr_list.append(_take_rope_key(z3[:, :, Z_KR:]))
        lru_list.append(fin)
        kr_tile = _place_rope_key(cache_krope[:, l].reshape(bl * past, ROPE))
        kc, vc = _kvcache(cache_ckv[:, l].reshape(bl * past, KV_LORA), kr_tile, cache_tabs, p)
        cache = (kc, vc.reshape(bl, past, HP))
        xs, _, _, _ = _trunk_layer(xs, mod_lat, p, lat_consts, state_rglru[:, l], cache, lat_cfg)
    return (xp.reshape(bc, lc, d), xs.reshape(bl, ll, d),
            jnp.stack(ckv_list, axis=1), jnp.stack(kr_list, axis=1), jnp.stack(lru_list, axis=1))
```

```python
import functools
import math

import numpy as np
import jax
import jax.numpy as jnp
from jax import lax
from jax.experimental import pallas as pl
from jax.experimental.pallas import tpu as pltpu
from jax.experimental.pallas import tpu_sc as plsc

F32 = jnp.float32
BF16 = jnp.bfloat16

D_MODEL = 1024
DEPTH = 2
GRID_W = 64
EPS = 1e-6
FNET_W = 256
FNET_GROUPS = 4
LRU_W = 256
LRU_BLOCKS = 4
LRU_C = 8.0
SC_W = 256
MLA_HEADS = 8
Q_LORA = 384
KV_LORA = 256
NOPE = 64
ROPE = 32
V_DIM = 64
QK_DIM = NOPE + ROPE
MLA_OUT = MLA_HEADS * V_DIM
N_FREQ = ROPE // 4
ROPE_BASE = 10000.0
ATTN_SCALE = QK_DIM ** -0.5
LOG2E = math.log2(math.e)
N_GROUPS = 4
EXP_PER_GROUP = 8
N_EXPERTS = N_GROUPS * EXP_PER_GROUP

LANE = 128
SUB = 8
HEAD_PAD = LANE
HP = MLA_HEADS * HEAD_PAD
ROPE_X1 = 0
ROPE_X2 = LANE // 2
DEN_LANE = V_DIM
Z_QKV = 6 * 256
Z_KR = Z_QKV + Q_LORA + KV_LORA
Z_W = Z_KR + LANE
QKV_W = Q_LORA + KV_LORA + LANE
VMEM_LIMIT = 52 * 1024 * 1024
PACK_W = D_MODEL // 2
HI_MASK = -65536
RT_ID0, RT_ID1, RT_W0, RT_W1 = 0, 1, 2, 3
RT_ROWS = 48
TOP_K = 2
MERGE_BLK = 256
PLAN_BLK = 256
DFT_SPLIT = 64
ATTN_SHORT_SEQS = 2
ATTN_SUB = 256
SC_CORES = 2
SC_SUBCORES = 16
SC_CHUNK = 128


def _cparams(sem):
    return pltpu.CompilerParams(dimension_semantics=sem, vmem_limit_bytes=VMEM_LIMIT)


def _bdot(a, b):
    return jnp.dot(a, b, preferred_element_type=F32)


def _rms(x, g):
    return x * lax.rsqrt(jnp.mean(x * x, axis=-1, keepdims=True) + EPS) * g


def _ada_kernel(c_ref, w_ref, b_ref, o_ref):
    c = c_ref[...]
    s = (c * jax.nn.sigmoid(c)).astype(BF16)
    o_ref[...] = _bdot(s, w_ref[...].astype(BF16)) + b_ref[...]


def _ada(cond8, w_ada, b_ada):
    nblk = 6 * D_MODEL // 1024
    return pl.pallas_call(
        _ada_kernel,
        grid=(DEPTH, nblk),
        in_specs=[
            pl.BlockSpec((SUB, D_MODEL), lambda l, n: (0, 0)),
            pl.BlockSpec((None, D_MODEL, 1024), lambda l, n: (l, 0, n)),
            pl.BlockSpec((None, 1, 1024), lambda l, n: (l, 0, n)),
        ],
        out_specs=pl.BlockSpec((None, SUB, 1024), lambda l, n: (l, 0, n)),
        out_shape=jax.ShapeDtypeStruct((DEPTH, SUB, 6 * D_MODEL), F32),
        compiler_params=_cparams(("arbitrary", "arbitrary")),
        name="ada",
    )(cond8, w_ada, b_ada.reshape(DEPTH, 1, 6 * D_MODEL))


def _front_kernel(x_ref, mod_ref, g_ref, w_ref, z_ref):
    x = x_ref[...]
    sh = mod_ref[:, 0:D_MODEL]
    sc = mod_ref[:, D_MODEL:2 * D_MODEL]
    h = _rms(x, g_ref[...]) * (1.0 + sc) + sh
    z_ref[...] = _bdot(h.astype(BF16), w_ref[...])


def _front(x, mod, g1, w_in, tm):
    t = x.shape[0]
    per_mod = t // (mod.shape[0] * tm)
    return pl.pallas_call(
        _front_kernel,
        grid=(t // tm,),
        in_specs=[
            pl.BlockSpec((tm, D_MODEL), lambda i: (i, 0)),
            pl.BlockSpec((None, 1, 6 * D_MODEL), lambda i: (i // per_mod, 0, 0)),
            pl.BlockSpec((1, D_MODEL), lambda i: (0, 0)),
            pl.BlockSpec((D_MODEL, Z_W), lambda i: (0, 0)),
        ],
        out_specs=pl.BlockSpec((tm, Z_W), lambda i: (i, 0)),
        out_shape=jax.ShapeDtypeStruct((t, Z_W), F32),
        compiler_params=_cparams(("arbitrary",)),
        name="front",
    )(x, mod, g1, w_in)


def _head_norm_rope_t(x_t, g_col, cos_t, sinr_t):
    ss = jnp.sum(x_t * x_t, axis=0, keepdims=True) * (1.0 / QK_DIM)
    x_t = x_t * lax.rsqrt(ss + EPS) * g_col
    return (x_t * cos_t + pltpu.roll(x_t, HEAD_PAD // 2, 0) * sinr_t).astype(BF16)


def _build_kv(ckv, kr_tile, cos_t, sinr_t, wk_t_ref, wv_ref, gk_col, eye_ref, kt_ref, v_ref):
    nt = (((1,), (1,)), ((), ()))
    cb = ckv.astype(BF16)
    kn_t = lax.dot_general(wk_t_ref[...], cb, nt, preferred_element_type=F32)
    kr_hi = kr_tile.astype(BF16)
    kr_lo = (kr_tile - kr_hi.astype(F32)).astype(BF16)
    kr_t = (lax.dot_general(eye_ref[...], kr_hi, nt, preferred_element_type=F32)
            + lax.dot_general(eye_ref[...], kr_lo, nt, preferred_element_type=F32))
    for h in range(MLA_HEADS):
        sl = slice(h * HEAD_PAD, (h + 1) * HEAD_PAD)
        kt_ref[sl, :] = _head_norm_rope_t(kn_t[sl, :] + kr_t, gk_col, cos_t, sinr_t)
    v = _bdot(cb, wv_ref[...])
    lane = lax.broadcasted_iota(jnp.int32, v.shape, 1)
    v_ref[...] = jnp.where((lane & (HEAD_PAD - 1)) == DEN_LANE, 1.0, v).astype(BF16)


def _qkv_kernel(z_ref, cos_t_ref, sinr_t_ref, gqa_ref, wq_ref, gkva_ref, wk_ref, wv_ref,
                gq_ref, gk_ref, eye_ref, q_ref, k_ref, v_ref, ckv_ref):
    cos_t, sinr_t = cos_t_ref[...], sinr_t_ref[...]
    q_c = z_ref[:, 0:Q_LORA]
    kv_c = z_ref[:, Q_LORA:Q_LORA + KV_LORA]
    kr_tile = z_ref[:, Q_LORA + KV_LORA:QKV_W]
    nt = (((1,), (1,)), ((), ()))
    qf_t = lax.dot_general(wq_ref[...], _rms(q_c, gqa_ref[...]).astype(BF16), nt, preferred_element_type=F32)
    gq_col = gq_ref[...] * (ATTN_SCALE * LOG2E)
    for h in range(MLA_HEADS):
        sl = slice(h * HEAD_PAD, (h + 1) * HEAD_PAD)
        q_ref[sl, :] = _head_norm_rope_t(qf_t[sl, :], gq_col, cos_t, sinr_t)
    ckv = _rms(kv_c, gkva_ref[...])
    ckv_ref[...] = ckv
    _build_kv(ckv, kr_tile, cos_t, sinr_t, wk_ref, wv_ref, gk_ref[...], eye_ref, k_ref, v_ref)


def _qkv(z, tabs, p, seq_len, tm):
    t = z.shape[0]
    if tm > seq_len:
        tabs = [jnp.tile(a, (tm // seq_len, 1)) for a in tabs]
    tabs_t = [a.T for a in tabs]
    per_seq = max(seq_len // tm, 1)
    tab_t_spec = pl.BlockSpec((LANE, tm), lambda i: (0, i % per_seq))
    full = lambda shape: pl.BlockSpec(shape, lambda i: (0,) * len(shape))
    return pl.pallas_call(
        _qkv_kernel,
        grid=(t // tm,),
        in_specs=[
            pl.BlockSpec((tm, QKV_W), lambda i: (i, Z_QKV // QKV_W)),
            tab_t_spec, tab_t_spec,
            full((1, Q_LORA)), full((HP, Q_LORA)), full((1, KV_LORA)),
            full((HP, KV_LORA)), full((KV_LORA, HP)), full((LANE, 1)), full((LANE, 1)), full((LANE, LANE)),
        ],
        out_specs=[
            pl.BlockSpec((HP, tm), lambda i: (0, i)),
            pl.BlockSpec((HP, tm), lambda i: (0, i)),
            pl.BlockSpec((tm, HP), lambda i: (i, 0)),
            pl.BlockSpec((tm, KV_LORA), lambda i: (i, 0)),
        ],
        out_shape=[
            jax.ShapeDtypeStruct((HP, t), BF16),
            jax.ShapeDtypeStruct((HP, t), BF16),
            jax.ShapeDtypeStruct((t, HP), BF16),
            jax.ShapeDtypeStruct((t, KV_LORA), F32),
        ],
        compiler_params=_cparams(("arbitrary",)),
        name="qkv",
    )(z, tabs_t[0], tabs_t[1], p["g_qa"], p["wq_t"], p["g_kva"], p["wk_t"], p["wv"],
      p["gq_col"], p["gk_col"], p["eye"])


def _kvcache_kernel(ckv_ref, kr_ref, cos_t_ref, sinr_t_ref, wk_ref, wv_ref, gk_ref, eye_ref, k_ref, v_ref):
    _build_kv(ckv_ref[...], kr_ref[...], cos_t_ref[...], sinr_t_ref[...], wk_ref, wv_ref, gk_ref[...], eye_ref,
              k_ref, v_ref)


def _kvcache(ckv, kr_tile, tabs, p):
    t = ckv.shape[0]
    return pl.pallas_call(
        _kvcache_kernel,
        out_shape=[jax.ShapeDtypeStruct((HP, t), BF16), jax.ShapeDtypeStruct((t, HP), BF16)],
        compiler_params=_cparams(None),
        name="kvcache",
    )(ckv, kr_tile, tabs[0].T, tabs[1].T, p["wk_t"], p["wv"], p["gk_col"], p["eye"])


def _fourier_kernel(nb, u_ref, cc_ref, sc_ref, cl_ref, sl_ref, o_ref, a_scr, b_scr):
    @pl.when(pl.program_id(1) == 0)
    def _():
        for b in range(nb):
            u = u_ref[b].astype(BF16)
            a_scr[:, b * FNET_W:(b + 1) * FNET_W] = _bdot(u, cc_ref[...]).astype(BF16)
            b_scr[:, b * FNET_W:(b + 1) * FNET_W] = _bdot(u, sc_ref[...]).astype(BF16)

    y = _bdot(cl_ref[...], a_scr[...]) - _bdot(sl_ref[...], b_scr[...])
    for b in range(nb):
        o_ref[b] = y[:, b * FNET_W:(b + 1) * FNET_W].astype(BF16)


def _fourier_split_kernel(nb, u_ref, cc_ref, sc_ref, ca_ref, sa_ref, cb_ref, sb_ref, o_ref,
                          a_scr, b_scr, cl_scr, sl_scr):
    @pl.when(pl.program_id(1) == 0)
    def _():
        for b in range(nb):
            u = u_ref[b].astype(BF16)
            a_scr[:, b * FNET_W:(b + 1) * FNET_W] = _bdot(u, cc_ref[...]).astype(BF16)
            b_scr[:, b * FNET_W:(b + 1) * FNET_W] = _bdot(u, sc_ref[...]).astype(BF16)

    cb, sb = cb_ref[...], sb_ref[...]
    for j in range(ca_ref.shape[0]):
        ca, sa = ca_ref[j:j + 1, :], sa_ref[j:j + 1, :]
        cl_scr[j * DFT_SPLIT:(j + 1) * DFT_SPLIT, :] = (ca * cb - sa * sb).astype(BF16)
        sl_scr[j * DFT_SPLIT:(j + 1) * DFT_SPLIT, :] = (sa * cb + ca * sb).astype(BF16)
    y = _bdot(cl_scr[...], a_scr[...]) - _bdot(sl_scr[...], b_scr[...])
    for b in range(nb):
        o_ref[b] = y[:, b * FNET_W:(b + 1) * FNET_W].astype(BF16)


def _fourier_split(z3, mats, nb, tq):
    s, l, _ = z3.shape
    cc, sc, ca, sa, cb, sb = mats
    full = lambda a: pl.BlockSpec(a.shape, lambda i, r: (0, 0))
    rows = pl.BlockSpec((tq // DFT_SPLIT, l), lambda i, r: (r, 0))
    return pl.pallas_call(
        functools.partial(_fourier_split_kernel, nb),
        grid=(s // nb, l // tq),
        in_specs=[pl.BlockSpec((nb, l, FNET_W), lambda i, r: (i, 0, 0)), full(cc), full(sc), rows, rows,
                  full(cb), full(sb)],
        out_specs=pl.BlockSpec((nb, tq, FNET_W), lambda i, r: (i, r, 0)),
        out_shape=jax.ShapeDtypeStruct((s, l, FNET_W), BF16),
        scratch_shapes=[pltpu.VMEM((l, nb * FNET_W), BF16), pltpu.VMEM((l, nb * FNET_W), BF16),
                        pltpu.VMEM((tq, l), BF16), pltpu.VMEM((tq, l), BF16)],
        compiler_params=_cparams(("arbitrary", "arbitrary")),
        name="fourier_split",
    )(z3, cc, sc, ca, sa, cb, sb)


def _fourier(z3, mats, nb, tq):
    s, l, _ = z3.shape
    if len(mats) == 6:
        return _fourier_split(z3, mats, nb, tq)
    cc, sc, cl, sl = mats
    return pl.pallas_call(
        functools.partial(_fourier_kernel, nb),
        grid=(s // nb, l // tq),
        in_specs=[
            pl.BlockSpec((nb, l, FNET_W), lambda i, r: (i, 0, 0)),
            pl.BlockSpec((FNET_W, FNET_W), lambda i, r: (0, 0)),
            pl.BlockSpec((FNET_W, FNET_W), lambda i, r: (0, 0)),
            pl.BlockSpec((tq, l), lambda i, r: (r, 0)),
            pl.BlockSpec((tq, l), lambda i, r: (r, 0)),
        ],
        out_specs=pl.BlockSpec((nb, tq, FNET_W), lambda i, r: (i, r, 0)),
        out_shape=jax.ShapeDtypeStruct((s, l, FNET_W), BF16),
        scratch_shapes=[pltpu.VMEM((l, nb * FNET_W), BF16), pltpu.VMEM((l, nb * FNET_W), BF16)],
        compiler_params=_cparams(("arbitrary", "arbitrary")),
        name="fourier",
    )(z3, cc, sc, cl, sl)


def _load_ext(ref, c, rows, nchunks):
    t0 = pl.multiple_of(c * rows, rows)
    main = ref[pl.ds(t0, rows), :]
    lo = pl.multiple_of(jnp.maximum(t0 - SUB, 0), SUB)
    hi = pl.multiple_of(jnp.minimum(t0 + rows, (nchunks - 1) * rows + rows - SUB), SUB)
    prev = jnp.where(c > 0, ref[pl.ds(lo, SUB), :], 0.0)
    nxt = jnp.where(c < nchunks - 1, ref[pl.ds(hi, SUB), :], 0.0)
    return jnp.concatenate([prev, main, nxt], axis=0)


def _shifted(ext, off, rows):
    n = ext.shape[0]
    r = ext if off == 0 else pltpu.roll(ext, (-off) % n, 0)
    return r[SUB:SUB + rows]


def _gelu_tanh(x):
    return 0.5 * x * (1.0 + jnp.tanh(math.sqrt(2.0 / math.pi) * (x + 0.044715 * (x * x * x))))


def _lru_gates(xc, d, wa_ref, wx_ref, ba_ref, bx_ref, lam_ref):
    xb = xc.astype(BF16)
    r = jax.nn.sigmoid(_bdot(xb, wa_ref[d]) + ba_ref[d])
    i = jax.nn.sigmoid(_bdot(xb, wx_ref[d]) + bx_ref[d])
    nl = -lam_ref[d]
    softplus = jnp.maximum(nl, 0.0) + jnp.log1p(jnp.exp(-jnp.abs(nl)))
    la = (-LRU_C) * r * softplus
    a = jnp.exp(la)
    one_m_a2 = -jnp.tanh(la) * (a * a + 1.0)
    b = jnp.sqrt(one_m_a2) * (i * xc)
    return a, b


def _group_scan(a, b, reverse):
    rows = a.shape[0]
    rm = lax.broadcasted_iota(jnp.int32, a.shape, 0) & (SUB - 1)
    for s in (1, 2, 4):
        if reverse:
            sh, m = rows - s, rm + s <= SUB - 1
        else:
            sh, m = s, rm >= s
        a_sh = pltpu.roll(a, sh, 0)
        b_sh = pltpu.roll(b, sh, 0)
        b = jnp.where(m, a * b_sh + b, b)
        a = jnp.where(m, a * a_sh, a)
    return a, b


def _carry_scan(a, b, carry, reverse):
    ng = a.shape[0] // SUB
    out = [None] * ng
    order = range(ng - 1, -1, -1) if reverse else range(ng)
    for g in order:
        hg = a[g * SUB:(g + 1) * SUB] * carry + b[g * SUB:(g + 1) * SUB]
        out[g] = hg
        carry = hg[0:1] if reverse else hg[SUB - 1:SUB]
    return jnp.concatenate(out, axis=0), carry


def _lru_kernel(nb, rows, nchunks, xl_ref, gl_ref, bs_ref, cs_ref, xs_ref, h0_ref, cw_ref, wa_ref, wx_ref,
                ba_ref, bx_ref, lam_ref, sw_ref, yl_ref, ys_ref, fin_ref, xc_scr, hf_scr, hb_scr):
    cw = cw_ref[...]
    sw = sw_ref[...]
    chunk = lambda c: pl.ds(pl.multiple_of(c * rows, rows), rows)
    seqs = range(nb)

    def conv_body(c, carry):
        for s in seqs:
            ext = _load_ext(xl_ref.at[s], c, rows, nchunks)
            xc = _shifted(ext, -2, rows) * cw[0:1]
            for k in range(1, 4):
                xc = xc + _shifted(ext, k - 2, rows) * cw[k:k + 1]
            xc_scr[s, chunk(c), :] = xc
        return carry

    lax.fori_loop(0, nchunks, conv_body, 0)

    def direction(s, c, d, carry, h_scr):
        a, b = _lru_gates(xc_scr[s, chunk(c), :], d, wa_ref, wx_ref, ba_ref, bx_ref, lam_ref)
        a, b = _group_scan(a, b, d == 1)
        h, carry = _carry_scan(a, b, carry, d == 1)
        h_scr[s, chunk(c), :] = h
        return carry

    def scan_body(j, carries):
        return tuple((direction(s, j, 0, cf, hf_scr), direction(s, nchunks - 1 - j, 1, cb, hb_scr))
                     for s, (cf, cb) in zip(seqs, carries))

    init = tuple((h0_ref[s, 0:1, :], h0_ref[s, 1:2, :]) for s in seqs)
    for s, (cf, cb) in zip(seqs, lax.fori_loop(0, nchunks, scan_body, init)):
        fin_ref[s, 0:1, :] = cf
        fin_ref[s, 1:2, :] = cb

    def out_body(c, carry):
        for s in seqs:
            y = (hf_scr[s, chunk(c), :] + hb_scr[s, chunk(c), :]) * _gelu_tanh(gl_ref[s, chunk(c), :])
            yl_ref[s, chunk(c), :] = y.astype(BF16)
            ext = _load_ext(cs_ref.at[s], c, rows, nchunks) * _load_ext(xs_ref.at[s], c, rows, nchunks)
            acc = _shifted(ext, -1, rows) * sw[0:1]
            acc = acc + _shifted(ext, 0, rows) * sw[1:2]
            acc = acc + _shifted(ext, 1, rows) * sw[2:3]
            ys_ref[s, chunk(c), :] = (bs_ref[s, chunk(c), :] * acc).astype(BF16)
        return carry

    lax.fori_loop(0, nchunks, out_body, 0)


def _lru(z3, h0, p, nb):
    s, l, _ = z3.shape
    rows = min(l, 1024)
    nchunks = l // rows
    half = lambda blk: pl.BlockSpec((nb, l, LANE), lambda i, j: (i, 0, blk + j))
    wspec = lambda shape: pl.BlockSpec((None,) + shape, lambda i, j: (j,) + (0,) * len(shape))
    seq_half = pl.BlockSpec((nb, l, LANE), lambda i, j: (i, 0, j))
    state = pl.BlockSpec((nb, 2, LANE), lambda i, j: (i, 0, j))
    return pl.pallas_call(
        functools.partial(_lru_kernel, nb, rows, nchunks),
        grid=(s // nb, 2),
        in_specs=[
            half(2), half(4), half(6), half(8), half(10), state,
            wspec((4, LANE)), wspec((2, LANE, LANE)), wspec((2, LANE, LANE)),
            wspec((2, 1, LANE)), wspec((2, 1, LANE)), wspec((2, 1, LANE)), wspec((3, LANE)),
        ],
        out_specs=[seq_half, seq_half, state],
        out_shape=[jax.ShapeDtypeStruct((s, l, LRU_W), BF16), jax.ShapeDtypeStruct((s, l, SC_W), BF16),
                   jax.ShapeDtypeStruct((s, 2, LRU_W), F32)],
        scratch_shapes=[pltpu.VMEM((nb, l, LANE), F32)] * 3,
        compiler_params=_cparams(("arbitrary", "arbitrary")),
        name="lru",
    )(z3, z3, z3, z3, z3, h0, p["lru_conv"], p["lru_wa"], p["lru_wx"], p["lru_ba"], p["lru_bx"], p["lru_lam"],
      p["sc_conv"])


def _store_head_pair(o_ref, rows, j, o_even, o_odd):
    lane = lax.broadcasted_iota(jnp.int32, o_even.shape, 1)
    pair = jnp.where(lane < V_DIM, o_even, pltpu.roll(o_odd, V_DIM, 1))
    o_ref[rows, j * LANE:(j + 1) * LANE] = pair.astype(BF16)


def _attn_kernel(hp, sub, has_cache, *refs):
    if has_cache:
        q_ref, k_ref, v_ref, kc_ref, vc_ref, o_ref, s_scr, p_scr, m_scr, sc_scr, pc_scr = refs
    else:
        q_ref, k_ref, v_ref, o_ref, s_scr, p_scr, m_scr = refs
    tn = (((0,), (0,)), ((), ()))
    items = [(slice(r * sub, (r + 1) * sub), slice(h * HEAD_PAD, (h + 1) * HEAD_PAD))
             for r in range(q_ref.shape[1] // sub) for h in range(hp)]
    n = len(items)

    def scores(i):
        rows, head = items[i]
        q_t = q_ref[head, rows]
        s = lax.dot_general(q_t, k_ref[head, :], tn, preferred_element_type=F32)
        m = jnp.max(s, axis=-1, keepdims=True)
        s_scr[i % 2] = s
        if has_cache:
            sc = lax.dot_general(q_t, kc_ref[head, :], tn, preferred_element_type=F32)
            m = jnp.maximum(m, jnp.max(sc, axis=-1, keepdims=True))
            sc_scr[i % 2] = sc
        m_scr[i % 2] = jnp.broadcast_to(m, m_scr.shape[1:])

    def probs(i):
        m = m_scr[i % 2][:, 0:1]
        p_scr[i % 2] = jnp.exp2(s_scr[i % 2] - m).astype(BF16)
        if has_cache:
            pc_scr[i % 2] = jnp.exp2(sc_scr[i % 2] - m).astype(BF16)

    pending = {}

    def weighted_values(i):
        rows, head = items[i]
        o = _bdot(p_scr[i % 2], v_ref[:, head])
        if has_cache:
            o = o + _bdot(pc_scr[i % 2], vc_ref[:, head])
        o = o / o[:, DEN_LANE:DEN_LANE + 1]
        h = i % hp
        if h % 2 == 0:
            pending[0] = o
        else:
            _store_head_pair(o_ref, rows, h // 2, pending.pop(0), o)

    scores(0)
    for i in range(n):
        if i + 1 < n:
            scores(i + 1)
        if i > 0:
            weighted_values(i - 1)
        probs(i)
    weighted_values(n - 1)


def _attn_short_kernel(hp, q_ref, k_ref, v_ref, o_ref):
    tn = (((0,), (0,)), ((), ()))
    nseq, l, _ = v_ref.shape
    items = [(b, slice(b * l, (b + 1) * l), slice(h * HEAD_PAD, (h + 1) * HEAD_PAD))
             for b in range(nseq) for h in range(hp)]
    scores = [lax.dot_general(q_ref[hd, cols], k_ref[hd, cols], tn, preferred_element_type=F32)
              for _, cols, hd in items]
    probs = [jnp.exp2(s - jnp.max(s, axis=-1, keepdims=True)).astype(BF16) for s in scores]
    outs = []
    for (b, _, hd), p in zip(items, probs):
        o = _bdot(p, v_ref[b, :, hd])
        outs.append(o / o[:, DEN_LANE:DEN_LANE + 1])
    for b in range(nseq):
        for j in range(hp // 2):
            _store_head_pair(o_ref.at[b], slice(None), j, outs[b * hp + 2 * j], outs[b * hp + 2 * j + 1])


def _attn(q, k, v, cache, hp, tq):
    s, l, _ = v.shape
    w = hp * HEAD_PAD
    qspec = pl.BlockSpec((w, tq), lambda i, h, r: (h, i * (l // tq) + r))
    ospec = pl.BlockSpec((None, tq, hp * V_DIM), lambda i, h, r: (i, r, h))
    if cache is None and tq == l:
        nseq = ATTN_SHORT_SEQS
        kspec = pl.BlockSpec((w, nseq * l), lambda i, h, r: (h, i))
        vspec = pl.BlockSpec((nseq, l, w), lambda i, h, r: (i, 0, h))
        return pl.pallas_call(
            functools.partial(_attn_short_kernel, hp),
            grid=(s // nseq, MLA_HEADS // hp, 1),
            in_specs=[kspec, kspec, vspec],
            out_specs=pl.BlockSpec((nseq, l, hp * V_DIM), lambda i, h, r: (i, 0, h)),
            out_shape=jax.ShapeDtypeStruct((s, l, MLA_OUT), BF16),
            compiler_params=_cparams(("arbitrary", "arbitrary", "arbitrary")),
            name="attn_short",
        )(q, k, v)
    kv_mode = dict(pipeline_mode=pl.Buffered(1))
    kspec = pl.BlockSpec((w, l), lambda i, h, r: (h, i), **kv_mode)
    vspec = pl.BlockSpec((None, l, w), lambda i, h, r: (i, 0, h), **kv_mode)
    in_specs = [qspec, kspec, vspec]
    args = [q, k, v]
    if cache is not None:
        lc = cache[1].shape[1]
        in_specs += [pl.BlockSpec((w, lc), lambda i, h, r: (h, i), **kv_mode),
                     pl.BlockSpec((None, lc, w), lambda i, h, r: (i, 0, h), **kv_mode)]
        args += list(cache)
    sub = min(tq, ATTN_SUB)
    scratch = [pltpu.VMEM((2, sub, l), F32), pltpu.VMEM((2, sub, l), BF16), pltpu.VMEM((2, sub, LANE), F32)]
    if cache is not None:
        scratch += [pltpu.VMEM((2, sub, lc), F32), pltpu.VMEM((2, sub, lc), BF16)]
    return pl.pallas_call(
        functools.partial(_attn_kernel, hp, sub, cache is not None),
        grid=(s, MLA_HEADS // hp, l // tq),
        in_specs=in_specs,
        out_specs=ospec,
        out_shape=jax.ShapeDtypeStruct((s, l, MLA_OUT), BF16),
        scratch_shapes=scratch,
        compiler_params=_cparams(("arbitrary", "arbitrary", "arbitrary")),
        name="attn",
    )(*args)


def _route(lt):
    row = lax.broadcasted_iota(jnp.int32, (SUB, lt.shape[1]), 0).astype(F32)
    neg = -jnp.inf
    gl = lt[0:SUB]
    g_ok = row < N_GROUPS
    glm = jnp.where(g_ok, gl, neg)
    gmax = jnp.max(glm, axis=0, keepdims=True)
    gsel = jnp.min(jnp.where(glm == gmax, row, float(SUB)), axis=0, keepdims=True)
    g_w = 1.0 / jnp.sum(jnp.where(g_ok, jnp.exp(gl - gmax), 0.0), axis=0, keepdims=True)
    es = lt[SUB:2 * SUB]
    for g in range(1, N_GROUPS):
        es = jnp.where(gsel == float(g), lt[(g + 1) * SUB:(g + 2) * SUB], es)
    m1 = jnp.max(es, axis=0, keepdims=True)
    i1 = jnp.min(jnp.where(es == m1, row, float(SUB)), axis=0, keepdims=True)
    es2 = jnp.where(row == i1, neg, es)
    m2 = jnp.max(es2, axis=0, keepdims=True)
    i2 = jnp.min(jnp.where(es2 == m2, row, float(SUB)), axis=0, keepdims=True)
    e2 = jnp.exp(m2 - m1)
    inv = g_w / (1.0 + e2)
    base = gsel * EXP_PER_GROUP
    return (jnp.where(row == RT_ID0, base + i1, 0.0) + jnp.where(row == RT_ID1, base + i2, 0.0)
            + jnp.where(row == RT_W0, inv, 0.0) + jnp.where(row == RT_W1, inv * e2, 0.0))


def _pack_bf16_pairs(x):
    k = x.shape[1] // 2
    bits = lax.bitcast_convert_type(x.astype(BF16).astype(F32), jnp.int32)
    return lax.shift_right_logical(bits[:, :k], 16) | (bits[:, k:] & HI_MASK)


def _unpack_bf16_pairs(w):
    lo = lax.bitcast_convert_type(lax.shift_left(w, 16), F32).astype(BF16)
    hi = lax.bitcast_convert_type(w & HI_MASK, F32).astype(BF16)
    return lo, hi


def _back_kernel(x_ref, mod_ref, g1_ref, g2_ref, yf_ref, yl_ref, ys_ref, o_ref,
                 wg_ref, wpf_ref, wpl_ref, wps_ref, wpm_ref, wo_ref, wrh_ref, wrl_ref, br_ref,
                 x1_ref, h2_ref, comb_ref):
    d = D_MODEL
    x = x_ref[...]
    sh1, sc1, gt1 = mod_ref[:, 0:d], mod_ref[:, d:2 * d], mod_ref[:, 2 * d:3 * d]
    sh2, sc2 = mod_ref[:, 3 * d:4 * d], mod_ref[:, 4 * d:5 * d]
    hb = (_rms(x, g1_ref[...]) * (1.0 + sc1) + sh1).astype(BF16)
    branches = ((yf_ref, wpf_ref), (yl_ref, wpl_ref), (ys_ref, wps_ref), (o_ref, wpm_ref))
    blocks = []
    for n in range(0, d, MERGE_BLK):
        acc = None
        for j, (y_ref, w_ref) in enumerate(branches):
            gate = jax.nn.sigmoid(_bdot(hb, wg_ref[:, j * d + n:j * d + n + MERGE_BLK]))
            term = gate * _bdot(y_ref[...], w_ref[:, n:n + MERGE_BLK])
            acc = term if acc is None else acc + term
        blocks.append(acc.astype(BF16))
    x1 = x + gt1 * _bdot(jnp.concatenate(blocks, axis=1), wo_ref[...])
    x1_ref[...] = x1
    h2 = _rms(x1, g2_ref[...]) * (1.0 + sc2) + sh2
    h2b = h2.astype(BF16)
    h2_ref[...] = _pack_bf16_pairs(h2)
    h2l = (h2 - h2b.astype(F32)).astype(BF16)
    nt = (((1,), (1,)), ((), ()))
    rdot = lambda w_ref, h: lax.dot_general(w_ref[...], h, nt, preferred_element_type=F32)
    logits_t = rdot(wrh_ref, h2b) + rdot(wrh_ref, h2l) + rdot(wrl_ref, h2b) + br_ref[...]
    comb_ref[...] = _route(logits_t)


def _back(x, mod, yf, yl, ys, o, p, tm):
    t = x.shape[0]
    per_mod = t // (mod.shape[0] * tm)
    row = lambda w: pl.BlockSpec((tm, w), lambda i: (i, 0))
    full = lambda a: pl.BlockSpec(a.shape, lambda i: (0,) * a.ndim, pipeline_mode=pl.Buffered(1))
    weights = [p["w_gate"], p["w_pf"], p["w_pl"], p["w_ps"], p["w_pm"], p["w_out"], p["wr_hi"], p["wr_lo"], p["b_r"]]
    return pl.pallas_call(
        _back_kernel,
        grid=(t // tm,),
        in_specs=[
            row(D_MODEL),
            pl.BlockSpec((None, 1, 6 * D_MODEL), lambda i: (i // per_mod, 0, 0)),
            pl.BlockSpec((1, D_MODEL), lambda i: (0, 0)),
            pl.BlockSpec((1, D_MODEL), lambda i: (0, 0)),
            row(FNET_W), row(LRU_W), row(SC_W), row(MLA_OUT),
        ] + [full(w) for w in weights],
        out_specs=[row(D_MODEL), row(PACK_W), pl.BlockSpec((SUB, tm), lambda i: (0, i))],
        out_shape=[
            jax.ShapeDtypeStruct((t, D_MODEL), F32),
            jax.ShapeDtypeStruct((t, PACK_W), jnp.int32),
            jax.ShapeDtypeStruct((SUB, t), F32),
        ],
        compiler_params=_cparams(("arbitrary",)),
        name="back",
    )(x, mod, p["norm1_g"], p["norm2_g"], yf, yl, ys, o, *weights)


def _plan_kernel(tm, rt_ref, pos_ref, te_ref, nu_ref):
    ids = jnp.concatenate([rt_ref[RT_ID0:RT_ID0 + 1, :], rt_ref[RT_ID1:RT_ID1 + 1, :]], axis=1)
    n_rows = ids.shape[1]
    erow = lax.broadcasted_iota(jnp.int32, (N_EXPERTS, PLAN_BLK), 0).astype(F32)
    col = lax.broadcasted_iota(jnp.int32, (N_EXPERTS, LANE), 0)
    blocks = [(ids[:, b * PLAN_BLK:(b + 1) * PLAN_BLK] == erow).astype(F32) for b in range(n_rows // PLAN_BLK)]
    tri = (lax.broadcasted_iota(jnp.int32, (PLAN_BLK, PLAN_BLK), 0)
           <= lax.broadcasted_iota(jnp.int32, (PLAN_BLK, PLAN_BLK), 1)).astype(BF16)
    prefix = [_bdot(blk.astype(BF16), tri) for blk in blocks]
    counts = prefix[0][:, PLAN_BLK - 1:PLAN_BLK]
    for pre in prefix[1:]:
        counts = counts + pre[:, PLAN_BLK - 1:PLAN_BLK]
    tiles = jnp.floor((counts + float(tm - 1)) * (1.0 / tm))
    tile_end = jnp.broadcast_to(tiles, (N_EXPERTS, LANE))
    for s in (1, 2, 4, 8, 16):
        tile_end = tile_end + jnp.where(col >= s, pltpu.roll(tile_end, s, 0), 0.0)
    base = (tile_end[:, 0:1] - tiles) * float(tm) - 1.0
    for b, (blk, pre) in enumerate(zip(blocks, prefix)):
        pos = jnp.sum(blk * (base + pre), axis=0, keepdims=True)
        pos_ref[:, b * PLAN_BLK:(b + 1) * PLAN_BLK] = pos.astype(jnp.int32)
        base = base + pre[:, PLAN_BLK - 1:PLAN_BLK]
    k = lax.broadcasted_iota(jnp.int32, (N_EXPERTS, LANE), 1).astype(F32)
    owner = jnp.sum((k >= tile_end).astype(F32), axis=0, keepdims=True)
    te_ref[...] = jnp.minimum(owner, float(N_EXPERTS - 1)).astype(jnp.int32)
    nu_ref[...] = jnp.broadcast_to(tile_end[N_EXPERTS - 1:N_EXPERTS, :], nu_ref.shape).astype(jnp.int32)


def _dispatch_plan(rt, tm):
    n_rows = TOP_K * rt.shape[1]
    n_tiles = n_rows // tm + N_EXPERTS
    assert n_tiles <= LANE and tm & (tm - 1) == 0 and n_rows % PLAN_BLK == 0
    pos, te, nu = pl.pallas_call(
        functools.partial(_plan_kernel, tm),
        out_shape=[jax.ShapeDtypeStruct((1, n_rows), jnp.int32), jax.ShapeDtypeStruct((1, LANE), jnp.int32),
                   jax.ShapeDtypeStruct((1, LANE), jnp.int32)],
        compiler_params=_cparams(None),
        name="plan",
    )(rt)
    return pos.reshape(n_rows), te[0, :n_tiles], nu[0, :1], n_tiles


def _sc_worker_rows(n_rows):
    workers = SC_CORES * SC_SUBCORES
    per_w = n_rows // workers
    assert per_w * workers == n_rows and per_w % SC_CHUNK == 0
    return per_w


def _sc_mesh():
    return plsc.VectorSubcoreMesh(core_axis_name="c", subcore_axis_name="s",
                                  num_cores=SC_CORES, num_subcores=SC_SUBCORES)


def _sc_scatter_rows(src, pos, n_out):
    t, w = src.shape
    per_w = _sc_worker_rows(pos.shape[0])

    @functools.partial(
        pl.kernel, mesh=_sc_mesh(), out_type=jax.ShapeDtypeStruct((n_out, w), src.dtype),
        scratch_types=[pltpu.VMEM((SC_CHUNK,), jnp.int32), pltpu.VMEM((SC_CHUNK, w), src.dtype)],
        name="moe_dispatch")
    def run(src_hbm, pos_hbm, out_hbm, idx_v, rows_v):
        wid = lax.axis_index("s") * SC_CORES + lax.axis_index("c")

        @pl.loop(0, per_w // SC_CHUNK)
        def _(i):
            j0 = wid * per_w + i * SC_CHUNK
            pltpu.sync_copy(pos_hbm.at[pl.ds(j0, SC_CHUNK)], idx_v)
            pltpu.sync_copy(src_hbm.at[pl.ds(lax.rem(j0, t), SC_CHUNK)], rows_v)
            pltpu.sync_copy(rows_v, out_hbm.at[idx_v])

    return run(src, pos)


def _sc_gather_rows(src, pos):
    w = src.shape[1]
    n = pos.shape[0]
    per_w = _sc_worker_rows(n)

    @functools.partial(
        pl.kernel, mesh=_sc_mesh(), out_type=jax.ShapeDtypeStruct((n, w), src.dtype),
        scratch_types=[pltpu.VMEM((SC_CHUNK,), jnp.int32), pltpu.VMEM((SC_CHUNK, w), src.dtype)],
        name="moe_return")
    def run(src_hbm, pos_hbm, out_hbm, idx_v, rows_v):
        wid = lax.axis_index("s") * SC_CORES + lax.axis_index("c")

        @pl.loop(0, per_w // SC_CHUNK)
        def _(i):
            j0 = wid * per_w + i * SC_CHUNK
            pltpu.sync_copy(pos_hbm.at[pl.ds(j0, SC_CHUNK)], idx_v)
            pltpu.sync_copy(src_hbm.at[idx_v], rows_v)
            pltpu.sync_copy(rows_v, out_hbm.at[pl.ds(j0, SC_CHUNK)])

    return run(src, pos)


def _experts_kernel(te_ref, nu_ref, x_ref, w1_ref, w3_ref, w2_ref, y_ref):
    half = D_MODEL // 2

    @pl.when(pl.program_id(0) < nu_ref[0])
    def _():
        lo, hi = _unpack_bf16_pairs(x_ref[...])
        a = _bdot(lo, w1_ref[:half, :].astype(BF16)) + _bdot(hi, w1_ref[half:, :].astype(BF16))
        u = _bdot(lo, w3_ref[:half, :].astype(BF16)) + _bdot(hi, w3_ref[half:, :].astype(BF16))
        mid = (a * jax.nn.sigmoid(a)) * u
        y_ref[...] = _pack_bf16_pairs(_bdot(mid.astype(BF16), w2_ref[...].astype(BF16)))


def _experts(xs, tile_expert, n_used, n_tiles, w1, w3, w2, layer, tm):
    wspec = lambda a: pl.BlockSpec((None, None) + a.shape[2:], lambda i, te, nu: (layer, te[i], 0, 0))
    grid_spec = pltpu.PrefetchScalarGridSpec(
        num_scalar_prefetch=2,
        grid=(n_tiles,),
        in_specs=[pl.BlockSpec((tm, PACK_W), lambda i, te, nu: (jnp.minimum(i, nu[0] - 1), 0)),
                  wspec(w1), wspec(w3), wspec(w2)],
        out_specs=pl.BlockSpec((tm, PACK_W), lambda i, te, nu: (jnp.where(i < nu[0], i, n_tiles), 0)),
    )
    return pl.pallas_call(
        _experts_kernel,
        grid_spec=grid_spec,
        out_shape=jax.ShapeDtypeStruct((xs.shape[0] + tm, PACK_W), jnp.int32),
        compiler_params=_cparams(("arbitrary",)),
        name="experts",
    )(tile_expert, n_used, xs, w1, w3, w2)


def _combine_kernel(x1_ref, mod_ref, rt_ref, g0_ref, g1_ref, o_ref):
    rt = rt_ref[...]
    lo0, hi0 = _unpack_bf16_pairs(g0_ref[...])
    lo1, hi1 = _unpack_bf16_pairs(g1_ref[...])
    w0 = rt[:, RT_W0:RT_W0 + 1]
    w1 = rt[:, RT_W1:RT_W1 + 1]
    half = D_MODEL // 2
    gt2 = mod_ref[:, 5 * D_MODEL:6 * D_MODEL]
    o_ref[:, :half] = x1_ref[:, :half] + gt2[:, :half] * (w0 * lo0.astype(F32) + w1 * lo1.astype(F32))
    o_ref[:, half:] = x1_ref[:, half:] + gt2[:, half:] * (w0 * hi0.astype(F32) + w1 * hi1.astype(F32))


def _combine(x1, mod, rt, g, tm):
    t = x1.shape[0]
    per_mod = t // (mod.shape[0] * tm)
    return pl.pallas_call(
        _combine_kernel,
        grid=(t // tm,),
        in_specs=[
            pl.BlockSpec((tm, D_MODEL), lambda i: (i, 0)),
            pl.BlockSpec((None, 1, 6 * D_MODEL), lambda i: (i // per_mod, 0, 0)),
            pl.BlockSpec((tm, LANE), lambda i: (i, 0)),
            pl.BlockSpec((tm, PACK_W), lambda i: (i, 0)),
            pl.BlockSpec((tm, PACK_W), lambda i: (i + t // tm, 0)),
        ],
        out_specs=pl.BlockSpec((tm, D_MODEL), lambda i: (i, 0)),
        out_shape=jax.ShapeDtypeStruct((t, D_MODEL), F32),
        compiler_params=_cparams(("arbitrary",)),
        name="combine",
    )(x1, mod, rt, g, g)


def _moe(h2p, rt, x1, mod, p, tm_e, tm_c):
    pos, tile_expert, n_used, n_tiles = _dispatch_plan(rt, tm_e)
    xs = _sc_scatter_rows(h2p, pos, n_tiles * tm_e)
    ys = _experts(xs, tile_expert, n_used, n_tiles, p["w1"], p["w3"], p["w2"], p["layer"], tm_e)
    g = _sc_gather_rows(ys, pos)
    rt_cols = jnp.pad(rt.T, ((0, 0), (0, LANE - rt.shape[0])))
    return _combine(x1, mod, rt_cols, g, tm_c)


def _channel_dft():
    n = np.arange(FNET_W // FNET_GROUPS)
    ang = 2.0 * np.pi * ((n[:, None] * n[None, :]) % n.size) / n.size
    eye = np.eye(FNET_GROUPS)
    scale = 1.0 / math.sqrt(n.size)
    return (jnp.asarray(np.kron(eye, np.cos(ang) * scale), F32).astype(BF16),
            jnp.asarray(np.kron(eye, np.sin(ang) * scale), F32).astype(BF16))


def _position_dft(l):
    scale = 1.0 / math.sqrt(l)
    if l <= 256:
        n = np.arange(l)
        ang = 2.0 * np.pi * ((n[:, None] * n[None, :]) % l) / l
        return jnp.asarray(np.cos(ang) * scale, F32).astype(BF16), jnp.asarray(np.sin(ang) * scale, F32).astype(BF16)
    m = DFT_SPLIT
    n = np.arange(l)
    ang_a = 2.0 * np.pi * ((np.arange(l // m)[:, None] * m * n[None, :]) % l) / l
    ang_b = 2.0 * np.pi * ((np.arange(m)[:, None] * n[None, :]) % l) / l
    return (jnp.asarray(np.cos(ang_a), F32), jnp.asarray(np.sin(ang_a), F32),
            jnp.asarray(np.cos(ang_b) * scale, F32), jnp.asarray(np.sin(ang_b) * scale, F32))


def _rope_tables(l, rotate):
    cos = np.ones((l, LANE))
    sinr = np.zeros((l, LANE))
    if rotate:
        t = np.arange(l)
        inv = ROPE_BASE ** (-np.arange(N_FREQ) / N_FREQ)
        ang = np.concatenate([(t // GRID_W)[:, None] * inv, (t % GRID_W)[:, None] * inv], axis=-1)
        half = ROPE // 2
        for lo in (ROPE_X1, ROPE_X2):
            cos[:, lo:lo + half] = np.cos(ang)
        sinr[:, ROPE_X1:ROPE_X1 + half] = -np.sin(ang)
        sinr[:, ROPE_X2:ROPE_X2 + half] = np.sin(ang)
    return tuple(jnp.asarray(a, F32) for a in (cos, sinr))


def _head_lane_source():
    half = ROPE // 2
    src = np.full((HEAD_PAD,), QK_DIM, np.int32)
    src[ROPE_X1:ROPE_X1 + half] = NOPE + np.arange(half)
    src[ROPE_X2:ROPE_X2 + half] = NOPE + half + np.arange(half)
    free = [i for i in range(HEAD_PAD) if src[i] == QK_DIM][:NOPE]
    src[free] = np.arange(NOPE)
    return src


def _place_head_dims(w, n_src):
    src = _head_lane_source()
    src = np.where(src < n_src, src, n_src)
    wz = jnp.concatenate([w[..., :n_src], jnp.zeros(w.shape[:-1] + (1,), w.dtype)], axis=-1)
    out = jnp.take(wz, jnp.asarray(src), axis=-1)
    return out.reshape(out.shape[:-2] + (out.shape[-2] * HEAD_PAD,))


def _place_rope_key(kr):
    half = ROPE // 2
    out = jnp.zeros(kr.shape[:-1] + (LANE,), kr.dtype)
    return out.at[..., ROPE_X1:ROPE_X1 + half].set(kr[..., :half]).at[..., ROPE_X2:ROPE_X2 + half].set(kr[..., half:])


def _take_rope_key(tile):
    half = ROPE // 2
    return jnp.concatenate([tile[..., ROPE_X1:ROPE_X1 + half], tile[..., ROPE_X2:ROPE_X2 + half]], axis=-1)


def _pad_heads(w, lo, hi):
    r = w.shape[0]
    part = w[:, :, lo:hi]
    out = jnp.zeros((r, MLA_HEADS, HEAD_PAD), w.dtype).at[:, :, : hi - lo].set(part)
    return out.reshape(r, HP)


def _blockdiag_halves(w):
    bw = LRU_W // LRU_BLOCKS
    out = jnp.zeros((2, 2, LANE, LANE), w.dtype)
    for half in range(2):
        for k in range(2):
            n = 2 * half + k
            out = out.at[half, :, k * bw:(k + 1) * bw, k * bw:(k + 1) * bw].set(w[:, n])
    return out


def _halves(v):
    return jnp.moveaxis(v.reshape(v.shape[:-1] + (2, LANE)), -2, 0)


def _layer_params(l, a):
    d = D_MODEL
    w_in = a["w_in"][l]
    w_in = jnp.concatenate([w_in[:, :Z_KR], _place_rope_key(w_in[:, Z_KR:])], axis=1)
    w_r = jnp.zeros((RT_ROWS, d), F32).at[:N_GROUPS].set(a["w_gr"][l].T).at[SUB:SUB + N_EXPERTS].set(a["w_er"][l].T)
    b_r = jnp.zeros((RT_ROWS, 1), F32).at[:N_GROUPS, 0].set(a["b_gr"][l]).at[SUB:SUB + N_EXPERTS, 0].set(a["b_er"][l])
    wr_hi = w_r.astype(BF16)
    gpad = lambda g: jnp.take(jnp.concatenate([g, jnp.zeros((1,), F32)]), jnp.asarray(_head_lane_source())).reshape(1, LANE)
    return {
        "norm1_g": a["norm1_g"][l].reshape(1, d), "norm2_g": a["norm2_g"][l].reshape(1, d),
        "w_in": w_in.astype(BF16),
        "lru_conv": _halves(a["lru_conv"][l]),
        "lru_wa": _blockdiag_halves(a["lru_wa"][l]).astype(BF16),
        "lru_wx": _blockdiag_halves(a["lru_wx"][l]).astype(BF16),
        "lru_ba": _halves(a["lru_ba"][l])[:, :, None, :], "lru_bx": _halves(a["lru_bx"][l])[:, :, None, :],
        "lru_lam": _halves(a["lru_lam"][l])[:, :, None, :],
        "sc_conv": _halves(a["sc_conv"][l]),
        "g_qa": a["g_qa"][l].reshape(1, Q_LORA), "g_kva": a["g_kva"][l].reshape(1, KV_LORA),
        "wq_t": _place_head_dims(a["w_qb"][l], QK_DIM).T.astype(BF16),
        "wk_t": _place_head_dims(a["w_kvb"][l], NOPE).T.astype(BF16),
        "wv": _pad_heads(a["w_kvb"][l], NOPE, NOPE + V_DIM).astype(BF16),
        "gq_col": gpad(a["g_qn"][l]).reshape(LANE, 1), "gk_col": gpad(a["g_kn"][l]).reshape(LANE, 1),
        "eye": jnp.eye(LANE, dtype=F32).astype(BF16),
        "w_gate": a["w_gate"][l].astype(BF16),
        "w_pf": a["w_pf"][l].astype(BF16), "w_pl": a["w_pl"][l].astype(BF16), "w_ps": a["w_ps"][l].astype(BF16),
        "w_pm": a["w_pm"][l].astype(BF16), "w_out": a["w_out"][l].astype(BF16),
        "wr_hi": wr_hi, "wr_lo": (w_r - wr_hi.astype(F32)).astype(BF16), "b_r": b_r,
        "w1": a["w1"], "w3": a["w3"], "w2": a["w2"], "layer": l,
    }


def _trunk_layer(x, mod, p, consts, h0, cache, cfg):
    s, l = cfg["s"], cfg["l"]
    z = _front(x, mod, p["norm1_g"], p["w_in"], cfg["tm_front"])
    z3 = z.reshape(s, l, Z_W)
    q, k, v, ckv = _qkv(z, consts["rope"], p, l, cfg["tm_qkv"])
    yf = _fourier(z3, consts["dft"], cfg["nb"], cfg["tq_f"])
    yl, ys, fin = _lru(z3, h0, p, cfg["nb_lru"])
    o = _attn(q, k, v.reshape(s, l, HP), cache, cfg["hp"], cfg["tq_a"])
    t = s * l
    x1, h2, comb = _back(x, mod, yf.reshape(t, FNET_W), yl.reshape(t, LRU_W), ys.reshape(t, SC_W),
                         o.reshape(t, MLA_OUT), p, cfg["tm_back"])
    x2 = _moe(h2, comb, x1, mod, p, cfg["tm_moe"], cfg["tm_comb"])
    return x2, z3, ckv, fin


def kernel(x_prompt, x_sample, cache_ckv, cache_krope, state_rglru, c, c_ctx, norm1_g, norm2_g, w_ada, b_ada, w_in, lru_conv, lru_wa, lru_ba, lru_wx, lru_bx, lru_lam, sc_conv, g_qa, w_qb, g_kva, w_kvb, g_qn, g_kn, w_pf, w_pl, w_ps, w_pm, w_gate, w_out, w_gr, b_gr, w_er, b_er, w1, w3, w2):
    a = dict(norm1_g=norm1_g, norm2_g=norm2_g, w_in=w_in, lru_conv=lru_conv, lru_wa=lru_wa, lru_ba=lru_ba,
             lru_wx=lru_wx, lru_bx=lru_bx, lru_lam=lru_lam, sc_conv=sc_conv, g_qa=g_qa, w_qb=w_qb, g_kva=g_kva,
             w_kvb=w_kvb, g_qn=g_qn, g_kn=g_kn, w_pf=w_pf, w_pl=w_pl, w_ps=w_ps, w_pm=w_pm, w_gate=w_gate,
             w_out=w_out, w_gr=w_gr, b_gr=b_gr, w_er=w_er, b_er=b_er, w1=w1, w3=w3, w2=w2)
    bc, lc, d = x_prompt.shape
    bl, ll, _ = x_sample.shape
    past = cache_ckv.shape[2]

    cond8 = jnp.zeros((SUB, d), F32).at[0].set(c_ctx).at[1:1 + bl].set(c)
    mod_all = _ada(cond8, w_ada, b_ada)

    cc, sc = _channel_dft()
    ctx_consts = {"dft": (cc, sc) + _position_dft(lc), "rope": _rope_tables(lc, False)}
    lat_consts = {"dft": (cc, sc) + _position_dft(ll), "rope": _rope_tables(ll, True)}
    cache_tabs = _rope_tables(bl * past, False)
    ctx_cfg = dict(s=bc, l=lc, tm_front=1024, tm_qkv=1024, nb=8, nb_lru=4, tq_f=lc, hp=MLA_HEADS, tq_a=lc, tm_back=1024, tm_moe=512, tm_comb=1024)
    lat_cfg = dict(s=bl, l=ll, tm_front=1024, tm_qkv=1024, nb=bl, nb_lru=1, tq_f=512, hp=MLA_HEADS, tq_a=256, tm_back=1024, tm_moe=512, tm_comb=1024)

    xp = x_prompt.reshape(bc * lc, d)
    xs = x_sample.reshape(bl * ll, d)
    h0_ctx = jnp.zeros((bc, 2, LRU_W), F32)
    ckv_list, kr_list, lru_list = [], [], []
    for l in range(DEPTH):
        p = _layer_params(l, a)
        mod_ctx = mod_all[l, 0:1].reshape(1, 1, 6 * d)
        mod_lat = mod_all[l, 1:1 + bl].reshape(bl, 1, 6 * d)
        xp, z3, ckv, fin = _trunk_layer(xp, mod_ctx, p, ctx_consts, h0_ctx, None, ctx_cfg)
        ckv_list.append(ckv.reshape(bc, lc, KV_LORA))
        kr_list.append(_take_rope_key(z3[:, :, Z_KR:]))
        lru_list.append(fin)
        kr_tile = _place_rope_key(cache_krope[:, l].reshape(bl * past, ROPE))
        kc, vc = _kvcache(cache_ckv[:, l].reshape(bl * past, KV_LORA), kr_tile, cache_tabs, p)
        cache = (kc, vc.reshape(bl, past, HP))
        xs, _, _, _ = _trunk_layer(xs, mod_lat, p, lat_consts, state_rglru[:, l], cache, lat_cfg)
    return (xp.reshape(bc, lc, d), xs.reshape(bl, ll, d),
            jnp.stack(ckv_list, axis=1), jnp.stack(kr_list, axis=1), jnp.stack(lru_list, axis=1))
```

```python
import functools
import math

import numpy as np
import jax
import jax.numpy as jnp
from jax import lax
from jax.experimental import pallas as pl
from jax.experimental.pallas import tpu as pltpu
from jax.experimental.pallas import tpu_sc as plsc

F32 = jnp.float32
BF16 = jnp.bfloat16

D_MODEL = 1024
DEPTH = 2
GRID_W = 64
EPS = 1e-6
FNET_W = 256
FNET_GROUPS = 4
LRU_W = 256
LRU_BLOCKS = 4
LRU_C = 8.0
SC_W = 256
MLA_HEADS = 8
Q_LORA = 384
KV_LORA = 256
NOPE = 64
ROPE = 32
V_DIM = 64
QK_DIM = NOPE + ROPE
MLA_OUT = MLA_HEADS * V_DIM
N_FREQ = ROPE // 4
ROPE_BASE = 10000.0
ATTN_SCALE = QK_DIM ** -0.5
LOG2E = math.log2(math.e)
N_GROUPS = 4
EXP_PER_GROUP = 8
N_EXPERTS = N_GROUPS * EXP_PER_GROUP

LANE = 128
SUB = 8
HEAD_PAD = LANE
HP = MLA_HEADS * HEAD_PAD
ROPE_X1 = 0
ROPE_X2 = LANE // 2
DEN_LANE = V_DIM
Z_QKV = 6 * 256
Z_KR = Z_QKV + Q_LORA + KV_LORA
Z_W = Z_KR + LANE
QKV_W = Q_LORA + KV_LORA + LANE
VMEM_LIMIT = 52 * 1024 * 1024
PACK_W = D_MODEL // 2
HI_MASK = -65536
RT_ID0, RT_ID1, RT_W0, RT_W1 = 0, 1, 2, 3
RT_ROWS = 48
TOP_K = 2
MERGE_BLK = 256
PLAN_BLK = 256
DFT_SPLIT = 64
ATTN_SHORT_SEQS = 2
ATTN_SUB = 256
SC_CORES = 2
SC_SUBCORES = 16
SC_CHUNK = 128


def _cparams(sem):
    return pltpu.CompilerParams(dimension_semantics=sem, vmem_limit_bytes=VMEM_LIMIT)


def _bdot(a, b):
    return jnp.dot(a, b, preferred_element_type=F32)


def _rms(x, g):
    return x * lax.rsqrt(jnp.mean(x * x, axis=-1, keepdims=True) + EPS) * g


def _ada_kernel(c_ref, w_ref, b_ref, o_ref):
    c = c_ref[...]
    s = (c * jax.nn.sigmoid(c)).astype(BF16)
    o_ref[...] = _bdot(s, w_ref[...].astype(BF16)) + b_ref[...]


def _ada(cond8, w_ada, b_ada):
    nblk = 6 * D_MODEL // 1024
    return pl.pallas_call(
        _ada_kernel,
        grid=(DEPTH, nblk),
        in_specs=[
            pl.BlockSpec((SUB, D_MODEL), lambda l, n: (0, 0)),
            pl.BlockSpec((None, D_MODEL, 1024), lambda l, n: (l, 0, n)),
            pl.BlockSpec((None, 1, 1024), lambda l, n: (l, 0, n)),
        ],
        out_specs=pl.BlockSpec((None, SUB, 1024), lambda l, n: (l, 0, n)),
        out_shape=jax.ShapeDtypeStruct((DEPTH, SUB, 6 * D_MODEL), F32),
        compiler_params=_cparams(("arbitrary", "arbitrary")),
        name="ada",
    )(cond8, w_ada, b_ada.reshape(DEPTH, 1, 6 * D_MODEL))


def _front_kernel(x_ref, mod_ref, g_ref, w_ref, z_ref):
    x = x_ref[...]
    sh = mod_ref[:, 0:D_MODEL]
    sc = mod_ref[:, D_MODEL:2 * D_MODEL]
    h = _rms(x, g_ref[...]) * (1.0 + sc) + sh
    z_ref[...] = _bdot(h.astype(BF16), w_ref[...])


def _front(x, mod, g1, w_in, tm):
    t = x.shape[0]
    per_mod = t // (mod.shape[0] * tm)
    return pl.pallas_call(
        _front_kernel,
        grid=(t // tm,),
        in_specs=[
            pl.BlockSpec((tm, D_MODEL), lambda i: (i, 0)),
            pl.BlockSpec((None, 1, 6 * D_MODEL), lambda i: (i // per_mod, 0, 0)),
            pl.BlockSpec((1, D_MODEL), lambda i: (0, 0)),
            pl.BlockSpec((D_MODEL, Z_W), lambda i: (0, 0)),
        ],
        out_specs=pl.BlockSpec((tm, Z_W), lambda i: (i, 0)),
        out_shape=jax.ShapeDtypeStruct((t, Z_W), F32),
        compiler_params=_cparams(("arbitrary",)),
        name="front",
    )(x, mod, g1, w_in)


def _head_norm_rope_t(x_t, g_col, cos_t, sinr_t):
    ss = jnp.sum(x_t * x_t, axis=0, keepdims=True) * (1.0 / QK_DIM)
    x_t = x_t * lax.rsqrt(ss + EPS) * g_col
    return (x_t * cos_t + pltpu.roll(x_t, HEAD_PAD // 2, 0) * sinr_t).astype(BF16)


def _build_kv(ckv, kr_tile, cos_t, sinr_t, wk_t_ref, wv_ref, gk_col, eye_ref, kt_ref, v_ref):
    nt = (((1,), (1,)), ((), ()))
    cb = ckv.astype(BF16)
    kn_t = lax.dot_general(wk_t_ref[...], cb, nt, preferred_element_type=F32)
    kr_hi = kr_tile.astype(BF16)
    kr_lo = (kr_tile - kr_hi.astype(F32)).astype(BF16)
    kr_t = (lax.dot_general(eye_ref[...], kr_hi, nt, preferred_element_type=F32)
            + lax.dot_general(eye_ref[...], kr_lo, nt, preferred_element_type=F32))
    for h in range(MLA_HEADS):
        sl = slice(h * HEAD_PAD, (h + 1) * HEAD_PAD)
        kt_ref[sl, :] = _head_norm_rope_t(kn_t[sl, :] + kr_t, gk_col, cos_t, sinr_t)
    v = _bdot(cb, wv_ref[...])
    lane = lax.broadcasted_iota(jnp.int32, v.shape, 1)
    v_ref[...] = jnp.where((lane & (HEAD_PAD - 1)) == DEN_LANE, 1.0, v).astype(BF16)


def _qkv_kernel(z_ref, cos_t_ref, sinr_t_ref, gqa_ref, wq_ref, gkva_ref, wk_ref, wv_ref,
                gq_ref, gk_ref, eye_ref, q_ref, k_ref, v_ref, ckv_ref):
    cos_t, sinr_t = cos_t_ref[...], sinr_t_ref[...]
    q_c = z_ref[:, 0:Q_LORA]
    kv_c = z_ref[:, Q_LORA:Q_LORA + KV_LORA]
    kr_tile = z_ref[:, Q_LORA + KV_LORA:QKV_W]
    nt = (((1,), (1,)), ((), ()))
    qf_t = lax.dot_general(wq_ref[...], _rms(q_c, gqa_ref[...]).astype(BF16), nt, preferred_element_type=F32)
    gq_col = gq_ref[...] * (ATTN_SCALE * LOG2E)
    for h in range(MLA_HEADS):
        sl = slice(h * HEAD_PAD, (h + 1) * HEAD_PAD)
        q_ref[sl, :] = _head_norm_rope_t(qf_t[sl, :], gq_col, cos_t, sinr_t)
    ckv = _rms(kv_c, gkva_ref[...])
    ckv_ref[...] = ckv
    _build_kv(ckv, kr_tile, cos_t, sinr_t, wk_ref, wv_ref, gk_ref[...], eye_ref, k_ref, v_ref)


def _qkv(z, tabs, p, seq_len, tm):
    t = z.shape[0]
    if tm > seq_len:
        tabs = [jnp.tile(a, (tm // seq_len, 1)) for a in tabs]
    tabs_t = [a.T for a in tabs]
    per_seq = max(seq_len // tm, 1)
    tab_t_spec = pl.BlockSpec((LANE, tm), lambda i: (0, i % per_seq))
    full = lambda shape: pl.BlockSpec(shape, lambda i: (0,) * len(shape))
    return pl.pallas_call(
        _qkv_kernel,
        grid=(t // tm,),
        in_specs=[
            pl.BlockSpec((tm, QKV_W), lambda i: (i, Z_QKV // QKV_W)),
            tab_t_spec, tab_t_spec,
            full((1, Q_LORA)), full((HP, Q_LORA)), full((1, KV_LORA)),
            full((HP, KV_LORA)), full((KV_LORA, HP)), full((LANE, 1)), full((LANE, 1)), full((LANE, LANE)),
        ],
        out_specs=[
            pl.BlockSpec((HP, tm), lambda i: (0, i)),
            pl.BlockSpec((HP, tm), lambda i: (0, i)),
            pl.BlockSpec((tm, HP), lambda i: (i, 0)),
            pl.BlockSpec((tm, KV_LORA), lambda i: (i, 0)),
        ],
        out_shape=[
            jax.ShapeDtypeStruct((HP, t), BF16),
            jax.ShapeDtypeStruct((HP, t), BF16),
            jax.ShapeDtypeStruct((t, HP), BF16),
            jax.ShapeDtypeStruct((t, KV_LORA), F32),
        ],
        compiler_params=_cparams(("arbitrary",)),
        name="qkv",
    )(z, tabs_t[0], tabs_t[1], p["g_qa"], p["wq_t"], p["g_kva"], p["wk_t"], p["wv"],
      p["gq_col"], p["gk_col"], p["eye"])


def _kvcache_kernel(ckv_ref, kr_ref, cos_t_ref, sinr_t_ref, wk_ref, wv_ref, gk_ref, eye_ref, k_ref, v_ref):
    _build_kv(ckv_ref[...], kr_ref[...], cos_t_ref[...], sinr_t_ref[...], wk_ref, wv_ref, gk_ref[...], eye_ref,
              k_ref, v_ref)


def _kvcache(ckv, kr_tile, tabs, p):
    t = ckv.shape[0]
    return pl.pallas_call(
        _kvcache_kernel,
        out_shape=[jax.ShapeDtypeStruct((HP, t), BF16), jax.ShapeDtypeStruct((t, HP), BF16)],
        compiler_params=_cparams(None),
        name="kvcache",
    )(ckv, kr_tile, tabs[0].T, tabs[1].T, p["wk_t"], p["wv"], p["gk_col"], p["eye"])


def _fourier_kernel(nb, u_ref, cc_ref, sc_ref, cl_ref, sl_ref, o_ref, a_scr, b_scr):
    @pl.when(pl.program_id(1) == 0)
    def _():
        for b in range(nb):
            u = u_ref[b].astype(BF16)
            a_scr[:, b * FNET_W:(b + 1) * FNET_W] = _bdot(u, cc_ref[...]).astype(BF16)
            b_scr[:, b * FNET_W:(b + 1) * FNET_W] = _bdot(u, sc_ref[...]).astype(BF16)

    y = _bdot(cl_ref[...], a_scr[...]) - _bdot(sl_ref[...], b_scr[...])
    for b in range(nb):
        o_ref[b] = y[:, b * FNET_W:(b + 1) * FNET_W].astype(BF16)


def _fourier_split_kernel(nb, u_ref, cc_ref, sc_ref, ca_ref, sa_ref, cb_ref, sb_ref, o_ref,
                          a_scr, b_scr, cl_scr, sl_scr):
    @pl.when(pl.program_id(1) == 0)
    def _():
        for b in range(nb):
            u = u_ref[b].astype(BF16)
            a_scr[:, b * FNET_W:(b + 1) * FNET_W] = _bdot(u, cc_ref[...]).astype(BF16)
            b_scr[:, b * FNET_W:(b + 1) * FNET_W] = _bdot(u, sc_ref[...]).astype(BF16)

    cb, sb = cb_ref[...], sb_ref[...]
    for j in range(ca_ref.shape[0]):
        ca, sa = ca_ref[j:j + 1, :], sa_ref[j:j + 1, :]
        cl_scr[j * DFT_SPLIT:(j + 1) * DFT_SPLIT, :] = (ca * cb - sa * sb).astype(BF16)
        sl_scr[j * DFT_SPLIT:(j + 1) * DFT_SPLIT, :] = (sa * cb + ca * sb).astype(BF16)
    y = _bdot(cl_scr[...], a_scr[...]) - _bdot(sl_scr[...], b_scr[...])
    for b in range(nb):
        o_ref[b] = y[:, b * FNET_W:(b + 1) * FNET_W].astype(BF16)


def _fourier_split(z3, mats, nb, tq):
    s, l, _ = z3.shape
    cc, sc, ca, sa, cb, sb = mats
    full = lambda a: pl.BlockSpec(a.shape, lambda i, r: (0, 0))
    rows = pl.BlockSpec((tq // DFT_SPLIT, l), lambda i, r: (r, 0))
    return pl.pallas_call(
        functools.partial(_fourier_split_kernel, nb),
        grid=(s // nb, l // tq),
        in_specs=[pl.BlockSpec((nb, l, FNET_W), lambda i, r: (i, 0, 0)), full(cc), full(sc), rows, rows,
                  full(cb), full(sb)],
        out_specs=pl.BlockSpec((nb, tq, FNET_W), lambda i, r: (i, r, 0)),
        out_shape=jax.ShapeDtypeStruct((s, l, FNET_W), BF16),
        scratch_shapes=[pltpu.VMEM((l, nb * FNET_W), BF16), pltpu.VMEM((l, nb * FNET_W), BF16),
                        pltpu.VMEM((tq, l), BF16), pltpu.VMEM((tq, l), BF16)],
        compiler_params=_cparams(("arbitrary", "arbitrary")),
        name="fourier_split",
    )(z3, cc, sc, ca, sa, cb, sb)


def _fourier(z3, mats, nb, tq):
    s, l, _ = z3.shape
    if len(mats) == 6:
        return _fourier_split(z3, mats, nb, tq)
    cc, sc, cl, sl = mats
    return pl.pallas_call(
        functools.partial(_fourier_kernel, nb),
        grid=(s // nb, l // tq),
        in_specs=[
            pl.BlockSpec((nb, l, FNET_W), lambda i, r: (i, 0, 0)),
            pl.BlockSpec((FNET_W, FNET_W), lambda i, r: (0, 0)),
            pl.BlockSpec((FNET_W, FNET_W), lambda i, r: (0, 0)),
            pl.BlockSpec((tq, l), lambda i, r: (r, 0)),
            pl.BlockSpec((tq, l), lambda i, r: (r, 0)),
        ],
        out_specs=pl.BlockSpec((nb, tq, FNET_W), lambda i, r: (i, r, 0)),
        out_shape=jax.ShapeDtypeStruct((s, l, FNET_W), BF16),
        scratch_shapes=[pltpu.VMEM((l, nb * FNET_W), BF16), pltpu.VMEM((l, nb * FNET_W), BF16)],
        compiler_params=_cparams(("arbitrary", "arbitrary")),
        name="fourier",
    )(z3, cc, sc, cl, sl)


def _load_ext(ref, c, rows, nchunks):
    t0 = pl.multiple_of(c * rows, rows)
    main = ref[pl.ds(t0, rows), :]
    lo = pl.multiple_of(jnp.maximum(t0 - SUB, 0), SUB)
    hi = pl.multiple_of(jnp.minimum(t0 + rows, (nchunks - 1) * rows + rows - SUB), SUB)
    prev = jnp.where(c > 0, ref[pl.ds(lo, SUB), :], 0.0)
    nxt = jnp.where(c < nchunks - 1, ref[pl.ds(hi, SUB), :], 0.0)
    return jnp.concatenate([prev, main, nxt], axis=0)


def _shifted(ext, off, rows):
    n = ext.shape[0]
    r = ext if off == 0 else pltpu.roll(ext, (-off) % n, 0)
    return r[SUB:SUB + rows]


def _gelu_tanh(x):
    return 0.5 * x * (1.0 + jnp.tanh(math.sqrt(2.0 / math.pi) * (x + 0.044715 * (x * x * x))))


def _lru_gates(xc, d, wa_ref, wx_ref, ba_ref, bx_ref, lam_ref):
    xb = xc.astype(BF16)
    r = jax.nn.sigmoid(_bdot(xb, wa_ref[d]) + ba_ref[d])
    i = jax.nn.sigmoid(_bdot(xb, wx_ref[d]) + bx_ref[d])
    nl = -lam_ref[d]
    softplus = jnp.maximum(nl, 0.0) + jnp.log1p(jnp.exp(-jnp.abs(nl)))
    la = (-LRU_C) * r * softplus
    a = jnp.exp(la)
    one_m_a2 = -jnp.tanh(la) * (a * a + 1.0)
    b = jnp.sqrt(one_m_a2) * (i * xc)
    return a, b


def _group_scan(a, b, reverse):
    rows = a.shape[0]
    rm = lax.broadcasted_iota(jnp.int32, a.shape, 0) & (SUB - 1)
    for s in (1, 2, 4):
        if reverse:
            sh, m = rows - s, rm + s <= SUB - 1
        else:
            sh, m = s, rm >= s
        a_sh = pltpu.roll(a, sh, 0)
        b_sh = pltpu.roll(b, sh, 0)
        b = jnp.where(m, a * b_sh + b, b)
        a = jnp.where(m, a * a_sh, a)
    return a, b


def _carry_scan(a, b, carry, reverse):
    ng = a.shape[0] // SUB
    out = [None] * ng
    order = range(ng - 1, -1, -1) if reverse else range(ng)
    for g in order:
        hg = a[g * SUB:(g + 1) * SUB] * carry + b[g * SUB:(g + 1) * SUB]
        out[g] = hg
        carry = hg[0:1] if reverse else hg[SUB - 1:SUB]
    return jnp.concatenate(out, axis=0), carry


def _lru_kernel(nb, rows, nchunks, xl_ref, gl_ref, bs_ref, cs_ref, xs_ref, h0_ref, cw_ref, wa_ref, wx_ref,
                ba_ref, bx_ref, lam_ref, sw_ref, yl_ref, ys_ref, fin_ref, xc_scr, hf_scr, hb_scr):
    cw = cw_ref[...]
    sw = sw_ref[...]
    chunk = lambda c: pl.ds(pl.multiple_of(c * rows, rows), rows)
    seqs = range(nb)

    def conv_body(c, carry):
        for s in seqs:
            ext = _load_ext(xl_ref.at[s], c, rows, nchunks)
            xc = _shifted(ext, -2, rows) * cw[0:1]
            for k in range(1, 4):
                xc = xc + _shifted(ext, k - 2, rows) * cw[k:k + 1]
            xc_scr[s, chunk(c), :] = xc
        return carry

    lax.fori_loop(0, nchunks, conv_body, 0)

    def direction(s, c, d, carry, h_scr):
        a, b = _lru_gates(xc_scr[s, chunk(c), :], d, wa_ref, wx_ref, ba_ref, bx_ref, lam_ref)
        a, b = _group_scan(a, b, d == 1)
        h, carry = _carry_scan(a, b, carry, d == 1)
        h_scr[s, chunk(c), :] = h
        return carry

    def scan_body(j, carries):
        return tuple((direction(s, j, 0, cf, hf_scr), direction(s, nchunks - 1 - j, 1, cb, hb_scr))
                     for s, (cf, cb) in zip(seqs, carries))

    init = tuple((h0_ref[s, 0:1, :], h0_ref[s, 1:2, :]) for s in seqs)
    for s, (cf, cb) in zip(seqs, lax.fori_loop(0, nchunks, scan_body, init)):
        fin_ref[s, 0:1, :] = cf
        fin_ref[s, 1:2, :] = cb

    def out_body(c, carry):
        for s in seqs:
            y = (hf_scr[s, chunk(c), :] + hb_scr[s, chunk(c), :]) * _gelu_tanh(gl_ref[s, chunk(c), :])
            yl_ref[s, chunk(c), :] = y.astype(BF16)
            ext = _load_ext(cs_ref.at[s], c, rows, nchunks) * _load_ext(xs_ref.at[s], c, rows, nchunks)
            acc = _shifted(ext, -1, rows) * sw[0:1]
            acc = acc + _shifted(ext, 0, rows) * sw[1:2]
            acc = acc + _shifted(ext, 1, rows) * sw[2:3]
            ys_ref[s, chunk(c), :] = (bs_ref[s, chunk(c), :] * acc).astype(BF16)
        return carry

    lax.fori_loop(0, nchunks, out_body, 0)


def _lru(z3, h0, p, nb):
    s, l, _ = z3.shape
    rows = min(l, 1024)
    nchunks = l // rows
    half = lambda blk: pl.BlockSpec((nb, l, LANE), lambda i, j: (i, 0, blk + j))
    wspec = lambda shape: pl.BlockSpec((None,) + shape, lambda i, j: (j,) + (0,) * len(shape))
    seq_half = pl.BlockSpec((nb, l, LANE), lambda i, j: (i, 0, j))
    state = pl.BlockSpec((nb, 2, LANE), lambda i, j: (i, 0, j))
    return pl.pallas_call(
        functools.partial(_lru_kernel, nb, rows, nchunks),
        grid=(s // nb, 2),
        in_specs=[
            half(2), half(4), half(6), half(8), half(10), state,
            wspec((4, LANE)), wspec((2, LANE, LANE)), wspec((2, LANE, LANE)),
            wspec((2, 1, LANE)), wspec((2, 1, LANE)), wspec((2, 1, LANE)), wspec((3, LANE)),
        ],
        out_specs=[seq_half, seq_half, state],
        out_shape=[jax.ShapeDtypeStruct((s, l, LRU_W), BF16), jax.ShapeDtypeStruct((s, l, SC_W), BF16),
                   jax.ShapeDtypeStruct((s, 2, LRU_W), F32)],
        scratch_shapes=[pltpu.VMEM((nb, l, LANE), F32)] * 3,
        compiler_params=_cparams(("arbitrary", "arbitrary")),
        name="lru",
    )(z3, z3, z3, z3, z3, h0, p["lru_conv"], p["lru_wa"], p["lru_wx"], p["lru_ba"], p["lru_bx"], p["lru_lam"],
      p["sc_conv"])


def _store_head_pair(o_ref, rows, j, o_even, o_odd):
    lane = lax.broadcasted_iota(jnp.int32, o_even.shape, 1)
    pair = jnp.where(lane < V_DIM, o_even, pltpu.roll(o_odd, V_DIM, 1))
    o_ref[rows, j * LANE:(j + 1) * LANE] = pair.astype(BF16)


def _attn_kernel(hp, sub, has_cache, *refs):
    if has_cache:
        q_ref, k_ref, v_ref, kc_ref, vc_ref, o_ref, s_scr, p_scr, m_scr, sc_scr, pc_scr = refs
    else:
        q_ref, k_ref, v_ref, o_ref, s_scr, p_scr, m_scr = refs
    tn = (((0,), (0,)), ((), ()))
    items = [(slice(r * sub, (r + 1) * sub), slice(h * HEAD_PAD, (h + 1) * HEAD_PAD))
             for r in range(q_ref.shape[1] // sub) for h in range(hp)]
    n = len(items)

    def scores(i):
        rows, head = items[i]
        q_t = q_ref[head, rows]
        s = lax.dot_general(q_t, k_ref[head, :], tn, preferred_element_type=F32)
        m = jnp.max(s, axis=-1, keepdims=True)
        s_scr[i % 2] = s
        if has_cache:
            sc = lax.dot_general(q_t, kc_ref[head, :], tn, preferred_element_type=F32)
            m = jnp.maximum(m, jnp.max(sc, axis=-1, keepdims=True))
            sc_scr[i % 2] = sc
        m_scr[i % 2] = jnp.broadcast_to(m, m_scr.shape[1:])

    def probs(i):
        m = m_scr[i % 2][:, 0:1]
        p_scr[i % 2] = jnp.exp2(s_scr[i % 2] - m).astype(BF16)
        if has_cache:
            pc_scr[i % 2] = jnp.exp2(sc_scr[i % 2] - m).astype(BF16)

    pending = {}

    def weighted_values(i):
        rows, head = items[i]
        o = _bdot(p_scr[i % 2], v_ref[:, head])
        if has_cache:
            o = o + _bdot(pc_scr[i % 2], vc_ref[:, head])
        o = o / o[:, DEN_LANE:DEN_LANE + 1]
        h = i % hp
        if h % 2 == 0:
            pending[0] = o
        else:
            _store_head_pair(o_ref, rows, h // 2, pending.pop(0), o)

    scores(0)
    for i in range(n):
        if i + 1 < n:
            scores(i + 1)
        if i > 0:
            weighted_values(i - 1)
        probs(i)
    weighted_values(n - 1)


def _attn_short_kernel(hp, q_ref, k_ref, v_ref, o_ref):
    tn = (((0,), (0,)), ((), ()))
    nseq, l, _ = v_ref.shape
    items = [(b, slice(b * l, (b + 1) * l), slice(h * HEAD_PAD, (h + 1) * HEAD_PAD))
             for b in range(nseq) for h in range(hp)]
    scores = [lax.dot_general(q_ref[hd, cols], k_ref[hd, cols], tn, preferred_element_type=F32)
              for _, cols, hd in items]
    probs = [jnp.exp2(s - jnp.max(s, axis=-1, keepdims=True)).astype(BF16) for s in scores]
    outs = []
    for (b, _, hd), p in zip(items, probs):
        o = _bdot(p, v_ref[b, :, hd])
        outs.append(o / o[:, DEN_LANE:DEN_LANE + 1])
    for b in range(nseq):
        for j in range(hp // 2):
            _store_head_pair(o_ref.at[b], slice(None), j, outs[b * hp + 2 * j], outs[b * hp + 2 * j + 1])


def _attn(q, k, v, cache, hp, tq):
    s, l, _ = v.shape
    w = hp * HEAD_PAD
    qspec = pl.BlockSpec((w, tq), lambda i, h, r: (h, i * (l // tq) + r))
    ospec = pl.BlockSpec((None, tq, hp * V_DIM), lambda i, h, r: (i, r, h))
    if cache is None and tq == l:
        nseq = ATTN_SHORT_SEQS
        kspec = pl.BlockSpec((w, nseq * l), lambda i, h, r: (h, i))
        vspec = pl.BlockSpec((nseq, l, w), lambda i, h, r: (i, 0, h))
        return pl.pallas_call(
            functools.partial(_attn_short_kernel, hp),
            grid=(s // nseq, MLA_HEADS // hp, 1),
            in_specs=[kspec, kspec, vspec],
            out_specs=pl.BlockSpec((nseq, l, hp * V_DIM), lambda i, h, r: (i, 0, h)),
            out_shape=jax.ShapeDtypeStruct((s, l, MLA_OUT), BF16),
            compiler_params=_cparams(("arbitrary", "arbitrary", "arbitrary")),
            name="attn_short",
        )(q, k, v)
    kv_mode = dict(pipeline_mode=pl.Buffered(1))
    kspec = pl.BlockSpec((w, l), lambda i, h, r: (h, i), **kv_mode)
    vspec = pl.BlockSpec((None, l, w), lambda i, h, r: (i, 0, h), **kv_mode)
    in_specs = [qspec, kspec, vspec]
    args = [q, k, v]
    if cache is not None:
        lc = cache[1].shape[1]
        in_specs += [pl.BlockSpec((w, lc), lambda i, h, r: (h, i), **kv_mode),
                     pl.BlockSpec((None, lc, w), lambda i, h, r: (i, 0, h), **kv_mode)]
        args += list(cache)
    sub = min(tq, ATTN_SUB)
    scratch = [pltpu.VMEM((2, sub, l), F32), pltpu.VMEM((2, sub, l), BF16), pltpu.VMEM((2, sub, LANE), F32)]
    if cache is not None:
        scratch += [pltpu.VMEM((2, sub, lc), F32), pltpu.VMEM((2, sub, lc), BF16)]
    return pl.pallas_call(
        functools.partial(_attn_kernel, hp, sub, cache is not None),
        grid=(s, MLA_HEADS // hp, l // tq),
        in_specs=in_specs,
        out_specs=ospec,
        out_shape=jax.ShapeDtypeStruct((s, l, MLA_OUT), BF16),
        scratch_shapes=scratch,
        compiler_params=_cparams(("arbitrary", "arbitrary", "arbitrary")),
        name="attn",
    )(*args)


def _route(lt):
    row = lax.broadcasted_iota(jnp.int32, (SUB, lt.shape[1]), 0).astype(F32)
    neg = -jnp.inf
    gl = lt[0:SUB]
    g_ok = row < N_GROUPS
    glm = jnp.where(g_ok, gl, neg)
    gmax = jnp.max(glm, axis=0, keepdims=True)
    gsel = jnp.min(jnp.where(glm == gmax, row, float(SUB)), axis=0, keepdims=True)
    g_w = 1.0 / jnp.sum(jnp.where(g_ok, jnp.exp(gl - gmax), 0.0), axis=0, keepdims=True)
    es = lt[SUB:2 * SUB]
    for g in range(1, N_GROUPS):
        es = jnp.where(gsel == float(g), lt[(g + 1) * SUB:(g + 2) * SUB], es)
    m1 = jnp.max(es, axis=0, keepdims=True)
    i1 = jnp.min(jnp.where(es == m1, row, float(SUB)), axis=0, keepdims=True)
    es2 = jnp.where(row == i1, neg, es)
    m2 = jnp.max(es2, axis=0, keepdims=True)
    i2 = jnp.min(jnp.where(es2 == m2, row, float(SUB)), axis=0, keepdims=True)
    e2 = jnp.exp(m2 - m1)
    inv = g_w / (1.0 + e2)
    base = gsel * EXP_PER_GROUP
    return (jnp.where(row == RT_ID0, base + i1, 0.0) + jnp.where(row == RT_ID1, base + i2, 0.0)
            + jnp.where(row == RT_W0, inv, 0.0) + jnp.where(row == RT_W1, inv * e2, 0.0))


def _pack_bf16_pairs(x):
    k = x.shape[1] // 2
    bits = lax.bitcast_convert_type(x.astype(BF16).astype(F32), jnp.int32)
    return lax.shift_right_logical(bits[:, :k], 16) | (bits[:, k:] & HI_MASK)


def _unpack_bf16_pairs(w):
    lo = lax.bitcast_convert_type(lax.shift_left(w, 16), F32).astype(BF16)
    hi = lax.bitcast_convert_type(w & HI_MASK, F32).astype(BF16)
    return lo, hi


def _back_kernel(x_ref, mod_ref, g1_ref, g2_ref, yf_ref, yl_ref, ys_ref, o_ref,
                 wg_ref, wpf_ref, wpl_ref, wps_ref, wpm_ref, wo_ref, wrh_ref, wrl_ref, br_ref,
                 x1_ref, h2_ref, comb_ref):
    d = D_MODEL
    x = x_ref[...]
    sh1, sc1, gt1 = mod_ref[:, 0:d], mod_ref[:, d:2 * d], mod_ref[:, 2 * d:3 * d]
    sh2, sc2 = mod_ref[:, 3 * d:4 * d], mod_ref[:, 4 * d:5 * d]
    hb = (_rms(x, g1_ref[...]) * (1.0 + sc1) + sh1).astype(BF16)
    branches = ((yf_ref, wpf_ref), (yl_ref, wpl_ref), (ys_ref, wps_ref), (o_ref, wpm_ref))
    blocks = []
    for n in range(0, d, MERGE_BLK):
        acc = None
        for j, (y_ref, w_ref) in enumerate(branches):
            gate = jax.nn.sigmoid(_bdot(hb, wg_ref[:, j * d + n:j * d + n + MERGE_BLK]))
            term = gate * _bdot(y_ref[...], w_ref[:, n:n + MERGE_BLK])
            acc = term if acc is None else acc + term
        blocks.append(acc.astype(BF16))
    x1 = x + gt1 * _bdot(jnp.concatenate(blocks, axis=1), wo_ref[...])
    x1_ref[...] = x1
    h2 = _rms(x1, g2_ref[...]) * (1.0 + sc2) + sh2
    h2b = h2.astype(BF16)
    h2_ref[...] = _pack_bf16_pairs(h2)
    h2l = (h2 - h2b.astype(F32)).astype(BF16)
    nt = (((1,), (1,)), ((), ()))
    rdot = lambda w_ref, h: lax.dot_general(w_ref[...], h, nt, preferred_element_type=F32)
    logits_t = rdot(wrh_ref, h2b) + rdot(wrh_ref, h2l) + rdot(wrl_ref, h2b) + br_ref[...]
    comb_ref[...] = _route(logits_t)


def _back(x, mod, yf, yl, ys, o, p, tm):
    t = x.shape[0]
    per_mod = t // (mod.shape[0] * tm)
    row = lambda w: pl.BlockSpec((tm, w), lambda i: (i, 0))
    full = lambda a: pl.BlockSpec(a.shape, lambda i: (0,) * a.ndim, pipeline_mode=pl.Buffered(1))
    weights = [p["w_gate"], p["w_pf"], p["w_pl"], p["w_ps"], p["w_pm"], p["w_out"], p["wr_hi"], p["wr_lo"], p["b_r"]]
    return pl.pallas_call(
        _back_kernel,
        grid=(t // tm,),
        in_specs=[
            row(D_MODEL),
            pl.BlockSpec((None, 1, 6 * D_MODEL), lambda i: (i // per_mod, 0, 0)),
            pl.BlockSpec((1, D_MODEL), lambda i: (0, 0)),
            pl.BlockSpec((1, D_MODEL), lambda i: (0, 0)),
            row(FNET_W), row(LRU_W), row(SC_W), row(MLA_OUT),
        ] + [full(w) for w in weights],
        out_specs=[row(D_MODEL), row(PACK_W), pl.BlockSpec((SUB, tm), lambda i: (0, i))],
        out_shape=[
            jax.ShapeDtypeStruct((t, D_MODEL), F32),
            jax.ShapeDtypeStruct((t, PACK_W), jnp.int32),
            jax.ShapeDtypeStruct((SUB, t), F32),
        ],
        compiler_params=_cparams(("arbitrary",)),
        name="back",
    )(x, mod, p["norm1_g"], p["norm2_g"], yf, yl, ys, o, *weights)


def _plan_kernel(tm, rt_ref, pos_ref, te_ref, nu_ref):
    ids = jnp.concatenate([rt_ref[RT_ID0:RT_ID0 + 1, :], rt_ref[RT_ID1:RT_ID1 + 1, :]], axis=1)
    n_rows = ids.shape[1]
    erow = lax.broadcasted_iota(jnp.int32, (N_EXPERTS, PLAN_BLK), 0).astype(F32)
    col = lax.broadcasted_iota(jnp.int32, (N_EXPERTS, LANE), 0)
    blocks = [(ids[:, b * PLAN_BLK:(b + 1) * PLAN_BLK] == erow).astype(F32) for b in range(n_rows // PLAN_BLK)]
    tri = (lax.broadcasted_iota(jnp.int32, (PLAN_BLK, PLAN_BLK), 0)
           <= lax.broadcasted_iota(jnp.int32, (PLAN_BLK, PLAN_BLK), 1)).astype(BF16)
    prefix = [_bdot(blk.astype(BF16), tri) for blk in blocks]
    counts = prefix[0][:, PLAN_BLK - 1:PLAN_BLK]
    for pre in prefix[1:]:
        counts = counts + pre[:, PLAN_BLK - 1:PLAN_BLK]
    tiles = jnp.floor((counts + float(tm - 1)) * (1.0 / tm))
    tile_end = jnp.broadcast_to(tiles, (N_EXPERTS, LANE))
    for s in (1, 2, 4, 8, 16):
        tile_end = tile_end + jnp.where(col >= s, pltpu.roll(tile_end, s, 0), 0.0)
    base = (tile_end[:, 0:1] - tiles) * float(tm) - 1.0
    for b, (blk, pre) in enumerate(zip(blocks, prefix)):
        pos = jnp.sum(blk * (base + pre), axis=0, keepdims=True)
        pos_ref[:, b * PLAN_BLK:(b + 1) * PLAN_BLK] = pos.astype(jnp.int32)
        base = base + pre[:, PLAN_BLK - 1:PLAN_BLK]
    k = lax.broadcasted_iota(jnp.int32, (N_EXPERTS, LANE), 1).astype(F32)
    owner = jnp.sum((k >= tile_end).astype(F32), axis=0, keepdims=True)
    te_ref[...] = jnp.minimum(owner, float(N_EXPERTS - 1)).astype(jnp.int32)
    nu_ref[...] = jnp.broadcast_to(tile_end[N_EXPERTS - 1:N_EXPERTS, :], nu_ref.shape).astype(jnp.int32)


def _dispatch_plan(rt, tm):
    n_rows = TOP_K * rt.shape[1]
    n_tiles = n_rows // tm + N_EXPERTS
    assert n_tiles <= LANE and tm & (tm - 1) == 0 and n_rows % PLAN_BLK == 0
    pos, te, nu = pl.pallas_call(
        functools.partial(_plan_kernel, tm),
        out_shape=[jax.ShapeDtypeStruct((1, n_rows), jnp.int32), jax.ShapeDtypeStruct((1, LANE), jnp.int32),
                   jax.ShapeDtypeStruct((1, LANE), jnp.int32)],
        compiler_params=_cparams(None),
        name="plan",
    )(rt)
    return pos.reshape(n_rows), te[0, :n_tiles], nu[0, :1], n_tiles


def _sc_worker_rows(n_rows):
    workers = SC_CORES * SC_SUBCORES
    per_w = n_rows // workers
    assert per_w * workers == n_rows and per_w % SC_CHUNK == 0
    return per_w


def _sc_mesh():
    return plsc.VectorSubcoreMesh(core_axis_name="c", subcore_axis_name="s",
                                  num_cores=SC_CORES, num_subcores=SC_SUBCORES)


def _sc_scatter_rows(src, pos, n_out):
    t, w = src.shape
    per_w = _sc_worker_rows(pos.shape[0])

    @functools.partial(
        pl.kernel, mesh=_sc_mesh(), out_type=jax.ShapeDtypeStruct((n_out, w), src.dtype),
        scratch_types=[pltpu.VMEM((SC_CHUNK,), jnp.int32), pltpu.VMEM((SC_CHUNK, w), src.dtype)],
        name="moe_dispatch")
    def run(src_hbm, pos_hbm, out_hbm, idx_v, rows_v):
        wid = lax.axis_index("s") * SC_CORES + lax.axis_index("c")

        @pl.loop(0, per_w // SC_CHUNK)
        def _(i):
            j0 = wid * per_w + i * SC_CHUNK
            pltpu.sync_copy(pos_hbm.at[pl.ds(j0, SC_CHUNK)], idx_v)
            pltpu.sync_copy(src_hbm.at[pl.ds(lax.rem(j0, t), SC_CHUNK)], rows_v)
            pltpu.sync_copy(rows_v, out_hbm.at[idx_v])

    return run(src, pos)


def _sc_gather_rows(src, pos):
    w = src.shape[1]
    n = pos.shape[0]
    per_w = _sc_worker_rows(n)

    @functools.partial(
        pl.kernel, mesh=_sc_mesh(), out_type=jax.ShapeDtypeStruct((n, w), src.dtype),
        scratch_types=[pltpu.VMEM((SC_CHUNK,), jnp.int32), pltpu.VMEM((SC_CHUNK, w), src.dtype)],
        name="moe_return")
    def run(src_hbm, pos_hbm, out_hbm, idx_v, rows_v):
        wid = lax.axis_index("s") * SC_CORES + lax.axis_index("c")

        @pl.loop(0, per_w // SC_CHUNK)
        def _(i):
            j0 = wid * per_w + i * SC_CHUNK
            pltpu.sync_copy(pos_hbm.at[pl.ds(j0, SC_CHUNK)], idx_v)
            pltpu.sync_copy(src_hbm.at[idx_v], rows_v)
            pltpu.sync_copy(rows_v, out_hbm.at[pl.ds(j0, SC_CHUNK)])

    return run(src, pos)


def _experts_kernel(te_ref, nu_ref, x_ref, w1_ref, w3_ref, w2_ref, y_ref):
    half = D_MODEL // 2

    @pl.when(pl.program_id(0) < nu_ref[0])
    def _():
        lo, hi = _unpack_bf16_pairs(x_ref[...])
        a = _bdot(lo, w1_ref[:half, :].astype(BF16)) + _bdot(hi, w1_ref[half:, :].astype(BF16))
        u = _bdot(lo, w3_ref[:half, :].astype(BF16)) + _bdot(hi, w3_ref[half:, :].astype(BF16))
        mid = (a * jax.nn.sigmoid(a)) * u
        y_ref[...] = _pack_bf16_pairs(_bdot(mid.astype(BF16), w2_ref[...].astype(BF16)))


def _experts(xs, tile_expert, n_used, n_tiles, w1, w3, w2, layer, tm):
    wspec = lambda a: pl.BlockSpec((None, None) + a.shape[2:], lambda i, te, nu: (layer, te[i], 0, 0))
    grid_spec = pltpu.PrefetchScalarGridSpec(
        num_scalar_prefetch=2,
        grid=(n_tiles,),
        in_specs=[pl.BlockSpec((tm, PACK_W), lambda i, te, nu: (jnp.minimum(i, nu[0] - 1), 0)),
                  wspec(w1), wspec(w3), wspec(w2)],
        out_specs=pl.BlockSpec((tm, PACK_W), lambda i, te, nu: (jnp.where(i < nu[0], i, n_tiles), 0)),
    )
    return pl.pallas_call(
        _experts_kernel,
        grid_spec=grid_spec,
        out_shape=jax.ShapeDtypeStruct((xs.shape[0] + tm, PACK_W), jnp.int32),
        compiler_params=_cparams(("arbitrary",)),
        name="experts",
    )(tile_expert, n_used, xs, w1, w3, w2)


def _combine_kernel(x1_ref, mod_ref, rt_ref, g0_ref, g1_ref, o_ref):
    rt = rt_ref[...]
    lo0, hi0 = _unpack_bf16_pairs(g0_ref[...])
    lo1, hi1 = _unpack_bf16_pairs(g1_ref[...])
    w0 = rt[:, RT_W0:RT_W0 + 1]
    w1 = rt[:, RT_W1:RT_W1 + 1]
    half = D_MODEL // 2
    gt2 = mod_ref[:, 5 * D_MODEL:6 * D_MODEL]
    o_ref[:, :half] = x1_ref[:, :half] + gt2[:, :half] * (w0 * lo0.astype(F32) + w1 * lo1.astype(F32))
    o_ref[:, half:] = x1_ref[:, half:] + gt2[:, half:] * (w0 * hi0.astype(F32) + w1 * hi1.astype(F32))


def _combine(x1, mod, rt, g, tm):
    t = x1.shape[0]
    per_mod = t // (mod.shape[0] * tm)
    return pl.pallas_call(
        _combine_kernel,
        grid=(t // tm,),
        in_specs=[
            pl.BlockSpec((tm, D_MODEL), lambda i: (i, 0)),
            pl.BlockSpec((None, 1, 6 * D_MODEL), lambda i: (i // per_mod, 0, 0)),
            pl.BlockSpec((tm, LANE), lambda i: (i, 0)),
            pl.BlockSpec((tm, PACK_W), lambda i: (i, 0)),
            pl.BlockSpec((tm, PACK_W), lambda i: (i + t // tm, 0)),
        ],
        out_specs=pl.BlockSpec((tm, D_MODEL), lambda i: (i, 0)),
        out_shape=jax.ShapeDtypeStruct((t, D_MODEL), F32),
        compiler_params=_cparams(("arbitrary",)),
        name="combine",
    )(x1, mod, rt, g, g)


def _moe(h2p, rt, x1, mod, p, tm_e, tm_c):
    pos, tile_expert, n_used, n_tiles = _dispatch_plan(rt, tm_e)
    xs = _sc_scatter_rows(h2p, pos, n_tiles * tm_e)
    ys = _experts(xs, tile_expert, n_used, n_tiles, p["w1"], p["w3"], p["w2"], p["layer"], tm_e)
    g = _sc_gather_rows(ys, pos)
    rt_cols = jnp.pad(rt.T, ((0, 0), (0, LANE - rt.shape[0])))
    return _combine(x1, mod, rt_cols, g, tm_c)


def _channel_dft():
    n = np.arange(FNET_W // FNET_GROUPS)
    ang = 2.0 * np.pi * ((n[:, None] * n[None, :]) % n.size) / n.size
    eye = np.eye(FNET_GROUPS)
    scale = 1.0 / math.sqrt(n.size)
    return (jnp.asarray(np.kron(eye, np.cos(ang) * scale), F32).astype(BF16),
            jnp.asarray(np.kron(eye, np.sin(ang) * scale), F32).astype(BF16))


def _position_dft(l):
    scale = 1.0 / math.sqrt(l)
    if l <= 256:
        n = np.arange(l)
        ang = 2.0 * np.pi * ((n[:, None] * n[None, :]) % l) / l
        return jnp.asarray(np.cos(ang) * scale, F32).astype(BF16), jnp.asarray(np.sin(ang) * scale, F32).astype(BF16)
    m = DFT_SPLIT
    n = np.arange(l)
    ang_a = 2.0 * np.pi * ((np.arange(l // m)[:, None] * m * n[None, :]) % l) / l
    ang_b = 2.0 * np.pi * ((np.arange(m)[:, None] * n[None, :]) % l) / l
    return (jnp.asarray(np.cos(ang_a), F32), jnp.asarray(np.sin(ang_a), F32),
            jnp.asarray(np.cos(ang_b) * scale, F32), jnp.asarray(np.sin(ang_b) * scale, F32))


def _rope_tables(l, rotate):
    cos = np.ones((l, LANE))
    sinr = np.zeros((l, LANE))
    if rotate:
        t = np.arange(l)
        inv = ROPE_BASE ** (-np.arange(N_FREQ) / N_FREQ)
        ang = np.concatenate([(t // GRID_W)[:, None] * inv, (t % GRID_W)[:, None] * inv], axis=-1)
        half = ROPE // 2
        for lo in (ROPE_X1, ROPE_X2):
            cos[:, lo:lo + half] = np.cos(ang)
        sinr[:, ROPE_X1:ROPE_X1 + half] = -np.sin(ang)
        sinr[:, ROPE_X2:ROPE_X2 + half] = np.sin(ang)
    return tuple(jnp.asarray(a, F32) for a in (cos, sinr))


def _head_lane_source():
    half = ROPE // 2
    src = np.full((HEAD_PAD,), QK_DIM, np.int32)
    src[ROPE_X1:ROPE_X1 + half] = NOPE + np.arange(half)
    src[ROPE_X2:ROPE_X2 + half] = NOPE + half + np.arange(half)
    free = [i for i in range(HEAD_PAD) if src[i] == QK_DIM][:NOPE]
    src[free] = np.arange(NOPE)
    return src


def _place_head_dims(w, n_src):
    src = _head_lane_source()
    src = np.where(src < n_src, src, n_src)
    wz = jnp.concatenate([w[..., :n_src], jnp.zeros(w.shape[:-1] + (1,), w.dtype)], axis=-1)
    out = jnp.take(wz, jnp.asarray(src), axis=-1)
    return out.reshape(out.shape[:-2] + (out.shape[-2] * HEAD_PAD,))


def _place_rope_key(kr):
    half = ROPE // 2
    out = jnp.zeros(kr.shape[:-1] + (LANE,), kr.dtype)
    return out.at[..., ROPE_X1:ROPE_X1 + half].set(kr[..., :half]).at[..., ROPE_X2:ROPE_X2 + half].set(kr[..., half:])


def _take_rope_key(tile):
    half = ROPE // 2
    return jnp.concatenate([tile[..., ROPE_X1:ROPE_X1 + half], tile[..., ROPE_X2:ROPE_X2 + half]], axis=-1)


def _pad_heads(w, lo, hi):
    r = w.shape[0]
    part = w[:, :, lo:hi]
    out = jnp.zeros((r, MLA_HEADS, HEAD_PAD), w.dtype).at[:, :, : hi - lo].set(part)
    return out.reshape(r, HP)


def _blockdiag_halves(w):
    bw = LRU_W // LRU_BLOCKS
    out = jnp.zeros((2, 2, LANE, LANE), w.dtype)
    for half in range(2):
        for k in range(2):
            n = 2 * half + k
            out = out.at[half, :, k * bw:(k + 1) * bw, k * bw:(k + 1) * bw].set(w[:, n])
    return out


def _halves(v):
    return jnp.moveaxis(v.reshape(v.shape[:-1] + (2, LANE)), -2, 0)


def _layer_params(l, a):
    d = D_MODEL
    w_in = a["w_in"][l]
    w_in = jnp.concatenate([w_in[:, :Z_KR], _place_rope_key(w_in[:, Z_KR:])], axis=1)
    w_r = jnp.zeros((RT_ROWS, d), F32).at[:N_GROUPS].set(a["w_gr"][l].T).at[SUB:SUB + N_EXPERTS].set(a["w_er"][l].T)
    b_r = jnp.zeros((RT_ROWS, 1), F32).at[:N_GROUPS, 0].set(a["b_gr"][l]).at[SUB:SUB + N_EXPERTS, 0].set(a["b_er"][l])
    wr_hi = w_r.astype(BF16)
    gpad = lambda g: jnp.take(jnp.concatenate([g, jnp.zeros((1,), F32)]), jnp.asarray(_head_lane_source())).reshape(1, LANE)
    return {
        "norm1_g": a["norm1_g"][l].reshape(1, d), "norm2_g": a["norm2_g"][l].reshape(1, d),
        "w_in": w_in.astype(BF16),
        "lru_conv": _halves(a["lru_conv"][l]),
        "lru_wa": _blockdiag_halves(a["lru_wa"][l]).astype(BF16),
        "lru_wx": _blockdiag_halves(a["lru_wx"][l]).astype(BF16),
        "lru_ba": _halves(a["lru_ba"][l])[:, :, None, :], "lru_bx": _halves(a["lru_bx"][l])[:, :, None, :],
        "lru_lam": _halves(a["lru_lam"][l])[:, :, None, :],
        "sc_conv": _halves(a["sc_conv"][l]),
        "g_qa": a["g_qa"][l].reshape(1, Q_LORA), "g_kva": a["g_kva"][l].reshape(1, KV_LORA),
        "wq_t": _place_head_dims(a["w_qb"][l], QK_DIM).T.astype(BF16),
        "wk_t": _place_head_dims(a["w_kvb"][l], NOPE).T.astype(BF16),
        "wv": _pad_heads(a["w_kvb"][l], NOPE, NOPE + V_DIM).astype(BF16),
        "gq_col": gpad(a["g_qn"][l]).reshape(LANE, 1), "gk_col": gpad(a["g_kn"][l]).reshape(LANE, 1),
        "eye": jnp.eye(LANE, dtype=F32).astype(BF16),
        "w_gate": a["w_gate"][l].astype(BF16),
        "w_pf": a["w_pf"][l].astype(BF16), "w_pl": a["w_pl"][l].astype(BF16), "w_ps": a["w_ps"][l].astype(BF16),
        "w_pm": a["w_pm"][l].astype(BF16), "w_out": a["w_out"][l].astype(BF16),
        "wr_hi": wr_hi, "wr_lo": (w_r - wr_hi.astype(F32)).astype(BF16), "b_r": b_r,
        "w1": a["w1"], "w3": a["w3"], "w2": a["w2"], "layer": l,
    }


def _trunk_layer(x, mod, p, consts, h0, cache, cfg):
    s, l = cfg["s"], cfg["l"]
    z = _front(x, mod, p["norm1_g"], p["w_in"], cfg["tm_front"])
    z3 = z.reshape(s, l, Z_W)
    q, k, v, ckv = _qkv(z, consts["rope"], p, l, cfg["tm_qkv"])
    yf = _fourier(z3, consts["dft"], cfg["nb"], cfg["tq_f"])
    yl, ys, fin = _lru(z3, h0, p, cfg["nb_lru"])
    o = _attn(q, k, v.reshape(s, l, HP), cache, cfg["hp"], cfg["tq_a"])
    t = s * l
    x1, h2, comb = _back(x, mod, yf.reshape(t, FNET_W), yl.reshape(t, LRU_W), ys.reshape(t, SC_W),
                         o.reshape(t, MLA_OUT), p, cfg["tm_back"])
    x2 = _moe(h2, comb, x1, mod, p, cfg["tm_moe"], cfg["tm_comb"])
    return x2, z3, ckv, fin


def kernel(x_prompt, x_sample, cache_ckv, cache_krope, state_rglru, c, c_ctx, norm1_g, norm2_g, w_ada, b_ada, w_in, lru_conv, lru_wa, lru_ba, lru_wx, lru_bx, lru_lam, sc_conv, g_qa, w_qb, g_kva, w_kvb, g_qn, g_kn, w_pf, w_pl, w_ps, w_pm, w_gate, w_out, w_gr, b_gr, w_er, b_er, w1, w3, w2):
    a = dict(norm1_g=norm1_g, norm2_g=norm2_g, w_in=w_in, lru_conv=lru_conv, lru_wa=lru_wa, lru_ba=lru_ba,
             lru_wx=lru_wx, lru_bx=lru_bx, lru_lam=lru_lam, sc_conv=sc_conv, g_qa=g_qa, w_qb=w_qb, g_kva=g_kva,
             w_kvb=w_kvb, g_qn=g_qn, g_kn=g_kn, w_pf=w_pf, w_pl=w_pl, w_ps=w_ps, w_pm=w_pm, w_gate=w_gate,
             w_out=w_out, w_gr=w_gr, b_gr=b_gr, w_er=w_er, b_er=b_er, w1=w1, w3=w3, w2=w2)
    bc, lc, d = x_prompt.shape
    bl, ll, _ = x_sample.shape
    past = cache_ckv.shape[2]

    cond8 = jnp.zeros((SUB, d), F32).at[0].set(c_ctx).at[1:1 + bl].set(c)
    mod_all = _ada(cond8, w_ada, b_ada)

    cc, sc = _channel_dft()
    ctx_consts = {"dft": (cc, sc) + _position_dft(lc), "rope": _rope_tables(lc, False)}
    lat_consts = {"dft": (cc, sc) + _position_dft(ll), "rope": _rope_tables(ll, True)}
    cache_tabs = _rope_tables(bl * past, False)
    ctx_cfg = dict(s=bc, l=lc, tm_front=1024, tm_qkv=1024, nb=8, nb_lru=4, tq_f=lc, hp=MLA_HEADS, tq_a=lc, tm_back=1024, tm_moe=512, tm_comb=1024)
    lat_cfg = dict(s=bl, l=ll, tm_front=1024, tm_qkv=1024, nb=bl, nb_lru=1, tq_f=512, hp=MLA_HEADS, tq_a=256, tm_back=1024, tm_moe=512, tm_comb=1024)

    xp = x_prompt.reshape(bc * lc, d)
    xs = x_sample.reshape(bl * ll, d)
    h0_ctx = jnp.zeros((bc, 2, LRU_W), F32)
    ckv_list, kr_list, lru_list = [], [], []
    for l in range(DEPTH):
        p = _layer_params(l, a)
        mod_ctx = mod_all[l, 0:1].reshape(1, 1, 6 * d)
        mod_lat = mod_all[l, 1:1 + bl].reshape(bl, 1, 6 * d)
        kr_tile = _place_rope_key(cache_krope[:, l].reshape(bl * past, ROPE))
        kc, vc = _kvcache(cache_ckv[:, l].reshape(bl * past, KV_LORA), kr_tile, cache_tabs, p)
        cache = (kc, vc.reshape(bl, past, HP))
        xs, _, _, _ = _trunk_layer(xs, mod_lat, p, lat_consts, state_rglru[:, l], cache, lat_cfg)
        xp, z3, ckv, fin = _trunk_layer(xp, mod_ctx, p, ctx_consts, h0_ctx, None, ctx_cfg)
        ckv_list.append(ckv.reshape(bc, lc, KV_LORA))
        kr_list.append(_take_rope_key(z3[:, :, Z_KR:]))
        lru_list.append(fin)
    return (xp.reshape(bc, lc, d), xs.reshape(bl, ll, d),
            jnp.stack(ckv_list, axis=1), jnp.stack(kr_list, axis=1), jnp.stack(lru_list, axis=1))
```

```python
import functools
import math

import numpy as np
import jax
import jax.numpy as jnp
from jax import lax
from jax.experimental import pallas as pl
from jax.experimental.pallas import tpu as pltpu
from jax.experimental.pallas import tpu_sc as plsc

F32 = jnp.float32
BF16 = jnp.bfloat16

D_MODEL = 1024
DEPTH = 2
GRID_W = 64
EPS = 1e-6
FNET_W = 256
FNET_GROUPS = 4
LRU_W = 256
LRU_BLOCKS = 4
LRU_C = 8.0
SC_W = 256
MLA_HEADS = 8
Q_LORA = 384
KV_LORA = 256
NOPE = 64
ROPE = 32
V_DIM = 64
QK_DIM = NOPE + ROPE
MLA_OUT = MLA_HEADS * V_DIM
N_FREQ = ROPE // 4
ROPE_BASE = 10000.0
ATTN_SCALE = QK_DIM ** -0.5
LOG2E = math.log2(math.e)
N_GROUPS = 4
EXP_PER_GROUP = 8
N_EXPERTS = N_GROUPS * EXP_PER_GROUP

LANE = 128
SUB = 8
HEAD_PAD = LANE
HP = MLA_HEADS * HEAD_PAD
ROPE_X1 = 0
ROPE_X2 = LANE // 2
DEN_LANE = V_DIM
Z_QKV = 6 * 256
Z_KR = Z_QKV + Q_LORA + KV_LORA
Z_W = Z_KR + LANE
QKV_W = Q_LORA + KV_LORA + LANE
VMEM_LIMIT = 52 * 1024 * 1024
PACK_W = D_MODEL // 2
HI_MASK = -65536
RT_ID0, RT_ID1, RT_W0, RT_W1 = 0, 1, 2, 3
RT_ROWS = 48
TOP_K = 2
MERGE_BLK = 256
PLAN_BLK = 256
DFT_SPLIT = 64
ATTN_SHORT_SEQS = 2
ATTN_SUB = 256
SC_CORES = 2
SC_SUBCORES = 16
SC_CHUNK = 128


def _cparams(sem):
    return pltpu.CompilerParams(dimension_semantics=sem, vmem_limit_bytes=VMEM_LIMIT)


def _bdot(a, b):
    return jnp.dot(a, b, preferred_element_type=F32)


def _rms(x, g):
    return x * lax.rsqrt(jnp.mean(x * x, axis=-1, keepdims=True) + EPS) * g


def _ada_kernel(c_ref, w_ref, b_ref, o_ref):
    c = c_ref[...]
    s = (c * jax.nn.sigmoid(c)).astype(BF16)
    o_ref[...] = _bdot(s, w_ref[...].astype(BF16)) + b_ref[...]


def _ada(cond8, w_ada, b_ada):
    nblk = 6 * D_MODEL // 1024
    return pl.pallas_call(
        _ada_kernel,
        grid=(DEPTH, nblk),
        in_specs=[
            pl.BlockSpec((SUB, D_MODEL), lambda l, n: (0, 0)),
            pl.BlockSpec((None, D_MODEL, 1024), lambda l, n: (l, 0, n)),
            pl.BlockSpec((None, 1, 1024), lambda l, n: (l, 0, n)),
        ],
        out_specs=pl.BlockSpec((None, SUB, 1024), lambda l, n: (l, 0, n)),
        out_shape=jax.ShapeDtypeStruct((DEPTH, SUB, 6 * D_MODEL), F32),
        compiler_params=_cparams(("arbitrary", "arbitrary")),
        name="ada",
    )(cond8, w_ada, b_ada.reshape(DEPTH, 1, 6 * D_MODEL))


def _front_kernel(x_ref, mod_ref, g_ref, w_ref, z_ref):
    x = x_ref[...]
    sh = mod_ref[:, 0:D_MODEL]
    sc = mod_ref[:, D_MODEL:2 * D_MODEL]
    h = _rms(x, g_ref[...]) * (1.0 + sc) + sh
    z_ref[...] = _bdot(h.astype(BF16), w_ref[...])


def _front(x, mod, g1, w_in, tm):
    t = x.shape[0]
    per_mod = t // (mod.shape[0] * tm)
    return pl.pallas_call(
        _front_kernel,
        grid=(t // tm,),
        in_specs=[
            pl.BlockSpec((tm, D_MODEL), lambda i: (i, 0)),
            pl.BlockSpec((None, 1, 6 * D_MODEL), lambda i: (i // per_mod, 0, 0)),
            pl.BlockSpec((1, D_MODEL), lambda i: (0, 0)),
            pl.BlockSpec((D_MODEL, Z_W), lambda i: (0, 0)),
        ],
        out_specs=pl.BlockSpec((tm, Z_W), lambda i: (i, 0)),
        out_shape=jax.ShapeDtypeStruct((t, Z_W), F32),
        compiler_params=_cparams(("arbitrary",)),
        name="front",
    )(x, mod, g1, w_in)


def _head_norm_rope_t(x_t, g_col, cos_t, sinr_t):
    ss = jnp.sum(x_t * x_t, axis=0, keepdims=True) * (1.0 / QK_DIM)
    x_t = x_t * lax.rsqrt(ss + EPS) * g_col
    return (x_t * cos_t + pltpu.roll(x_t, HEAD_PAD // 2, 0) * sinr_t).astype(BF16)


def _build_kv(ckv, kr_tile, cos_t, sinr_t, wk_t_ref, wv_ref, gk_col, eye_ref, kt_ref, v_ref):
    nt = (((1,), (1,)), ((), ()))
    cb = ckv.astype(BF16)
    kn_t = lax.dot_general(wk_t_ref[...], cb, nt, preferred_element_type=F32)
    kr_hi = kr_tile.astype(BF16)
    kr_lo = (kr_tile - kr_hi.astype(F32)).astype(BF16)
    kr_t = (lax.dot_general(eye_ref[...], kr_hi, nt, preferred_element_type=F32)
            + lax.dot_general(eye_ref[...], kr_lo, nt, preferred_element_type=F32))
    for h in range(MLA_HEADS):
        sl = slice(h * HEAD_PAD, (h + 1) * HEAD_PAD)
        kt_ref[sl, :] = _head_norm_rope_t(kn_t[sl, :] + kr_t, gk_col, cos_t, sinr_t)
    v = _bdot(cb, wv_ref[...])
    lane = lax.broadcasted_iota(jnp.int32, v.shape, 1)
    v_ref[...] = jnp.where((lane & (HEAD_PAD - 1)) == DEN_LANE, 1.0, v).astype(BF16)


def _qkv_kernel(z_ref, cos_t_ref, sinr_t_ref, gqa_ref, wq_ref, gkva_ref, wk_ref, wv_ref,
                gq_ref, gk_ref, eye_ref, q_ref, k_ref, v_ref, ckv_ref):
    cos_t, sinr_t = cos_t_ref[...], sinr_t_ref[...]
    q_c = z_ref[:, 0:Q_LORA]
    kv_c = z_ref[:, Q_LORA:Q_LORA + KV_LORA]
    kr_tile = z_ref[:, Q_LORA + KV_LORA:QKV_W]
    nt = (((1,), (1,)), ((), ()))
    qf_t = lax.dot_general(wq_ref[...], _rms(q_c, gqa_ref[...]).astype(BF16), nt, preferred_element_type=F32)
    gq_col = gq_ref[...] * (ATTN_SCALE * LOG2E)
    for h in range(MLA_HEADS):
        sl = slice(h * HEAD_PAD, (h + 1) * HEAD_PAD)
        q_ref[sl, :] = _head_norm_rope_t(qf_t[sl, :], gq_col, cos_t, sinr_t)
    ckv = _rms(kv_c, gkva_ref[...])
    ckv_ref[...] = ckv
    _build_kv(ckv, kr_tile, cos_t, sinr_t, wk_ref, wv_ref, gk_ref[...], eye_ref, k_ref, v_ref)


def _qkv(z, tabs, p, seq_len, tm):
    t = z.shape[0]
    if tm > seq_len:
        tabs = [jnp.tile(a, (tm // seq_len, 1)) for a in tabs]
    tabs_t = [a.T for a in tabs]
    per_seq = max(seq_len // tm, 1)
    tab_t_spec = pl.BlockSpec((LANE, tm), lambda i: (0, i % per_seq))
    full = lambda shape: pl.BlockSpec(shape, lambda i: (0,) * len(shape))
    return pl.pallas_call(
        _qkv_kernel,
        grid=(t // tm,),
        in_specs=[
            pl.BlockSpec((tm, QKV_W), lambda i: (i, Z_QKV // QKV_W)),
            tab_t_spec, tab_t_spec,
            full((1, Q_LORA)), full((HP, Q_LORA)), full((1, KV_LORA)),
            full((HP, KV_LORA)), full((KV_LORA, HP)), full((LANE, 1)), full((LANE, 1)), full((LANE, LANE)),
        ],
        out_specs=[
            pl.BlockSpec((HP, tm), lambda i: (0, i)),
            pl.BlockSpec((HP, tm), lambda i: (0, i)),
            pl.BlockSpec((tm, HP), lambda i: (i, 0)),
            pl.BlockSpec((tm, KV_LORA), lambda i: (i, 0)),
        ],
        out_shape=[
            jax.ShapeDtypeStruct((HP, t), BF16),
            jax.ShapeDtypeStruct((HP, t), BF16),
            jax.ShapeDtypeStruct((t, HP), BF16),
            jax.ShapeDtypeStruct((t, KV_LORA), F32),
        ],
        compiler_params=_cparams(("arbitrary",)),
        name="qkv",
    )(z, tabs_t[0], tabs_t[1], p["g_qa"], p["wq_t"], p["g_kva"], p["wk_t"], p["wv"],
      p["gq_col"], p["gk_col"], p["eye"])


def _kvcache_kernel(ckv_ref, kr_ref, cos_t_ref, sinr_t_ref, wk_ref, wv_ref, gk_ref, eye_ref, k_ref, v_ref):
    _build_kv(ckv_ref[...], kr_ref[...], cos_t_ref[...], sinr_t_ref[...], wk_ref, wv_ref, gk_ref[...], eye_ref,
              k_ref, v_ref)


def _kvcache(ckv, kr_tile, tabs, p):
    t = ckv.shape[0]
    return pl.pallas_call(
        _kvcache_kernel,
        out_shape=[jax.ShapeDtypeStruct((HP, t), BF16), jax.ShapeDtypeStruct((t, HP), BF16)],
        compiler_params=_cparams(None),
        name="kvcache",
    )(ckv, kr_tile, tabs[0].T, tabs[1].T, p["wk_t"], p["wv"], p["gk_col"], p["eye"])


def _fourier_kernel(nb, u_ref, cc_ref, sc_ref, cl_ref, sl_ref, o_ref, a_scr, b_scr):
    @pl.when(pl.program_id(1) == 0)
    def _():
        for b in range(nb):
            u = u_ref[b].astype(BF16)
            a_scr[:, b * FNET_W:(b + 1) * FNET_W] = _bdot(u, cc_ref[...]).astype(BF16)
            b_scr[:, b * FNET_W:(b + 1) * FNET_W] = _bdot(u, sc_ref[...]).astype(BF16)

    y = _bdot(cl_ref[...], a_scr[...]) - _bdot(sl_ref[...], b_scr[...])
    for b in range(nb):
        o_ref[b] = y[:, b * FNET_W:(b + 1) * FNET_W].astype(BF16)


def _fourier_split_kernel(nb, u_ref, cc_ref, sc_ref, ca_ref, sa_ref, cb_ref, sb_ref, o_ref,
                          a_scr, b_scr, cl_scr, sl_scr):
    @pl.when(pl.program_id(1) == 0)
    def _():
        for b in range(nb):
            u = u_ref[b].astype(BF16)
            a_scr[:, b * FNET_W:(b + 1) * FNET_W] = _bdot(u, cc_ref[...]).astype(BF16)
            b_scr[:, b * FNET_W:(b + 1) * FNET_W] = _bdot(u, sc_ref[...]).astype(BF16)

    cb, sb = cb_ref[...], sb_ref[...]
    for j in range(ca_ref.shape[0]):
        ca, sa = ca_ref[j:j + 1, :], sa_ref[j:j + 1, :]
        cl_scr[j * DFT_SPLIT:(j + 1) * DFT_SPLIT, :] = (ca * cb - sa * sb).astype(BF16)
        sl_scr[j * DFT_SPLIT:(j + 1) * DFT_SPLIT, :] = (sa * cb + ca * sb).astype(BF16)
    y = _bdot(cl_scr[...], a_scr[...]) - _bdot(sl_scr[...], b_scr[...])
    for b in range(nb):
        o_ref[b] = y[:, b * FNET_W:(b + 1) * FNET_W].astype(BF16)


def _fourier_split(z3, mats, nb, tq):
    s, l, _ = z3.shape
    cc, sc, ca, sa, cb, sb = mats
    full = lambda a: pl.BlockSpec(a.shape, lambda i, r: (0, 0))
    rows = pl.BlockSpec((tq // DFT_SPLIT, l), lambda i, r: (r, 0))
    return pl.pallas_call(
        functools.partial(_fourier_split_kernel, nb),
        grid=(s // nb, l // tq),
        in_specs=[pl.BlockSpec((nb, l, FNET_W), lambda i, r: (i, 0, 0)), full(cc), full(sc), rows, rows,
                  full(cb), full(sb)],
        out_specs=pl.BlockSpec((nb, tq, FNET_W), lambda i, r: (i, r, 0)),
        out_shape=jax.ShapeDtypeStruct((s, l, FNET_W), BF16),
        scratch_shapes=[pltpu.VMEM((l, nb * FNET_W), BF16), pltpu.VMEM((l, nb * FNET_W), BF16),
                        pltpu.VMEM((tq, l), BF16), pltpu.VMEM((tq, l), BF16)],
        compiler_params=_cparams(("arbitrary", "arbitrary")),
        name="fourier_split",
    )(z3, cc, sc, ca, sa, cb, sb)


def _fourier(z3, mats, nb, tq):
    s, l, _ = z3.shape
    if len(mats) == 6:
        return _fourier_split(z3, mats, nb, tq)
    cc, sc, cl, sl = mats
    return pl.pallas_call(
        functools.partial(_fourier_kernel, nb),
        grid=(s // nb, l // tq),
        in_specs=[
            pl.BlockSpec((nb, l, FNET_W), lambda i, r: (i, 0, 0)),
            pl.BlockSpec((FNET_W, FNET_W), lambda i, r: (0, 0)),
            pl.BlockSpec((FNET_W, FNET_W), lambda i, r: (0, 0)),
            pl.BlockSpec((tq, l), lambda i, r: (r, 0)),
            pl.BlockSpec((tq, l), lambda i, r: (r, 0)),
        ],
        out_specs=pl.BlockSpec((nb, tq, FNET_W), lambda i, r: (i, r, 0)),
        out_shape=jax.ShapeDtypeStruct((s, l, FNET_W), BF16),
        scratch_shapes=[pltpu.VMEM((l, nb * FNET_W), BF16), pltpu.VMEM((l, nb * FNET_W), BF16)],
        compiler_params=_cparams(("arbitrary", "arbitrary")),
        name="fourier",
    )(z3, cc, sc, cl, sl)


def _load_ext(ref, c, rows, nchunks):
    t0 = pl.multiple_of(c * rows, rows)
    main = ref[pl.ds(t0, rows), :]
    lo = pl.multiple_of(jnp.maximum(t0 - SUB, 0), SUB)
    hi = pl.multiple_of(jnp.minimum(t0 + rows, (nchunks - 1) * rows + rows - SUB), SUB)
    prev = jnp.where(c > 0, ref[pl.ds(lo, SUB), :], 0.0)
    nxt = jnp.where(c < nchunks - 1, ref[pl.ds(hi, SUB), :], 0.0)
    return jnp.concatenate([prev, main, nxt], axis=0)


def _shifted(ext, off, rows):
    n = ext.shape[0]
    r = ext if off == 0 else pltpu.roll(ext, (-off) % n, 0)
    return r[SUB:SUB + rows]


def _gelu_tanh(x):
    return 0.5 * x * (1.0 + jnp.tanh(math.sqrt(2.0 / math.pi) * (x + 0.044715 * (x * x * x))))


def _lru_gates(xc, d, wa_ref, wx_ref, ba_ref, bx_ref, lam_ref):
    xb = xc.astype(BF16)
    r = jax.nn.sigmoid(_bdot(xb, wa_ref[d]) + ba_ref[d])
    i = jax.nn.sigmoid(_bdot(xb, wx_ref[d]) + bx_ref[d])
    nl = -lam_ref[d]
    softplus = jnp.maximum(nl, 0.0) + jnp.log1p(jnp.exp(-jnp.abs(nl)))
    la = (-LRU_C) * r * softplus
    a = jnp.exp(la)
    one_m_a2 = -jnp.tanh(la) * (a * a + 1.0)
    b = jnp.sqrt(one_m_a2) * (i * xc)
    return a, b


def _group_scan(a, b, reverse):
    rows = a.shape[0]
    rm = lax.broadcasted_iota(jnp.int32, a.shape, 0) & (SUB - 1)
    for s in (1, 2, 4):
        if reverse:
            sh, m = rows - s, rm + s <= SUB - 1
        else:
            sh, m = s, rm >= s
        a_sh = pltpu.roll(a, sh, 0)
        b_sh = pltpu.roll(b, sh, 0)
        b = jnp.where(m, a * b_sh + b, b)
        a = jnp.where(m, a * a_sh, a)
    return a, b


def _carry_scan(a, b, carry, reverse):
    ng = a.shape[0] // SUB
    out = [None] * ng
    order = range(ng - 1, -1, -1) if reverse else range(ng)
    for g in order:
        hg = a[g * SUB:(g + 1) * SUB] * carry + b[g * SUB:(g + 1) * SUB]
        out[g] = hg
        carry = hg[0:1] if reverse else hg[SUB - 1:SUB]
    return jnp.concatenate(out, axis=0), carry


def _lru_kernel(nb, rows, nchunks, xl_ref, gl_ref, bs_ref, cs_ref, xs_ref, h0_ref, cw_ref, wa_ref, wx_ref,
                ba_ref, bx_ref, lam_ref, sw_ref, yl_ref, ys_ref, fin_ref, xc_scr, hf_scr, hb_scr):
    cw = cw_ref[...]
    sw = sw_ref[...]
    chunk = lambda c: pl.ds(pl.multiple_of(c * rows, rows), rows)
    seqs = range(nb)

    def conv_body(c, carry):
        for s in seqs:
            ext = _load_ext(xl_ref.at[s], c, rows, nchunks)
            xc = _shifted(ext, -2, rows) * cw[0:1]
            for k in range(1, 4):
                xc = xc + _shifted(ext, k - 2, rows) * cw[k:k + 1]
            xc_scr[s, chunk(c), :] = xc
        return carry

    lax.fori_loop(0, nchunks, conv_body, 0)

    def direction(s, c, d, carry, h_scr):
        a, b = _lru_gates(xc_scr[s, chunk(c), :], d, wa_ref, wx_ref, ba_ref, bx_ref, lam_ref)
        a, b = _group_scan(a, b, d == 1)
        h, carry = _carry_scan(a, b, carry, d == 1)
        h_scr[s, chunk(c), :] = h
        return carry

    def scan_body(j, carries):
        return tuple((direction(s, j, 0, cf, hf_scr), direction(s, nchunks - 1 - j, 1, cb, hb_scr))
                     for s, (cf, cb) in zip(seqs, carries))

    init = tuple((h0_ref[s, 0:1, :], h0_ref[s, 1:2, :]) for s in seqs)
    for s, (cf, cb) in zip(seqs, lax.fori_loop(0, nchunks, scan_body, init)):
        fin_ref[s, 0:1, :] = cf
        fin_ref[s, 1:2, :] = cb

    def out_body(c, carry):
        for s in seqs:
            y = (hf_scr[s, chunk(c), :] + hb_scr[s, chunk(c), :]) * _gelu_tanh(gl_ref[s, chunk(c), :])
            yl_ref[s, chunk(c), :] = y.astype(BF16)
            ext = _load_ext(cs_ref.at[s], c, rows, nchunks) * _load_ext(xs_ref.at[s], c, rows, nchunks)
            acc = _shifted(ext, -1, rows) * sw[0:1]
            acc = acc + _shifted(ext, 0, rows) * sw[1:2]
            acc = acc + _shifted(ext, 1, rows) * sw[2:3]
            ys_ref[s, chunk(c), :] = (bs_ref[s, chunk(c), :] * acc).astype(BF16)
        return carry

    lax.fori_loop(0, nchunks, out_body, 0)


def _lru(z3, h0, p, nb):
    s, l, _ = z3.shape
    rows = min(l, 1024)
    nchunks = l // rows
    half = lambda blk: pl.BlockSpec((nb, l, LANE), lambda i, j: (i, 0, blk + j))
    wspec = lambda shape: pl.BlockSpec((None,) + shape, lambda i, j: (j,) + (0,) * len(shape))
    seq_half = pl.BlockSpec((nb, l, LANE), lambda i, j: (i, 0, j))
    state = pl.BlockSpec((nb, 2, LANE), lambda i, j: (i, 0, j))
    return pl.pallas_call(
        functools.partial(_lru_kernel, nb, rows, nchunks),
        grid=(s // nb, 2),
        in_specs=[
            half(2), half(4), half(6), half(8), half(10), state,
            wspec((4, LANE)), wspec((2, LANE, LANE)), wspec((2, LANE, LANE)),
            wspec((2, 1, LANE)), wspec((2, 1, LANE)), wspec((2, 1, LANE)), wspec((3, LANE)),
        ],
        out_specs=[seq_half, seq_half, state],
        out_shape=[jax.ShapeDtypeStruct((s, l, LRU_W), BF16), jax.ShapeDtypeStruct((s, l, SC_W), BF16),
                   jax.ShapeDtypeStruct((s, 2, LRU_W), F32)],
        scratch_shapes=[pltpu.VMEM((nb, l, LANE), F32)] * 3,
        compiler_params=_cparams(("arbitrary", "arbitrary")),
        name="lru",
    )(z3, z3, z3, z3, z3, h0, p["lru_conv"], p["lru_wa"], p["lru_wx"], p["lru_ba"], p["lru_bx"], p["lru_lam"],
      p["sc_conv"])


def _store_head_pair(o_ref, rows, j, o_even, o_odd):
    lane = lax.broadcasted_iota(jnp.int32, o_even.shape, 1)
    pair = jnp.where(lane < V_DIM, o_even, pltpu.roll(o_odd, V_DIM, 1))
    o_ref[rows, j * LANE:(j + 1) * LANE] = pair.astype(BF16)


def _attn_kernel(hp, sub, has_cache, *refs):
    if has_cache:
        q_ref, k_ref, v_ref, kc_ref, vc_ref, o_ref, s_scr, p_scr, m_scr, sc_scr, pc_scr = refs
    else:
        q_ref, k_ref, v_ref, o_ref, s_scr, p_scr, m_scr = refs
    tn = (((0,), (0,)), ((), ()))
    items = [(slice(r * sub, (r + 1) * sub), slice(h * HEAD_PAD, (h + 1) * HEAD_PAD))
             for r in range(q_ref.shape[1] // sub) for h in range(hp)]
    n = len(items)

    def scores(i):
        rows, head = items[i]
        q_t = q_ref[head, rows]
        s = lax.dot_general(q_t, k_ref[head, :], tn, preferred_element_type=F32)
        m = jnp.max(s, axis=-1, keepdims=True)
        s_scr[i % 2] = s
        if has_cache:
            sc = lax.dot_general(q_t, kc_ref[head, :], tn, preferred_element_type=F32)
            m = jnp.maximum(m, jnp.max(sc, axis=-1, keepdims=True))
            sc_scr[i % 2] = sc
        m_scr[i % 2] = jnp.broadcast_to(m, m_scr.shape[1:])

    def probs(i):
        m = m_scr[i % 2][:, 0:1]
        p_scr[i % 2] = jnp.exp2(s_scr[i % 2] - m).astype(BF16)
        if has_cache:
            pc_scr[i % 2] = jnp.exp2(sc_scr[i % 2] - m).astype(BF16)

    pending = {}

    def weighted_values(i):
        rows, head = items[i]
        o = _bdot(p_scr[i % 2], v_ref[:, head])
        if has_cache:
            o = o + _bdot(pc_scr[i % 2], vc_ref[:, head])
        o = o / o[:, DEN_LANE:DEN_LANE + 1]
        h = i % hp
        if h % 2 == 0:
            pending[0] = o
        else:
            _store_head_pair(o_ref, rows, h // 2, pending.pop(0), o)

    scores(0)
    for i in range(n):
        if i + 1 < n:
            scores(i + 1)
        if i > 0:
            weighted_values(i - 1)
        probs(i)
    weighted_values(n - 1)


def _attn_short_kernel(hp, q_ref, k_ref, v_ref, o_ref):
    tn = (((0,), (0,)), ((), ()))
    nseq, l, _ = v_ref.shape
    items = [(b, slice(b * l, (b + 1) * l), slice(h * HEAD_PAD, (h + 1) * HEAD_PAD))
             for b in range(nseq) for h in range(hp)]
    scores = [lax.dot_general(q_ref[hd, cols], k_ref[hd, cols], tn, preferred_element_type=F32)
              for _, cols, hd in items]
    probs = [jnp.exp2(s - jnp.max(s, axis=-1, keepdims=True)).astype(BF16) for s in scores]
    outs = []
    for (b, _, hd), p in zip(items, probs):
        o = _bdot(p, v_ref[b, :, hd])
        outs.append(o / o[:, DEN_LANE:DEN_LANE + 1])
    for b in range(nseq):
        for j in range(hp // 2):
            _store_head_pair(o_ref.at[b], slice(None), j, outs[b * hp + 2 * j], outs[b * hp + 2 * j + 1])


def _attn(q, k, v, cache, hp, tq):
    s, l, _ = v.shape
    w = hp * HEAD_PAD
    qspec = pl.BlockSpec((w, tq), lambda i, h, r: (h, i * (l // tq) + r))
    ospec = pl.BlockSpec((None, tq, hp * V_DIM), lambda i, h, r: (i, r, h))
    if cache is None and tq == l:
        nseq = ATTN_SHORT_SEQS
        kspec = pl.BlockSpec((w, nseq * l), lambda i, h, r: (h, i))
        vspec = pl.BlockSpec((nseq, l, w), lambda i, h, r: (i, 0, h))
        return pl.pallas_call(
            functools.partial(_attn_short_kernel, hp),
            grid=(s // nseq, MLA_HEADS // hp, 1),
            in_specs=[kspec, kspec, vspec],
            out_specs=pl.BlockSpec((nseq, l, hp * V_DIM), lambda i, h, r: (i, 0, h)),
            out_shape=jax.ShapeDtypeStruct((s, l, MLA_OUT), BF16),
            compiler_params=_cparams(("arbitrary", "arbitrary", "arbitrary")),
            name="attn_short",
        )(q, k, v)
    kv_mode = dict(pipeline_mode=pl.Buffered(1))
    kspec = pl.BlockSpec((w, l), lambda i, h, r: (h, i), **kv_mode)
    vspec = pl.BlockSpec((None, l, w), lambda i, h, r: (i, 0, h), **kv_mode)
    in_specs = [qspec, kspec, vspec]
    args = [q, k, v]
    if cache is not None:
        lc = cache[1].shape[1]
        in_specs += [pl.BlockSpec((w, lc), lambda i, h, r: (h, i), **kv_mode),
                     pl.BlockSpec((None, lc, w), lambda i, h, r: (i, 0, h), **kv_mode)]
        args += list(cache)
    sub = min(tq, ATTN_SUB)
    scratch = [pltpu.VMEM((2, sub, l), F32), pltpu.VMEM((2, sub, l), BF16), pltpu.VMEM((2, sub, LANE), F32)]
    if cache is not None:
        scratch += [pltpu.VMEM((2, sub, lc), F32), pltpu.VMEM((2, sub, lc), BF16)]
    return pl.pallas_call(
        functools.partial(_attn_kernel, hp, sub, cache is not None),
        grid=(s, MLA_HEADS // hp, l // tq),
        in_specs=in_specs,
        out_specs=ospec,
        out_shape=jax.ShapeDtypeStruct((s, l, MLA_OUT), BF16),
        scratch_shapes=scratch,
        compiler_params=_cparams(("arbitrary", "arbitrary", "arbitrary")),
        name="attn",
    )(*args)


def _route(lt):
    row = lax.broadcasted_iota(jnp.int32, (SUB, lt.shape[1]), 0).astype(F32)
    neg = -jnp.inf
    gl = lt[0:SUB]
    g_ok = row < N_GROUPS
    glm = jnp.where(g_ok, gl, neg)
    gmax = jnp.max(glm, axis=0, keepdims=True)
    gsel = jnp.min(jnp.where(glm == gmax, row, float(SUB)), axis=0, keepdims=True)
    g_w = 1.0 / jnp.sum(jnp.where(g_ok, jnp.exp(gl - gmax), 0.0), axis=0, keepdims=True)
    es = lt[SUB:2 * SUB]
    for g in range(1, N_GROUPS):
        es = jnp.where(gsel == float(g), lt[(g + 1) * SUB:(g + 2) * SUB], es)
    m1 = jnp.max(es, axis=0, keepdims=True)
    i1 = jnp.min(jnp.where(es == m1, row, float(SUB)), axis=0, keepdims=True)
    es2 = jnp.where(row == i1, neg, es)
    m2 = jnp.max(es2, axis=0, keepdims=True)
    i2 = jnp.min(jnp.where(es2 == m2, row, float(SUB)), axis=0, keepdims=True)
    e2 = jnp.exp(m2 - m1)
    inv = g_w / (1.0 + e2)
    base = gsel * EXP_PER_GROUP
    return (jnp.where(row == RT_ID0, base + i1, 0.0) + jnp.where(row == RT_ID1, base + i2, 0.0)
            + jnp.where(row == RT_W0, inv, 0.0) + jnp.where(row == RT_W1, inv * e2, 0.0))


def _pack_bf16_pairs(x):
    k = x.shape[1] // 2
    bits = lax.bitcast_convert_type(x.astype(BF16).astype(F32), jnp.int32)
    return lax.shift_right_logical(bits[:, :k], 16) | (bits[:, k:] & HI_MASK)


def _unpack_bf16_pairs(w):
    lo = lax.bitcast_convert_type(lax.shift_left(w, 16), F32).astype(BF16)
    hi = lax.bitcast_convert_type(w & HI_MASK, F32).astype(BF16)
    return lo, hi


def _back_kernel(x_ref, mod_ref, g1_ref, g2_ref, yf_ref, yl_ref, ys_ref, o_ref,
                 wg_ref, wpf_ref, wpl_ref, wps_ref, wpm_ref, wo_ref, wrh_ref, wrl_ref, br_ref,
                 x1_ref, h2_ref, comb_ref):
    d = D_MODEL
    x = x_ref[...]
    sh1, sc1, gt1 = mod_ref[:, 0:d], mod_ref[:, d:2 * d], mod_ref[:, 2 * d:3 * d]
    sh2, sc2 = mod_ref[:, 3 * d:4 * d], mod_ref[:, 4 * d:5 * d]
    hb = (_rms(x, g1_ref[...]) * (1.0 + sc1) + sh1).astype(BF16)
    branches = ((yf_ref, wpf_ref), (yl_ref, wpl_ref), (ys_ref, wps_ref), (o_ref, wpm_ref))
    blocks = []
    for n in range(0, d, MERGE_BLK):
        acc = None
        for j, (y_ref, w_ref) in enumerate(branches):
            gate = jax.nn.sigmoid(_bdot(hb, wg_ref[:, j * d + n:j * d + n + MERGE_BLK]))
            term = gate * _bdot(y_ref[...], w_ref[:, n:n + MERGE_BLK])
            acc = term if acc is None else acc + term
        blocks.append(acc.astype(BF16))
    x1 = x + gt1 * _bdot(jnp.concatenate(blocks, axis=1), wo_ref[...])
    x1_ref[...] = x1
    h2 = _rms(x1, g2_ref[...]) * (1.0 + sc2) + sh2
    h2b = h2.astype(BF16)
    h2_ref[...] = _pack_bf16_pairs(h2)
    h2l = (h2 - h2b.astype(F32)).astype(BF16)
    nt = (((1,), (1,)), ((), ()))
    rdot = lambda w_ref, h: lax.dot_general(w_ref[...], h, nt, preferred_element_type=F32)
    logits_t = rdot(wrh_ref, h2b) + rdot(wrh_ref, h2l) + rdot(wrl_ref, h2b) + br_ref[...]
    comb_ref[...] = _route(logits_t)


def _back(x, mod, yf, yl, ys, o, p, tm):
    t = x.shape[0]
    per_mod = t // (mod.shape[0] * tm)
    row = lambda w: pl.BlockSpec((tm, w), lambda i: (i, 0))
    full = lambda a: pl.BlockSpec(a.shape, lambda i: (0,) * a.ndim, pipeline_mode=pl.Buffered(1))
    weights = [p["w_gate"], p["w_pf"], p["w_pl"], p["w_ps"], p["w_pm"], p["w_out"], p["wr_hi"], p["wr_lo"], p["b_r"]]
    return pl.pallas_call(
        _back_kernel,
        grid=(t // tm,),
        in_specs=[
            row(D_MODEL),
            pl.BlockSpec((None, 1, 6 * D_MODEL), lambda i: (i // per_mod, 0, 0)),
            pl.BlockSpec((1, D_MODEL), lambda i: (0, 0)),
            pl.BlockSpec((1, D_MODEL), lambda i: (0, 0)),
            row(FNET_W), row(LRU_W), row(SC_W), row(MLA_OUT),
        ] + [full(w) for w in weights],
        out_specs=[row(D_MODEL), row(PACK_W), pl.BlockSpec((SUB, tm), lambda i: (0, i))],
        out_shape=[
            jax.ShapeDtypeStruct((t, D_MODEL), F32),
            jax.ShapeDtypeStruct((t, PACK_W), jnp.int32),
            jax.ShapeDtypeStruct((SUB, t), F32),
        ],
        compiler_params=_cparams(("arbitrary",)),
        name="back",
    )(x, mod, p["norm1_g"], p["norm2_g"], yf, yl, ys, o, *weights)


def _plan_kernel(tm, rt_ref, pos_ref, te_ref, nu_ref):
    ids = jnp.concatenate([rt_ref[RT_ID0:RT_ID0 + 1, :], rt_ref[RT_ID1:RT_ID1 + 1, :]], axis=1)
    n_rows = ids.shape[1]
    erow = lax.broadcasted_iota(jnp.int32, (N_EXPERTS, PLAN_BLK), 0).astype(F32)
    col = lax.broadcasted_iota(jnp.int32, (N_EXPERTS, LANE), 0)
    blocks = [(ids[:, b * PLAN_BLK:(b + 1) * PLAN_BLK] == erow).astype(F32) for b in range(n_rows // PLAN_BLK)]
    tri = (lax.broadcasted_iota(jnp.int32, (PLAN_BLK, PLAN_BLK), 0)
           <= lax.broadcasted_iota(jnp.int32, (PLAN_BLK, PLAN_BLK), 1)).astype(BF16)
    prefix = [_bdot(blk.astype(BF16), tri) for blk in blocks]
    counts = prefix[0][:, PLAN_BLK - 1:PLAN_BLK]
    for pre in prefix[1:]:
        counts = counts + pre[:, PLAN_BLK - 1:PLAN_BLK]
    tiles = jnp.floor((counts + float(tm - 1)) * (1.0 / tm))
    tile_end = jnp.broadcast_to(tiles, (N_EXPERTS, LANE))
    for s in (1, 2, 4, 8, 16):
        tile_end = tile_end + jnp.where(col >= s, pltpu.roll(tile_end, s, 0), 0.0)
    base = (tile_end[:, 0:1] - tiles) * float(tm) - 1.0
    for b, (blk, pre) in enumerate(zip(blocks, prefix)):
        pos = jnp.sum(blk * (base + pre), axis=0, keepdims=True)
        pos_ref[:, b * PLAN_BLK:(b + 1) * PLAN_BLK] = pos.astype(jnp.int32)
        base = base + pre[:, PLAN_BLK - 1:PLAN_BLK]
    k = lax.broadcasted_iota(jnp.int32, (N_EXPERTS, LANE), 1).astype(F32)
    owner = jnp.sum((k >= tile_end).astype(F32), axis=0, keepdims=True)
    te_ref[...] = jnp.minimum(owner, float(N_EXPERTS - 1)).astype(jnp.int32)
    nu_ref[...] = jnp.broadcast_to(tile_end[N_EXPERTS - 1:N_EXPERTS, :], nu_ref.shape).astype(jnp.int32)


def _dispatch_plan(rt, tm):
    n_rows = TOP_K * rt.shape[1]
    n_tiles = n_rows // tm + N_EXPERTS
    assert n_tiles <= LANE and tm & (tm - 1) == 0 and n_rows % PLAN_BLK == 0
    pos, te, nu = pl.pallas_call(
        functools.partial(_plan_kernel, tm),
        out_shape=[jax.ShapeDtypeStruct((1, n_rows), jnp.int32), jax.ShapeDtypeStruct((1, LANE), jnp.int32),
                   jax.ShapeDtypeStruct((1, LANE), jnp.int32)],
        compiler_params=_cparams(None),
        name="plan",
    )(rt)
    return pos.reshape(n_rows), te[0, :n_tiles], nu[0, :1], n_tiles


def _sc_worker_rows(n_rows):
    workers = SC_CORES * SC_SUBCORES
    per_w = n_rows // workers
    assert per_w * workers == n_rows and per_w % SC_CHUNK == 0
    return per_w


def _sc_mesh():
    return plsc.VectorSubcoreMesh(core_axis_name="c", subcore_axis_name="s",
                                  num_cores=SC_CORES, num_subcores=SC_SUBCORES)


def _sc_scatter_rows(src, pos, n_out):
    t, w = src.shape
    per_w = _sc_worker_rows(pos.shape[0])

    @functools.partial(
        pl.kernel, mesh=_sc_mesh(), out_type=jax.ShapeDtypeStruct((n_out, w), src.dtype),
        scratch_types=[pltpu.VMEM((SC_CHUNK,), jnp.int32), pltpu.VMEM((SC_CHUNK, w), src.dtype)],
        name="moe_dispatch")
    def run(src_hbm, pos_hbm, out_hbm, idx_v, rows_v):
        wid = lax.axis_index("s") * SC_CORES + lax.axis_index("c")

        @pl.loop(0, per_w // SC_CHUNK)
        def _(i):
            j0 = wid * per_w + i * SC_CHUNK
            pltpu.sync_copy(pos_hbm.at[pl.ds(j0, SC_CHUNK)], idx_v)
            pltpu.sync_copy(src_hbm.at[pl.ds(lax.rem(j0, t), SC_CHUNK)], rows_v)
            pltpu.sync_copy(rows_v, out_hbm.at[idx_v])

    return run(src, pos)


def _sc_gather_rows(src, pos):
    w = src.shape[1]
    n = pos.shape[0]
    per_w = _sc_worker_rows(n)

    @functools.partial(
        pl.kernel, mesh=_sc_mesh(), out_type=jax.ShapeDtypeStruct((n, w), src.dtype),
        scratch_types=[pltpu.VMEM((SC_CHUNK,), jnp.int32), pltpu.VMEM((SC_CHUNK, w), src.dtype)],
        name="moe_return")
    def run(src_hbm, pos_hbm, out_hbm, idx_v, rows_v):
        wid = lax.axis_index("s") * SC_CORES + lax.axis_index("c")

        @pl.loop(0, per_w // SC_CHUNK)
        def _(i):
            j0 = wid * per_w + i * SC_CHUNK
            pltpu.sync_copy(pos_hbm.at[pl.ds(j0, SC_CHUNK)], idx_v)
            pltpu.sync_copy(src_hbm.at[idx_v], rows_v)
            pltpu.sync_copy(rows_v, out_hbm.at[pl.ds(j0, SC_CHUNK)])

    return run(src, pos)


def _experts_kernel(te_ref, nu_ref, x_ref, w1_ref, w3_ref, w2_ref, y_ref):
    half = D_MODEL // 2

    @pl.when(pl.program_id(0) < nu_ref[0])
    def _():
        lo, hi = _unpack_bf16_pairs(x_ref[...])
        a = _bdot(lo, w1_ref[:half, :].astype(BF16)) + _bdot(hi, w1_ref[half:, :].astype(BF16))
        u = _bdot(lo, w3_ref[:half, :].astype(BF16)) + _bdot(hi, w3_ref[half:, :].astype(BF16))
        mid = (a * jax.nn.sigmoid(a)) * u
        y_ref[...] = _pack_bf16_pairs(_bdot(mid.astype(BF16), w2_ref[...].astype(BF16)))


def _experts(xs, tile_expert, n_used, n_tiles, w1, w3, w2, layer, tm):
    wspec = lambda a: pl.BlockSpec((None, None) + a.shape[2:], lambda i, te, nu: (layer, te[i], 0, 0))
    grid_spec = pltpu.PrefetchScalarGridSpec(
        num_scalar_prefetch=2,
        grid=(n_tiles,),
        in_specs=[pl.BlockSpec((tm, PACK_W), lambda i, te, nu: (jnp.minimum(i, nu[0] - 1), 0)),
                  wspec(w1), wspec(w3), wspec(w2)],
        out_specs=pl.BlockSpec((tm, PACK_W), lambda i, te, nu: (jnp.where(i < nu[0], i, n_tiles), 0)),
    )
    return pl.pallas_call(
        _experts_kernel,
        grid_spec=grid_spec,
        out_shape=jax.ShapeDtypeStruct((xs.shape[0] + tm, PACK_W), jnp.int32),
        compiler_params=_cparams(("arbitrary",)),
        name="experts",
    )(tile_expert, n_used, xs, w1, w3, w2)


def _combine_kernel(x1_ref, mod_ref, rt_ref, g0_ref, g1_ref, o_ref):
    rt = rt_ref[...]
    lo0, hi0 = _unpack_bf16_pairs(g0_ref[...])
    lo1, hi1 = _unpack_bf16_pairs(g1_ref[...])
    w0 = rt[:, RT_W0:RT_W0 + 1]
    w1 = rt[:, RT_W1:RT_W1 + 1]
    half = D_MODEL // 2
    gt2 = mod_ref[:, 5 * D_MODEL:6 * D_MODEL]
    o_ref[:, :half] = x1_ref[:, :half] + gt2[:, :half] * (w0 * lo0.astype(F32) + w1 * lo1.astype(F32))
    o_ref[:, half:] = x1_ref[:, half:] + gt2[:, half:] * (w0 * hi0.astype(F32) + w1 * hi1.astype(F32))


def _combine(x1, mod, rt, g, tm):
    t = x1.shape[0]
    per_mod = t // (mod.shape[0] * tm)
    return pl.pallas_call(
        _combine_kernel,
        grid=(t // tm,),
        in_specs=[
            pl.BlockSpec((tm, D_MODEL), lambda i: (i, 0)),
            pl.BlockSpec((None, 1, 6 * D_MODEL), lambda i: (i // per_mod, 0, 0)),
            pl.BlockSpec((tm, LANE), lambda i: (i, 0)),
            pl.BlockSpec((tm, PACK_W), lambda i: (i, 0)),
            pl.BlockSpec((tm, PACK_W), lambda i: (i + t // tm, 0)),
        ],
        out_specs=pl.BlockSpec((tm, D_MODEL), lambda i: (i, 0)),
        out_shape=jax.ShapeDtypeStruct((t, D_MODEL), F32),
        compiler_params=_cparams(("arbitrary",)),
        name="combine",
    )(x1, mod, rt, g, g)


def _moe(h2p, rt, x1, mod, p, tm_e, tm_c):
    pos, tile_expert, n_used, n_tiles = _dispatch_plan(rt, tm_e)
    xs = _sc_scatter_rows(h2p, pos, n_tiles * tm_e)
    ys = _experts(xs, tile_expert, n_used, n_tiles, p["w1"], p["w3"], p["w2"], p["layer"], tm_e)
    g = _sc_gather_rows(ys, pos)
    rt_cols = jnp.pad(rt.T, ((0, 0), (0, LANE - rt.shape[0])))
    return _combine(x1, mod, rt_cols, g, tm_c)


def _channel_dft():
    n = np.arange(FNET_W // FNET_GROUPS)
    ang = 2.0 * np.pi * ((n[:, None] * n[None, :]) % n.size) / n.size
    eye = np.eye(FNET_GROUPS)
    scale = 1.0 / math.sqrt(n.size)
    return (jnp.asarray(np.kron(eye, np.cos(ang) * scale), F32).astype(BF16),
            jnp.asarray(np.kron(eye, np.sin(ang) * scale), F32).astype(BF16))


def _position_dft(l):
    scale = 1.0 / math.sqrt(l)
    if l <= 256:
        n = np.arange(l)
        ang = 2.0 * np.pi * ((n[:, None] * n[None, :]) % l) / l
        return jnp.asarray(np.cos(ang) * scale, F32).astype(BF16), jnp.asarray(np.sin(ang) * scale, F32).astype(BF16)
    m = DFT_SPLIT
    n = np.arange(l)
    ang_a = 2.0 * np.pi * ((np.arange(l // m)[:, None] * m * n[None, :]) % l) / l
    ang_b = 2.0 * np.pi * ((np.arange(m)[:, None] * n[None, :]) % l) / l
    return (jnp.asarray(np.cos(ang_a), F32), jnp.asarray(np.sin(ang_a), F32),
            jnp.asarray(np.cos(ang_b) * scale, F32), jnp.asarray(np.sin(ang_b) * scale, F32))


def _rope_tables(l, rotate):
    cos = np.ones((l, LANE))
    sinr = np.zeros((l, LANE))
    if rotate:
        t = np.arange(l)
        inv = ROPE_BASE ** (-np.arange(N_FREQ) / N_FREQ)
        ang = np.concatenate([(t // GRID_W)[:, None] * inv, (t % GRID_W)[:, None] * inv], axis=-1)
        half = ROPE // 2
        for lo in (ROPE_X1, ROPE_X2):
            cos[:, lo:lo + half] = np.cos(ang)
        sinr[:, ROPE_X1:ROPE_X1 + half] = -np.sin(ang)
        sinr[:, ROPE_X2:ROPE_X2 + half] = np.sin(ang)
    return tuple(jnp.asarray(a, F32) for a in (cos, sinr))


def _head_lane_source():
    half = ROPE // 2
    src = np.full((HEAD_PAD,), QK_DIM, np.int32)
    src[ROPE_X1:ROPE_X1 + half] = NOPE + np.arange(half)
    src[ROPE_X2:ROPE_X2 + half] = NOPE + half + np.arange(half)
    free = [i for i in range(HEAD_PAD) if src[i] == QK_DIM][:NOPE]
    src[free] = np.arange(NOPE)
    return src


def _place_head_dims(w, n_src):
    src = _head_lane_source()
    src = np.where(src < n_src, src, n_src)
    wz = jnp.concatenate([w[..., :n_src], jnp.zeros(w.shape[:-1] + (1,), w.dtype)], axis=-1)
    out = jnp.take(wz, jnp.asarray(src), axis=-1)
    return out.reshape(out.shape[:-2] + (out.shape[-2] * HEAD_PAD,))


def _place_rope_key(kr):
    half = ROPE // 2
    out = jnp.zeros(kr.shape[:-1] + (LANE,), kr.dtype)
    return out.at[..., ROPE_X1:ROPE_X1 + half].set(kr[..., :half]).at[..., ROPE_X2:ROPE_X2 + half].set(kr[..., half:])


def _take_rope_key(tile):
    half = ROPE // 2
    return jnp.concatenate([tile[..., ROPE_X1:ROPE_X1 + half], tile[..., ROPE_X2:ROPE_X2 + half]], axis=-1)


def _pad_heads(w, lo, hi):
    r = w.shape[0]
    part = w[:, :, lo:hi]
    out = jnp.zeros((r, MLA_HEADS, HEAD_PAD), w.dtype).at[:, :, : hi - lo].set(part)
    return out.reshape(r, HP)


def _blockdiag_halves(w):
    bw = LRU_W // LRU_BLOCKS
    out = jnp.zeros((2, 2, LANE, LANE), w.dtype)
    for half in range(2):
        for k in range(2):
            n = 2 * half + k
            out = out.at[half, :, k * bw:(k + 1) * bw, k * bw:(k + 1) * bw].set(w[:, n])
    return out


def _halves(v):
    return jnp.moveaxis(v.reshape(v.shape[:-1] + (2, LANE)), -2, 0)


def _layer_params(l, a):
    d = D_MODEL
    w_in = a["w_in"][l]
    w_in = jnp.concatenate([w_in[:, :Z_KR], _place_rope_key(w_in[:, Z_KR:])], axis=1)
    w_r = jnp.zeros((RT_ROWS, d), F32).at[:N_GROUPS].set(a["w_gr"][l].T).at[SUB:SUB + N_EXPERTS].set(a["w_er"][l].T)
    b_r = jnp.zeros((RT_ROWS, 1), F32).at[:N_GROUPS, 0].set(a["b_gr"][l]).at[SUB:SUB + N_EXPERTS, 0].set(a["b_er"][l])
    wr_hi = w_r.astype(BF16)
    gpad = lambda g: jnp.take(jnp.concatenate([g, jnp.zeros((1,), F32)]), jnp.asarray(_head_lane_source())).reshape(1, LANE)
    return {
        "norm1_g": a["norm1_g"][l].reshape(1, d), "norm2_g": a["norm2_g"][l].reshape(1, d),
        "w_in": w_in.astype(BF16),
        "lru_conv": _halves(a["lru_conv"][l]),
        "lru_wa": _blockdiag_halves(a["lru_wa"][l]).astype(BF16),
        "lru_wx": _blockdiag_halves(a["lru_wx"][l]).astype(BF16),
        "lru_ba": _halves(a["lru_ba"][l])[:, :, None, :], "lru_bx": _halves(a["lru_bx"][l])[:, :, None, :],
        "lru_lam": _halves(a["lru_lam"][l])[:, :, None, :],
        "sc_conv": _halves(a["sc_conv"][l]),
        "g_qa": a["g_qa"][l].reshape(1, Q_LORA), "g_kva": a["g_kva"][l].reshape(1, KV_LORA),
        "wq_t": _place_head_dims(a["w_qb"][l], QK_DIM).T.astype(BF16),
        "wk_t": _place_head_dims(a["w_kvb"][l], NOPE).T.astype(BF16),
        "wv": _pad_heads(a["w_kvb"][l], NOPE, NOPE + V_DIM).astype(BF16),
        "gq_col": gpad(a["g_qn"][l]).reshape(LANE, 1), "gk_col": gpad(a["g_kn"][l]).reshape(LANE, 1),
        "eye": jnp.eye(LANE, dtype=F32).astype(BF16),
        "w_gate": a["w_gate"][l].astype(BF16),
        "w_pf": a["w_pf"][l].astype(BF16), "w_pl": a["w_pl"][l].astype(BF16), "w_ps": a["w_ps"][l].astype(BF16),
        "w_pm": a["w_pm"][l].astype(BF16), "w_out": a["w_out"][l].astype(BF16),
        "wr_hi": wr_hi, "wr_lo": (w_r - wr_hi.astype(F32)).astype(BF16), "b_r": b_r,
        "w1": a["w1"], "w3": a["w3"], "w2": a["w2"], "layer": l,
    }


def _trunk_layer(x, mod, p, consts, h0, cache, cfg):
    s, l = cfg["s"], cfg["l"]
    z = _front(x, mod, p["norm1_g"], p["w_in"], cfg["tm_front"])
    z3 = z.reshape(s, l, Z_W)
    q, k, v, ckv = _qkv(z, consts["rope"], p, l, cfg["tm_qkv"])
    yf = _fourier(z3, consts["dft"], cfg["nb"], cfg["tq_f"])
    yl, ys, fin = _lru(z3, h0, p, cfg["nb_lru"])
    o = _attn(q, k, v.reshape(s, l, HP), cache, cfg["hp"], cfg["tq_a"])
    t = s * l
    x1, h2, comb = _back(x, mod, yf.reshape(t, FNET_W), yl.reshape(t, LRU_W), ys.reshape(t, SC_W),
                         o.reshape(t, MLA_OUT), p, cfg["tm_back"])
    x2 = _moe(h2, comb, x1, mod, p, cfg["tm_moe"], cfg["tm_comb"])
    return x2, z3, ckv, fin


def kernel(x_prompt, x_sample, cache_ckv, cache_krope, state_rglru, c, c_ctx, norm1_g, norm2_g, w_ada, b_ada, w_in, lru_conv, lru_wa, lru_ba, lru_wx, lru_bx, lru_lam, sc_conv, g_qa, w_qb, g_kva, w_kvb, g_qn, g_kn, w_pf, w_pl, w_ps, w_pm, w_gate, w_out, w_gr, b_gr, w_er, b_er, w1, w3, w2):
    a = dict(norm1_g=norm1_g, norm2_g=norm2_g, w_in=w_in, lru_conv=lru_conv, lru_wa=lru_wa, lru_ba=lru_ba,
             lru_wx=lru_wx, lru_bx=lru_bx, lru_lam=lru_lam, sc_conv=sc_conv, g_qa=g_qa, w_qb=w_qb, g_kva=g_kva,
             w_kvb=w_kvb, g_qn=g_qn, g_kn=g_kn, w_pf=w_pf, w_pl=w_pl, w_ps=w_ps, w_pm=w_pm, w_gate=w_gate,
             w_out=w_out, w_gr=w_gr, b_gr=b_gr, w_er=w_er, b_er=b_er, w1=w1, w3=w3, w2=w2)
    bc, lc, d = x_prompt.shape
    bl, ll, _ = x_sample.shape
    past = cache_ckv.shape[2]

    cond8 = jnp.zeros((SUB, d), F32).at[0].set(c_ctx).at[1:1 + bl].set(c)
    mod_all = _ada(cond8, w_ada, b_ada)

    cc, sc = _channel_dft()
    ctx_consts = {"dft": (cc, sc) + _position_dft(lc), "rope": _rope_tables(lc, False)}
    lat_consts = {"dft": (cc, sc) + _position_dft(ll), "rope": _rope_tables(ll, True)}
    cache_tabs = _rope_tables(bl * past, False)
    ctx_cfg = dict(s=bc, l=lc, tm_front=1024, tm_qkv=1024, nb=8, nb_lru=4, tq_f=lc, hp=MLA_HEADS, tq_a=lc, tm_back=1024, tm_moe=256, tm_comb=1024)
    lat_cfg = dict(s=bl, l=ll, tm_front=1024, tm_qkv=1024, nb=bl, nb_lru=1, tq_f=512, hp=MLA_HEADS, tq_a=256, tm_back=1024, tm_moe=256, tm_comb=1024)

    xp = x_prompt.reshape(bc * lc, d)
    xs = x_sample.reshape(bl * ll, d)
    h0_ctx = jnp.zeros((bc, 2, LRU_W), F32)
    ckv_list, kr_list, lru_list = [], [], []
    for l in range(DEPTH):
        p = _layer_params(l, a)
        mod_ctx = mod_all[l, 0:1].reshape(1, 1, 6 * d)
        mod_lat = mod_all[l, 1:1 + bl].reshape(bl, 1, 6 * d)
        xp, z3, ckv, fin = _trunk_layer(xp, mod_ctx, p, ctx_consts, h0_ctx, None, ctx_cfg)
        ckv_list.append(ckv.reshape(bc, lc, KV_LORA))
        kr_list.append(_take_rope_key(z3[:, :, Z_KR:]))
        lru_list.append(fin)
        kr_tile = _place_rope_key(cache_krope[:, l].reshape(bl * past, ROPE))
        kc, vc = _kvcache(cache_ckv[:, l].reshape(bl * past, KV_LORA), kr_tile, cache_tabs, p)
        cache = (kc, vc.reshape(bl, past, HP))
        xs, _, _, _ = _trunk_layer(xs, mod_lat, p, lat_consts, state_rglru[:, l], cache, lat_cfg)
    return (xp.reshape(bc, lc, d), xs.reshape(bl, ll, d),
            jnp.stack(ckv_list, axis=1), jnp.stack(kr_list, axis=1), jnp.stack(lru_list, axis=1))
```
